```python
import jax, jax.numpy as jnp
from jax import lax
import numpy as np

D_MODEL = 1024
BATCH = 8
SEQ = 4096
DEPTH = 4

N_MIXERS = 3
PLE_DIM = 256
D_FF = 4 * D_MODEL
RMS_EPS = 1e-6
SC_WIDTH = 3
ATTN_HEAD_DIM = 128
ATTN_HEADS_PER_GROUP = D_MODEL // ATTN_HEAD_DIM
DILATED_PATTERNS = ((128, 1), (512, 4), (2048, 16))
N_ATTN_GROUPS = len(DILATED_PATTERNS)
ROPE_THETA = 500000.0
ROPE_DIM = ATTN_HEAD_DIM // 4
D_RNN = 1280
N_LRU_BLOCKS = 10
LRU_BLOCK = D_RNN // N_LRU_BLOCKS
LRU_CONV_WIDTH = 4
LRU_C = 8.0

N_LAYERS_A = len(range(0, DEPTH, N_MIXERS))
N_LAYERS_B = len(range(1, DEPTH, N_MIXERS))
N_LAYERS_C = len(range(2, DEPTH, N_MIXERS))

kernel_name = 'hybrid_conv_dilattn_rglru_trunk'


def rmsnorm(x, g):
    xf = x.astype(jnp.float32)
    y = xf * lax.rsqrt(jnp.mean(xf * xf, axis=-1, keepdims=True) + RMS_EPS)
    return (y * g.astype(jnp.float32)).astype(x.dtype)


def causal_depthwise_conv(x, w):
    k_width, ch = w.shape
    return lax.conv_general_dilated(
        x, w[:, None, :].astype(x.dtype), window_strides=(1,),
        padding=[(k_width - 1, 0)], dimension_numbers=('NWC', 'WIO', 'NWC'),
        feature_group_count=ch)


def partial_rotary(x, positions):
    half = ROPE_DIM // 2
    inv_freq = ROPE_THETA ** (-2.0 * jnp.arange(half, dtype=jnp.float32) / ROPE_DIM)
    ang = positions.astype(jnp.float32)[..., None] * inv_freq
    cos = jnp.cos(ang)[:, :, None, :]
    sin = jnp.sin(ang)[:, :, None, :]
    xf = x.astype(jnp.float32)
    x1 = xf[..., :half]
    x2 = xf[..., half:ROPE_DIM]
    out = jnp.concatenate([x1 * cos - x2 * sin, x2 * cos + x1 * sin, xf[..., ROPE_DIM:]], axis=-1)
    return out.astype(x.dtype)


def dilated_window_attention(q, k, v, window, dilation):
    b_, s_, h_, dh = q.shape
    n_back = window // dilation
    blk = n_back
    sub_len = s_ // dilation
    n_blk = -(-sub_len // blk)
    sub_pad = n_blk * blk

    def to_sub(t):
        t = t.reshape(b_, sub_len, dilation, h_, dh).transpose(0, 2, 1, 3, 4)
        t = t.reshape(b_ * dilation, sub_len, h_, dh)
        t = jnp.pad(t, ((0, 0), (0, sub_pad - sub_len), (0, 0), (0, 0)))
        return t.reshape(b_ * dilation, n_blk, blk, h_, dh)

    def with_prev(t):
        prev = jnp.pad(t[:, :-1], ((0, 0), (1, 0), (0, 0), (0, 0), (0, 0)))
        return jnp.concatenate([prev, t], axis=2)

    qb = to_sub(q)
    kk = with_prev(to_sub(k))
    vv = with_prev(to_sub(v))
    s = jnp.einsum('nbqhd,nbkhd->nbhqk', qb, kk,
                   preferred_element_type=jnp.float32) * (dh ** -0.5)
    qi = jnp.arange(blk)[:, None]
    kj = jnp.arange(2 * blk)[None, :]
    rel = blk + qi - kj
    band = (rel >= 0) & (rel <= n_back)
    k_pos = jnp.arange(n_blk)[:, None, None] * blk - blk + kj[None]
    valid = band[None] & (k_pos >= 0)
    s = jnp.where(valid[None, :, None], s, -jnp.inf)
    lse = jax.nn.logsumexp(s, axis=-1)
    probs = jnp.exp(s - lse[..., None])
    o = jnp.einsum('nbhqk,nbkhd->nbqhd', probs, vv.astype(jnp.float32))

    def from_sub(t):
        t = t.reshape((b_, dilation, sub_pad) + t.shape[3:])[:, :, :sub_len]
        t = jnp.moveaxis(t, 1, 2)
        return t.reshape((b_, s_) + t.shape[3:])

    return from_sub(o), from_sub(jnp.swapaxes(lse, 2, 3))


def short_conv_mixer(h, w_in, w_conv, w_out):
    gate_b, gate_c, xin = jnp.split(h @ w_in, 3, axis=-1)
    y = gate_b * causal_depthwise_conv(gate_c * xin, w_conv)
    return y @ w_out


def dilated_attention_mixer(h, positions, w_qkv, w_o):
    b_, s_, _ = h.shape
    g_, hh, dh = N_ATTN_GROUPS, ATTN_HEADS_PER_GROUP, ATTN_HEAD_DIM
    qkv = (h @ w_qkv).reshape(b_, s_, 3, g_ * hh, dh)
    q = partial_rotary(qkv[:, :, 0], positions).reshape(b_, s_, g_, hh, dh)
    k = partial_rotary(qkv[:, :, 1], positions).reshape(b_, s_, g_, hh, dh)
    v = qkv[:, :, 2].reshape(b_, s_, g_, hh, dh)
    outs, lses = [], []
    for g, (window, dilation) in enumerate(DILATED_PATTERNS):
        o_g, lse_g = dilated_window_attention(q[:, :, g], k[:, :, g], v[:, :, g], window, dilation)
        outs.append(o_g)
        lses.append(lse_g)
    weights = jax.nn.softmax(jnp.stack(lses, axis=0), axis=0)
    o = jnp.sum(weights[..., None] * jnp.stack(outs, axis=0), axis=0)
    return o.reshape(b_, s_, hh * dh).astype(h.dtype) @ w_o


def rglru_mixer(h, w_in, conv_w, conv_b, w_a, b_a, w_x, b_x, lam, w_out):
    b_, s_, _ = h.shape
    gate, xr = jnp.split(h @ w_in, 2, axis=-1)
    xr = causal_depthwise_conv(xr, conv_w) + conv_b
    xb = xr.reshape(b_, s_, N_LRU_BLOCKS, LRU_BLOCK)
    r_gate = jax.nn.sigmoid(jnp.einsum('bsnj,njk->bsnk', xb, w_a).reshape(b_, s_, D_RNN) + b_a)
    i_gate = jax.nn.sigmoid(jnp.einsum('bsnj,njk->bsnk', xb, w_x).reshape(b_, s_, D_RNN) + b_x)
    log_a = -LRU_C * r_gate.astype(jnp.float32) * jax.nn.softplus(-lam.astype(jnp.float32))
    a = jnp.exp(log_a)
    mult = jnp.sqrt(-jnp.expm1(2.0 * log_a))
    u = mult * (i_gate * xr).astype(jnp.float32)

    def combine(left, right):
        a_l, u_l = left
        a_r, u_r = right
        return a_l * a_r, a_r * u_l + u_r

    _, hs = lax.associative_scan(combine, (a, u), axis=1)
    y = hs.astype(h.dtype) * jax.nn.gelu(gate)
    return y @ w_out


def squared_relu_mlp(h, w_up, w_down):
    return jnp.square(jax.nn.relu(h @ w_up)) @ w_down


def _fwd_setup_inputs(seed: int = 0) -> dict:
    key = jax.random.key(seed)
    ks = iter(jax.random.split(key, 40))

    def nrm(shape, fan_in):
        return jax.random.normal(next(ks), shape, jnp.float32) * (fan_in ** -0.5)

    def gain(shape):
        return 1.0 + 0.05 * jax.random.normal(next(ks), shape, jnp.float32)

    def bias(shape):
        return 0.1 * jax.random.normal(next(ks), shape, jnp.float32)

    attn_width = N_ATTN_GROUPS * ATTN_HEADS_PER_GROUP * ATTN_HEAD_DIM
    u = jax.random.uniform(next(ks), (N_LAYERS_C, D_RNN), jnp.float32, 0.9, 0.999)
    sig = u ** (1.0 / LRU_C)
    return {
        'x': jax.random.normal(next(ks), (BATCH, SEQ, D_MODEL), jnp.float32),
        'p': jax.random.normal(next(ks), (DEPTH, BATCH, SEQ, PLE_DIM), jnp.float32),
        'positions': (jnp.arange(SEQ, dtype=jnp.int32)[None, :]
                      + jax.random.randint(next(ks), (BATCH, 1), 0, 1024, jnp.int32)),
        'norm_mix': gain((DEPTH, D_MODEL)),
        'norm_mlp': gain((DEPTH, D_MODEL)),
        'norm_ple': gain((DEPTH, D_MODEL)),
        'norm_final': gain((D_MODEL,)),
        'sc_w_in': nrm((N_LAYERS_A, D_MODEL, 3 * D_MODEL), D_MODEL),
        'sc_w_conv': nrm((N_LAYERS_A, SC_WIDTH, D_MODEL), SC_WIDTH),
        'sc_w_out': nrm((N_LAYERS_A, D_MODEL, D_MODEL), D_MODEL),
        'attn_w_qkv': nrm((N_LAYERS_B, D_MODEL, 3 * attn_width), D_MODEL),
        'attn_w_o': nrm((N_LAYERS_B, ATTN_HEADS_PER_GROUP * ATTN_HEAD_DIM, D_MODEL),
                        ATTN_HEADS_PER_GROUP * ATTN_HEAD_DIM),
        'lru_w_in': nrm((N_LAYERS_C, D_MODEL, 2 * D_RNN), D_MODEL),
        'lru_conv_w': nrm((N_LAYERS_C, LRU_CONV_WIDTH, D_RNN), LRU_CONV_WIDTH),
        'lru_conv_b': bias((N_LAYERS_C, D_RNN)),
        'lru_w_a': nrm((N_LAYERS_C, N_LRU_BLOCKS, LRU_BLOCK, LRU_BLOCK), LRU_BLOCK),
        'lru_b_a': bias((N_LAYERS_C, D_RNN)),
        'lru_w_x': nrm((N_LAYERS_C, N_LRU_BLOCKS, LRU_BLOCK, LRU_BLOCK), LRU_BLOCK),
        'lru_b_x': bias((N_LAYERS_C, D_RNN)),
        'lru_lambda': jnp.log(sig) - jnp.log1p(-sig),
        'lru_w_out': nrm((N_LAYERS_C, D_RNN, D_MODEL), D_RNN),
        'mlp_w_up': nrm((DEPTH, D_MODEL, D_FF), D_MODEL),
        'mlp_w_down': nrm((DEPTH, D_FF, D_MODEL), D_FF),
        'ple_w_gate': nrm((DEPTH, D_MODEL, D_MODEL), D_MODEL),
        'ple_w_proj': nrm((DEPTH, PLE_DIM, D_MODEL), PLE_DIM),
    }


def _fwd_reference(x, p, positions, norm_mix, norm_mlp, norm_ple, norm_final,
              sc_w_in, sc_w_conv, sc_w_out, attn_w_qkv, attn_w_o,
              lru_w_in, lru_conv_w, lru_conv_b, lru_w_a, lru_b_a, lru_w_x, lru_b_x,
              lru_lambda, lru_w_out, mlp_w_up, mlp_w_down, ple_w_gate, ple_w_proj):
    h = x
    for i in range(DEPTH):
        kind, j = i % N_MIXERS, i // N_MIXERS
        hn = rmsnorm(h, norm_mix[i])
        if kind == 0:
            mixed = short_conv_mixer(hn, sc_w_in[j], sc_w_conv[j], sc_w_out[j])
        elif kind == 1:
            mixed = dilated_attention_mixer(hn, positions, attn_w_qkv[j], attn_w_o[j])
        else:
            mixed = rglru_mixer(hn, lru_w_in[j], lru_conv_w[j], lru_conv_b[j], lru_w_a[j],
                                lru_b_a[j], lru_w_x[j], lru_b_x[j], lru_lambda[j], lru_w_out[j])
        h = h + mixed
        h = h + squared_relu_mlp(rmsnorm(h, norm_mlp[i]), mlp_w_up[i], mlp_w_down[i])
        ple_gate = jax.nn.sigmoid(rmsnorm(h, norm_ple[i]) @ ple_w_gate[i])
        h = h + ple_gate * (p[i].astype(h.dtype) @ ple_w_proj[i])
    return rmsnorm(h, norm_final)


import jax as _jax
import jax.numpy as _jnp

TWIN_FORMAT = 'train_step'
FWD_PARAMS = ['x', 'p', 'positions', 'norm_mix', 'norm_mlp', 'norm_ple', 'norm_final', 'sc_w_in', 'sc_w_conv', 'sc_w_out', 'attn_w_qkv', 'attn_w_o', 'lru_w_in', 'lru_conv_w', 'lru_conv_b', 'lru_w_a', 'lru_b_a', 'lru_w_x', 'lru_b_x', 'lru_lambda', 'lru_w_out', 'mlp_w_up', 'mlp_w_down', 'ple_w_gate', 'ple_w_proj']
TWIN_WEIGHTS = ['norm_mix', 'norm_mlp', 'norm_ple', 'norm_final', 'sc_w_in', 'sc_w_conv', 'sc_w_out', 'attn_w_qkv', 'attn_w_o', 'lru_w_in', 'lru_conv_w', 'lru_conv_b', 'lru_w_a', 'lru_b_a', 'lru_w_x', 'lru_b_x', 'lru_lambda', 'lru_w_out', 'mlp_w_up', 'mlp_w_down', 'ple_w_gate', 'ple_w_proj']
TWIN_DIFF_INPUT = 'x'
TWIN_INPUTS = ['x', 'p', 'positions', 'norm_mix', 'norm_mlp', 'norm_ple', 'norm_final', 'sc_w_in', 'sc_w_conv', 'sc_w_out', 'attn_w_qkv', 'attn_w_o', 'lru_w_in', 'lru_conv_w', 'lru_conv_b', 'lru_w_a', 'lru_b_a', 'lru_w_x', 'lru_b_x', 'lru_lambda', 'lru_w_out', 'mlp_w_up', 'mlp_w_down', 'ple_w_gate', 'ple_w_proj', 'loss_target', 'm_norm_mix', 'm_norm_mlp', 'm_norm_ple', 'm_norm_final', 'm_sc_w_in', 'm_sc_w_conv', 'm_sc_w_out', 'm_attn_w_qkv', 'm_attn_w_o', 'm_lru_w_in', 'm_lru_conv_w', 'm_lru_conv_b', 'm_lru_w_a', 'm_lru_b_a', 'm_lru_w_x', 'm_lru_b_x', 'm_lru_lambda', 'm_lru_w_out', 'm_mlp_w_up', 'm_mlp_w_down', 'm_ple_w_gate', 'm_ple_w_proj', 'v_norm_mix', 'v_norm_mlp', 'v_norm_ple', 'v_norm_final', 'v_sc_w_in', 'v_sc_w_conv', 'v_sc_w_out', 'v_attn_w_qkv', 'v_attn_w_o', 'v_lru_w_in', 'v_lru_conv_w', 'v_lru_conv_b', 'v_lru_w_a', 'v_lru_b_a', 'v_lru_w_x', 'v_lru_b_x', 'v_lru_lambda', 'v_lru_w_out', 'v_mlp_w_up', 'v_mlp_w_down', 'v_ple_w_gate', 'v_ple_w_proj']
TWIN_OUTPUTS = ['loss', 'grad_x', 'grad_norm_mix', 'grad_norm_mlp', 'grad_norm_ple', 'grad_norm_final', 'grad_sc_w_in', 'grad_sc_w_conv', 'grad_sc_w_out', 'grad_attn_w_qkv', 'grad_attn_w_o', 'grad_lru_w_in', 'grad_lru_conv_w', 'grad_lru_conv_b', 'grad_lru_w_a', 'grad_lru_b_a', 'grad_lru_w_x', 'grad_lru_b_x', 'grad_lru_lambda', 'grad_lru_w_out', 'grad_mlp_w_up', 'grad_mlp_w_down', 'grad_ple_w_gate', 'grad_ple_w_proj', 'delta_norm_mix', 'delta_norm_mlp', 'delta_norm_ple', 'delta_norm_final', 'delta_sc_w_in', 'delta_sc_w_conv', 'delta_sc_w_out', 'delta_attn_w_qkv', 'delta_attn_w_o', 'delta_lru_w_in', 'delta_lru_conv_w', 'delta_lru_conv_b', 'delta_lru_w_a', 'delta_lru_b_a', 'delta_lru_w_x', 'delta_lru_b_x', 'delta_lru_lambda', 'delta_lru_w_out', 'delta_mlp_w_up', 'delta_mlp_w_down', 'delta_ple_w_gate', 'delta_ple_w_proj', 'new_m_norm_mix', 'new_m_norm_mlp', 'new_m_norm_ple', 'new_m_norm_final', 'new_m_sc_w_in', 'new_m_sc_w_conv', 'new_m_sc_w_out', 'new_m_attn_w_qkv', 'new_m_attn_w_o', 'new_m_lru_w_in', 'new_m_lru_conv_w', 'new_m_lru_conv_b', 'new_m_lru_w_a', 'new_m_lru_b_a', 'new_m_lru_w_x', 'new_m_lru_b_x', 'new_m_lru_lambda', 'new_m_lru_w_out', 'new_m_mlp_w_up', 'new_m_mlp_w_down', 'new_m_ple_w_gate', 'new_m_ple_w_proj', 'new_v_norm_mix', 'new_v_norm_mlp', 'new_v_norm_ple', 'new_v_norm_final', 'new_v_sc_w_in', 'new_v_sc_w_conv', 'new_v_sc_w_out', 'new_v_attn_w_qkv', 'new_v_attn_w_o', 'new_v_lru_w_in', 'new_v_lru_conv_w', 'new_v_lru_conv_b', 'new_v_lru_w_a', 'new_v_lru_b_a', 'new_v_lru_w_x', 'new_v_lru_b_x', 'new_v_lru_lambda', 'new_v_lru_w_out', 'new_v_mlp_w_up', 'new_v_mlp_w_down', 'new_v_ple_w_gate', 'new_v_ple_w_proj']
TWIN_LEAF_KINDS = {'loss': 'loss', 'grad_x': 'grad_x', 'grad_norm_mix': 'grad_w', 'grad_norm_mlp': 'grad_w', 'grad_norm_ple': 'grad_w', 'grad_norm_final': 'grad_w', 'grad_sc_w_in': 'grad_w', 'grad_sc_w_conv': 'grad_w', 'grad_sc_w_out': 'grad_w', 'grad_attn_w_qkv': 'grad_w', 'grad_attn_w_o': 'grad_w', 'grad_lru_w_in': 'grad_w', 'grad_lru_conv_w': 'grad_w', 'grad_lru_conv_b': 'grad_w', 'grad_lru_w_a': 'grad_w', 'grad_lru_b_a': 'grad_w', 'grad_lru_w_x': 'grad_w', 'grad_lru_b_x': 'grad_w', 'grad_lru_lambda': 'grad_w', 'grad_lru_w_out': 'grad_w', 'grad_mlp_w_up': 'grad_w', 'grad_mlp_w_down': 'grad_w', 'grad_ple_w_gate': 'grad_w', 'grad_ple_w_proj': 'grad_w', 'delta_norm_mix': 'delta_w', 'delta_norm_mlp': 'delta_w', 'delta_norm_ple': 'delta_w', 'delta_norm_final': 'delta_w', 'delta_sc_w_in': 'delta_w', 'delta_sc_w_conv': 'delta_w', 'delta_sc_w_out': 'delta_w', 'delta_attn_w_qkv': 'delta_w', 'delta_attn_w_o': 'delta_w', 'delta_lru_w_in': 'delta_w', 'delta_lru_conv_w': 'delta_w', 'delta_lru_conv_b': 'delta_w', 'delta_lru_w_a': 'delta_w', 'delta_lru_b_a': 'delta_w', 'delta_lru_w_x': 'delta_w', 'delta_lru_b_x': 'delta_w', 'delta_lru_lambda': 'delta_w', 'delta_lru_w_out': 'delta_w', 'delta_mlp_w_up': 'delta_w', 'delta_mlp_w_down': 'delta_w', 'delta_ple_w_gate': 'delta_w', 'delta_ple_w_proj': 'delta_w', 'new_m_norm_mix': 'new_m', 'new_m_norm_mlp': 'new_m', 'new_m_norm_ple': 'new_m', 'new_m_norm_final': 'new_m', 'new_m_sc_w_in': 'new_m', 'new_m_sc_w_conv': 'new_m', 'new_m_sc_w_out': 'new_m', 'new_m_attn_w_qkv': 'new_m', 'new_m_attn_w_o': 'new_m', 'new_m_lru_w_in': 'new_m', 'new_m_lru_conv_w': 'new_m', 'new_m_lru_conv_b': 'new_m', 'new_m_lru_w_a': 'new_m', 'new_m_lru_b_a': 'new_m', 'new_m_lru_w_x': 'new_m', 'new_m_lru_b_x': 'new_m', 'new_m_lru_lambda': 'new_m', 'new_m_lru_w_out': 'new_m', 'new_m_mlp_w_up': 'new_m', 'new_m_mlp_w_down': 'new_m', 'new_m_ple_w_gate': 'new_m', 'new_m_ple_w_proj': 'new_m', 'new_v_norm_mix': 'new_v', 'new_v_norm_mlp': 'new_v', 'new_v_norm_ple': 'new_v', 'new_v_norm_final': 'new_v', 'new_v_sc_w_in': 'new_v', 'new_v_sc_w_conv': 'new_v', 'new_v_sc_w_out': 'new_v', 'new_v_attn_w_qkv': 'new_v', 'new_v_attn_w_o': 'new_v', 'new_v_lru_w_in': 'new_v', 'new_v_lru_conv_w': 'new_v', 'new_v_lru_conv_b': 'new_v', 'new_v_lru_w_a': 'new_v', 'new_v_lru_b_a': 'new_v', 'new_v_lru_w_x': 'new_v', 'new_v_lru_b_x': 'new_v', 'new_v_lru_lambda': 'new_v', 'new_v_lru_w_out': 'new_v', 'new_v_mlp_w_up': 'new_v', 'new_v_mlp_w_down': 'new_v', 'new_v_ple_w_gate': 'new_v', 'new_v_ple_w_proj': 'new_v'}


def _forward(args):
    return _fwd_reference(*[args[k] for k in FWD_PARAMS])


def _output_shape():
    def fwd():
        inp = _fwd_setup_inputs(0)
        return _fwd_reference(*[inp[k] for k in FWD_PARAMS])
    out = _jax.eval_shape(fwd)
    return out.shape, out.dtype

N_MICROBATCH = 1
ADAM_LR = 0.001
ADAM_B1 = 0.9
ADAM_B2 = 0.999
ADAM_EPS = 1e-08
ADAM_WD = 0.01
ADAM_STEP = 10
PER_EXAMPLE_BATCH_AXIS = {'x': 0, 'p': 1, 'positions': 0, 'loss_target': 0}
SHARED_INPUTS = []
_WEIGHT_DTYPES = {'norm_mix': _jnp.float32, 'norm_mlp': _jnp.float32, 'norm_ple': _jnp.float32, 'norm_final': _jnp.float32, 'sc_w_in': _jnp.float32, 'sc_w_conv': _jnp.float32, 'sc_w_out': _jnp.float32, 'attn_w_qkv': _jnp.float32, 'attn_w_o': _jnp.float32, 'lru_w_in': _jnp.float32, 'lru_conv_w': _jnp.float32, 'lru_conv_b': _jnp.float32, 'lru_w_a': _jnp.float32, 'lru_b_a': _jnp.float32, 'lru_w_x': _jnp.float32, 'lru_b_x': _jnp.float32, 'lru_lambda': _jnp.float32, 'lru_w_out': _jnp.float32, 'mlp_w_up': _jnp.float32, 'mlp_w_down': _jnp.float32, 'ple_w_gate': _jnp.float32, 'ple_w_proj': _jnp.float32}
MOMENT_SCALE = {'norm_mix': 1.966137e-01, 'norm_mlp': 1.300362e-01, 'norm_ple': 1.760412e-02, 'norm_final': 3.247119e+01, 'sc_w_in': 1.158218e-01, 'sc_w_conv': 1.167934e-01, 'sc_w_out': 1.155072e-01, 'attn_w_qkv': 2.243921e-02, 'attn_w_o': 7.055946e-02, 'lru_w_in': 1.771597e-01, 'lru_conv_w': 1.939718e-01, 'lru_conv_b': 6.222726e-01, 'lru_w_a': 2.008273e-02, 'lru_b_a': 4.069457e-02, 'lru_w_x': 4.581045e-02, 'lru_b_x': 9.759458e-02, 'lru_lambda': 9.933087e-02, 'lru_w_out': 2.619523e-01, 'mlp_w_up': 6.612871e-02, 'mlp_w_down': 2.140696e-01, 'ple_w_gate': 1.794910e-02, 'ple_w_proj': 4.394769e-02}


def _to_microbatches(a, axis):
    t = _jnp.moveaxis(a, axis, 0)
    t = t.reshape((N_MICROBATCH, t.shape[0] // N_MICROBATCH) + t.shape[1:])
    return _jnp.moveaxis(t, 1, axis + 1)


def setup_inputs(seed: int = 0) -> dict:
    inp = _fwd_setup_inputs(seed)
    key = _jax.random.fold_in(_jax.random.key(seed), 7919)
    shape, _ = _output_shape()
    out = dict(inp)
    out["loss_target"] = _jax.random.normal(_jax.random.fold_in(key, 0), shape, _jnp.float32)
    for i, name in enumerate(TWIN_WEIGHTS):
        w = inp[name].astype(_jnp.float32)
        if MOMENT_SCALE is None:
            s = _jnp.sqrt(_jnp.mean(_jnp.square(w)) + 1e-30)
        else:
            s = MOMENT_SCALE[name]
        km, kv = _jax.random.split(_jax.random.fold_in(key, i + 1))
        out[name] = w
        out["m_" + name] = s * _jax.random.normal(km, w.shape, _jnp.float32)
        out["v_" + name] = (s * s) * _jax.random.uniform(kv, w.shape, _jnp.float32, 0.5, 1.5)
    if N_MICROBATCH > 1:
        for name, axis in PER_EXAMPLE_BATCH_AXIS.items():
            out[name] = _to_microbatches(out[name], axis)
    return {'x': out['x'], 'p': out['p'], 'positions': out['positions'], 'norm_mix': out['norm_mix'], 'norm_mlp': out['norm_mlp'], 'norm_ple': out['norm_ple'], 'norm_final': out['norm_final'], 'sc_w_in': out['sc_w_in'], 'sc_w_conv': out['sc_w_conv'], 'sc_w_out': out['sc_w_out'], 'attn_w_qkv': out['attn_w_qkv'], 'attn_w_o': out['attn_w_o'], 'lru_w_in': out['lru_w_in'], 'lru_conv_w': out['lru_conv_w'], 'lru_conv_b': out['lru_conv_b'], 'lru_w_a': out['lru_w_a'], 'lru_b_a': out['lru_b_a'], 'lru_w_x': out['lru_w_x'], 'lru_b_x': out['lru_b_x'], 'lru_lambda': out['lru_lambda'], 'lru_w_out': out['lru_w_out'], 'mlp_w_up': out['mlp_w_up'], 'mlp_w_down': out['mlp_w_down'], 'ple_w_gate': out['ple_w_gate'], 'ple_w_proj': out['ple_w_proj'], 'loss_target': out['loss_target'], 'm_norm_mix': out['m_norm_mix'], 'm_norm_mlp': out['m_norm_mlp'], 'm_norm_ple': out['m_norm_ple'], 'm_norm_final': out['m_norm_final'], 'm_sc_w_in': out['m_sc_w_in'], 'm_sc_w_conv': out['m_sc_w_conv'], 'm_sc_w_out': out['m_sc_w_out'], 'm_attn_w_qkv': out['m_attn_w_qkv'], 'm_attn_w_o': out['m_attn_w_o'], 'm_lru_w_in': out['m_lru_w_in'], 'm_lru_conv_w': out['m_lru_conv_w'], 'm_lru_conv_b': out['m_lru_conv_b'], 'm_lru_w_a': out['m_lru_w_a'], 'm_lru_b_a': out['m_lru_b_a'], 'm_lru_w_x': out['m_lru_w_x'], 'm_lru_b_x': out['m_lru_b_x'], 'm_lru_lambda': out['m_lru_lambda'], 'm_lru_w_out': out['m_lru_w_out'], 'm_mlp_w_up': out['m_mlp_w_up'], 'm_mlp_w_down': out['m_mlp_w_down'], 'm_ple_w_gate': out['m_ple_w_gate'], 'm_ple_w_proj': out['m_ple_w_proj'], 'v_norm_mix': out['v_norm_mix'], 'v_norm_mlp': out['v_norm_mlp'], 'v_norm_ple': out['v_norm_ple'], 'v_norm_final': out['v_norm_final'], 'v_sc_w_in': out['v_sc_w_in'], 'v_sc_w_conv': out['v_sc_w_conv'], 'v_sc_w_out': out['v_sc_w_out'], 'v_attn_w_qkv': out['v_attn_w_qkv'], 'v_attn_w_o': out['v_attn_w_o'], 'v_lru_w_in': out['v_lru_w_in'], 'v_lru_conv_w': out['v_lru_conv_w'], 'v_lru_conv_b': out['v_lru_conv_b'], 'v_lru_w_a': out['v_lru_w_a'], 'v_lru_b_a': out['v_lru_b_a'], 'v_lru_w_x': out['v_lru_w_x'], 'v_lru_b_x': out['v_lru_b_x'], 'v_lru_lambda': out['v_lru_lambda'], 'v_lru_w_out': out['v_lru_w_out'], 'v_mlp_w_up': out['v_mlp_w_up'], 'v_mlp_w_down': out['v_mlp_w_down'], 'v_ple_w_gate': out['v_ple_w_gate'], 'v_ple_w_proj': out['v_ple_w_proj']}


def _loss(weights, diff, rest, loss_target):
    with _jax.named_scope("forward"):
        args = {**rest, TWIN_DIFF_INPUT: diff, **{k: w.astype(_WEIGHT_DTYPES[k]) for k, w in weights.items()}}
        y = _forward(args)
    with _jax.named_scope("loss_head"):
        err = _jnp.square(y.astype(_jnp.float32) - loss_target)
        return 0.5 * _jnp.sum(_jnp.mean(err, axis=-1)) if err.ndim else 0.5 * err


def _adamw(w, g, m, v):
    m = ADAM_B1 * m + (1.0 - ADAM_B1) * g
    v = ADAM_B2 * v + (1.0 - ADAM_B2) * _jnp.square(g)
    m_hat = m / (1.0 - ADAM_B1 ** ADAM_STEP)
    v_hat = v / (1.0 - ADAM_B2 ** ADAM_STEP)
    delta = -ADAM_LR * (m_hat / (_jnp.sqrt(v_hat) + ADAM_EPS) + ADAM_WD * w)
    return delta, m, v


def reference(x, p, positions, norm_mix, norm_mlp, norm_ple, norm_final, sc_w_in, sc_w_conv, sc_w_out, attn_w_qkv, attn_w_o, lru_w_in, lru_conv_w, lru_conv_b, lru_w_a, lru_b_a, lru_w_x, lru_b_x, lru_lambda, lru_w_out, mlp_w_up, mlp_w_down, ple_w_gate, ple_w_proj, loss_target, m_norm_mix, m_norm_mlp, m_norm_ple, m_norm_final, m_sc_w_in, m_sc_w_conv, m_sc_w_out, m_attn_w_qkv, m_attn_w_o, m_lru_w_in, m_lru_conv_w, m_lru_conv_b, m_lru_w_a, m_lru_b_a, m_lru_w_x, m_lru_b_x, m_lru_lambda, m_lru_w_out, m_mlp_w_up, m_mlp_w_down, m_ple_w_gate, m_ple_w_proj, v_norm_mix, v_norm_mlp, v_norm_ple, v_norm_final, v_sc_w_in, v_sc_w_conv, v_sc_w_out, v_attn_w_qkv, v_attn_w_o, v_lru_w_in, v_lru_conv_w, v_lru_conv_b, v_lru_w_a, v_lru_b_a, v_lru_w_x, v_lru_b_x, v_lru_lambda, v_lru_w_out, v_mlp_w_up, v_mlp_w_down, v_ple_w_gate, v_ple_w_proj):
    given = dict(x=x, p=p, positions=positions, norm_mix=norm_mix, norm_mlp=norm_mlp, norm_ple=norm_ple, norm_final=norm_final, sc_w_in=sc_w_in, sc_w_conv=sc_w_conv, sc_w_out=sc_w_out, attn_w_qkv=attn_w_qkv, attn_w_o=attn_w_o, lru_w_in=lru_w_in, lru_conv_w=lru_conv_w, lru_conv_b=lru_conv_b, lru_w_a=lru_w_a, lru_b_a=lru_b_a, lru_w_x=lru_w_x, lru_b_x=lru_b_x, lru_lambda=lru_lambda, lru_w_out=lru_w_out, mlp_w_up=mlp_w_up, mlp_w_down=mlp_w_down, ple_w_gate=ple_w_gate, ple_w_proj=ple_w_proj, loss_target=loss_target, m_norm_mix=m_norm_mix, m_norm_mlp=m_norm_mlp, m_norm_ple=m_norm_ple, m_norm_final=m_norm_final, m_sc_w_in=m_sc_w_in, m_sc_w_conv=m_sc_w_conv, m_sc_w_out=m_sc_w_out, m_attn_w_qkv=m_attn_w_qkv, m_attn_w_o=m_attn_w_o, m_lru_w_in=m_lru_w_in, m_lru_conv_w=m_lru_conv_w, m_lru_conv_b=m_lru_conv_b, m_lru_w_a=m_lru_w_a, m_lru_b_a=m_lru_b_a, m_lru_w_x=m_lru_w_x, m_lru_b_x=m_lru_b_x, m_lru_lambda=m_lru_lambda, m_lru_w_out=m_lru_w_out, m_mlp_w_up=m_mlp_w_up, m_mlp_w_down=m_mlp_w_down, m_ple_w_gate=m_ple_w_gate, m_ple_w_proj=m_ple_w_proj, v_norm_mix=v_norm_mix, v_norm_mlp=v_norm_mlp, v_norm_ple=v_norm_ple, v_norm_final=v_norm_final, v_sc_w_in=v_sc_w_in, v_sc_w_conv=v_sc_w_conv, v_sc_w_out=v_sc_w_out, v_attn_w_qkv=v_attn_w_qkv, v_attn_w_o=v_attn_w_o, v_lru_w_in=v_lru_w_in, v_lru_conv_w=v_lru_conv_w, v_lru_conv_b=v_lru_conv_b, v_lru_w_a=v_lru_w_a, v_lru_b_a=v_lru_b_a, v_lru_w_x=v_lru_w_x, v_lru_b_x=v_lru_b_x, v_lru_lambda=v_lru_lambda, v_lru_w_out=v_lru_w_out, v_mlp_w_up=v_mlp_w_up, v_mlp_w_down=v_mlp_w_down, v_ple_w_gate=v_ple_w_gate, v_ple_w_proj=v_ple_w_proj)
    weights = {n: given[n] for n in TWIN_WEIGHTS}
    shared = {n: given[n] for n in SHARED_INPUTS}
    per_example = {n: given[n] for n in ['x', 'p', 'positions']}
    grad_fn = _jax.value_and_grad(_loss, argnums=(0, 1))

    def one_microbatch(ex, loss_target):
        ex = dict(ex)
        diff = ex.pop(TWIN_DIFF_INPUT)
        return grad_fn(weights, diff, {**shared, **ex}, loss_target)

    if N_MICROBATCH == 1:
        loss, (grad_w, grad_x) = one_microbatch(per_example, given["loss_target"])
    else:
        def body(carry, xs):
            loss_sum, grad_sum = carry
            l_k, (gw_k, gx_k) = one_microbatch(xs[0], xs[1])
            with _jax.named_scope("update"):
                return (loss_sum + l_k, _jax.tree.map(_jnp.add, grad_sum, gw_k)), gx_k

        init = (_jnp.zeros((), _jnp.float32), _jax.tree.map(_jnp.zeros_like, weights))
        (loss, grad_w), grad_x = _jax.lax.scan(body, init, (per_example, given["loss_target"]))
    with _jax.named_scope("update"):
        delta_w, new_m, new_v = {}, {}, {}
        for n in TWIN_WEIGHTS:
            delta_w[n], new_m[n], new_v[n] = _adamw(weights[n], grad_w[n], given["m_" + n], given["v_" + n])
    return (loss, grad_x, *[grad_w[n] for n in TWIN_WEIGHTS], *[delta_w[n] for n in TWIN_WEIGHTS],
            *[new_m[n] for n in TWIN_WEIGHTS], *[new_v[n] for n in TWIN_WEIGHTS])
```

```python
import functools
import math

import jax
import jax.numpy as jnp
from jax import lax
from jax.experimental import pallas as pl
from jax.experimental.pallas import tpu as pltpu

F32 = jnp.float32
BF16 = jnp.bfloat16
SDS = jax.ShapeDtypeStruct

N_DEV = 8
RMS_EPS = 1e-6
N_MIXERS = 3
HEAD_DIM = 128
DILATED_PATTERNS = ((128, 1), (512, 4), (2048, 16))
ATTN_BLOCK = 128
ROPE_THETA = 500000.0
ROPE_DIM = HEAD_DIM // 4
LRU_BLOCK = 128
LRU_C = 8.0
ADAM_LR, ADAM_B1, ADAM_B2, ADAM_EPS, ADAM_WD, ADAM_STEP = 0.001, 0.9, 0.999, 1e-08, 0.01, 10

HALO = 16
SUB = 8
PACK_LANES = 512
PACK_ALIGN = 16 * PACK_LANES
VMEM_LIMIT = 56 * 1024 * 1024
NEG = -1e30

SHARD_AXIS = {
    'norm_mix': None, 'norm_mlp': None, 'norm_ple': None, 'norm_final': None,
    'sc_w_in': 2, 'sc_w_conv': 2, 'sc_w_out': 1, 'attn_w_qkv': 2, 'attn_w_o': 1,
    'lru_w_in': 2, 'lru_conv_w': 2, 'lru_conv_b': 1, 'lru_w_a': None, 'lru_b_a': 1,
    'lru_w_x': None, 'lru_b_x': 1, 'lru_lambda': 1, 'lru_w_out': 1,
    'mlp_w_up': 2, 'mlp_w_down': 1, 'ple_w_gate': 1, 'ple_w_proj': 2,
}
WEIGHTS = list(SHARD_AXIS)
BIG = ['sc_w_in', 'sc_w_out', 'attn_w_qkv', 'attn_w_o', 'lru_w_in', 'lru_w_out',
       'mlp_w_up', 'mlp_w_down', 'ple_w_gate', 'ple_w_proj']
SMALL = ['sc_w_conv', 'lru_conv_w', 'lru_conv_b', 'lru_b_a', 'lru_b_x', 'lru_lambda']
REPL = [n for n in WEIGHTS if SHARD_AXIS[n] is None]


def _params(*sem):
    return pltpu.CompilerParams(dimension_semantics=sem or None, vmem_limit_bytes=VMEM_LIMIT)


def _row_tile(t, pref=256):
    tr = min(t, pref)
    assert t % tr == 0 and tr % HALO == 0
    return tr


def _row(tr, c, col=0):
    return pl.BlockSpec((tr, c), lambda i, col=col: (i, col))


def _full(shape):
    return pl.BlockSpec(shape, lambda *_: (0,) * len(shape))


def _sigmoid(x):
    return 1.0 / (1.0 + jnp.exp(-x))


def _expm1(x):
    taylor = x * (1.0 + x * (0.5 + x * (1.0 / 6.0 + x * (1.0 / 24.0 + x * (1.0 / 120.0)))))
    return jnp.where(jnp.abs(x) < 0.1, taylor, jnp.exp(x) - 1.0)


def _softplus(x):
    z = jnp.exp(-jnp.abs(x))
    log1p = jnp.where(z < 0.01, z * (1.0 - z * (0.5 - z * (1.0 / 3.0 - z * 0.25))), jnp.log(1.0 + z))
    return jnp.maximum(x, 0.0) + log1p


_GELU_K = math.sqrt(2.0 / math.pi)


def _gelu_and_grad(x):
    inner = _GELU_K * (x + 0.044715 * x * x * x)
    th = jnp.tanh(inner)
    g = 0.5 * x * (1.0 + th)
    dg = 0.5 * (1.0 + th) + 0.5 * x * (1.0 - th * th) * _GELU_K * (1.0 + 3.0 * 0.044715 * x * x)
    return g, dg


def _shift_down(x, k, prev):
    row = lax.broadcasted_iota(jnp.int32, (SUB, x.shape[1]), 0)
    xr = pltpu.roll(x, k, 0)
    top = jnp.where(row < k, pltpu.roll(prev, k, 0), xr[0:SUB])
    return jnp.concatenate([top, xr[SUB:]], axis=0)


def _shift_up(x, k, nxt):
    r = x.shape[0]
    row = lax.broadcasted_iota(jnp.int32, (SUB, x.shape[1]), 0)
    xr = pltpu.roll(x, r - k, 0)
    bot = jnp.where(row >= SUB - k, pltpu.roll(nxt, SUB - k, 0), xr[r - SUB:r])
    return jnp.concatenate([xr[:r - SUB], bot], axis=0)


_DIMS = {'nn': (((1,), (0,)), ((), ())), 'nt': (((1,), (1,)), ((), ())), 'tn': (((0,), (0,)), ((), ()))}


def _pick_tile(dim, pref):
    if dim <= pref:
        return dim
    return next(c for c in range(pref - pref % 128, 0, -128) if dim % c == 0)


def _mm(a, b, dims, name, out_dtypes=(F32,), a_pro=None, extras=(), epi=None, tm=1024, tn=1024, tk=512):
    if dims == 'nn':
        (m, k), (k2, n) = a.shape, b.shape
    elif dims == 'nt':
        (m, k), (n, k2) = a.shape, b.shape
    else:
        (k, m), (k2, n) = a.shape, b.shape
    assert k == k2, (name, a.shape, b.shape)
    tm, tn, tk = _pick_tile(m, tm), _pick_tile(n, tn), _pick_tile(k, tk)
    assert m % tm == 0 and n % tn == 0 and k % tk == 0, (name, m, n, k)
    nk = k // tk
    a_spec = pl.BlockSpec((tk, tm), lambda i, j, kk: (kk, i)) if dims == 'tn' else pl.BlockSpec((tm, tk), lambda i, j, kk: (i, kk))
    b_spec = pl.BlockSpec((tn, tk), lambda i, j, kk: (j, kk)) if dims == 'nt' else pl.BlockSpec((tk, tn), lambda i, j, kk: (kk, j))
    o_spec = pl.BlockSpec((tm, tn), lambda i, j, kk: (i, j))
    n_ex, n_out = len(extras), len(out_dtypes)
    for e in extras:
        assert e.shape == (m, n), (name, e.shape)

    def body(a_ref, b_ref, *rest):
        ex_refs, out_refs, acc = rest[:n_ex], rest[n_ex:n_ex + n_out], rest[-1]
        kk = pl.program_id(2)

        @pl.when(kk == 0)
        def _():
            acc[...] = jnp.zeros_like(acc)

        av = a_ref[...]
        if a_pro is not None:
            av = a_pro(av.astype(F32))
        acc[...] += lax.dot_general(av.astype(BF16), b_ref[...].astype(BF16), _DIMS[dims],
                                    preferred_element_type=F32)

        @pl.when(kk == nk - 1)
        def _():
            res = acc[...]
            outs = (res,) if epi is None else epi(res, *[e[...] for e in ex_refs])
            for o_ref, o in zip(out_refs, outs):
                o_ref[...] = o.astype(o_ref.dtype)

    out = pl.pallas_call(
        body, grid=(m // tm, n // tn, nk),
        in_specs=[a_spec, b_spec] + [o_spec] * n_ex,
        out_specs=[o_spec] * n_out,
        out_shape=[SDS((m, n), d) for d in out_dtypes],
        scratch_shapes=[pltpu.VMEM((tm, tn), F32)],
        compiler_params=_params('parallel', 'parallel', 'arbitrary'), name=name)(a, b, *extras)
    return out[0] if n_out == 1 else out


def _relu2(u):
    r = jnp.maximum(u, 0.0)
    return r * r


def _rms_fwd(h, g, name):
    t, d = h.shape
    tr = _row_tile(t)

    def body(h_ref, g_ref, o_ref):
        x = h_ref[...]
        r = lax.rsqrt(jnp.mean(x * x, axis=-1, keepdims=True) + RMS_EPS)
        o_ref[...] = (x * r * g_ref[...]).astype(o_ref.dtype)

    return pl.pallas_call(body, grid=(t // tr,), in_specs=[_row(tr, d), _full((1, d))], out_specs=_row(tr, d),
                          out_shape=SDS((t, d), BF16), compiler_params=_params('parallel'), name=name)(h, g.reshape(1, d))


def _rms_bwd(h, g, dhn, dres, name):
    t, d = h.shape
    tr = _row_tile(t)

    def body(h_ref, g_ref, dhn_ref, dres_ref, dh_ref, dg_ref):
        @pl.when(pl.program_id(0) == 0)
        def _():
            dg_ref[...] = jnp.zeros_like(dg_ref)

        x = h_ref[...]
        r = lax.rsqrt(jnp.mean(x * x, axis=-1, keepdims=True) + RMS_EPS)
        dy = dhn_ref[...].astype(F32)
        gy = dy * g_ref[...]
        dx = r * gy - x * (r * r * r) * jnp.mean(gy * x, axis=-1, keepdims=True)
        dh_ref[...] = dres_ref[...] + dx
        dg_ref[...] += jnp.sum(dy * (x * r), axis=0, keepdims=True)

    return pl.pallas_call(body, grid=(t // tr,),
                          in_specs=[_row(tr, d), _full((1, d)), _row(tr, d), _row(tr, d)],
                          out_specs=[_row(tr, d), _full((1, d))],
                          out_shape=[SDS((t, d), F32), SDS((1, d), F32)],
                          compiler_params=_params('arbitrary'), name=name)(h, g.reshape(1, d), dhn, dres)


def _head(h, g, target, name):
    t, d = h.shape
    tr = _row_tile(t)

    def body(h_ref, g_ref, t_ref, dh_ref, loss_ref, dg_ref):
        @pl.when(pl.program_id(0) == 0)
        def _():
            dg_ref[...] = jnp.zeros_like(dg_ref)
            loss_ref[...] = jnp.zeros_like(loss_ref)

        x = h_ref[...]
        gv = g_ref[...]
        r = lax.rsqrt(jnp.mean(x * x, axis=-1, keepdims=True) + RMS_EPS)
        xh = x * r
        e = xh * gv - t_ref[...]
        per_tok = jnp.mean(e * e, axis=-1, keepdims=True)
        loss_ref[...] += jnp.broadcast_to(0.5 * jnp.sum(per_tok, axis=0, keepdims=True), loss_ref.shape)
        dy = e * (1.0 / d)
        gy = dy * gv
        dh_ref[...] = r * gy - x * (r * r * r) * jnp.mean(gy * x, axis=-1, keepdims=True)
        dg_ref[...] += jnp.sum(dy * xh, axis=0, keepdims=True)

    return pl.pallas_call(body, grid=(t // tr,),
                          in_specs=[_row(tr, d), _full((1, d)), _row(tr, d)],
                          out_specs=[_row(tr, d), _full((1, 128)), _full((1, d))],
                          out_shape=[SDS((t, d), F32), SDS((1, 128), F32), SDS((1, d), F32)],
                          compiler_params=_params('arbitrary'), name=name)(h, g.reshape(1, d), target)


def _ple_bwd_gate(dh3, gate, pp, name):
    t, d = dh3.shape
    tr = _row_tile(t)

    def body(dh_ref, g_ref, pp_ref, dpp_ref, dgl_ref):
        dh = dh_ref[...]
        gt = g_ref[...]
        dpp_ref[...] = (dh * gt).astype(dpp_ref.dtype)
        dgl_ref[...] = (dh * pp_ref[...] * gt * (1.0 - gt)).astype(dgl_ref.dtype)

    return pl.pallas_call(body, grid=(t // tr,), in_specs=[_row(tr, d)] * 3, out_specs=[_row(tr, d)] * 2,
                          out_shape=[SDS((t, d), BF16), SDS((t, d), BF16)],
                          compiler_params=_params('parallel'), name=name)(dh3, gate, pp)


def _halo_prev(tr, c, col=0):
    return pl.BlockSpec((HALO, c), lambda i, col=col: (jnp.maximum(i * (tr // HALO) - 1, 0), col))


def _halo_next(tr, c, t, col=0):
    return pl.BlockSpec((HALO, c), lambda i, col=col: (jnp.minimum((i + 1) * (tr // HALO), t // HALO - 1), col))


def _sc_fwd(z, w, name):
    t, c3 = z.shape
    c = c3 // 3
    tr = _row_tile(t)

    def body(z_ref, zp_ref, w_ref, y_ref):
        i = pl.program_id(0)
        zz = z_ref[...]
        gb, cx = zz[:, :c], zz[:, c:2 * c] * zz[:, 2 * c:]
        zp = zp_ref[SUB:HALO, :]
        cxp = jnp.where(i > 0, zp[:, c:2 * c] * zp[:, 2 * c:], 0.0)
        wv = w_ref[...]
        conv = wv[2:3] * cx + wv[1:2] * _shift_down(cx, 1, cxp) + wv[0:1] * _shift_down(cx, 2, cxp)
        y_ref[...] = (gb * conv).astype(y_ref.dtype)

    return pl.pallas_call(body, grid=(t // tr,),
                          in_specs=[_row(tr, c3), _halo_prev(tr, c3), _full((3, c))],
                          out_specs=_row(tr, c), out_shape=SDS((t, c), BF16),
                          compiler_params=_params('parallel'), name=name)(z, z, w)


def _sc_bwd(dy, z, w, name):
    t, c3 = z.shape
    c = c3 // 3
    tr = _row_tile(t)
    nt = t // tr

    def body(dy_ref, dyn_ref, z_ref, zp_ref, zn_ref, w_ref, dz_ref, dw_ref):
        i = pl.program_id(0)

        @pl.when(i == 0)
        def _():
            dw_ref[...] = jnp.zeros_like(dw_ref)

        zz = z_ref[...]
        gb, gc, xi = zz[:, :c], zz[:, c:2 * c], zz[:, 2 * c:]
        cx = gc * xi
        zp = zp_ref[SUB:HALO, :]
        cxp = jnp.where(i > 0, zp[:, c:2 * c] * zp[:, 2 * c:], 0.0)
        wv = w_ref[...]
        cx1, cx2 = _shift_down(cx, 1, cxp), _shift_down(cx, 2, cxp)
        conv = wv[2:3] * cx + wv[1:2] * cx1 + wv[0:1] * cx2
        dyv = dy_ref[...]
        dconv = dyv * gb
        dcn = jnp.where(i < nt - 1, dyn_ref[0:SUB, :] * zn_ref[0:SUB, :c], 0.0)
        dcx = wv[2:3] * dconv + wv[1:2] * _shift_up(dconv, 1, dcn) + wv[0:1] * _shift_up(dconv, 2, dcn)
        dz_ref[:, :c] = (dyv * conv).astype(dz_ref.dtype)
        dz_ref[:, c:2 * c] = (dcx * xi).astype(dz_ref.dtype)
        dz_ref[:, 2 * c:] = (dcx * gc).astype(dz_ref.dtype)
        dw_ref[...] += jnp.concatenate([jnp.sum(dconv * cx2, axis=0, keepdims=True),
                                        jnp.sum(dconv * cx1, axis=0, keepdims=True),
                                        jnp.sum(dconv * cx, axis=0, keepdims=True)], axis=0)

    return pl.pallas_call(body, grid=(nt,),
                          in_specs=[_row(tr, c), _halo_next(tr, c, t), _row(tr, c3), _halo_prev(tr, c3),
                                    _halo_next(tr, c3, t), _full((3, c))],
                          out_specs=[_row(tr, c3), _full((3, c))],
                          out_shape=[SDS((t, c3), BF16), SDS((3, c), F32)],
                          compiler_params=_params('arbitrary'), name=name)(dy, dy, z, z, z, w)


def _rope_tables(pos_ref, invf_ref, sign):
    lane = lax.broadcasted_iota(jnp.int32, (pos_ref.shape[0], HEAD_DIM), 1)
    ang = pos_ref[...].astype(F32) * invf_ref[...]
    half = ROPE_DIM // 2
    cos = jnp.where(lane < ROPE_DIM, jnp.cos(ang), 1.0)
    sin = jnp.sin(ang) * sign
    sin = jnp.where(lane < half, -sin, jnp.where(lane < ROPE_DIM, sin, 0.0))
    return lane, cos, sin


def _rope_apply(x, lane, cos, sin):
    half = ROPE_DIM // 2
    xs = jnp.where(lane < half, pltpu.roll(x, HEAD_DIM - half, 1), pltpu.roll(x, half, 1))
    return x * cos + xs * sin


def _rope_fwd(qkv, pos, invf, name):
    t, w3 = qkv.shape
    w = w3 // 3
    tr = _row_tile(t)

    def body(q_ref, k_ref, v_ref, pos_ref, invf_ref, qo_ref, ko_ref, vo_ref):
        lane, cos, sin = _rope_tables(pos_ref, invf_ref, 1.0)
        for hh in range(w // HEAD_DIM):
            cs = slice(hh * HEAD_DIM, (hh + 1) * HEAD_DIM)
            qo_ref[:, cs] = _rope_apply(q_ref[:, cs], lane, cos, sin).astype(qo_ref.dtype)
            ko_ref[:, cs] = _rope_apply(k_ref[:, cs], lane, cos, sin).astype(ko_ref.dtype)
        vo_ref[...] = v_ref[...].astype(vo_ref.dtype)

    return pl.pallas_call(body, grid=(t // tr,),
                          in_specs=[_row(tr, w, 0), _row(tr, w, 1), _row(tr, w, 2), _row(tr, 1), _full((1, HEAD_DIM))],
                          out_specs=[_row(tr, w)] * 3, out_shape=[SDS((t, w), BF16)] * 3,
                          compiler_params=_params('parallel'), name=name)(qkv, qkv, qkv, pos, invf)


def _rope_bwd(dqs, dks, dvs, pos, invf, name):
    t, d = dqs[0].shape
    ng = len(dqs)
    w = ng * d
    tr = _row_tile(t)

    def body(*refs):
        dq_refs, dk_refs, dv_refs = refs[:ng], refs[ng:2 * ng], refs[2 * ng:3 * ng]
        pos_ref, invf_ref, o_ref = refs[3 * ng], refs[3 * ng + 1], refs[3 * ng + 2]
        lane, cos, sin = _rope_tables(pos_ref, invf_ref, -1.0)
        for g in range(ng):
            for hh in range(d // HEAD_DIM):
                cs = slice(hh * HEAD_DIM, (hh + 1) * HEAD_DIM)
                base = g * d + hh * HEAD_DIM
                o_ref[:, base:base + HEAD_DIM] = _rope_apply(dq_refs[g][:, cs], lane, cos, sin).astype(o_ref.dtype)
                o_ref[:, w + base:w + base + HEAD_DIM] = _rope_apply(dk_refs[g][:, cs], lane, cos, sin).astype(o_ref.dtype)
            o_ref[:, 2 * w + g * d:2 * w + (g + 1) * d] = dv_refs[g][...].astype(o_ref.dtype)

    return pl.pallas_call(body, grid=(t // tr,),
                          in_specs=[_row(tr, d)] * (3 * ng) + [_row(tr, 1), _full((1, HEAD_DIM))],
                          out_specs=_row(tr, 3 * w), out_shape=SDS((t, 3 * w), BF16),
                          compiler_params=_params('parallel'), name=name)(*dqs, *dks, *dvs, pos, invf)


def _attn_masks():
    qi = lax.broadcasted_iota(jnp.int32, (ATTN_BLOCK, ATTN_BLOCK), 0)
    kj = lax.broadcasted_iota(jnp.int32, (ATTN_BLOCK, ATTN_BLOCK), 1)
    return kj >= qi, kj <= qi


def _attn_cols(l, width):
    ncol = width // HEAD_DIM
    cpb = max(1, min(ncol, 32 // (l // ATTN_BLOCK)))
    assert ncol % cpb == 0
    return cpb


def _attn_fwd(q, k, v, name):
    l, width = q.shape
    cpb = _attn_cols(l, width)
    nb = l // ATTN_BLOCK
    scale = HEAD_DIM ** -0.5

    def body(q_ref, k_ref, v_ref, o_ref, lse_ref):
        m_prev, m_cur = _attn_masks()
        for col in range(cpb):
            cs = slice(col * HEAD_DIM, (col + 1) * HEAD_DIM)

            def step(b, carry, cs=cs):
                r0 = pl.multiple_of(b * ATTN_BLOCK, ATTN_BLOCK)
                rp = pl.multiple_of(jnp.maximum(b - 1, 0) * ATTN_BLOCK, ATTN_BLOCK)
                qb = q_ref[pl.ds(r0, ATTN_BLOCK), cs]
                s_p = lax.dot_general(qb, k_ref[pl.ds(rp, ATTN_BLOCK), cs], _DIMS['nt'], preferred_element_type=F32) * scale
                s_c = lax.dot_general(qb, k_ref[pl.ds(r0, ATTN_BLOCK), cs], _DIMS['nt'], preferred_element_type=F32) * scale
                s_p = jnp.where(jnp.logical_and(m_prev, b > 0), s_p, NEG)
                s_c = jnp.where(m_cur, s_c, NEG)
                m = jnp.maximum(jnp.max(s_p, axis=-1, keepdims=True), jnp.max(s_c, axis=-1, keepdims=True))
                p_p, p_c = jnp.exp(s_p - m), jnp.exp(s_c - m)
                den = jnp.sum(p_p, axis=-1, keepdims=True) + jnp.sum(p_c, axis=-1, keepdims=True)
                acc = jnp.dot(p_p.astype(BF16), v_ref[pl.ds(rp, ATTN_BLOCK), cs], preferred_element_type=F32)
                acc += jnp.dot(p_c.astype(BF16), v_ref[pl.ds(r0, ATTN_BLOCK), cs], preferred_element_type=F32)
                o_ref[pl.ds(r0, ATTN_BLOCK), cs] = acc / den
                lse_ref[pl.ds(r0, ATTN_BLOCK), cs] = jnp.broadcast_to(m + jnp.log(den), (ATTN_BLOCK, HEAD_DIM))
                return carry

            lax.fori_loop(0, nb, step, 0)

    spec = pl.BlockSpec((l, cpb * HEAD_DIM), lambda j: (0, j))
    return pl.pallas_call(body, grid=(width // (cpb * HEAD_DIM),), in_specs=[spec] * 3, out_specs=[spec] * 2,
                          out_shape=[SDS((l, width), F32)] * 2,
                          compiler_params=_params('parallel'), name=name)(q, k, v)


def _attn_bwd(q, k, v, do, lse, delta, name):
    l, width = q.shape
    cpb = _attn_cols(l, width)
    nb = l // ATTN_BLOCK
    scale = HEAD_DIM ** -0.5

    def body(q_ref, k_ref, v_ref, do_ref, lse_ref, dl_ref, dq_ref, dk_ref, dv_ref):
        m_prev, m_cur = _attn_masks()
        dk_ref[...] = jnp.zeros_like(dk_ref)
        dv_ref[...] = jnp.zeros_like(dv_ref)
        for col in range(cpb):
            cs = slice(col * HEAD_DIM, (col + 1) * HEAD_DIM)

            def step(b, carry, cs=cs):
                r0 = pl.multiple_of(b * ATTN_BLOCK, ATTN_BLOCK)
                rp = pl.multiple_of(jnp.maximum(b - 1, 0) * ATTN_BLOCK, ATTN_BLOCK)
                qb, dob = q_ref[pl.ds(r0, ATTN_BLOCK), cs], do_ref[pl.ds(r0, ATTN_BLOCK), cs]
                kp, kc = k_ref[pl.ds(rp, ATTN_BLOCK), cs], k_ref[pl.ds(r0, ATTN_BLOCK), cs]
                vp, vc = v_ref[pl.ds(rp, ATTN_BLOCK), cs], v_ref[pl.ds(r0, ATTN_BLOCK), cs]
                lse_b = lse_ref[pl.ds(r0, ATTN_BLOCK), cs]
                dl_b = dl_ref[pl.ds(r0, ATTN_BLOCK), cs]
                s_p = lax.dot_general(qb, kp, _DIMS['nt'], preferred_element_type=F32) * scale
                s_c = lax.dot_general(qb, kc, _DIMS['nt'], preferred_element_type=F32) * scale
                p_p = jnp.exp(jnp.where(jnp.logical_and(m_prev, b > 0), s_p, NEG) - lse_b)
                p_c = jnp.exp(jnp.where(m_cur, s_c, NEG) - lse_b)
                dp_p = lax.dot_general(dob, vp, _DIMS['nt'], preferred_element_type=F32)
                dp_c = lax.dot_general(dob, vc, _DIMS['nt'], preferred_element_type=F32)
                ds_p = (p_p * (dp_p - dl_b) * scale).astype(BF16)
                ds_c = (p_c * (dp_c - dl_b) * scale).astype(BF16)
                dq_ref[pl.ds(r0, ATTN_BLOCK), cs] = (jnp.dot(ds_p, kp, preferred_element_type=F32)
                                                     + jnp.dot(ds_c, kc, preferred_element_type=F32))
                dk_ref[pl.ds(rp, ATTN_BLOCK), cs] += lax.dot_general(ds_p, qb, _DIMS['tn'], preferred_element_type=F32)
                dk_ref[pl.ds(r0, ATTN_BLOCK), cs] += lax.dot_general(ds_c, qb, _DIMS['tn'], preferred_element_type=F32)
                dv_ref[pl.ds(rp, ATTN_BLOCK), cs] += lax.dot_general(p_p.astype(BF16), dob, _DIMS['tn'], preferred_element_type=F32)
                dv_ref[pl.ds(r0, ATTN_BLOCK), cs] += lax.dot_general(p_c.astype(BF16), dob, _DIMS['tn'], preferred_element_type=F32)
                return carry

            lax.fori_loop(0, nb, step, 0)

    spec = pl.BlockSpec((l, cpb * HEAD_DIM), lambda j: (0, j))
    return pl.pallas_call(body, grid=(width // (cpb * HEAD_DIM),), in_specs=[spec] * 6, out_specs=[spec] * 3,
                          out_shape=[SDS((l, width), F32)] * 3,
                          compiler_params=_params('parallel'), name=name)(q, k, v, do, lse, delta)


def _attn_combine(os_, lses, name):
    t, d = os_[0].shape
    ng = len(os_)
    tr = _row_tile(t)

    def body(*refs):
        o_refs, l_refs, o_out, lse_out = refs[:ng], refs[ng:2 * ng], refs[2 * ng], refs[2 * ng + 1]
        ls = [r[...] for r in l_refs]
        m = functools.reduce(jnp.maximum, ls)
        ws = [jnp.exp(x - m) for x in ls]
        den = functools.reduce(lambda a, b: a + b, ws)
        acc = functools.reduce(lambda a, b: a + b, [w * o[...] for w, o in zip(ws, o_refs)])
        o_out[...] = (acc / den).astype(o_out.dtype)
        lse_out[...] = m + jnp.log(den)

    return pl.pallas_call(body, grid=(t // tr,), in_specs=[_row(tr, d)] * (2 * ng), out_specs=[_row(tr, d)] * 2,
                          out_shape=[SDS((t, d), BF16), SDS((t, d), F32)],
                          compiler_params=_params('parallel'), name=name)(*os_, *lses)


def _delta_epilogue(acc, o):
    prod = acc * o.astype(F32)
    segs = [jnp.broadcast_to(jnp.sum(prod[:, s:s + HEAD_DIM], axis=-1, keepdims=True), (acc.shape[0], HEAD_DIM))
            for s in range(0, acc.shape[1], HEAD_DIM)]
    return acc, jnp.concatenate(segs, axis=-1)


LRU_TILE = 128


def _lru_gates(xr, wa_ref, ba, wx_ref, bx, lam):
    nb = wa_ref.shape[0]
    xb = xr.astype(BF16)
    ra = jnp.concatenate([jnp.dot(xb[:, n * LRU_BLOCK:(n + 1) * LRU_BLOCK], wa_ref[n], preferred_element_type=F32)
                          for n in range(nb)], axis=-1) + ba
    ia = jnp.concatenate([jnp.dot(xb[:, n * LRU_BLOCK:(n + 1) * LRU_BLOCK], wx_ref[n], preferred_element_type=F32)
                          for n in range(nb)], axis=-1) + bx
    r, ig = _sigmoid(ra), _sigmoid(ia)
    sp = _softplus(-lam)
    log_a = -LRU_C * r * sp
    a = jnp.exp(log_a)
    mult = jnp.sqrt(-_expm1(2.0 * log_a))
    return xb, r, ig, sp, a, mult


def _lru_fwd(z, cw, cb, wa, ba, wx, bx, lam, name):
    t, c2 = z.shape
    c = c2 // 2
    nb = c // LRU_BLOCK
    tr = _row_tile(t, LRU_TILE)

    def body(g_ref, x_ref, xp_ref, cw_ref, cb_ref, wa_ref, ba_ref, wx_ref, bx_ref, lam_ref,
             y_ref, hs_ref, xr_ref, car_ref):
        i = pl.program_id(0)

        @pl.when(i == 0)
        def _():
            car_ref[...] = jnp.zeros_like(car_ref)

        x0 = x_ref[...]
        xp = jnp.where(i > 0, xp_ref[SUB:HALO, :], 0.0)
        cwv = cw_ref[...]
        xr = (cb_ref[...] + cwv[3:4] * x0 + cwv[2:3] * _shift_down(x0, 1, xp)
              + cwv[1:2] * _shift_down(x0, 2, xp) + cwv[0:1] * _shift_down(x0, 3, xp))
        xr_ref[...] = xr
        _, _, ig, _, a, mult = _lru_gates(xr, wa_ref, ba_ref[...], wx_ref, bx_ref[...], lam_ref[...])
        u = mult * (ig * xr)
        row = lax.broadcasted_iota(jnp.int32, (SUB, c), 0)
        car = car_ref[...]
        for j in range(tr // SUB):
            ab, ub = a[j * SUB:(j + 1) * SUB], u[j * SUB:(j + 1) * SUB]
            for s in (1, 2, 4):
                a_sh = jnp.where(row >= s, pltpu.roll(ab, s, 0), 1.0)
                u_sh = jnp.where(row >= s, pltpu.roll(ub, s, 0), 0.0)
                ub = ab * u_sh + ub
                ab = ab * a_sh
            hb = ub + ab * car
            hs_ref[j * SUB:(j + 1) * SUB, :] = hb
            car = jnp.broadcast_to(hb[SUB - 1:SUB], (SUB, c))
        car_ref[...] = car
        gl, _ = _gelu_and_grad(g_ref[...])
        y_ref[...] = (hs_ref[...] * gl).astype(y_ref.dtype)

    return pl.pallas_call(
        body, grid=(t // tr,),
        in_specs=[_row(tr, c, 0), _row(tr, c, 1), _halo_prev(tr, c, 1), _full((4, c)), _full((1, c)),
                  _full((nb, LRU_BLOCK, LRU_BLOCK)), _full((1, c)), _full((nb, LRU_BLOCK, LRU_BLOCK)), _full((1, c)), _full((1, c))],
        out_specs=[_row(tr, c)] * 3,
        out_shape=[SDS((t, c), BF16), SDS((t, c), F32), SDS((t, c), F32)],
        scratch_shapes=[pltpu.VMEM((SUB, c), F32)],
        compiler_params=_params('arbitrary'), name=name)(
            z, z, z, cw, cb.reshape(1, c), wa, ba.reshape(1, c), wx, bx.reshape(1, c), lam.reshape(1, c))


def _lru_bwd(dy, z, xr, hs, cw, wa, ba, wx, bx, lam, name):
    t, c2 = z.shape
    c = c2 // 2
    nb = c // LRU_BLOCK
    tr = _row_tile(t, LRU_TILE)
    nt = t // tr

    def rev(col=0):
        return pl.BlockSpec((tr, c), lambda i, col=col: (nt - 1 - i, col))

    def rev_prev(col=0):
        return pl.BlockSpec((HALO, c), lambda i, col=col: (jnp.maximum((nt - 1 - i) * (tr // HALO) - 1, 0), col))

    def body(dy_ref, g_ref, x_ref, xp_ref, xr_ref, hs_ref, hp_ref, cw_ref, wa_ref, ba_ref, wx_ref, bx_ref, lam_ref,
             dz_ref, dwa_ref, dwx_ref, dvec_ref, lcar_ref, ahead_ref, dxhead_ref, lam_s):
        i = pl.program_id(0)
        first_tile = i == nt - 1

        @pl.when(i == 0)
        def _():
            lcar_ref[...] = jnp.zeros_like(lcar_ref)
            ahead_ref[...] = jnp.zeros_like(ahead_ref)
            dxhead_ref[...] = jnp.zeros_like(dxhead_ref)
            dwa_ref[...] = jnp.zeros_like(dwa_ref)
            dwx_ref[...] = jnp.zeros_like(dwx_ref)
            dvec_ref[...] = jnp.zeros_like(dvec_ref)

        xrv = xr_ref[...]
        lamv = lam_ref[...]
        xb, r, ig, sp, a, mult = _lru_gates(xrv, wa_ref, ba_ref[...], wx_ref, bx_ref[...], lamv)
        hsv = hs_ref[...]
        dyv = dy_ref[...]
        gl, dgl = _gelu_and_grad(g_ref[...])
        dhs = dyv * gl
        dz_ref[:, :c] = (dyv * hsv * dgl).astype(dz_ref.dtype)

        a_next = _shift_up(a, 1, ahead_ref[...])
        row = lax.broadcasted_iota(jnp.int32, (SUB, c), 0)
        car = lcar_ref[...]
        for j in reversed(range(tr // SUB)):
            ab, ub = a_next[j * SUB:(j + 1) * SUB], dhs[j * SUB:(j + 1) * SUB]
            for s in (1, 2, 4):
                a_sh = jnp.where(row < SUB - s, pltpu.roll(ab, SUB - s, 0), 1.0)
                u_sh = jnp.where(row < SUB - s, pltpu.roll(ub, SUB - s, 0), 0.0)
                ub = ab * u_sh + ub
                ab = ab * a_sh
            lb = ub + ab * car
            lam_s[j * SUB:(j + 1) * SUB, :] = lb
            car = jnp.broadcast_to(lb[0:1], (SUB, c))
        lcar_ref[...] = car
        ahead_ref[...] = a[0:SUB]
        lmb = lam_s[...]

        hp = jnp.where(first_tile, 0.0, hp_ref[SUB:HALO, :])
        h_prev = _shift_down(hsv, 1, hp)
        d_a = lmb * h_prev
        d_mult = lmb * (ig * xrv)
        d_ixr = lmb * mult
        d_ig = d_ixr * xrv
        dxr = d_ixr * ig
        d_la = d_a * a - d_mult * (a * a) / mult
        d_r = d_la * (-LRU_C * sp)
        d_sp = jnp.sum(d_la * (-LRU_C * r), axis=0, keepdims=True)
        d_ra = d_r * r * (1.0 - r)
        d_ia = d_ig * ig * (1.0 - ig)
        d_rab, d_iab = d_ra.astype(BF16), d_ia.astype(BF16)
        parts = []
        for n in range(nb):
            cs = slice(n * LRU_BLOCK, (n + 1) * LRU_BLOCK)
            parts.append(lax.dot_general(d_rab[:, cs], wa_ref[n], _DIMS['nt'], preferred_element_type=F32)
                         + lax.dot_general(d_iab[:, cs], wx_ref[n], _DIMS['nt'], preferred_element_type=F32))
            dwa_ref[n] += lax.dot_general(xb[:, cs], d_rab[:, cs], _DIMS['tn'], preferred_element_type=F32)
            dwx_ref[n] += lax.dot_general(xb[:, cs], d_iab[:, cs], _DIMS['tn'], preferred_element_type=F32)
        dxr = dxr + jnp.concatenate(parts, axis=-1)

        cwv = cw_ref[...]
        nxt = dxhead_ref[...]
        dx0 = (cwv[3:4] * dxr + cwv[2:3] * _shift_up(dxr, 1, nxt) + cwv[1:2] * _shift_up(dxr, 2, nxt)
               + cwv[0:1] * _shift_up(dxr, 3, nxt))
        dxhead_ref[...] = dxr[0:SUB]
        dz_ref[:, c:] = dx0.astype(dz_ref.dtype)

        x0 = x_ref[...]
        xp = jnp.where(first_tile, 0.0, xp_ref[SUB:HALO, :])
        sums = [jnp.sum(d_ra, axis=0, keepdims=True), jnp.sum(d_ia, axis=0, keepdims=True),
                d_sp * (-_sigmoid(-lamv)), jnp.sum(dxr, axis=0, keepdims=True),
                jnp.sum(dxr * _shift_down(x0, 3, xp), axis=0, keepdims=True),
                jnp.sum(dxr * _shift_down(x0, 2, xp), axis=0, keepdims=True),
                jnp.sum(dxr * _shift_down(x0, 1, xp), axis=0, keepdims=True),
                jnp.sum(dxr * x0, axis=0, keepdims=True)]
        dvec_ref[...] += jnp.concatenate(sums, axis=0)

    wspec = _full((nb, LRU_BLOCK, LRU_BLOCK))
    return pl.pallas_call(
        body, grid=(nt,),
        in_specs=[rev(), rev(0), rev(1), rev_prev(1), rev(), rev(), rev_prev(), _full((4, c)),
                  wspec, _full((1, c)), wspec, _full((1, c)), _full((1, c))],
        out_specs=[pl.BlockSpec((tr, c2), lambda i: (nt - 1 - i, 0)), wspec, wspec, _full((SUB, c))],
        out_shape=[SDS((t, c2), BF16), SDS((nb, LRU_BLOCK, LRU_BLOCK), F32), SDS((nb, LRU_BLOCK, LRU_BLOCK), F32),
                   SDS((SUB, c), F32)],
        scratch_shapes=[pltpu.VMEM((SUB, c), F32), pltpu.VMEM((SUB, c), F32), pltpu.VMEM((SUB, c), F32),
                        pltpu.VMEM((tr, c), F32)],
        compiler_params=_params('arbitrary'), name=name)(
            dy, z, z, z, xr, hs, hs, cw, wa, ba.reshape(1, c), wx, bx.reshape(1, c), lam.reshape(1, c))


def _dilate(x, d):
    t, w = x.shape
    return x.reshape(t // d, d * w)


def _local_step(x, p, pos, w, target):
    t, d = x.shape
    depth = p.shape[0]
    half = ROPE_DIM // 2
    invf = ROPE_THETA ** (-2.0 * jnp.arange(half, dtype=F32) / ROPE_DIM)
    invf = jnp.concatenate([invf, invf, jnp.zeros((HEAD_DIM - ROPE_DIM,), F32)]).reshape(1, HEAD_DIM)
    ng = len(DILATED_PATTERNS)
    saved = []
    h = x
    for i in range(depth):
        kind, j = i % N_MIXERS, i // N_MIXERS
        s = {'h0': h}
        hn = _rms_fwd(h, w['norm_mix'][i], f'rms_mix_fwd_{i}')
        s['hn'] = hn
        if kind == 0:
            z = _mm(hn, w['sc_w_in'][j], 'nn', f'sc_in_{i}')
            y = _sc_fwd(z, w['sc_w_conv'][j], f'sc_conv_fwd_{i}')
            h1 = _mm(y, w['sc_w_out'][j], 'nn', f'sc_out_{i}', extras=(h,), epi=lambda acc, res: (acc + res,))
            s.update(z=z, y=y)
        elif kind == 1:
            qkv = _mm(hn, w['attn_w_qkv'][j], 'nn', f'attn_qkv_{i}')
            q, k, v = _rope_fwd(qkv, pos, invf, f'rope_fwd_{i}')
            os_, lses, views = [], [], []
            for g, (_, dil) in enumerate(DILATED_PATTERNS):
                qg, kg, vg = (_dilate(a[:, g * d:(g + 1) * d], dil) for a in (q, k, v))
                og, lg = _attn_fwd(qg, kg, vg, f'attn_fwd_{i}_g{g}')
                os_.append(og.reshape(t, d))
                lses.append(lg.reshape(t, d))
                views.append((qg, kg, vg))
            o, lse = _attn_combine(os_, lses, f'attn_combine_{i}')
            h1 = _mm(o, w['attn_w_o'][j], 'nn', f'attn_out_{i}', extras=(h,), epi=lambda acc, res: (acc + res,))
            s.update(views=views, o=o, lse=lse)
        else:
            z = _mm(hn, w['lru_w_in'][j], 'nn', f'lru_in_{i}', tn=1280)
            y, hs, xr = _lru_fwd(z, w['lru_conv_w'][j], w['lru_conv_b'][j], w['lru_w_a'][j], w['lru_b_a'][j],
                                 w['lru_w_x'][j], w['lru_b_x'][j], w['lru_lambda'][j], f'lru_fwd_{i}')
            h1 = _mm(y, w['lru_w_out'][j], 'nn', f'lru_out_{i}', extras=(h,), epi=lambda acc, res: (acc + res,), tk=640)
            s.update(z=z, y=y, hs=hs, xr=xr)
        s['h1'] = h1
        hm = _rms_fwd(h1, w['norm_mlp'][i], f'rms_mlp_fwd_{i}')
        u = _mm(hm, w['mlp_w_up'][i], 'nn', f'mlp_up_{i}', out_dtypes=(BF16,))
        h2 = _mm(u, w['mlp_w_down'][i], 'nn', f'mlp_down_{i}', a_pro=_relu2, extras=(h1,), epi=lambda acc, res: (acc + res,))
        hp = _rms_fwd(h2, w['norm_ple'][i], f'rms_ple_fwd_{i}')
        pp = _mm(p[i], w['ple_w_proj'][i], 'nn', f'ple_proj_{i}')
        h3, gate = _mm(hp, w['ple_w_gate'][i], 'nn', f'ple_gate_{i}', out_dtypes=(F32, F32), extras=(pp, h2),
                       epi=lambda acc, ppv, res: (res + _sigmoid(acc) * ppv, _sigmoid(acc)))
        s.update(hm=hm, u=u, h2=h2, hp=hp, pp=pp, gate=gate)
        saved.append(s)
        h = h3

    dh, loss, dg_final = _head(h, w['norm_final'], target, 'loss_head')
    grads = {n: [None] * len(w[n]) for n in w if n != 'norm_final'}
    grads['norm_final'] = dg_final.reshape(d)
    for i in reversed(range(depth)):
        kind, j = i % N_MIXERS, i // N_MIXERS
        s = saved[i]
        dpp, dgl = _ple_bwd_gate(dh, s['gate'], s['pp'], f'ple_bwd_gate_{i}')
        grads['ple_w_proj'][i] = _mm(p[i], dpp, 'tn', f'ple_dproj_{i}', out_dtypes=(BF16,))
        grads['ple_w_gate'][i] = _mm(s['hp'], dgl, 'tn', f'ple_dgate_{i}', out_dtypes=(BF16,))
        dhp = _mm(dgl, w['ple_w_gate'][i], 'nt', f'ple_dhp_{i}')
        dh, dg = _rms_bwd(s['h2'], w['norm_ple'][i], dhp, dh, f'rms_ple_bwd_{i}')
        grads['norm_ple'][i] = dg.reshape(d)
        du = _mm(dh, w['mlp_w_down'][i], 'nt', f'mlp_du_{i}', out_dtypes=(BF16,), extras=(s['u'],),
                 epi=lambda acc, uv: (acc * 2.0 * jnp.maximum(uv.astype(F32), 0.0),))
        grads['mlp_w_down'][i] = _mm(s['u'], dh, 'tn', f'mlp_ddown_{i}', out_dtypes=(BF16,), a_pro=_relu2)
        grads['mlp_w_up'][i] = _mm(s['hm'], du, 'tn', f'mlp_dup_{i}', out_dtypes=(BF16,))
        dhm = _mm(du, w['mlp_w_up'][i], 'nt', f'mlp_dhm_{i}')
        dh, dg = _rms_bwd(s['h1'], w['norm_mlp'][i], dhm, dh, f'rms_mlp_bwd_{i}')
        grads['norm_mlp'][i] = dg.reshape(d)
        if kind == 0:
            dy = _mm(dh, w['sc_w_out'][j], 'nt', f'sc_dy_{i}')
            grads['sc_w_out'][j] = _mm(s['y'], dh, 'tn', f'sc_dout_{i}', out_dtypes=(BF16,))
            dz, dwc = _sc_bwd(dy, s['z'], w['sc_w_conv'][j], f'sc_conv_bwd_{i}')
            grads['sc_w_conv'][j] = dwc
            grads['sc_w_in'][j] = _mm(s['hn'], dz, 'tn', f'sc_din_{i}', out_dtypes=(BF16,))
            dhn = _mm(dz, w['sc_w_in'][j], 'nt', f'sc_dhn_{i}')
        elif kind == 1:
            do, delta = _mm(dh, w['attn_w_o'][j], 'nt', f'attn_do_{i}', out_dtypes=(BF16, F32), extras=(s['o'],),
                            epi=_delta_epilogue, tn=d)
            grads['attn_w_o'][j] = _mm(s['o'], dh, 'tn', f'attn_dwo_{i}', out_dtypes=(BF16,))
            dqs, dks, dvs = [], [], []
            for g, (_, dil) in enumerate(DILATED_PATTERNS):
                qg, kg, vg = s['views'][g]
                dqg, dkg, dvg = _attn_bwd(qg, kg, vg, _dilate(do, dil), _dilate(s['lse'], dil), _dilate(delta, dil),
                                          f'attn_bwd_{i}_g{g}')
                dqs.append(dqg.reshape(t, d))
                dks.append(dkg.reshape(t, d))
                dvs.append(dvg.reshape(t, d))
            dqkv = _rope_bwd(dqs, dks, dvs, pos, invf, f'rope_bwd_{i}')
            grads['attn_w_qkv'][j] = _mm(s['hn'], dqkv, 'tn', f'attn_dqkv_{i}', out_dtypes=(BF16,))
            dhn = _mm(dqkv, w['attn_w_qkv'][j], 'nt', f'attn_dhn_{i}')
        else:
            dy = _mm(dh, w['lru_w_out'][j], 'nt', f'lru_dy_{i}', tn=1280)
            grads['lru_w_out'][j] = _mm(s['y'], dh, 'tn', f'lru_dout_{i}', out_dtypes=(BF16,), tm=1280)
            dz, dwa, dwx, dvec = _lru_bwd(dy, s['z'], s['xr'], s['hs'], w['lru_conv_w'][j], w['lru_w_a'][j],
                                          w['lru_b_a'][j], w['lru_w_x'][j], w['lru_b_x'][j], w['lru_lambda'][j],
                                          f'lru_bwd_{i}')
            grads['lru_w_a'][j], grads['lru_w_x'][j] = dwa, dwx
            grads['lru_b_a'][j], grads['lru_b_x'][j] = dvec[0:1], dvec[1:2]
            grads['lru_lambda'][j], grads['lru_conv_b'][j] = dvec[2:3], dvec[3:4]
            grads['lru_conv_w'][j] = dvec[4:8]
            grads['lru_w_in'][j] = _mm(s['hn'], dz, 'tn', f'lru_din_{i}', out_dtypes=(BF16,), tn=1280)
            dhn = _mm(dz, w['lru_w_in'][j], 'nt', f'lru_dhn_{i}', tk=640)
        dh, dg = _rms_bwd(s['h0'], w['norm_mix'][i], dhn, dh, f'rms_mix_bwd_{i}')
        grads['norm_mix'][i] = dg.reshape(d)
    return loss, dh, grads


_MESH = pl.DeviceIdType.MESH
_ANY = pl.BlockSpec(memory_space=pl.ANY)


def _all_gather(flat, name):
    r, l = flat.shape

    def body(x_ref, out_ref, send_sems, recv_sems, local_sem):
        x, y, c = lax.axis_index('x'), lax.axis_index('y'), lax.axis_index('c')
        me, sibling = (x, y, c), (x, y, 1 - c)
        chips = [(1 - x, y), (x, 1 - y), (1 - x, 1 - y)]

        def slab(px, py, pc):
            return out_ref.at[4 * px + 2 * py + pc]

        def copy(k, block, to, src=None):
            return pltpu.make_async_remote_copy(
                src_ref=slab(*block) if src is None else src, dst_ref=slab(*block),
                send_sem=send_sems.at[k], recv_sem=recv_sems.at[k], device_id=to, device_id_type=_MESH)

        mine = pltpu.make_async_copy(x_ref, slab(*me), local_sem)
        mine.start()
        first = [copy(0, me, sibling, src=x_ref)]
        first += [copy(1 + j, me, (*chip, c), src=x_ref) for j, chip in enumerate(chips)]
        for cp in first:
            cp.start()
        passed = [copy(4 + j, (*chip, c), sibling) for j, chip in enumerate(chips)]
        for j, chip in enumerate(chips):
            copy(1 + j, (*chip, c), me).wait_recv()
            passed[j].start()
        copy(0, sibling, me).wait_recv()
        for j, chip in enumerate(chips):
            copy(4 + j, (*chip, 1 - c), me).wait_recv()
        for cp in first + passed:
            cp.wait_send()
        mine.wait()

    return pl.pallas_call(
        body, out_shape=SDS((N_DEV, r, l), flat.dtype), in_specs=[_ANY], out_specs=_ANY,
        scratch_shapes=[pltpu.SemaphoreType.DMA((7,)), pltpu.SemaphoreType.DMA((7,)), pltpu.SemaphoreType.DMA],
        name=name)(flat)


def _all_to_all(slabs, name):
    _, r, l = slabs.shape

    def body(in_ref, out_ref, send_sems, recv_sems, local_sem):
        x, y, c = lax.axis_index('x'), lax.axis_index('y'), lax.axis_index('c')
        my_idx = 4 * x + 2 * y + c

        def peer(k):
            return (1 - x if k & 4 else x, 1 - y if k & 2 else y, 1 - c if k & 1 else c)

        mine = pltpu.make_async_copy(in_ref.at[my_idx], out_ref.at[my_idx], local_sem)
        mine.start()
        copies = []
        for k in range(1, N_DEV):
            px, py, pc = peer(k)
            copies.append(pltpu.make_async_remote_copy(
                src_ref=in_ref.at[4 * px + 2 * py + pc], dst_ref=out_ref.at[my_idx],
                send_sem=send_sems.at[k - 1], recv_sem=recv_sems.at[k - 1],
                device_id=(px, py, pc), device_id_type=_MESH))
        for cp in copies:
            cp.start()
        for cp in copies:
            cp.wait_recv()
        for cp in copies:
            cp.wait_send()
        mine.wait()

    return pl.pallas_call(
        body, out_shape=SDS(slabs.shape, slabs.dtype), in_specs=[_ANY], out_specs=_ANY,
        scratch_shapes=[pltpu.SemaphoreType.DMA((7,)), pltpu.SemaphoreType.DMA((7,)), pltpu.SemaphoreType.DMA],
        name=name)(slabs)


def _sum_slabs(slabs, rows, name):
    n, _, l = slabs.shape
    tr = next(c for c in range(512, 0, -16) if rows % c == 0)

    def body(in_ref, o_ref):
        acc = in_ref[0].astype(F32)
        for s in range(1, n):
            acc = acc + in_ref[s].astype(F32)
        o_ref[...] = acc

    return pl.pallas_call(body, grid=(rows // tr,),
                          in_specs=[pl.BlockSpec((n, tr, l), lambda i: (0, i, 0))],
                          out_specs=pl.BlockSpec((tr, l), lambda i: (i, 0)),
                          out_shape=SDS((rows, l), F32), compiler_params=_params('parallel'), name=name)(slabs)


def _adamw(wgt, g, m, v, name):
    shape = wgt.shape
    cols = shape[-1]
    rows = wgt.size // cols
    tr = 512 if rows % 512 == 0 else rows
    c1 = 1.0 - ADAM_B1 ** ADAM_STEP
    c2 = 1.0 - ADAM_B2 ** ADAM_STEP

    def body(w_ref, g_ref, m_ref, v_ref, d_ref, mo_ref, vo_ref):
        gv = g_ref[...]
        mn = ADAM_B1 * m_ref[...] + (1.0 - ADAM_B1) * gv
        vn = ADAM_B2 * v_ref[...] + (1.0 - ADAM_B2) * (gv * gv)
        d_ref[...] = -ADAM_LR * ((mn / c1) / (jnp.sqrt(vn / c2) + ADAM_EPS) + ADAM_WD * w_ref[...])
        mo_ref[...] = mn
        vo_ref[...] = vn

    spec = pl.BlockSpec((tr, cols), lambda i: (i, 0))
    outs = pl.pallas_call(body, grid=(rows // tr,), in_specs=[spec] * 4, out_specs=[spec] * 3,
                          out_shape=[SDS((rows, cols), F32)] * 3, compiler_params=_params('parallel'), name=name)(
        *(a.reshape(rows, cols) for a in (wgt, g, m, v)))
    return tuple(o.reshape(shape) for o in outs)


def _pad_rows(flat, dtype):
    if flat.dtype == F32 and dtype == 'bits':
        flat = lax.bitcast_convert_type(flat, BF16).reshape(-1)
    n = flat.shape[0]
    padded = -(-n // PACK_ALIGN) * PACK_ALIGN
    return jnp.pad(flat, (0, padded - n)).reshape(padded // PACK_LANES, PACK_LANES)


def _bits_to_f32(rows, n):
    return lax.bitcast_convert_type(rows.reshape(-1)[:2 * n].reshape(n, 2), F32)


def _split_shards(full, axis):
    shape = full.shape
    n8 = shape[axis] // N_DEV
    a = full.reshape(shape[:axis] + (N_DEV, n8) + shape[axis + 1:])
    return jnp.moveaxis(a, axis, 0).reshape(N_DEV, -1)


def _join_shards(stack, shard_shape, axis):
    a = stack.reshape((N_DEV,) + tuple(shard_shape))
    a = jnp.moveaxis(a, 0, axis)
    return a.reshape(shard_shape[:axis] + (N_DEV * shard_shape[axis],) + shard_shape[axis + 1:])


def kernel(x, p, positions, norm_mix, norm_mlp, norm_ple, norm_final, sc_w_in, sc_w_conv, sc_w_out, attn_w_qkv, attn_w_o, lru_w_in, lru_conv_w, lru_conv_b, lru_w_a, lru_b_a, lru_w_x, lru_b_x, lru_lambda, lru_w_out, mlp_w_up, mlp_w_down, ple_w_gate, ple_w_proj, loss_target, m_norm_mix, m_norm_mlp, m_norm_ple, m_norm_final, m_sc_w_in, m_sc_w_conv, m_sc_w_out, m_attn_w_qkv, m_attn_w_o, m_lru_w_in, m_lru_conv_w, m_lru_conv_b, m_lru_w_a, m_lru_b_a, m_lru_w_x, m_lru_b_x, m_lru_lambda, m_lru_w_out, m_mlp_w_up, m_mlp_w_down, m_ple_w_gate, m_ple_w_proj, v_norm_mix, v_norm_mlp, v_norm_ple, v_norm_final, v_sc_w_in, v_sc_w_conv, v_sc_w_out, v_attn_w_qkv, v_attn_w_o, v_lru_w_in, v_lru_conv_w, v_lru_conv_b, v_lru_w_a, v_lru_b_a, v_lru_w_x, v_lru_b_x, v_lru_lambda, v_lru_w_out, v_mlp_w_up, v_mlp_w_down, v_ple_w_gate, v_ple_w_proj):
    loc = dict(locals())
    shards = {n: loc[n] for n in WEIGHTS}
    moms = {n: loc['m_' + n] for n in WEIGHTS}
    vels = {n: loc['v_' + n] for n in WEIGHTS}

    pieces, layout, off = [], {}, 0
    for n in BIG + SMALL:
        rows = _pad_rows(shards[n].reshape(-1).astype(BF16) if n in BIG else shards[n].reshape(-1),
                         'bf16' if n in BIG else 'bits')
        layout[n] = (off, rows.shape[0])
        off += rows.shape[0]
        pieces.append(rows)
    gathered = _all_gather(jnp.concatenate(pieces, axis=0), 'gather_weights')
    full = {}
    for n in BIG + SMALL:
        o, r = layout[n]
        sh = shards[n].shape
        blk = gathered[:, o:o + r].reshape(N_DEV, -1)
        if n in BIG:
            stack = blk[:, :shards[n].size]
        else:
            stack = lax.bitcast_convert_type(blk[:, :2 * shards[n].size].reshape(N_DEV, shards[n].size, 2), F32)
        full[n] = _join_shards(stack, sh, SHARD_AXIS[n])
    for n in REPL:
        full[n] = shards[n]
    full['lru_w_a'] = full['lru_w_a'].astype(BF16)
    full['lru_w_x'] = full['lru_w_x'].astype(BF16)

    t, d = x.shape[1], x.shape[2]
    loss, grad_x, grads = _local_step(x.reshape(t, d), p.reshape(p.shape[0], t, p.shape[3]),
                                      positions.reshape(t, 1), full, loss_target.reshape(t, d))
    gfull = {n: (grads[n] if n == 'norm_final' else jnp.stack(grads[n], axis=0).reshape(
        (len(grads[n]),) + tuple(full[n].shape[1:]))) for n in WEIGHTS}

    big_rows, glayout, off = [], {}, 0
    for n in BIG:
        st = _split_shards(gfull[n].astype(BF16), SHARD_AXIS[n])
        padded = -(-st.shape[1] // PACK_ALIGN) * PACK_ALIGN
        rows = jnp.pad(st, ((0, 0), (0, padded - st.shape[1]))).reshape(N_DEV, padded // PACK_LANES, PACK_LANES)
        glayout[n] = (off, rows.shape[1])
        off += rows.shape[1]
        big_rows.append(rows)
    rows_big = off
    for n in SMALL + REPL:
        if n in SMALL:
            st = _split_shards(gfull[n].astype(F32), SHARD_AXIS[n])
        else:
            st = jnp.broadcast_to(gfull[n].astype(F32).reshape(1, -1), (N_DEV, gfull[n].size))
        bits = lax.bitcast_convert_type(st, BF16).reshape(N_DEV, -1)
        padded = -(-bits.shape[1] // PACK_ALIGN) * PACK_ALIGN
        rows = jnp.pad(bits, ((0, 0), (0, padded - bits.shape[1]))).reshape(N_DEV, padded // PACK_LANES, PACK_LANES)
        glayout[n] = (off - rows_big, rows.shape[1])
        off += rows.shape[1]
        big_rows.append(rows)
    received = _all_to_all(jnp.concatenate(big_rows, axis=1), 'exchange_grads')
    sum_big = _sum_slabs(received, rows_big, 'sum_grads_bf16')
    small_bits = received[:, rows_big:].reshape(N_DEV, off - rows_big, PACK_LANES // 2, 2)
    sum_small = _sum_slabs(lax.bitcast_convert_type(small_bits, F32), off - rows_big, 'sum_grads_f32')
    gshard = {}
    for n in BIG:
        o, r = glayout[n]
        gshard[n] = sum_big[o:o + r].reshape(-1)[:shards[n].size].reshape(shards[n].shape)
    for n in SMALL + REPL:
        o, r = glayout[n]
        gshard[n] = sum_small[o:o + r].reshape(-1)[:shards[n].size].reshape(shards[n].shape)

    deltas, new_m, new_v = {}, {}, {}
    for n in WEIGHTS:
        deltas[n], new_m[n], new_v[n] = _adamw(shards[n], gshard[n], moms[n], vels[n], f'adamw_{n}')
    loss = lax.psum(loss[0, 0], ('x', 'y', 'c'))
    return (loss, grad_x.reshape(x.shape), *[gshard[n] for n in WEIGHTS], *[deltas[n] for n in WEIGHTS],
            *[new_m[n] for n in WEIGHTS], *[new_v[n] for n in WEIGHTS])
```

```python
import functools
import math

import jax
import jax.numpy as jnp
from jax import lax
from jax.experimental import pallas as pl
from jax.experimental.pallas import tpu as pltpu

F32 = jnp.float32
BF16 = jnp.bfloat16
SDS = jax.ShapeDtypeStruct

N_DEV = 8
RMS_EPS = 1e-6
N_MIXERS = 3
HEAD_DIM = 128
DILATED_PATTERNS = ((128, 1), (512, 4), (2048, 16))
ATTN_BLOCK = 128
ROPE_THETA = 500000.0
ROPE_DIM = HEAD_DIM // 4
LRU_BLOCK = 128
LRU_C = 8.0
ADAM_LR, ADAM_B1, ADAM_B2, ADAM_EPS, ADAM_WD, ADAM_STEP = 0.001, 0.9, 0.999, 1e-08, 0.01, 10

HALO = 16
SUB = 8
VMEM_LIMIT = 56 * 1024 * 1024
NEG = -1e30

SHARD_AXIS = {
    'norm_mix': None, 'norm_mlp': None, 'norm_ple': None, 'norm_final': None,
    'sc_w_in': 2, 'sc_w_conv': 2, 'sc_w_out': 1, 'attn_w_qkv': 2, 'attn_w_o': 1,
    'lru_w_in': 2, 'lru_conv_w': 2, 'lru_conv_b': 1, 'lru_w_a': None, 'lru_b_a': 1,
    'lru_w_x': None, 'lru_b_x': 1, 'lru_lambda': 1, 'lru_w_out': 1,
    'mlp_w_up': 2, 'mlp_w_down': 1, 'ple_w_gate': 1, 'ple_w_proj': 2,
}
WEIGHTS = list(SHARD_AXIS)


def _params(*sem):
    return pltpu.CompilerParams(dimension_semantics=sem or None, vmem_limit_bytes=VMEM_LIMIT)


def _row_tile(t, pref=256):
    tr = min(t, pref)
    assert t % tr == 0 and tr % HALO == 0
    return tr


def _row(tr, c, col=0):
    return pl.BlockSpec((tr, c), lambda i, col=col: (i, col))


def _full(shape):
    return pl.BlockSpec(shape, lambda *_: (0,) * len(shape))


def _sigmoid(x):
    return 1.0 / (1.0 + jnp.exp(-x))


def _expm1(x):
    taylor = x * (1.0 + x * (0.5 + x * (1.0 / 6.0 + x * (1.0 / 24.0 + x * (1.0 / 120.0)))))
    return jnp.where(jnp.abs(x) < 0.1, taylor, jnp.exp(x) - 1.0)


def _softplus(x):
    z = jnp.exp(-jnp.abs(x))
    log1p = jnp.where(z < 0.01, z * (1.0 - z * (0.5 - z * (1.0 / 3.0 - z * 0.25))), jnp.log(1.0 + z))
    return jnp.maximum(x, 0.0) + log1p


_GELU_K = math.sqrt(2.0 / math.pi)


def _gelu_and_grad(x):
    inner = _GELU_K * (x + 0.044715 * x * x * x)
    th = jnp.tanh(inner)
    g = 0.5 * x * (1.0 + th)
    dg = 0.5 * (1.0 + th) + 0.5 * x * (1.0 - th * th) * _GELU_K * (1.0 + 3.0 * 0.044715 * x * x)
    return g, dg


def _shift_down(x, k, prev):
    row = lax.broadcasted_iota(jnp.int32, (SUB, x.shape[1]), 0)
    xr = pltpu.roll(x, k, 0)
    top = jnp.where(row < k, pltpu.roll(prev, k, 0), xr[0:SUB])
    return jnp.concatenate([top, xr[SUB:]], axis=0)


def _shift_up(x, k, nxt):
    r = x.shape[0]
    row = lax.broadcasted_iota(jnp.int32, (SUB, x.shape[1]), 0)
    xr = pltpu.roll(x, r - k, 0)
    bot = jnp.where(row >= SUB - k, pltpu.roll(nxt, SUB - k, 0), xr[r - SUB:r])
    return jnp.concatenate([xr[:r - SUB], bot], axis=0)


_DIMS = {'nn': (((1,), (0,)), ((), ())), 'nt': (((1,), (1,)), ((), ())), 'tn': (((0,), (0,)), ((), ()))}


def _pick_tile(dim, pref):
    if dim <= pref:
        return dim
    return next(c for c in range(pref - pref % 128, 0, -128) if dim % c == 0)


def _mm(a, b, dims, name, out_dtypes=(F32,), a_pro=None, extras=(), epi=None, tm=1024, tn=1024, tk=512,
        out_stacked=False):
    stacked = b.ndim == 3
    b_rows, b_cols = (b.shape[1], N_DEV * b.shape[2]) if stacked else b.shape
    if dims == 'nn':
        (m, k), (k2, n) = a.shape, (b_rows, b_cols)
    elif dims == 'nt':
        (m, k), (n, k2) = a.shape, (b_rows, b_cols)
    else:
        (k, m), (k2, n) = a.shape, (b_rows, b_cols)
    assert k == k2, (name, a.shape, b.shape)
    assert not (stacked and dims == 'tn') and not (out_stacked and (extras or dims != 'tn'))
    tm = _pick_tile(m, tm)
    tn = _pick_tile(n // N_DEV if (out_stacked or (stacked and dims == 'nn')) else n, tn)
    tk = _pick_tile(k // N_DEV if (stacked and dims == 'nt') else k, tk)
    assert m % tm == 0 and n % tn == 0 and k % tk == 0, (name, m, n, k)
    nk = k // tk
    a_spec = pl.BlockSpec((tk, tm), lambda i, j, kk: (kk, i)) if dims == 'tn' else pl.BlockSpec((tm, tk), lambda i, j, kk: (i, kk))
    if not stacked:
        b_spec = pl.BlockSpec((tn, tk), lambda i, j, kk: (j, kk)) if dims == 'nt' else pl.BlockSpec((tk, tn), lambda i, j, kk: (kk, j))
    elif dims == 'nn':
        per = b.shape[2] // tn
        b_spec = pl.BlockSpec((None, tk, tn), lambda i, j, kk: (j // per, kk, j % per))
    else:
        per = b.shape[2] // tk
        b_spec = pl.BlockSpec((None, tn, tk), lambda i, j, kk: (kk // per, j, kk % per))
    if out_stacked:
        per_o = n // N_DEV // tn
        o_spec = pl.BlockSpec((None, tm, tn), lambda i, j, kk: (j // per_o, i, j % per_o))
        o_shape = (N_DEV, m, n // N_DEV)
    else:
        o_spec = pl.BlockSpec((tm, tn), lambda i, j, kk: (i, j))
        o_shape = (m, n)
    n_ex, n_out = len(extras), len(out_dtypes)
    for e in extras:
        assert e.shape == (m, n), (name, e.shape)

    def body(a_ref, b_ref, *rest):
        ex_refs, out_refs, acc = rest[:n_ex], rest[n_ex:n_ex + n_out], rest[-1]
        kk = pl.program_id(2)

        @pl.when(kk == 0)
        def _():
            acc[...] = jnp.zeros_like(acc)

        av = a_ref[...]
        if a_pro is not None:
            av = a_pro(av.astype(F32))
        acc[...] += lax.dot_general(av.astype(BF16), b_ref[...].astype(BF16), _DIMS[dims],
                                    preferred_element_type=F32)

        @pl.when(kk == nk - 1)
        def _():
            res = acc[...]
            outs = (res,) if epi is None else epi(res, *[e[...] for e in ex_refs])
            for o_ref, o in zip(out_refs, outs):
                o_ref[...] = o.astype(o_ref.dtype)

    out = pl.pallas_call(
        body, grid=(m // tm, n // tn, nk),
        in_specs=[a_spec, b_spec] + [o_spec] * n_ex,
        out_specs=[o_spec] * n_out,
        out_shape=[SDS(o_shape, d) for d in out_dtypes],
        scratch_shapes=[pltpu.VMEM((tm, tn), F32)],
        compiler_params=_params('parallel', 'parallel', 'arbitrary'), name=name)(a, b, *extras)
    return out[0] if n_out == 1 else out


def _relu2(u):
    r = jnp.maximum(u, 0.0)
    return r * r


def _rms_fwd(h, g, name):
    t, d = h.shape
    tr = _row_tile(t)

    def body(h_ref, g_ref, o_ref):
        x = h_ref[...]
        r = lax.rsqrt(jnp.mean(x * x, axis=-1, keepdims=True) + RMS_EPS)
        o_ref[...] = (x * r * g_ref[...]).astype(o_ref.dtype)

    return pl.pallas_call(body, grid=(t // tr,), in_specs=[_row(tr, d), _full((1, d))], out_specs=_row(tr, d),
                          out_shape=SDS((t, d), BF16), compiler_params=_params('parallel'), name=name)(h, g.reshape(1, d))


def _rms_bwd(h, g, dhn, dres, name):
    t, d = h.shape
    tr = _row_tile(t)

    def body(h_ref, g_ref, dhn_ref, dres_ref, dh_ref, dg_ref):
        @pl.when(pl.program_id(0) == 0)
        def _():
            dg_ref[...] = jnp.zeros_like(dg_ref)

        x = h_ref[...]
        r = lax.rsqrt(jnp.mean(x * x, axis=-1, keepdims=True) + RMS_EPS)
        dy = dhn_ref[...].astype(F32)
        gy = dy * g_ref[...]
        dx = r * gy - x * (r * r * r) * jnp.mean(gy * x, axis=-1, keepdims=True)
        dh_ref[...] = dres_ref[...] + dx
        dg_ref[...] += jnp.sum(dy * (x * r), axis=0, keepdims=True)

    return pl.pallas_call(body, grid=(t // tr,),
                          in_specs=[_row(tr, d), _full((1, d)), _row(tr, d), _row(tr, d)],
                          out_specs=[_row(tr, d), _full((1, d))],
                          out_shape=[SDS((t, d), F32), SDS((1, d), F32)],
                          compiler_params=_params('arbitrary'), name=name)(h, g.reshape(1, d), dhn, dres)


def _head(h, g, target, name):
    t, d = h.shape
    tr = _row_tile(t)

    def body(h_ref, g_ref, t_ref, dh_ref, loss_ref, dg_ref):
        @pl.when(pl.program_id(0) == 0)
        def _():
            dg_ref[...] = jnp.zeros_like(dg_ref)
            loss_ref[...] = jnp.zeros_like(loss_ref)

        x = h_ref[...]
        gv = g_ref[...]
        r = lax.rsqrt(jnp.mean(x * x, axis=-1, keepdims=True) + RMS_EPS)
        xh = x * r
        e = xh * gv - t_ref[...]
        per_tok = jnp.mean(e * e, axis=-1, keepdims=True)
        loss_ref[...] += jnp.broadcast_to(0.5 * jnp.sum(per_tok, axis=0, keepdims=True), loss_ref.shape)
        dy = e * (1.0 / d)
        gy = dy * gv
        dh_ref[...] = r * gy - x * (r * r * r) * jnp.mean(gy * x, axis=-1, keepdims=True)
        dg_ref[...] += jnp.sum(dy * xh, axis=0, keepdims=True)

    return pl.pallas_call(body, grid=(t // tr,),
                          in_specs=[_row(tr, d), _full((1, d)), _row(tr, d)],
                          out_specs=[_row(tr, d), _full((1, 128)), _full((1, d))],
                          out_shape=[SDS((t, d), F32), SDS((1, 128), F32), SDS((1, d), F32)],
                          compiler_params=_params('arbitrary'), name=name)(h, g.reshape(1, d), target)


def _ple_bwd_gate(dh3, gate, pp, name):
    t, d = dh3.shape
    tr = _row_tile(t)

    def body(dh_ref, g_ref, pp_ref, dpp_ref, dgl_ref):
        dh = dh_ref[...]
        gt = g_ref[...]
        dpp_ref[...] = (dh * gt).astype(dpp_ref.dtype)
        dgl_ref[...] = (dh * pp_ref[...] * gt * (1.0 - gt)).astype(dgl_ref.dtype)

    return pl.pallas_call(body, grid=(t // tr,), in_specs=[_row(tr, d)] * 3, out_specs=[_row(tr, d)] * 2,
                          out_shape=[SDS((t, d), BF16), SDS((t, d), BF16)],
                          compiler_params=_params('parallel'), name=name)(dh3, gate, pp)


def _halo_prev(tr, c, col=0):
    return pl.BlockSpec((HALO, c), lambda i, col=col: (jnp.maximum(i * (tr // HALO) - 1, 0), col))


def _halo_next(tr, c, t, col=0):
    return pl.BlockSpec((HALO, c), lambda i, col=col: (jnp.minimum((i + 1) * (tr // HALO), t // HALO - 1), col))


def _sc_fwd(z, w, name):
    t, c3 = z.shape
    c = c3 // 3
    tr = _row_tile(t)

    def body(z_ref, zp_ref, w_ref, y_ref):
        i = pl.program_id(0)
        zz = z_ref[...]
        gb, cx = zz[:, :c], zz[:, c:2 * c] * zz[:, 2 * c:]
        zp = zp_ref[SUB:HALO, :]
        cxp = jnp.where(i > 0, zp[:, c:2 * c] * zp[:, 2 * c:], 0.0)
        wv = w_ref[...]
        conv = wv[2:3] * cx + wv[1:2] * _shift_down(cx, 1, cxp) + wv[0:1] * _shift_down(cx, 2, cxp)
        y_ref[...] = (gb * conv).astype(y_ref.dtype)

    return pl.pallas_call(body, grid=(t // tr,),
                          in_specs=[_row(tr, c3), _halo_prev(tr, c3), _full((3, c))],
                          out_specs=_row(tr, c), out_shape=SDS((t, c), BF16),
                          compiler_params=_params('parallel'), name=name)(z, z, w)


def _sc_bwd(dy, z, w, name):
    t, c3 = z.shape
    c = c3 // 3
    tr = _row_tile(t)
    nt = t // tr

    def body(dy_ref, dyn_ref, z_ref, zp_ref, zn_ref, w_ref, dz_ref, dw_ref):
        i = pl.program_id(0)

        @pl.when(i == 0)
        def _():
            dw_ref[...] = jnp.zeros_like(dw_ref)

        zz = z_ref[...]
        gb, gc, xi = zz[:, :c], zz[:, c:2 * c], zz[:, 2 * c:]
        cx = gc * xi
        zp = zp_ref[SUB:HALO, :]
        cxp = jnp.where(i > 0, zp[:, c:2 * c] * zp[:, 2 * c:], 0.0)
        wv = w_ref[...]
        cx1, cx2 = _shift_down(cx, 1, cxp), _shift_down(cx, 2, cxp)
        conv = wv[2:3] * cx + wv[1:2] * cx1 + wv[0:1] * cx2
        dyv = dy_ref[...]
        dconv = dyv * gb
        dcn = jnp.where(i < nt - 1, dyn_ref[0:SUB, :] * zn_ref[0:SUB, :c], 0.0)
        dcx = wv[2:3] * dconv + wv[1:2] * _shift_up(dconv, 1, dcn) + wv[0:1] * _shift_up(dconv, 2, dcn)
        dz_ref[:, :c] = (dyv * conv).astype(dz_ref.dtype)
        dz_ref[:, c:2 * c] = (dcx * xi).astype(dz_ref.dtype)
        dz_ref[:, 2 * c:] = (dcx * gc).astype(dz_ref.dtype)
        dw_ref[...] += jnp.concatenate([jnp.sum(dconv * cx2, axis=0, keepdims=True),
                                        jnp.sum(dconv * cx1, axis=0, keepdims=True),
                                        jnp.sum(dconv * cx, axis=0, keepdims=True)], axis=0)

    return pl.pallas_call(body, grid=(nt,),
                          in_specs=[_row(tr, c), _halo_next(tr, c, t), _row(tr, c3), _halo_prev(tr, c3),
                                    _halo_next(tr, c3, t), _full((3, c))],
                          out_specs=[_row(tr, c3), _full((3, c))],
                          out_shape=[SDS((t, c3), BF16), SDS((3, c), F32)],
                          compiler_params=_params('arbitrary'), name=name)(dy, dy, z, z, z, w)


def _rope_tables(pos_ref, invf_ref, sign):
    lane = lax.broadcasted_iota(jnp.int32, (pos_ref.shape[0], HEAD_DIM), 1)
    ang = pos_ref[...].astype(F32) * invf_ref[...]
    half = ROPE_DIM // 2
    cos = jnp.where(lane < ROPE_DIM, jnp.cos(ang), 1.0)
    sin = jnp.sin(ang) * sign
    sin = jnp.where(lane < half, -sin, jnp.where(lane < ROPE_DIM, sin, 0.0))
    return lane, cos, sin


def _rope_apply(x, lane, cos, sin):
    half = ROPE_DIM // 2
    xs = jnp.where(lane < half, pltpu.roll(x, HEAD_DIM - half, 1), pltpu.roll(x, half, 1))
    return x * cos + xs * sin


def _rope_fwd(qkv, pos, invf, name):
    t, w3 = qkv.shape
    w = w3 // 3
    tr = _row_tile(t)

    def body(q_ref, k_ref, v_ref, pos_ref, invf_ref, qo_ref, ko_ref, vo_ref):
        lane, cos, sin = _rope_tables(pos_ref, invf_ref, 1.0)
        for hh in range(w // HEAD_DIM):
            cs = slice(hh * HEAD_DIM, (hh + 1) * HEAD_DIM)
            qo_ref[:, cs] = _rope_apply(q_ref[:, cs], lane, cos, sin).astype(qo_ref.dtype)
            ko_ref[:, cs] = _rope_apply(k_ref[:, cs], lane, cos, sin).astype(ko_ref.dtype)
        vo_ref[...] = v_ref[...].astype(vo_ref.dtype)

    return pl.pallas_call(body, grid=(t // tr,),
                          in_specs=[_row(tr, w, 0), _row(tr, w, 1), _row(tr, w, 2), _row(tr, 1), _full((1, HEAD_DIM))],
                          out_specs=[_row(tr, w)] * 3, out_shape=[SDS((t, w), BF16)] * 3,
                          compiler_params=_params('parallel'), name=name)(qkv, qkv, qkv, pos, invf)


def _rope_bwd(dqs, dks, dvs, pos, invf, name):
    t, d = dqs[0].shape
    ng = len(dqs)
    w = ng * d
    tr = _row_tile(t)

    def body(*refs):
        dq_refs, dk_refs, dv_refs = refs[:ng], refs[ng:2 * ng], refs[2 * ng:3 * ng]
        pos_ref, invf_ref, o_ref = refs[3 * ng], refs[3 * ng + 1], refs[3 * ng + 2]
        lane, cos, sin = _rope_tables(pos_ref, invf_ref, -1.0)
        for g in range(ng):
            for hh in range(d // HEAD_DIM):
                cs = slice(hh * HEAD_DIM, (hh + 1) * HEAD_DIM)
                base = g * d + hh * HEAD_DIM
                o_ref[:, base:base + HEAD_DIM] = _rope_apply(dq_refs[g][:, cs], lane, cos, sin).astype(o_ref.dtype)
                o_ref[:, w + base:w + base + HEAD_DIM] = _rope_apply(dk_refs[g][:, cs], lane, cos, sin).astype(o_ref.dtype)
            o_ref[:, 2 * w + g * d:2 * w + (g + 1) * d] = dv_refs[g][...].astype(o_ref.dtype)

    return pl.pallas_call(body, grid=(t // tr,),
                          in_specs=[_row(tr, d)] * (3 * ng) + [_row(tr, 1), _full((1, HEAD_DIM))],
                          out_specs=_row(tr, 3 * w), out_shape=SDS((t, 3 * w), BF16),
                          compiler_params=_params('parallel'), name=name)(*dqs, *dks, *dvs, pos, invf)


def _attn_masks():
    qi = lax.broadcasted_iota(jnp.int32, (ATTN_BLOCK, ATTN_BLOCK), 0)
    kj = lax.broadcasted_iota(jnp.int32, (ATTN_BLOCK, ATTN_BLOCK), 1)
    return kj >= qi, kj <= qi


def _attn_cols(l, width):
    ncol = width // HEAD_DIM
    cpb = max(1, min(ncol, 32 // (l // ATTN_BLOCK)))
    assert ncol % cpb == 0
    return cpb


def _attn_fwd(q, k, v, name):
    l, width = q.shape
    cpb = _attn_cols(l, width)
    nb = l // ATTN_BLOCK
    scale = HEAD_DIM ** -0.5

    def body(q_ref, k_ref, v_ref, o_ref, lse_ref):
        m_prev, m_cur = _attn_masks()
        for col in range(cpb):
            cs = slice(col * HEAD_DIM, (col + 1) * HEAD_DIM)

            def step(b, carry, cs=cs):
                r0 = pl.multiple_of(b * ATTN_BLOCK, ATTN_BLOCK)
                rp = pl.multiple_of(jnp.maximum(b - 1, 0) * ATTN_BLOCK, ATTN_BLOCK)
                qb = q_ref[pl.ds(r0, ATTN_BLOCK), cs]
                s_p = lax.dot_general(qb, k_ref[pl.ds(rp, ATTN_BLOCK), cs], _DIMS['nt'], preferred_element_type=F32) * scale
                s_c = lax.dot_general(qb, k_ref[pl.ds(r0, ATTN_BLOCK), cs], _DIMS['nt'], preferred_element_type=F32) * scale
                s_p = jnp.where(jnp.logical_and(m_prev, b > 0), s_p, NEG)
                s_c = jnp.where(m_cur, s_c, NEG)
                m = jnp.maximum(jnp.max(s_p, axis=-1, keepdims=True), jnp.max(s_c, axis=-1, keepdims=True))
                p_p, p_c = jnp.exp(s_p - m), jnp.exp(s_c - m)
                den = jnp.sum(p_p, axis=-1, keepdims=True) + jnp.sum(p_c, axis=-1, keepdims=True)
                acc = jnp.dot(p_p.astype(BF16), v_ref[pl.ds(rp, ATTN_BLOCK), cs], preferred_element_type=F32)
                acc += jnp.dot(p_c.astype(BF16), v_ref[pl.ds(r0, ATTN_BLOCK), cs], preferred_element_type=F32)
                o_ref[pl.ds(r0, ATTN_BLOCK), cs] = acc / den
                lse_ref[pl.ds(r0, ATTN_BLOCK), cs] = jnp.broadcast_to(m + jnp.log(den), (ATTN_BLOCK, HEAD_DIM))
                return carry

            lax.fori_loop(0, nb, step, 0)

    spec = pl.BlockSpec((l, cpb * HEAD_DIM), lambda j: (0, j))
    return pl.pallas_call(body, grid=(width // (cpb * HEAD_DIM),), in_specs=[spec] * 3, out_specs=[spec] * 2,
                          out_shape=[SDS((l, width), F32)] * 2,
                          compiler_params=_params('parallel'), name=name)(q, k, v)


def _attn_bwd(q, k, v, do, lse, delta, name):
    l, width = q.shape
    cpb = _attn_cols(l, width)
    nb = l // ATTN_BLOCK
    scale = HEAD_DIM ** -0.5

    def body(q_ref, k_ref, v_ref, do_ref, lse_ref, dl_ref, dq_ref, dk_ref, dv_ref):
        m_prev, m_cur = _attn_masks()
        dk_ref[...] = jnp.zeros_like(dk_ref)
        dv_ref[...] = jnp.zeros_like(dv_ref)
        for col in range(cpb):
            cs = slice(col * HEAD_DIM, (col + 1) * HEAD_DIM)

            def step(b, carry, cs=cs):
                r0 = pl.multiple_of(b * ATTN_BLOCK, ATTN_BLOCK)
                rp = pl.multiple_of(jnp.maximum(b - 1, 0) * ATTN_BLOCK, ATTN_BLOCK)
                qb, dob = q_ref[pl.ds(r0, ATTN_BLOCK), cs], do_ref[pl.ds(r0, ATTN_BLOCK), cs]
                kp, kc = k_ref[pl.ds(rp, ATTN_BLOCK), cs], k_ref[pl.ds(r0, ATTN_BLOCK), cs]
                vp, vc = v_ref[pl.ds(rp, ATTN_BLOCK), cs], v_ref[pl.ds(r0, ATTN_BLOCK), cs]
                lse_b = lse_ref[pl.ds(r0, ATTN_BLOCK), cs]
                dl_b = dl_ref[pl.ds(r0, ATTN_BLOCK), cs]
                s_p = lax.dot_general(qb, kp, _DIMS['nt'], preferred_element_type=F32) * scale
                s_c = lax.dot_general(qb, kc, _DIMS['nt'], preferred_element_type=F32) * scale
                p_p = jnp.exp(jnp.where(jnp.logical_and(m_prev, b > 0), s_p, NEG) - lse_b)
                p_c = jnp.exp(jnp.where(m_cur, s_c, NEG) - lse_b)
                dp_p = lax.dot_general(dob, vp, _DIMS['nt'], preferred_element_type=F32)
                dp_c = lax.dot_general(dob, vc, _DIMS['nt'], preferred_element_type=F32)
                ds_p = (p_p * (dp_p - dl_b) * scale).astype(BF16)
                ds_c = (p_c * (dp_c - dl_b) * scale).astype(BF16)
                dq_ref[pl.ds(r0, ATTN_BLOCK), cs] = (jnp.dot(ds_p, kp, preferred_element_type=F32)
                                                     + jnp.dot(ds_c, kc, preferred_element_type=F32))
                dk_ref[pl.ds(rp, ATTN_BLOCK), cs] += lax.dot_general(ds_p, qb, _DIMS['tn'], preferred_element_type=F32)
                dk_ref[pl.ds(r0, ATTN_BLOCK), cs] += lax.dot_general(ds_c, qb, _DIMS['tn'], preferred_element_type=F32)
                dv_ref[pl.ds(rp, ATTN_BLOCK), cs] += lax.dot_general(p_p.astype(BF16), dob, _DIMS['tn'], preferred_element_type=F32)
                dv_ref[pl.ds(r0, ATTN_BLOCK), cs] += lax.dot_general(p_c.astype(BF16), dob, _DIMS['tn'], preferred_element_type=F32)
                return carry

            lax.fori_loop(0, nb, step, 0)

    spec = pl.BlockSpec((l, cpb * HEAD_DIM), lambda j: (0, j))
    return pl.pallas_call(body, grid=(width // (cpb * HEAD_DIM),), in_specs=[spec] * 6, out_specs=[spec] * 3,
                          out_shape=[SDS((l, width), F32)] * 3,
                          compiler_params=_params('parallel'), name=name)(q, k, v, do, lse, delta)


def _attn_combine(os_, lses, name):
    t, d = os_[0].shape
    ng = len(os_)
    tr = _row_tile(t)

    def body(*refs):
        o_refs, l_refs, o_out, lse_out = refs[:ng], refs[ng:2 * ng], refs[2 * ng], refs[2 * ng + 1]
        ls = [r[...] for r in l_refs]
        m = functools.reduce(jnp.maximum, ls)
        ws = [jnp.exp(x - m) for x in ls]
        den = functools.reduce(lambda a, b: a + b, ws)
        acc = functools.reduce(lambda a, b: a + b, [w * o[...] for w, o in zip(ws, o_refs)])
        o_out[...] = (acc / den).astype(o_out.dtype)
        lse_out[...] = m + jnp.log(den)

    return pl.pallas_call(body, grid=(t // tr,), in_specs=[_row(tr, d)] * (2 * ng), out_specs=[_row(tr, d)] * 2,
                          out_shape=[SDS((t, d), BF16), SDS((t, d), F32)],
                          compiler_params=_params('parallel'), name=name)(*os_, *lses)


def _delta_epilogue(acc, o):
    prod = acc * o.astype(F32)
    segs = [jnp.broadcast_to(jnp.sum(prod[:, s:s + HEAD_DIM], axis=-1, keepdims=True), (acc.shape[0], HEAD_DIM))
            for s in range(0, acc.shape[1], HEAD_DIM)]
    return acc, jnp.concatenate(segs, axis=-1)


LRU_TILE = 128


def _lru_gates(xr, wa_ref, ba, wx_ref, bx, lam):
    nb = wa_ref.shape[0]
    xb = xr.astype(BF16)
    ra = jnp.concatenate([jnp.dot(xb[:, n * LRU_BLOCK:(n + 1) * LRU_BLOCK], wa_ref[n], preferred_element_type=F32)
                          for n in range(nb)], axis=-1) + ba
    ia = jnp.concatenate([jnp.dot(xb[:, n * LRU_BLOCK:(n + 1) * LRU_BLOCK], wx_ref[n], preferred_element_type=F32)
                          for n in range(nb)], axis=-1) + bx
    r, ig = _sigmoid(ra), _sigmoid(ia)
    sp = _softplus(-lam)
    log_a = -LRU_C * r * sp
    a = jnp.exp(log_a)
    mult = jnp.sqrt(-_expm1(2.0 * log_a))
    return xb, r, ig, sp, a, mult


def _lru_fwd(z, cw, cb, wa, ba, wx, bx, lam, name):
    t, c2 = z.shape
    c = c2 // 2
    nb = c // LRU_BLOCK
    tr = _row_tile(t, LRU_TILE)

    def body(g_ref, x_ref, xp_ref, cw_ref, cb_ref, wa_ref, ba_ref, wx_ref, bx_ref, lam_ref,
             y_ref, hs_ref, xr_ref, car_ref):
        i = pl.program_id(0)

        @pl.when(i == 0)
        def _():
            car_ref[...] = jnp.zeros_like(car_ref)

        x0 = x_ref[...]
        xp = jnp.where(i > 0, xp_ref[SUB:HALO, :], 0.0)
        cwv = cw_ref[...]
        xr = (cb_ref[...] + cwv[3:4] * x0 + cwv[2:3] * _shift_down(x0, 1, xp)
              + cwv[1:2] * _shift_down(x0, 2, xp) + cwv[0:1] * _shift_down(x0, 3, xp))
        xr_ref[...] = xr
        _, _, ig, _, a, mult = _lru_gates(xr, wa_ref, ba_ref[...], wx_ref, bx_ref[...], lam_ref[...])
        u = mult * (ig * xr)
        row = lax.broadcasted_iota(jnp.int32, (SUB, c), 0)
        car = car_ref[...]
        for j in range(tr // SUB):
            ab, ub = a[j * SUB:(j + 1) * SUB], u[j * SUB:(j + 1) * SUB]
            for s in (1, 2, 4):
                a_sh = jnp.where(row >= s, pltpu.roll(ab, s, 0), 1.0)
                u_sh = jnp.where(row >= s, pltpu.roll(ub, s, 0), 0.0)
                ub = ab * u_sh + ub
                ab = ab * a_sh
            hb = ub + ab * car
            hs_ref[j * SUB:(j + 1) * SUB, :] = hb
            car = jnp.broadcast_to(hb[SUB - 1:SUB], (SUB, c))
        car_ref[...] = car
        gl, _ = _gelu_and_grad(g_ref[...])
        y_ref[...] = (hs_ref[...] * gl).astype(y_ref.dtype)

    return pl.pallas_call(
        body, grid=(t // tr,),
        in_specs=[_row(tr, c, 0), _row(tr, c, 1), _halo_prev(tr, c, 1), _full((4, c)), _full((1, c)),
                  _full((nb, LRU_BLOCK, LRU_BLOCK)), _full((1, c)), _full((nb, LRU_BLOCK, LRU_BLOCK)), _full((1, c)), _full((1, c))],
        out_specs=[_row(tr, c)] * 3,
        out_shape=[SDS((t, c), BF16), SDS((t, c), F32), SDS((t, c), F32)],
        scratch_shapes=[pltpu.VMEM((SUB, c), F32)],
        compiler_params=_params('arbitrary'), name=name)(
            z, z, z, cw, cb.reshape(1, c), wa, ba.reshape(1, c), wx, bx.reshape(1, c), lam.reshape(1, c))


def _lru_bwd(dy, z, xr, hs, cw, wa, ba, wx, bx, lam, name):
    t, c2 = z.shape
    c = c2 // 2
    nb = c // LRU_BLOCK
    tr = _row_tile(t, LRU_TILE)
    nt = t // tr

    def rev(col=0):
        return pl.BlockSpec((tr, c), lambda i, col=col: (nt - 1 - i, col))

    def rev_prev(col=0):
        return pl.BlockSpec((HALO, c), lambda i, col=col: (jnp.maximum((nt - 1 - i) * (tr // HALO) - 1, 0), col))

    def body(dy_ref, g_ref, x_ref, xp_ref, xr_ref, hs_ref, hp_ref, cw_ref, wa_ref, ba_ref, wx_ref, bx_ref, lam_ref,
             dz_ref, dwa_ref, dwx_ref, dvec_ref, lcar_ref, ahead_ref, dxhead_ref, lam_s):
        i = pl.program_id(0)
        first_tile = i == nt - 1

        @pl.when(i == 0)
        def _():
            lcar_ref[...] = jnp.zeros_like(lcar_ref)
            ahead_ref[...] = jnp.zeros_like(ahead_ref)
            dxhead_ref[...] = jnp.zeros_like(dxhead_ref)
            dwa_ref[...] = jnp.zeros_like(dwa_ref)
            dwx_ref[...] = jnp.zeros_like(dwx_ref)
            dvec_ref[...] = jnp.zeros_like(dvec_ref)

        xrv = xr_ref[...]
        lamv = lam_ref[...]
        xb, r, ig, sp, a, mult = _lru_gates(xrv, wa_ref, ba_ref[...], wx_ref, bx_ref[...], lamv)
        hsv = hs_ref[...]
        dyv = dy_ref[...]
        gl, dgl = _gelu_and_grad(g_ref[...])
        dhs = dyv * gl
        dz_ref[:, :c] = (dyv * hsv * dgl).astype(dz_ref.dtype)

        a_next = _shift_up(a, 1, ahead_ref[...])
        row = lax.broadcasted_iota(jnp.int32, (SUB, c), 0)
        car = lcar_ref[...]
        for j in reversed(range(tr // SUB)):
            ab, ub = a_next[j * SUB:(j + 1) * SUB], dhs[j * SUB:(j + 1) * SUB]
            for s in (1, 2, 4):
                a_sh = jnp.where(row < SUB - s, pltpu.roll(ab, SUB - s, 0), 1.0)
                u_sh = jnp.where(row < SUB - s, pltpu.roll(ub, SUB - s, 0), 0.0)
                ub = ab * u_sh + ub
                ab = ab * a_sh
            lb = ub + ab * car
            lam_s[j * SUB:(j + 1) * SUB, :] = lb
            car = jnp.broadcast_to(lb[0:1], (SUB, c))
        lcar_ref[...] = car
        ahead_ref[...] = a[0:SUB]
        lmb = lam_s[...]

        hp = jnp.where(first_tile, 0.0, hp_ref[SUB:HALO, :])
        h_prev = _shift_down(hsv, 1, hp)
        d_a = lmb * h_prev
        d_mult = lmb * (ig * xrv)
        d_ixr = lmb * mult
        d_ig = d_ixr * xrv
        dxr = d_ixr * ig
        d_la = d_a * a - d_mult * (a * a) / mult
        d_r = d_la * (-LRU_C * sp)
        d_sp = jnp.sum(d_la * (-LRU_C * r), axis=0, keepdims=True)
        d_ra = d_r * r * (1.0 - r)
        d_ia = d_ig * ig * (1.0 - ig)
        d_rab, d_iab = d_ra.astype(BF16), d_ia.astype(BF16)
        parts = []
        for n in range(nb):
            cs = slice(n * LRU_BLOCK, (n + 1) * LRU_BLOCK)
            parts.append(lax.dot_general(d_rab[:, cs], wa_ref[n], _DIMS['nt'], preferred_element_type=F32)
                         + lax.dot_general(d_iab[:, cs], wx_ref[n], _DIMS['nt'], preferred_element_type=F32))
            dwa_ref[n] += lax.dot_general(xb[:, cs], d_rab[:, cs], _DIMS['tn'], preferred_element_type=F32)
            dwx_ref[n] += lax.dot_general(xb[:, cs], d_iab[:, cs], _DIMS['tn'], preferred_element_type=F32)
        dxr = dxr + jnp.concatenate(parts, axis=-1)

        cwv = cw_ref[...]
        nxt = dxhead_ref[...]
        dx0 = (cwv[3:4] * dxr + cwv[2:3] * _shift_up(dxr, 1, nxt) + cwv[1:2] * _shift_up(dxr, 2, nxt)
               + cwv[0:1] * _shift_up(dxr, 3, nxt))
        dxhead_ref[...] = dxr[0:SUB]
        dz_ref[:, c:] = dx0.astype(dz_ref.dtype)

        x0 = x_ref[...]
        xp = jnp.where(first_tile, 0.0, xp_ref[SUB:HALO, :])
        sums = [jnp.sum(d_ra, axis=0, keepdims=True), jnp.sum(d_ia, axis=0, keepdims=True),
                d_sp * (-_sigmoid(-lamv)), jnp.sum(dxr, axis=0, keepdims=True),
                jnp.sum(dxr * _shift_down(x0, 3, xp), axis=0, keepdims=True),
                jnp.sum(dxr * _shift_down(x0, 2, xp), axis=0, keepdims=True),
                jnp.sum(dxr * _shift_down(x0, 1, xp), axis=0, keepdims=True),
                jnp.sum(dxr * x0, axis=0, keepdims=True)]
        dvec_ref[...] += jnp.concatenate(sums, axis=0)

    wspec = _full((nb, LRU_BLOCK, LRU_BLOCK))
    return pl.pallas_call(
        body, grid=(nt,),
        in_specs=[rev(), rev(0), rev(1), rev_prev(1), rev(), rev(), rev_prev(), _full((4, c)),
                  wspec, _full((1, c)), wspec, _full((1, c)), _full((1, c))],
        out_specs=[pl.BlockSpec((tr, c2), lambda i: (nt - 1 - i, 0)), wspec, wspec, _full((SUB, c))],
        out_shape=[SDS((t, c2), BF16), SDS((nb, LRU_BLOCK, LRU_BLOCK), F32), SDS((nb, LRU_BLOCK, LRU_BLOCK), F32),
                   SDS((SUB, c), F32)],
        scratch_shapes=[pltpu.VMEM((SUB, c), F32), pltpu.VMEM((SUB, c), F32), pltpu.VMEM((SUB, c), F32),
                        pltpu.VMEM((tr, c), F32)],
        compiler_params=_params('arbitrary'), name=name)(
            dy, z, z, z, xr, hs, hs, cw, wa, ba.reshape(1, c), wx, bx.reshape(1, c), lam.reshape(1, c))


def _dilate(x, d):
    t, w = x.shape
    return x.reshape(t // d, d * w)


def _local_step(x, p, pos, target, rep, weights_for_layer, emit_grads):
    t, d = x.shape
    depth = p.shape[0]
    w = rep
    half = ROPE_DIM // 2
    invf = ROPE_THETA ** (-2.0 * jnp.arange(half, dtype=F32) / ROPE_DIM)
    invf = jnp.concatenate([invf, invf, jnp.zeros((HEAD_DIM - ROPE_DIM,), F32)]).reshape(1, HEAD_DIM)
    ng = len(DILATED_PATTERNS)
    saved = []
    h = x
    for i in range(depth):
        kind, j = i % N_MIXERS, i // N_MIXERS
        wl = weights_for_layer(i)
        s = {'h0': h, 'wl': wl}
        hn = _rms_fwd(h, w['norm_mix'][i], f'rms_mix_fwd_{i}')
        s['hn'] = hn
        if kind == 0:
            z = _mm(hn, wl['w_in'], 'nn', f'sc_in_{i}')
            y = _sc_fwd(z, wl['small'], f'sc_conv_fwd_{i}')
            h1 = _mm(y, wl['w_out'], 'nn', f'sc_out_{i}', extras=(h,), epi=lambda acc, res: (acc + res,))
            s.update(z=z, y=y)
        elif kind == 1:
            qkv = _mm(hn, wl['w_in'], 'nn', f'attn_qkv_{i}', tn=1152)
            q, k, v = _rope_fwd(qkv, pos, invf, f'rope_fwd_{i}')
            os_, lses, views = [], [], []
            for g, (_, dil) in enumerate(DILATED_PATTERNS):
                qg, kg, vg = (_dilate(a[:, g * d:(g + 1) * d], dil) for a in (q, k, v))
                og, lg = _attn_fwd(qg, kg, vg, f'attn_fwd_{i}_g{g}')
                os_.append(og.reshape(t, d))
                lses.append(lg.reshape(t, d))
                views.append((qg, kg, vg))
            o, lse = _attn_combine(os_, lses, f'attn_combine_{i}')
            h1 = _mm(o, wl['w_out'], 'nn', f'attn_out_{i}', extras=(h,), epi=lambda acc, res: (acc + res,))
            s.update(views=views, o=o, lse=lse)
        else:
            z = _mm(hn, wl['w_in'], 'nn', f'lru_in_{i}', tn=1280)
            sm = wl['small']
            y, hs, xr = _lru_fwd(z, sm[0:4], sm[4:5], w['lru_w_a'][j], sm[5:6], w['lru_w_x'][j], sm[6:7], sm[7:8],
                                 f'lru_fwd_{i}')
            h1 = _mm(y, wl['w_out'], 'nn', f'lru_out_{i}', extras=(h,), epi=lambda acc, res: (acc + res,), tk=640)
            s.update(z=z, y=y, hs=hs, xr=xr)
        s['h1'] = h1
        hm = _rms_fwd(h1, w['norm_mlp'][i], f'rms_mlp_fwd_{i}')
        u = _mm(hm, wl['mlp_up'], 'nn', f'mlp_up_{i}', out_dtypes=(BF16,))
        h2 = _mm(u, wl['mlp_down'], 'nn', f'mlp_down_{i}', a_pro=_relu2, extras=(h1,), epi=lambda acc, res: (acc + res,))
        hp = _rms_fwd(h2, w['norm_ple'][i], f'rms_ple_fwd_{i}')
        pp = _mm(p[i], wl['ple_proj'], 'nn', f'ple_proj_{i}')
        h3, gate = _mm(hp, wl['ple_gate'], 'nn', f'ple_gate_{i}', out_dtypes=(F32, F32), extras=(pp, h2),
                       epi=lambda acc, ppv, res: (res + _sigmoid(acc) * ppv, _sigmoid(acc)))
        s.update(hm=hm, u=u, h2=h2, hp=hp, pp=pp, gate=gate)
        saved.append(s)
        h = h3

    dh, loss, dg_final = _head(h, w['norm_final'], target, 'loss_head')
    grads = {n: [None] * len(w[n]) for n in w if n != 'norm_final'}
    grads['norm_final'] = dg_final.reshape(d)
    for i in reversed(range(depth)):
        kind, j = i % N_MIXERS, i // N_MIXERS
        s = saved[i]
        wl, gl = s['wl'], {}
        dpp, dgl = _ple_bwd_gate(dh, s['gate'], s['pp'], f'ple_bwd_gate_{i}')
        gl['ple_proj'] = _mm(p[i], dpp, 'tn', f'ple_dproj_{i}', out_dtypes=(BF16,))
        gl['ple_gate'] = _mm(s['hp'], dgl, 'tn', f'ple_dgate_{i}', out_dtypes=(BF16,))
        dhp = _mm(dgl, wl['ple_gate'], 'nt', f'ple_dhp_{i}')
        dh, dg = _rms_bwd(s['h2'], w['norm_ple'][i], dhp, dh, f'rms_ple_bwd_{i}')
        grads['norm_ple'][i] = dg.reshape(d)
        du = _mm(dh, wl['mlp_down'], 'nt', f'mlp_du_{i}', out_dtypes=(BF16,), extras=(s['u'],),
                 epi=lambda acc, uv: (acc * 2.0 * jnp.maximum(uv.astype(F32), 0.0),))
        gl['mlp_down'] = _mm(s['u'], dh, 'tn', f'mlp_ddown_{i}', out_dtypes=(BF16,), a_pro=_relu2)
        gl['mlp_up'] = _mm(s['hm'], du, 'tn', f'mlp_dup_{i}', out_dtypes=(BF16,), out_stacked=True)
        dhm = _mm(du, wl['mlp_up'], 'nt', f'mlp_dhm_{i}')
        dh, dg = _rms_bwd(s['h1'], w['norm_mlp'][i], dhm, dh, f'rms_mlp_bwd_{i}')
        grads['norm_mlp'][i] = dg.reshape(d)
        if kind == 0:
            dy = _mm(dh, wl['w_out'], 'nt', f'sc_dy_{i}')
            gl['w_out'] = _mm(s['y'], dh, 'tn', f'sc_dout_{i}', out_dtypes=(BF16,))
            dz, dwc = _sc_bwd(dy, s['z'], wl['small'], f'sc_conv_bwd_{i}')
            gl['small'] = dwc
            gl['w_in'] = _mm(s['hn'], dz, 'tn', f'sc_din_{i}', out_dtypes=(BF16,), out_stacked=True)
            dhn = _mm(dz, wl['w_in'], 'nt', f'sc_dhn_{i}')
        elif kind == 1:
            do, delta = _mm(dh, wl['w_out'], 'nt', f'attn_do_{i}', out_dtypes=(BF16, F32), extras=(s['o'],),
                            epi=_delta_epilogue, tn=d)
            gl['w_out'] = _mm(s['o'], dh, 'tn', f'attn_dwo_{i}', out_dtypes=(BF16,))
            dqs, dks, dvs = [], [], []
            for g, (_, dil) in enumerate(DILATED_PATTERNS):
                qg, kg, vg = s['views'][g]
                dqg, dkg, dvg = _attn_bwd(qg, kg, vg, _dilate(do, dil), _dilate(s['lse'], dil), _dilate(delta, dil),
                                          f'attn_bwd_{i}_g{g}')
                dqs.append(dqg.reshape(t, d))
                dks.append(dkg.reshape(t, d))
                dvs.append(dvg.reshape(t, d))
            dqkv = _rope_bwd(dqs, dks, dvs, pos, invf, f'rope_bwd_{i}')
            gl['w_in'] = _mm(s['hn'], dqkv, 'tn', f'attn_dqkv_{i}', out_dtypes=(BF16,), out_stacked=True, tn=1152)
            dhn = _mm(dqkv, wl['w_in'], 'nt', f'attn_dhn_{i}', tk=1152)
        else:
            dy = _mm(dh, wl['w_out'], 'nt', f'lru_dy_{i}', tn=1280)
            gl['w_out'] = _mm(s['y'], dh, 'tn', f'lru_dout_{i}', out_dtypes=(BF16,), tm=1280)
            sm = wl['small']
            dz, dwa, dwx, dvec = _lru_bwd(dy, s['z'], s['xr'], s['hs'], sm[0:4], w['lru_w_a'][j], sm[5:6],
                                          w['lru_w_x'][j], sm[6:7], sm[7:8], f'lru_bwd_{i}')
            grads['lru_w_a'][j], grads['lru_w_x'][j] = dwa, dwx
            gl['small'] = dvec
            gl['w_in'] = _mm(s['hn'], dz, 'tn', f'lru_din_{i}', out_dtypes=(BF16,), tn=1280)
            dhn = _mm(dz, wl['w_in'], 'nt', f'lru_dhn_{i}', tk=640)
        dh, dg = _rms_bwd(s['h0'], w['norm_mix'][i], dhn, dh, f'rms_mix_bwd_{i}')
        grads['norm_mix'][i] = dg.reshape(d)
        emit_grads(i, gl)
    return loss, dh, grads


_MESH = pl.DeviceIdType.MESH
_ANY = pl.BlockSpec(memory_space=pl.ANY)


def _block_view(ref, kind, idx):
    if kind == 'stack':
        return ref.at[idx]
    r = ref.shape[0] // N_DEV
    return ref.at[pl.ds(idx * r, r)]


def _gather_many(arrs, kinds, name):
    n = len(arrs)
    out_shapes = [SDS((N_DEV,) + a.shape if kd == 'stack' else (N_DEV * a.shape[0],) + a.shape[1:], a.dtype)
                  for a, kd in zip(arrs, kinds)]

    def body(*refs):
        x_refs, out_refs = refs[:n], refs[n:2 * n]
        send_sems, recv_sems, local_sems = refs[2 * n:]
        x, y, c = lax.axis_index('x'), lax.axis_index('y'), lax.axis_index('c')
        me, sibling = (x, y, c), (x, y, 1 - c)
        chips = [(1 - x, y), (x, 1 - y), (1 - x, 1 - y)]

        def slab(t, px, py, pc):
            return _block_view(out_refs[t], kinds[t], 4 * px + 2 * py + pc)

        def copy(t, k, block, to, src=None):
            return pltpu.make_async_remote_copy(
                src_ref=slab(t, *block) if src is None else src, dst_ref=slab(t, *block),
                send_sem=send_sems.at[7 * t + k], recv_sem=recv_sems.at[7 * t + k], device_id=to, device_id_type=_MESH)

        mine = [pltpu.make_async_copy(x_refs[t], slab(t, *me), local_sems.at[t]) for t in range(n)]
        for cp in mine:
            cp.start()
        first = [copy(t, 0, me, sibling, src=x_refs[t]) for t in range(n)]
        first += [copy(t, 1 + j, me, (*chip, c), src=x_refs[t]) for j, chip in enumerate(chips) for t in range(n)]
        for cp in first:
            cp.start()
        passed = []
        for j, chip in enumerate(chips):
            for t in range(n):
                copy(t, 1 + j, (*chip, c), me).wait_recv()
                passed.append(copy(t, 4 + j, (*chip, c), sibling))
                passed[-1].start()
        for t in range(n):
            copy(t, 0, sibling, me).wait_recv()
            for j, chip in enumerate(chips):
                copy(t, 4 + j, (*chip, 1 - c), me).wait_recv()
        for cp in first + passed:
            cp.wait_send()
        for cp in mine:
            cp.wait()

    return pl.pallas_call(
        body, out_shape=out_shapes, in_specs=[_ANY] * n, out_specs=[_ANY] * n,
        scratch_shapes=[pltpu.SemaphoreType.DMA((7 * n,)), pltpu.SemaphoreType.DMA((7 * n,)), pltpu.SemaphoreType.DMA((n,))],
        name=name)(*arrs)


def _exchange_many(arrs, kinds, name):
    n = len(arrs)
    out_shapes = [SDS(a.shape if kd == 'stack' else (N_DEV, a.shape[0] // N_DEV) + a.shape[1:], a.dtype)
                  for a, kd in zip(arrs, kinds)]

    def body(*refs):
        in_refs, out_refs = refs[:n], refs[n:2 * n]
        send_sems, recv_sems, local_sems = refs[2 * n:]
        x, y, c = lax.axis_index('x'), lax.axis_index('y'), lax.axis_index('c')
        my_idx = 4 * x + 2 * y + c
        mine = [pltpu.make_async_copy(_block_view(in_refs[t], kinds[t], my_idx), out_refs[t].at[my_idx], local_sems.at[t])
                for t in range(n)]
        for cp in mine:
            cp.start()
        copies = []
        for k in range(1, N_DEV):
            px, py, pc = (1 - x if k & 4 else x, 1 - y if k & 2 else y, 1 - c if k & 1 else c)
            for t in range(n):
                copies.append(pltpu.make_async_remote_copy(
                    src_ref=_block_view(in_refs[t], kinds[t], 4 * px + 2 * py + pc), dst_ref=out_refs[t].at[my_idx],
                    send_sem=send_sems.at[7 * t + k - 1], recv_sem=recv_sems.at[7 * t + k - 1],
                    device_id=(px, py, pc), device_id_type=_MESH))
        for cp in copies:
            cp.start()
        for cp in copies:
            cp.wait_recv()
        for cp in copies:
            cp.wait_send()
        for cp in mine:
            cp.wait()

    return pl.pallas_call(
        body, out_shape=out_shapes, in_specs=[_ANY] * n, out_specs=[_ANY] * n,
        scratch_shapes=[pltpu.SemaphoreType.DMA((7 * n,)), pltpu.SemaphoreType.DMA((7 * n,)), pltpu.SemaphoreType.DMA((n,))],
        name=name)(*arrs)


ADAMW_BLOCK_ELEMS = 128 * 1024


def _adamw_sum(wgt, parts, m, v, name):
    nl, r, c = wgt.shape
    assert len(parts) == nl and all(q.shape == (N_DEV, r, c) for q in parts), (name, wgt.shape, [q.shape for q in parts])
    tr = next((t for t in range(min(r, 512), 0, -16) if r % t == 0 and t * c <= ADAMW_BLOCK_ELEMS and t % 16 == 0), r)
    c1 = 1.0 - ADAM_B1 ** ADAM_STEP
    c2 = 1.0 - ADAM_B2 ** ADAM_STEP

    def body(w_ref, m_ref, v_ref, *rest):
        part_refs, (g_ref, d_ref, mo_ref, vo_ref) = rest[:nl], rest[nl:]
        for q in range(nl):
            @pl.when(pl.program_id(0) == q)
            def _(q=q):
                gv = part_refs[q][0].astype(F32)
                for s in range(1, N_DEV):
                    gv = gv + part_refs[q][s].astype(F32)
                mn = ADAM_B1 * m_ref[...] + (1.0 - ADAM_B1) * gv
                vn = ADAM_B2 * v_ref[...] + (1.0 - ADAM_B2) * (gv * gv)
                g_ref[...] = gv
                d_ref[...] = -ADAM_LR * ((mn / c1) / (jnp.sqrt(vn / c2) + ADAM_EPS) + ADAM_WD * w_ref[...])
                mo_ref[...] = mn
                vo_ref[...] = vn

    spec = pl.BlockSpec((None, tr, c), lambda l, i: (l, i, 0))
    part_specs = [pl.BlockSpec((N_DEV, tr, c), lambda l, i, q=q: (0, jnp.where(l == q, i, 0), 0)) for q in range(nl)]
    return pl.pallas_call(body, grid=(nl, r // tr), in_specs=[spec] * 3 + part_specs, out_specs=[spec] * 4,
                          out_shape=[SDS((nl, r, c), F32)] * 4, compiler_params=_params('arbitrary', 'arbitrary'),
                          name=name)(wgt, m, v, *parts)


MIXER_WEIGHTS = {0: ('sc_w_in', 'sc_w_out'), 1: ('attn_w_qkv', 'attn_w_o'), 2: ('lru_w_in', 'lru_w_out')}
STACKED_OPERANDS = ('sc_w_in', 'attn_w_qkv', 'mlp_w_up')
LRU_SMALL = ('lru_conv_w', 'lru_conv_b', 'lru_b_a', 'lru_b_x', 'lru_lambda')


def _layer_items(i):
    w_in, w_out = MIXER_WEIGHTS[i % N_MIXERS]
    j = i // N_MIXERS
    return [('w_in', w_in, j), ('w_out', w_out, j), ('mlp_up', 'mlp_w_up', i), ('mlp_down', 'mlp_w_down', i),
            ('ple_gate', 'ple_w_gate', i), ('ple_proj', 'ple_w_proj', i)]


def _cols_to_full(stacked):
    return jnp.moveaxis(stacked, 0, 1).reshape(stacked.shape[1], -1)


def _full_to_cols(full):
    k, n = full.shape
    return jnp.moveaxis(full.reshape(k, N_DEV, n // N_DEV), 1, 0)


def _pad_to(a, rows):
    return jnp.pad(a, ((0, rows - a.shape[0]), (0, 0)))


def _small_block(src, i):
    kind, j = i % N_MIXERS, i // N_MIXERS
    if kind == 0:
        return _pad_to(src['sc_w_conv'][j], SUB)
    if kind == 2:
        return jnp.concatenate([src[n][j].reshape(-1, src[n].shape[-1]) for n in LRU_SMALL], axis=0)
    return None


def kernel(x, p, positions, norm_mix, norm_mlp, norm_ple, norm_final, sc_w_in, sc_w_conv, sc_w_out, attn_w_qkv, attn_w_o, lru_w_in, lru_conv_w, lru_conv_b, lru_w_a, lru_b_a, lru_w_x, lru_b_x, lru_lambda, lru_w_out, mlp_w_up, mlp_w_down, ple_w_gate, ple_w_proj, loss_target, m_norm_mix, m_norm_mlp, m_norm_ple, m_norm_final, m_sc_w_in, m_sc_w_conv, m_sc_w_out, m_attn_w_qkv, m_attn_w_o, m_lru_w_in, m_lru_conv_w, m_lru_conv_b, m_lru_w_a, m_lru_b_a, m_lru_w_x, m_lru_b_x, m_lru_lambda, m_lru_w_out, m_mlp_w_up, m_mlp_w_down, m_ple_w_gate, m_ple_w_proj, v_norm_mix, v_norm_mlp, v_norm_ple, v_norm_final, v_sc_w_in, v_sc_w_conv, v_sc_w_out, v_attn_w_qkv, v_attn_w_o, v_lru_w_in, v_lru_conv_w, v_lru_conv_b, v_lru_w_a, v_lru_b_a, v_lru_w_x, v_lru_b_x, v_lru_lambda, v_lru_w_out, v_mlp_w_up, v_mlp_w_down, v_ple_w_gate, v_ple_w_proj):
    loc = dict(locals())
    shards = {n: loc[n] for n in WEIGHTS}
    moms = {n: loc['m_' + n] for n in WEIGHTS}
    vels = {n: loc['v_' + n] for n in WEIGHTS}

    depth, t, d = p.shape[0], x.shape[1], x.shape[2]

    def comm_kind(name):
        return 'stack' if SHARD_AXIS[name] == 2 else 'rows'

    def gathered_layer(i):
        items = _layer_items(i)
        arrs = [shards[n][idx].astype(BF16) for _, n, idx in items]
        kinds = [comm_kind(n) for _, n, _ in items]
        small = _small_block(shards, i)
        if small is not None:
            arrs.append(small)
            kinds.append('stack')
        outs = _gather_many(arrs, kinds, f'gather_weights_{i}')
        wl = {key: (_cols_to_full(o) if kd == 'stack' and n not in STACKED_OPERANDS else o)
              for (key, n, _), kd, o in zip(items, kinds, outs)}
        if small is not None:
            wl['small'] = _cols_to_full(outs[-1])[:shards['sc_w_conv'].shape[1] if i % N_MIXERS == 0 else SUB]
        return wl

    weights = [gathered_layer(i) for i in range(depth)]

    received = {}

    def emit_grads(i, gl):
        items = _layer_items(i)
        kinds = [comm_kind(n) for _, n, _ in items]
        arrs = [_full_to_cols(gl[key]) if kd == 'stack' and gl[key].ndim == 2 else gl[key]
                for (key, _, _), kd in zip(items, kinds)]
        if i % N_MIXERS == 0:
            arrs.append(_full_to_cols(_pad_to(gl['small'], SUB)))
        elif i % N_MIXERS == 2:
            dv = gl['small']
            arrs.append(_full_to_cols(jnp.concatenate([dv[4:8], dv[3:4], dv[0:1], dv[1:2], dv[2:3]], axis=0)))
        kinds += ['stack'] * (len(arrs) - len(kinds))
        outs = _exchange_many(arrs, kinds, f'exchange_grads_{i}')
        for (_, n, idx), o in zip(items, outs):
            received[(n, idx)] = o
        if len(outs) > len(items):
            received[('small', i)] = outs[-1]

    rep = {n: shards[n] for n in ('norm_mix', 'norm_mlp', 'norm_ple', 'norm_final')}
    rep['lru_w_a'], rep['lru_w_x'] = shards['lru_w_a'].astype(BF16), shards['lru_w_x'].astype(BF16)
    loss, grad_x, rgrads = _local_step(x.reshape(t, d), p.reshape(depth, t, p.shape[3]), positions.reshape(t, 1),
                                       loss_target.reshape(t, d), rep, lambda i: weights[i], emit_grads)

    norm_names = ('norm_mix', 'norm_mlp', 'norm_ple', 'norm_final')
    gate_names = ('lru_w_a', 'lru_w_x')

    def norm_block(src):
        cat = jnp.concatenate([src[n].reshape(-1, d) for n in norm_names], axis=0)
        return _pad_to(cat, -(-cat.shape[0] // HALO) * HALO)

    def gate_block(src):
        return jnp.concatenate([src[n].reshape(-1, LRU_BLOCK) for n in gate_names], axis=0)

    rfull = {n: (rgrads[n] if n == 'norm_final' else jnp.stack(rgrads[n], axis=0)) for n in norm_names + gate_names}
    parts_norm, parts_gate = _gather_many([norm_block(rfull), gate_block(rfull)], ['stack', 'stack'],
                                          'gather_replicated_grads')

    res = {}
    for n in WEIGHTS:
        if SHARD_AXIS[n] is not None and shards[n].ndim == 3 and n not in ('sc_w_conv', 'lru_conv_w'):
            res[n] = _adamw_sum(shards[n], [received[(n, l)] for l in range(shards[n].shape[0])], moms[n], vels[n],
                                f'adamw_{n}')
    def small_adamw(layers, name):
        w_, m_, v_ = (jnp.stack([_small_block(src, i) for i in layers]) for src in (shards, moms, vels))
        return _adamw_sum(w_, [received[('small', i)] for i in layers], m_, v_, name)

    sc = small_adamw([i for i in range(depth) if i % N_MIXERS == 0], 'adamw_sc_w_conv')
    res['sc_w_conv'] = tuple(o[:, :shards['sc_w_conv'].shape[1]] for o in sc)
    lru = small_adamw([i for i in range(depth) if i % N_MIXERS == 2], 'adamw_lru_small')
    row = 0
    for n in LRU_SMALL:
        k = shards[n].size // shards[n].shape[0] // shards[n].shape[-1]
        res[n] = tuple(o[:, row:row + k].reshape(shards[n].shape) for o in lru)
        row += k
    norms = _adamw_sum(norm_block(shards)[None], [parts_norm], norm_block(moms)[None], norm_block(vels)[None],
                       'adamw_norms')
    gates = _adamw_sum(gate_block(shards)[None], [parts_gate], gate_block(moms)[None], gate_block(vels)[None],
                       'adamw_lru_gates')
    for names, outs in ((norm_names, norms), (gate_names, gates)):
        row = 0
        for n in names:
            k = shards[n].size // outs[0].shape[-1]
            res[n] = tuple(o[0, row:row + k].reshape(shards[n].shape) for o in outs)
            row += k

    loss = lax.psum(loss[0, 0], ('x', 'y', 'c'))
    return (loss, grad_x.reshape(x.shape), *[res[n][0] for n in WEIGHTS], *[res[n][1] for n in WEIGHTS],
            *[res[n][2] for n in WEIGHTS], *[res[n][3] for n in WEIGHTS])
```

```python
import functools
import math

import jax
import jax.numpy as jnp
from jax import lax
from jax.experimental import pallas as pl
from jax.experimental.pallas import tpu as pltpu

F32 = jnp.float32
BF16 = jnp.bfloat16
SDS = jax.ShapeDtypeStruct

N_DEV = 8
RMS_EPS = 1e-6
N_MIXERS = 3
HEAD_DIM = 128
DILATED_PATTERNS = ((128, 1), (512, 4), (2048, 16))
ATTN_BLOCK = 128
ROPE_THETA = 500000.0
ROPE_DIM = HEAD_DIM // 4
LRU_BLOCK = 128
LRU_C = 8.0
ADAM_LR, ADAM_B1, ADAM_B2, ADAM_EPS, ADAM_WD, ADAM_STEP = 0.001, 0.9, 0.999, 1e-08, 0.01, 10

HALO = 16
SUB = 8
VMEM_LIMIT = 56 * 1024 * 1024
NEG = -1e30

SHARD_AXIS = {
    'norm_mix': None, 'norm_mlp': None, 'norm_ple': None, 'norm_final': None,
    'sc_w_in': 2, 'sc_w_conv': 2, 'sc_w_out': 1, 'attn_w_qkv': 2, 'attn_w_o': 1,
    'lru_w_in': 2, 'lru_conv_w': 2, 'lru_conv_b': 1, 'lru_w_a': None, 'lru_b_a': 1,
    'lru_w_x': None, 'lru_b_x': 1, 'lru_lambda': 1, 'lru_w_out': 1,
    'mlp_w_up': 2, 'mlp_w_down': 1, 'ple_w_gate': 1, 'ple_w_proj': 2,
}
WEIGHTS = list(SHARD_AXIS)


def _params(*sem):
    return pltpu.CompilerParams(dimension_semantics=sem or None, vmem_limit_bytes=VMEM_LIMIT)


def _row_tile(t, pref=256):
    tr = min(t, pref)
    assert t % tr == 0 and tr % HALO == 0
    return tr


def _row(tr, c, col=0):
    return pl.BlockSpec((tr, c), lambda i, col=col: (i, col))


def _full(shape):
    return pl.BlockSpec(shape, lambda *_: (0,) * len(shape))


def _sigmoid(x):
    return 1.0 / (1.0 + jnp.exp(-x))


def _expm1(x):
    taylor = x * (1.0 + x * (0.5 + x * (1.0 / 6.0 + x * (1.0 / 24.0 + x * (1.0 / 120.0)))))
    return jnp.where(jnp.abs(x) < 0.1, taylor, jnp.exp(x) - 1.0)


def _softplus(x):
    z = jnp.exp(-jnp.abs(x))
    log1p = jnp.where(z < 0.01, z * (1.0 - z * (0.5 - z * (1.0 / 3.0 - z * 0.25))), jnp.log(1.0 + z))
    return jnp.maximum(x, 0.0) + log1p


_GELU_K = math.sqrt(2.0 / math.pi)


def _gelu_and_grad(x):
    inner = _GELU_K * (x + 0.044715 * x * x * x)
    th = jnp.tanh(inner)
    g = 0.5 * x * (1.0 + th)
    dg = 0.5 * (1.0 + th) + 0.5 * x * (1.0 - th * th) * _GELU_K * (1.0 + 3.0 * 0.044715 * x * x)
    return g, dg


def _shift_down(x, k, prev):
    row = lax.broadcasted_iota(jnp.int32, (SUB, x.shape[1]), 0)
    xr = pltpu.roll(x, k, 0)
    top = jnp.where(row < k, pltpu.roll(prev, k, 0), xr[0:SUB])
    return jnp.concatenate([top, xr[SUB:]], axis=0)


def _shift_up(x, k, nxt):
    r = x.shape[0]
    row = lax.broadcasted_iota(jnp.int32, (SUB, x.shape[1]), 0)
    xr = pltpu.roll(x, r - k, 0)
    bot = jnp.where(row >= SUB - k, pltpu.roll(nxt, SUB - k, 0), xr[r - SUB:r])
    return jnp.concatenate([xr[:r - SUB], bot], axis=0)


_DIMS = {'nn': (((1,), (0,)), ((), ())), 'nt': (((1,), (1,)), ((), ())), 'tn': (((0,), (0,)), ((), ()))}


def _pick_tile(dim, pref):
    if dim <= pref:
        return dim
    return next(c for c in range(pref - pref % 128, 0, -128) if dim % c == 0)


def _mm(a, b, dims, name, out_dtypes=(F32,), a_pro=None, extras=(), epi=None, tm=1024, tn=1024, tk=512,
        out_stacked=False):
    stacked = b.ndim == 3
    b_rows, b_cols = (b.shape[1], N_DEV * b.shape[2]) if stacked else b.shape
    if dims == 'nn':
        (m, k), (k2, n) = a.shape, (b_rows, b_cols)
    elif dims == 'nt':
        (m, k), (n, k2) = a.shape, (b_rows, b_cols)
    else:
        (k, m), (k2, n) = a.shape, (b_rows, b_cols)
    assert k == k2, (name, a.shape, b.shape)
    assert not (stacked and dims == 'tn') and not (out_stacked and (extras or dims != 'tn'))
    tm = _pick_tile(m, tm)
    tn = _pick_tile(n // N_DEV if (out_stacked or (stacked and dims == 'nn')) else n, tn)
    tk = _pick_tile(k // N_DEV if (stacked and dims == 'nt') else k, tk)
    assert m % tm == 0 and n % tn == 0 and k % tk == 0, (name, m, n, k)
    nk = k // tk
    a_spec = pl.BlockSpec((tk, tm), lambda i, j, kk: (kk, i)) if dims == 'tn' else pl.BlockSpec((tm, tk), lambda i, j, kk: (i, kk))
    if not stacked:
        b_spec = pl.BlockSpec((tn, tk), lambda i, j, kk: (j, kk)) if dims == 'nt' else pl.BlockSpec((tk, tn), lambda i, j, kk: (kk, j))
    elif dims == 'nn':
        per = b.shape[2] // tn
        b_spec = pl.BlockSpec((None, tk, tn), lambda i, j, kk: (j // per, kk, j % per))
    else:
        per = b.shape[2] // tk
        b_spec = pl.BlockSpec((None, tn, tk), lambda i, j, kk: (kk // per, j, kk % per))
    if out_stacked:
        per_o = n // N_DEV // tn
        o_spec = pl.BlockSpec((None, tm, tn), lambda i, j, kk: (j // per_o, i, j % per_o))
        o_shape = (N_DEV, m, n // N_DEV)
    else:
        o_spec = pl.BlockSpec((tm, tn), lambda i, j, kk: (i, j))
        o_shape = (m, n)
    n_ex, n_out = len(extras), len(out_dtypes)
    for e in extras:
        assert e.shape == (m, n), (name, e.shape)

    def body(a_ref, b_ref, *rest):
        ex_refs, out_refs, acc = rest[:n_ex], rest[n_ex:n_ex + n_out], rest[-1]
        kk = pl.program_id(2)

        @pl.when(kk == 0)
        def _():
            acc[...] = jnp.zeros_like(acc)

        av = a_ref[...]
        if a_pro is not None:
            av = a_pro(av.astype(F32))
        acc[...] += lax.dot_general(av.astype(BF16), b_ref[...].astype(BF16), _DIMS[dims],
                                    preferred_element_type=F32)

        @pl.when(kk == nk - 1)
        def _():
            res = acc[...]
            outs = (res,) if epi is None else epi(res, *[e[...] for e in ex_refs])
            for o_ref, o in zip(out_refs, outs):
                o_ref[...] = o.astype(o_ref.dtype)

    out = pl.pallas_call(
        body, grid=(m // tm, n // tn, nk),
        in_specs=[a_spec, b_spec] + [o_spec] * n_ex,
        out_specs=[o_spec] * n_out,
        out_shape=[SDS(o_shape, d) for d in out_dtypes],
        scratch_shapes=[pltpu.VMEM((tm, tn), F32)],
        compiler_params=_params('parallel', 'parallel', 'arbitrary'), name=name)(a, b, *extras)
    return out[0] if n_out == 1 else out


def _relu2(u):
    r = jnp.maximum(u, 0.0)
    return r * r


def _rms_fwd(h, g, name):
    t, d = h.shape
    tr = _row_tile(t)

    def body(h_ref, g_ref, o_ref):
        x = h_ref[...]
        r = lax.rsqrt(jnp.mean(x * x, axis=-1, keepdims=True) + RMS_EPS)
        o_ref[...] = (x * r * g_ref[...]).astype(o_ref.dtype)

    return pl.pallas_call(body, grid=(t // tr,), in_specs=[_row(tr, d), _full((1, d))], out_specs=_row(tr, d),
                          out_shape=SDS((t, d), BF16), compiler_params=_params('parallel'), name=name)(h, g.reshape(1, d))


def _rms_bwd(h, g, dhn, dres, name):
    t, d = h.shape
    tr = _row_tile(t)

    def body(h_ref, g_ref, dhn_ref, dres_ref, dh_ref, dg_ref):
        @pl.when(pl.program_id(0) == 0)
        def _():
            dg_ref[...] = jnp.zeros_like(dg_ref)

        x = h_ref[...]
        r = lax.rsqrt(jnp.mean(x * x, axis=-1, keepdims=True) + RMS_EPS)
        dy = dhn_ref[...].astype(F32)
        gy = dy * g_ref[...]
        dx = r * gy - x * (r * r * r) * jnp.mean(gy * x, axis=-1, keepdims=True)
        dh_ref[...] = dres_ref[...] + dx
        dg_ref[...] += jnp.sum(dy * (x * r), axis=0, keepdims=True)

    return pl.pallas_call(body, grid=(t // tr,),
                          in_specs=[_row(tr, d), _full((1, d)), _row(tr, d), _row(tr, d)],
                          out_specs=[_row(tr, d), _full((1, d))],
                          out_shape=[SDS((t, d), F32), SDS((1, d), F32)],
                          compiler_params=_params('arbitrary'), name=name)(h, g.reshape(1, d), dhn, dres)


def _head(h, g, target, name):
    t, d = h.shape
    tr = _row_tile(t)

    def body(h_ref, g_ref, t_ref, dh_ref, loss_ref, dg_ref):
        @pl.when(pl.program_id(0) == 0)
        def _():
            dg_ref[...] = jnp.zeros_like(dg_ref)
            loss_ref[...] = jnp.zeros_like(loss_ref)

        x = h_ref[...]
        gv = g_ref[...]
        r = lax.rsqrt(jnp.mean(x * x, axis=-1, keepdims=True) + RMS_EPS)
        xh = x * r
        e = xh * gv - t_ref[...]
        per_tok = jnp.mean(e * e, axis=-1, keepdims=True)
        loss_ref[...] += jnp.broadcast_to(0.5 * jnp.sum(per_tok, axis=0, keepdims=True), loss_ref.shape)
        dy = e * (1.0 / d)
        gy = dy * gv
        dh_ref[...] = r * gy - x * (r * r * r) * jnp.mean(gy * x, axis=-1, keepdims=True)
        dg_ref[...] += jnp.sum(dy * xh, axis=0, keepdims=True)

    return pl.pallas_call(body, grid=(t // tr,),
                          in_specs=[_row(tr, d), _full((1, d)), _row(tr, d)],
                          out_specs=[_row(tr, d), _full((1, 128)), _full((1, d))],
                          out_shape=[SDS((t, d), F32), SDS((1, 128), F32), SDS((1, d), F32)],
                          compiler_params=_params('arbitrary'), name=name)(h, g.reshape(1, d), target)


def _ple_bwd_gate(dh3, gate, pp, name):
    t, d = dh3.shape
    tr = _row_tile(t)

    def body(dh_ref, g_ref, pp_ref, dpp_ref, dgl_ref):
        dh = dh_ref[...]
        gt = g_ref[...]
        dpp_ref[...] = (dh * gt).astype(dpp_ref.dtype)
        dgl_ref[...] = (dh * pp_ref[...] * gt * (1.0 - gt)).astype(dgl_ref.dtype)

    return pl.pallas_call(body, grid=(t // tr,), in_specs=[_row(tr, d)] * 3, out_specs=[_row(tr, d)] * 2,
                          out_shape=[SDS((t, d), BF16), SDS((t, d), BF16)],
                          compiler_params=_params('parallel'), name=name)(dh3, gate, pp)


def _halo_prev(tr, c, col=0):
    return pl.BlockSpec((HALO, c), lambda i, col=col: (jnp.maximum(i * (tr // HALO) - 1, 0), col))


def _halo_next(tr, c, t, col=0):
    return pl.BlockSpec((HALO, c), lambda i, col=col: (jnp.minimum((i + 1) * (tr // HALO), t // HALO - 1), col))


def _sc_fwd(z, w, name):
    t, c3 = z.shape
    c = c3 // 3
    tr = _row_tile(t)

    def body(z_ref, zp_ref, w_ref, y_ref):
        i = pl.program_id(0)
        zz = z_ref[...]
        gb, cx = zz[:, :c], zz[:, c:2 * c] * zz[:, 2 * c:]
        zp = zp_ref[SUB:HALO, :]
        cxp = jnp.where(i > 0, zp[:, c:2 * c] * zp[:, 2 * c:], 0.0)
        wv = w_ref[...]
        conv = wv[2:3] * cx + wv[1:2] * _shift_down(cx, 1, cxp) + wv[0:1] * _shift_down(cx, 2, cxp)
        y_ref[...] = (gb * conv).astype(y_ref.dtype)

    return pl.pallas_call(body, grid=(t // tr,),
                          in_specs=[_row(tr, c3), _halo_prev(tr, c3), _full((3, c))],
                          out_specs=_row(tr, c), out_shape=SDS((t, c), BF16),
                          compiler_params=_params('parallel'), name=name)(z, z, w)


def _sc_bwd(dy, z, w, name):
    t, c3 = z.shape
    c = c3 // 3
    tr = _row_tile(t)
    nt = t // tr

    def body(dy_ref, dyn_ref, z_ref, zp_ref, zn_ref, w_ref, dz_ref, dw_ref):
        i = pl.program_id(0)

        @pl.when(i == 0)
        def _():
            dw_ref[...] = jnp.zeros_like(dw_ref)

        zz = z_ref[...]
        gb, gc, xi = zz[:, :c], zz[:, c:2 * c], zz[:, 2 * c:]
        cx = gc * xi
        zp = zp_ref[SUB:HALO, :]
        cxp = jnp.where(i > 0, zp[:, c:2 * c] * zp[:, 2 * c:], 0.0)
        wv = w_ref[...]
        cx1, cx2 = _shift_down(cx, 1, cxp), _shift_down(cx, 2, cxp)
        conv = wv[2:3] * cx + wv[1:2] * cx1 + wv[0:1] * cx2
        dyv = dy_ref[...]
        dconv = dyv * gb
        dcn = jnp.where(i < nt - 1, dyn_ref[0:SUB, :] * zn_ref[0:SUB, :c], 0.0)
        dcx = wv[2:3] * dconv + wv[1:2] * _shift_up(dconv, 1, dcn) + wv[0:1] * _shift_up(dconv, 2, dcn)
        dz_ref[:, :c] = (dyv * conv).astype(dz_ref.dtype)
        dz_ref[:, c:2 * c] = (dcx * xi).astype(dz_ref.dtype)
        dz_ref[:, 2 * c:] = (dcx * gc).astype(dz_ref.dtype)
        dw_ref[...] += jnp.concatenate([jnp.sum(dconv * cx2, axis=0, keepdims=True),
                                        jnp.sum(dconv * cx1, axis=0, keepdims=True),
                                        jnp.sum(dconv * cx, axis=0, keepdims=True)], axis=0)

    return pl.pallas_call(body, grid=(nt,),
                          in_specs=[_row(tr, c), _halo_next(tr, c, t), _row(tr, c3), _halo_prev(tr, c3),
                                    _halo_next(tr, c3, t), _full((3, c))],
                          out_specs=[_row(tr, c3), _full((3, c))],
                          out_shape=[SDS((t, c3), BF16), SDS((3, c), F32)],
                          compiler_params=_params('arbitrary'), name=name)(dy, dy, z, z, z, w)


def _rope_tables(pos_ref, invf_ref, sign):
    lane = lax.broadcasted_iota(jnp.int32, (pos_ref.shape[0], HEAD_DIM), 1)
    ang = pos_ref[...].astype(F32) * invf_ref[...]
    half = ROPE_DIM // 2
    cos = jnp.where(lane < ROPE_DIM, jnp.cos(ang), 1.0)
    sin = jnp.sin(ang) * sign
    sin = jnp.where(lane < half, -sin, jnp.where(lane < ROPE_DIM, sin, 0.0))
    return lane, cos, sin


def _rope_apply(x, lane, cos, sin):
    half = ROPE_DIM // 2
    xs = jnp.where(lane < half, pltpu.roll(x, HEAD_DIM - half, 1), pltpu.roll(x, half, 1))
    return x * cos + xs * sin


def _rope_fwd(qkv, pos, invf, name):
    t, w3 = qkv.shape
    w = w3 // 3
    tr = _row_tile(t)

    def body(q_ref, k_ref, v_ref, pos_ref, invf_ref, qo_ref, ko_ref, vo_ref):
        lane, cos, sin = _rope_tables(pos_ref, invf_ref, 1.0)
        for hh in range(w // HEAD_DIM):
            cs = slice(hh * HEAD_DIM, (hh + 1) * HEAD_DIM)
            qo_ref[:, cs] = _rope_apply(q_ref[:, cs], lane, cos, sin).astype(qo_ref.dtype)
            ko_ref[:, cs] = _rope_apply(k_ref[:, cs], lane, cos, sin).astype(ko_ref.dtype)
        vo_ref[...] = v_ref[...].astype(vo_ref.dtype)

    return pl.pallas_call(body, grid=(t // tr,),
                          in_specs=[_row(tr, w, 0), _row(tr, w, 1), _row(tr, w, 2), _row(tr, 1), _full((1, HEAD_DIM))],
                          out_specs=[_row(tr, w)] * 3, out_shape=[SDS((t, w), BF16)] * 3,
                          compiler_params=_params('parallel'), name=name)(qkv, qkv, qkv, pos, invf)


def _rope_bwd(dqs, dks, dvs, pos, invf, name):
    t, d = dqs[0].shape
    ng = len(dqs)
    w = ng * d
    tr = _row_tile(t)

    def body(*refs):
        dq_refs, dk_refs, dv_refs = refs[:ng], refs[ng:2 * ng], refs[2 * ng:3 * ng]
        pos_ref, invf_ref, o_ref = refs[3 * ng], refs[3 * ng + 1], refs[3 * ng + 2]
        lane, cos, sin = _rope_tables(pos_ref, invf_ref, -1.0)
        for g in range(ng):
            for hh in range(d // HEAD_DIM):
                cs = slice(hh * HEAD_DIM, (hh + 1) * HEAD_DIM)
                base = g * d + hh * HEAD_DIM
                o_ref[:, base:base + HEAD_DIM] = _rope_apply(dq_refs[g][:, cs], lane, cos, sin).astype(o_ref.dtype)
                o_ref[:, w + base:w + base + HEAD_DIM] = _rope_apply(dk_refs[g][:, cs], lane, cos, sin).astype(o_ref.dtype)
            o_ref[:, 2 * w + g * d:2 * w + (g + 1) * d] = dv_refs[g][...].astype(o_ref.dtype)

    return pl.pallas_call(body, grid=(t // tr,),
                          in_specs=[_row(tr, d)] * (3 * ng) + [_row(tr, 1), _full((1, HEAD_DIM))],
                          out_specs=_row(tr, 3 * w), out_shape=SDS((t, 3 * w), BF16),
                          compiler_params=_params('parallel'), name=name)(*dqs, *dks, *dvs, pos, invf)


def _attn_masks():
    qi = lax.broadcasted_iota(jnp.int32, (ATTN_BLOCK, ATTN_BLOCK), 0)
    kj = lax.broadcasted_iota(jnp.int32, (ATTN_BLOCK, ATTN_BLOCK), 1)
    return kj >= qi, kj <= qi


def _attn_cols(l, width):
    ncol = width // HEAD_DIM
    cpb = max(1, min(ncol, 32 // (l // ATTN_BLOCK)))
    assert ncol % cpb == 0
    return cpb


def _attn_fwd(q, k, v, name):
    l, width = q.shape
    cpb = _attn_cols(l, width)
    nb = l // ATTN_BLOCK
    scale = HEAD_DIM ** -0.5

    def body(q_ref, k_ref, v_ref, o_ref, lse_ref):
        m_prev, m_cur = _attn_masks()
        for col in range(cpb):
            cs = slice(col * HEAD_DIM, (col + 1) * HEAD_DIM)

            def step(b, carry, cs=cs):
                r0 = pl.multiple_of(b * ATTN_BLOCK, ATTN_BLOCK)
                rp = pl.multiple_of(jnp.maximum(b - 1, 0) * ATTN_BLOCK, ATTN_BLOCK)
                qb = q_ref[pl.ds(r0, ATTN_BLOCK), cs]
                s_p = lax.dot_general(qb, k_ref[pl.ds(rp, ATTN_BLOCK), cs], _DIMS['nt'], preferred_element_type=F32) * scale
                s_c = lax.dot_general(qb, k_ref[pl.ds(r0, ATTN_BLOCK), cs], _DIMS['nt'], preferred_element_type=F32) * scale
                s_p = jnp.where(jnp.logical_and(m_prev, b > 0), s_p, NEG)
                s_c = jnp.where(m_cur, s_c, NEG)
                m = jnp.maximum(jnp.max(s_p, axis=-1, keepdims=True), jnp.max(s_c, axis=-1, keepdims=True))
                p_p, p_c = jnp.exp(s_p - m), jnp.exp(s_c - m)
                den = jnp.sum(p_p, axis=-1, keepdims=True) + jnp.sum(p_c, axis=-1, keepdims=True)
                acc = jnp.dot(p_p.astype(BF16), v_ref[pl.ds(rp, ATTN_BLOCK), cs], preferred_element_type=F32)
                acc += jnp.dot(p_c.astype(BF16), v_ref[pl.ds(r0, ATTN_BLOCK), cs], preferred_element_type=F32)
                o_ref[pl.ds(r0, ATTN_BLOCK), cs] = acc / den
                lse_ref[pl.ds(r0, ATTN_BLOCK), cs] = jnp.broadcast_to(m + jnp.log(den), (ATTN_BLOCK, HEAD_DIM))
                return carry

            lax.fori_loop(0, nb, step, 0)

    spec = pl.BlockSpec((l, cpb * HEAD_DIM), lambda j: (0, j))
    return pl.pallas_call(body, grid=(width // (cpb * HEAD_DIM),), in_specs=[spec] * 3, out_specs=[spec] * 2,
                          out_shape=[SDS((l, width), F32)] * 2,
                          compiler_params=_params('parallel'), name=name)(q, k, v)


def _attn_bwd(q, k, v, do, lse, delta, name):
    l, width = q.shape
    cpb = _attn_cols(l, width)
    nb = l // ATTN_BLOCK
    scale = HEAD_DIM ** -0.5

    def body(q_ref, k_ref, v_ref, do_ref, lse_ref, dl_ref, dq_ref, dk_ref, dv_ref):
        m_prev, m_cur = _attn_masks()
        dk_ref[...] = jnp.zeros_like(dk_ref)
        dv_ref[...] = jnp.zeros_like(dv_ref)
        for col in range(cpb):
            cs = slice(col * HEAD_DIM, (col + 1) * HEAD_DIM)

            def step(b, carry, cs=cs):
                r0 = pl.multiple_of(b * ATTN_BLOCK, ATTN_BLOCK)
                rp = pl.multiple_of(jnp.maximum(b - 1, 0) * ATTN_BLOCK, ATTN_BLOCK)
                qb, dob = q_ref[pl.ds(r0, ATTN_BLOCK), cs], do_ref[pl.ds(r0, ATTN_BLOCK), cs]
                kp, kc = k_ref[pl.ds(rp, ATTN_BLOCK), cs], k_ref[pl.ds(r0, ATTN_BLOCK), cs]
                vp, vc = v_ref[pl.ds(rp, ATTN_BLOCK), cs], v_ref[pl.ds(r0, ATTN_BLOCK), cs]
                lse_b = lse_ref[pl.ds(r0, ATTN_BLOCK), cs]
                dl_b = dl_ref[pl.ds(r0, ATTN_BLOCK), cs]
                s_p = lax.dot_general(qb, kp, _DIMS['nt'], preferred_element_type=F32) * scale
                s_c = lax.dot_general(qb, kc, _DIMS['nt'], preferred_element_type=F32) * scale
                p_p = jnp.exp(jnp.where(jnp.logical_and(m_prev, b > 0), s_p, NEG) - lse_b)
                p_c = jnp.exp(jnp.where(m_cur, s_c, NEG) - lse_b)
                dp_p = lax.dot_general(dob, vp, _DIMS['nt'], preferred_element_type=F32)
                dp_c = lax.dot_general(dob, vc, _DIMS['nt'], preferred_element_type=F32)
                ds_p = (p_p * (dp_p - dl_b) * scale).astype(BF16)
                ds_c = (p_c * (dp_c - dl_b) * scale).astype(BF16)
                dq_ref[pl.ds(r0, ATTN_BLOCK), cs] = (jnp.dot(ds_p, kp, preferred_element_type=F32)
                                                     + jnp.dot(ds_c, kc, preferred_element_type=F32))
                dk_ref[pl.ds(rp, ATTN_BLOCK), cs] += lax.dot_general(ds_p, qb, _DIMS['tn'], preferred_element_type=F32)
                dk_ref[pl.ds(r0, ATTN_BLOCK), cs] += lax.dot_general(ds_c, qb, _DIMS['tn'], preferred_element_type=F32)
                dv_ref[pl.ds(rp, ATTN_BLOCK), cs] += lax.dot_general(p_p.astype(BF16), dob, _DIMS['tn'], preferred_element_type=F32)
                dv_ref[pl.ds(r0, ATTN_BLOCK), cs] += lax.dot_general(p_c.astype(BF16), dob, _DIMS['tn'], preferred_element_type=F32)
                return carry

            lax.fori_loop(0, nb, step, 0)

    spec = pl.BlockSpec((l, cpb * HEAD_DIM), lambda j: (0, j))
    return pl.pallas_call(body, grid=(width // (cpb * HEAD_DIM),), in_specs=[spec] * 6, out_specs=[spec] * 3,
                          out_shape=[SDS((l, width), F32)] * 3,
                          compiler_params=_params('parallel'), name=name)(q, k, v, do, lse, delta)


def _attn_combine(os_, lses, name):
    t, d = os_[0].shape
    ng = len(os_)
    tr = _row_tile(t)

    def body(*refs):
        o_refs, l_refs, o_out, lse_out = refs[:ng], refs[ng:2 * ng], refs[2 * ng], refs[2 * ng + 1]
        ls = [r[...] for r in l_refs]
        m = functools.reduce(jnp.maximum, ls)
        ws = [jnp.exp(x - m) for x in ls]
        den = functools.reduce(lambda a, b: a + b, ws)
        acc = functools.reduce(lambda a, b: a + b, [w * o[...] for w, o in zip(ws, o_refs)])
        o_out[...] = (acc / den).astype(o_out.dtype)
        lse_out[...] = m + jnp.log(den)

    return pl.pallas_call(body, grid=(t // tr,), in_specs=[_row(tr, d)] * (2 * ng), out_specs=[_row(tr, d)] * 2,
                          out_shape=[SDS((t, d), BF16), SDS((t, d), F32)],
                          compiler_params=_params('parallel'), name=name)(*os_, *lses)


def _delta_epilogue(acc, o):
    prod = acc * o.astype(F32)
    segs = [jnp.broadcast_to(jnp.sum(prod[:, s:s + HEAD_DIM], axis=-1, keepdims=True), (acc.shape[0], HEAD_DIM))
            for s in range(0, acc.shape[1], HEAD_DIM)]
    return acc, jnp.concatenate(segs, axis=-1)


LRU_TILE = 128


def _lru_gates(xr, wa_ref, ba, wx_ref, bx, lam):
    nb = wa_ref.shape[0]
    xb = xr.astype(BF16)
    ra = jnp.concatenate([jnp.dot(xb[:, n * LRU_BLOCK:(n + 1) * LRU_BLOCK], wa_ref[n], preferred_element_type=F32)
                          for n in range(nb)], axis=-1) + ba
    ia = jnp.concatenate([jnp.dot(xb[:, n * LRU_BLOCK:(n + 1) * LRU_BLOCK], wx_ref[n], preferred_element_type=F32)
                          for n in range(nb)], axis=-1) + bx
    r, ig = _sigmoid(ra), _sigmoid(ia)
    sp = _softplus(-lam)
    log_a = -LRU_C * r * sp
    a = jnp.exp(log_a)
    mult = jnp.sqrt(-_expm1(2.0 * log_a))
    return xb, r, ig, sp, a, mult


def _lru_fwd(z, cw, cb, wa, ba, wx, bx, lam, name):
    t, c2 = z.shape
    c = c2 // 2
    nb = c // LRU_BLOCK
    tr = _row_tile(t, LRU_TILE)

    def body(g_ref, x_ref, xp_ref, cw_ref, cb_ref, wa_ref, ba_ref, wx_ref, bx_ref, lam_ref,
             y_ref, hs_ref, xr_ref, car_ref):
        i = pl.program_id(0)

        @pl.when(i == 0)
        def _():
            car_ref[...] = jnp.zeros_like(car_ref)

        x0 = x_ref[...]
        xp = jnp.where(i > 0, xp_ref[SUB:HALO, :], 0.0)
        cwv = cw_ref[...]
        xr = (cb_ref[...] + cwv[3:4] * x0 + cwv[2:3] * _shift_down(x0, 1, xp)
              + cwv[1:2] * _shift_down(x0, 2, xp) + cwv[0:1] * _shift_down(x0, 3, xp))
        xr_ref[...] = xr
        _, _, ig, _, a, mult = _lru_gates(xr, wa_ref, ba_ref[...], wx_ref, bx_ref[...], lam_ref[...])
        u = mult * (ig * xr)
        row = lax.broadcasted_iota(jnp.int32, (SUB, c), 0)
        car = car_ref[...]
        for j in range(tr // SUB):
            ab, ub = a[j * SUB:(j + 1) * SUB], u[j * SUB:(j + 1) * SUB]
            for s in (1, 2, 4):
                a_sh = jnp.where(row >= s, pltpu.roll(ab, s, 0), 1.0)
                u_sh = jnp.where(row >= s, pltpu.roll(ub, s, 0), 0.0)
                ub = ab * u_sh + ub
                ab = ab * a_sh
            hb = ub + ab * car
            hs_ref[j * SUB:(j + 1) * SUB, :] = hb
            car = jnp.broadcast_to(hb[SUB - 1:SUB], (SUB, c))
        car_ref[...] = car
        gl, _ = _gelu_and_grad(g_ref[...])
        y_ref[...] = (hs_ref[...] * gl).astype(y_ref.dtype)

    return pl.pallas_call(
        body, grid=(t // tr,),
        in_specs=[_row(tr, c, 0), _row(tr, c, 1), _halo_prev(tr, c, 1), _full((4, c)), _full((1, c)),
                  _full((nb, LRU_BLOCK, LRU_BLOCK)), _full((1, c)), _full((nb, LRU_BLOCK, LRU_BLOCK)), _full((1, c)), _full((1, c))],
        out_specs=[_row(tr, c)] * 3,
        out_shape=[SDS((t, c), BF16), SDS((t, c), F32), SDS((t, c), F32)],
        scratch_shapes=[pltpu.VMEM((SUB, c), F32)],
        compiler_params=_params('arbitrary'), name=name)(
            z, z, z, cw, cb.reshape(1, c), wa, ba.reshape(1, c), wx, bx.reshape(1, c), lam.reshape(1, c))


def _lru_bwd(dy, z, xr, hs, cw, wa, ba, wx, bx, lam, name):
    t, c2 = z.shape
    c = c2 // 2
    nb = c // LRU_BLOCK
    tr = _row_tile(t, LRU_TILE)
    nt = t // tr

    def rev(col=0):
        return pl.BlockSpec((tr, c), lambda i, col=col: (nt - 1 - i, col))

    def rev_prev(col=0):
        return pl.BlockSpec((HALO, c), lambda i, col=col: (jnp.maximum((nt - 1 - i) * (tr // HALO) - 1, 0), col))

    def body(dy_ref, g_ref, x_ref, xp_ref, xr_ref, hs_ref, hp_ref, cw_ref, wa_ref, ba_ref, wx_ref, bx_ref, lam_ref,
             dz_ref, dwa_ref, dwx_ref, dvec_ref, lcar_ref, ahead_ref, dxhead_ref, lam_s):
        i = pl.program_id(0)
        first_tile = i == nt - 1

        @pl.when(i == 0)
        def _():
            lcar_ref[...] = jnp.zeros_like(lcar_ref)
            ahead_ref[...] = jnp.zeros_like(ahead_ref)
            dxhead_ref[...] = jnp.zeros_like(dxhead_ref)
            dwa_ref[...] = jnp.zeros_like(dwa_ref)
            dwx_ref[...] = jnp.zeros_like(dwx_ref)
            dvec_ref[...] = jnp.zeros_like(dvec_ref)

        xrv = xr_ref[...]
        lamv = lam_ref[...]
        xb, r, ig, sp, a, mult = _lru_gates(xrv, wa_ref, ba_ref[...], wx_ref, bx_ref[...], lamv)
        hsv = hs_ref[...]
        dyv = dy_ref[...]
        gl, dgl = _gelu_and_grad(g_ref[...])
        dhs = dyv * gl
        dz_ref[:, :c] = (dyv * hsv * dgl).astype(dz_ref.dtype)

        a_next = _shift_up(a, 1, ahead_ref[...])
        row = lax.broadcasted_iota(jnp.int32, (SUB, c), 0)
        car = lcar_ref[...]
        for j in reversed(range(tr // SUB)):
            ab, ub = a_next[j * SUB:(j + 1) * SUB], dhs[j * SUB:(j + 1) * SUB]
            for s in (1, 2, 4):
                a_sh = jnp.where(row < SUB - s, pltpu.roll(ab, SUB - s, 0), 1.0)
                u_sh = jnp.where(row < SUB - s, pltpu.roll(ub, SUB - s, 0), 0.0)
                ub = ab * u_sh + ub
                ab = ab * a_sh
            lb = ub + ab * car
            lam_s[j * SUB:(j + 1) * SUB, :] = lb
            car = jnp.broadcast_to(lb[0:1], (SUB, c))
        lcar_ref[...] = car
        ahead_ref[...] = a[0:SUB]
        lmb = lam_s[...]

        hp = jnp.where(first_tile, 0.0, hp_ref[SUB:HALO, :])
        h_prev = _shift_down(hsv, 1, hp)
        d_a = lmb * h_prev
        d_mult = lmb * (ig * xrv)
        d_ixr = lmb * mult
        d_ig = d_ixr * xrv
        dxr = d_ixr * ig
        d_la = d_a * a - d_mult * (a * a) / mult
        d_r = d_la * (-LRU_C * sp)
        d_sp = jnp.sum(d_la * (-LRU_C * r), axis=0, keepdims=True)
        d_ra = d_r * r * (1.0 - r)
        d_ia = d_ig * ig * (1.0 - ig)
        d_rab, d_iab = d_ra.astype(BF16), d_ia.astype(BF16)
        parts = []
        for n in range(nb):
            cs = slice(n * LRU_BLOCK, (n + 1) * LRU_BLOCK)
            parts.append(lax.dot_general(d_rab[:, cs], wa_ref[n], _DIMS['nt'], preferred_element_type=F32)
                         + lax.dot_general(d_iab[:, cs], wx_ref[n], _DIMS['nt'], preferred_element_type=F32))
            dwa_ref[n] += lax.dot_general(xb[:, cs], d_rab[:, cs], _DIMS['tn'], preferred_element_type=F32)
            dwx_ref[n] += lax.dot_general(xb[:, cs], d_iab[:, cs], _DIMS['tn'], preferred_element_type=F32)
        dxr = dxr + jnp.concatenate(parts, axis=-1)

        cwv = cw_ref[...]
        nxt = dxhead_ref[...]
        dx0 = (cwv[3:4] * dxr + cwv[2:3] * _shift_up(dxr, 1, nxt) + cwv[1:2] * _shift_up(dxr, 2, nxt)
               + cwv[0:1] * _shift_up(dxr, 3, nxt))
        dxhead_ref[...] = dxr[0:SUB]
        dz_ref[:, c:] = dx0.astype(dz_ref.dtype)

        x0 = x_ref[...]
        xp = jnp.where(first_tile, 0.0, xp_ref[SUB:HALO, :])
        sums = [jnp.sum(d_ra, axis=0, keepdims=True), jnp.sum(d_ia, axis=0, keepdims=True),
                d_sp * (-_sigmoid(-lamv)), jnp.sum(dxr, axis=0, keepdims=True),
                jnp.sum(dxr * _shift_down(x0, 3, xp), axis=0, keepdims=True),
                jnp.sum(dxr * _shift_down(x0, 2, xp), axis=0, keepdims=True),
                jnp.sum(dxr * _shift_down(x0, 1, xp), axis=0, keepdims=True),
                jnp.sum(dxr * x0, axis=0, keepdims=True)]
        dvec_ref[...] += jnp.concatenate(sums, axis=0)

    wspec = _full((nb, LRU_BLOCK, LRU_BLOCK))
    return pl.pallas_call(
        body, grid=(nt,),
        in_specs=[rev(), rev(0), rev(1), rev_prev(1), rev(), rev(), rev_prev(), _full((4, c)),
                  wspec, _full((1, c)), wspec, _full((1, c)), _full((1, c))],
        out_specs=[pl.BlockSpec((tr, c2), lambda i: (nt - 1 - i, 0)), wspec, wspec, _full((SUB, c))],
        out_shape=[SDS((t, c2), BF16), SDS((nb, LRU_BLOCK, LRU_BLOCK), F32), SDS((nb, LRU_BLOCK, LRU_BLOCK), F32),
                   SDS((SUB, c), F32)],
        scratch_shapes=[pltpu.VMEM((SUB, c), F32), pltpu.VMEM((SUB, c), F32), pltpu.VMEM((SUB, c), F32),
                        pltpu.VMEM((tr, c), F32)],
        compiler_params=_params('arbitrary'), name=name)(
            dy, z, z, z, xr, hs, hs, cw, wa, ba.reshape(1, c), wx, bx.reshape(1, c), lam.reshape(1, c))


def _dilate(x, d):
    t, w = x.shape
    return x.reshape(t // d, d * w)


def _local_step(x, p, pos, target, rep, weights_for_layer, emit_grads):
    t, d = x.shape
    depth = p.shape[0]
    w = rep
    half = ROPE_DIM // 2
    invf = ROPE_THETA ** (-2.0 * jnp.arange(half, dtype=F32) / ROPE_DIM)
    invf = jnp.concatenate([invf, invf, jnp.zeros((HEAD_DIM - ROPE_DIM,), F32)]).reshape(1, HEAD_DIM)
    ng = len(DILATED_PATTERNS)
    saved = []
    h = x
    for i in range(depth):
        kind, j = i % N_MIXERS, i // N_MIXERS
        wl = weights_for_layer(i, h)
        s = {'h0': h, 'wl': wl}
        hn = _rms_fwd(h, w['norm_mix'][i], f'rms_mix_fwd_{i}')
        s['hn'] = hn
        if kind == 0:
            z = _mm(hn, wl['w_in'], 'nn', f'sc_in_{i}')
            y = _sc_fwd(z, wl['small'], f'sc_conv_fwd_{i}')
            h1 = _mm(y, wl['w_out'], 'nn', f'sc_out_{i}', extras=(h,), epi=lambda acc, res: (acc + res,))
            s.update(z=z, y=y)
        elif kind == 1:
            qkv = _mm(hn, wl['w_in'], 'nn', f'attn_qkv_{i}', tn=1152)
            q, k, v = _rope_fwd(qkv, pos, invf, f'rope_fwd_{i}')
            os_, lses, views = [], [], []
            for g, (_, dil) in enumerate(DILATED_PATTERNS):
                qg, kg, vg = (_dilate(a[:, g * d:(g + 1) * d], dil) for a in (q, k, v))
                og, lg = _attn_fwd(qg, kg, vg, f'attn_fwd_{i}_g{g}')
                os_.append(og.reshape(t, d))
                lses.append(lg.reshape(t, d))
                views.append((qg, kg, vg))
            o, lse = _attn_combine(os_, lses, f'attn_combine_{i}')
            h1 = _mm(o, wl['w_out'], 'nn', f'attn_out_{i}', extras=(h,), epi=lambda acc, res: (acc + res,))
            s.update(views=views, o=o, lse=lse)
        else:
            z = _mm(hn, wl['w_in'], 'nn', f'lru_in_{i}', tn=1280)
            sm = wl['small']
            y, hs, xr = _lru_fwd(z, sm[0:4], sm[4:5], w['lru_w_a'][j], sm[5:6], w['lru_w_x'][j], sm[6:7], sm[7:8],
                                 f'lru_fwd_{i}')
            h1 = _mm(y, wl['w_out'], 'nn', f'lru_out_{i}', extras=(h,), epi=lambda acc, res: (acc + res,), tk=640)
            s.update(z=z, y=y, hs=hs, xr=xr)
        s['h1'] = h1
        hm = _rms_fwd(h1, w['norm_mlp'][i], f'rms_mlp_fwd_{i}')
        u = _mm(hm, wl['mlp_up'], 'nn', f'mlp_up_{i}', out_dtypes=(BF16,))
        h2 = _mm(u, wl['mlp_down'], 'nn', f'mlp_down_{i}', a_pro=_relu2, extras=(h1,), epi=lambda acc, res: (acc + res,))
        hp = _rms_fwd(h2, w['norm_ple'][i], f'rms_ple_fwd_{i}')
        pp = _mm(p[i], wl['ple_proj'], 'nn', f'ple_proj_{i}')
        h3, gate = _mm(hp, wl['ple_gate'], 'nn', f'ple_gate_{i}', out_dtypes=(F32, F32), extras=(pp, h2),
                       epi=lambda acc, ppv, res: (res + _sigmoid(acc) * ppv, _sigmoid(acc)))
        s.update(hm=hm, u=u, h2=h2, hp=hp, pp=pp, gate=gate)
        saved.append(s)
        h = h3

    dh, loss, dg_final = _head(h, w['norm_final'], target, 'loss_head')
    grads = {n: [None] * len(w[n]) for n in w if n != 'norm_final'}
    grads['norm_final'] = dg_final.reshape(d)
    started = 0.0
    for i in reversed(range(depth)):
        kind, j = i % N_MIXERS, i // N_MIXERS
        s = saved[i]
        wl, gl = s['wl'], {}
        dpp, dgl = _ple_bwd_gate(dh, s['gate'], s['pp'], f'ple_bwd_gate_{i}')
        gl['ple_proj'] = _mm(p[i], dpp, 'tn', f'ple_dproj_{i}', out_dtypes=(BF16,))
        gl['ple_gate'] = _mm(s['hp'], dgl, 'tn', f'ple_dgate_{i}', out_dtypes=(BF16,))
        dhp = _mm(dgl, wl['ple_gate'], 'nt', f'ple_dhp_{i}')
        dh, dg = _rms_bwd(s['h2'], w['norm_ple'][i] + started, dhp, dh, f'rms_ple_bwd_{i}')
        grads['norm_ple'][i] = dg.reshape(d)
        du = _mm(dh, wl['mlp_down'], 'nt', f'mlp_du_{i}', out_dtypes=(BF16,), extras=(s['u'],),
                 epi=lambda acc, uv: (acc * 2.0 * jnp.maximum(uv.astype(F32), 0.0),))
        gl['mlp_down'] = _mm(s['u'], dh, 'tn', f'mlp_ddown_{i}', out_dtypes=(BF16,), a_pro=_relu2)
        gl['mlp_up'] = _mm(s['hm'], du, 'tn', f'mlp_dup_{i}', out_dtypes=(BF16,), out_stacked=True)
        dhm = _mm(du, wl['mlp_up'], 'nt', f'mlp_dhm_{i}')
        dh, dg = _rms_bwd(s['h1'], w['norm_mlp'][i], dhm, dh, f'rms_mlp_bwd_{i}')
        grads['norm_mlp'][i] = dg.reshape(d)
        if kind == 0:
            dy = _mm(dh, wl['w_out'], 'nt', f'sc_dy_{i}')
            gl['w_out'] = _mm(s['y'], dh, 'tn', f'sc_dout_{i}', out_dtypes=(BF16,))
            dz, dwc = _sc_bwd(dy, s['z'], wl['small'], f'sc_conv_bwd_{i}')
            gl['small'] = dwc
            gl['w_in'] = _mm(s['hn'], dz, 'tn', f'sc_din_{i}', out_dtypes=(BF16,), out_stacked=True)
            dhn = _mm(dz, wl['w_in'], 'nt', f'sc_dhn_{i}')
        elif kind == 1:
            do, delta = _mm(dh, wl['w_out'], 'nt', f'attn_do_{i}', out_dtypes=(BF16, F32), extras=(s['o'],),
                            epi=_delta_epilogue, tn=d)
            gl['w_out'] = _mm(s['o'], dh, 'tn', f'attn_dwo_{i}', out_dtypes=(BF16,))
            dqs, dks, dvs = [], [], []
            for g, (_, dil) in enumerate(DILATED_PATTERNS):
                qg, kg, vg = s['views'][g]
                dqg, dkg, dvg = _attn_bwd(qg, kg, vg, _dilate(do, dil), _dilate(s['lse'], dil), _dilate(delta, dil),
                                          f'attn_bwd_{i}_g{g}')
                dqs.append(dqg.reshape(t, d))
                dks.append(dkg.reshape(t, d))
                dvs.append(dvg.reshape(t, d))
            dqkv = _rope_bwd(dqs, dks, dvs, pos, invf, f'rope_bwd_{i}')
            gl['w_in'] = _mm(s['hn'], dqkv, 'tn', f'attn_dqkv_{i}', out_dtypes=(BF16,), out_stacked=True, tn=1152)
            dhn = _mm(dqkv, wl['w_in'], 'nt', f'attn_dhn_{i}', tk=1152)
        else:
            dy = _mm(dh, wl['w_out'], 'nt', f'lru_dy_{i}', tn=1280)
            gl['w_out'] = _mm(s['y'], dh, 'tn', f'lru_dout_{i}', out_dtypes=(BF16,), tm=1280)
            sm = wl['small']
            dz, dwa, dwx, dvec = _lru_bwd(dy, s['z'], s['xr'], s['hs'], sm[0:4], w['lru_w_a'][j], sm[5:6],
                                          w['lru_w_x'][j], sm[6:7], sm[7:8], f'lru_bwd_{i}')
            grads['lru_w_a'][j], grads['lru_w_x'][j] = dwa, dwx
            gl['small'] = dvec
            gl['w_in'] = _mm(s['hn'], dz, 'tn', f'lru_din_{i}', out_dtypes=(BF16,), tn=1280)
            dhn = _mm(dz, wl['w_in'], 'nt', f'lru_dhn_{i}', tk=640)
        dh, dg = _rms_bwd(s['h0'], w['norm_mix'][i], dhn, dh, f'rms_mix_bwd_{i}')
        grads['norm_mix'][i] = dg.reshape(d)
        started = emit_grads(i, gl)
    return loss, dh, grads


_MESH = pl.DeviceIdType.MESH
_ANY = pl.BlockSpec(memory_space=pl.ANY)


def _block_view(ref, kind, idx):
    if kind == 'stack':
        return ref.at[idx]
    r = ref.shape[0] // N_DEV
    return ref.at[pl.ds(idx * r, r)]


def _gather_many(arrs, kinds, name):
    n = len(arrs)
    out_shapes = [SDS((N_DEV,) + a.shape if kd == 'stack' else (N_DEV * a.shape[0],) + a.shape[1:], a.dtype)
                  for a, kd in zip(arrs, kinds)]

    def body(*refs):
        x_refs, out_refs = refs[:n], refs[n:2 * n]
        send_sems, recv_sems, local_sems = refs[2 * n:]
        x, y, c = lax.axis_index('x'), lax.axis_index('y'), lax.axis_index('c')
        me, sibling = (x, y, c), (x, y, 1 - c)
        chips = [(1 - x, y), (x, 1 - y), (1 - x, 1 - y)]

        def slab(t, px, py, pc):
            return _block_view(out_refs[t], kinds[t], 4 * px + 2 * py + pc)

        def copy(t, k, block, to, src=None):
            return pltpu.make_async_remote_copy(
                src_ref=slab(t, *block) if src is None else src, dst_ref=slab(t, *block),
                send_sem=send_sems.at[7 * t + k], recv_sem=recv_sems.at[7 * t + k], device_id=to, device_id_type=_MESH)

        mine = [pltpu.make_async_copy(x_refs[t], slab(t, *me), local_sems.at[t]) for t in range(n)]
        for cp in mine:
            cp.start()
        first = [copy(t, 0, me, sibling, src=x_refs[t]) for t in range(n)]
        first += [copy(t, 1 + j, me, (*chip, c), src=x_refs[t]) for j, chip in enumerate(chips) for t in range(n)]
        for cp in first:
            cp.start()
        passed = []
        for j, chip in enumerate(chips):
            for t in range(n):
                copy(t, 1 + j, (*chip, c), me).wait_recv()
                passed.append(copy(t, 4 + j, (*chip, c), sibling))
                passed[-1].start()
        for t in range(n):
            copy(t, 0, sibling, me).wait_recv()
            for j, chip in enumerate(chips):
                copy(t, 4 + j, (*chip, 1 - c), me).wait_recv()
        for cp in first + passed:
            cp.wait_send()
        for cp in mine:
            cp.wait()

    return pl.pallas_call(
        body, out_shape=out_shapes, in_specs=[_ANY] * n, out_specs=[_ANY] * n,
        scratch_shapes=[pltpu.SemaphoreType.DMA((7 * n,)), pltpu.SemaphoreType.DMA((7 * n,)), pltpu.SemaphoreType.DMA((n,))],
        name=name)(*arrs)


_HBM = pl.BlockSpec(memory_space=pltpu.HBM)
_SEM = pl.BlockSpec(memory_space=pltpu.SEMAPHORE)
_EFFECT = pltpu.SideEffectType.DATAFLOW_SIDE_EFFECTING


def _direct_copies(mode, kinds, src_refs, land_refs, send_sems, recv_sems):
    x, y, c = lax.axis_index('x'), lax.axis_index('y'), lax.axis_index('c')
    my_idx = 4 * x + 2 * y + c
    copies = []
    for k in range(1, N_DEV):
        px, py, pc = (1 - x if k & 4 else x, 1 - y if k & 2 else y, 1 - c if k & 1 else c)
        for t, kd in enumerate(kinds):
            if mode == 'gather':
                src, dst = src_refs[t], _block_view(land_refs[t], kd, my_idx)
            else:
                src, dst = _block_view(src_refs[t], kd, 4 * px + 2 * py + pc), land_refs[t].at[my_idx]
            copies.append(pltpu.make_async_remote_copy(
                src_ref=src, dst_ref=dst, send_sem=send_sems.at[7 * t + k - 1], recv_sem=recv_sems.at[7 * t + k - 1],
                device_id=(px, py, pc), device_id_type=_MESH))
    return copies


def _own_part(mode, kind, src, land):
    idx = 4 * lax.axis_index('x') + 2 * lax.axis_index('y') + lax.axis_index('c')
    zeros = (0,) * (src.ndim - 1)
    if mode == 'gather':
        part = src
    elif kind == 'stack':
        part = lax.dynamic_index_in_dim(src, idx, 0, keepdims=False)
    else:
        r = src.shape[0] // N_DEV
        part = lax.dynamic_slice_in_dim(src, idx * r, r, 0)
    if mode == 'gather' and kind == 'rows':
        return lax.dynamic_update_slice(land, part, (idx * part.shape[0],) + zeros)
    return lax.dynamic_update_slice(land, part[None], (idx,) + (0,) * part.ndim)


def _send_start(mode, srcs, kinds, name, after=None):
    n = len(srcs)
    after = [] if after is None else [after]
    lands = []
    for a, kd in zip(srcs, kinds):
        if mode == 'gather':
            shape = (N_DEV,) + a.shape if kd == 'stack' else (N_DEV * a.shape[0],) + a.shape[1:]
        else:
            shape = a.shape if kd == 'stack' else (N_DEV, a.shape[0] // N_DEV) + a.shape[1:]
        lands.append(_own_part(mode, kd, a, lax.empty(shape, a.dtype)))

    def body(*refs):
        src_refs, land_refs = refs[:n], refs[n:2 * n]
        send_sems, recv_sems = refs[2 * n + len(after):2 * n + len(after) + 2]
        token = refs[-1]
        for cp in _direct_copies(mode, kinds, src_refs, land_refs, send_sems, recv_sems):
            cp.start()
        token[...] = jnp.zeros_like(token)

    outs = pl.pallas_call(
        body, name=name,
        out_shape=(pltpu.SemaphoreType.DMA((7 * n,)), pltpu.SemaphoreType.DMA((7 * n,)),
                   *[pltpu.HBM(a.shape, a.dtype) for a in srcs + lands], SDS((SUB, 128), F32)),
        in_specs=[_HBM] * (2 * n) + [_ANY] * len(after),
        out_specs=(_SEM, _SEM, *[_HBM] * (2 * n), pl.BlockSpec(memory_space=pltpu.VMEM)),
        input_output_aliases={i: 2 + i for i in range(2 * n)},
        compiler_params=pltpu.CompilerParams(has_side_effects=_EFFECT),
    )(*[pltpu.with_memory_space_constraint(a, pltpu.HBM) for a in srcs + lands], *after)
    return (outs[0], outs[1], list(outs[2:2 + 2 * n])), outs[-1]


def _send_wait(mode, flight, kinds, after, name):
    send, recv, bufs = flight
    n = len(kinds)

    def body(*refs):
        src_refs, land_refs, (send_sems, recv_sems) = refs[:n], refs[n:2 * n], refs[2 * n:2 * n + 2]
        copies = _direct_copies(mode, kinds, src_refs, land_refs, send_sems, recv_sems)
        for cp in copies:
            cp.wait_send()
        for cp in copies:
            cp.wait_recv()

    outs = pl.pallas_call(
        body, name=name, out_shape=[pltpu.HBM(a.shape, a.dtype) for a in bufs],
        in_specs=[_HBM] * (2 * n) + [_SEM, _SEM, _ANY], out_specs=[_HBM] * (2 * n),
        input_output_aliases={i: i for i in range(2 * n)},
        compiler_params=pltpu.CompilerParams(has_side_effects=_EFFECT),
    )(*bufs, send, recv, after)
    return list(outs[n:])


ADAMW_BLOCK_ELEMS = 128 * 1024


def _adamw_sum(wgt, parts, m, v, name):
    nl, r, c = wgt.shape
    assert len(parts) == nl and all(q.shape == (N_DEV, r, c) for q in parts), (name, wgt.shape, [q.shape for q in parts])
    tr = next((t for t in range(min(r, 512), 0, -16) if r % t == 0 and t * c <= ADAMW_BLOCK_ELEMS and t % 16 == 0), r)
    c1 = 1.0 - ADAM_B1 ** ADAM_STEP
    c2 = 1.0 - ADAM_B2 ** ADAM_STEP

    def body(w_ref, m_ref, v_ref, *rest):
        part_refs, (g_ref, d_ref, mo_ref, vo_ref) = rest[:nl], rest[nl:]
        for q in range(nl):
            @pl.when(pl.program_id(0) == q)
            def _(q=q):
                gv = part_refs[q][0].astype(F32)
                for s in range(1, N_DEV):
                    gv = gv + part_refs[q][s].astype(F32)
                mn = ADAM_B1 * m_ref[...] + (1.0 - ADAM_B1) * gv
                vn = ADAM_B2 * v_ref[...] + (1.0 - ADAM_B2) * (gv * gv)
                g_ref[...] = gv
                d_ref[...] = -ADAM_LR * ((mn / c1) / (jnp.sqrt(vn / c2) + ADAM_EPS) + ADAM_WD * w_ref[...])
                mo_ref[...] = mn
                vo_ref[...] = vn

    spec = pl.BlockSpec((None, tr, c), lambda l, i: (l, i, 0))
    part_specs = [pl.BlockSpec((N_DEV, tr, c), lambda l, i, q=q: (0, jnp.where(l == q, i, 0), 0)) for q in range(nl)]
    return pl.pallas_call(body, grid=(nl, r // tr), in_specs=[spec] * 3 + part_specs, out_specs=[spec] * 4,
                          out_shape=[SDS((nl, r, c), F32)] * 4, compiler_params=_params('arbitrary', 'arbitrary'),
                          name=name)(wgt, m, v, *parts)


MIXER_WEIGHTS = {0: ('sc_w_in', 'sc_w_out'), 1: ('attn_w_qkv', 'attn_w_o'), 2: ('lru_w_in', 'lru_w_out')}
STACKED_OPERANDS = ('sc_w_in', 'attn_w_qkv', 'mlp_w_up')
LRU_SMALL = ('lru_conv_w', 'lru_conv_b', 'lru_b_a', 'lru_b_x', 'lru_lambda')


def _layer_items(i):
    w_in, w_out = MIXER_WEIGHTS[i % N_MIXERS]
    j = i // N_MIXERS
    return [('w_in', w_in, j), ('w_out', w_out, j), ('mlp_up', 'mlp_w_up', i), ('mlp_down', 'mlp_w_down', i),
            ('ple_gate', 'ple_w_gate', i), ('ple_proj', 'ple_w_proj', i)]


def _cols_to_full(stacked):
    return jnp.moveaxis(stacked, 0, 1).reshape(stacked.shape[1], -1)


def _full_to_cols(full):
    k, n = full.shape
    return jnp.moveaxis(full.reshape(k, N_DEV, n // N_DEV), 1, 0)


def _pad_to(a, rows):
    return jnp.pad(a, ((0, rows - a.shape[0]), (0, 0)))


def _small_block(src, i):
    kind, j = i % N_MIXERS, i // N_MIXERS
    if kind == 0:
        return _pad_to(src['sc_w_conv'][j], SUB)
    if kind == 2:
        return jnp.concatenate([src[n][j].reshape(-1, src[n].shape[-1]) for n in LRU_SMALL], axis=0)
    return None


def kernel(x, p, positions, norm_mix, norm_mlp, norm_ple, norm_final, sc_w_in, sc_w_conv, sc_w_out, attn_w_qkv, attn_w_o, lru_w_in, lru_conv_w, lru_conv_b, lru_w_a, lru_b_a, lru_w_x, lru_b_x, lru_lambda, lru_w_out, mlp_w_up, mlp_w_down, ple_w_gate, ple_w_proj, loss_target, m_norm_mix, m_norm_mlp, m_norm_ple, m_norm_final, m_sc_w_in, m_sc_w_conv, m_sc_w_out, m_attn_w_qkv, m_attn_w_o, m_lru_w_in, m_lru_conv_w, m_lru_conv_b, m_lru_w_a, m_lru_b_a, m_lru_w_x, m_lru_b_x, m_lru_lambda, m_lru_w_out, m_mlp_w_up, m_mlp_w_down, m_ple_w_gate, m_ple_w_proj, v_norm_mix, v_norm_mlp, v_norm_ple, v_norm_final, v_sc_w_in, v_sc_w_conv, v_sc_w_out, v_attn_w_qkv, v_attn_w_o, v_lru_w_in, v_lru_conv_w, v_lru_conv_b, v_lru_w_a, v_lru_b_a, v_lru_w_x, v_lru_b_x, v_lru_lambda, v_lru_w_out, v_mlp_w_up, v_mlp_w_down, v_ple_w_gate, v_ple_w_proj):
    loc = dict(locals())
    shards = {n: loc[n] for n in WEIGHTS}
    moms = {n: loc['m_' + n] for n in WEIGHTS}
    vels = {n: loc['v_' + n] for n in WEIGHTS}

    depth, t, d = p.shape[0], x.shape[1], x.shape[2]

    def comm_kind(name):
        return 'stack' if SHARD_AXIS[name] == 2 else 'rows'

    def layer_shards(i):
        items = _layer_items(i)
        arrs = [shards[n][idx].astype(BF16) for _, n, idx in items]
        kinds = [comm_kind(n) for _, n, _ in items]
        small = _small_block(shards, i)
        if small is not None:
            arrs.append(small)
            kinds.append('stack')
        return items, arrs, kinds

    def layer_weights(i, items, kinds, outs):
        wl = {key: (_cols_to_full(o) if kd == 'stack' and n not in STACKED_OPERANDS else o)
              for (key, n, _), kd, o in zip(items, kinds, outs)}
        if len(outs) > len(items):
            wl['small'] = _cols_to_full(outs[-1])[:shards['sc_w_conv'].shape[1] if i % N_MIXERS == 0 else SUB]
        return wl

    items0, arrs0, kinds0 = layer_shards(0)
    outs0 = _gather_many(arrs0, kinds0, 'gather_weights_0')
    gathers, tokens = {}, []
    for i in range(1, depth):
        items, arrs, kinds = layer_shards(i)
        flight, token = _send_start('gather', arrs, kinds, f'gather_weights_start_{i}', after=outs0[0])
        gathers[i] = (items, kinds, flight)
        tokens.append(token[0, 0])

    def weights_for_layer(i, h):
        if i == 0:
            return layer_weights(0, items0, kinds0, outs0)
        items, kinds, flight = gathers[i]
        return layer_weights(i, items, kinds, _send_wait('gather', flight, kinds, h, f'gather_weights_wait_{i}'))

    exchanges = {}

    def emit_grads(i, gl):
        items = _layer_items(i)
        kinds = [comm_kind(n) for _, n, _ in items]
        arrs = [_full_to_cols(gl[key]) if kd == 'stack' and gl[key].ndim == 2 else gl[key]
                for (key, _, _), kd in zip(items, kinds)]
        if i % N_MIXERS == 0:
            arrs.append(_full_to_cols(_pad_to(gl['small'], SUB)))
        elif i % N_MIXERS == 2:
            dv = gl['small']
            arrs.append(_full_to_cols(jnp.concatenate([dv[4:8], dv[3:4], dv[0:1], dv[1:2], dv[2:3]], axis=0)))
        kinds += ['stack'] * (len(arrs) - len(kinds))
        flight, token = _send_start('exchange', arrs, kinds, f'exchange_grads_start_{i}')
        exchanges[i] = (items, kinds, flight)
        return token[0, 0]

    rep = {n: shards[n] for n in ('norm_mix', 'norm_mlp', 'norm_ple', 'norm_final')}
    rep['norm_mix'] = rep['norm_mix'] + sum(tokens)
    rep['lru_w_a'], rep['lru_w_x'] = shards['lru_w_a'].astype(BF16), shards['lru_w_x'].astype(BF16)
    loss, grad_x, rgrads = _local_step(x.reshape(t, d), p.reshape(depth, t, p.shape[3]), positions.reshape(t, 1),
                                       loss_target.reshape(t, d), rep, weights_for_layer, emit_grads)
    received = {}
    for i, (items, kinds, flight) in exchanges.items():
        outs = _send_wait('exchange', flight, kinds, grad_x, f'exchange_grads_wait_{i}')
        for (_, n, idx), o in zip(items, outs):
            received[(n, idx)] = o
        if len(outs) > len(items):
            received[('small', i)] = outs[-1]

    norm_names = ('norm_mix', 'norm_mlp', 'norm_ple', 'norm_final')
    gate_names = ('lru_w_a', 'lru_w_x')

    def norm_block(src):
        cat = jnp.concatenate([src[n].reshape(-1, d) for n in norm_names], axis=0)
        return _pad_to(cat, -(-cat.shape[0] // HALO) * HALO)

    def gate_block(src):
        return jnp.concatenate([src[n].reshape(-1, LRU_BLOCK) for n in gate_names], axis=0)

    rfull = {n: (rgrads[n] if n == 'norm_final' else jnp.stack(rgrads[n], axis=0)) for n in norm_names + gate_names}
    parts_norm, parts_gate = _gather_many([norm_block(rfull), gate_block(rfull)], ['stack', 'stack'],
                                          'gather_replicated_grads')

    res = {}
    for n in WEIGHTS:
        if SHARD_AXIS[n] is not None and shards[n].ndim == 3 and n not in ('sc_w_conv', 'lru_conv_w'):
            res[n] = _adamw_sum(shards[n], [received[(n, l)] for l in range(shards[n].shape[0])], moms[n], vels[n],
                                f'adamw_{n}')
    def small_adamw(layers, name):
        w_, m_, v_ = (jnp.stack([_small_block(src, i) for i in layers]) for src in (shards, moms, vels))
        return _adamw_sum(w_, [received[('small', i)] for i in layers], m_, v_, name)

    sc = small_adamw([i for i in range(depth) if i % N_MIXERS == 0], 'adamw_sc_w_conv')
    res['sc_w_conv'] = tuple(o[:, :shards['sc_w_conv'].shape[1]] for o in sc)
    lru = small_adamw([i for i in range(depth) if i % N_MIXERS == 2], 'adamw_lru_small')
    row = 0
    for n in LRU_SMALL:
        k = shards[n].size // shards[n].shape[0] // shards[n].shape[-1]
        res[n] = tuple(o[:, row:row + k].reshape(shards[n].shape) for o in lru)
        row += k
    norms = _adamw_sum(norm_block(shards)[None], [parts_norm], norm_block(moms)[None], norm_block(vels)[None],
                       'adamw_norms')
    gates = _adamw_sum(gate_block(shards)[None], [parts_gate], gate_block(moms)[None], gate_block(vels)[None],
                       'adamw_lru_gates')
    for names, outs in ((norm_names, norms), (gate_names, gates)):
        row = 0
        for n in names:
            k = shards[n].size // outs[0].shape[-1]
            res[n] = tuple(o[0, row:row + k].reshape(shards[n].shape) for o in outs)
            row += k

    loss = lax.psum(loss[0, 0], ('x', 'y', 'c'))
    return (loss, grad_x.reshape(x.shape), *[res[n][0] for n in WEIGHTS], *[res[n][1] for n in WEIGHTS],
            *[res[n][2] for n in WEIGHTS], *[res[n][3] for n in WEIGHTS])
```

```python
import functools
import math

import jax
import jax.numpy as jnp
from jax import lax
from jax.experimental import pallas as pl
from jax.experimental.pallas import tpu as pltpu

F32 = jnp.float32
BF16 = jnp.bfloat16
SDS = jax.ShapeDtypeStruct

N_DEV = 8
RMS_EPS = 1e-6
N_MIXERS = 3
HEAD_DIM = 128
DILATED_PATTERNS = ((128, 1), (512, 4), (2048, 16))
ATTN_BLOCK = 128
ROPE_THETA = 500000.0
ROPE_DIM = HEAD_DIM // 4
LRU_BLOCK = 128
LRU_C = 8.0
ADAM_LR, ADAM_B1, ADAM_B2, ADAM_EPS, ADAM_WD, ADAM_STEP = 0.001, 0.9, 0.999, 1e-08, 0.01, 10

HALO = 16
SUB = 8
VMEM_LIMIT = 56 * 1024 * 1024
NEG = -1e30

SHARD_AXIS = {
    'norm_mix': None, 'norm_mlp': None, 'norm_ple': None, 'norm_final': None,
    'sc_w_in': 2, 'sc_w_conv': 2, 'sc_w_out': 1, 'attn_w_qkv': 2, 'attn_w_o': 1,
    'lru_w_in': 2, 'lru_conv_w': 2, 'lru_conv_b': 1, 'lru_w_a': None, 'lru_b_a': 1,
    'lru_w_x': None, 'lru_b_x': 1, 'lru_lambda': 1, 'lru_w_out': 1,
    'mlp_w_up': 2, 'mlp_w_down': 1, 'ple_w_gate': 1, 'ple_w_proj': 2,
}
WEIGHTS = list(SHARD_AXIS)


def _params(*sem):
    return pltpu.CompilerParams(dimension_semantics=sem or None, vmem_limit_bytes=VMEM_LIMIT)


def _row_tile(t, pref=256):
    tr = min(t, pref)
    assert t % tr == 0 and tr % HALO == 0
    return tr


def _row(tr, c, col=0):
    return pl.BlockSpec((tr, c), lambda i, col=col: (i, col))


def _full(shape):
    return pl.BlockSpec(shape, lambda *_: (0,) * len(shape))


def _sigmoid(x):
    return 1.0 / (1.0 + jnp.exp(-x))


def _expm1(x):
    taylor = x * (1.0 + x * (0.5 + x * (1.0 / 6.0 + x * (1.0 / 24.0 + x * (1.0 / 120.0)))))
    return jnp.where(jnp.abs(x) < 0.1, taylor, jnp.exp(x) - 1.0)


def _softplus(x):
    z = jnp.exp(-jnp.abs(x))
    log1p = jnp.where(z < 0.01, z * (1.0 - z * (0.5 - z * (1.0 / 3.0 - z * 0.25))), jnp.log(1.0 + z))
    return jnp.maximum(x, 0.0) + log1p


_GELU_K = math.sqrt(2.0 / math.pi)


def _gelu_and_grad(x):
    inner = _GELU_K * (x + 0.044715 * x * x * x)
    th = jnp.tanh(inner)
    g = 0.5 * x * (1.0 + th)
    dg = 0.5 * (1.0 + th) + 0.5 * x * (1.0 - th * th) * _GELU_K * (1.0 + 3.0 * 0.044715 * x * x)
    return g, dg


def _shift_down(x, k, prev):
    row = lax.broadcasted_iota(jnp.int32, (SUB, x.shape[1]), 0)
    xr = pltpu.roll(x, k, 0)
    top = jnp.where(row < k, pltpu.roll(prev, k, 0), xr[0:SUB])
    return jnp.concatenate([top, xr[SUB:]], axis=0)


def _shift_up(x, k, nxt):
    r = x.shape[0]
    row = lax.broadcasted_iota(jnp.int32, (SUB, x.shape[1]), 0)
    xr = pltpu.roll(x, r - k, 0)
    bot = jnp.where(row >= SUB - k, pltpu.roll(nxt, SUB - k, 0), xr[r - SUB:r])
    return jnp.concatenate([xr[:r - SUB], bot], axis=0)


_DIMS = {'nn': (((1,), (0,)), ((), ())), 'nt': (((1,), (1,)), ((), ())), 'tn': (((0,), (0,)), ((), ()))}


def _pick_tile(dim, pref):
    if dim <= pref:
        return dim
    return next(c for c in range(pref - pref % 128, 0, -128) if dim % c == 0)


def _mm(a, b, dims, name, out_dtypes=(F32,), a_pro=None, extras=(), epi=None, tm=1024, tn=1024, tk=512,
        out_stacked=False, dep=None):
    deps = [] if dep is None else [dep]
    stacked = b.ndim == 3
    b_rows, b_cols = (b.shape[1], N_DEV * b.shape[2]) if stacked else b.shape
    if dims == 'nn':
        (m, k), (k2, n) = a.shape, (b_rows, b_cols)
    elif dims == 'nt':
        (m, k), (n, k2) = a.shape, (b_rows, b_cols)
    else:
        (k, m), (k2, n) = a.shape, (b_rows, b_cols)
    assert k == k2, (name, a.shape, b.shape)
    assert not (stacked and dims == 'tn') and not (out_stacked and (extras or dims != 'tn'))
    tm = _pick_tile(m, tm)
    tn = _pick_tile(n // N_DEV if (out_stacked or (stacked and dims == 'nn')) else n, tn)
    tk = _pick_tile(k // N_DEV if (stacked and dims == 'nt') else k, tk)
    assert m % tm == 0 and n % tn == 0 and k % tk == 0, (name, m, n, k)
    nk = k // tk
    a_spec = pl.BlockSpec((tk, tm), lambda i, j, kk: (kk, i)) if dims == 'tn' else pl.BlockSpec((tm, tk), lambda i, j, kk: (i, kk))
    if not stacked:
        b_spec = pl.BlockSpec((tn, tk), lambda i, j, kk: (j, kk)) if dims == 'nt' else pl.BlockSpec((tk, tn), lambda i, j, kk: (kk, j))
    elif dims == 'nn':
        per = b.shape[2] // tn
        b_spec = pl.BlockSpec((None, tk, tn), lambda i, j, kk: (j // per, kk, j % per))
    else:
        per = b.shape[2] // tk
        b_spec = pl.BlockSpec((None, tn, tk), lambda i, j, kk: (kk // per, j, kk % per))
    if out_stacked:
        per_o = n // N_DEV // tn
        o_spec = pl.BlockSpec((None, tm, tn), lambda i, j, kk: (j // per_o, i, j % per_o))
        o_shape = (N_DEV, m, n // N_DEV)
    else:
        o_spec = pl.BlockSpec((tm, tn), lambda i, j, kk: (i, j))
        o_shape = (m, n)
    n_ex, n_out = len(extras), len(out_dtypes)
    for e in extras:
        assert e.shape == (m, n), (name, e.shape)

    def body(a_ref, b_ref, *rest):
        rest = rest[len(deps):]
        ex_refs, out_refs, acc = rest[:n_ex], rest[n_ex:n_ex + n_out], rest[-1]
        kk = pl.program_id(2)

        @pl.when(kk == 0)
        def _():
            acc[...] = jnp.zeros_like(acc)

        av = a_ref[...]
        if a_pro is not None:
            av = a_pro(av.astype(F32))
        acc[...] += lax.dot_general(av.astype(BF16), b_ref[...].astype(BF16), _DIMS[dims],
                                    preferred_element_type=F32)

        @pl.when(kk == nk - 1)
        def _():
            res = acc[...]
            outs = (res,) if epi is None else epi(res, *[e[...] for e in ex_refs])
            for o_ref, o in zip(out_refs, outs):
                o_ref[...] = o.astype(o_ref.dtype)

    out = pl.pallas_call(
        body, grid=(m // tm, n // tn, nk),
        in_specs=[a_spec, b_spec] + [_ANY] * len(deps) + [o_spec] * n_ex,
        out_specs=[o_spec] * n_out,
        out_shape=[SDS(o_shape, d) for d in out_dtypes],
        scratch_shapes=[pltpu.VMEM((tm, tn), F32)],
        compiler_params=_params('parallel', 'parallel', 'arbitrary'), name=name)(a, b, *deps, *extras)
    return out[0] if n_out == 1 else out


def _relu2(u):
    r = jnp.maximum(u, 0.0)
    return r * r


def _rms_fwd(h, g, name):
    t, d = h.shape
    tr = _row_tile(t)

    def body(h_ref, g_ref, o_ref):
        x = h_ref[...]
        r = lax.rsqrt(jnp.mean(x * x, axis=-1, keepdims=True) + RMS_EPS)
        o_ref[...] = (x * r * g_ref[...]).astype(o_ref.dtype)

    return pl.pallas_call(body, grid=(t // tr,), in_specs=[_row(tr, d), _full((1, d))], out_specs=_row(tr, d),
                          out_shape=SDS((t, d), BF16), compiler_params=_params('parallel'), name=name)(h, g.reshape(1, d))


def _rms_bwd(h, g, dhn, dres, name):
    t, d = h.shape
    tr = _row_tile(t)

    def body(h_ref, g_ref, dhn_ref, dres_ref, dh_ref, dg_ref):
        @pl.when(pl.program_id(0) == 0)
        def _():
            dg_ref[...] = jnp.zeros_like(dg_ref)

        x = h_ref[...]
        r = lax.rsqrt(jnp.mean(x * x, axis=-1, keepdims=True) + RMS_EPS)
        dy = dhn_ref[...].astype(F32)
        gy = dy * g_ref[...]
        dx = r * gy - x * (r * r * r) * jnp.mean(gy * x, axis=-1, keepdims=True)
        dh_ref[...] = dres_ref[...] + dx
        dg_ref[...] += jnp.sum(dy * (x * r), axis=0, keepdims=True)

    return pl.pallas_call(body, grid=(t // tr,),
                          in_specs=[_row(tr, d), _full((1, d)), _row(tr, d), _row(tr, d)],
                          out_specs=[_row(tr, d), _full((1, d))],
                          out_shape=[SDS((t, d), F32), SDS((1, d), F32)],
                          compiler_params=_params('arbitrary'), name=name)(h, g.reshape(1, d), dhn, dres)


def _head(h, g, target, name):
    t, d = h.shape
    tr = _row_tile(t)

    def body(h_ref, g_ref, t_ref, dh_ref, loss_ref, dg_ref):
        @pl.when(pl.program_id(0) == 0)
        def _():
            dg_ref[...] = jnp.zeros_like(dg_ref)
            loss_ref[...] = jnp.zeros_like(loss_ref)

        x = h_ref[...]
        gv = g_ref[...]
        r = lax.rsqrt(jnp.mean(x * x, axis=-1, keepdims=True) + RMS_EPS)
        xh = x * r
        e = xh * gv - t_ref[...]
        per_tok = jnp.mean(e * e, axis=-1, keepdims=True)
        loss_ref[...] += jnp.broadcast_to(0.5 * jnp.sum(per_tok, axis=0, keepdims=True), loss_ref.shape)
        dy = e * (1.0 / d)
        gy = dy * gv
        dh_ref[...] = r * gy - x * (r * r * r) * jnp.mean(gy * x, axis=-1, keepdims=True)
        dg_ref[...] += jnp.sum(dy * xh, axis=0, keepdims=True)

    return pl.pallas_call(body, grid=(t // tr,),
                          in_specs=[_row(tr, d), _full((1, d)), _row(tr, d)],
                          out_specs=[_row(tr, d), _full((1, 128)), _full((1, d))],
                          out_shape=[SDS((t, d), F32), SDS((1, 128), F32), SDS((1, d), F32)],
                          compiler_params=_params('arbitrary'), name=name)(h, g.reshape(1, d), target)


def _ple_bwd_gate(dh3, gate, pp, name):
    t, d = dh3.shape
    tr = _row_tile(t)

    def body(dh_ref, g_ref, pp_ref, dpp_ref, dgl_ref):
        dh = dh_ref[...]
        gt = g_ref[...]
        dpp_ref[...] = (dh * gt).astype(dpp_ref.dtype)
        dgl_ref[...] = (dh * pp_ref[...] * gt * (1.0 - gt)).astype(dgl_ref.dtype)

    return pl.pallas_call(body, grid=(t // tr,), in_specs=[_row(tr, d)] * 3, out_specs=[_row(tr, d)] * 2,
                          out_shape=[SDS((t, d), BF16), SDS((t, d), BF16)],
                          compiler_params=_params('parallel'), name=name)(dh3, gate, pp)


def _halo_prev(tr, c, col=0):
    return pl.BlockSpec((HALO, c), lambda i, col=col: (jnp.maximum(i * (tr // HALO) - 1, 0), col))


def _halo_next(tr, c, t, col=0):
    return pl.BlockSpec((HALO, c), lambda i, col=col: (jnp.minimum((i + 1) * (tr // HALO), t // HALO - 1), col))


def _sc_fwd(z, w, name):
    t, c3 = z.shape
    c = c3 // 3
    tr = _row_tile(t)

    def body(z_ref, zp_ref, w_ref, y_ref):
        i = pl.program_id(0)
        zz = z_ref[...]
        gb, cx = zz[:, :c], zz[:, c:2 * c] * zz[:, 2 * c:]
        zp = zp_ref[SUB:HALO, :]
        cxp = jnp.where(i > 0, zp[:, c:2 * c] * zp[:, 2 * c:], 0.0)
        wv = w_ref[...]
        conv = wv[2:3] * cx + wv[1:2] * _shift_down(cx, 1, cxp) + wv[0:1] * _shift_down(cx, 2, cxp)
        y_ref[...] = (gb * conv).astype(y_ref.dtype)

    return pl.pallas_call(body, grid=(t // tr,),
                          in_specs=[_row(tr, c3), _halo_prev(tr, c3), _full((3, c))],
                          out_specs=_row(tr, c), out_shape=SDS((t, c), BF16),
                          compiler_params=_params('parallel'), name=name)(z, z, w)


def _sc_bwd(dy, z, w, name):
    t, c3 = z.shape
    c = c3 // 3
    tr = _row_tile(t)
    nt = t // tr

    def body(dy_ref, dyn_ref, z_ref, zp_ref, zn_ref, w_ref, dz_ref, dw_ref):
        i = pl.program_id(0)

        @pl.when(i == 0)
        def _():
            dw_ref[...] = jnp.zeros_like(dw_ref)

        zz = z_ref[...]
        gb, gc, xi = zz[:, :c], zz[:, c:2 * c], zz[:, 2 * c:]
        cx = gc * xi
        zp = zp_ref[SUB:HALO, :]
        cxp = jnp.where(i > 0, zp[:, c:2 * c] * zp[:, 2 * c:], 0.0)
        wv = w_ref[...]
        cx1, cx2 = _shift_down(cx, 1, cxp), _shift_down(cx, 2, cxp)
        conv = wv[2:3] * cx + wv[1:2] * cx1 + wv[0:1] * cx2
        dyv = dy_ref[...]
        dconv = dyv * gb
        dcn = jnp.where(i < nt - 1, dyn_ref[0:SUB, :] * zn_ref[0:SUB, :c], 0.0)
        dcx = wv[2:3] * dconv + wv[1:2] * _shift_up(dconv, 1, dcn) + wv[0:1] * _shift_up(dconv, 2, dcn)
        dz_ref[:, :c] = (dyv * conv).astype(dz_ref.dtype)
        dz_ref[:, c:2 * c] = (dcx * xi).astype(dz_ref.dtype)
        dz_ref[:, 2 * c:] = (dcx * gc).astype(dz_ref.dtype)
        dw_ref[...] += jnp.concatenate([jnp.sum(dconv * cx2, axis=0, keepdims=True),
                                        jnp.sum(dconv * cx1, axis=0, keepdims=True),
                                        jnp.sum(dconv * cx, axis=0, keepdims=True)], axis=0)

    return pl.pallas_call(body, grid=(nt,),
                          in_specs=[_row(tr, c), _halo_next(tr, c, t), _row(tr, c3), _halo_prev(tr, c3),
                                    _halo_next(tr, c3, t), _full((3, c))],
                          out_specs=[_row(tr, c3), _full((3, c))],
                          out_shape=[SDS((t, c3), BF16), SDS((3, c), F32)],
                          compiler_params=_params('arbitrary'), name=name)(dy, dy, z, z, z, w)


def _rope_tables(pos_ref, invf_ref, sign):
    lane = lax.broadcasted_iota(jnp.int32, (pos_ref.shape[0], HEAD_DIM), 1)
    ang = pos_ref[...].astype(F32) * invf_ref[...]
    half = ROPE_DIM // 2
    cos = jnp.where(lane < ROPE_DIM, jnp.cos(ang), 1.0)
    sin = jnp.sin(ang) * sign
    sin = jnp.where(lane < half, -sin, jnp.where(lane < ROPE_DIM, sin, 0.0))
    return lane, cos, sin


def _rope_apply(x, lane, cos, sin):
    half = ROPE_DIM // 2
    xs = jnp.where(lane < half, pltpu.roll(x, HEAD_DIM - half, 1), pltpu.roll(x, half, 1))
    return x * cos + xs * sin


def _rope_fwd(qkv, pos, invf, name):
    t, w3 = qkv.shape
    w = w3 // 3
    tr = _row_tile(t)

    def body(q_ref, k_ref, v_ref, pos_ref, invf_ref, qo_ref, ko_ref, vo_ref):
        lane, cos, sin = _rope_tables(pos_ref, invf_ref, 1.0)
        for hh in range(w // HEAD_DIM):
            cs = slice(hh * HEAD_DIM, (hh + 1) * HEAD_DIM)
            qo_ref[:, cs] = _rope_apply(q_ref[:, cs], lane, cos, sin).astype(qo_ref.dtype)
            ko_ref[:, cs] = _rope_apply(k_ref[:, cs], lane, cos, sin).astype(ko_ref.dtype)
        vo_ref[...] = v_ref[...].astype(vo_ref.dtype)

    return pl.pallas_call(body, grid=(t // tr,),
                          in_specs=[_row(tr, w, 0), _row(tr, w, 1), _row(tr, w, 2), _row(tr, 1), _full((1, HEAD_DIM))],
                          out_specs=[_row(tr, w)] * 3, out_shape=[SDS((t, w), BF16)] * 3,
                          compiler_params=_params('parallel'), name=name)(qkv, qkv, qkv, pos, invf)


def _rope_bwd(dqs, dks, dvs, pos, invf, name):
    t, d = dqs[0].shape
    ng = len(dqs)
    w = ng * d
    tr = _row_tile(t)

    def body(*refs):
        dq_refs, dk_refs, dv_refs = refs[:ng], refs[ng:2 * ng], refs[2 * ng:3 * ng]
        pos_ref, invf_ref, o_ref = refs[3 * ng], refs[3 * ng + 1], refs[3 * ng + 2]
        lane, cos, sin = _rope_tables(pos_ref, invf_ref, -1.0)
        for g in range(ng):
            for hh in range(d // HEAD_DIM):
                cs = slice(hh * HEAD_DIM, (hh + 1) * HEAD_DIM)
                base = g * d + hh * HEAD_DIM
                o_ref[:, base:base + HEAD_DIM] = _rope_apply(dq_refs[g][:, cs], lane, cos, sin).astype(o_ref.dtype)
                o_ref[:, w + base:w + base + HEAD_DIM] = _rope_apply(dk_refs[g][:, cs], lane, cos, sin).astype(o_ref.dtype)
            o_ref[:, 2 * w + g * d:2 * w + (g + 1) * d] = dv_refs[g][...].astype(o_ref.dtype)

    return pl.pallas_call(body, grid=(t // tr,),
                          in_specs=[_row(tr, d)] * (3 * ng) + [_row(tr, 1), _full((1, HEAD_DIM))],
                          out_specs=_row(tr, 3 * w), out_shape=SDS((t, 3 * w), BF16),
                          compiler_params=_params('parallel'), name=name)(*dqs, *dks, *dvs, pos, invf)


def _attn_masks():
    qi = lax.broadcasted_iota(jnp.int32, (ATTN_BLOCK, ATTN_BLOCK), 0)
    kj = lax.broadcasted_iota(jnp.int32, (ATTN_BLOCK, ATTN_BLOCK), 1)
    return kj >= qi, kj <= qi


def _attn_cols(l, width):
    ncol = width // HEAD_DIM
    cpb = max(1, min(ncol, 32 // (l // ATTN_BLOCK)))
    assert ncol % cpb == 0
    return cpb


def _attn_fwd(q, k, v, name):
    l, width = q.shape
    cpb = _attn_cols(l, width)
    nb = l // ATTN_BLOCK
    scale = HEAD_DIM ** -0.5

    def body(q_ref, k_ref, v_ref, o_ref, lse_ref):
        m_prev, m_cur = _attn_masks()
        for col in range(cpb):
            cs = slice(col * HEAD_DIM, (col + 1) * HEAD_DIM)

            def step(b, carry, cs=cs):
                r0 = pl.multiple_of(b * ATTN_BLOCK, ATTN_BLOCK)
                rp = pl.multiple_of(jnp.maximum(b - 1, 0) * ATTN_BLOCK, ATTN_BLOCK)
                qb = q_ref[pl.ds(r0, ATTN_BLOCK), cs]
                s_p = lax.dot_general(qb, k_ref[pl.ds(rp, ATTN_BLOCK), cs], _DIMS['nt'], preferred_element_type=F32) * scale
                s_c = lax.dot_general(qb, k_ref[pl.ds(r0, ATTN_BLOCK), cs], _DIMS['nt'], preferred_element_type=F32) * scale
                s_p = jnp.where(jnp.logical_and(m_prev, b > 0), s_p, NEG)
                s_c = jnp.where(m_cur, s_c, NEG)
                m = jnp.maximum(jnp.max(s_p, axis=-1, keepdims=True), jnp.max(s_c, axis=-1, keepdims=True))
                p_p, p_c = jnp.exp(s_p - m), jnp.exp(s_c - m)
                den = jnp.sum(p_p, axis=-1, keepdims=True) + jnp.sum(p_c, axis=-1, keepdims=True)
                acc = jnp.dot(p_p.astype(BF16), v_ref[pl.ds(rp, ATTN_BLOCK), cs], preferred_element_type=F32)
                acc += jnp.dot(p_c.astype(BF16), v_ref[pl.ds(r0, ATTN_BLOCK), cs], preferred_element_type=F32)
                o_ref[pl.ds(r0, ATTN_BLOCK), cs] = acc / den
                lse_ref[pl.ds(r0, ATTN_BLOCK), cs] = jnp.broadcast_to(m + jnp.log(den), (ATTN_BLOCK, HEAD_DIM))
                return carry

            lax.fori_loop(0, nb, step, 0)

    spec = pl.BlockSpec((l, cpb * HEAD_DIM), lambda j: (0, j))
    return pl.pallas_call(body, grid=(width // (cpb * HEAD_DIM),), in_specs=[spec] * 3, out_specs=[spec] * 2,
                          out_shape=[SDS((l, width), F32)] * 2,
                          compiler_params=_params('parallel'), name=name)(q, k, v)


def _attn_bwd(q, k, v, do, lse, delta, name):
    l, width = q.shape
    cpb = _attn_cols(l, width)
    nb = l // ATTN_BLOCK
    scale = HEAD_DIM ** -0.5

    def body(q_ref, k_ref, v_ref, do_ref, lse_ref, dl_ref, dq_ref, dk_ref, dv_ref):
        m_prev, m_cur = _attn_masks()
        dk_ref[...] = jnp.zeros_like(dk_ref)
        dv_ref[...] = jnp.zeros_like(dv_ref)
        for col in range(cpb):
            cs = slice(col * HEAD_DIM, (col + 1) * HEAD_DIM)

            def step(b, carry, cs=cs):
                r0 = pl.multiple_of(b * ATTN_BLOCK, ATTN_BLOCK)
                rp = pl.multiple_of(jnp.maximum(b - 1, 0) * ATTN_BLOCK, ATTN_BLOCK)
                qb, dob = q_ref[pl.ds(r0, ATTN_BLOCK), cs], do_ref[pl.ds(r0, ATTN_BLOCK), cs]
                kp, kc = k_ref[pl.ds(rp, ATTN_BLOCK), cs], k_ref[pl.ds(r0, ATTN_BLOCK), cs]
                vp, vc = v_ref[pl.ds(rp, ATTN_BLOCK), cs], v_ref[pl.ds(r0, ATTN_BLOCK), cs]
                lse_b = lse_ref[pl.ds(r0, ATTN_BLOCK), cs]
                dl_b = dl_ref[pl.ds(r0, ATTN_BLOCK), cs]
                s_p = lax.dot_general(qb, kp, _DIMS['nt'], preferred_element_type=F32) * scale
                s_c = lax.dot_general(qb, kc, _DIMS['nt'], preferred_element_type=F32) * scale
                p_p = jnp.exp(jnp.where(jnp.logical_and(m_prev, b > 0), s_p, NEG) - lse_b)
                p_c = jnp.exp(jnp.where(m_cur, s_c, NEG) - lse_b)
                dp_p = lax.dot_general(dob, vp, _DIMS['nt'], preferred_element_type=F32)
                dp_c = lax.dot_general(dob, vc, _DIMS['nt'], preferred_element_type=F32)
                ds_p = (p_p * (dp_p - dl_b) * scale).astype(BF16)
                ds_c = (p_c * (dp_c - dl_b) * scale).astype(BF16)
                dq_ref[pl.ds(r0, ATTN_BLOCK), cs] = (jnp.dot(ds_p, kp, preferred_element_type=F32)
                                                     + jnp.dot(ds_c, kc, preferred_element_type=F32))
                dk_ref[pl.ds(rp, ATTN_BLOCK), cs] += lax.dot_general(ds_p, qb, _DIMS['tn'], preferred_element_type=F32)
                dk_ref[pl.ds(r0, ATTN_BLOCK), cs] += lax.dot_general(ds_c, qb, _DIMS['tn'], preferred_element_type=F32)
                dv_ref[pl.ds(rp, ATTN_BLOCK), cs] += lax.dot_general(p_p.astype(BF16), dob, _DIMS['tn'], preferred_element_type=F32)
                dv_ref[pl.ds(r0, ATTN_BLOCK), cs] += lax.dot_general(p_c.astype(BF16), dob, _DIMS['tn'], preferred_element_type=F32)
                return carry

            lax.fori_loop(0, nb, step, 0)

    spec = pl.BlockSpec((l, cpb * HEAD_DIM), lambda j: (0, j))
    return pl.pallas_call(body, grid=(width // (cpb * HEAD_DIM),), in_specs=[spec] * 6, out_specs=[spec] * 3,
                          out_shape=[SDS((l, width), F32)] * 3,
                          compiler_params=_params('parallel'), name=name)(q, k, v, do, lse, delta)


def _attn_combine(os_, lses, name):
    t, d = os_[0].shape
    ng = len(os_)
    tr = _row_tile(t)

    def body(*refs):
        o_refs, l_refs, o_out, lse_out = refs[:ng], refs[ng:2 * ng], refs[2 * ng], refs[2 * ng + 1]
        ls = [r[...] for r in l_refs]
        m = functools.reduce(jnp.maximum, ls)
        ws = [jnp.exp(x - m) for x in ls]
        den = functools.reduce(lambda a, b: a + b, ws)
        acc = functools.reduce(lambda a, b: a + b, [w * o[...] for w, o in zip(ws, o_refs)])
        o_out[...] = (acc / den).astype(o_out.dtype)
        lse_out[...] = m + jnp.log(den)

    return pl.pallas_call(body, grid=(t // tr,), in_specs=[_row(tr, d)] * (2 * ng), out_specs=[_row(tr, d)] * 2,
                          out_shape=[SDS((t, d), BF16), SDS((t, d), F32)],
                          compiler_params=_params('parallel'), name=name)(*os_, *lses)


def _delta_epilogue(acc, o):
    prod = acc * o.astype(F32)
    segs = [jnp.broadcast_to(jnp.sum(prod[:, s:s + HEAD_DIM], axis=-1, keepdims=True), (acc.shape[0], HEAD_DIM))
            for s in range(0, acc.shape[1], HEAD_DIM)]
    return acc, jnp.concatenate(segs, axis=-1)


LRU_TILE = 128


def _lru_gates(xr, wa_ref, ba, wx_ref, bx, lam):
    nb = wa_ref.shape[0]
    xb = xr.astype(BF16)
    ra = jnp.concatenate([jnp.dot(xb[:, n * LRU_BLOCK:(n + 1) * LRU_BLOCK], wa_ref[n], preferred_element_type=F32)
                          for n in range(nb)], axis=-1) + ba
    ia = jnp.concatenate([jnp.dot(xb[:, n * LRU_BLOCK:(n + 1) * LRU_BLOCK], wx_ref[n], preferred_element_type=F32)
                          for n in range(nb)], axis=-1) + bx
    r, ig = _sigmoid(ra), _sigmoid(ia)
    sp = _softplus(-lam)
    log_a = -LRU_C * r * sp
    a = jnp.exp(log_a)
    mult = jnp.sqrt(-_expm1(2.0 * log_a))
    return xb, r, ig, sp, a, mult


def _lru_fwd(z, cw, cb, wa, ba, wx, bx, lam, name):
    t, c2 = z.shape
    c = c2 // 2
    nb = c // LRU_BLOCK
    tr = _row_tile(t, LRU_TILE)

    def body(g_ref, x_ref, xp_ref, cw_ref, cb_ref, wa_ref, ba_ref, wx_ref, bx_ref, lam_ref,
             y_ref, hs_ref, xr_ref, car_ref):
        i = pl.program_id(0)

        @pl.when(i == 0)
        def _():
            car_ref[...] = jnp.zeros_like(car_ref)

        x0 = x_ref[...]
        xp = jnp.where(i > 0, xp_ref[SUB:HALO, :], 0.0)
        cwv = cw_ref[...]
        xr = (cb_ref[...] + cwv[3:4] * x0 + cwv[2:3] * _shift_down(x0, 1, xp)
              + cwv[1:2] * _shift_down(x0, 2, xp) + cwv[0:1] * _shift_down(x0, 3, xp))
        xr_ref[...] = xr
        _, _, ig, _, a, mult = _lru_gates(xr, wa_ref, ba_ref[...], wx_ref, bx_ref[...], lam_ref[...])
        u = mult * (ig * xr)
        row = lax.broadcasted_iota(jnp.int32, (SUB, c), 0)
        car = car_ref[...]
        for j in range(tr // SUB):
            ab, ub = a[j * SUB:(j + 1) * SUB], u[j * SUB:(j + 1) * SUB]
            for s in (1, 2, 4):
                a_sh = jnp.where(row >= s, pltpu.roll(ab, s, 0), 1.0)
                u_sh = jnp.where(row >= s, pltpu.roll(ub, s, 0), 0.0)
                ub = ab * u_sh + ub
                ab = ab * a_sh
            hb = ub + ab * car
            hs_ref[j * SUB:(j + 1) * SUB, :] = hb
            car = jnp.broadcast_to(hb[SUB - 1:SUB], (SUB, c))
        car_ref[...] = car
        gl, _ = _gelu_and_grad(g_ref[...])
        y_ref[...] = (hs_ref[...] * gl).astype(y_ref.dtype)

    return pl.pallas_call(
        body, grid=(t // tr,),
        in_specs=[_row(tr, c, 0), _row(tr, c, 1), _halo_prev(tr, c, 1), _full((4, c)), _full((1, c)),
                  _full((nb, LRU_BLOCK, LRU_BLOCK)), _full((1, c)), _full((nb, LRU_BLOCK, LRU_BLOCK)), _full((1, c)), _full((1, c))],
        out_specs=[_row(tr, c)] * 3,
        out_shape=[SDS((t, c), BF16), SDS((t, c), F32), SDS((t, c), F32)],
        scratch_shapes=[pltpu.VMEM((SUB, c), F32)],
        compiler_params=_params('arbitrary'), name=name)(
            z, z, z, cw, cb.reshape(1, c), wa, ba.reshape(1, c), wx, bx.reshape(1, c), lam.reshape(1, c))


def _lru_bwd(dy, z, xr, hs, cw, wa, ba, wx, bx, lam, name):
    t, c2 = z.shape
    c = c2 // 2
    nb = c // LRU_BLOCK
    tr = _row_tile(t, LRU_TILE)
    nt = t // tr

    def rev(col=0):
        return pl.BlockSpec((tr, c), lambda i, col=col: (nt - 1 - i, col))

    def rev_prev(col=0):
        return pl.BlockSpec((HALO, c), lambda i, col=col: (jnp.maximum((nt - 1 - i) * (tr // HALO) - 1, 0), col))

    def body(dy_ref, g_ref, x_ref, xp_ref, xr_ref, hs_ref, hp_ref, cw_ref, wa_ref, ba_ref, wx_ref, bx_ref, lam_ref,
             dz_ref, dwa_ref, dwx_ref, dvec_ref, lcar_ref, ahead_ref, dxhead_ref, lam_s):
        i = pl.program_id(0)
        first_tile = i == nt - 1

        @pl.when(i == 0)
        def _():
            lcar_ref[...] = jnp.zeros_like(lcar_ref)
            ahead_ref[...] = jnp.zeros_like(ahead_ref)
            dxhead_ref[...] = jnp.zeros_like(dxhead_ref)
            dwa_ref[...] = jnp.zeros_like(dwa_ref)
            dwx_ref[...] = jnp.zeros_like(dwx_ref)
            dvec_ref[...] = jnp.zeros_like(dvec_ref)

        xrv = xr_ref[...]
        lamv = lam_ref[...]
        xb, r, ig, sp, a, mult = _lru_gates(xrv, wa_ref, ba_ref[...], wx_ref, bx_ref[...], lamv)
        hsv = hs_ref[...]
        dyv = dy_ref[...]
        gl, dgl = _gelu_and_grad(g_ref[...])
        dhs = dyv * gl
        dz_ref[:, :c] = (dyv * hsv * dgl).astype(dz_ref.dtype)

        a_next = _shift_up(a, 1, ahead_ref[...])
        row = lax.broadcasted_iota(jnp.int32, (SUB, c), 0)
        car = lcar_ref[...]
        for j in reversed(range(tr // SUB)):
            ab, ub = a_next[j * SUB:(j + 1) * SUB], dhs[j * SUB:(j + 1) * SUB]
            for s in (1, 2, 4):
                a_sh = jnp.where(row < SUB - s, pltpu.roll(ab, SUB - s, 0), 1.0)
                u_sh = jnp.where(row < SUB - s, pltpu.roll(ub, SUB - s, 0), 0.0)
                ub = ab * u_sh + ub
                ab = ab * a_sh
            lb = ub + ab * car
            lam_s[j * SUB:(j + 1) * SUB, :] = lb
            car = jnp.broadcast_to(lb[0:1], (SUB, c))
        lcar_ref[...] = car
        ahead_ref[...] = a[0:SUB]
        lmb = lam_s[...]

        hp = jnp.where(first_tile, 0.0, hp_ref[SUB:HALO, :])
        h_prev = _shift_down(hsv, 1, hp)
        d_a = lmb * h_prev
        d_mult = lmb * (ig * xrv)
        d_ixr = lmb * mult
        d_ig = d_ixr * xrv
        dxr = d_ixr * ig
        d_la = d_a * a - d_mult * (a * a) / mult
        d_r = d_la * (-LRU_C * sp)
        d_sp = jnp.sum(d_la * (-LRU_C * r), axis=0, keepdims=True)
        d_ra = d_r * r * (1.0 - r)
        d_ia = d_ig * ig * (1.0 - ig)
        d_rab, d_iab = d_ra.astype(BF16), d_ia.astype(BF16)
        parts = []
        for n in range(nb):
            cs = slice(n * LRU_BLOCK, (n + 1) * LRU_BLOCK)
            parts.append(lax.dot_general(d_rab[:, cs], wa_ref[n], _DIMS['nt'], preferred_element_type=F32)
                         + lax.dot_general(d_iab[:, cs], wx_ref[n], _DIMS['nt'], preferred_element_type=F32))
            dwa_ref[n] += lax.dot_general(xb[:, cs], d_rab[:, cs], _DIMS['tn'], preferred_element_type=F32)
            dwx_ref[n] += lax.dot_general(xb[:, cs], d_iab[:, cs], _DIMS['tn'], preferred_element_type=F32)
        dxr = dxr + jnp.concatenate(parts, axis=-1)

        cwv = cw_ref[...]
        nxt = dxhead_ref[...]
        dx0 = (cwv[3:4] * dxr + cwv[2:3] * _shift_up(dxr, 1, nxt) + cwv[1:2] * _shift_up(dxr, 2, nxt)
               + cwv[0:1] * _shift_up(dxr, 3, nxt))
        dxhead_ref[...] = dxr[0:SUB]
        dz_ref[:, c:] = dx0.astype(dz_ref.dtype)

        x0 = x_ref[...]
        xp = jnp.where(first_tile, 0.0, xp_ref[SUB:HALO, :])
        sums = [jnp.sum(d_ra, axis=0, keepdims=True), jnp.sum(d_ia, axis=0, keepdims=True),
                d_sp * (-_sigmoid(-lamv)), jnp.sum(dxr, axis=0, keepdims=True),
                jnp.sum(dxr * _shift_down(x0, 3, xp), axis=0, keepdims=True),
                jnp.sum(dxr * _shift_down(x0, 2, xp), axis=0, keepdims=True),
                jnp.sum(dxr * _shift_down(x0, 1, xp), axis=0, keepdims=True),
                jnp.sum(dxr * x0, axis=0, keepdims=True)]
        dvec_ref[...] += jnp.concatenate(sums, axis=0)

    wspec = _full((nb, LRU_BLOCK, LRU_BLOCK))
    return pl.pallas_call(
        body, grid=(nt,),
        in_specs=[rev(), rev(0), rev(1), rev_prev(1), rev(), rev(), rev_prev(), _full((4, c)),
                  wspec, _full((1, c)), wspec, _full((1, c)), _full((1, c))],
        out_specs=[pl.BlockSpec((tr, c2), lambda i: (nt - 1 - i, 0)), wspec, wspec, _full((SUB, c))],
        out_shape=[SDS((t, c2), BF16), SDS((nb, LRU_BLOCK, LRU_BLOCK), F32), SDS((nb, LRU_BLOCK, LRU_BLOCK), F32),
                   SDS((SUB, c), F32)],
        scratch_shapes=[pltpu.VMEM((SUB, c), F32), pltpu.VMEM((SUB, c), F32), pltpu.VMEM((SUB, c), F32),
                        pltpu.VMEM((tr, c), F32)],
        compiler_params=_params('arbitrary'), name=name)(
            dy, z, z, z, xr, hs, hs, cw, wa, ba.reshape(1, c), wx, bx.reshape(1, c), lam.reshape(1, c))


def _dilate(x, d):
    t, w = x.shape
    return x.reshape(t // d, d * w)


def _local_step(x, p, pos, target, rep, weights_for_layer, emit_grads):
    t, d = x.shape
    depth = p.shape[0]
    w = rep
    half = ROPE_DIM // 2
    invf = ROPE_THETA ** (-2.0 * jnp.arange(half, dtype=F32) / ROPE_DIM)
    invf = jnp.concatenate([invf, invf, jnp.zeros((HEAD_DIM - ROPE_DIM,), F32)]).reshape(1, HEAD_DIM)
    ng = len(DILATED_PATTERNS)
    saved = []
    h = x
    for i in range(depth):
        kind, j = i % N_MIXERS, i // N_MIXERS
        wl, tok = weights_for_layer(i, h)
        s = {'h0': h, 'wl': wl}
        hn = _rms_fwd(h, w['norm_mix'][i], f'rms_mix_fwd_{i}')
        s['hn'] = hn
        if kind == 0:
            z = _mm(hn, wl['w_in'], 'nn', f'sc_in_{i}', dep=tok)
            y = _sc_fwd(z, wl['small'], f'sc_conv_fwd_{i}')
            h1 = _mm(y, wl['w_out'], 'nn', f'sc_out_{i}', extras=(h,), epi=lambda acc, res: (acc + res,))
            s.update(z=z, y=y)
        elif kind == 1:
            qkv = _mm(hn, wl['w_in'], 'nn', f'attn_qkv_{i}', tn=1152, dep=tok)
            q, k, v = _rope_fwd(qkv, pos, invf, f'rope_fwd_{i}')
            os_, lses, views = [], [], []
            for g, (_, dil) in enumerate(DILATED_PATTERNS):
                qg, kg, vg = (_dilate(a[:, g * d:(g + 1) * d], dil) for a in (q, k, v))
                og, lg = _attn_fwd(qg, kg, vg, f'attn_fwd_{i}_g{g}')
                os_.append(og.reshape(t, d))
                lses.append(lg.reshape(t, d))
                views.append((qg, kg, vg))
            o, lse = _attn_combine(os_, lses, f'attn_combine_{i}')
            h1 = _mm(o, wl['w_out'], 'nn', f'attn_out_{i}', extras=(h,), epi=lambda acc, res: (acc + res,))
            s.update(views=views, o=o, lse=lse)
        else:
            z = _mm(hn, wl['w_in'], 'nn', f'lru_in_{i}', tn=1280, dep=tok)
            sm = wl['small']
            y, hs, xr = _lru_fwd(z, sm[0:4], sm[4:5], w['lru_w_a'][j], sm[5:6], w['lru_w_x'][j], sm[6:7], sm[7:8],
                                 f'lru_fwd_{i}')
            h1 = _mm(y, wl['w_out'], 'nn', f'lru_out_{i}', extras=(h,), epi=lambda acc, res: (acc + res,), tk=640)
            s.update(z=z, y=y, hs=hs, xr=xr)
        s['h1'] = h1
        hm = _rms_fwd(h1, w['norm_mlp'][i], f'rms_mlp_fwd_{i}')
        u = _mm(hm, wl['mlp_up'], 'nn', f'mlp_up_{i}', out_dtypes=(BF16,))
        h2 = _mm(u, wl['mlp_down'], 'nn', f'mlp_down_{i}', a_pro=_relu2, extras=(h1,), epi=lambda acc, res: (acc + res,))
        hp = _rms_fwd(h2, w['norm_ple'][i], f'rms_ple_fwd_{i}')
        pp = _mm(p[i], wl['ple_proj'], 'nn', f'ple_proj_{i}')
        h3, gate = _mm(hp, wl['ple_gate'], 'nn', f'ple_gate_{i}', out_dtypes=(F32, F32), extras=(pp, h2),
                       epi=lambda acc, ppv, res: (res + _sigmoid(acc) * ppv, _sigmoid(acc)))
        s.update(hm=hm, u=u, h2=h2, hp=hp, pp=pp, gate=gate)
        saved.append(s)
        h = h3

    dh, loss, dg_final = _head(h, w['norm_final'], target, 'loss_head')
    grads = {n: [None] * len(w[n]) for n in w if n != 'norm_final'}
    grads['norm_final'] = dg_final.reshape(d)
    started = None
    for i in reversed(range(depth)):
        kind, j = i % N_MIXERS, i // N_MIXERS
        s = saved[i]
        wl, gl = s['wl'], {}
        dpp, dgl = _ple_bwd_gate(dh, s['gate'], s['pp'], f'ple_bwd_gate_{i}')
        gl['ple_proj'] = _mm(p[i], dpp, 'tn', f'ple_dproj_{i}', out_dtypes=(BF16,), dep=started)
        gl['ple_gate'] = _mm(s['hp'], dgl, 'tn', f'ple_dgate_{i}', out_dtypes=(BF16,))
        dhp = _mm(dgl, wl['ple_gate'], 'nt', f'ple_dhp_{i}')
        dh, dg = _rms_bwd(s['h2'], w['norm_ple'][i], dhp, dh, f'rms_ple_bwd_{i}')
        grads['norm_ple'][i] = dg.reshape(d)
        du = _mm(dh, wl['mlp_down'], 'nt', f'mlp_du_{i}', out_dtypes=(BF16,), extras=(s['u'],),
                 epi=lambda acc, uv: (acc * 2.0 * jnp.maximum(uv.astype(F32), 0.0),))
        gl['mlp_down'] = _mm(s['u'], dh, 'tn', f'mlp_ddown_{i}', out_dtypes=(BF16,), a_pro=_relu2)
        gl['mlp_up'] = _mm(s['hm'], du, 'tn', f'mlp_dup_{i}', out_dtypes=(BF16,), out_stacked=True)
        dhm = _mm(du, wl['mlp_up'], 'nt', f'mlp_dhm_{i}')
        dh, dg = _rms_bwd(s['h1'], w['norm_mlp'][i], dhm, dh, f'rms_mlp_bwd_{i}')
        grads['norm_mlp'][i] = dg.reshape(d)
        started = emit_grads(i, 'mlp', gl)
        gl = {}
        if kind == 0:
            dy = _mm(dh, wl['w_out'], 'nt', f'sc_dy_{i}', dep=started)
            gl['w_out'] = _mm(s['y'], dh, 'tn', f'sc_dout_{i}', out_dtypes=(BF16,))
            dz, dwc = _sc_bwd(dy, s['z'], wl['small'], f'sc_conv_bwd_{i}')
            gl['small'] = dwc
            gl['w_in'] = _mm(s['hn'], dz, 'tn', f'sc_din_{i}', out_dtypes=(BF16,), out_stacked=True)
            dhn = _mm(dz, wl['w_in'], 'nt', f'sc_dhn_{i}')
        elif kind == 1:
            do, delta = _mm(dh, wl['w_out'], 'nt', f'attn_do_{i}', out_dtypes=(BF16, F32), extras=(s['o'],),
                            epi=_delta_epilogue, tn=d, dep=started)
            gl['w_out'] = _mm(s['o'], dh, 'tn', f'attn_dwo_{i}', out_dtypes=(BF16,))
            dqs, dks, dvs = [], [], []
            for g, (_, dil) in enumerate(DILATED_PATTERNS):
                qg, kg, vg = s['views'][g]
                dqg, dkg, dvg = _attn_bwd(qg, kg, vg, _dilate(do, dil), _dilate(s['lse'], dil), _dilate(delta, dil),
                                          f'attn_bwd_{i}_g{g}')
                dqs.append(dqg.reshape(t, d))
                dks.append(dkg.reshape(t, d))
                dvs.append(dvg.reshape(t, d))
            dqkv = _rope_bwd(dqs, dks, dvs, pos, invf, f'rope_bwd_{i}')
            gl['w_in'] = _mm(s['hn'], dqkv, 'tn', f'attn_dqkv_{i}', out_dtypes=(BF16,), out_stacked=True, tn=1152)
            dhn = _mm(dqkv, wl['w_in'], 'nt', f'attn_dhn_{i}', tk=1152)
        else:
            dy = _mm(dh, wl['w_out'], 'nt', f'lru_dy_{i}', tn=1280, dep=started)
            gl['w_out'] = _mm(s['y'], dh, 'tn', f'lru_dout_{i}', out_dtypes=(BF16,), tm=1280)
            sm = wl['small']
            dz, dwa, dwx, dvec = _lru_bwd(dy, s['z'], s['xr'], s['hs'], sm[0:4], w['lru_w_a'][j], sm[5:6],
                                          w['lru_w_x'][j], sm[6:7], sm[7:8], f'lru_bwd_{i}')
            grads['lru_w_a'][j], grads['lru_w_x'][j] = dwa, dwx
            gl['small'] = dvec
            gl['w_in'] = _mm(s['hn'], dz, 'tn', f'lru_din_{i}', out_dtypes=(BF16,), tn=1280)
            dhn = _mm(dz, wl['w_in'], 'nt', f'lru_dhn_{i}', tk=640)
        dh, dg = _rms_bwd(s['h0'], w['norm_mix'][i], dhn, dh, f'rms_mix_bwd_{i}')
        grads['norm_mix'][i] = dg.reshape(d)
        started = emit_grads(i, 'mixer', gl)
    return loss, dh, grads


_MESH = pl.DeviceIdType.MESH
_ANY = pl.BlockSpec(memory_space=pl.ANY)


def _block_view(ref, kind, idx):
    if kind == 'stack':
        return ref.at[idx]
    r = ref.shape[0] // N_DEV
    return ref.at[pl.ds(idx * r, r)]


def _gather_many(arrs, kinds, name):
    n = len(arrs)
    out_shapes = [SDS((N_DEV,) + a.shape if kd == 'stack' else (N_DEV * a.shape[0],) + a.shape[1:], a.dtype)
                  for a, kd in zip(arrs, kinds)]

    def body(*refs):
        x_refs, out_refs = refs[:n], refs[n:2 * n]
        send_sems, recv_sems, local_sems = refs[2 * n:]
        x, y, c = lax.axis_index('x'), lax.axis_index('y'), lax.axis_index('c')
        me, sibling = (x, y, c), (x, y, 1 - c)
        chips = [(1 - x, y), (x, 1 - y), (1 - x, 1 - y)]

        def slab(t, px, py, pc):
            return _block_view(out_refs[t], kinds[t], 4 * px + 2 * py + pc)

        def copy(t, k, block, to, src=None):
            return pltpu.make_async_remote_copy(
                src_ref=slab(t, *block) if src is None else src, dst_ref=slab(t, *block),
                send_sem=send_sems.at[7 * t + k], recv_sem=recv_sems.at[7 * t + k], device_id=to, device_id_type=_MESH)

        mine = [pltpu.make_async_copy(x_refs[t], slab(t, *me), local_sems.at[t]) for t in range(n)]
        for cp in mine:
            cp.start()
        first = [copy(t, 0, me, sibling, src=x_refs[t]) for t in range(n)]
        first += [copy(t, 1 + j, me, (*chip, c), src=x_refs[t]) for j, chip in enumerate(chips) for t in range(n)]
        for cp in first:
            cp.start()
        passed = []
        for j, chip in enumerate(chips):
            for t in range(n):
                copy(t, 1 + j, (*chip, c), me).wait_recv()
                passed.append(copy(t, 4 + j, (*chip, c), sibling))
                passed[-1].start()
        for t in range(n):
            copy(t, 0, sibling, me).wait_recv()
            for j, chip in enumerate(chips):
                copy(t, 4 + j, (*chip, 1 - c), me).wait_recv()
        for cp in first + passed:
            cp.wait_send()
        for cp in mine:
            cp.wait()

    return pl.pallas_call(
        body, out_shape=out_shapes, in_specs=[_ANY] * n, out_specs=[_ANY] * n,
        scratch_shapes=[pltpu.SemaphoreType.DMA((7 * n,)), pltpu.SemaphoreType.DMA((7 * n,)), pltpu.SemaphoreType.DMA((n,))],
        name=name)(*arrs)


_HBM = pl.BlockSpec(memory_space=pltpu.HBM)
_SEM = pl.BlockSpec(memory_space=pltpu.SEMAPHORE)
_EFFECT = pltpu.SideEffectType.DATAFLOW_SIDE_EFFECTING


def _direct_copies(mode, kinds, src_refs, land_refs, send_sems, recv_sems):
    x, y, c = lax.axis_index('x'), lax.axis_index('y'), lax.axis_index('c')
    my_idx = 4 * x + 2 * y + c
    copies = []
    for k in range(1, N_DEV):
        px, py, pc = (1 - x if k & 4 else x, 1 - y if k & 2 else y, 1 - c if k & 1 else c)
        for t, kd in enumerate(kinds):
            if mode == 'gather':
                src, dst = src_refs[t], _block_view(land_refs[t], kd, my_idx)
            else:
                src, dst = _block_view(src_refs[t], kd, 4 * px + 2 * py + pc), land_refs[t].at[my_idx]
            copies.append(pltpu.make_async_remote_copy(
                src_ref=src, dst_ref=dst, send_sem=send_sems.at[7 * t + k - 1], recv_sem=recv_sems.at[7 * t + k - 1],
                device_id=(px, py, pc), device_id_type=_MESH))
    return copies


def _own_part(mode, kind, src, land):
    idx = 4 * lax.axis_index('x') + 2 * lax.axis_index('y') + lax.axis_index('c')
    zeros = (0,) * (src.ndim - 1)
    if mode == 'gather':
        part = src
    elif kind == 'stack':
        part = lax.dynamic_index_in_dim(src, idx, 0, keepdims=False)
    else:
        r = src.shape[0] // N_DEV
        part = lax.dynamic_slice_in_dim(src, idx * r, r, 0)
    if mode == 'gather' and kind == 'rows':
        return lax.dynamic_update_slice(land, part, (idx * part.shape[0],) + zeros)
    return lax.dynamic_update_slice(land, part[None], (idx,) + (0,) * part.ndim)


def _send_start(mode, srcs, kinds, name, after=None):
    n = len(srcs)
    after = [] if after is None else [after]
    lands = []
    for a, kd in zip(srcs, kinds):
        if mode == 'gather':
            shape = (N_DEV,) + a.shape if kd == 'stack' else (N_DEV * a.shape[0],) + a.shape[1:]
        else:
            shape = a.shape if kd == 'stack' else (N_DEV, a.shape[0] // N_DEV) + a.shape[1:]
        lands.append(_own_part(mode, kd, a, lax.empty(shape, a.dtype)))

    def body(*refs):
        src_refs, land_refs = refs[:n], refs[n:2 * n]
        send_sems, recv_sems = refs[2 * n + len(after):2 * n + len(after) + 2]
        token = refs[-1]
        for cp in _direct_copies(mode, kinds, src_refs, land_refs, send_sems, recv_sems):
            cp.start()
        token[...] = jnp.zeros_like(token)

    outs = pl.pallas_call(
        body, name=name,
        out_shape=(pltpu.SemaphoreType.DMA((7 * n,)), pltpu.SemaphoreType.DMA((7 * n,)),
                   *[pltpu.HBM(a.shape, a.dtype) for a in srcs + lands], SDS((SUB, 128), F32)),
        in_specs=[_HBM] * (2 * n) + [_ANY] * len(after),
        out_specs=(_SEM, _SEM, *[_HBM] * (2 * n), pl.BlockSpec(memory_space=pltpu.VMEM)),
        input_output_aliases={i: 2 + i for i in range(2 * n)},
        compiler_params=pltpu.CompilerParams(has_side_effects=_EFFECT),
    )(*[pltpu.with_memory_space_constraint(a, pltpu.HBM) for a in srcs + lands], *after)
    return (outs[0], outs[1], list(outs[2:2 + 2 * n])), outs[-1]


def _send_wait(mode, flight, kinds, after, name):
    send, recv, bufs = flight
    n = len(kinds)

    def body(*refs):
        src_refs, land_refs, (send_sems, recv_sems) = refs[:n], refs[n:2 * n], refs[2 * n:2 * n + 2]
        copies = _direct_copies(mode, kinds, src_refs, land_refs, send_sems, recv_sems)
        for cp in copies:
            cp.wait_send()
        for cp in copies:
            cp.wait_recv()

    outs = pl.pallas_call(
        body, name=name, out_shape=[pltpu.HBM(a.shape, a.dtype) for a in bufs],
        in_specs=[_HBM] * (2 * n) + [_SEM, _SEM, _ANY], out_specs=[_HBM] * (2 * n),
        input_output_aliases={i: i for i in range(2 * n)},
        compiler_params=pltpu.CompilerParams(has_side_effects=_EFFECT),
    )(*bufs, send, recv, after)
    return list(outs[n:])


ADAMW_BLOCK_ELEMS = 128 * 1024


def _adamw_sum(wgt, parts, m, v, name):
    nl, r, c = wgt.shape
    assert len(parts) == nl and all(q.shape == (N_DEV, r, c) for q in parts), (name, wgt.shape, [q.shape for q in parts])
    tr = next((t for t in range(min(r, 512), 0, -16) if r % t == 0 and t * c <= ADAMW_BLOCK_ELEMS and t % 16 == 0), r)
    c1 = 1.0 - ADAM_B1 ** ADAM_STEP
    c2 = 1.0 - ADAM_B2 ** ADAM_STEP

    def body(w_ref, m_ref, v_ref, *rest):
        part_refs, (g_ref, d_ref, mo_ref, vo_ref) = rest[:nl], rest[nl:]
        for q in range(nl):
            @pl.when(pl.program_id(0) == q)
            def _(q=q):
                gv = part_refs[q][0].astype(F32)
                for s in range(1, N_DEV):
                    gv = gv + part_refs[q][s].astype(F32)
                mn = ADAM_B1 * m_ref[...] + (1.0 - ADAM_B1) * gv
                vn = ADAM_B2 * v_ref[...] + (1.0 - ADAM_B2) * (gv * gv)
                g_ref[...] = gv
                d_ref[...] = -ADAM_LR * ((mn / c1) / (jnp.sqrt(vn / c2) + ADAM_EPS) + ADAM_WD * w_ref[...])
                mo_ref[...] = mn
                vo_ref[...] = vn

    spec = pl.BlockSpec((None, tr, c), lambda l, i: (l, i, 0))
    part_specs = [pl.BlockSpec((N_DEV, tr, c), lambda l, i, q=q: (0, jnp.where(l == q, i, 0), 0)) for q in range(nl)]
    return pl.pallas_call(body, grid=(nl, r // tr), in_specs=[spec] * 3 + part_specs, out_specs=[spec] * 4,
                          out_shape=[SDS((nl, r, c), F32)] * 4, compiler_params=_params('arbitrary', 'arbitrary'),
                          name=name)(wgt, m, v, *parts)


MIXER_WEIGHTS = {0: ('sc_w_in', 'sc_w_out'), 1: ('attn_w_qkv', 'attn_w_o'), 2: ('lru_w_in', 'lru_w_out')}
STACKED_OPERANDS = ('sc_w_in', 'attn_w_qkv', 'mlp_w_up')
LRU_SMALL = ('lru_conv_w', 'lru_conv_b', 'lru_b_a', 'lru_b_x', 'lru_lambda')


def _layer_items(i):
    w_in, w_out = MIXER_WEIGHTS[i % N_MIXERS]
    j = i // N_MIXERS
    return [('w_in', w_in, j), ('w_out', w_out, j), ('mlp_up', 'mlp_w_up', i), ('mlp_down', 'mlp_w_down', i),
            ('ple_gate', 'ple_w_gate', i), ('ple_proj', 'ple_w_proj', i)]


def _cols_to_full(stacked):
    return jnp.moveaxis(stacked, 0, 1).reshape(stacked.shape[1], -1)


def _full_to_cols(full):
    k, n = full.shape
    return jnp.moveaxis(full.reshape(k, N_DEV, n // N_DEV), 1, 0)


def _pad_to(a, rows):
    return jnp.pad(a, ((0, rows - a.shape[0]), (0, 0)))


def _small_block(src, i):
    kind, j = i % N_MIXERS, i // N_MIXERS
    if kind == 0:
        return _pad_to(src['sc_w_conv'][j], SUB)
    if kind == 2:
        return jnp.concatenate([src[n][j].reshape(-1, src[n].shape[-1]) for n in LRU_SMALL], axis=0)
    return None


def kernel(x, p, positions, norm_mix, norm_mlp, norm_ple, norm_final, sc_w_in, sc_w_conv, sc_w_out, attn_w_qkv, attn_w_o, lru_w_in, lru_conv_w, lru_conv_b, lru_w_a, lru_b_a, lru_w_x, lru_b_x, lru_lambda, lru_w_out, mlp_w_up, mlp_w_down, ple_w_gate, ple_w_proj, loss_target, m_norm_mix, m_norm_mlp, m_norm_ple, m_norm_final, m_sc_w_in, m_sc_w_conv, m_sc_w_out, m_attn_w_qkv, m_attn_w_o, m_lru_w_in, m_lru_conv_w, m_lru_conv_b, m_lru_w_a, m_lru_b_a, m_lru_w_x, m_lru_b_x, m_lru_lambda, m_lru_w_out, m_mlp_w_up, m_mlp_w_down, m_ple_w_gate, m_ple_w_proj, v_norm_mix, v_norm_mlp, v_norm_ple, v_norm_final, v_sc_w_in, v_sc_w_conv, v_sc_w_out, v_attn_w_qkv, v_attn_w_o, v_lru_w_in, v_lru_conv_w, v_lru_conv_b, v_lru_w_a, v_lru_b_a, v_lru_w_x, v_lru_b_x, v_lru_lambda, v_lru_w_out, v_mlp_w_up, v_mlp_w_down, v_ple_w_gate, v_ple_w_proj):
    loc = dict(locals())
    shards = {n: loc[n] for n in WEIGHTS}
    moms = {n: loc['m_' + n] for n in WEIGHTS}
    vels = {n: loc['v_' + n] for n in WEIGHTS}

    depth, t, d = p.shape[0], x.shape[1], x.shape[2]

    def comm_kind(name):
        return 'stack' if SHARD_AXIS[name] == 2 else 'rows'

    def layer_shards(i):
        items = _layer_items(i)
        arrs = [shards[n][idx].astype(BF16) for _, n, idx in items]
        kinds = [comm_kind(n) for _, n, _ in items]
        small = _small_block(shards, i)
        if small is not None:
            arrs.append(small)
            kinds.append('stack')
        return items, arrs, kinds

    def layer_weights(i, items, kinds, outs):
        wl = {key: (_cols_to_full(o) if kd == 'stack' and n not in STACKED_OPERANDS else o)
              for (key, n, _), kd, o in zip(items, kinds, outs)}
        if len(outs) > len(items):
            wl['small'] = _cols_to_full(outs[-1])[:shards['sc_w_conv'].shape[1] if i % N_MIXERS == 0 else SUB]
        return wl

    items0, arrs0, kinds0 = layer_shards(0)
    outs0 = _gather_many(arrs0, kinds0, 'gather_weights_0')
    pending = {}

    def start_gather(i, after):
        if i >= depth:
            return None
        items, arrs, kinds = layer_shards(i)
        flight, token = _send_start('gather', arrs, kinds, f'gather_weights_start_{i}', after=after)
        pending[i] = (items, kinds, flight)
        return token

    first_token = start_gather(1, outs0[0])

    def weights_for_layer(i, h):
        if i == 0:
            return layer_weights(0, items0, kinds0, outs0), first_token
        items, kinds, flight = pending.pop(i)
        outs = _send_wait('gather', flight, kinds, h, f'gather_weights_wait_{i}')
        return layer_weights(i, items, kinds, outs), start_gather(i + 1, outs[0])

    part_keys = {'mlp': ('mlp_up', 'mlp_down', 'ple_gate', 'ple_proj'), 'mixer': ('w_in', 'w_out')}
    exchanges = {}

    def emit_grads(i, part, gl):
        items = [it for it in _layer_items(i) if it[0] in part_keys[part]]
        kinds = [comm_kind(n) for _, n, _ in items]
        arrs = [_full_to_cols(gl[key]) if kd == 'stack' and gl[key].ndim == 2 else gl[key]
                for (key, _, _), kd in zip(items, kinds)]
        if part == 'mixer' and i % N_MIXERS == 0:
            arrs.append(_full_to_cols(_pad_to(gl['small'], SUB)))
        elif part == 'mixer' and i % N_MIXERS == 2:
            dv = gl['small']
            arrs.append(_full_to_cols(jnp.concatenate([dv[4:8], dv[3:4], dv[0:1], dv[1:2], dv[2:3]], axis=0)))
        kinds += ['stack'] * (len(arrs) - len(kinds))
        flight, token = _send_start('exchange', arrs, kinds, f'exchange_grads_start_{part}_{i}')
        exchanges[(i, part)] = (items, kinds, flight)
        return token

    rep = {n: shards[n] for n in ('norm_mix', 'norm_mlp', 'norm_ple', 'norm_final')}
    rep['lru_w_a'], rep['lru_w_x'] = shards['lru_w_a'].astype(BF16), shards['lru_w_x'].astype(BF16)
    loss, grad_x, rgrads = _local_step(x.reshape(t, d), p.reshape(depth, t, p.shape[3]), positions.reshape(t, 1),
                                       loss_target.reshape(t, d), rep, weights_for_layer, emit_grads)
    received = {}
    for (i, part), (items, kinds, flight) in exchanges.items():
        outs = _send_wait('exchange', flight, kinds, grad_x, f'exchange_grads_wait_{part}_{i}')
        for (_, n, idx), o in zip(items, outs):
            received[(n, idx)] = o
        if len(outs) > len(items):
            received[('small', i)] = outs[-1]

    norm_names = ('norm_mix', 'norm_mlp', 'norm_ple', 'norm_final')
    gate_names = ('lru_w_a', 'lru_w_x')

    def norm_block(src):
        cat = jnp.concatenate([src[n].reshape(-1, d) for n in norm_names], axis=0)
        return _pad_to(cat, -(-cat.shape[0] // HALO) * HALO)

    def gate_block(src):
        return jnp.concatenate([src[n].reshape(-1, LRU_BLOCK) for n in gate_names], axis=0)

    rfull = {n: (rgrads[n] if n == 'norm_final' else jnp.stack(rgrads[n], axis=0)) for n in norm_names + gate_names}
    parts_norm, parts_gate = _gather_many([norm_block(rfull), gate_block(rfull)], ['stack', 'stack'],
                                          'gather_replicated_grads')

    res = {}
    for n in WEIGHTS:
        if SHARD_AXIS[n] is not None and shards[n].ndim == 3 and n not in ('sc_w_conv', 'lru_conv_w'):
            res[n] = _adamw_sum(shards[n], [received[(n, l)] for l in range(shards[n].shape[0])], moms[n], vels[n],
                                f'adamw_{n}')
    def small_adamw(layers, name):
        w_, m_, v_ = (jnp.stack([_small_block(src, i) for i in layers]) for src in (shards, moms, vels))
        return _adamw_sum(w_, [received[('small', i)] for i in layers], m_, v_, name)

    sc = small_adamw([i for i in range(depth) if i % N_MIXERS == 0], 'adamw_sc_w_conv')
    res['sc_w_conv'] = tuple(o[:, :shards['sc_w_conv'].shape[1]] for o in sc)
    lru = small_adamw([i for i in range(depth) if i % N_MIXERS == 2], 'adamw_lru_small')
    row = 0
    for n in LRU_SMALL:
        k = shards[n].size // shards[n].shape[0] // shards[n].shape[-1]
        res[n] = tuple(o[:, row:row + k].reshape(shards[n].shape) for o in lru)
        row += k
    norms = _adamw_sum(norm_block(shards)[None], [parts_norm], norm_block(moms)[None], norm_block(vels)[None],
                       'adamw_norms')
    gates = _adamw_sum(gate_block(shards)[None], [parts_gate], gate_block(moms)[None], gate_block(vels)[None],
                       'adamw_lru_gates')
    for names, outs in ((norm_names, norms), (gate_names, gates)):
        row = 0
        for n in names:
            k = shards[n].size // outs[0].shape[-1]
            res[n] = tuple(o[0, row:row + k].reshape(shards[n].shape) for o in outs)
            row += k

    loss = lax.psum(loss[0, 0], ('x', 'y', 'c'))
    return (loss, grad_x.reshape(x.shape), *[res[n][0] for n in WEIGHTS], *[res[n][1] for n in WEIGHTS],
            *[res[n][2] for n in WEIGHTS], *[res[n][3] for n in WEIGHTS])
```

```python
import functools
import math

import jax
import jax.numpy as jnp
from jax import lax
from jax.experimental import pallas as pl
from jax.experimental.pallas import tpu as pltpu

F32 = jnp.float32
BF16 = jnp.bfloat16
SDS = jax.ShapeDtypeStruct

N_DEV = 8
RMS_EPS = 1e-6
N_MIXERS = 3
HEAD_DIM = 128
DILATED_PATTERNS = ((128, 1), (512, 4), (2048, 16))
ATTN_BLOCK = 128
ROPE_THETA = 500000.0
ROPE_DIM = HEAD_DIM // 4
LRU_BLOCK = 128
LRU_C = 8.0
ADAM_LR, ADAM_B1, ADAM_B2, ADAM_EPS, ADAM_WD, ADAM_STEP = 0.001, 0.9, 0.999, 1e-08, 0.01, 10

HALO = 16
SUB = 8
VMEM_LIMIT = 56 * 1024 * 1024
NEG = -1e30

SHARD_AXIS = {
    'norm_mix': None, 'norm_mlp': None, 'norm_ple': None, 'norm_final': None,
    'sc_w_in': 2, 'sc_w_conv': 2, 'sc_w_out': 1, 'attn_w_qkv': 2, 'attn_w_o': 1,
    'lru_w_in': 2, 'lru_conv_w': 2, 'lru_conv_b': 1, 'lru_w_a': None, 'lru_b_a': 1,
    'lru_w_x': None, 'lru_b_x': 1, 'lru_lambda': 1, 'lru_w_out': 1,
    'mlp_w_up': 2, 'mlp_w_down': 1, 'ple_w_gate': 1, 'ple_w_proj': 2,
}
WEIGHTS = list(SHARD_AXIS)


def _params(*sem):
    return pltpu.CompilerParams(dimension_semantics=sem or None, vmem_limit_bytes=VMEM_LIMIT)


def _row_tile(t, pref=256):
    tr = min(t, pref)
    assert t % tr == 0 and tr % HALO == 0
    return tr


def _row(tr, c, col=0):
    return pl.BlockSpec((tr, c), lambda i, col=col: (i, col))


def _full(shape):
    return pl.BlockSpec(shape, lambda *_: (0,) * len(shape))


def _sigmoid(x):
    return 1.0 / (1.0 + jnp.exp(-x))


def _expm1(x):
    taylor = x * (1.0 + x * (0.5 + x * (1.0 / 6.0 + x * (1.0 / 24.0 + x * (1.0 / 120.0)))))
    return jnp.where(jnp.abs(x) < 0.1, taylor, jnp.exp(x) - 1.0)


def _softplus(x):
    z = jnp.exp(-jnp.abs(x))
    log1p = jnp.where(z < 0.01, z * (1.0 - z * (0.5 - z * (1.0 / 3.0 - z * 0.25))), jnp.log(1.0 + z))
    return jnp.maximum(x, 0.0) + log1p


_GELU_K = math.sqrt(2.0 / math.pi)


def _gelu_and_grad(x):
    inner = _GELU_K * (x + 0.044715 * x * x * x)
    th = jnp.tanh(inner)
    g = 0.5 * x * (1.0 + th)
    dg = 0.5 * (1.0 + th) + 0.5 * x * (1.0 - th * th) * _GELU_K * (1.0 + 3.0 * 0.044715 * x * x)
    return g, dg


def _shift_down(x, k, prev):
    row = lax.broadcasted_iota(jnp.int32, (SUB, x.shape[1]), 0)
    xr = pltpu.roll(x, k, 0)
    top = jnp.where(row < k, pltpu.roll(prev, k, 0), xr[0:SUB])
    return jnp.concatenate([top, xr[SUB:]], axis=0)


def _shift_up(x, k, nxt):
    r = x.shape[0]
    row = lax.broadcasted_iota(jnp.int32, (SUB, x.shape[1]), 0)
    xr = pltpu.roll(x, r - k, 0)
    bot = jnp.where(row >= SUB - k, pltpu.roll(nxt, SUB - k, 0), xr[r - SUB:r])
    return jnp.concatenate([xr[:r - SUB], bot], axis=0)


_DIMS = {'nn': (((1,), (0,)), ((), ())), 'nt': (((1,), (1,)), ((), ())), 'tn': (((0,), (0,)), ((), ()))}


def _pick_tile(dim, pref):
    if dim <= pref:
        return dim
    return next(c for c in range(pref - pref % 128, 0, -128) if dim % c == 0)


def _mm(a, b, dims, name, out_dtypes=(F32,), a_pro=None, extras=(), epi=None, tm=1024, tn=1024, tk=1024,
        out_stacked=False, dep=None):
    deps = [] if dep is None else [dep]
    stacked = b.ndim == 3
    b_rows, b_cols = (b.shape[1], N_DEV * b.shape[2]) if stacked else b.shape
    if dims == 'nn':
        (m, k), (k2, n) = a.shape, (b_rows, b_cols)
    elif dims == 'nt':
        (m, k), (n, k2) = a.shape, (b_rows, b_cols)
    else:
        (k, m), (k2, n) = a.shape, (b_rows, b_cols)
    assert k == k2, (name, a.shape, b.shape)
    assert not (stacked and dims == 'tn') and not (out_stacked and (extras or dims != 'tn'))
    tm = _pick_tile(m, tm)
    tn = _pick_tile(n // N_DEV if (out_stacked or (stacked and dims == 'nn')) else n, tn)
    tk = _pick_tile(k // N_DEV if (stacked and dims == 'nt') else k, tk)
    assert m % tm == 0 and n % tn == 0 and k % tk == 0, (name, m, n, k)
    nk = k // tk
    a_spec = pl.BlockSpec((tk, tm), lambda i, j, kk: (kk, i)) if dims == 'tn' else pl.BlockSpec((tm, tk), lambda i, j, kk: (i, kk))
    if not stacked:
        b_spec = pl.BlockSpec((tn, tk), lambda i, j, kk: (j, kk)) if dims == 'nt' else pl.BlockSpec((tk, tn), lambda i, j, kk: (kk, j))
    elif dims == 'nn':
        per = b.shape[2] // tn
        b_spec = pl.BlockSpec((None, tk, tn), lambda i, j, kk: (j // per, kk, j % per))
    else:
        per = b.shape[2] // tk
        b_spec = pl.BlockSpec((None, tn, tk), lambda i, j, kk: (kk // per, j, kk % per))
    if out_stacked:
        per_o = n // N_DEV // tn
        o_spec = pl.BlockSpec((None, tm, tn), lambda i, j, kk: (j // per_o, i, j % per_o))
        o_shape = (N_DEV, m, n // N_DEV)
    else:
        o_spec = pl.BlockSpec((tm, tn), lambda i, j, kk: (i, j))
        o_shape = (m, n)
    n_ex, n_out = len(extras), len(out_dtypes)
    for e in extras:
        assert e.shape == (m, n), (name, e.shape)

    def body(a_ref, b_ref, *rest):
        rest = rest[len(deps):]
        ex_refs, out_refs = rest[:n_ex], rest[n_ex:n_ex + n_out]
        kk = pl.program_id(2)
        av = a_ref[...]
        if a_pro is not None:
            av = a_pro(av.astype(F32))
        part = lax.dot_general(av.astype(BF16), b_ref[...].astype(BF16), _DIMS[dims], preferred_element_type=F32)

        def finish(res):
            outs = (res,) if epi is None else epi(res, *[e[...] for e in ex_refs])
            for o_ref, o in zip(out_refs, outs):
                o_ref[...] = o.astype(o_ref.dtype)

        if nk == 1:
            finish(part)
        else:
            acc = rest[-1]

            @pl.when(kk == 0)
            def _():
                acc[...] = part

            @pl.when(kk > 0)
            def _():
                acc[...] += part

            @pl.when(kk == nk - 1)
            def _():
                finish(acc[...])

    out = pl.pallas_call(
        body, grid=(m // tm, n // tn, nk),
        in_specs=[a_spec, b_spec] + [_ANY] * len(deps) + [o_spec] * n_ex,
        out_specs=[o_spec] * n_out,
        out_shape=[SDS(o_shape, d) for d in out_dtypes],
        scratch_shapes=[] if nk == 1 else [pltpu.VMEM((tm, tn), F32)],
        compiler_params=_params('parallel', 'parallel', 'arbitrary'), name=name)(a, b, *deps, *extras)
    return out[0] if n_out == 1 else out


def _relu2(u):
    r = jnp.maximum(u, 0.0)
    return r * r


def _rms_fwd(h, g, name):
    t, d = h.shape
    tr = _row_tile(t)

    def body(h_ref, g_ref, o_ref):
        x = h_ref[...]
        r = lax.rsqrt(jnp.mean(x * x, axis=-1, keepdims=True) + RMS_EPS)
        o_ref[...] = (x * r * g_ref[...]).astype(o_ref.dtype)

    return pl.pallas_call(body, grid=(t // tr,), in_specs=[_row(tr, d), _full((1, d))], out_specs=_row(tr, d),
                          out_shape=SDS((t, d), BF16), compiler_params=_params('parallel'), name=name)(h, g.reshape(1, d))


def _rms_bwd(h, g, dhn, dres, name):
    t, d = h.shape
    tr = _row_tile(t)

    def body(h_ref, g_ref, dhn_ref, dres_ref, dh_ref, dg_ref):
        @pl.when(pl.program_id(0) == 0)
        def _():
            dg_ref[...] = jnp.zeros_like(dg_ref)

        x = h_ref[...]
        r = lax.rsqrt(jnp.mean(x * x, axis=-1, keepdims=True) + RMS_EPS)
        dy = dhn_ref[...].astype(F32)
        gy = dy * g_ref[...]
        dx = r * gy - x * (r * r * r) * jnp.mean(gy * x, axis=-1, keepdims=True)
        dh_ref[...] = dres_ref[...] + dx
        dg_ref[...] += jnp.sum(dy * (x * r), axis=0, keepdims=True)

    return pl.pallas_call(body, grid=(t // tr,),
                          in_specs=[_row(tr, d), _full((1, d)), _row(tr, d), _row(tr, d)],
                          out_specs=[_row(tr, d), _full((1, d))],
                          out_shape=[SDS((t, d), F32), SDS((1, d), F32)],
                          compiler_params=_params('arbitrary'), name=name)(h, g.reshape(1, d), dhn, dres)


def _head(h, g, target, name):
    t, d = h.shape
    tr = _row_tile(t)

    def body(h_ref, g_ref, t_ref, dh_ref, loss_ref, dg_ref):
        @pl.when(pl.program_id(0) == 0)
        def _():
            dg_ref[...] = jnp.zeros_like(dg_ref)
            loss_ref[...] = jnp.zeros_like(loss_ref)

        x = h_ref[...]
        gv = g_ref[...]
        r = lax.rsqrt(jnp.mean(x * x, axis=-1, keepdims=True) + RMS_EPS)
        xh = x * r
        e = xh * gv - t_ref[...]
        per_tok = jnp.mean(e * e, axis=-1, keepdims=True)
        loss_ref[...] += jnp.broadcast_to(0.5 * jnp.sum(per_tok, axis=0, keepdims=True), loss_ref.shape)
        dy = e * (1.0 / d)
        gy = dy * gv
        dh_ref[...] = r * gy - x * (r * r * r) * jnp.mean(gy * x, axis=-1, keepdims=True)
        dg_ref[...] += jnp.sum(dy * xh, axis=0, keepdims=True)

    return pl.pallas_call(body, grid=(t // tr,),
                          in_specs=[_row(tr, d), _full((1, d)), _row(tr, d)],
                          out_specs=[_row(tr, d), _full((1, 128)), _full((1, d))],
                          out_shape=[SDS((t, d), F32), SDS((1, 128), F32), SDS((1, d), F32)],
                          compiler_params=_params('arbitrary'), name=name)(h, g.reshape(1, d), target)


def _ple_bwd_gate(dh3, gate, pp, name):
    t, d = dh3.shape
    tr = _row_tile(t)

    def body(dh_ref, g_ref, pp_ref, dpp_ref, dgl_ref):
        dh = dh_ref[...]
        gt = g_ref[...]
        dpp_ref[...] = (dh * gt).astype(dpp_ref.dtype)
        dgl_ref[...] = (dh * pp_ref[...] * gt * (1.0 - gt)).astype(dgl_ref.dtype)

    return pl.pallas_call(body, grid=(t // tr,), in_specs=[_row(tr, d)] * 3, out_specs=[_row(tr, d)] * 2,
                          out_shape=[SDS((t, d), BF16), SDS((t, d), BF16)],
                          compiler_params=_params('parallel'), name=name)(dh3, gate, pp)


def _halo_prev(tr, c, col=0):
    return pl.BlockSpec((HALO, c), lambda i, col=col: (jnp.maximum(i * (tr // HALO) - 1, 0), col))


def _halo_next(tr, c, t, col=0):
    return pl.BlockSpec((HALO, c), lambda i, col=col: (jnp.minimum((i + 1) * (tr // HALO), t // HALO - 1), col))


def _sc_fwd(z, w, name):
    t, c3 = z.shape
    c = c3 // 3
    tr = _row_tile(t)

    def body(z_ref, zp_ref, w_ref, y_ref):
        i = pl.program_id(0)
        zz = z_ref[...]
        gb, cx = zz[:, :c], zz[:, c:2 * c] * zz[:, 2 * c:]
        zp = zp_ref[SUB:HALO, :]
        cxp = jnp.where(i > 0, zp[:, c:2 * c] * zp[:, 2 * c:], 0.0)
        wv = w_ref[...]
        conv = wv[2:3] * cx + wv[1:2] * _shift_down(cx, 1, cxp) + wv[0:1] * _shift_down(cx, 2, cxp)
        y_ref[...] = (gb * conv).astype(y_ref.dtype)

    return pl.pallas_call(body, grid=(t // tr,),
                          in_specs=[_row(tr, c3), _halo_prev(tr, c3), _full((3, c))],
                          out_specs=_row(tr, c), out_shape=SDS((t, c), BF16),
                          compiler_params=_params('parallel'), name=name)(z, z, w)


def _sc_bwd(dy, z, w, name):
    t, c3 = z.shape
    c = c3 // 3
    tr = _row_tile(t)
    nt = t // tr

    def body(dy_ref, dyn_ref, z_ref, zp_ref, zn_ref, w_ref, dz_ref, dw_ref):
        i = pl.program_id(0)

        @pl.when(i == 0)
        def _():
            dw_ref[...] = jnp.zeros_like(dw_ref)

        zz = z_ref[...]
        gb, gc, xi = zz[:, :c], zz[:, c:2 * c], zz[:, 2 * c:]
        cx = gc * xi
        zp = zp_ref[SUB:HALO, :]
        cxp = jnp.where(i > 0, zp[:, c:2 * c] * zp[:, 2 * c:], 0.0)
        wv = w_ref[...]
        cx1, cx2 = _shift_down(cx, 1, cxp), _shift_down(cx, 2, cxp)
        conv = wv[2:3] * cx + wv[1:2] * cx1 + wv[0:1] * cx2
        dyv = dy_ref[...]
        dconv = dyv * gb
        dcn = jnp.where(i < nt - 1, dyn_ref[0:SUB, :] * zn_ref[0:SUB, :c], 0.0)
        dcx = wv[2:3] * dconv + wv[1:2] * _shift_up(dconv, 1, dcn) + wv[0:1] * _shift_up(dconv, 2, dcn)
        dz_ref[:, :c] = (dyv * conv).astype(dz_ref.dtype)
        dz_ref[:, c:2 * c] = (dcx * xi).astype(dz_ref.dtype)
        dz_ref[:, 2 * c:] = (dcx * gc).astype(dz_ref.dtype)
        dw_ref[...] += jnp.concatenate([jnp.sum(dconv * cx2, axis=0, keepdims=True),
                                        jnp.sum(dconv * cx1, axis=0, keepdims=True),
                                        jnp.sum(dconv * cx, axis=0, keepdims=True)], axis=0)

    return pl.pallas_call(body, grid=(nt,),
                          in_specs=[_row(tr, c), _halo_next(tr, c, t), _row(tr, c3), _halo_prev(tr, c3),
                                    _halo_next(tr, c3, t), _full((3, c))],
                          out_specs=[_row(tr, c3), _full((3, c))],
                          out_shape=[SDS((t, c3), BF16), SDS((3, c), F32)],
                          compiler_params=_params('arbitrary'), name=name)(dy, dy, z, z, z, w)


def _rope_tables(pos_ref, invf_ref, sign):
    lane = lax.broadcasted_iota(jnp.int32, (pos_ref.shape[0], HEAD_DIM), 1)
    ang = pos_ref[...].astype(F32) * invf_ref[...]
    half = ROPE_DIM // 2
    cos = jnp.where(lane < ROPE_DIM, jnp.cos(ang), 1.0)
    sin = jnp.sin(ang) * sign
    sin = jnp.where(lane < half, -sin, jnp.where(lane < ROPE_DIM, sin, 0.0))
    return lane, cos, sin


def _rope_apply(x, lane, cos, sin):
    half = ROPE_DIM // 2
    xs = jnp.where(lane < half, pltpu.roll(x, HEAD_DIM - half, 1), pltpu.roll(x, half, 1))
    return x * cos + xs * sin


def _rope_fwd(qkv, pos, invf, name):
    t, w3 = qkv.shape
    w = w3 // 3
    tr = _row_tile(t)

    def body(q_ref, k_ref, v_ref, pos_ref, invf_ref, qo_ref, ko_ref, vo_ref):
        lane, cos, sin = _rope_tables(pos_ref, invf_ref, 1.0)
        for hh in range(w // HEAD_DIM):
            cs = slice(hh * HEAD_DIM, (hh + 1) * HEAD_DIM)
            qo_ref[:, cs] = _rope_apply(q_ref[:, cs], lane, cos, sin).astype(qo_ref.dtype)
            ko_ref[:, cs] = _rope_apply(k_ref[:, cs], lane, cos, sin).astype(ko_ref.dtype)
        vo_ref[...] = v_ref[...].astype(vo_ref.dtype)

    return pl.pallas_call(body, grid=(t // tr,),
                          in_specs=[_row(tr, w, 0), _row(tr, w, 1), _row(tr, w, 2), _row(tr, 1), _full((1, HEAD_DIM))],
                          out_specs=[_row(tr, w)] * 3, out_shape=[SDS((t, w), BF16)] * 3,
                          compiler_params=_params('parallel'), name=name)(qkv, qkv, qkv, pos, invf)


def _rope_bwd(dqs, dks, dvs, pos, invf, name):
    t, d = dqs[0].shape
    ng = len(dqs)
    w = ng * d
    tr = _row_tile(t)

    def body(*refs):
        dq_refs, dk_refs, dv_refs = refs[:ng], refs[ng:2 * ng], refs[2 * ng:3 * ng]
        pos_ref, invf_ref, o_ref = refs[3 * ng], refs[3 * ng + 1], refs[3 * ng + 2]
        lane, cos, sin = _rope_tables(pos_ref, invf_ref, -1.0)
        for g in range(ng):
            for hh in range(d // HEAD_DIM):
                cs = slice(hh * HEAD_DIM, (hh + 1) * HEAD_DIM)
                base = g * d + hh * HEAD_DIM
                o_ref[:, base:base + HEAD_DIM] = _rope_apply(dq_refs[g][:, cs], lane, cos, sin).astype(o_ref.dtype)
                o_ref[:, w + base:w + base + HEAD_DIM] = _rope_apply(dk_refs[g][:, cs], lane, cos, sin).astype(o_ref.dtype)
            o_ref[:, 2 * w + g * d:2 * w + (g + 1) * d] = dv_refs[g][...].astype(o_ref.dtype)

    return pl.pallas_call(body, grid=(t // tr,),
                          in_specs=[_row(tr, d)] * (3 * ng) + [_row(tr, 1), _full((1, HEAD_DIM))],
                          out_specs=_row(tr, 3 * w), out_shape=SDS((t, 3 * w), BF16),
                          compiler_params=_params('parallel'), name=name)(*dqs, *dks, *dvs, pos, invf)


def _attn_masks():
    qi = lax.broadcasted_iota(jnp.int32, (ATTN_BLOCK, ATTN_BLOCK), 0)
    kj = lax.broadcasted_iota(jnp.int32, (ATTN_BLOCK, ATTN_BLOCK), 1)
    return kj >= qi, kj <= qi


def _attn_cols(l, width):
    ncol = width // HEAD_DIM
    cpb = max(1, min(ncol, 32 // (l // ATTN_BLOCK)))
    assert ncol % cpb == 0
    return cpb


def _attn_fwd(q, k, v, name):
    l, width = q.shape
    cpb = _attn_cols(l, width)
    nb = l // ATTN_BLOCK
    scale = HEAD_DIM ** -0.5

    def body(q_ref, k_ref, v_ref, o_ref, lse_ref):
        m_prev, m_cur = _attn_masks()
        for col in range(cpb):
            cs = slice(col * HEAD_DIM, (col + 1) * HEAD_DIM)

            def step(b, carry, cs=cs):
                r0 = pl.multiple_of(b * ATTN_BLOCK, ATTN_BLOCK)
                rp = pl.multiple_of(jnp.maximum(b - 1, 0) * ATTN_BLOCK, ATTN_BLOCK)
                qb = q_ref[pl.ds(r0, ATTN_BLOCK), cs]
                s_p = lax.dot_general(qb, k_ref[pl.ds(rp, ATTN_BLOCK), cs], _DIMS['nt'], preferred_element_type=F32) * scale
                s_c = lax.dot_general(qb, k_ref[pl.ds(r0, ATTN_BLOCK), cs], _DIMS['nt'], preferred_element_type=F32) * scale
                s_p = jnp.where(jnp.logical_and(m_prev, b > 0), s_p, NEG)
                s_c = jnp.where(m_cur, s_c, NEG)
                m = jnp.maximum(jnp.max(s_p, axis=-1, keepdims=True), jnp.max(s_c, axis=-1, keepdims=True))
                p_p, p_c = jnp.exp(s_p - m), jnp.exp(s_c - m)
                den = jnp.sum(p_p, axis=-1, keepdims=True) + jnp.sum(p_c, axis=-1, keepdims=True)
                acc = jnp.dot(p_p.astype(BF16), v_ref[pl.ds(rp, ATTN_BLOCK), cs], preferred_element_type=F32)
                acc += jnp.dot(p_c.astype(BF16), v_ref[pl.ds(r0, ATTN_BLOCK), cs], preferred_element_type=F32)
                o_ref[pl.ds(r0, ATTN_BLOCK), cs] = acc / den
                lse_ref[pl.ds(r0, ATTN_BLOCK), cs] = jnp.broadcast_to(m + jnp.log(den), (ATTN_BLOCK, HEAD_DIM))
                return carry

            lax.fori_loop(0, nb, step, 0, unroll=min(nb, 4))

    spec = pl.BlockSpec((l, cpb * HEAD_DIM), lambda j: (0, j))
    return pl.pallas_call(body, grid=(width // (cpb * HEAD_DIM),), in_specs=[spec] * 3, out_specs=[spec] * 2,
                          out_shape=[SDS((l, width), F32)] * 2,
                          compiler_params=_params('parallel'), name=name)(q, k, v)


def _attn_bwd(q, k, v, do, lse, delta, name):
    l, width = q.shape
    cpb = _attn_cols(l, width)
    nb = l // ATTN_BLOCK
    scale = HEAD_DIM ** -0.5

    def body(q_ref, k_ref, v_ref, do_ref, lse_ref, dl_ref, dq_ref, dk_ref, dv_ref):
        m_prev, m_cur = _attn_masks()
        dk_ref[...] = jnp.zeros_like(dk_ref)
        dv_ref[...] = jnp.zeros_like(dv_ref)
        for col in range(cpb):
            cs = slice(col * HEAD_DIM, (col + 1) * HEAD_DIM)

            def step(b, carry, cs=cs):
                r0 = pl.multiple_of(b * ATTN_BLOCK, ATTN_BLOCK)
                rp = pl.multiple_of(jnp.maximum(b - 1, 0) * ATTN_BLOCK, ATTN_BLOCK)
                qb, dob = q_ref[pl.ds(r0, ATTN_BLOCK), cs], do_ref[pl.ds(r0, ATTN_BLOCK), cs]
                kp, kc = k_ref[pl.ds(rp, ATTN_BLOCK), cs], k_ref[pl.ds(r0, ATTN_BLOCK), cs]
                vp, vc = v_ref[pl.ds(rp, ATTN_BLOCK), cs], v_ref[pl.ds(r0, ATTN_BLOCK), cs]
                lse_b = lse_ref[pl.ds(r0, ATTN_BLOCK), cs]
                dl_b = dl_ref[pl.ds(r0, ATTN_BLOCK), cs]
                s_p = lax.dot_general(qb, kp, _DIMS['nt'], preferred_element_type=F32) * scale
                s_c = lax.dot_general(qb, kc, _DIMS['nt'], preferred_element_type=F32) * scale
                p_p = jnp.exp(jnp.where(jnp.logical_and(m_prev, b > 0), s_p, NEG) - lse_b)
                p_c = jnp.exp(jnp.where(m_cur, s_c, NEG) - lse_b)
                dp_p = lax.dot_general(dob, vp, _DIMS['nt'], preferred_element_type=F32)
                dp_c = lax.dot_general(dob, vc, _DIMS['nt'], preferred_element_type=F32)
                ds_p = (p_p * (dp_p - dl_b) * scale).astype(BF16)
                ds_c = (p_c * (dp_c - dl_b) * scale).astype(BF16)
                dq_ref[pl.ds(r0, ATTN_BLOCK), cs] = (jnp.dot(ds_p, kp, preferred_element_type=F32)
                                                     + jnp.dot(ds_c, kc, preferred_element_type=F32))
                dk_ref[pl.ds(rp, ATTN_BLOCK), cs] += lax.dot_general(ds_p, qb, _DIMS['tn'], preferred_element_type=F32)
                dk_ref[pl.ds(r0, ATTN_BLOCK), cs] += lax.dot_general(ds_c, qb, _DIMS['tn'], preferred_element_type=F32)
                dv_ref[pl.ds(rp, ATTN_BLOCK), cs] += lax.dot_general(p_p.astype(BF16), dob, _DIMS['tn'], preferred_element_type=F32)
                dv_ref[pl.ds(r0, ATTN_BLOCK), cs] += lax.dot_general(p_c.astype(BF16), dob, _DIMS['tn'], preferred_element_type=F32)
                return carry

            lax.fori_loop(0, nb, step, 0, unroll=min(nb, 2))

    spec = pl.BlockSpec((l, cpb * HEAD_DIM), lambda j: (0, j))
    return pl.pallas_call(body, grid=(width // (cpb * HEAD_DIM),), in_specs=[spec] * 6, out_specs=[spec] * 3,
                          out_shape=[SDS((l, width), F32)] * 3,
                          compiler_params=_params('parallel'), name=name)(q, k, v, do, lse, delta)


def _attn_combine(os_, lses, name):
    t, d = os_[0].shape
    ng = len(os_)
    tr = _row_tile(t)

    def body(*refs):
        o_refs, l_refs, o_out, lse_out = refs[:ng], refs[ng:2 * ng], refs[2 * ng], refs[2 * ng + 1]
        ls = [r[...] for r in l_refs]
        m = functools.reduce(jnp.maximum, ls)
        ws = [jnp.exp(x - m) for x in ls]
        den = functools.reduce(lambda a, b: a + b, ws)
        acc = functools.reduce(lambda a, b: a + b, [w * o[...] for w, o in zip(ws, o_refs)])
        o_out[...] = (acc / den).astype(o_out.dtype)
        lse_out[...] = m + jnp.log(den)

    return pl.pallas_call(body, grid=(t // tr,), in_specs=[_row(tr, d)] * (2 * ng), out_specs=[_row(tr, d)] * 2,
                          out_shape=[SDS((t, d), BF16), SDS((t, d), F32)],
                          compiler_params=_params('parallel'), name=name)(*os_, *lses)


def _delta_epilogue(acc, o):
    prod = acc * o.astype(F32)
    segs = [jnp.broadcast_to(jnp.sum(prod[:, s:s + HEAD_DIM], axis=-1, keepdims=True), (acc.shape[0], HEAD_DIM))
            for s in range(0, acc.shape[1], HEAD_DIM)]
    return acc, jnp.concatenate(segs, axis=-1)


LRU_TILE = 128


def _lru_gates(xr, wa_ref, ba, wx_ref, bx, lam):
    nb = wa_ref.shape[0]
    xb = xr.astype(BF16)
    ra = jnp.concatenate([jnp.dot(xb[:, n * LRU_BLOCK:(n + 1) * LRU_BLOCK], wa_ref[n], preferred_element_type=F32)
                          for n in range(nb)], axis=-1) + ba
    ia = jnp.concatenate([jnp.dot(xb[:, n * LRU_BLOCK:(n + 1) * LRU_BLOCK], wx_ref[n], preferred_element_type=F32)
                          for n in range(nb)], axis=-1) + bx
    r, ig = _sigmoid(ra), _sigmoid(ia)
    sp = _softplus(-lam)
    log_a = -LRU_C * r * sp
    a = jnp.exp(log_a)
    mult = jnp.sqrt(-_expm1(2.0 * log_a))
    return xb, r, ig, sp, a, mult


def _lru_fwd(z, cw, cb, wa, ba, wx, bx, lam, name):
    t, c2 = z.shape
    c = c2 // 2
    nb = c // LRU_BLOCK
    tr = _row_tile(t, LRU_TILE)

    def body(g_ref, x_ref, xp_ref, cw_ref, cb_ref, wa_ref, ba_ref, wx_ref, bx_ref, lam_ref,
             y_ref, hs_ref, xr_ref, car_ref):
        i = pl.program_id(0)

        @pl.when(i == 0)
        def _():
            car_ref[...] = jnp.zeros_like(car_ref)

        x0 = x_ref[...]
        xp = jnp.where(i > 0, xp_ref[SUB:HALO, :], 0.0)
        cwv = cw_ref[...]
        xr = (cb_ref[...] + cwv[3:4] * x0 + cwv[2:3] * _shift_down(x0, 1, xp)
              + cwv[1:2] * _shift_down(x0, 2, xp) + cwv[0:1] * _shift_down(x0, 3, xp))
        xr_ref[...] = xr
        _, _, ig, _, a, mult = _lru_gates(xr, wa_ref, ba_ref[...], wx_ref, bx_ref[...], lam_ref[...])
        u = mult * (ig * xr)
        row = lax.broadcasted_iota(jnp.int32, (SUB, c), 0)
        car = car_ref[...]
        for j in range(tr // SUB):
            ab, ub = a[j * SUB:(j + 1) * SUB], u[j * SUB:(j + 1) * SUB]
            for s in (1, 2, 4):
                a_sh = jnp.where(row >= s, pltpu.roll(ab, s, 0), 1.0)
                u_sh = jnp.where(row >= s, pltpu.roll(ub, s, 0), 0.0)
                ub = ab * u_sh + ub
                ab = ab * a_sh
            hb = ub + ab * car
            hs_ref[j * SUB:(j + 1) * SUB, :] = hb
            car = jnp.broadcast_to(hb[SUB - 1:SUB], (SUB, c))
        car_ref[...] = car
        gl, _ = _gelu_and_grad(g_ref[...])
        y_ref[...] = (hs_ref[...] * gl).astype(y_ref.dtype)

    return pl.pallas_call(
        body, grid=(t // tr,),
        in_specs=[_row(tr, c, 0), _row(tr, c, 1), _halo_prev(tr, c, 1), _full((4, c)), _full((1, c)),
                  _full((nb, LRU_BLOCK, LRU_BLOCK)), _full((1, c)), _full((nb, LRU_BLOCK, LRU_BLOCK)), _full((1, c)), _full((1, c))],
        out_specs=[_row(tr, c)] * 3,
        out_shape=[SDS((t, c), BF16), SDS((t, c), F32), SDS((t, c), F32)],
        scratch_shapes=[pltpu.VMEM((SUB, c), F32)],
        compiler_params=_params('arbitrary'), name=name)(
            z, z, z, cw, cb.reshape(1, c), wa, ba.reshape(1, c), wx, bx.reshape(1, c), lam.reshape(1, c))


def _lru_bwd(dy, z, xr, hs, cw, wa, ba, wx, bx, lam, name):
    t, c2 = z.shape
    c = c2 // 2
    nb = c // LRU_BLOCK
    tr = _row_tile(t, LRU_TILE)
    nt = t // tr

    def rev(col=0):
        return pl.BlockSpec((tr, c), lambda i, col=col: (nt - 1 - i, col))

    def rev_prev(col=0):
        return pl.BlockSpec((HALO, c), lambda i, col=col: (jnp.maximum((nt - 1 - i) * (tr // HALO) - 1, 0), col))

    def body(dy_ref, g_ref, x_ref, xp_ref, xr_ref, hs_ref, hp_ref, cw_ref, wa_ref, ba_ref, wx_ref, bx_ref, lam_ref,
             dz_ref, dwa_ref, dwx_ref, dvec_ref, lcar_ref, ahead_ref, dxhead_ref, lam_s):
        i = pl.program_id(0)
        first_tile = i == nt - 1

        @pl.when(i == 0)
        def _():
            lcar_ref[...] = jnp.zeros_like(lcar_ref)
            ahead_ref[...] = jnp.zeros_like(ahead_ref)
            dxhead_ref[...] = jnp.zeros_like(dxhead_ref)
            dwa_ref[...] = jnp.zeros_like(dwa_ref)
            dwx_ref[...] = jnp.zeros_like(dwx_ref)
            dvec_ref[...] = jnp.zeros_like(dvec_ref)

        xrv = xr_ref[...]
        lamv = lam_ref[...]
        xb, r, ig, sp, a, mult = _lru_gates(xrv, wa_ref, ba_ref[...], wx_ref, bx_ref[...], lamv)
        hsv = hs_ref[...]
        dyv = dy_ref[...]
        gl, dgl = _gelu_and_grad(g_ref[...])
        dhs = dyv * gl
        dz_ref[:, :c] = (dyv * hsv * dgl).astype(dz_ref.dtype)

        a_next = _shift_up(a, 1, ahead_ref[...])
        row = lax.broadcasted_iota(jnp.int32, (SUB, c), 0)
        car = lcar_ref[...]
        for j in reversed(range(tr // SUB)):
            ab, ub = a_next[j * SUB:(j + 1) * SUB], dhs[j * SUB:(j + 1) * SUB]
            for s in (1, 2, 4):
                a_sh = jnp.where(row < SUB - s, pltpu.roll(ab, SUB - s, 0), 1.0)
                u_sh = jnp.where(row < SUB - s, pltpu.roll(ub, SUB - s, 0), 0.0)
                ub = ab * u_sh + ub
                ab = ab * a_sh
            lb = ub + ab * car
            lam_s[j * SUB:(j + 1) * SUB, :] = lb
            car = jnp.broadcast_to(lb[0:1], (SUB, c))
        lcar_ref[...] = car
        ahead_ref[...] = a[0:SUB]
        lmb = lam_s[...]

        hp = jnp.where(first_tile, 0.0, hp_ref[SUB:HALO, :])
        h_prev = _shift_down(hsv, 1, hp)
        d_a = lmb * h_prev
        d_mult = lmb * (ig * xrv)
        d_ixr = lmb * mult
        d_ig = d_ixr * xrv
        dxr = d_ixr * ig
        d_la = d_a * a - d_mult * (a * a) / mult
        d_r = d_la * (-LRU_C * sp)
        d_sp = jnp.sum(d_la * (-LRU_C * r), axis=0, keepdims=True)
        d_ra = d_r * r * (1.0 - r)
        d_ia = d_ig * ig * (1.0 - ig)
        d_rab, d_iab = d_ra.astype(BF16), d_ia.astype(BF16)
        parts = []
        for n in range(nb):
            cs = slice(n * LRU_BLOCK, (n + 1) * LRU_BLOCK)
            parts.append(lax.dot_general(d_rab[:, cs], wa_ref[n], _DIMS['nt'], preferred_element_type=F32)
                         + lax.dot_general(d_iab[:, cs], wx_ref[n], _DIMS['nt'], preferred_element_type=F32))
            dwa_ref[n] += lax.dot_general(xb[:, cs], d_rab[:, cs], _DIMS['tn'], preferred_element_type=F32)
            dwx_ref[n] += lax.dot_general(xb[:, cs], d_iab[:, cs], _DIMS['tn'], preferred_element_type=F32)
        dxr = dxr + jnp.concatenate(parts, axis=-1)

        cwv = cw_ref[...]
        nxt = dxhead_ref[...]
        dx0 = (cwv[3:4] * dxr + cwv[2:3] * _shift_up(dxr, 1, nxt) + cwv[1:2] * _shift_up(dxr, 2, nxt)
               + cwv[0:1] * _shift_up(dxr, 3, nxt))
        dxhead_ref[...] = dxr[0:SUB]
        dz_ref[:, c:] = dx0.astype(dz_ref.dtype)

        x0 = x_ref[...]
        xp = jnp.where(first_tile, 0.0, xp_ref[SUB:HALO, :])
        sums = [jnp.sum(d_ra, axis=0, keepdims=True), jnp.sum(d_ia, axis=0, keepdims=True),
                d_sp * (-_sigmoid(-lamv)), jnp.sum(dxr, axis=0, keepdims=True),
                jnp.sum(dxr * _shift_down(x0, 3, xp), axis=0, keepdims=True),
                jnp.sum(dxr * _shift_down(x0, 2, xp), axis=0, keepdims=True),
                jnp.sum(dxr * _shift_down(x0, 1, xp), axis=0, keepdims=True),
                jnp.sum(dxr * x0, axis=0, keepdims=True)]
        dvec_ref[...] += jnp.concatenate(sums, axis=0)

    wspec = _full((nb, LRU_BLOCK, LRU_BLOCK))
    return pl.pallas_call(
        body, grid=(nt,),
        in_specs=[rev(), rev(0), rev(1), rev_prev(1), rev(), rev(), rev_prev(), _full((4, c)),
                  wspec, _full((1, c)), wspec, _full((1, c)), _full((1, c))],
        out_specs=[pl.BlockSpec((tr, c2), lambda i: (nt - 1 - i, 0)), wspec, wspec, _full((SUB, c))],
        out_shape=[SDS((t, c2), BF16), SDS((nb, LRU_BLOCK, LRU_BLOCK), F32), SDS((nb, LRU_BLOCK, LRU_BLOCK), F32),
                   SDS((SUB, c), F32)],
        scratch_shapes=[pltpu.VMEM((SUB, c), F32), pltpu.VMEM((SUB, c), F32), pltpu.VMEM((SUB, c), F32),
                        pltpu.VMEM((tr, c), F32)],
        compiler_params=_params('arbitrary'), name=name)(
            dy, z, z, z, xr, hs, hs, cw, wa, ba.reshape(1, c), wx, bx.reshape(1, c), lam.reshape(1, c))


def _dilate(x, d):
    t, w = x.shape
    return x.reshape(t // d, d * w)


def _local_step(x, p, pos, target, rep, weights_for_layer, emit_grads):
    t, d = x.shape
    depth = p.shape[0]
    w = rep
    half = ROPE_DIM // 2
    invf = ROPE_THETA ** (-2.0 * jnp.arange(half, dtype=F32) / ROPE_DIM)
    invf = jnp.concatenate([invf, invf, jnp.zeros((HEAD_DIM - ROPE_DIM,), F32)]).reshape(1, HEAD_DIM)
    ng = len(DILATED_PATTERNS)
    saved = []
    h = x
    for i in range(depth):
        kind, j = i % N_MIXERS, i // N_MIXERS
        wl, tok = weights_for_layer(i, h)
        s = {'h0': h, 'wl': wl}
        hn = _rms_fwd(h, w['norm_mix'][i], f'rms_mix_fwd_{i}')
        s['hn'] = hn
        if kind == 0:
            z = _mm(hn, wl['w_in'], 'nn', f'sc_in_{i}', dep=tok)
            y = _sc_fwd(z, wl['small'], f'sc_conv_fwd_{i}')
            h1 = _mm(y, wl['w_out'], 'nn', f'sc_out_{i}', extras=(h,), epi=lambda acc, res: (acc + res,))
            s.update(z=z, y=y)
        elif kind == 1:
            qkv = _mm(hn, wl['w_in'], 'nn', f'attn_qkv_{i}', tn=1152, dep=tok)
            q, k, v = _rope_fwd(qkv, pos, invf, f'rope_fwd_{i}')
            os_, lses, views = [], [], []
            for g, (_, dil) in enumerate(DILATED_PATTERNS):
                qg, kg, vg = (_dilate(a[:, g * d:(g + 1) * d], dil) for a in (q, k, v))
                og, lg = _attn_fwd(qg, kg, vg, f'attn_fwd_{i}_g{g}')
                os_.append(og.reshape(t, d))
                lses.append(lg.reshape(t, d))
                views.append((qg, kg, vg))
            o, lse = _attn_combine(os_, lses, f'attn_combine_{i}')
            h1 = _mm(o, wl['w_out'], 'nn', f'attn_out_{i}', extras=(h,), epi=lambda acc, res: (acc + res,))
            s.update(views=views, o=o, lse=lse)
        else:
            z = _mm(hn, wl['w_in'], 'nn', f'lru_in_{i}', tn=1280, dep=tok)
            sm = wl['small']
            y, hs, xr = _lru_fwd(z, sm[0:4], sm[4:5], w['lru_w_a'][j], sm[5:6], w['lru_w_x'][j], sm[6:7], sm[7:8],
                                 f'lru_fwd_{i}')
            h1 = _mm(y, wl['w_out'], 'nn', f'lru_out_{i}', extras=(h,), epi=lambda acc, res: (acc + res,), tk=640)
            s.update(z=z, y=y, hs=hs, xr=xr)
        s['h1'] = h1
        hm = _rms_fwd(h1, w['norm_mlp'][i], f'rms_mlp_fwd_{i}')
        u = _mm(hm, wl['mlp_up'], 'nn', f'mlp_up_{i}', out_dtypes=(BF16,))
        h2 = _mm(u, wl['mlp_down'], 'nn', f'mlp_down_{i}', a_pro=_relu2, extras=(h1,), epi=lambda acc, res: (acc + res,))
        hp = _rms_fwd(h2, w['norm_ple'][i], f'rms_ple_fwd_{i}')
        pp = _mm(p[i], wl['ple_proj'], 'nn', f'ple_proj_{i}')
        h3, gate = _mm(hp, wl['ple_gate'], 'nn', f'ple_gate_{i}', out_dtypes=(F32, F32), extras=(pp, h2),
                       epi=lambda acc, ppv, res: (res + _sigmoid(acc) * ppv, _sigmoid(acc)))
        s.update(hm=hm, u=u, h2=h2, hp=hp, pp=pp, gate=gate)
        saved.append(s)
        h = h3

    dh, loss, dg_final = _head(h, w['norm_final'], target, 'loss_head')
    grads = {n: [None] * len(w[n]) for n in w if n != 'norm_final'}
    grads['norm_final'] = dg_final.reshape(d)
    started = None
    for i in reversed(range(depth)):
        kind, j = i % N_MIXERS, i // N_MIXERS
        s = saved[i]
        wl, gl = s['wl'], {}
        dpp, dgl = _ple_bwd_gate(dh, s['gate'], s['pp'], f'ple_bwd_gate_{i}')
        gl['ple_proj'] = _mm(p[i], dpp, 'tn', f'ple_dproj_{i}', out_dtypes=(BF16,), dep=started)
        gl['ple_gate'] = _mm(s['hp'], dgl, 'tn', f'ple_dgate_{i}', out_dtypes=(BF16,))
        dhp = _mm(dgl, wl['ple_gate'], 'nt', f'ple_dhp_{i}')
        dh, dg = _rms_bwd(s['h2'], w['norm_ple'][i], dhp, dh, f'rms_ple_bwd_{i}')
        grads['norm_ple'][i] = dg.reshape(d)
        du = _mm(dh, wl['mlp_down'], 'nt', f'mlp_du_{i}', out_dtypes=(BF16,), extras=(s['u'],),
                 epi=lambda acc, uv: (acc * 2.0 * jnp.maximum(uv.astype(F32), 0.0),))
        gl['mlp_down'] = _mm(s['u'], dh, 'tn', f'mlp_ddown_{i}', out_dtypes=(BF16,), a_pro=_relu2)
        gl['mlp_up'] = _mm(s['hm'], du, 'tn', f'mlp_dup_{i}', out_dtypes=(BF16,), out_stacked=True)
        dhm = _mm(du, wl['mlp_up'], 'nt', f'mlp_dhm_{i}')
        dh, dg = _rms_bwd(s['h1'], w['norm_mlp'][i], dhm, dh, f'rms_mlp_bwd_{i}')
        grads['norm_mlp'][i] = dg.reshape(d)
        started = emit_grads(i, 'mlp', gl)
        gl = {}
        if kind == 0:
            dy = _mm(dh, wl['w_out'], 'nt', f'sc_dy_{i}', dep=started)
            gl['w_out'] = _mm(s['y'], dh, 'tn', f'sc_dout_{i}', out_dtypes=(BF16,))
            dz, dwc = _sc_bwd(dy, s['z'], wl['small'], f'sc_conv_bwd_{i}')
            gl['small'] = dwc
            gl['w_in'] = _mm(s['hn'], dz, 'tn', f'sc_din_{i}', out_dtypes=(BF16,), out_stacked=True)
            started = emit_grads(i, 'mixer', gl)
            dhn = _mm(dz, wl['w_in'], 'nt', f'sc_dhn_{i}', dep=started)
        elif kind == 1:
            do, delta = _mm(dh, wl['w_out'], 'nt', f'attn_do_{i}', out_dtypes=(BF16, F32), extras=(s['o'],),
                            epi=_delta_epilogue, tn=d, dep=started)
            gl['w_out'] = _mm(s['o'], dh, 'tn', f'attn_dwo_{i}', out_dtypes=(BF16,))
            dqs, dks, dvs = [], [], []
            for g, (_, dil) in enumerate(DILATED_PATTERNS):
                qg, kg, vg = s['views'][g]
                dqg, dkg, dvg = _attn_bwd(qg, kg, vg, _dilate(do, dil), _dilate(s['lse'], dil), _dilate(delta, dil),
                                          f'attn_bwd_{i}_g{g}')
                dqs.append(dqg.reshape(t, d))
                dks.append(dkg.reshape(t, d))
                dvs.append(dvg.reshape(t, d))
            dqkv = _rope_bwd(dqs, dks, dvs, pos, invf, f'rope_bwd_{i}')
            gl['w_in'] = _mm(s['hn'], dqkv, 'tn', f'attn_dqkv_{i}', out_dtypes=(BF16,), out_stacked=True, tn=1152)
            started = emit_grads(i, 'mixer', gl)
            dhn = _mm(dqkv, wl['w_in'], 'nt', f'attn_dhn_{i}', tk=1152, dep=started)
        else:
            dy = _mm(dh, wl['w_out'], 'nt', f'lru_dy_{i}', tn=1280, dep=started)
            gl['w_out'] = _mm(s['y'], dh, 'tn', f'lru_dout_{i}', out_dtypes=(BF16,), tm=1280)
            sm = wl['small']
            dz, dwa, dwx, dvec = _lru_bwd(dy, s['z'], s['xr'], s['hs'], sm[0:4], w['lru_w_a'][j], sm[5:6],
                                          w['lru_w_x'][j], sm[6:7], sm[7:8], f'lru_bwd_{i}')
            grads['lru_w_a'][j], grads['lru_w_x'][j] = dwa, dwx
            gl['small'] = dvec
            gl['w_in'] = _mm(s['hn'], dz, 'tn', f'lru_din_{i}', out_dtypes=(BF16,), tn=1280)
            started = emit_grads(i, 'mixer', gl)
            dhn = _mm(dz, wl['w_in'], 'nt', f'lru_dhn_{i}', dep=started)
        dh, dg = _rms_bwd(s['h0'], w['norm_mix'][i], dhn, dh, f'rms_mix_bwd_{i}')
        grads['norm_mix'][i] = dg.reshape(d)
        started = None
    return loss, dh, grads


_MESH = pl.DeviceIdType.MESH
_ANY = pl.BlockSpec(memory_space=pl.ANY)


def _block_view(ref, kind, idx):
    if kind == 'stack':
        return ref.at[idx]
    r = ref.shape[0] // N_DEV
    return ref.at[pl.ds(idx * r, r)]


def _gather_many(arrs, kinds, name):
    n = len(arrs)
    out_shapes = [SDS((N_DEV,) + a.shape if kd == 'stack' else (N_DEV * a.shape[0],) + a.shape[1:], a.dtype)
                  for a, kd in zip(arrs, kinds)]

    def body(*refs):
        x_refs, out_refs = refs[:n], refs[n:2 * n]
        send_sems, recv_sems, local_sems = refs[2 * n:]
        x, y, c = lax.axis_index('x'), lax.axis_index('y'), lax.axis_index('c')
        me, sibling = (x, y, c), (x, y, 1 - c)
        chips = [(1 - x, y), (x, 1 - y), (1 - x, 1 - y)]

        def slab(t, px, py, pc):
            return _block_view(out_refs[t], kinds[t], 4 * px + 2 * py + pc)

        def copy(t, k, block, to, src=None):
            return pltpu.make_async_remote_copy(
                src_ref=slab(t, *block) if src is None else src, dst_ref=slab(t, *block),
                send_sem=send_sems.at[7 * t + k], recv_sem=recv_sems.at[7 * t + k], device_id=to, device_id_type=_MESH)

        mine = [pltpu.make_async_copy(x_refs[t], slab(t, *me), local_sems.at[t]) for t in range(n)]
        for cp in mine:
            cp.start()
        first = [copy(t, 0, me, sibling, src=x_refs[t]) for t in range(n)]
        first += [copy(t, 1 + j, me, (*chip, c), src=x_refs[t]) for j, chip in enumerate(chips) for t in range(n)]
        for cp in first:
            cp.start()
        passed = []
        for j, chip in enumerate(chips):
            for t in range(n):
                copy(t, 1 + j, (*chip, c), me).wait_recv()
                passed.append(copy(t, 4 + j, (*chip, c), sibling))
                passed[-1].start()
        for t in range(n):
            copy(t, 0, sibling, me).wait_recv()
            for j, chip in enumerate(chips):
                copy(t, 4 + j, (*chip, 1 - c), me).wait_recv()
        for cp in first + passed:
            cp.wait_send()
        for cp in mine:
            cp.wait()

    return pl.pallas_call(
        body, out_shape=out_shapes, in_specs=[_ANY] * n, out_specs=[_ANY] * n,
        scratch_shapes=[pltpu.SemaphoreType.DMA((7 * n,)), pltpu.SemaphoreType.DMA((7 * n,)), pltpu.SemaphoreType.DMA((n,))],
        name=name)(*arrs)


_HBM = pl.BlockSpec(memory_space=pltpu.HBM)
_SEM = pl.BlockSpec(memory_space=pltpu.SEMAPHORE)
_EFFECT = pltpu.SideEffectType.DATAFLOW_SIDE_EFFECTING


def _direct_copies(mode, kinds, src_refs, land_refs, send_sems, recv_sems):
    x, y, c = lax.axis_index('x'), lax.axis_index('y'), lax.axis_index('c')
    my_idx = 4 * x + 2 * y + c
    copies = []
    for k in range(1, N_DEV):
        px, py, pc = (1 - x if k & 4 else x, 1 - y if k & 2 else y, 1 - c if k & 1 else c)
        for t, kd in enumerate(kinds):
            if mode == 'gather':
                src, dst = src_refs[t], _block_view(land_refs[t], kd, my_idx)
            else:
                src, dst = _block_view(src_refs[t], kd, 4 * px + 2 * py + pc), land_refs[t].at[my_idx]
            copies.append(pltpu.make_async_remote_copy(
                src_ref=src, dst_ref=dst, send_sem=send_sems.at[7 * t + k - 1], recv_sem=recv_sems.at[7 * t + k - 1],
                device_id=(px, py, pc), device_id_type=_MESH))
    return copies


def _own_part(mode, kind, src, land):
    idx = 4 * lax.axis_index('x') + 2 * lax.axis_index('y') + lax.axis_index('c')
    zeros = (0,) * (src.ndim - 1)
    if mode == 'gather':
        part = src
    elif kind == 'stack':
        part = lax.dynamic_index_in_dim(src, idx, 0, keepdims=False)
    else:
        r = src.shape[0] // N_DEV
        part = lax.dynamic_slice_in_dim(src, idx * r, r, 0)
    if mode == 'gather' and kind == 'rows':
        return lax.dynamic_update_slice(land, part, (idx * part.shape[0],) + zeros)
    return lax.dynamic_update_slice(land, part[None], (idx,) + (0,) * part.ndim)


def _send_start(mode, srcs, kinds, name, after=None):
    n = len(srcs)
    after = [] if after is None else [after]
    lands = []
    for a, kd in zip(srcs, kinds):
        if mode == 'gather':
            shape = (N_DEV,) + a.shape if kd == 'stack' else (N_DEV * a.shape[0],) + a.shape[1:]
        else:
            shape = a.shape if kd == 'stack' else (N_DEV, a.shape[0] // N_DEV) + a.shape[1:]
        lands.append(_own_part(mode, kd, a, lax.empty(shape, a.dtype)))

    def body(*refs):
        src_refs, land_refs = refs[:n], refs[n:2 * n]
        send_sems, recv_sems = refs[2 * n + len(after):2 * n + len(after) + 2]
        token = refs[-1]
        for cp in _direct_copies(mode, kinds, src_refs, land_refs, send_sems, recv_sems):
            cp.start()
        token[...] = jnp.zeros_like(token)

    outs = pl.pallas_call(
        body, name=name,
        out_shape=(pltpu.SemaphoreType.DMA((7 * n,)), pltpu.SemaphoreType.DMA((7 * n,)),
                   *[pltpu.HBM(a.shape, a.dtype) for a in srcs + lands], SDS((SUB, 128), F32)),
        in_specs=[_HBM] * (2 * n) + [_ANY] * len(after),
        out_specs=(_SEM, _SEM, *[_HBM] * (2 * n), pl.BlockSpec(memory_space=pltpu.VMEM)),
        input_output_aliases={i: 2 + i for i in range(2 * n)},
        compiler_params=pltpu.CompilerParams(has_side_effects=_EFFECT),
    )(*[pltpu.with_memory_space_constraint(a, pltpu.HBM) for a in srcs + lands], *after)
    return (outs[0], outs[1], list(outs[2:2 + 2 * n])), outs[-1]


def _send_wait(mode, flight, kinds, after, name):
    send, recv, bufs = flight
    n = len(kinds)

    def body(*refs):
        src_refs, land_refs, (send_sems, recv_sems) = refs[:n], refs[n:2 * n], refs[2 * n:2 * n + 2]
        copies = _direct_copies(mode, kinds, src_refs, land_refs, send_sems, recv_sems)
        for cp in copies:
            cp.wait_send()
        for cp in copies:
            cp.wait_recv()

    outs = pl.pallas_call(
        body, name=name, out_shape=[pltpu.HBM(a.shape, a.dtype) for a in bufs],
        in_specs=[_HBM] * (2 * n) + [_SEM, _SEM, _ANY], out_specs=[_HBM] * (2 * n),
        input_output_aliases={i: i for i in range(2 * n)},
        compiler_params=pltpu.CompilerParams(has_side_effects=_EFFECT),
    )(*bufs, send, recv, after)
    return list(outs[n:])


ADAMW_BLOCK_ELEMS = 128 * 1024


def _adamw_sum(wgt, parts, m, v, name):
    nl, r, c = wgt.shape
    assert len(parts) == nl and all(q.shape == (N_DEV, r, c) for q in parts), (name, wgt.shape, [q.shape for q in parts])
    tr = next((t for t in range(min(r, 512), 0, -16) if r % t == 0 and t * c <= ADAMW_BLOCK_ELEMS and t % 16 == 0), r)
    c1 = 1.0 - ADAM_B1 ** ADAM_STEP
    c2 = 1.0 - ADAM_B2 ** ADAM_STEP

    def body(w_ref, m_ref, v_ref, *rest):
        part_refs, (g_ref, d_ref, mo_ref, vo_ref) = rest[:nl], rest[nl:]
        for q in range(nl):
            @pl.when(pl.program_id(0) == q)
            def _(q=q):
                gv = part_refs[q][0].astype(F32)
                for s in range(1, N_DEV):
                    gv = gv + part_refs[q][s].astype(F32)
                mn = ADAM_B1 * m_ref[...] + (1.0 - ADAM_B1) * gv
                vn = ADAM_B2 * v_ref[...] + (1.0 - ADAM_B2) * (gv * gv)
                g_ref[...] = gv
                d_ref[...] = -ADAM_LR * ((mn / c1) / (jnp.sqrt(vn / c2) + ADAM_EPS) + ADAM_WD * w_ref[...])
                mo_ref[...] = mn
                vo_ref[...] = vn

    spec = pl.BlockSpec((None, tr, c), lambda l, i: (l, i, 0))
    part_specs = [pl.BlockSpec((N_DEV, tr, c), lambda l, i, q=q: (0, jnp.where(l == q, i, 0), 0)) for q in range(nl)]
    return pl.pallas_call(body, grid=(nl, r // tr), in_specs=[spec] * 3 + part_specs, out_specs=[spec] * 4,
                          out_shape=[SDS((nl, r, c), F32)] * 4, compiler_params=_params('arbitrary', 'arbitrary'),
                          name=name)(wgt, m, v, *parts)


MIXER_WEIGHTS = {0: ('sc_w_in', 'sc_w_out'), 1: ('attn_w_qkv', 'attn_w_o'), 2: ('lru_w_in', 'lru_w_out')}
STACKED_OPERANDS = ('sc_w_in', 'attn_w_qkv', 'mlp_w_up')
LRU_SMALL = ('lru_conv_w', 'lru_conv_b', 'lru_b_a', 'lru_b_x', 'lru_lambda')


def _layer_items(i):
    w_in, w_out = MIXER_WEIGHTS[i % N_MIXERS]
    j = i // N_MIXERS
    return [('w_in', w_in, j), ('w_out', w_out, j), ('mlp_up', 'mlp_w_up', i), ('mlp_down', 'mlp_w_down', i),
            ('ple_gate', 'ple_w_gate', i), ('ple_proj', 'ple_w_proj', i)]


def _cols_to_full(stacked):
    return jnp.moveaxis(stacked, 0, 1).reshape(stacked.shape[1], -1)


def _full_to_cols(full):
    k, n = full.shape
    return jnp.moveaxis(full.reshape(k, N_DEV, n // N_DEV), 1, 0)


def _pad_to(a, rows):
    return jnp.pad(a, ((0, rows - a.shape[0]), (0, 0)))


def _small_block(src, i):
    kind, j = i % N_MIXERS, i // N_MIXERS
    if kind == 0:
        return _pad_to(src['sc_w_conv'][j], SUB)
    if kind == 2:
        return jnp.concatenate([src[n][j].reshape(-1, src[n].shape[-1]) for n in LRU_SMALL], axis=0)
    return None


def kernel(x, p, positions, norm_mix, norm_mlp, norm_ple, norm_final, sc_w_in, sc_w_conv, sc_w_out, attn_w_qkv, attn_w_o, lru_w_in, lru_conv_w, lru_conv_b, lru_w_a, lru_b_a, lru_w_x, lru_b_x, lru_lambda, lru_w_out, mlp_w_up, mlp_w_down, ple_w_gate, ple_w_proj, loss_target, m_norm_mix, m_norm_mlp, m_norm_ple, m_norm_final, m_sc_w_in, m_sc_w_conv, m_sc_w_out, m_attn_w_qkv, m_attn_w_o, m_lru_w_in, m_lru_conv_w, m_lru_conv_b, m_lru_w_a, m_lru_b_a, m_lru_w_x, m_lru_b_x, m_lru_lambda, m_lru_w_out, m_mlp_w_up, m_mlp_w_down, m_ple_w_gate, m_ple_w_proj, v_norm_mix, v_norm_mlp, v_norm_ple, v_norm_final, v_sc_w_in, v_sc_w_conv, v_sc_w_out, v_attn_w_qkv, v_attn_w_o, v_lru_w_in, v_lru_conv_w, v_lru_conv_b, v_lru_w_a, v_lru_b_a, v_lru_w_x, v_lru_b_x, v_lru_lambda, v_lru_w_out, v_mlp_w_up, v_mlp_w_down, v_ple_w_gate, v_ple_w_proj):
    loc = dict(locals())
    shards = {n: loc[n] for n in WEIGHTS}
    moms = {n: loc['m_' + n] for n in WEIGHTS}
    vels = {n: loc['v_' + n] for n in WEIGHTS}

    depth, t, d = p.shape[0], x.shape[1], x.shape[2]

    def comm_kind(name):
        return 'stack' if SHARD_AXIS[name] == 2 else 'rows'

    def layer_shards(i):
        items = _layer_items(i)
        arrs = [shards[n][idx].astype(BF16) for _, n, idx in items]
        kinds = [comm_kind(n) for _, n, _ in items]
        small = _small_block(shards, i)
        if small is not None:
            arrs.append(small)
            kinds.append('stack')
        return items, arrs, kinds

    def layer_weights(i, items, kinds, outs):
        wl = {key: (_cols_to_full(o) if kd == 'stack' and n not in STACKED_OPERANDS else o)
              for (key, n, _), kd, o in zip(items, kinds, outs)}
        if len(outs) > len(items):
            wl['small'] = _cols_to_full(outs[-1])[:shards['sc_w_conv'].shape[1] if i % N_MIXERS == 0 else SUB]
        return wl

    items0, arrs0, kinds0 = layer_shards(0)
    outs0 = _gather_many(arrs0, kinds0, 'gather_weights_0')
    pending = {}

    def start_gather(i, after):
        if i >= depth:
            return None
        items, arrs, kinds = layer_shards(i)
        flight, token = _send_start('gather', arrs, kinds, f'gather_weights_start_{i}', after=after)
        pending[i] = (items, kinds, flight)
        return token

    first_token = start_gather(1, outs0[0])

    def weights_for_layer(i, h):
        if i == 0:
            return layer_weights(0, items0, kinds0, outs0), first_token
        items, kinds, flight = pending.pop(i)
        outs = _send_wait('gather', flight, kinds, h, f'gather_weights_wait_{i}')
        return layer_weights(i, items, kinds, outs), start_gather(i + 1, outs[0])

    part_keys = {'mlp': ('mlp_up', 'mlp_down', 'ple_gate', 'ple_proj'), 'mixer': ('w_in', 'w_out')}
    exchanges = {}

    def emit_grads(i, part, gl):
        items = [it for it in _layer_items(i) if it[0] in part_keys[part]]
        kinds = [comm_kind(n) for _, n, _ in items]
        arrs = [_full_to_cols(gl[key]) if kd == 'stack' and gl[key].ndim == 2 else gl[key]
                for (key, _, _), kd in zip(items, kinds)]
        if part == 'mixer' and i % N_MIXERS == 0:
            arrs.append(_full_to_cols(_pad_to(gl['small'], SUB)))
        elif part == 'mixer' and i % N_MIXERS == 2:
            dv = gl['small']
            arrs.append(_full_to_cols(jnp.concatenate([dv[4:8], dv[3:4], dv[0:1], dv[1:2], dv[2:3]], axis=0)))
        kinds += ['stack'] * (len(arrs) - len(kinds))
        flight, token = _send_start('exchange', arrs, kinds, f'exchange_grads_start_{part}_{i}')
        exchanges[(i, part)] = (items, kinds, flight)
        return token

    rep = {n: shards[n] for n in ('norm_mix', 'norm_mlp', 'norm_ple', 'norm_final')}
    rep['lru_w_a'], rep['lru_w_x'] = shards['lru_w_a'].astype(BF16), shards['lru_w_x'].astype(BF16)
    loss, grad_x, rgrads = _local_step(x.reshape(t, d), p.reshape(depth, t, p.shape[3]), positions.reshape(t, 1),
                                       loss_target.reshape(t, d), rep, weights_for_layer, emit_grads)

    norm_names = ('norm_mix', 'norm_mlp', 'norm_ple', 'norm_final')
    gate_names = ('lru_w_a', 'lru_w_x')

    def norm_block(src):
        cat = jnp.concatenate([src[n].reshape(-1, d) for n in norm_names], axis=0)
        return _pad_to(cat, -(-cat.shape[0] // HALO) * HALO)

    def gate_block(src):
        return jnp.concatenate([src[n].reshape(-1, LRU_BLOCK) for n in gate_names], axis=0)

    rfull = {n: (rgrads[n] if n == 'norm_final' else jnp.stack(rgrads[n], axis=0)) for n in norm_names + gate_names}
    repl_flight, repl_token = _send_start('gather', [norm_block(rfull), gate_block(rfull)], ['stack', 'stack'],
                                          'gather_replicated_grads_start')
    received = {}
    for (i, part), (items, kinds, flight) in exchanges.items():
        outs = _send_wait('exchange', flight, kinds, repl_token, f'exchange_grads_wait_{part}_{i}')
        for (_, n, idx), o in zip(items, outs):
            received[(n, idx)] = o
        if len(outs) > len(items):
            received[('small', i)] = outs[-1]

    res = {}
    for n in WEIGHTS:
        if SHARD_AXIS[n] is not None and shards[n].ndim == 3 and n not in ('sc_w_conv', 'lru_conv_w'):
            res[n] = _adamw_sum(shards[n], [received[(n, l)] for l in range(shards[n].shape[0])], moms[n], vels[n],
                                f'adamw_{n}')
    def small_adamw(layers, name):
        w_, m_, v_ = (jnp.stack([_small_block(src, i) for i in layers]) for src in (shards, moms, vels))
        return _adamw_sum(w_, [received[('small', i)] for i in layers], m_, v_, name)

    sc = small_adamw([i for i in range(depth) if i % N_MIXERS == 0], 'adamw_sc_w_conv')
    res['sc_w_conv'] = tuple(o[:, :shards['sc_w_conv'].shape[1]] for o in sc)
    lru = small_adamw([i for i in range(depth) if i % N_MIXERS == 2], 'adamw_lru_small')
    row = 0
    for n in LRU_SMALL:
        k = shards[n].size // shards[n].shape[0] // shards[n].shape[-1]
        res[n] = tuple(o[:, row:row + k].reshape(shards[n].shape) for o in lru)
        row += k
    parts_norm, parts_gate = _send_wait('gather', repl_flight, ['stack', 'stack'], lru[0],
                                        'gather_replicated_grads_wait')
    norms = _adamw_sum(norm_block(shards)[None], [parts_norm], norm_block(moms)[None], norm_block(vels)[None],
                       'adamw_norms')
    gates = _adamw_sum(gate_block(shards)[None], [parts_gate], gate_block(moms)[None], gate_block(vels)[None],
                       'adamw_lru_gates')
    for names, outs in ((norm_names, norms), (gate_names, gates)):
        row = 0
        for n in names:
            k = shards[n].size // outs[0].shape[-1]
            res[n] = tuple(o[0, row:row + k].reshape(shards[n].shape) for o in outs)
            row += k

    loss = lax.psum(loss[0, 0], ('x', 'y', 'c'))
    return (loss, grad_x.reshape(x.shape), *[res[n][0] for n in WEIGHTS], *[res[n][1] for n in WEIGHTS],
            *[res[n][2] for n in WEIGHTS], *[res[n][3] for n in WEIGHTS])
```

```python
import functools
import math

import jax
import jax.numpy as jnp
from jax import lax
from jax.experimental import pallas as pl
from jax.experimental.pallas import tpu as pltpu

F32 = jnp.float32
BF16 = jnp.bfloat16
SDS = jax.ShapeDtypeStruct

N_DEV = 8
RMS_EPS = 1e-6
N_MIXERS = 3
HEAD_DIM = 128
DILATED_PATTERNS = ((128, 1), (512, 4), (2048, 16))
ATTN_BLOCK = 128
ROPE_THETA = 500000.0
ROPE_DIM = HEAD_DIM // 4
LRU_BLOCK = 128
LRU_C = 8.0
ADAM_LR, ADAM_B1, ADAM_B2, ADAM_EPS, ADAM_WD, ADAM_STEP = 0.001, 0.9, 0.999, 1e-08, 0.01, 10

HALO = 16
SUB = 8
VMEM_LIMIT = 56 * 1024 * 1024
NEG = -1e30

SHARD_AXIS = {
    'norm_mix': None, 'norm_mlp': None, 'norm_ple': None, 'norm_final': None,
    'sc_w_in': 2, 'sc_w_conv': 2, 'sc_w_out': 1, 'attn_w_qkv': 2, 'attn_w_o': 1,
    'lru_w_in': 2, 'lru_conv_w': 2, 'lru_conv_b': 1, 'lru_w_a': None, 'lru_b_a': 1,
    'lru_w_x': None, 'lru_b_x': 1, 'lru_lambda': 1, 'lru_w_out': 1,
    'mlp_w_up': 2, 'mlp_w_down': 1, 'ple_w_gate': 1, 'ple_w_proj': 2,
}
WEIGHTS = list(SHARD_AXIS)


def _params(*sem):
    return pltpu.CompilerParams(dimension_semantics=sem or None, vmem_limit_bytes=VMEM_LIMIT)


def _row_tile(t, pref=256):
    tr = min(t, pref)
    assert t % tr == 0 and tr % HALO == 0
    return tr


def _row(tr, c, col=0):
    return pl.BlockSpec((tr, c), lambda i, col=col: (i, col))


def _full(shape):
    return pl.BlockSpec(shape, lambda *_: (0,) * len(shape))


def _sigmoid(x):
    return 1.0 / (1.0 + jnp.exp(-x))


def _expm1(x):
    taylor = x * (1.0 + x * (0.5 + x * (1.0 / 6.0 + x * (1.0 / 24.0 + x * (1.0 / 120.0)))))
    return jnp.where(jnp.abs(x) < 0.1, taylor, jnp.exp(x) - 1.0)


def _softplus(x):
    z = jnp.exp(-jnp.abs(x))
    log1p = jnp.where(z < 0.01, z * (1.0 - z * (0.5 - z * (1.0 / 3.0 - z * 0.25))), jnp.log(1.0 + z))
    return jnp.maximum(x, 0.0) + log1p


_GELU_K = math.sqrt(2.0 / math.pi)


def _gelu_and_grad(x):
    inner = _GELU_K * (x + 0.044715 * x * x * x)
    th = jnp.tanh(inner)
    g = 0.5 * x * (1.0 + th)
    dg = 0.5 * (1.0 + th) + 0.5 * x * (1.0 - th * th) * _GELU_K * (1.0 + 3.0 * 0.044715 * x * x)
    return g, dg


def _shift_down(x, k, prev):
    row = lax.broadcasted_iota(jnp.int32, (SUB, x.shape[1]), 0)
    xr = pltpu.roll(x, k, 0)
    top = jnp.where(row < k, pltpu.roll(prev, k, 0), xr[0:SUB])
    return jnp.concatenate([top, xr[SUB:]], axis=0)


def _shift_up(x, k, nxt):
    r = x.shape[0]
    row = lax.broadcasted_iota(jnp.int32, (SUB, x.shape[1]), 0)
    xr = pltpu.roll(x, r - k, 0)
    bot = jnp.where(row >= SUB - k, pltpu.roll(nxt, SUB - k, 0), xr[r - SUB:r])
    return jnp.concatenate([xr[:r - SUB], bot], axis=0)


_DIMS = {'nn': (((1,), (0,)), ((), ())), 'nt': (((1,), (1,)), ((), ())), 'tn': (((0,), (0,)), ((), ()))}


def _pick_tile(dim, pref):
    if dim <= pref:
        return dim
    return next(c for c in range(pref - pref % 128, 0, -128) if dim % c == 0)


def _mm(a, b, dims, name, out_dtypes=(F32,), a_pro=None, extras=(), epi=None, tm=1024, tn=1024, tk=1024,
        out_stacked=False, dep=None):
    deps = [] if dep is None else [dep]
    stacked = b.ndim == 3
    b_rows, b_cols = (b.shape[1], N_DEV * b.shape[2]) if stacked else b.shape
    if dims == 'nn':
        (m, k), (k2, n) = a.shape, (b_rows, b_cols)
    elif dims == 'nt':
        (m, k), (n, k2) = a.shape, (b_rows, b_cols)
    else:
        (k, m), (k2, n) = a.shape, (b_rows, b_cols)
    assert k == k2, (name, a.shape, b.shape)
    assert not (stacked and dims == 'tn') and not (out_stacked and (extras or dims != 'tn'))
    tm = _pick_tile(m, tm)
    tn = _pick_tile(n // N_DEV if (out_stacked or (stacked and dims == 'nn')) else n, tn)
    tk = _pick_tile(k // N_DEV if (stacked and dims == 'nt') else k, tk)
    assert m % tm == 0 and n % tn == 0 and k % tk == 0, (name, m, n, k)
    nk = k // tk
    a_spec = pl.BlockSpec((tk, tm), lambda i, j, kk: (kk, i)) if dims == 'tn' else pl.BlockSpec((tm, tk), lambda i, j, kk: (i, kk))
    if not stacked:
        b_spec = pl.BlockSpec((tn, tk), lambda i, j, kk: (j, kk)) if dims == 'nt' else pl.BlockSpec((tk, tn), lambda i, j, kk: (kk, j))
    elif dims == 'nn':
        per = b.shape[2] // tn
        b_spec = pl.BlockSpec((None, tk, tn), lambda i, j, kk: (j // per, kk, j % per))
    else:
        per = b.shape[2] // tk
        b_spec = pl.BlockSpec((None, tn, tk), lambda i, j, kk: (kk // per, j, kk % per))
    if out_stacked:
        per_o = n // N_DEV // tn
        o_spec = pl.BlockSpec((None, tm, tn), lambda i, j, kk: (j // per_o, i, j % per_o))
        o_shape = (N_DEV, m, n // N_DEV)
    else:
        o_spec = pl.BlockSpec((tm, tn), lambda i, j, kk: (i, j))
        o_shape = (m, n)
    n_ex, n_out = len(extras), len(out_dtypes)
    for e in extras:
        assert e.shape == (m, n), (name, e.shape)

    def body(a_ref, b_ref, *rest):
        rest = rest[len(deps):]
        ex_refs, out_refs = rest[:n_ex], rest[n_ex:n_ex + n_out]
        kk = pl.program_id(2)
        av = a_ref[...]
        if a_pro is not None:
            av = a_pro(av.astype(F32))
        part = lax.dot_general(av.astype(BF16), b_ref[...].astype(BF16), _DIMS[dims], preferred_element_type=F32)

        def finish(res):
            outs = (res,) if epi is None else epi(res, *[e[...] for e in ex_refs])
            for o_ref, o in zip(out_refs, outs):
                o_ref[...] = o.astype(o_ref.dtype)

        if nk == 1:
            finish(part)
        else:
            acc = rest[-1]

            @pl.when(kk == 0)
            def _():
                acc[...] = part

            @pl.when(kk > 0)
            def _():
                acc[...] += part

            @pl.when(kk == nk - 1)
            def _():
                finish(acc[...])

    out = pl.pallas_call(
        body, grid=(m // tm, n // tn, nk),
        in_specs=[a_spec, b_spec] + [_ANY] * len(deps) + [o_spec] * n_ex,
        out_specs=[o_spec] * n_out,
        out_shape=[SDS(o_shape, d) for d in out_dtypes],
        scratch_shapes=[] if nk == 1 else [pltpu.VMEM((tm, tn), F32)],
        compiler_params=_params('parallel', 'parallel', 'arbitrary'), name=name)(a, b, *deps, *extras)
    return out[0] if n_out == 1 else out


def _relu2(u):
    r = jnp.maximum(u, 0.0)
    return r * r


def _rms_fwd(h, g, name):
    t, d = h.shape
    tr = _row_tile(t)

    def body(h_ref, g_ref, o_ref):
        x = h_ref[...]
        r = lax.rsqrt(jnp.mean(x * x, axis=-1, keepdims=True) + RMS_EPS)
        o_ref[...] = (x * r * g_ref[...]).astype(o_ref.dtype)

    return pl.pallas_call(body, grid=(t // tr,), in_specs=[_row(tr, d), _full((1, d))], out_specs=_row(tr, d),
                          out_shape=SDS((t, d), BF16), compiler_params=_params('parallel'), name=name)(h, g.reshape(1, d))


def _rms_bwd(h, g, dhn, dres, name):
    t, d = h.shape
    tr = _row_tile(t)

    def body(h_ref, g_ref, dhn_ref, dres_ref, dh_ref, dg_ref):
        @pl.when(pl.program_id(0) == 0)
        def _():
            dg_ref[...] = jnp.zeros_like(dg_ref)

        x = h_ref[...]
        r = lax.rsqrt(jnp.mean(x * x, axis=-1, keepdims=True) + RMS_EPS)
        dy = dhn_ref[...].astype(F32)
        gy = dy * g_ref[...]
        dx = r * gy - x * (r * r * r) * jnp.mean(gy * x, axis=-1, keepdims=True)
        dh_ref[...] = dres_ref[...] + dx
        dg_ref[...] += jnp.sum(dy * (x * r), axis=0, keepdims=True)

    return pl.pallas_call(body, grid=(t // tr,),
                          in_specs=[_row(tr, d), _full((1, d)), _row(tr, d), _row(tr, d)],
                          out_specs=[_row(tr, d), _full((1, d))],
                          out_shape=[SDS((t, d), F32), SDS((1, d), F32)],
                          compiler_params=_params('arbitrary'), name=name)(h, g.reshape(1, d), dhn, dres)


def _head(h, g, target, name):
    t, d = h.shape
    tr = _row_tile(t)

    def body(h_ref, g_ref, t_ref, dh_ref, loss_ref, dg_ref):
        @pl.when(pl.program_id(0) == 0)
        def _():
            dg_ref[...] = jnp.zeros_like(dg_ref)
            loss_ref[...] = jnp.zeros_like(loss_ref)

        x = h_ref[...]
        gv = g_ref[...]
        r = lax.rsqrt(jnp.mean(x * x, axis=-1, keepdims=True) + RMS_EPS)
        xh = x * r
        e = xh * gv - t_ref[...]
        per_tok = jnp.mean(e * e, axis=-1, keepdims=True)
        loss_ref[...] += jnp.broadcast_to(0.5 * jnp.sum(per_tok, axis=0, keepdims=True), loss_ref.shape)
        dy = e * (1.0 / d)
        gy = dy * gv
        dh_ref[...] = r * gy - x * (r * r * r) * jnp.mean(gy * x, axis=-1, keepdims=True)
        dg_ref[...] += jnp.sum(dy * xh, axis=0, keepdims=True)

    return pl.pallas_call(body, grid=(t // tr,),
                          in_specs=[_row(tr, d), _full((1, d)), _row(tr, d)],
                          out_specs=[_row(tr, d), _full((1, 128)), _full((1, d))],
                          out_shape=[SDS((t, d), F32), SDS((1, 128), F32), SDS((1, d), F32)],
                          compiler_params=_params('arbitrary'), name=name)(h, g.reshape(1, d), target)


def _ple_bwd_gate(dh3, gate, pp, name):
    t, d = dh3.shape
    tr = _row_tile(t)

    def body(dh_ref, g_ref, pp_ref, dpp_ref, dgl_ref):
        dh = dh_ref[...]
        gt = g_ref[...]
        dpp_ref[...] = (dh * gt).astype(dpp_ref.dtype)
        dgl_ref[...] = (dh * pp_ref[...] * gt * (1.0 - gt)).astype(dgl_ref.dtype)

    return pl.pallas_call(body, grid=(t // tr,), in_specs=[_row(tr, d)] * 3, out_specs=[_row(tr, d)] * 2,
                          out_shape=[SDS((t, d), BF16), SDS((t, d), BF16)],
                          compiler_params=_params('parallel'), name=name)(dh3, gate, pp)


def _halo_prev(tr, c, col=0):
    return pl.BlockSpec((HALO, c), lambda i, col=col: (jnp.maximum(i * (tr // HALO) - 1, 0), col))


def _halo_next(tr, c, t, col=0):
    return pl.BlockSpec((HALO, c), lambda i, col=col: (jnp.minimum((i + 1) * (tr // HALO), t // HALO - 1), col))


def _sc_fwd(z, w, name):
    t, c3 = z.shape
    c = c3 // 3
    tr = _row_tile(t)

    def body(z_ref, zp_ref, w_ref, y_ref):
        i = pl.program_id(0)
        zz = z_ref[...]
        gb, cx = zz[:, :c], zz[:, c:2 * c] * zz[:, 2 * c:]
        zp = zp_ref[SUB:HALO, :]
        cxp = jnp.where(i > 0, zp[:, c:2 * c] * zp[:, 2 * c:], 0.0)
        wv = w_ref[...]
        conv = wv[2:3] * cx + wv[1:2] * _shift_down(cx, 1, cxp) + wv[0:1] * _shift_down(cx, 2, cxp)
        y_ref[...] = (gb * conv).astype(y_ref.dtype)

    return pl.pallas_call(body, grid=(t // tr,),
                          in_specs=[_row(tr, c3), _halo_prev(tr, c3), _full((3, c))],
                          out_specs=_row(tr, c), out_shape=SDS((t, c), BF16),
                          compiler_params=_params('parallel'), name=name)(z, z, w)


def _sc_bwd(dy, z, w, name):
    t, c3 = z.shape
    c = c3 // 3
    tr = _row_tile(t)
    nt = t // tr

    def body(dy_ref, dyn_ref, z_ref, zp_ref, zn_ref, w_ref, dz_ref, dw_ref):
        i = pl.program_id(0)

        @pl.when(i == 0)
        def _():
            dw_ref[...] = jnp.zeros_like(dw_ref)

        zz = z_ref[...]
        gb, gc, xi = zz[:, :c], zz[:, c:2 * c], zz[:, 2 * c:]
        cx = gc * xi
        zp = zp_ref[SUB:HALO, :]
        cxp = jnp.where(i > 0, zp[:, c:2 * c] * zp[:, 2 * c:], 0.0)
        wv = w_ref[...]
        cx1, cx2 = _shift_down(cx, 1, cxp), _shift_down(cx, 2, cxp)
        conv = wv[2:3] * cx + wv[1:2] * cx1 + wv[0:1] * cx2
        dyv = dy_ref[...]
        dconv = dyv * gb
        dcn = jnp.where(i < nt - 1, dyn_ref[0:SUB, :] * zn_ref[0:SUB, :c], 0.0)
        dcx = wv[2:3] * dconv + wv[1:2] * _shift_up(dconv, 1, dcn) + wv[0:1] * _shift_up(dconv, 2, dcn)
        dz_ref[:, :c] = (dyv * conv).astype(dz_ref.dtype)
        dz_ref[:, c:2 * c] = (dcx * xi).astype(dz_ref.dtype)
        dz_ref[:, 2 * c:] = (dcx * gc).astype(dz_ref.dtype)
        dw_ref[...] += jnp.concatenate([jnp.sum(dconv * cx2, axis=0, keepdims=True),
                                        jnp.sum(dconv * cx1, axis=0, keepdims=True),
                                        jnp.sum(dconv * cx, axis=0, keepdims=True)], axis=0)

    return pl.pallas_call(body, grid=(nt,),
                          in_specs=[_row(tr, c), _halo_next(tr, c, t), _row(tr, c3), _halo_prev(tr, c3),
                                    _halo_next(tr, c3, t), _full((3, c))],
                          out_specs=[_row(tr, c3), _full((3, c))],
                          out_shape=[SDS((t, c3), BF16), SDS((3, c), F32)],
                          compiler_params=_params('arbitrary'), name=name)(dy, dy, z, z, z, w)


def _perm(tr, dil, inverse=False):
    n = tr // dil
    a = lax.broadcasted_iota(jnp.int32, (tr, tr), 1 if inverse else 0)
    b = lax.broadcasted_iota(jnp.int32, (tr, tr), 0 if inverse else 1)
    return (b == (a % n) * dil + a // n).astype(BF16)


def _permute(pm, x, terms):
    if x.dtype == BF16:
        return jnp.dot(pm, x, preferred_element_type=F32)
    acc = None
    for _ in range(terms):
        part = x.astype(BF16)
        y = jnp.dot(pm, part, preferred_element_type=F32)
        acc = y if acc is None else acc + y
        x = x - part.astype(F32)
    return acc


def _store_dilated(o_ref, y, dil, d):
    n = y.shape[0] // dil
    for rho in range(dil):
        o_ref[:, rho * d:(rho + 1) * d] = y[rho * n:(rho + 1) * n].astype(o_ref.dtype)


def _load_dilated(ref, dil, d):
    return jnp.concatenate([ref[:, rho * d:(rho + 1) * d] for rho in range(dil)], axis=0) if dil > 1 else ref[...]


def _rope_heads(x, lane, cos, sin):
    return jnp.concatenate([_rope_apply(x[:, s:s + HEAD_DIM], lane, cos, sin)
                            for s in range(0, x.shape[1], HEAD_DIM)], axis=1)


def _rope_tables(pos, invf, sign):
    lane = lax.broadcasted_iota(jnp.int32, (pos.shape[0], HEAD_DIM), 1)
    ang = pos.astype(F32) * invf
    half = ROPE_DIM // 2
    cos = jnp.where(lane < ROPE_DIM, jnp.cos(ang), 1.0)
    sin = jnp.sin(ang) * sign
    sin = jnp.where(lane < half, -sin, jnp.where(lane < ROPE_DIM, sin, 0.0))
    return lane, cos, sin


def _rope_apply(x, lane, cos, sin):
    half = ROPE_DIM // 2
    xs = jnp.where(lane < half, pltpu.roll(x, HEAD_DIM - half, 1), pltpu.roll(x, half, 1))
    return x * cos + xs * sin


def _dilated_spec(tr, dil, d):
    return pl.BlockSpec((tr // dil, dil * d), lambda i: (i, 0))


def _rope_fwd(qkv, pos, invf, dils, name):
    t, w3 = qkv.shape
    w, ng = w3 // 3, len(dils)
    d = w // ng
    tr = _row_tile(t)

    def body(q_ref, k_ref, v_ref, pos_ref, invf_ref, *out_refs):
        lane, cos, sin = _rope_tables(pos_ref[...], invf_ref[...], 1.0)
        for g, dil in enumerate(dils):
            cs = slice(g * d, (g + 1) * d)
            vals = [_rope_heads(q_ref[:, cs], lane, cos, sin).astype(BF16),
                    _rope_heads(k_ref[:, cs], lane, cos, sin).astype(BF16), v_ref[:, cs].astype(BF16)]
            if dil > 1:
                pm = _perm(tr, dil)
                vals = [_permute(pm, a, 1) for a in vals]
            for o_ref, a in zip(out_refs[g::ng], vals):
                _store_dilated(o_ref, a, dil, d)

    outs = pl.pallas_call(body, grid=(t // tr,),
                          in_specs=[_row(tr, w, 0), _row(tr, w, 1), _row(tr, w, 2), _row(tr, 1), _full((1, HEAD_DIM))],
                          out_specs=[_dilated_spec(tr, dil, d) for dil in dils] * 3,
                          out_shape=[SDS((t // dil, dil * d), BF16) for dil in dils] * 3,
                          compiler_params=_params('parallel'), name=name)(qkv, qkv, qkv, pos, invf)
    return outs[:ng], outs[ng:2 * ng], outs[2 * ng:]


def _rope_bwd(dqs, dks, dvs, pos, invf, dils, name):
    ng = len(dils)
    t = dqs[0].shape[0] * dils[0]
    d = dqs[0].shape[1] // dils[0]
    w = ng * d
    tr = _row_tile(t)

    def body(*refs):
        dq_refs, dk_refs, dv_refs = refs[:ng], refs[ng:2 * ng], refs[2 * ng:3 * ng]
        pos_ref, invf_ref, o_ref = refs[3 * ng:]
        pos_f = jnp.broadcast_to(pos_ref[...].astype(F32), (tr, HEAD_DIM))
        for g, dil in enumerate(dils):
            pos_g = pos_f if dil == 1 else _permute(_perm(tr, dil), pos_f, 3)
            lane, cos, sin = _rope_tables(pos_g, invf_ref[...], -1.0)
            vals = [_rope_heads(_load_dilated(dq_refs[g], dil, d), lane, cos, sin),
                    _rope_heads(_load_dilated(dk_refs[g], dil, d), lane, cos, sin), _load_dilated(dv_refs[g], dil, d)]
            back = _perm(tr, dil, inverse=True) if dil > 1 else None
            for sec, a in enumerate(vals):
                a = a.astype(BF16)
                if dil > 1:
                    a = _permute(back, a, 1)
                o_ref[:, sec * w + g * d:sec * w + (g + 1) * d] = a.astype(o_ref.dtype)

    return pl.pallas_call(body, grid=(t // tr,),
                          in_specs=[_dilated_spec(tr, dil, d) for dil in dils] * 3 + [_row(tr, 1), _full((1, HEAD_DIM))],
                          out_specs=_row(tr, 3 * w), out_shape=SDS((t, 3 * w), BF16),
                          compiler_params=_params('parallel'), name=name)(*dqs, *dks, *dvs, pos, invf)


def _dilate_many(arrs, dil, terms, out_dtypes, name):
    t, d = arrs[0].shape
    tr = _row_tile(t)
    na = len(arrs)

    def body(*refs):
        pm = _perm(tr, dil)
        for a_ref, o_ref, k in zip(refs[:na], refs[na:], terms):
            _store_dilated(o_ref, _permute(pm, a_ref[...], k), dil, d)

    return pl.pallas_call(body, grid=(t // tr,), in_specs=[_row(tr, d)] * na,
                          out_specs=[_dilated_spec(tr, dil, d)] * na,
                          out_shape=[SDS((t // dil, dil * d), dt) for dt in out_dtypes],
                          compiler_params=_params('parallel'), name=name)(*arrs)


def _attn_masks():
    qi = lax.broadcasted_iota(jnp.int32, (ATTN_BLOCK, ATTN_BLOCK), 0)
    kj = lax.broadcasted_iota(jnp.int32, (ATTN_BLOCK, ATTN_BLOCK), 1)
    return kj >= qi, kj <= qi


def _attn_cols(l, width):
    ncol = width // HEAD_DIM
    cpb = max(1, min(ncol, 32 // (l // ATTN_BLOCK)))
    assert ncol % cpb == 0
    return cpb


def _attn_fwd(q, k, v, name):
    l, width = q.shape
    cpb = _attn_cols(l, width)
    nb = l // ATTN_BLOCK
    scale = HEAD_DIM ** -0.5

    def body(q_ref, k_ref, v_ref, o_ref, lse_ref):
        m_prev, m_cur = _attn_masks()
        for col in range(cpb):
            cs = slice(col * HEAD_DIM, (col + 1) * HEAD_DIM)

            def step(b, carry, cs=cs):
                r0 = pl.multiple_of(b * ATTN_BLOCK, ATTN_BLOCK)
                rp = pl.multiple_of(jnp.maximum(b - 1, 0) * ATTN_BLOCK, ATTN_BLOCK)
                qb = q_ref[pl.ds(r0, ATTN_BLOCK), cs]
                s_p = lax.dot_general(qb, k_ref[pl.ds(rp, ATTN_BLOCK), cs], _DIMS['nt'], preferred_element_type=F32) * scale
                s_c = lax.dot_general(qb, k_ref[pl.ds(r0, ATTN_BLOCK), cs], _DIMS['nt'], preferred_element_type=F32) * scale
                s_p = jnp.where(jnp.logical_and(m_prev, b > 0), s_p, NEG)
                s_c = jnp.where(m_cur, s_c, NEG)
                m = jnp.maximum(jnp.max(s_p, axis=-1, keepdims=True), jnp.max(s_c, axis=-1, keepdims=True))
                p_p, p_c = jnp.exp(s_p - m), jnp.exp(s_c - m)
                den = jnp.sum(p_p, axis=-1, keepdims=True) + jnp.sum(p_c, axis=-1, keepdims=True)
                acc = jnp.dot(p_p.astype(BF16), v_ref[pl.ds(rp, ATTN_BLOCK), cs], preferred_element_type=F32)
                acc += jnp.dot(p_c.astype(BF16), v_ref[pl.ds(r0, ATTN_BLOCK), cs], preferred_element_type=F32)
                o_ref[pl.ds(r0, ATTN_BLOCK), cs] = acc / den
                lse_ref[pl.ds(r0, ATTN_BLOCK), cs] = jnp.broadcast_to(m + jnp.log(den), (ATTN_BLOCK, HEAD_DIM))
                return carry

            lax.fori_loop(0, nb, step, 0, unroll=min(nb, 4))

    spec = pl.BlockSpec((l, cpb * HEAD_DIM), lambda j: (0, j))
    return pl.pallas_call(body, grid=(width // (cpb * HEAD_DIM),), in_specs=[spec] * 3, out_specs=[spec] * 2,
                          out_shape=[SDS((l, width), F32)] * 2,
                          compiler_params=_params('parallel'), name=name)(q, k, v)


def _attn_bwd(q, k, v, do, lse, delta, name):
    l, width = q.shape
    cpb = _attn_cols(l, width)
    nb = l // ATTN_BLOCK
    scale = HEAD_DIM ** -0.5

    def body(q_ref, k_ref, v_ref, do_ref, lse_ref, dl_ref, dq_ref, dk_ref, dv_ref):
        m_prev, m_cur = _attn_masks()
        dk_ref[...] = jnp.zeros_like(dk_ref)
        dv_ref[...] = jnp.zeros_like(dv_ref)
        for col in range(cpb):
            cs = slice(col * HEAD_DIM, (col + 1) * HEAD_DIM)

            def step(b, carry, cs=cs):
                r0 = pl.multiple_of(b * ATTN_BLOCK, ATTN_BLOCK)
                rp = pl.multiple_of(jnp.maximum(b - 1, 0) * ATTN_BLOCK, ATTN_BLOCK)
                qb, dob = q_ref[pl.ds(r0, ATTN_BLOCK), cs], do_ref[pl.ds(r0, ATTN_BLOCK), cs].astype(BF16)
                kp, kc = k_ref[pl.ds(rp, ATTN_BLOCK), cs], k_ref[pl.ds(r0, ATTN_BLOCK), cs]
                vp, vc = v_ref[pl.ds(rp, ATTN_BLOCK), cs], v_ref[pl.ds(r0, ATTN_BLOCK), cs]
                lse_b = lse_ref[pl.ds(r0, ATTN_BLOCK), cs]
                dl_b = dl_ref[pl.ds(r0, ATTN_BLOCK), cs]
                s_p = lax.dot_general(qb, kp, _DIMS['nt'], preferred_element_type=F32) * scale
                s_c = lax.dot_general(qb, kc, _DIMS['nt'], preferred_element_type=F32) * scale
                p_p = jnp.exp(jnp.where(jnp.logical_and(m_prev, b > 0), s_p, NEG) - lse_b)
                p_c = jnp.exp(jnp.where(m_cur, s_c, NEG) - lse_b)
                dp_p = lax.dot_general(dob, vp, _DIMS['nt'], preferred_element_type=F32)
                dp_c = lax.dot_general(dob, vc, _DIMS['nt'], preferred_element_type=F32)
                ds_p = (p_p * (dp_p - dl_b) * scale).astype(BF16)
                ds_c = (p_c * (dp_c - dl_b) * scale).astype(BF16)
                dq_ref[pl.ds(r0, ATTN_BLOCK), cs] = (jnp.dot(ds_p, kp, preferred_element_type=F32)
                                                     + jnp.dot(ds_c, kc, preferred_element_type=F32))
                dk_ref[pl.ds(rp, ATTN_BLOCK), cs] += lax.dot_general(ds_p, qb, _DIMS['tn'], preferred_element_type=F32)
                dk_ref[pl.ds(r0, ATTN_BLOCK), cs] += lax.dot_general(ds_c, qb, _DIMS['tn'], preferred_element_type=F32)
                dv_ref[pl.ds(rp, ATTN_BLOCK), cs] += lax.dot_general(p_p.astype(BF16), dob, _DIMS['tn'], preferred_element_type=F32)
                dv_ref[pl.ds(r0, ATTN_BLOCK), cs] += lax.dot_general(p_c.astype(BF16), dob, _DIMS['tn'], preferred_element_type=F32)
                return carry

            lax.fori_loop(0, nb, step, 0, unroll=min(nb, 2))

    spec = pl.BlockSpec((l, cpb * HEAD_DIM), lambda j: (0, j))
    return pl.pallas_call(body, grid=(width // (cpb * HEAD_DIM),), in_specs=[spec] * 6, out_specs=[spec] * 3,
                          out_shape=[SDS((l, width), F32)] * 3,
                          compiler_params=_params('parallel'), name=name)(q, k, v, do, lse, delta)


def _attn_combine(os_, lses, dils, name):
    ng = len(dils)
    t = os_[0].shape[0] * dils[0]
    d = os_[0].shape[1] // dils[0]
    tr = _row_tile(t)

    def body(*refs):
        o_refs, l_refs, o_out, lse_out = refs[:ng], refs[ng:2 * ng], refs[2 * ng], refs[2 * ng + 1]
        ovs, ls = [], []
        for g, dil in enumerate(dils):
            ov, lv = _load_dilated(o_refs[g], dil, d), _load_dilated(l_refs[g], dil, d)
            if dil > 1:
                back = _perm(tr, dil, inverse=True)
                ov, lv = _permute(back, ov, 2), _permute(back, lv, 3)
            ovs.append(ov)
            ls.append(lv)
        m = functools.reduce(jnp.maximum, ls)
        ws = [jnp.exp(x - m) for x in ls]
        den = functools.reduce(lambda a, b: a + b, ws)
        acc = functools.reduce(lambda a, b: a + b, [w * o for w, o in zip(ws, ovs)])
        o_out[...] = (acc / den).astype(o_out.dtype)
        lse_out[...] = m + jnp.log(den)

    return pl.pallas_call(body, grid=(t // tr,), in_specs=[_dilated_spec(tr, dil, d) for dil in dils] * 2,
                          out_specs=[_row(tr, d)] * 2, out_shape=[SDS((t, d), BF16), SDS((t, d), F32)],
                          compiler_params=_params('parallel'), name=name)(*os_, *lses)


def _delta_epilogue(acc, o):
    prod = acc * o.astype(F32)
    segs = [jnp.broadcast_to(jnp.sum(prod[:, s:s + HEAD_DIM], axis=-1, keepdims=True), (acc.shape[0], HEAD_DIM))
            for s in range(0, acc.shape[1], HEAD_DIM)]
    return acc, jnp.concatenate(segs, axis=-1)


LRU_TILE = 128


def _lru_gates(xr, wa_ref, ba, wx_ref, bx, lam):
    nb = wa_ref.shape[0]
    xb = xr.astype(BF16)
    ra = jnp.concatenate([jnp.dot(xb[:, n * LRU_BLOCK:(n + 1) * LRU_BLOCK], wa_ref[n], preferred_element_type=F32)
                          for n in range(nb)], axis=-1) + ba
    ia = jnp.concatenate([jnp.dot(xb[:, n * LRU_BLOCK:(n + 1) * LRU_BLOCK], wx_ref[n], preferred_element_type=F32)
                          for n in range(nb)], axis=-1) + bx
    r, ig = _sigmoid(ra), _sigmoid(ia)
    sp = _softplus(-lam)
    log_a = -LRU_C * r * sp
    a = jnp.exp(log_a)
    mult = jnp.sqrt(-_expm1(2.0 * log_a))
    return xb, r, ig, sp, a, mult


def _lru_fwd(z, cw, cb, wa, ba, wx, bx, lam, name):
    t, c2 = z.shape
    c = c2 // 2
    nb = c // LRU_BLOCK
    tr = _row_tile(t, LRU_TILE)

    def body(g_ref, x_ref, xp_ref, cw_ref, cb_ref, wa_ref, ba_ref, wx_ref, bx_ref, lam_ref,
             y_ref, hs_ref, xr_ref, car_ref):
        i = pl.program_id(0)

        @pl.when(i == 0)
        def _():
            car_ref[...] = jnp.zeros_like(car_ref)

        x0 = x_ref[...]
        xp = jnp.where(i > 0, xp_ref[SUB:HALO, :], 0.0)
        cwv = cw_ref[...]
        xr = (cb_ref[...] + cwv[3:4] * x0 + cwv[2:3] * _shift_down(x0, 1, xp)
              + cwv[1:2] * _shift_down(x0, 2, xp) + cwv[0:1] * _shift_down(x0, 3, xp))
        xr_ref[...] = xr
        _, _, ig, _, a, mult = _lru_gates(xr, wa_ref, ba_ref[...], wx_ref, bx_ref[...], lam_ref[...])
        u = mult * (ig * xr)
        row = lax.broadcasted_iota(jnp.int32, (SUB, c), 0)
        car = car_ref[...]
        for j in range(tr // SUB):
            ab, ub = a[j * SUB:(j + 1) * SUB], u[j * SUB:(j + 1) * SUB]
            for s in (1, 2, 4):
                a_sh = jnp.where(row >= s, pltpu.roll(ab, s, 0), 1.0)
                u_sh = jnp.where(row >= s, pltpu.roll(ub, s, 0), 0.0)
                ub = ab * u_sh + ub
                ab = ab * a_sh
            hb = ub + ab * car
            hs_ref[j * SUB:(j + 1) * SUB, :] = hb
            car = jnp.broadcast_to(hb[SUB - 1:SUB], (SUB, c))
        car_ref[...] = car
        gl, _ = _gelu_and_grad(g_ref[...])
        y_ref[...] = (hs_ref[...] * gl).astype(y_ref.dtype)

    return pl.pallas_call(
        body, grid=(t // tr,),
        in_specs=[_row(tr, c, 0), _row(tr, c, 1), _halo_prev(tr, c, 1), _full((4, c)), _full((1, c)),
                  _full((nb, LRU_BLOCK, LRU_BLOCK)), _full((1, c)), _full((nb, LRU_BLOCK, LRU_BLOCK)), _full((1, c)), _full((1, c))],
        out_specs=[_row(tr, c)] * 3,
        out_shape=[SDS((t, c), BF16), SDS((t, c), F32), SDS((t, c), F32)],
        scratch_shapes=[pltpu.VMEM((SUB, c), F32)],
        compiler_params=_params('arbitrary'), name=name)(
            z, z, z, cw, cb.reshape(1, c), wa, ba.reshape(1, c), wx, bx.reshape(1, c), lam.reshape(1, c))


def _lru_bwd(dy, z, xr, hs, cw, wa, ba, wx, bx, lam, name):
    t, c2 = z.shape
    c = c2 // 2
    nb = c // LRU_BLOCK
    tr = _row_tile(t, LRU_TILE)
    nt = t // tr

    def rev(col=0):
        return pl.BlockSpec((tr, c), lambda i, col=col: (nt - 1 - i, col))

    def rev_prev(col=0):
        return pl.BlockSpec((HALO, c), lambda i, col=col: (jnp.maximum((nt - 1 - i) * (tr // HALO) - 1, 0), col))

    def body(dy_ref, g_ref, x_ref, xp_ref, xr_ref, hs_ref, hp_ref, cw_ref, wa_ref, ba_ref, wx_ref, bx_ref, lam_ref,
             dz_ref, dwa_ref, dwx_ref, dvec_ref, lcar_ref, ahead_ref, dxhead_ref, lam_s):
        i = pl.program_id(0)
        first_tile = i == nt - 1

        @pl.when(i == 0)
        def _():
            lcar_ref[...] = jnp.zeros_like(lcar_ref)
            ahead_ref[...] = jnp.zeros_like(ahead_ref)
            dxhead_ref[...] = jnp.zeros_like(dxhead_ref)
            dwa_ref[...] = jnp.zeros_like(dwa_ref)
            dwx_ref[...] = jnp.zeros_like(dwx_ref)
            dvec_ref[...] = jnp.zeros_like(dvec_ref)

        xrv = xr_ref[...]
        lamv = lam_ref[...]
        xb, r, ig, sp, a, mult = _lru_gates(xrv, wa_ref, ba_ref[...], wx_ref, bx_ref[...], lamv)
        hsv = hs_ref[...]
        dyv = dy_ref[...]
        gl, dgl = _gelu_and_grad(g_ref[...])
        dhs = dyv * gl
        dz_ref[:, :c] = (dyv * hsv * dgl).astype(dz_ref.dtype)

        a_next = _shift_up(a, 1, ahead_ref[...])
        row = lax.broadcasted_iota(jnp.int32, (SUB, c), 0)
        car = lcar_ref[...]
        for j in reversed(range(tr // SUB)):
            ab, ub = a_next[j * SUB:(j + 1) * SUB], dhs[j * SUB:(j + 1) * SUB]
            for s in (1, 2, 4):
                a_sh = jnp.where(row < SUB - s, pltpu.roll(ab, SUB - s, 0), 1.0)
                u_sh = jnp.where(row < SUB - s, pltpu.roll(ub, SUB - s, 0), 0.0)
                ub = ab * u_sh + ub
                ab = ab * a_sh
            lb = ub + ab * car
            lam_s[j * SUB:(j + 1) * SUB, :] = lb
            car = jnp.broadcast_to(lb[0:1], (SUB, c))
        lcar_ref[...] = car
        ahead_ref[...] = a[0:SUB]
        lmb = lam_s[...]

        hp = jnp.where(first_tile, 0.0, hp_ref[SUB:HALO, :])
        h_prev = _shift_down(hsv, 1, hp)
        d_a = lmb * h_prev
        d_mult = lmb * (ig * xrv)
        d_ixr = lmb * mult
        d_ig = d_ixr * xrv
        dxr = d_ixr * ig
        d_la = d_a * a - d_mult * (a * a) / mult
        d_r = d_la * (-LRU_C * sp)
        d_sp = jnp.sum(d_la * (-LRU_C * r), axis=0, keepdims=True)
        d_ra = d_r * r * (1.0 - r)
        d_ia = d_ig * ig * (1.0 - ig)
        d_rab, d_iab = d_ra.astype(BF16), d_ia.astype(BF16)
        parts = []
        for n in range(nb):
            cs = slice(n * LRU_BLOCK, (n + 1) * LRU_BLOCK)
            parts.append(lax.dot_general(d_rab[:, cs], wa_ref[n], _DIMS['nt'], preferred_element_type=F32)
                         + lax.dot_general(d_iab[:, cs], wx_ref[n], _DIMS['nt'], preferred_element_type=F32))
            dwa_ref[n] += lax.dot_general(xb[:, cs], d_rab[:, cs], _DIMS['tn'], preferred_element_type=F32)
            dwx_ref[n] += lax.dot_general(xb[:, cs], d_iab[:, cs], _DIMS['tn'], preferred_element_type=F32)
        dxr = dxr + jnp.concatenate(parts, axis=-1)

        cwv = cw_ref[...]
        nxt = dxhead_ref[...]
        dx0 = (cwv[3:4] * dxr + cwv[2:3] * _shift_up(dxr, 1, nxt) + cwv[1:2] * _shift_up(dxr, 2, nxt)
               + cwv[0:1] * _shift_up(dxr, 3, nxt))
        dxhead_ref[...] = dxr[0:SUB]
        dz_ref[:, c:] = dx0.astype(dz_ref.dtype)

        x0 = x_ref[...]
        xp = jnp.where(first_tile, 0.0, xp_ref[SUB:HALO, :])
        sums = [jnp.sum(d_ra, axis=0, keepdims=True), jnp.sum(d_ia, axis=0, keepdims=True),
                d_sp * (-_sigmoid(-lamv)), jnp.sum(dxr, axis=0, keepdims=True),
                jnp.sum(dxr * _shift_down(x0, 3, xp), axis=0, keepdims=True),
                jnp.sum(dxr * _shift_down(x0, 2, xp), axis=0, keepdims=True),
                jnp.sum(dxr * _shift_down(x0, 1, xp), axis=0, keepdims=True),
                jnp.sum(dxr * x0, axis=0, keepdims=True)]
        dvec_ref[...] += jnp.concatenate(sums, axis=0)

    wspec = _full((nb, LRU_BLOCK, LRU_BLOCK))
    return pl.pallas_call(
        body, grid=(nt,),
        in_specs=[rev(), rev(0), rev(1), rev_prev(1), rev(), rev(), rev_prev(), _full((4, c)),
                  wspec, _full((1, c)), wspec, _full((1, c)), _full((1, c))],
        out_specs=[pl.BlockSpec((tr, c2), lambda i: (nt - 1 - i, 0)), wspec, wspec, _full((SUB, c))],
        out_shape=[SDS((t, c2), BF16), SDS((nb, LRU_BLOCK, LRU_BLOCK), F32), SDS((nb, LRU_BLOCK, LRU_BLOCK), F32),
                   SDS((SUB, c), F32)],
        scratch_shapes=[pltpu.VMEM((SUB, c), F32), pltpu.VMEM((SUB, c), F32), pltpu.VMEM((SUB, c), F32),
                        pltpu.VMEM((tr, c), F32)],
        compiler_params=_params('arbitrary'), name=name)(
            dy, z, z, z, xr, hs, hs, cw, wa, ba.reshape(1, c), wx, bx.reshape(1, c), lam.reshape(1, c))


def _local_step(x, p, pos, target, rep, weights_for_layer, emit_grads):
    t, d = x.shape
    depth = p.shape[0]
    w = rep
    half = ROPE_DIM // 2
    invf = ROPE_THETA ** (-2.0 * jnp.arange(half, dtype=F32) / ROPE_DIM)
    invf = jnp.concatenate([invf, invf, jnp.zeros((HEAD_DIM - ROPE_DIM,), F32)]).reshape(1, HEAD_DIM)
    dils = tuple(dil for _, dil in DILATED_PATTERNS)
    saved = []
    h = x
    for i in range(depth):
        kind, j = i % N_MIXERS, i // N_MIXERS
        wl, tok = weights_for_layer(i, h)
        s = {'h0': h, 'wl': wl}
        hn = _rms_fwd(h, w['norm_mix'][i], f'rms_mix_fwd_{i}')
        s['hn'] = hn
        if kind == 0:
            z = _mm(hn, wl['w_in'], 'nn', f'sc_in_{i}', dep=tok)
            y = _sc_fwd(z, wl['small'], f'sc_conv_fwd_{i}')
            h1 = _mm(y, wl['w_out'], 'nn', f'sc_out_{i}', extras=(h,), epi=lambda acc, res: (acc + res,))
            s.update(z=z, y=y)
        elif kind == 1:
            qkv = _mm(hn, wl['w_in'], 'nn', f'attn_qkv_{i}', tn=1152, dep=tok)
            qs, ks, vs = _rope_fwd(qkv, pos, invf, dils, f'rope_fwd_{i}')
            views = list(zip(qs, ks, vs))
            os_, lses = zip(*[_attn_fwd(qg, kg, vg, f'attn_fwd_{i}_g{g}') for g, (qg, kg, vg) in enumerate(views)])
            o, lse = _attn_combine(os_, lses, dils, f'attn_combine_{i}')
            h1 = _mm(o, wl['w_out'], 'nn', f'attn_out_{i}', extras=(h,), epi=lambda acc, res: (acc + res,))
            s.update(views=views, o=o, lse=lse)
        else:
            z = _mm(hn, wl['w_in'], 'nn', f'lru_in_{i}', tn=1280, dep=tok)
            sm = wl['small']
            y, hs, xr = _lru_fwd(z, sm[0:4], sm[4:5], w['lru_w_a'][j], sm[5:6], w['lru_w_x'][j], sm[6:7], sm[7:8],
                                 f'lru_fwd_{i}')
            h1 = _mm(y, wl['w_out'], 'nn', f'lru_out_{i}', extras=(h,), epi=lambda acc, res: (acc + res,), tk=640)
            s.update(z=z, y=y, hs=hs, xr=xr)
        s['h1'] = h1
        hm = _rms_fwd(h1, w['norm_mlp'][i], f'rms_mlp_fwd_{i}')
        u = _mm(hm, wl['mlp_up'], 'nn', f'mlp_up_{i}', out_dtypes=(BF16,))
        h2 = _mm(u, wl['mlp_down'], 'nn', f'mlp_down_{i}', a_pro=_relu2, extras=(h1,), epi=lambda acc, res: (acc + res,))
        hp = _rms_fwd(h2, w['norm_ple'][i], f'rms_ple_fwd_{i}')
        pp = _mm(p[i], wl['ple_proj'], 'nn', f'ple_proj_{i}')
        h3, gate = _mm(hp, wl['ple_gate'], 'nn', f'ple_gate_{i}', out_dtypes=(F32, F32), extras=(pp, h2),
                       epi=lambda acc, ppv, res: (res + _sigmoid(acc) * ppv, _sigmoid(acc)))
        s.update(hm=hm, u=u, h2=h2, hp=hp, pp=pp, gate=gate)
        saved.append(s)
        h = h3

    dh, loss, dg_final = _head(h, w['norm_final'], target, 'loss_head')
    grads = {n: [None] * len(w[n]) for n in w if n != 'norm_final'}
    grads['norm_final'] = dg_final.reshape(d)
    started = None
    for i in reversed(range(depth)):
        kind, j = i % N_MIXERS, i // N_MIXERS
        s = saved[i]
        wl, gl = s['wl'], {}
        dpp, dgl = _ple_bwd_gate(dh, s['gate'], s['pp'], f'ple_bwd_gate_{i}')
        gl['ple_proj'] = _mm(p[i], dpp, 'tn', f'ple_dproj_{i}', out_dtypes=(BF16,), dep=started)
        gl['ple_gate'] = _mm(s['hp'], dgl, 'tn', f'ple_dgate_{i}', out_dtypes=(BF16,))
        dhp = _mm(dgl, wl['ple_gate'], 'nt', f'ple_dhp_{i}')
        dh, dg = _rms_bwd(s['h2'], w['norm_ple'][i], dhp, dh, f'rms_ple_bwd_{i}')
        grads['norm_ple'][i] = dg.reshape(d)
        du = _mm(dh, wl['mlp_down'], 'nt', f'mlp_du_{i}', out_dtypes=(BF16,), extras=(s['u'],),
                 epi=lambda acc, uv: (acc * 2.0 * jnp.maximum(uv.astype(F32), 0.0),))
        gl['mlp_down'] = _mm(s['u'], dh, 'tn', f'mlp_ddown_{i}', out_dtypes=(BF16,), a_pro=_relu2)
        gl['mlp_up'] = _mm(s['hm'], du, 'tn', f'mlp_dup_{i}', out_dtypes=(BF16,), out_stacked=True)
        dhm = _mm(du, wl['mlp_up'], 'nt', f'mlp_dhm_{i}')
        dh, dg = _rms_bwd(s['h1'], w['norm_mlp'][i], dhm, dh, f'rms_mlp_bwd_{i}')
        grads['norm_mlp'][i] = dg.reshape(d)
        started = emit_grads(i, 'mlp', gl)
        gl = {}
        if kind == 0:
            dy = _mm(dh, wl['w_out'], 'nt', f'sc_dy_{i}', dep=started)
            gl['w_out'] = _mm(s['y'], dh, 'tn', f'sc_dout_{i}', out_dtypes=(BF16,))
            dz, dwc = _sc_bwd(dy, s['z'], wl['small'], f'sc_conv_bwd_{i}')
            gl['small'] = dwc
            gl['w_in'] = _mm(s['hn'], dz, 'tn', f'sc_din_{i}', out_dtypes=(BF16,), out_stacked=True)
            started = emit_grads(i, 'mixer', gl)
            dhn = _mm(dz, wl['w_in'], 'nt', f'sc_dhn_{i}', dep=started)
        elif kind == 1:
            do, delta = _mm(dh, wl['w_out'], 'nt', f'attn_do_{i}', out_dtypes=(BF16, F32), extras=(s['o'],),
                            epi=_delta_epilogue, tn=d, dep=started)
            gl['w_out'] = _mm(s['o'], dh, 'tn', f'attn_dwo_{i}', out_dtypes=(BF16,))
            rows_in = {1: (do, s['lse'], delta)}
            for dil in dils:
                if dil not in rows_in:
                    rows_in[dil] = _dilate_many([do, s['lse'], delta], dil, (1, 3, 3), (BF16, F32, F32),
                                                f'attn_dilate_{i}_d{dil}')
            dqs, dks, dvs = zip(*[_attn_bwd(*s['views'][g], *rows_in[dil], f'attn_bwd_{i}_g{g}')
                                  for g, dil in enumerate(dils)])
            dqkv = _rope_bwd(dqs, dks, dvs, pos, invf, dils, f'rope_bwd_{i}')
            gl['w_in'] = _mm(s['hn'], dqkv, 'tn', f'attn_dqkv_{i}', out_dtypes=(BF16,), out_stacked=True, tn=1152)
            started = emit_grads(i, 'mixer', gl)
            dhn = _mm(dqkv, wl['w_in'], 'nt', f'attn_dhn_{i}', tk=1152, dep=started)
        else:
            dy = _mm(dh, wl['w_out'], 'nt', f'lru_dy_{i}', tn=1280, dep=started)
            gl['w_out'] = _mm(s['y'], dh, 'tn', f'lru_dout_{i}', out_dtypes=(BF16,), tm=1280)
            sm = wl['small']
            dz, dwa, dwx, dvec = _lru_bwd(dy, s['z'], s['xr'], s['hs'], sm[0:4], w['lru_w_a'][j], sm[5:6],
                                          w['lru_w_x'][j], sm[6:7], sm[7:8], f'lru_bwd_{i}')
            grads['lru_w_a'][j], grads['lru_w_x'][j] = dwa, dwx
            gl['small'] = dvec
            gl['w_in'] = _mm(s['hn'], dz, 'tn', f'lru_din_{i}', out_dtypes=(BF16,), tn=1280)
            started = emit_grads(i, 'mixer', gl)
            dhn = _mm(dz, wl['w_in'], 'nt', f'lru_dhn_{i}', dep=started)
        dh, dg = _rms_bwd(s['h0'], w['norm_mix'][i], dhn, dh, f'rms_mix_bwd_{i}')
        grads['norm_mix'][i] = dg.reshape(d)
        started = None
    return loss, dh, grads


_MESH = pl.DeviceIdType.MESH
_ANY = pl.BlockSpec(memory_space=pl.ANY)


def _block_view(ref, kind, idx):
    if kind == 'stack':
        return ref.at[idx]
    r = ref.shape[0] // N_DEV
    return ref.at[pl.ds(idx * r, r)]


def _gather_many(arrs, kinds, name):
    n = len(arrs)
    out_shapes = [SDS((N_DEV,) + a.shape if kd == 'stack' else (N_DEV * a.shape[0],) + a.shape[1:], a.dtype)
                  for a, kd in zip(arrs, kinds)]

    def body(*refs):
        x_refs, out_refs = refs[:n], refs[n:2 * n]
        send_sems, recv_sems, local_sems = refs[2 * n:]
        x, y, c = lax.axis_index('x'), lax.axis_index('y'), lax.axis_index('c')
        me, sibling = (x, y, c), (x, y, 1 - c)
        chips = [(1 - x, y), (x, 1 - y), (1 - x, 1 - y)]

        def slab(t, px, py, pc):
            return _block_view(out_refs[t], kinds[t], 4 * px + 2 * py + pc)

        def copy(t, k, block, to, src=None):
            return pltpu.make_async_remote_copy(
                src_ref=slab(t, *block) if src is None else src, dst_ref=slab(t, *block),
                send_sem=send_sems.at[7 * t + k], recv_sem=recv_sems.at[7 * t + k], device_id=to, device_id_type=_MESH)

        mine = [pltpu.make_async_copy(x_refs[t], slab(t, *me), local_sems.at[t]) for t in range(n)]
        for cp in mine:
            cp.start()
        first = [copy(t, 0, me, sibling, src=x_refs[t]) for t in range(n)]
        first += [copy(t, 1 + j, me, (*chip, c), src=x_refs[t]) for j, chip in enumerate(chips) for t in range(n)]
        for cp in first:
            cp.start()
        passed = []
        for j, chip in enumerate(chips):
            for t in range(n):
                copy(t, 1 + j, (*chip, c), me).wait_recv()
                passed.append(copy(t, 4 + j, (*chip, c), sibling))
                passed[-1].start()
        for t in range(n):
            copy(t, 0, sibling, me).wait_recv()
            for j, chip in enumerate(chips):
                copy(t, 4 + j, (*chip, 1 - c), me).wait_recv()
        for cp in first + passed:
            cp.wait_send()
        for cp in mine:
            cp.wait()

    return pl.pallas_call(
        body, out_shape=out_shapes, in_specs=[_ANY] * n, out_specs=[_ANY] * n,
        scratch_shapes=[pltpu.SemaphoreType.DMA((7 * n,)), pltpu.SemaphoreType.DMA((7 * n,)), pltpu.SemaphoreType.DMA((n,))],
        name=name)(*arrs)


_HBM = pl.BlockSpec(memory_space=pltpu.HBM)
_SEM = pl.BlockSpec(memory_space=pltpu.SEMAPHORE)
_EFFECT = pltpu.SideEffectType.DATAFLOW_SIDE_EFFECTING


def _direct_copies(mode, kinds, src_refs, land_refs, send_sems, recv_sems):
    x, y, c = lax.axis_index('x'), lax.axis_index('y'), lax.axis_index('c')
    my_idx = 4 * x + 2 * y + c
    copies = []
    for k in range(1, N_DEV):
        px, py, pc = (1 - x if k & 4 else x, 1 - y if k & 2 else y, 1 - c if k & 1 else c)
        for t, kd in enumerate(kinds):
            if mode == 'gather':
                src, dst = src_refs[t], _block_view(land_refs[t], kd, my_idx)
            else:
                src, dst = _block_view(src_refs[t], kd, 4 * px + 2 * py + pc), land_refs[t].at[my_idx]
            copies.append(pltpu.make_async_remote_copy(
                src_ref=src, dst_ref=dst, send_sem=send_sems.at[7 * t + k - 1], recv_sem=recv_sems.at[7 * t + k - 1],
                device_id=(px, py, pc), device_id_type=_MESH))
    return copies


def _own_part(mode, kind, src, land):
    idx = 4 * lax.axis_index('x') + 2 * lax.axis_index('y') + lax.axis_index('c')
    zeros = (0,) * (src.ndim - 1)
    if mode == 'gather':
        part = src
    elif kind == 'stack':
        part = lax.dynamic_index_in_dim(src, idx, 0, keepdims=False)
    else:
        r = src.shape[0] // N_DEV
        part = lax.dynamic_slice_in_dim(src, idx * r, r, 0)
    if mode == 'gather' and kind == 'rows':
        return lax.dynamic_update_slice(land, part, (idx * part.shape[0],) + zeros)
    return lax.dynamic_update_slice(land, part[None], (idx,) + (0,) * part.ndim)


def _send_start(mode, srcs, kinds, name, after=None):
    n = len(srcs)
    after = [] if after is None else [after]
    lands = []
    for a, kd in zip(srcs, kinds):
        if mode == 'gather':
            shape = (N_DEV,) + a.shape if kd == 'stack' else (N_DEV * a.shape[0],) + a.shape[1:]
        else:
            shape = a.shape if kd == 'stack' else (N_DEV, a.shape[0] // N_DEV) + a.shape[1:]
        lands.append(_own_part(mode, kd, a, lax.empty(shape, a.dtype)))

    def body(*refs):
        src_refs, land_refs = refs[:n], refs[n:2 * n]
        send_sems, recv_sems = refs[2 * n + len(after):2 * n + len(after) + 2]
        token = refs[-1]
        for cp in _direct_copies(mode, kinds, src_refs, land_refs, send_sems, recv_sems):
            cp.start()
        token[...] = jnp.zeros_like(token)

    outs = pl.pallas_call(
        body, name=name,
        out_shape=(pltpu.SemaphoreType.DMA((7 * n,)), pltpu.SemaphoreType.DMA((7 * n,)),
                   *[pltpu.HBM(a.shape, a.dtype) for a in srcs + lands], SDS((SUB, 128), F32)),
        in_specs=[_HBM] * (2 * n) + [_ANY] * len(after),
        out_specs=(_SEM, _SEM, *[_HBM] * (2 * n), pl.BlockSpec(memory_space=pltpu.VMEM)),
        input_output_aliases={i: 2 + i for i in range(2 * n)},
        compiler_params=pltpu.CompilerParams(has_side_effects=_EFFECT),
    )(*[pltpu.with_memory_space_constraint(a, pltpu.HBM) for a in srcs + lands], *after)
    return (outs[0], outs[1], list(outs[2:2 + 2 * n])), outs[-1]


def _send_wait(mode, flight, kinds, after, name):
    send, recv, bufs = flight
    n = len(kinds)

    def body(*refs):
        src_refs, land_refs, (send_sems, recv_sems) = refs[:n], refs[n:2 * n], refs[2 * n:2 * n + 2]
        copies = _direct_copies(mode, kinds, src_refs, land_refs, send_sems, recv_sems)
        for cp in copies:
            cp.wait_send()
        for cp in copies:
            cp.wait_recv()

    outs = pl.pallas_call(
        body, name=name, out_shape=[pltpu.HBM(a.shape, a.dtype) for a in bufs],
        in_specs=[_HBM] * (2 * n) + [_SEM, _SEM, _ANY], out_specs=[_HBM] * (2 * n),
        input_output_aliases={i: i for i in range(2 * n)},
        compiler_params=pltpu.CompilerParams(has_side_effects=_EFFECT),
    )(*bufs, send, recv, after)
    return list(outs[n:])


ADAMW_BLOCK_ELEMS = 128 * 1024


def _adamw_sum(wgt, parts, m, v, name):
    nl, r, c = wgt.shape
    assert len(parts) == nl and all(q.shape == (N_DEV, r, c) for q in parts), (name, wgt.shape, [q.shape for q in parts])
    tr = next((t for t in range(min(r, 512), 0, -16) if r % t == 0 and t * c <= ADAMW_BLOCK_ELEMS and t % 16 == 0), r)
    c1 = 1.0 - ADAM_B1 ** ADAM_STEP
    c2 = 1.0 - ADAM_B2 ** ADAM_STEP

    def body(w_ref, m_ref, v_ref, *rest):
        part_refs, (g_ref, d_ref, mo_ref, vo_ref) = rest[:nl], rest[nl:]
        for q in range(nl):
            @pl.when(pl.program_id(0) == q)
            def _(q=q):
                gv = part_refs[q][0].astype(F32)
                for s in range(1, N_DEV):
                    gv = gv + part_refs[q][s].astype(F32)
                mn = ADAM_B1 * m_ref[...] + (1.0 - ADAM_B1) * gv
                vn = ADAM_B2 * v_ref[...] + (1.0 - ADAM_B2) * (gv * gv)
                g_ref[...] = gv
                d_ref[...] = -ADAM_LR * ((mn / c1) / (jnp.sqrt(vn / c2) + ADAM_EPS) + ADAM_WD * w_ref[...])
                mo_ref[...] = mn
                vo_ref[...] = vn

    spec = pl.BlockSpec((None, tr, c), lambda l, i: (l, i, 0))
    part_specs = [pl.BlockSpec((N_DEV, tr, c), lambda l, i, q=q: (0, jnp.where(l == q, i, 0), 0)) for q in range(nl)]
    return pl.pallas_call(body, grid=(nl, r // tr), in_specs=[spec] * 3 + part_specs, out_specs=[spec] * 4,
                          out_shape=[SDS((nl, r, c), F32)] * 4, compiler_params=_params('arbitrary', 'arbitrary'),
                          name=name)(wgt, m, v, *parts)


MIXER_WEIGHTS = {0: ('sc_w_in', 'sc_w_out'), 1: ('attn_w_qkv', 'attn_w_o'), 2: ('lru_w_in', 'lru_w_out')}
STACKED_OPERANDS = ('sc_w_in', 'attn_w_qkv', 'mlp_w_up')
LRU_SMALL = ('lru_conv_w', 'lru_conv_b', 'lru_b_a', 'lru_b_x', 'lru_lambda')


def _layer_items(i):
    w_in, w_out = MIXER_WEIGHTS[i % N_MIXERS]
    j = i // N_MIXERS
    return [('w_in', w_in, j), ('w_out', w_out, j), ('mlp_up', 'mlp_w_up', i), ('mlp_down', 'mlp_w_down', i),
            ('ple_gate', 'ple_w_gate', i), ('ple_proj', 'ple_w_proj', i)]


def _cols_to_full(stacked):
    return jnp.moveaxis(stacked, 0, 1).reshape(stacked.shape[1], -1)


def _full_to_cols(full):
    k, n = full.shape
    return jnp.moveaxis(full.reshape(k, N_DEV, n // N_DEV), 1, 0)


def _pad_to(a, rows):
    return jnp.pad(a, ((0, rows - a.shape[0]), (0, 0)))


def _small_block(src, i):
    kind, j = i % N_MIXERS, i // N_MIXERS
    if kind == 0:
        return _pad_to(src['sc_w_conv'][j], SUB)
    if kind == 2:
        return jnp.concatenate([src[n][j].reshape(-1, src[n].shape[-1]) for n in LRU_SMALL], axis=0)
    return None


def kernel(x, p, positions, norm_mix, norm_mlp, norm_ple, norm_final, sc_w_in, sc_w_conv, sc_w_out, attn_w_qkv, attn_w_o, lru_w_in, lru_conv_w, lru_conv_b, lru_w_a, lru_b_a, lru_w_x, lru_b_x, lru_lambda, lru_w_out, mlp_w_up, mlp_w_down, ple_w_gate, ple_w_proj, loss_target, m_norm_mix, m_norm_mlp, m_norm_ple, m_norm_final, m_sc_w_in, m_sc_w_conv, m_sc_w_out, m_attn_w_qkv, m_attn_w_o, m_lru_w_in, m_lru_conv_w, m_lru_conv_b, m_lru_w_a, m_lru_b_a, m_lru_w_x, m_lru_b_x, m_lru_lambda, m_lru_w_out, m_mlp_w_up, m_mlp_w_down, m_ple_w_gate, m_ple_w_proj, v_norm_mix, v_norm_mlp, v_norm_ple, v_norm_final, v_sc_w_in, v_sc_w_conv, v_sc_w_out, v_attn_w_qkv, v_attn_w_o, v_lru_w_in, v_lru_conv_w, v_lru_conv_b, v_lru_w_a, v_lru_b_a, v_lru_w_x, v_lru_b_x, v_lru_lambda, v_lru_w_out, v_mlp_w_up, v_mlp_w_down, v_ple_w_gate, v_ple_w_proj):
    loc = dict(locals())
    shards = {n: loc[n] for n in WEIGHTS}
    moms = {n: loc['m_' + n] for n in WEIGHTS}
    vels = {n: loc['v_' + n] for n in WEIGHTS}

    depth, t, d = p.shape[0], x.shape[1], x.shape[2]

    def comm_kind(name):
        return 'stack' if SHARD_AXIS[name] == 2 else 'rows'

    def layer_shards(i):
        items = _layer_items(i)
        arrs = [shards[n][idx].astype(BF16) for _, n, idx in items]
        kinds = [comm_kind(n) for _, n, _ in items]
        small = _small_block(shards, i)
        if small is not None:
            arrs.append(small)
            kinds.append('stack')
        return items, arrs, kinds

    def layer_weights(i, items, kinds, outs):
        wl = {key: (_cols_to_full(o) if kd == 'stack' and n not in STACKED_OPERANDS else o)
              for (key, n, _), kd, o in zip(items, kinds, outs)}
        if len(outs) > len(items):
            wl['small'] = _cols_to_full(outs[-1])[:shards['sc_w_conv'].shape[1] if i % N_MIXERS == 0 else SUB]
        return wl

    items0, arrs0, kinds0 = layer_shards(0)
    outs0 = _gather_many(arrs0, kinds0, 'gather_weights_0')
    pending = {}

    def start_gather(i, after):
        if i >= depth:
            return None
        items, arrs, kinds = layer_shards(i)
        flight, token = _send_start('gather', arrs, kinds, f'gather_weights_start_{i}', after=after)
        pending[i] = (items, kinds, flight)
        return token

    first_token = start_gather(1, outs0[0])

    def weights_for_layer(i, h):
        if i == 0:
            return layer_weights(0, items0, kinds0, outs0), first_token
        items, kinds, flight = pending.pop(i)
        outs = _send_wait('gather', flight, kinds, h, f'gather_weights_wait_{i}')
        return layer_weights(i, items, kinds, outs), start_gather(i + 1, outs[0])

    part_keys = {'mlp': ('mlp_up', 'mlp_down', 'ple_gate', 'ple_proj'), 'mixer': ('w_in', 'w_out')}
    exchanges = {}

    def emit_grads(i, part, gl):
        items = [it for it in _layer_items(i) if it[0] in part_keys[part]]
        kinds = [comm_kind(n) for _, n, _ in items]
        arrs = [_full_to_cols(gl[key]) if kd == 'stack' and gl[key].ndim == 2 else gl[key]
                for (key, _, _), kd in zip(items, kinds)]
        if part == 'mixer' and i % N_MIXERS == 0:
            arrs.append(_full_to_cols(_pad_to(gl['small'], SUB)))
        elif part == 'mixer' and i % N_MIXERS == 2:
            dv = gl['small']
            arrs.append(_full_to_cols(jnp.concatenate([dv[4:8], dv[3:4], dv[0:1], dv[1:2], dv[2:3]], axis=0)))
        kinds += ['stack'] * (len(arrs) - len(kinds))
        flight, token = _send_start('exchange', arrs, kinds, f'exchange_grads_start_{part}_{i}')
        exchanges[(i, part)] = (items, kinds, flight)
        return token

    rep = {n: shards[n] for n in ('norm_mix', 'norm_mlp', 'norm_ple', 'norm_final')}
    rep['lru_w_a'], rep['lru_w_x'] = shards['lru_w_a'].astype(BF16), shards['lru_w_x'].astype(BF16)
    loss, grad_x, rgrads = _local_step(x.reshape(t, d), p.reshape(depth, t, p.shape[3]), positions.reshape(t, 1),
                                       loss_target.reshape(t, d), rep, weights_for_layer, emit_grads)

    norm_names = ('norm_mix', 'norm_mlp', 'norm_ple', 'norm_final')
    gate_names = ('lru_w_a', 'lru_w_x')

    def norm_block(src):
        cat = jnp.concatenate([src[n].reshape(-1, d) for n in norm_names], axis=0)
        return _pad_to(cat, -(-cat.shape[0] // HALO) * HALO)

    def gate_block(src):
        return jnp.concatenate([src[n].reshape(-1, LRU_BLOCK) for n in gate_names], axis=0)

    rfull = {n: (rgrads[n] if n == 'norm_final' else jnp.stack(rgrads[n], axis=0)) for n in norm_names + gate_names}
    repl_flight, repl_token = _send_start('gather', [norm_block(rfull), gate_block(rfull)], ['stack', 'stack'],
                                          'gather_replicated_grads_start')
    received = {}
    for (i, part), (items, kinds, flight) in exchanges.items():
        outs = _send_wait('exchange', flight, kinds, repl_token, f'exchange_grads_wait_{part}_{i}')
        for (_, n, idx), o in zip(items, outs):
            received[(n, idx)] = o
        if len(outs) > len(items):
            received[('small', i)] = outs[-1]

    res = {}
    for n in WEIGHTS:
        if SHARD_AXIS[n] is not None and shards[n].ndim == 3 and n not in ('sc_w_conv', 'lru_conv_w'):
            res[n] = _adamw_sum(shards[n], [received[(n, l)] for l in range(shards[n].shape[0])], moms[n], vels[n],
                                f'adamw_{n}')
    def small_adamw(layers, name):
        w_, m_, v_ = (jnp.stack([_small_block(src, i) for i in layers]) for src in (shards, moms, vels))
        return _adamw_sum(w_, [received[('small', i)] for i in layers], m_, v_, name)

    sc = small_adamw([i for i in range(depth) if i % N_MIXERS == 0], 'adamw_sc_w_conv')
    res['sc_w_conv'] = tuple(o[:, :shards['sc_w_conv'].shape[1]] for o in sc)
    lru = small_adamw([i for i in range(depth) if i % N_MIXERS == 2], 'adamw_lru_small')
    row = 0
    for n in LRU_SMALL:
        k = shards[n].size // shards[n].shape[0] // shards[n].shape[-1]
        res[n] = tuple(o[:, row:row + k].reshape(shards[n].shape) for o in lru)
        row += k
    parts_norm, parts_gate = _send_wait('gather', repl_flight, ['stack', 'stack'], lru[0],
                                        'gather_replicated_grads_wait')
    norms = _adamw_sum(norm_block(shards)[None], [parts_norm], norm_block(moms)[None], norm_block(vels)[None],
                       'adamw_norms')
    gates = _adamw_sum(gate_block(shards)[None], [parts_gate], gate_block(moms)[None], gate_block(vels)[None],
                       'adamw_lru_gates')
    for names, outs in ((norm_names, norms), (gate_names, gates)):
        row = 0
        for n in names:
            k = shards[n].size // outs[0].shape[-1]
            res[n] = tuple(o[0, row:row + k].reshape(shards[n].shape) for o in outs)
            row += k

    loss = lax.psum(loss[0, 0], ('x', 'y', 'c'))
    return (loss, grad_x.reshape(x.shape), *[res[n][0] for n in WEIGHTS], *[res[n][1] for n in WEIGHTS],
            *[res[n][2] for n in WEIGHTS], *[res[n][3] for n in WEIGHTS])
```

```python
import functools
import math

import jax
import jax.numpy as jnp
from jax import lax
from jax.experimental import pallas as pl
from jax.experimental.pallas import tpu as pltpu

F32 = jnp.float32
BF16 = jnp.bfloat16
SDS = jax.ShapeDtypeStruct

N_DEV = 8
RMS_EPS = 1e-6
N_MIXERS = 3
HEAD_DIM = 128
DILATED_PATTERNS = ((128, 1), (512, 4), (2048, 16))
ATTN_BLOCK = 128
ROPE_THETA = 500000.0
ROPE_DIM = HEAD_DIM // 4
LRU_BLOCK = 128
LRU_C = 8.0
ADAM_LR, ADAM_B1, ADAM_B2, ADAM_EPS, ADAM_WD, ADAM_STEP = 0.001, 0.9, 0.999, 1e-08, 0.01, 10

HALO = 16
SUB = 8
VMEM_LIMIT = 56 * 1024 * 1024
NEG = -1e30

SHARD_AXIS = {
    'norm_mix': None, 'norm_mlp': None, 'norm_ple': None, 'norm_final': None,
    'sc_w_in': 2, 'sc_w_conv': 2, 'sc_w_out': 1, 'attn_w_qkv': 2, 'attn_w_o': 1,
    'lru_w_in': 2, 'lru_conv_w': 2, 'lru_conv_b': 1, 'lru_w_a': None, 'lru_b_a': 1,
    'lru_w_x': None, 'lru_b_x': 1, 'lru_lambda': 1, 'lru_w_out': 1,
    'mlp_w_up': 2, 'mlp_w_down': 1, 'ple_w_gate': 1, 'ple_w_proj': 2,
}
WEIGHTS = list(SHARD_AXIS)


def _params(*sem):
    return pltpu.CompilerParams(dimension_semantics=sem or None, vmem_limit_bytes=VMEM_LIMIT)


def _row_tile(t, pref=256):
    tr = min(t, pref)
    assert t % tr == 0 and tr % HALO == 0
    return tr


def _row(tr, c, col=0):
    return pl.BlockSpec((tr, c), lambda i, col=col: (i, col))


def _full(shape):
    return pl.BlockSpec(shape, lambda *_: (0,) * len(shape))


def _sigmoid(x):
    return 1.0 / (1.0 + jnp.exp(-x))


def _expm1(x):
    taylor = x * (1.0 + x * (0.5 + x * (1.0 / 6.0 + x * (1.0 / 24.0 + x * (1.0 / 120.0)))))
    return jnp.where(jnp.abs(x) < 0.1, taylor, jnp.exp(x) - 1.0)


def _softplus(x):
    z = jnp.exp(-jnp.abs(x))
    log1p = jnp.where(z < 0.01, z * (1.0 - z * (0.5 - z * (1.0 / 3.0 - z * 0.25))), jnp.log(1.0 + z))
    return jnp.maximum(x, 0.0) + log1p


_GELU_K = math.sqrt(2.0 / math.pi)


def _gelu_and_grad(x):
    inner = _GELU_K * (x + 0.044715 * x * x * x)
    th = jnp.tanh(inner)
    g = 0.5 * x * (1.0 + th)
    dg = 0.5 * (1.0 + th) + 0.5 * x * (1.0 - th * th) * _GELU_K * (1.0 + 3.0 * 0.044715 * x * x)
    return g, dg


def _shift_down(x, k, prev):
    row = lax.broadcasted_iota(jnp.int32, (SUB, x.shape[1]), 0)
    xr = pltpu.roll(x, k, 0)
    top = jnp.where(row < k, pltpu.roll(prev, k, 0), xr[0:SUB])
    return jnp.concatenate([top, xr[SUB:]], axis=0)


def _shift_up(x, k, nxt):
    r = x.shape[0]
    row = lax.broadcasted_iota(jnp.int32, (SUB, x.shape[1]), 0)
    xr = pltpu.roll(x, r - k, 0)
    bot = jnp.where(row >= SUB - k, pltpu.roll(nxt, SUB - k, 0), xr[r - SUB:r])
    return jnp.concatenate([xr[:r - SUB], bot], axis=0)


_DIMS = {'nn': (((1,), (0,)), ((), ())), 'nt': (((1,), (1,)), ((), ())), 'tn': (((0,), (0,)), ((), ()))}


MM_VMEM_BUDGET = 50 * 1024 * 1024
MM_MIN_TK = 1024
MM_MIN_TM = 1024


def _tile_options(dim):
    return [c for c in range(dim, 127, -128) if dim % c == 0] or [dim]


def _choose_tiles(m, n, k, n_span, k_span, a_size, b_size, mn_size, a_temp):
    best = None
    for tm in _tile_options(m):
        for tn in _tile_options(n_span):
            for tk in _tile_options(k_span):
                nk = k // tk
                need = (2 * (tm * tk * a_size + tk * tn * b_size + tm * tn * mn_size) + tm * tn * 4 * (1 + (nk > 1))
                        + tm * tk * 4 * a_temp)
                score = (-min(tk, MM_MIN_TK), -min(tm, MM_MIN_TM), -tm * tn, nk, -min(tm, 2 * MM_MIN_TM), -tn)
                if need <= MM_VMEM_BUDGET and (best is None or score < best[0]):
                    best = (score, (tm, tn, tk))
    return best[1]


def _mm(a, b, dims, name, out_dtypes=(F32,), a_pro=None, extras=(), epi=None, out_stacked=False, dep=None):
    deps = [] if dep is None else [dep]
    stacked = b.ndim == 3
    b_rows, b_cols = (b.shape[1], N_DEV * b.shape[2]) if stacked else b.shape
    if dims == 'nn':
        (m, k), (k2, n) = a.shape, (b_rows, b_cols)
    elif dims == 'nt':
        (m, k), (n, k2) = a.shape, (b_rows, b_cols)
    else:
        (k, m), (k2, n) = a.shape, (b_rows, b_cols)
    assert k == k2, (name, a.shape, b.shape)
    assert not (stacked and dims == 'tn') and not (out_stacked and (extras or dims != 'tn'))
    tm, tn, tk = _choose_tiles(
        m, n, k, n // N_DEV if (out_stacked or (stacked and dims == 'nn')) else n,
        k // N_DEV if (stacked and dims == 'nt') else k, a.dtype.itemsize, b.dtype.itemsize,
        sum(e.dtype.itemsize for e in extras) + sum(jnp.dtype(dt).itemsize for dt in out_dtypes),
        a_pro is not None or a.dtype != BF16)
    assert m % tm == 0 and n % tn == 0 and k % tk == 0, (name, m, n, k)
    nk = k // tk
    a_spec = pl.BlockSpec((tk, tm), lambda i, j, kk: (kk, i)) if dims == 'tn' else pl.BlockSpec((tm, tk), lambda i, j, kk: (i, kk))
    if not stacked:
        b_spec = pl.BlockSpec((tn, tk), lambda i, j, kk: (j, kk)) if dims == 'nt' else pl.BlockSpec((tk, tn), lambda i, j, kk: (kk, j))
    elif dims == 'nn':
        per = b.shape[2] // tn
        b_spec = pl.BlockSpec((None, tk, tn), lambda i, j, kk: (j // per, kk, j % per))
    else:
        per = b.shape[2] // tk
        b_spec = pl.BlockSpec((None, tn, tk), lambda i, j, kk: (kk // per, j, kk % per))
    if out_stacked:
        per_o = n // N_DEV // tn
        o_spec = pl.BlockSpec((None, tm, tn), lambda i, j, kk: (j // per_o, i, j % per_o))
        o_shape = (N_DEV, m, n // N_DEV)
    else:
        o_spec = pl.BlockSpec((tm, tn), lambda i, j, kk: (i, j))
        o_shape = (m, n)
    n_ex, n_out = len(extras), len(out_dtypes)
    for e in extras:
        assert e.shape == (m, n), (name, e.shape)

    def body(a_ref, b_ref, *rest):
        rest = rest[len(deps):]
        ex_refs, out_refs = rest[:n_ex], rest[n_ex:n_ex + n_out]
        kk = pl.program_id(2)
        av = a_ref[...]
        if a_pro is not None:
            av = a_pro(av.astype(F32))
        part = lax.dot_general(av.astype(BF16), b_ref[...].astype(BF16), _DIMS[dims], preferred_element_type=F32)

        def finish(res):
            outs = (res,) if epi is None else epi(res, *[e[...] for e in ex_refs])
            for o_ref, o in zip(out_refs, outs):
                o_ref[...] = o.astype(o_ref.dtype)

        if nk == 1:
            finish(part)
        else:
            acc = rest[-1]

            @pl.when(kk == 0)
            def _():
                acc[...] = part

            @pl.when(kk > 0)
            def _():
                acc[...] += part

            @pl.when(kk == nk - 1)
            def _():
                finish(acc[...])

    out = pl.pallas_call(
        body, grid=(m // tm, n // tn, nk),
        in_specs=[a_spec, b_spec] + [_ANY] * len(deps) + [o_spec] * n_ex,
        out_specs=[o_spec] * n_out,
        out_shape=[SDS(o_shape, d) for d in out_dtypes],
        scratch_shapes=[] if nk == 1 else [pltpu.VMEM((tm, tn), F32)],
        compiler_params=_params('parallel', 'parallel', 'arbitrary'), name=name)(a, b, *deps, *extras)
    return out[0] if n_out == 1 else out


def _relu2(u):
    r = jnp.maximum(u, 0.0)
    return r * r


def _rms_fwd(h, g, name):
    t, d = h.shape
    tr = _row_tile(t)

    def body(h_ref, g_ref, o_ref):
        x = h_ref[...]
        r = lax.rsqrt(jnp.mean(x * x, axis=-1, keepdims=True) + RMS_EPS)
        o_ref[...] = (x * r * g_ref[...]).astype(o_ref.dtype)

    return pl.pallas_call(body, grid=(t // tr,), in_specs=[_row(tr, d), _full((1, d))], out_specs=_row(tr, d),
                          out_shape=SDS((t, d), BF16), compiler_params=_params('parallel'), name=name)(h, g.reshape(1, d))


def _rms_bwd(h, g, dhn, dres, name):
    t, d = h.shape
    tr = _row_tile(t)

    def body(h_ref, g_ref, dhn_ref, dres_ref, dh_ref, dg_ref):
        @pl.when(pl.program_id(0) == 0)
        def _():
            dg_ref[...] = jnp.zeros_like(dg_ref)

        x = h_ref[...]
        r = lax.rsqrt(jnp.mean(x * x, axis=-1, keepdims=True) + RMS_EPS)
        dy = dhn_ref[...].astype(F32)
        gy = dy * g_ref[...]
        dx = r * gy - x * (r * r * r) * jnp.mean(gy * x, axis=-1, keepdims=True)
        dh_ref[...] = dres_ref[...] + dx
        dg_ref[...] += jnp.sum(dy * (x * r), axis=0, keepdims=True)

    return pl.pallas_call(body, grid=(t // tr,),
                          in_specs=[_row(tr, d), _full((1, d)), _row(tr, d), _row(tr, d)],
                          out_specs=[_row(tr, d), _full((1, d))],
                          out_shape=[SDS((t, d), F32), SDS((1, d), F32)],
                          compiler_params=_params('arbitrary'), name=name)(h, g.reshape(1, d), dhn, dres)


def _head(h, g, target, name):
    t, d = h.shape
    tr = _row_tile(t)

    def body(h_ref, g_ref, t_ref, dh_ref, loss_ref, dg_ref):
        @pl.when(pl.program_id(0) == 0)
        def _():
            dg_ref[...] = jnp.zeros_like(dg_ref)
            loss_ref[...] = jnp.zeros_like(loss_ref)

        x = h_ref[...]
        gv = g_ref[...]
        r = lax.rsqrt(jnp.mean(x * x, axis=-1, keepdims=True) + RMS_EPS)
        xh = x * r
        e = xh * gv - t_ref[...]
        per_tok = jnp.mean(e * e, axis=-1, keepdims=True)
        loss_ref[...] += jnp.broadcast_to(0.5 * jnp.sum(per_tok, axis=0, keepdims=True), loss_ref.shape)
        dy = e * (1.0 / d)
        gy = dy * gv
        dh_ref[...] = r * gy - x * (r * r * r) * jnp.mean(gy * x, axis=-1, keepdims=True)
        dg_ref[...] += jnp.sum(dy * xh, axis=0, keepdims=True)

    return pl.pallas_call(body, grid=(t // tr,),
                          in_specs=[_row(tr, d), _full((1, d)), _row(tr, d)],
                          out_specs=[_row(tr, d), _full((1, 128)), _full((1, d))],
                          out_shape=[SDS((t, d), F32), SDS((1, 128), F32), SDS((1, d), F32)],
                          compiler_params=_params('arbitrary'), name=name)(h, g.reshape(1, d), target)


def _ple_bwd_gate(dh3, gate, pp, name):
    t, d = dh3.shape
    tr = _row_tile(t)

    def body(dh_ref, g_ref, pp_ref, dpp_ref, dgl_ref):
        dh = dh_ref[...]
        gt = g_ref[...]
        dpp_ref[...] = (dh * gt).astype(dpp_ref.dtype)
        dgl_ref[...] = (dh * pp_ref[...] * gt * (1.0 - gt)).astype(dgl_ref.dtype)

    return pl.pallas_call(body, grid=(t // tr,), in_specs=[_row(tr, d)] * 3, out_specs=[_row(tr, d)] * 2,
                          out_shape=[SDS((t, d), BF16), SDS((t, d), BF16)],
                          compiler_params=_params('parallel'), name=name)(dh3, gate, pp)


def _halo_prev(tr, c, col=0):
    return pl.BlockSpec((HALO, c), lambda i, col=col: (jnp.maximum(i * (tr // HALO) - 1, 0), col))


def _halo_next(tr, c, t, col=0):
    return pl.BlockSpec((HALO, c), lambda i, col=col: (jnp.minimum((i + 1) * (tr // HALO), t // HALO - 1), col))


def _sc_fwd(z, w, name):
    t, c3 = z.shape
    c = c3 // 3
    tr = _row_tile(t)

    def body(z_ref, zp_ref, w_ref, y_ref):
        i = pl.program_id(0)
        zz = z_ref[...]
        gb, cx = zz[:, :c], zz[:, c:2 * c] * zz[:, 2 * c:]
        zp = zp_ref[SUB:HALO, :]
        cxp = jnp.where(i > 0, zp[:, c:2 * c] * zp[:, 2 * c:], 0.0)
        wv = w_ref[...]
        conv = wv[2:3] * cx + wv[1:2] * _shift_down(cx, 1, cxp) + wv[0:1] * _shift_down(cx, 2, cxp)
        y_ref[...] = (gb * conv).astype(y_ref.dtype)

    return pl.pallas_call(body, grid=(t // tr,),
                          in_specs=[_row(tr, c3), _halo_prev(tr, c3), _full((3, c))],
                          out_specs=_row(tr, c), out_shape=SDS((t, c), BF16),
                          compiler_params=_params('parallel'), name=name)(z, z, w)


def _sc_bwd(dy, z, w, name):
    t, c3 = z.shape
    c = c3 // 3
    tr = _row_tile(t)
    nt = t // tr

    def body(dy_ref, dyn_ref, z_ref, zp_ref, zn_ref, w_ref, dz_ref, dw_ref):
        i = pl.program_id(0)

        @pl.when(i == 0)
        def _():
            dw_ref[...] = jnp.zeros_like(dw_ref)

        zz = z_ref[...]
        gb, gc, xi = zz[:, :c], zz[:, c:2 * c], zz[:, 2 * c:]
        cx = gc * xi
        zp = zp_ref[SUB:HALO, :]
        cxp = jnp.where(i > 0, zp[:, c:2 * c] * zp[:, 2 * c:], 0.0)
        wv = w_ref[...]
        cx1, cx2 = _shift_down(cx, 1, cxp), _shift_down(cx, 2, cxp)
        conv = wv[2:3] * cx + wv[1:2] * cx1 + wv[0:1] * cx2
        dyv = dy_ref[...]
        dconv = dyv * gb
        dcn = jnp.where(i < nt - 1, dyn_ref[0:SUB, :] * zn_ref[0:SUB, :c], 0.0)
        dcx = wv[2:3] * dconv + wv[1:2] * _shift_up(dconv, 1, dcn) + wv[0:1] * _shift_up(dconv, 2, dcn)
        dz_ref[:, :c] = (dyv * conv).astype(dz_ref.dtype)
        dz_ref[:, c:2 * c] = (dcx * xi).astype(dz_ref.dtype)
        dz_ref[:, 2 * c:] = (dcx * gc).astype(dz_ref.dtype)
        dw_ref[...] += jnp.concatenate([jnp.sum(dconv * cx2, axis=0, keepdims=True),
                                        jnp.sum(dconv * cx1, axis=0, keepdims=True),
                                        jnp.sum(dconv * cx, axis=0, keepdims=True)], axis=0)

    return pl.pallas_call(body, grid=(nt,),
                          in_specs=[_row(tr, c), _halo_next(tr, c, t), _row(tr, c3), _halo_prev(tr, c3),
                                    _halo_next(tr, c3, t), _full((3, c))],
                          out_specs=[_row(tr, c3), _full((3, c))],
                          out_shape=[SDS((t, c3), BF16), SDS((3, c), F32)],
                          compiler_params=_params('arbitrary'), name=name)(dy, dy, z, z, z, w)


def _perm(tr, dil, inverse=False):
    n = tr // dil
    a = lax.broadcasted_iota(jnp.int32, (tr, tr), 1 if inverse else 0)
    b = lax.broadcasted_iota(jnp.int32, (tr, tr), 0 if inverse else 1)
    return (b == (a % n) * dil + a // n).astype(BF16)


def _permute(pm, x, terms):
    if x.dtype == BF16:
        return jnp.dot(pm, x, preferred_element_type=F32)
    acc = None
    for _ in range(terms):
        part = x.astype(BF16)
        y = jnp.dot(pm, part, preferred_element_type=F32)
        acc = y if acc is None else acc + y
        x = x - part.astype(F32)
    return acc


def _store_dilated(o_ref, y, dil, d):
    n = y.shape[0] // dil
    for rho in range(dil):
        o_ref[:, rho * d:(rho + 1) * d] = y[rho * n:(rho + 1) * n].astype(o_ref.dtype)


def _load_dilated(ref, dil, d):
    return jnp.concatenate([ref[:, rho * d:(rho + 1) * d] for rho in range(dil)], axis=0) if dil > 1 else ref[...]


def _rope_heads(x, lane, cos, sin):
    return jnp.concatenate([_rope_apply(x[:, s:s + HEAD_DIM], lane, cos, sin)
                            for s in range(0, x.shape[1], HEAD_DIM)], axis=1)


def _rope_tables(pos, invf, sign):
    lane = lax.broadcasted_iota(jnp.int32, (pos.shape[0], HEAD_DIM), 1)
    ang = pos.astype(F32) * invf
    half = ROPE_DIM // 2
    cos = jnp.where(lane < ROPE_DIM, jnp.cos(ang), 1.0)
    sin = jnp.sin(ang) * sign
    sin = jnp.where(lane < half, -sin, jnp.where(lane < ROPE_DIM, sin, 0.0))
    return lane, cos, sin


def _rope_apply(x, lane, cos, sin):
    half = ROPE_DIM // 2
    xs = jnp.where(lane < half, pltpu.roll(x, HEAD_DIM - half, 1), pltpu.roll(x, half, 1))
    return x * cos + xs * sin


def _dilated_spec(tr, dil, d):
    return pl.BlockSpec((tr // dil, dil * d), lambda i: (i, 0))


def _rope_fwd(qkv, pos, invf, dils, name):
    t, w3 = qkv.shape
    w, ng = w3 // 3, len(dils)
    d = w // ng
    tr = _row_tile(t)

    def body(q_ref, k_ref, v_ref, pos_ref, invf_ref, *out_refs):
        lane, cos, sin = _rope_tables(pos_ref[...], invf_ref[...], 1.0)
        for g, dil in enumerate(dils):
            cs = slice(g * d, (g + 1) * d)
            vals = [_rope_heads(q_ref[:, cs], lane, cos, sin).astype(BF16),
                    _rope_heads(k_ref[:, cs], lane, cos, sin).astype(BF16), v_ref[:, cs].astype(BF16)]
            if dil > 1:
                pm = _perm(tr, dil)
                vals = [_permute(pm, a, 1) for a in vals]
            for o_ref, a in zip(out_refs[g::ng], vals):
                _store_dilated(o_ref, a, dil, d)

    outs = pl.pallas_call(body, grid=(t // tr,),
                          in_specs=[_row(tr, w, 0), _row(tr, w, 1), _row(tr, w, 2), _row(tr, 1), _full((1, HEAD_DIM))],
                          out_specs=[_dilated_spec(tr, dil, d) for dil in dils] * 3,
                          out_shape=[SDS((t // dil, dil * d), BF16) for dil in dils] * 3,
                          compiler_params=_params('parallel'), name=name)(qkv, qkv, qkv, pos, invf)
    return outs[:ng], outs[ng:2 * ng], outs[2 * ng:]


def _rope_bwd(dqs, dks, dvs, pos, invf, dils, name):
    ng = len(dils)
    t = dqs[0].shape[0] * dils[0]
    d = dqs[0].shape[1] // dils[0]
    w = ng * d
    tr = _row_tile(t)

    def body(*refs):
        dq_refs, dk_refs, dv_refs = refs[:ng], refs[ng:2 * ng], refs[2 * ng:3 * ng]
        pos_ref, invf_ref, o_ref = refs[3 * ng:]
        pos_f = jnp.broadcast_to(pos_ref[...].astype(F32), (tr, HEAD_DIM))
        for g, dil in enumerate(dils):
            pos_g = pos_f if dil == 1 else _permute(_perm(tr, dil), pos_f, 3)
            lane, cos, sin = _rope_tables(pos_g, invf_ref[...], -1.0)
            vals = [_rope_heads(_load_dilated(dq_refs[g], dil, d), lane, cos, sin),
                    _rope_heads(_load_dilated(dk_refs[g], dil, d), lane, cos, sin), _load_dilated(dv_refs[g], dil, d)]
            back = _perm(tr, dil, inverse=True) if dil > 1 else None
            for sec, a in enumerate(vals):
                a = a.astype(BF16)
                if dil > 1:
                    a = _permute(back, a, 1)
                o_ref[:, sec * w + g * d:sec * w + (g + 1) * d] = a.astype(o_ref.dtype)

    return pl.pallas_call(body, grid=(t // tr,),
                          in_specs=[_dilated_spec(tr, dil, d) for dil in dils] * 3 + [_row(tr, 1), _full((1, HEAD_DIM))],
                          out_specs=_row(tr, 3 * w), out_shape=SDS((t, 3 * w), BF16),
                          compiler_params=_params('parallel'), name=name)(*dqs, *dks, *dvs, pos, invf)


def _dilate_many(arrs, dil, terms, out_dtypes, name):
    t, d = arrs[0].shape
    tr = _row_tile(t)
    na = len(arrs)

    def body(*refs):
        pm = _perm(tr, dil)
        for a_ref, o_ref, k in zip(refs[:na], refs[na:], terms):
            _store_dilated(o_ref, _permute(pm, a_ref[...], k), dil, d)

    return pl.pallas_call(body, grid=(t // tr,), in_specs=[_row(tr, d)] * na,
                          out_specs=[_dilated_spec(tr, dil, d)] * na,
                          out_shape=[SDS((t // dil, dil * d), dt) for dt in out_dtypes],
                          compiler_params=_params('parallel'), name=name)(*arrs)


def _attn_masks():
    qi = lax.broadcasted_iota(jnp.int32, (ATTN_BLOCK, ATTN_BLOCK), 0)
    kj = lax.broadcasted_iota(jnp.int32, (ATTN_BLOCK, ATTN_BLOCK), 1)
    return kj >= qi, kj <= qi


def _attn_cols(l, width):
    ncol = width // HEAD_DIM
    cpb = max(1, min(ncol, 32 // (l // ATTN_BLOCK)))
    assert ncol % cpb == 0
    return cpb


def _attn_fwd(q, k, v, name):
    l, width = q.shape
    cpb = _attn_cols(l, width)
    nb = l // ATTN_BLOCK
    scale = HEAD_DIM ** -0.5

    def body(q_ref, k_ref, v_ref, o_ref, lse_ref):
        m_prev, m_cur = _attn_masks()
        for col in range(cpb):
            cs = slice(col * HEAD_DIM, (col + 1) * HEAD_DIM)

            def step(b, carry, cs=cs):
                r0 = pl.multiple_of(b * ATTN_BLOCK, ATTN_BLOCK)
                rp = pl.multiple_of(jnp.maximum(b - 1, 0) * ATTN_BLOCK, ATTN_BLOCK)
                qb = q_ref[pl.ds(r0, ATTN_BLOCK), cs]
                s_p = lax.dot_general(qb, k_ref[pl.ds(rp, ATTN_BLOCK), cs], _DIMS['nt'], preferred_element_type=F32) * scale
                s_c = lax.dot_general(qb, k_ref[pl.ds(r0, ATTN_BLOCK), cs], _DIMS['nt'], preferred_element_type=F32) * scale
                s_p = jnp.where(jnp.logical_and(m_prev, b > 0), s_p, NEG)
                s_c = jnp.where(m_cur, s_c, NEG)
                m = jnp.maximum(jnp.max(s_p, axis=-1, keepdims=True), jnp.max(s_c, axis=-1, keepdims=True))
                p_p, p_c = jnp.exp(s_p - m), jnp.exp(s_c - m)
                den = jnp.sum(p_p, axis=-1, keepdims=True) + jnp.sum(p_c, axis=-1, keepdims=True)
                acc = jnp.dot(p_p.astype(BF16), v_ref[pl.ds(rp, ATTN_BLOCK), cs], preferred_element_type=F32)
                acc += jnp.dot(p_c.astype(BF16), v_ref[pl.ds(r0, ATTN_BLOCK), cs], preferred_element_type=F32)
                o_ref[pl.ds(r0, ATTN_BLOCK), cs] = acc / den
                lse_ref[pl.ds(r0, ATTN_BLOCK), cs] = jnp.broadcast_to(m + jnp.log(den), (ATTN_BLOCK, HEAD_DIM))
                return carry

            lax.fori_loop(0, nb, step, 0, unroll=min(nb, 4))

    spec = pl.BlockSpec((l, cpb * HEAD_DIM), lambda j: (0, j))
    return pl.pallas_call(body, grid=(width // (cpb * HEAD_DIM),), in_specs=[spec] * 3, out_specs=[spec] * 2,
                          out_shape=[SDS((l, width), F32)] * 2,
                          compiler_params=_params('parallel'), name=name)(q, k, v)


def _attn_bwd(q, k, v, do, lse, delta, name):
    l, width = q.shape
    cpb = _attn_cols(l, width)
    nb = l // ATTN_BLOCK
    scale = HEAD_DIM ** -0.5

    def body(q_ref, k_ref, v_ref, do_ref, lse_ref, dl_ref, dq_ref, dk_ref, dv_ref):
        m_prev, m_cur = _attn_masks()
        dk_ref[...] = jnp.zeros_like(dk_ref)
        dv_ref[...] = jnp.zeros_like(dv_ref)
        for col in range(cpb):
            cs = slice(col * HEAD_DIM, (col + 1) * HEAD_DIM)

            def step(b, carry, cs=cs):
                r0 = pl.multiple_of(b * ATTN_BLOCK, ATTN_BLOCK)
                rp = pl.multiple_of(jnp.maximum(b - 1, 0) * ATTN_BLOCK, ATTN_BLOCK)
                qb, dob = q_ref[pl.ds(r0, ATTN_BLOCK), cs], do_ref[pl.ds(r0, ATTN_BLOCK), cs].astype(BF16)
                kp, kc = k_ref[pl.ds(rp, ATTN_BLOCK), cs], k_ref[pl.ds(r0, ATTN_BLOCK), cs]
                vp, vc = v_ref[pl.ds(rp, ATTN_BLOCK), cs], v_ref[pl.ds(r0, ATTN_BLOCK), cs]
                lse_b = lse_ref[pl.ds(r0, ATTN_BLOCK), cs]
                dl_b = dl_ref[pl.ds(r0, ATTN_BLOCK), cs]
                s_p = lax.dot_general(qb, kp, _DIMS['nt'], preferred_element_type=F32) * scale
                s_c = lax.dot_general(qb, kc, _DIMS['nt'], preferred_element_type=F32) * scale
                p_p = jnp.exp(jnp.where(jnp.logical_and(m_prev, b > 0), s_p, NEG) - lse_b)
                p_c = jnp.exp(jnp.where(m_cur, s_c, NEG) - lse_b)
                dp_p = lax.dot_general(dob, vp, _DIMS['nt'], preferred_element_type=F32)
                dp_c = lax.dot_general(dob, vc, _DIMS['nt'], preferred_element_type=F32)
                ds_p = (p_p * (dp_p - dl_b) * scale).astype(BF16)
                ds_c = (p_c * (dp_c - dl_b) * scale).astype(BF16)
                dq_ref[pl.ds(r0, ATTN_BLOCK), cs] = (jnp.dot(ds_p, kp, preferred_element_type=F32)
                                                     + jnp.dot(ds_c, kc, preferred_element_type=F32))
                dk_ref[pl.ds(rp, ATTN_BLOCK), cs] += lax.dot_general(ds_p, qb, _DIMS['tn'], preferred_element_type=F32)
                dk_ref[pl.ds(r0, ATTN_BLOCK), cs] += lax.dot_general(ds_c, qb, _DIMS['tn'], preferred_element_type=F32)
                dv_ref[pl.ds(rp, ATTN_BLOCK), cs] += lax.dot_general(p_p.astype(BF16), dob, _DIMS['tn'], preferred_element_type=F32)
                dv_ref[pl.ds(r0, ATTN_BLOCK), cs] += lax.dot_general(p_c.astype(BF16), dob, _DIMS['tn'], preferred_element_type=F32)
                return carry

            lax.fori_loop(0, nb, step, 0, unroll=min(nb, 2))

    spec = pl.BlockSpec((l, cpb * HEAD_DIM), lambda j: (0, j))
    return pl.pallas_call(body, grid=(width // (cpb * HEAD_DIM),), in_specs=[spec] * 6, out_specs=[spec] * 3,
                          out_shape=[SDS((l, width), F32)] * 3,
                          compiler_params=_params('parallel'), name=name)(q, k, v, do, lse, delta)


def _attn_combine(os_, lses, dils, name):
    ng = len(dils)
    t = os_[0].shape[0] * dils[0]
    d = os_[0].shape[1] // dils[0]
    tr = _row_tile(t)

    def body(*refs):
        o_refs, l_refs, o_out, lse_out = refs[:ng], refs[ng:2 * ng], refs[2 * ng], refs[2 * ng + 1]
        ovs, ls = [], []
        for g, dil in enumerate(dils):
            ov, lv = _load_dilated(o_refs[g], dil, d), _load_dilated(l_refs[g], dil, d)
            if dil > 1:
                back = _perm(tr, dil, inverse=True)
                ov, lv = _permute(back, ov, 2), _permute(back, lv, 3)
            ovs.append(ov)
            ls.append(lv)
        m = functools.reduce(jnp.maximum, ls)
        ws = [jnp.exp(x - m) for x in ls]
        den = functools.reduce(lambda a, b: a + b, ws)
        acc = functools.reduce(lambda a, b: a + b, [w * o for w, o in zip(ws, ovs)])
        o_out[...] = (acc / den).astype(o_out.dtype)
        lse_out[...] = m + jnp.log(den)

    return pl.pallas_call(body, grid=(t // tr,), in_specs=[_dilated_spec(tr, dil, d) for dil in dils] * 2,
                          out_specs=[_row(tr, d)] * 2, out_shape=[SDS((t, d), BF16), SDS((t, d), F32)],
                          compiler_params=_params('parallel'), name=name)(*os_, *lses)


def _delta_epilogue(acc, o):
    prod = acc * o.astype(F32)
    segs = [jnp.broadcast_to(jnp.sum(prod[:, s:s + HEAD_DIM], axis=-1, keepdims=True), (acc.shape[0], HEAD_DIM))
            for s in range(0, acc.shape[1], HEAD_DIM)]
    return acc, jnp.concatenate(segs, axis=-1)


LRU_TILE = 128


def _lru_gates(xr, wa_ref, ba, wx_ref, bx, lam):
    nb = wa_ref.shape[0]
    xb = xr.astype(BF16)
    ra = jnp.concatenate([jnp.dot(xb[:, n * LRU_BLOCK:(n + 1) * LRU_BLOCK], wa_ref[n], preferred_element_type=F32)
                          for n in range(nb)], axis=-1) + ba
    ia = jnp.concatenate([jnp.dot(xb[:, n * LRU_BLOCK:(n + 1) * LRU_BLOCK], wx_ref[n], preferred_element_type=F32)
                          for n in range(nb)], axis=-1) + bx
    r, ig = _sigmoid(ra), _sigmoid(ia)
    sp = _softplus(-lam)
    log_a = -LRU_C * r * sp
    a = jnp.exp(log_a)
    mult = jnp.sqrt(-_expm1(2.0 * log_a))
    return xb, r, ig, sp, a, mult


def _lru_fwd(z, cw, cb, wa, ba, wx, bx, lam, name):
    t, c2 = z.shape
    c = c2 // 2
    nb = c // LRU_BLOCK
    tr = _row_tile(t, LRU_TILE)

    def body(g_ref, x_ref, xp_ref, cw_ref, cb_ref, wa_ref, ba_ref, wx_ref, bx_ref, lam_ref,
             y_ref, hs_ref, xr_ref, car_ref):
        i = pl.program_id(0)

        @pl.when(i == 0)
        def _():
            car_ref[...] = jnp.zeros_like(car_ref)

        x0 = x_ref[...]
        xp = jnp.where(i > 0, xp_ref[SUB:HALO, :], 0.0)
        cwv = cw_ref[...]
        xr = (cb_ref[...] + cwv[3:4] * x0 + cwv[2:3] * _shift_down(x0, 1, xp)
              + cwv[1:2] * _shift_down(x0, 2, xp) + cwv[0:1] * _shift_down(x0, 3, xp))
        xr_ref[...] = xr
        _, _, ig, _, a, mult = _lru_gates(xr, wa_ref, ba_ref[...], wx_ref, bx_ref[...], lam_ref[...])
        u = mult * (ig * xr)
        row = lax.broadcasted_iota(jnp.int32, (SUB, c), 0)
        car = car_ref[...]
        for j in range(tr // SUB):
            ab, ub = a[j * SUB:(j + 1) * SUB], u[j * SUB:(j + 1) * SUB]
            for s in (1, 2, 4):
                a_sh = jnp.where(row >= s, pltpu.roll(ab, s, 0), 1.0)
                u_sh = jnp.where(row >= s, pltpu.roll(ub, s, 0), 0.0)
                ub = ab * u_sh + ub
                ab = ab * a_sh
            hb = ub + ab * car
            hs_ref[j * SUB:(j + 1) * SUB, :] = hb
            car = jnp.broadcast_to(hb[SUB - 1:SUB], (SUB, c))
        car_ref[...] = car
        gl, _ = _gelu_and_grad(g_ref[...])
        y_ref[...] = (hs_ref[...] * gl).astype(y_ref.dtype)

    return pl.pallas_call(
        body, grid=(t // tr,),
        in_specs=[_row(tr, c, 0), _row(tr, c, 1), _halo_prev(tr, c, 1), _full((4, c)), _full((1, c)),
                  _full((nb, LRU_BLOCK, LRU_BLOCK)), _full((1, c)), _full((nb, LRU_BLOCK, LRU_BLOCK)), _full((1, c)), _full((1, c))],
        out_specs=[_row(tr, c)] * 3,
        out_shape=[SDS((t, c), BF16), SDS((t, c), F32), SDS((t, c), F32)],
        scratch_shapes=[pltpu.VMEM((SUB, c), F32)],
        compiler_params=_params('arbitrary'), name=name)(
            z, z, z, cw, cb.reshape(1, c), wa, ba.reshape(1, c), wx, bx.reshape(1, c), lam.reshape(1, c))


def _lru_bwd(dy, z, xr, hs, cw, wa, ba, wx, bx, lam, name):
    t, c2 = z.shape
    c = c2 // 2
    nb = c // LRU_BLOCK
    tr = _row_tile(t, LRU_TILE)
    nt = t // tr

    def rev(col=0):
        return pl.BlockSpec((tr, c), lambda i, col=col: (nt - 1 - i, col))

    def rev_prev(col=0):
        return pl.BlockSpec((HALO, c), lambda i, col=col: (jnp.maximum((nt - 1 - i) * (tr // HALO) - 1, 0), col))

    def body(dy_ref, g_ref, x_ref, xp_ref, xr_ref, hs_ref, hp_ref, cw_ref, wa_ref, ba_ref, wx_ref, bx_ref, lam_ref,
             dz_ref, dwa_ref, dwx_ref, dvec_ref, lcar_ref, ahead_ref, dxhead_ref, lam_s):
        i = pl.program_id(0)
        first_tile = i == nt - 1

        @pl.when(i == 0)
        def _():
            lcar_ref[...] = jnp.zeros_like(lcar_ref)
            ahead_ref[...] = jnp.zeros_like(ahead_ref)
            dxhead_ref[...] = jnp.zeros_like(dxhead_ref)
            dwa_ref[...] = jnp.zeros_like(dwa_ref)
            dwx_ref[...] = jnp.zeros_like(dwx_ref)
            dvec_ref[...] = jnp.zeros_like(dvec_ref)

        xrv = xr_ref[...]
        lamv = lam_ref[...]
        xb, r, ig, sp, a, mult = _lru_gates(xrv, wa_ref, ba_ref[...], wx_ref, bx_ref[...], lamv)
        hsv = hs_ref[...]
        dyv = dy_ref[...]
        gl, dgl = _gelu_and_grad(g_ref[...])
        dhs = dyv * gl
        dz_ref[:, :c] = (dyv * hsv * dgl).astype(dz_ref.dtype)

        a_next = _shift_up(a, 1, ahead_ref[...])
        row = lax.broadcasted_iota(jnp.int32, (SUB, c), 0)
        car = lcar_ref[...]
        for j in reversed(range(tr // SUB)):
            ab, ub = a_next[j * SUB:(j + 1) * SUB], dhs[j * SUB:(j + 1) * SUB]
            for s in (1, 2, 4):
                a_sh = jnp.where(row < SUB - s, pltpu.roll(ab, SUB - s, 0), 1.0)
                u_sh = jnp.where(row < SUB - s, pltpu.roll(ub, SUB - s, 0), 0.0)
                ub = ab * u_sh + ub
                ab = ab * a_sh
            lb = ub + ab * car
            lam_s[j * SUB:(j + 1) * SUB, :] = lb
            car = jnp.broadcast_to(lb[0:1], (SUB, c))
        lcar_ref[...] = car
        ahead_ref[...] = a[0:SUB]
        lmb = lam_s[...]

        hp = jnp.where(first_tile, 0.0, hp_ref[SUB:HALO, :])
        h_prev = _shift_down(hsv, 1, hp)
        d_a = lmb * h_prev
        d_mult = lmb * (ig * xrv)
        d_ixr = lmb * mult
        d_ig = d_ixr * xrv
        dxr = d_ixr * ig
        d_la = d_a * a - d_mult * (a * a) / mult
        d_r = d_la * (-LRU_C * sp)
        d_sp = jnp.sum(d_la * (-LRU_C * r), axis=0, keepdims=True)
        d_ra = d_r * r * (1.0 - r)
        d_ia = d_ig * ig * (1.0 - ig)
        d_rab, d_iab = d_ra.astype(BF16), d_ia.astype(BF16)
        parts = []
        for n in range(nb):
            cs = slice(n * LRU_BLOCK, (n + 1) * LRU_BLOCK)
            parts.append(lax.dot_general(d_rab[:, cs], wa_ref[n], _DIMS['nt'], preferred_element_type=F32)
                         + lax.dot_general(d_iab[:, cs], wx_ref[n], _DIMS['nt'], preferred_element_type=F32))
            dwa_ref[n] += lax.dot_general(xb[:, cs], d_rab[:, cs], _DIMS['tn'], preferred_element_type=F32)
            dwx_ref[n] += lax.dot_general(xb[:, cs], d_iab[:, cs], _DIMS['tn'], preferred_element_type=F32)
        dxr = dxr + jnp.concatenate(parts, axis=-1)

        cwv = cw_ref[...]
        nxt = dxhead_ref[...]
        dx0 = (cwv[3:4] * dxr + cwv[2:3] * _shift_up(dxr, 1, nxt) + cwv[1:2] * _shift_up(dxr, 2, nxt)
               + cwv[0:1] * _shift_up(dxr, 3, nxt))
        dxhead_ref[...] = dxr[0:SUB]
        dz_ref[:, c:] = dx0.astype(dz_ref.dtype)

        x0 = x_ref[...]
        xp = jnp.where(first_tile, 0.0, xp_ref[SUB:HALO, :])
        sums = [jnp.sum(d_ra, axis=0, keepdims=True), jnp.sum(d_ia, axis=0, keepdims=True),
                d_sp * (-_sigmoid(-lamv)), jnp.sum(dxr, axis=0, keepdims=True),
                jnp.sum(dxr * _shift_down(x0, 3, xp), axis=0, keepdims=True),
                jnp.sum(dxr * _shift_down(x0, 2, xp), axis=0, keepdims=True),
                jnp.sum(dxr * _shift_down(x0, 1, xp), axis=0, keepdims=True),
                jnp.sum(dxr * x0, axis=0, keepdims=True)]
        dvec_ref[...] += jnp.concatenate(sums, axis=0)

    wspec = _full((nb, LRU_BLOCK, LRU_BLOCK))
    return pl.pallas_call(
        body, grid=(nt,),
        in_specs=[rev(), rev(0), rev(1), rev_prev(1), rev(), rev(), rev_prev(), _full((4, c)),
                  wspec, _full((1, c)), wspec, _full((1, c)), _full((1, c))],
        out_specs=[pl.BlockSpec((tr, c2), lambda i: (nt - 1 - i, 0)), wspec, wspec, _full((SUB, c))],
        out_shape=[SDS((t, c2), BF16), SDS((nb, LRU_BLOCK, LRU_BLOCK), F32), SDS((nb, LRU_BLOCK, LRU_BLOCK), F32),
                   SDS((SUB, c), F32)],
        scratch_shapes=[pltpu.VMEM((SUB, c), F32), pltpu.VMEM((SUB, c), F32), pltpu.VMEM((SUB, c), F32),
                        pltpu.VMEM((tr, c), F32)],
        compiler_params=_params('arbitrary'), name=name)(
            dy, z, z, z, xr, hs, hs, cw, wa, ba.reshape(1, c), wx, bx.reshape(1, c), lam.reshape(1, c))


def _local_step(x, p, pos, target, rep, weights_for_layer, emit_grads):
    t, d = x.shape
    depth = p.shape[0]
    w = rep
    half = ROPE_DIM // 2
    invf = ROPE_THETA ** (-2.0 * jnp.arange(half, dtype=F32) / ROPE_DIM)
    invf = jnp.concatenate([invf, invf, jnp.zeros((HEAD_DIM - ROPE_DIM,), F32)]).reshape(1, HEAD_DIM)
    dils = tuple(dil for _, dil in DILATED_PATTERNS)
    saved = []
    h = x
    for i in range(depth):
        kind, j = i % N_MIXERS, i // N_MIXERS
        wl, tok = weights_for_layer(i, h)
        s = {'h0': h, 'wl': wl}
        hn = _rms_fwd(h, w['norm_mix'][i], f'rms_mix_fwd_{i}')
        s['hn'] = hn
        if kind == 0:
            z = _mm(hn, wl['w_in'], 'nn', f'sc_in_{i}', dep=tok)
            y = _sc_fwd(z, wl['small'], f'sc_conv_fwd_{i}')
            h1 = _mm(y, wl['w_out'], 'nn', f'sc_out_{i}', extras=(h,), epi=lambda acc, res: (acc + res,))
            s.update(z=z, y=y)
        elif kind == 1:
            qkv = _mm(hn, wl['w_in'], 'nn', f'attn_qkv_{i}', dep=tok)
            qs, ks, vs = _rope_fwd(qkv, pos, invf, dils, f'rope_fwd_{i}')
            views = list(zip(qs, ks, vs))
            os_, lses = zip(*[_attn_fwd(qg, kg, vg, f'attn_fwd_{i}_g{g}') for g, (qg, kg, vg) in enumerate(views)])
            o, lse = _attn_combine(os_, lses, dils, f'attn_combine_{i}')
            h1 = _mm(o, wl['w_out'], 'nn', f'attn_out_{i}', extras=(h,), epi=lambda acc, res: (acc + res,))
            s.update(views=views, o=o, lse=lse)
        else:
            z = _mm(hn, wl['w_in'], 'nn', f'lru_in_{i}', dep=tok)
            sm = wl['small']
            y, hs, xr = _lru_fwd(z, sm[0:4], sm[4:5], w['lru_w_a'][j], sm[5:6], w['lru_w_x'][j], sm[6:7], sm[7:8],
                                 f'lru_fwd_{i}')
            h1 = _mm(y, wl['w_out'], 'nn', f'lru_out_{i}', extras=(h,), epi=lambda acc, res: (acc + res,))
            s.update(z=z, y=y, hs=hs, xr=xr)
        s['h1'] = h1
        hm = _rms_fwd(h1, w['norm_mlp'][i], f'rms_mlp_fwd_{i}')
        u = _mm(hm, wl['mlp_up'], 'nn', f'mlp_up_{i}', out_dtypes=(BF16,))
        h2 = _mm(u, wl['mlp_down'], 'nn', f'mlp_down_{i}', a_pro=_relu2, extras=(h1,), epi=lambda acc, res: (acc + res,))
        hp = _rms_fwd(h2, w['norm_ple'][i], f'rms_ple_fwd_{i}')
        pp = _mm(p[i], wl['ple_proj'], 'nn', f'ple_proj_{i}')
        h3, gate = _mm(hp, wl['ple_gate'], 'nn', f'ple_gate_{i}', out_dtypes=(F32, F32), extras=(pp, h2),
                       epi=lambda acc, ppv, res: (res + _sigmoid(acc) * ppv, _sigmoid(acc)))
        s.update(hm=hm, u=u, h2=h2, hp=hp, pp=pp, gate=gate)
        saved.append(s)
        h = h3

    dh, loss, dg_final = _head(h, w['norm_final'], target, 'loss_head')
    grads = {n: [None] * len(w[n]) for n in w if n != 'norm_final'}
    grads['norm_final'] = dg_final.reshape(d)
    started = None
    for i in reversed(range(depth)):
        kind, j = i % N_MIXERS, i // N_MIXERS
        s = saved[i]
        wl, gl = s['wl'], {}
        dpp, dgl = _ple_bwd_gate(dh, s['gate'], s['pp'], f'ple_bwd_gate_{i}')
        gl['ple_proj'] = _mm(p[i], dpp, 'tn', f'ple_dproj_{i}', out_dtypes=(BF16,), dep=started)
        gl['ple_gate'] = _mm(s['hp'], dgl, 'tn', f'ple_dgate_{i}', out_dtypes=(BF16,))
        dhp = _mm(dgl, wl['ple_gate'], 'nt', f'ple_dhp_{i}')
        dh, dg = _rms_bwd(s['h2'], w['norm_ple'][i], dhp, dh, f'rms_ple_bwd_{i}')
        grads['norm_ple'][i] = dg.reshape(d)
        du = _mm(dh, wl['mlp_down'], 'nt', f'mlp_du_{i}', out_dtypes=(BF16,), extras=(s['u'],),
                 epi=lambda acc, uv: (acc * 2.0 * jnp.maximum(uv.astype(F32), 0.0),))
        gl['mlp_down'] = _mm(s['u'], dh, 'tn', f'mlp_ddown_{i}', out_dtypes=(BF16,), a_pro=_relu2)
        gl['mlp_up'] = _mm(s['hm'], du, 'tn', f'mlp_dup_{i}', out_dtypes=(BF16,), out_stacked=True)
        dhm = _mm(du, wl['mlp_up'], 'nt', f'mlp_dhm_{i}')
        dh, dg = _rms_bwd(s['h1'], w['norm_mlp'][i], dhm, dh, f'rms_mlp_bwd_{i}')
        grads['norm_mlp'][i] = dg.reshape(d)
        started = emit_grads(i, 'mlp', gl)
        gl = {}
        if kind == 0:
            dy = _mm(dh, wl['w_out'], 'nt', f'sc_dy_{i}', dep=started)
            gl['w_out'] = _mm(s['y'], dh, 'tn', f'sc_dout_{i}', out_dtypes=(BF16,))
            dz, dwc = _sc_bwd(dy, s['z'], wl['small'], f'sc_conv_bwd_{i}')
            gl['small'] = dwc
            gl['w_in'] = _mm(s['hn'], dz, 'tn', f'sc_din_{i}', out_dtypes=(BF16,))
            started = emit_grads(i, 'mixer', gl)
            dhn = _mm(dz, wl['w_in'], 'nt', f'sc_dhn_{i}', dep=started)
        elif kind == 1:
            do, delta = _mm(dh, wl['w_out'], 'nt', f'attn_do_{i}', out_dtypes=(BF16, F32), extras=(s['o'],),
                            epi=_delta_epilogue, dep=started)
            gl['w_out'] = _mm(s['o'], dh, 'tn', f'attn_dwo_{i}', out_dtypes=(BF16,))
            rows_in = {1: (do, s['lse'], delta)}
            for dil in dils:
                if dil not in rows_in:
                    rows_in[dil] = _dilate_many([do, s['lse'], delta], dil, (1, 3, 3), (BF16, F32, F32),
                                                f'attn_dilate_{i}_d{dil}')
            dqs, dks, dvs = zip(*[_attn_bwd(*s['views'][g], *rows_in[dil], f'attn_bwd_{i}_g{g}')
                                  for g, dil in enumerate(dils)])
            dqkv = _rope_bwd(dqs, dks, dvs, pos, invf, dils, f'rope_bwd_{i}')
            gl['w_in'] = _mm(s['hn'], dqkv, 'tn', f'attn_dqkv_{i}', out_dtypes=(BF16,), out_stacked=True)
            started = emit_grads(i, 'mixer', gl)
            dhn = _mm(dqkv, wl['w_in'], 'nt', f'attn_dhn_{i}', dep=started)
        else:
            dy = _mm(dh, wl['w_out'], 'nt', f'lru_dy_{i}', dep=started)
            gl['w_out'] = _mm(s['y'], dh, 'tn', f'lru_dout_{i}', out_dtypes=(BF16,))
            sm = wl['small']
            dz, dwa, dwx, dvec = _lru_bwd(dy, s['z'], s['xr'], s['hs'], sm[0:4], w['lru_w_a'][j], sm[5:6],
                                          w['lru_w_x'][j], sm[6:7], sm[7:8], f'lru_bwd_{i}')
            grads['lru_w_a'][j], grads['lru_w_x'][j] = dwa, dwx
            gl['small'] = dvec
            gl['w_in'] = _mm(s['hn'], dz, 'tn', f'lru_din_{i}', out_dtypes=(BF16,))
            started = emit_grads(i, 'mixer', gl)
            dhn = _mm(dz, wl['w_in'], 'nt', f'lru_dhn_{i}', dep=started)
        dh, dg = _rms_bwd(s['h0'], w['norm_mix'][i], dhn, dh, f'rms_mix_bwd_{i}')
        grads['norm_mix'][i] = dg.reshape(d)
        started = None
    return loss, dh, grads


_MESH = pl.DeviceIdType.MESH
_ANY = pl.BlockSpec(memory_space=pl.ANY)


def _block_view(ref, kind, idx):
    if kind == 'stack':
        return ref.at[idx]
    r = ref.shape[0] // N_DEV
    return ref.at[pl.ds(idx * r, r)]


def _gather_many(arrs, kinds, name):
    n = len(arrs)
    out_shapes = [SDS((N_DEV,) + a.shape if kd == 'stack' else (N_DEV * a.shape[0],) + a.shape[1:], a.dtype)
                  for a, kd in zip(arrs, kinds)]

    def body(*refs):
        x_refs, out_refs = refs[:n], refs[n:2 * n]
        send_sems, recv_sems, local_sems = refs[2 * n:]
        x, y, c = lax.axis_index('x'), lax.axis_index('y'), lax.axis_index('c')
        me, sibling = (x, y, c), (x, y, 1 - c)
        chips = [(1 - x, y), (x, 1 - y), (1 - x, 1 - y)]

        def slab(t, px, py, pc):
            return _block_view(out_refs[t], kinds[t], 4 * px + 2 * py + pc)

        def copy(t, k, block, to, src=None):
            return pltpu.make_async_remote_copy(
                src_ref=slab(t, *block) if src is None else src, dst_ref=slab(t, *block),
                send_sem=send_sems.at[7 * t + k], recv_sem=recv_sems.at[7 * t + k], device_id=to, device_id_type=_MESH)

        mine = [pltpu.make_async_copy(x_refs[t], slab(t, *me), local_sems.at[t]) for t in range(n)]
        for cp in mine:
            cp.start()
        first = [copy(t, 0, me, sibling, src=x_refs[t]) for t in range(n)]
        first += [copy(t, 1 + j, me, (*chip, c), src=x_refs[t]) for j, chip in enumerate(chips) for t in range(n)]
        for cp in first:
            cp.start()
        passed = []
        for j, chip in enumerate(chips):
            for t in range(n):
                copy(t, 1 + j, (*chip, c), me).wait_recv()
                passed.append(copy(t, 4 + j, (*chip, c), sibling))
                passed[-1].start()
        for t in range(n):
            copy(t, 0, sibling, me).wait_recv()
            for j, chip in enumerate(chips):
                copy(t, 4 + j, (*chip, 1 - c), me).wait_recv()
        for cp in first + passed:
            cp.wait_send()
        for cp in mine:
            cp.wait()

    return pl.pallas_call(
        body, out_shape=out_shapes, in_specs=[_ANY] * n, out_specs=[_ANY] * n,
        scratch_shapes=[pltpu.SemaphoreType.DMA((7 * n,)), pltpu.SemaphoreType.DMA((7 * n,)), pltpu.SemaphoreType.DMA((n,))],
        name=name)(*arrs)


_HBM = pl.BlockSpec(memory_space=pltpu.HBM)
_SEM = pl.BlockSpec(memory_space=pltpu.SEMAPHORE)
_EFFECT = pltpu.SideEffectType.DATAFLOW_SIDE_EFFECTING


def _direct_copies(mode, kinds, src_refs, land_refs, send_sems, recv_sems):
    x, y, c = lax.axis_index('x'), lax.axis_index('y'), lax.axis_index('c')
    my_idx = 4 * x + 2 * y + c
    copies = []
    for k in range(1, N_DEV):
        px, py, pc = (1 - x if k & 4 else x, 1 - y if k & 2 else y, 1 - c if k & 1 else c)
        for t, kd in enumerate(kinds):
            if mode == 'gather':
                src, dst = src_refs[t], _block_view(land_refs[t], kd, my_idx)
            else:
                src, dst = _block_view(src_refs[t], kd, 4 * px + 2 * py + pc), land_refs[t].at[my_idx]
            copies.append(pltpu.make_async_remote_copy(
                src_ref=src, dst_ref=dst, send_sem=send_sems.at[7 * t + k - 1], recv_sem=recv_sems.at[7 * t + k - 1],
                device_id=(px, py, pc), device_id_type=_MESH))
    return copies


def _own_part(mode, kind, src, land):
    idx = 4 * lax.axis_index('x') + 2 * lax.axis_index('y') + lax.axis_index('c')
    zeros = (0,) * (src.ndim - 1)
    if mode == 'gather':
        part = src
    elif kind == 'stack':
        part = lax.dynamic_index_in_dim(src, idx, 0, keepdims=False)
    else:
        r = src.shape[0] // N_DEV
        part = lax.dynamic_slice_in_dim(src, idx * r, r, 0)
    if mode == 'gather' and kind == 'rows':
        return lax.dynamic_update_slice(land, part, (idx * part.shape[0],) + zeros)
    return lax.dynamic_update_slice(land, part[None], (idx,) + (0,) * part.ndim)


def _send_start(mode, srcs, kinds, name, after=None):
    n = len(srcs)
    after = [] if after is None else [after]
    lands = []
    for a, kd in zip(srcs, kinds):
        if mode == 'gather':
            shape = (N_DEV,) + a.shape if kd == 'stack' else (N_DEV * a.shape[0],) + a.shape[1:]
        else:
            shape = a.shape if kd == 'stack' else (N_DEV, a.shape[0] // N_DEV) + a.shape[1:]
        lands.append(_own_part(mode, kd, a, lax.empty(shape, a.dtype)))

    def body(*refs):
        src_refs, land_refs = refs[:n], refs[n:2 * n]
        send_sems, recv_sems = refs[2 * n + len(after):2 * n + len(after) + 2]
        token = refs[-1]
        for cp in _direct_copies(mode, kinds, src_refs, land_refs, send_sems, recv_sems):
            cp.start()
        token[...] = jnp.zeros_like(token)

    outs = pl.pallas_call(
        body, name=name,
        out_shape=(pltpu.SemaphoreType.DMA((7 * n,)), pltpu.SemaphoreType.DMA((7 * n,)),
                   *[pltpu.HBM(a.shape, a.dtype) for a in srcs + lands], SDS((SUB, 128), F32)),
        in_specs=[_HBM] * (2 * n) + [_ANY] * len(after),
        out_specs=(_SEM, _SEM, *[_HBM] * (2 * n), pl.BlockSpec(memory_space=pltpu.VMEM)),
        input_output_aliases={i: 2 + i for i in range(2 * n)},
        compiler_params=pltpu.CompilerParams(has_side_effects=_EFFECT),
    )(*[pltpu.with_memory_space_constraint(a, pltpu.HBM) for a in srcs + lands], *after)
    return (outs[0], outs[1], list(outs[2:2 + 2 * n])), outs[-1]


def _send_wait(mode, flight, kinds, after, name):
    send, recv, bufs = flight
    n = len(kinds)

    def body(*refs):
        src_refs, land_refs, (send_sems, recv_sems) = refs[:n], refs[n:2 * n], refs[2 * n:2 * n + 2]
        copies = _direct_copies(mode, kinds, src_refs, land_refs, send_sems, recv_sems)
        for cp in copies:
            cp.wait_send()
        for cp in copies:
            cp.wait_recv()

    outs = pl.pallas_call(
        body, name=name, out_shape=[pltpu.HBM(a.shape, a.dtype) for a in bufs],
        in_specs=[_HBM] * (2 * n) + [_SEM, _SEM, _ANY], out_specs=[_HBM] * (2 * n),
        input_output_aliases={i: i for i in range(2 * n)},
        compiler_params=pltpu.CompilerParams(has_side_effects=_EFFECT),
    )(*bufs, send, recv, after)
    return list(outs[n:])


ADAMW_BLOCK_ELEMS = 128 * 1024


def _adamw_sum(wgt, parts, m, v, name):
    nl, r, c = wgt.shape
    assert len(parts) == nl and all(q.shape == (N_DEV, r, c) for q in parts), (name, wgt.shape, [q.shape for q in parts])
    tr = next((t for t in range(min(r, 512), 0, -16) if r % t == 0 and t * c <= ADAMW_BLOCK_ELEMS and t % 16 == 0), r)
    c1 = 1.0 - ADAM_B1 ** ADAM_STEP
    c2 = 1.0 - ADAM_B2 ** ADAM_STEP

    def body(w_ref, m_ref, v_ref, *rest):
        part_refs, (g_ref, d_ref, mo_ref, vo_ref) = rest[:nl], rest[nl:]
        for q in range(nl):
            @pl.when(pl.program_id(0) == q)
            def _(q=q):
                gv = part_refs[q][0].astype(F32)
                for s in range(1, N_DEV):
                    gv = gv + part_refs[q][s].astype(F32)
                mn = ADAM_B1 * m_ref[...] + (1.0 - ADAM_B1) * gv
                vn = ADAM_B2 * v_ref[...] + (1.0 - ADAM_B2) * (gv * gv)
                g_ref[...] = gv
                d_ref[...] = -ADAM_LR * ((mn / c1) / (jnp.sqrt(vn / c2) + ADAM_EPS) + ADAM_WD * w_ref[...])
                mo_ref[...] = mn
                vo_ref[...] = vn

    spec = pl.BlockSpec((None, tr, c), lambda l, i: (l, i, 0))
    part_specs = [pl.BlockSpec((N_DEV, tr, c), lambda l, i, q=q: (0, jnp.where(l == q, i, 0), 0)) for q in range(nl)]
    return pl.pallas_call(body, grid=(nl, r // tr), in_specs=[spec] * 3 + part_specs, out_specs=[spec] * 4,
                          out_shape=[SDS((nl, r, c), F32)] * 4, compiler_params=_params('arbitrary', 'arbitrary'),
                          name=name)(wgt, m, v, *parts)


MIXER_WEIGHTS = {0: ('sc_w_in', 'sc_w_out'), 1: ('attn_w_qkv', 'attn_w_o'), 2: ('lru_w_in', 'lru_w_out')}
STACKED_OPERANDS = ('attn_w_qkv', 'mlp_w_up')
LRU_SMALL = ('lru_conv_w', 'lru_conv_b', 'lru_b_a', 'lru_b_x', 'lru_lambda')


def _layer_items(i):
    w_in, w_out = MIXER_WEIGHTS[i % N_MIXERS]
    j = i // N_MIXERS
    return [('w_in', w_in, j), ('w_out', w_out, j), ('mlp_up', 'mlp_w_up', i), ('mlp_down', 'mlp_w_down', i),
            ('ple_gate', 'ple_w_gate', i), ('ple_proj', 'ple_w_proj', i)]


def _cols_to_full(stacked):
    return jnp.moveaxis(stacked, 0, 1).reshape(stacked.shape[1], -1)


def _full_to_cols(full):
    k, n = full.shape
    return jnp.moveaxis(full.reshape(k, N_DEV, n // N_DEV), 1, 0)


def _pad_to(a, rows):
    return jnp.pad(a, ((0, rows - a.shape[0]), (0, 0)))


def _small_block(src, i):
    kind, j = i % N_MIXERS, i // N_MIXERS
    if kind == 0:
        return _pad_to(src['sc_w_conv'][j], SUB)
    if kind == 2:
        return jnp.concatenate([src[n][j].reshape(-1, src[n].shape[-1]) for n in LRU_SMALL], axis=0)
    return None


def kernel(x, p, positions, norm_mix, norm_mlp, norm_ple, norm_final, sc_w_in, sc_w_conv, sc_w_out, attn_w_qkv, attn_w_o, lru_w_in, lru_conv_w, lru_conv_b, lru_w_a, lru_b_a, lru_w_x, lru_b_x, lru_lambda, lru_w_out, mlp_w_up, mlp_w_down, ple_w_gate, ple_w_proj, loss_target, m_norm_mix, m_norm_mlp, m_norm_ple, m_norm_final, m_sc_w_in, m_sc_w_conv, m_sc_w_out, m_attn_w_qkv, m_attn_w_o, m_lru_w_in, m_lru_conv_w, m_lru_conv_b, m_lru_w_a, m_lru_b_a, m_lru_w_x, m_lru_b_x, m_lru_lambda, m_lru_w_out, m_mlp_w_up, m_mlp_w_down, m_ple_w_gate, m_ple_w_proj, v_norm_mix, v_norm_mlp, v_norm_ple, v_norm_final, v_sc_w_in, v_sc_w_conv, v_sc_w_out, v_attn_w_qkv, v_attn_w_o, v_lru_w_in, v_lru_conv_w, v_lru_conv_b, v_lru_w_a, v_lru_b_a, v_lru_w_x, v_lru_b_x, v_lru_lambda, v_lru_w_out, v_mlp_w_up, v_mlp_w_down, v_ple_w_gate, v_ple_w_proj):
    loc = dict(locals())
    shards = {n: loc[n] for n in WEIGHTS}
    moms = {n: loc['m_' + n] for n in WEIGHTS}
    vels = {n: loc['v_' + n] for n in WEIGHTS}

    depth, t, d = p.shape[0], x.shape[1], x.shape[2]

    def comm_kind(name):
        return 'stack' if SHARD_AXIS[name] == 2 else 'rows'

    def layer_shards(i):
        items = _layer_items(i)
        arrs = [shards[n][idx].astype(BF16) for _, n, idx in items]
        kinds = [comm_kind(n) for _, n, _ in items]
        small = _small_block(shards, i)
        if small is not None:
            arrs.append(small)
            kinds.append('stack')
        return items, arrs, kinds

    def layer_weights(i, items, kinds, outs):
        wl = {key: (_cols_to_full(o) if kd == 'stack' and n not in STACKED_OPERANDS else o)
              for (key, n, _), kd, o in zip(items, kinds, outs)}
        if len(outs) > len(items):
            wl['small'] = _cols_to_full(outs[-1])[:shards['sc_w_conv'].shape[1] if i % N_MIXERS == 0 else SUB]
        return wl

    items0, arrs0, kinds0 = layer_shards(0)
    outs0 = _gather_many(arrs0, kinds0, 'gather_weights_0')
    pending = {}

    def start_gather(i, after):
        if i >= depth:
            return None
        items, arrs, kinds = layer_shards(i)
        flight, token = _send_start('gather', arrs, kinds, f'gather_weights_start_{i}', after=after)
        pending[i] = (items, kinds, flight)
        return token

    first_token = start_gather(1, outs0[0])

    def weights_for_layer(i, h):
        if i == 0:
            return layer_weights(0, items0, kinds0, outs0), first_token
        items, kinds, flight = pending.pop(i)
        outs = _send_wait('gather', flight, kinds, h, f'gather_weights_wait_{i}')
        return layer_weights(i, items, kinds, outs), start_gather(i + 1, outs[0])

    part_keys = {'mlp': ('mlp_up', 'mlp_down', 'ple_gate', 'ple_proj'), 'mixer': ('w_in', 'w_out')}
    exchanges = {}

    def emit_grads(i, part, gl):
        items = [it for it in _layer_items(i) if it[0] in part_keys[part]]
        kinds = [comm_kind(n) for _, n, _ in items]
        arrs = [_full_to_cols(gl[key]) if kd == 'stack' and gl[key].ndim == 2 else gl[key]
                for (key, _, _), kd in zip(items, kinds)]
        if part == 'mixer' and i % N_MIXERS == 0:
            arrs.append(_full_to_cols(_pad_to(gl['small'], SUB)))
        elif part == 'mixer' and i % N_MIXERS == 2:
            dv = gl['small']
            arrs.append(_full_to_cols(jnp.concatenate([dv[4:8], dv[3:4], dv[0:1], dv[1:2], dv[2:3]], axis=0)))
        kinds += ['stack'] * (len(arrs) - len(kinds))
        flight, token = _send_start('exchange', arrs, kinds, f'exchange_grads_start_{part}_{i}')
        exchanges[(i, part)] = (items, kinds, flight)
        return token

    rep = {n: shards[n] for n in ('norm_mix', 'norm_mlp', 'norm_ple', 'norm_final')}
    rep['lru_w_a'], rep['lru_w_x'] = shards['lru_w_a'].astype(BF16), shards['lru_w_x'].astype(BF16)
    loss, grad_x, rgrads = _local_step(x.reshape(t, d), p.reshape(depth, t, p.shape[3]), positions.reshape(t, 1),
                                       loss_target.reshape(t, d), rep, weights_for_layer, emit_grads)

    norm_names = ('norm_mix', 'norm_mlp', 'norm_ple', 'norm_final')
    gate_names = ('lru_w_a', 'lru_w_x')

    def norm_block(src):
        cat = jnp.concatenate([src[n].reshape(-1, d) for n in norm_names], axis=0)
        return _pad_to(cat, -(-cat.shape[0] // HALO) * HALO)

    def gate_block(src):
        return jnp.concatenate([src[n].reshape(-1, LRU_BLOCK) for n in gate_names], axis=0)

    rfull = {n: (rgrads[n] if n == 'norm_final' else jnp.stack(rgrads[n], axis=0)) for n in norm_names + gate_names}
    repl_flight, repl_token = _send_start('gather', [norm_block(rfull), gate_block(rfull)], ['stack', 'stack'],
                                          'gather_replicated_grads_start')
    received = {}
    for (i, part), (items, kinds, flight) in exchanges.items():
        outs = _send_wait('exchange', flight, kinds, repl_token, f'exchange_grads_wait_{part}_{i}')
        for (_, n, idx), o in zip(items, outs):
            received[(n, idx)] = o
        if len(outs) > len(items):
            received[('small', i)] = outs[-1]

    res = {}
    for n in WEIGHTS:
        if SHARD_AXIS[n] is not None and shards[n].ndim == 3 and n not in ('sc_w_conv', 'lru_conv_w'):
            res[n] = _adamw_sum(shards[n], [received[(n, l)] for l in range(shards[n].shape[0])], moms[n], vels[n],
                                f'adamw_{n}')
    def small_adamw(layers, name):
        w_, m_, v_ = (jnp.stack([_small_block(src, i) for i in layers]) for src in (shards, moms, vels))
        return _adamw_sum(w_, [received[('small', i)] for i in layers], m_, v_, name)

    sc = small_adamw([i for i in range(depth) if i % N_MIXERS == 0], 'adamw_sc_w_conv')
    res['sc_w_conv'] = tuple(o[:, :shards['sc_w_conv'].shape[1]] for o in sc)
    lru = small_adamw([i for i in range(depth) if i % N_MIXERS == 2], 'adamw_lru_small')
    row = 0
    for n in LRU_SMALL:
        k = shards[n].size // shards[n].shape[0] // shards[n].shape[-1]
        res[n] = tuple(o[:, row:row + k].reshape(shards[n].shape) for o in lru)
        row += k
    parts_norm, parts_gate = _send_wait('gather', repl_flight, ['stack', 'stack'], lru[0],
                                        'gather_replicated_grads_wait')
    norms = _adamw_sum(norm_block(shards)[None], [parts_norm], norm_block(moms)[None], norm_block(vels)[None],
                       'adamw_norms')
    gates = _adamw_sum(gate_block(shards)[None], [parts_gate], gate_block(moms)[None], gate_block(vels)[None],
                       'adamw_lru_gates')
    for names, outs in ((norm_names, norms), (gate_names, gates)):
        row = 0
        for n in names:
            k = shards[n].size // outs[0].shape[-1]
            res[n] = tuple(o[0, row:row + k].reshape(shards[n].shape) for o in outs)
            row += k

    loss = lax.psum(loss[0, 0], ('x', 'y', 'c'))
    return (loss, grad_x.reshape(x.shape), *[res[n][0] for n in WEIGHTS], *[res[n][1] for n in WEIGHTS],
            *[res[n][2] for n in WEIGHTS], *[res[n][3] for n in WEIGHTS])
```

```python
import functools
import math

import jax
import jax.numpy as jnp
from jax import lax
from jax.experimental import pallas as pl
from jax.experimental.pallas import tpu as pltpu

F32 = jnp.float32
BF16 = jnp.bfloat16
SDS = jax.ShapeDtypeStruct

N_DEV = 8
RMS_EPS = 1e-6
N_MIXERS = 3
HEAD_DIM = 128
DILATED_PATTERNS = ((128, 1), (512, 4), (2048, 16))
ATTN_BLOCK = 128
ROPE_THETA = 500000.0
ROPE_DIM = HEAD_DIM // 4
LRU_BLOCK = 128
LRU_C = 8.0
ADAM_LR, ADAM_B1, ADAM_B2, ADAM_EPS, ADAM_WD, ADAM_STEP = 0.001, 0.9, 0.999, 1e-08, 0.01, 10

HALO = 16
SUB = 8
VMEM_LIMIT = 56 * 1024 * 1024
NEG = -1e30

SHARD_AXIS = {
    'norm_mix': None, 'norm_mlp': None, 'norm_ple': None, 'norm_final': None,
    'sc_w_in': 2, 'sc_w_conv': 2, 'sc_w_out': 1, 'attn_w_qkv': 2, 'attn_w_o': 1,
    'lru_w_in': 2, 'lru_conv_w': 2, 'lru_conv_b': 1, 'lru_w_a': None, 'lru_b_a': 1,
    'lru_w_x': None, 'lru_b_x': 1, 'lru_lambda': 1, 'lru_w_out': 1,
    'mlp_w_up': 2, 'mlp_w_down': 1, 'ple_w_gate': 1, 'ple_w_proj': 2,
}
WEIGHTS = list(SHARD_AXIS)


def _params(*sem):
    return pltpu.CompilerParams(dimension_semantics=sem or None, vmem_limit_bytes=VMEM_LIMIT)


def _row_tile(t, pref=256):
    tr = min(t, pref)
    assert t % tr == 0 and tr % HALO == 0
    return tr


def _row(tr, c, col=0):
    return pl.BlockSpec((tr, c), lambda i, col=col: (i, col))


def _full(shape):
    return pl.BlockSpec(shape, lambda *_: (0,) * len(shape))


def _sigmoid(x):
    return 1.0 / (1.0 + jnp.exp(-x))


def _expm1(x):
    taylor = x * (1.0 + x * (0.5 + x * (1.0 / 6.0 + x * (1.0 / 24.0 + x * (1.0 / 120.0)))))
    return jnp.where(jnp.abs(x) < 0.1, taylor, jnp.exp(x) - 1.0)


def _softplus(x):
    z = jnp.exp(-jnp.abs(x))
    log1p = jnp.where(z < 0.01, z * (1.0 - z * (0.5 - z * (1.0 / 3.0 - z * 0.25))), jnp.log(1.0 + z))
    return jnp.maximum(x, 0.0) + log1p


_GELU_K = math.sqrt(2.0 / math.pi)


def _gelu_and_grad(x):
    inner = _GELU_K * (x + 0.044715 * x * x * x)
    th = jnp.tanh(inner)
    g = 0.5 * x * (1.0 + th)
    dg = 0.5 * (1.0 + th) + 0.5 * x * (1.0 - th * th) * _GELU_K * (1.0 + 3.0 * 0.044715 * x * x)
    return g, dg


def _shift_down(x, k, prev):
    row = lax.broadcasted_iota(jnp.int32, (SUB, x.shape[1]), 0)
    xr = pltpu.roll(x, k, 0)
    top = jnp.where(row < k, pltpu.roll(prev, k, 0), xr[0:SUB])
    return jnp.concatenate([top, xr[SUB:]], axis=0)


def _shift_up(x, k, nxt):
    r = x.shape[0]
    row = lax.broadcasted_iota(jnp.int32, (SUB, x.shape[1]), 0)
    xr = pltpu.roll(x, r - k, 0)
    bot = jnp.where(row >= SUB - k, pltpu.roll(nxt, SUB - k, 0), xr[r - SUB:r])
    return jnp.concatenate([xr[:r - SUB], bot], axis=0)


_DIMS = {'nn': (((1,), (0,)), ((), ())), 'nt': (((1,), (1,)), ((), ())), 'tn': (((0,), (0,)), ((), ()))}


MM_VMEM_BUDGET = 50 * 1024 * 1024
MM_MIN_TK = 1024
MM_MIN_TM = 1024


def _tile_options(dim):
    return [c for c in range(dim, 127, -128) if dim % c == 0] or [dim]


def _choose_tiles(m, n, k, n_span, k_span, a_size, b_size, mn_size, a_temp):
    best = None
    for tm in _tile_options(m):
        for tn in _tile_options(n_span):
            for tk in _tile_options(k_span):
                nk = k // tk
                need = (2 * (tm * tk * a_size + tk * tn * b_size + tm * tn * mn_size) + tm * tn * 4 * (1 + (nk > 1))
                        + tm * tk * 4 * a_temp)
                score = (-min(tk, MM_MIN_TK), -min(tm, MM_MIN_TM), -tm * tn, nk, -min(tm, 2 * MM_MIN_TM), -tn)
                if need <= MM_VMEM_BUDGET and (best is None or score < best[0]):
                    best = (score, (tm, tn, tk))
    return best[1]


def _mm(a, b, dims, name, out_dtypes=(F32,), a_pro=None, extras=(), epi=None, out_stacked=False, dep=None):
    deps = [] if dep is None else [dep]
    stacked = b.ndim == 3
    b_rows, b_cols = (b.shape[1], N_DEV * b.shape[2]) if stacked else b.shape
    if dims == 'nn':
        (m, k), (k2, n) = a.shape, (b_rows, b_cols)
    elif dims == 'nt':
        (m, k), (n, k2) = a.shape, (b_rows, b_cols)
    else:
        (k, m), (k2, n) = a.shape, (b_rows, b_cols)
    assert k == k2, (name, a.shape, b.shape)
    assert not (stacked and dims == 'tn') and not (out_stacked and (extras or dims != 'tn'))
    tm, tn, tk = _choose_tiles(
        m, n, k, n // N_DEV if (out_stacked or (stacked and dims == 'nn')) else n,
        k // N_DEV if (stacked and dims == 'nt') else k, a.dtype.itemsize, b.dtype.itemsize,
        sum(e.dtype.itemsize for e in extras) + sum(jnp.dtype(dt).itemsize for dt in out_dtypes),
        a_pro is not None or a.dtype != BF16)
    assert m % tm == 0 and n % tn == 0 and k % tk == 0, (name, m, n, k)
    nk = k // tk
    a_spec = pl.BlockSpec((tk, tm), lambda i, j, kk: (kk, i)) if dims == 'tn' else pl.BlockSpec((tm, tk), lambda i, j, kk: (i, kk))
    if not stacked:
        b_spec = pl.BlockSpec((tn, tk), lambda i, j, kk: (j, kk)) if dims == 'nt' else pl.BlockSpec((tk, tn), lambda i, j, kk: (kk, j))
    elif dims == 'nn':
        per = b.shape[2] // tn
        b_spec = pl.BlockSpec((None, tk, tn), lambda i, j, kk: (j // per, kk, j % per))
    else:
        per = b.shape[2] // tk
        b_spec = pl.BlockSpec((None, tn, tk), lambda i, j, kk: (kk // per, j, kk % per))
    if out_stacked:
        per_o = n // N_DEV // tn
        o_spec = pl.BlockSpec((None, tm, tn), lambda i, j, kk: (j // per_o, i, j % per_o))
        o_shape = (N_DEV, m, n // N_DEV)
    else:
        o_spec = pl.BlockSpec((tm, tn), lambda i, j, kk: (i, j))
        o_shape = (m, n)
    n_ex, n_out = len(extras), len(out_dtypes)
    for e in extras:
        assert e.shape == (m, n), (name, e.shape)

    def body(a_ref, b_ref, *rest):
        rest = rest[len(deps):]
        ex_refs, out_refs = rest[:n_ex], rest[n_ex:n_ex + n_out]
        kk = pl.program_id(2)
        av = a_ref[...]
        if a_pro is not None:
            av = a_pro(av.astype(F32))
        part = lax.dot_general(av.astype(BF16), b_ref[...].astype(BF16), _DIMS[dims], preferred_element_type=F32)

        def finish(res):
            outs = (res,) if epi is None else epi(res, *[e[...] for e in ex_refs])
            for o_ref, o in zip(out_refs, outs):
                o_ref[...] = o.astype(o_ref.dtype)

        if nk == 1:
            finish(part)
        else:
            acc = rest[-1]

            @pl.when(kk == 0)
            def _():
                acc[...] = part

            @pl.when(kk > 0)
            def _():
                acc[...] += part

            @pl.when(kk == nk - 1)
            def _():
                finish(acc[...])

    out = pl.pallas_call(
        body, grid=(m // tm, n // tn, nk),
        in_specs=[a_spec, b_spec] + [_ANY] * len(deps) + [o_spec] * n_ex,
        out_specs=[o_spec] * n_out,
        out_shape=[SDS(o_shape, d) for d in out_dtypes],
        scratch_shapes=[] if nk == 1 else [pltpu.VMEM((tm, tn), F32)],
        compiler_params=_params('parallel', 'parallel', 'arbitrary'), name=name)(a, b, *deps, *extras)
    return out[0] if n_out == 1 else out


def _relu2(u):
    r = jnp.maximum(u, 0.0)
    return r * r


def _rms_fwd(h, g, name):
    t, d = h.shape
    tr = _row_tile(t)

    def body(h_ref, g_ref, o_ref):
        x = h_ref[...]
        r = lax.rsqrt(jnp.mean(x * x, axis=-1, keepdims=True) + RMS_EPS)
        o_ref[...] = (x * r * g_ref[...]).astype(o_ref.dtype)

    return pl.pallas_call(body, grid=(t // tr,), in_specs=[_row(tr, d), _full((1, d))], out_specs=_row(tr, d),
                          out_shape=SDS((t, d), BF16), compiler_params=_params('parallel'), name=name)(h, g.reshape(1, d))


def _rms_bwd(h, g, dhn, dres, name):
    t, d = h.shape
    tr = _row_tile(t)

    def body(h_ref, g_ref, dhn_ref, dres_ref, dh_ref, dg_ref):
        @pl.when(pl.program_id(0) == 0)
        def _():
            dg_ref[...] = jnp.zeros_like(dg_ref)

        x = h_ref[...]
        r = lax.rsqrt(jnp.mean(x * x, axis=-1, keepdims=True) + RMS_EPS)
        dy = dhn_ref[...].astype(F32)
        gy = dy * g_ref[...]
        dx = r * gy - x * (r * r * r) * jnp.mean(gy * x, axis=-1, keepdims=True)
        dh_ref[...] = dres_ref[...] + dx
        dg_ref[...] += jnp.sum(dy * (x * r), axis=0, keepdims=True)

    return pl.pallas_call(body, grid=(t // tr,),
                          in_specs=[_row(tr, d), _full((1, d)), _row(tr, d), _row(tr, d)],
                          out_specs=[_row(tr, d), _full((1, d))],
                          out_shape=[SDS((t, d), F32), SDS((1, d), F32)],
                          compiler_params=_params('arbitrary'), name=name)(h, g.reshape(1, d), dhn, dres)


def _head(h, g, target, name):
    t, d = h.shape
    tr = _row_tile(t)

    def body(h_ref, g_ref, t_ref, dh_ref, loss_ref, dg_ref):
        @pl.when(pl.program_id(0) == 0)
        def _():
            dg_ref[...] = jnp.zeros_like(dg_ref)
            loss_ref[...] = jnp.zeros_like(loss_ref)

        x = h_ref[...]
        gv = g_ref[...]
        r = lax.rsqrt(jnp.mean(x * x, axis=-1, keepdims=True) + RMS_EPS)
        xh = x * r
        e = xh * gv - t_ref[...]
        per_tok = jnp.mean(e * e, axis=-1, keepdims=True)
        loss_ref[...] += jnp.broadcast_to(0.5 * jnp.sum(per_tok, axis=0, keepdims=True), loss_ref.shape)
        dy = e * (1.0 / d)
        gy = dy * gv
        dh_ref[...] = r * gy - x * (r * r * r) * jnp.mean(gy * x, axis=-1, keepdims=True)
        dg_ref[...] += jnp.sum(dy * xh, axis=0, keepdims=True)

    return pl.pallas_call(body, grid=(t // tr,),
                          in_specs=[_row(tr, d), _full((1, d)), _row(tr, d)],
                          out_specs=[_row(tr, d), _full((1, 128)), _full((1, d))],
                          out_shape=[SDS((t, d), F32), SDS((1, 128), F32), SDS((1, d), F32)],
                          compiler_params=_params('arbitrary'), name=name)(h, g.reshape(1, d), target)


def _ple_bwd_gate(dh3, gate, pp, name):
    t, d = dh3.shape
    tr = _row_tile(t)

    def body(dh_ref, g_ref, pp_ref, dpp_ref, dgl_ref):
        dh = dh_ref[...]
        gt = g_ref[...]
        dpp_ref[...] = (dh * gt).astype(dpp_ref.dtype)
        dgl_ref[...] = (dh * pp_ref[...] * gt * (1.0 - gt)).astype(dgl_ref.dtype)

    return pl.pallas_call(body, grid=(t // tr,), in_specs=[_row(tr, d)] * 3, out_specs=[_row(tr, d)] * 2,
                          out_shape=[SDS((t, d), BF16), SDS((t, d), BF16)],
                          compiler_params=_params('parallel'), name=name)(dh3, gate, pp)


def _halo_prev(tr, c, col=0):
    return pl.BlockSpec((HALO, c), lambda i, col=col: (jnp.maximum(i * (tr // HALO) - 1, 0), col))


def _halo_next(tr, c, t, col=0):
    return pl.BlockSpec((HALO, c), lambda i, col=col: (jnp.minimum((i + 1) * (tr // HALO), t // HALO - 1), col))


def _sc_fwd(z, w, name):
    t, c3 = z.shape
    c = c3 // 3
    tr = _row_tile(t)

    def body(z_ref, zp_ref, w_ref, y_ref):
        i = pl.program_id(0)
        zz = z_ref[...]
        gb, cx = zz[:, :c], zz[:, c:2 * c] * zz[:, 2 * c:]
        zp = zp_ref[SUB:HALO, :]
        cxp = jnp.where(i > 0, zp[:, c:2 * c] * zp[:, 2 * c:], 0.0)
        wv = w_ref[...]
        conv = wv[2:3] * cx + wv[1:2] * _shift_down(cx, 1, cxp) + wv[0:1] * _shift_down(cx, 2, cxp)
        y_ref[...] = (gb * conv).astype(y_ref.dtype)

    return pl.pallas_call(body, grid=(t // tr,),
                          in_specs=[_row(tr, c3), _halo_prev(tr, c3), _full((3, c))],
                          out_specs=_row(tr, c), out_shape=SDS((t, c), BF16),
                          compiler_params=_params('parallel'), name=name)(z, z, w)


def _sc_bwd(dy, z, w, name):
    t, c3 = z.shape
    c = c3 // 3
    tr = _row_tile(t)
    nt = t // tr

    def body(dy_ref, dyn_ref, z_ref, zp_ref, zn_ref, w_ref, dz_ref, dw_ref):
        i = pl.program_id(0)

        @pl.when(i == 0)
        def _():
            dw_ref[...] = jnp.zeros_like(dw_ref)

        zz = z_ref[...]
        gb, gc, xi = zz[:, :c], zz[:, c:2 * c], zz[:, 2 * c:]
        cx = gc * xi
        zp = zp_ref[SUB:HALO, :]
        cxp = jnp.where(i > 0, zp[:, c:2 * c] * zp[:, 2 * c:], 0.0)
        wv = w_ref[...]
        cx1, cx2 = _shift_down(cx, 1, cxp), _shift_down(cx, 2, cxp)
        conv = wv[2:3] * cx + wv[1:2] * cx1 + wv[0:1] * cx2
        dyv = dy_ref[...]
        dconv = dyv * gb
        dcn = jnp.where(i < nt - 1, dyn_ref[0:SUB, :] * zn_ref[0:SUB, :c], 0.0)
        dcx = wv[2:3] * dconv + wv[1:2] * _shift_up(dconv, 1, dcn) + wv[0:1] * _shift_up(dconv, 2, dcn)
        dz_ref[:, :c] = (dyv * conv).astype(dz_ref.dtype)
        dz_ref[:, c:2 * c] = (dcx * xi).astype(dz_ref.dtype)
        dz_ref[:, 2 * c:] = (dcx * gc).astype(dz_ref.dtype)
        dw_ref[...] += jnp.concatenate([jnp.sum(dconv * cx2, axis=0, keepdims=True),
                                        jnp.sum(dconv * cx1, axis=0, keepdims=True),
                                        jnp.sum(dconv * cx, axis=0, keepdims=True)], axis=0)

    return pl.pallas_call(body, grid=(nt,),
                          in_specs=[_row(tr, c), _halo_next(tr, c, t), _row(tr, c3), _halo_prev(tr, c3),
                                    _halo_next(tr, c3, t), _full((3, c))],
                          out_specs=[_row(tr, c3), _full((3, c))],
                          out_shape=[SDS((t, c3), BF16), SDS((3, c), F32)],
                          compiler_params=_params('arbitrary'), name=name)(dy, dy, z, z, z, w)


def _perm(tr, dil, inverse=False):
    n = tr // dil
    a = lax.broadcasted_iota(jnp.int32, (tr, tr), 1 if inverse else 0)
    b = lax.broadcasted_iota(jnp.int32, (tr, tr), 0 if inverse else 1)
    return (b == (a % n) * dil + a // n).astype(BF16)


def _permute(pm, x, terms):
    if x.dtype == BF16:
        return jnp.dot(pm, x, preferred_element_type=F32)
    acc = None
    for _ in range(terms):
        part = x.astype(BF16)
        y = jnp.dot(pm, part, preferred_element_type=F32)
        acc = y if acc is None else acc + y
        x = x - part.astype(F32)
    return acc


def _store_dilated(o_ref, y, dil, d):
    n = y.shape[0] // dil
    for rho in range(dil):
        o_ref[:, rho * d:(rho + 1) * d] = y[rho * n:(rho + 1) * n].astype(o_ref.dtype)


def _load_dilated(ref, dil, d):
    return jnp.concatenate([ref[:, rho * d:(rho + 1) * d] for rho in range(dil)], axis=0) if dil > 1 else ref[...]


def _rope_heads(x, lane, cos, sin):
    return jnp.concatenate([_rope_apply(x[:, s:s + HEAD_DIM], lane, cos, sin)
                            for s in range(0, x.shape[1], HEAD_DIM)], axis=1)


def _rope_tables(pos, invf, sign):
    lane = lax.broadcasted_iota(jnp.int32, (pos.shape[0], HEAD_DIM), 1)
    ang = pos.astype(F32) * invf
    half = ROPE_DIM // 2
    cos = jnp.where(lane < ROPE_DIM, jnp.cos(ang), 1.0)
    sin = jnp.sin(ang) * sign
    sin = jnp.where(lane < half, -sin, jnp.where(lane < ROPE_DIM, sin, 0.0))
    return lane, cos, sin


def _rope_apply(x, lane, cos, sin):
    half = ROPE_DIM // 2
    xs = jnp.where(lane < half, pltpu.roll(x, HEAD_DIM - half, 1), pltpu.roll(x, half, 1))
    return x * cos + xs * sin


def _dilated_spec(tr, dil, d):
    return pl.BlockSpec((tr // dil, dil * d), lambda i: (i, 0))


def _rope_fwd(qkv, pos, invf, dils, name):
    t, w3 = qkv.shape
    w, ng = w3 // 3, len(dils)
    d = w // ng
    tr = _row_tile(t)

    def body(q_ref, k_ref, v_ref, pos_ref, invf_ref, *out_refs):
        lane, cos, sin = _rope_tables(pos_ref[...], invf_ref[...], 1.0)
        for g, dil in enumerate(dils):
            cs = slice(g * d, (g + 1) * d)
            vals = [_rope_heads(q_ref[:, cs], lane, cos, sin).astype(BF16),
                    _rope_heads(k_ref[:, cs], lane, cos, sin).astype(BF16), v_ref[:, cs].astype(BF16)]
            if dil > 1:
                pm = _perm(tr, dil)
                vals = [_permute(pm, a, 1) for a in vals]
            for o_ref, a in zip(out_refs[g::ng], vals):
                _store_dilated(o_ref, a, dil, d)

    outs = pl.pallas_call(body, grid=(t // tr,),
                          in_specs=[_row(tr, w, 0), _row(tr, w, 1), _row(tr, w, 2), _row(tr, 1), _full((1, HEAD_DIM))],
                          out_specs=[_dilated_spec(tr, dil, d) for dil in dils] * 3,
                          out_shape=[SDS((t // dil, dil * d), BF16) for dil in dils] * 3,
                          compiler_params=_params('parallel'), name=name)(qkv, qkv, qkv, pos, invf)
    return outs[:ng], outs[ng:2 * ng], outs[2 * ng:]


def _rope_bwd(dqs, dks, dvs, pos, invf, dils, name):
    ng = len(dils)
    t = dqs[0].shape[0] * dils[0]
    d = dqs[0].shape[1] // dils[0]
    w = ng * d
    tr = _row_tile(t)

    def body(*refs):
        dq_refs, dk_refs, dv_refs = refs[:ng], refs[ng:2 * ng], refs[2 * ng:3 * ng]
        pos_ref, invf_ref, o_ref = refs[3 * ng:]
        pos_f = jnp.broadcast_to(pos_ref[...].astype(F32), (tr, HEAD_DIM))
        for g, dil in enumerate(dils):
            pos_g = pos_f if dil == 1 else _permute(_perm(tr, dil), pos_f, 3)
            lane, cos, sin = _rope_tables(pos_g, invf_ref[...], -1.0)
            vals = [_rope_heads(_load_dilated(dq_refs[g], dil, d), lane, cos, sin),
                    _rope_heads(_load_dilated(dk_refs[g], dil, d), lane, cos, sin), _load_dilated(dv_refs[g], dil, d)]
            back = _perm(tr, dil, inverse=True) if dil > 1 else None
            for sec, a in enumerate(vals):
                a = a.astype(BF16)
                if dil > 1:
                    a = _permute(back, a, 1)
                o_ref[:, sec * w + g * d:sec * w + (g + 1) * d] = a.astype(o_ref.dtype)

    return pl.pallas_call(body, grid=(t // tr,),
                          in_specs=[_dilated_spec(tr, dil, d) for dil in dils] * 3 + [_row(tr, 1), _full((1, HEAD_DIM))],
                          out_specs=_row(tr, 3 * w), out_shape=SDS((t, 3 * w), BF16),
                          compiler_params=_params('parallel'), name=name)(*dqs, *dks, *dvs, pos, invf)


def _dilate_many(arrs, dil, terms, out_dtypes, name):
    t, d = arrs[0].shape
    tr = _row_tile(t)
    na = len(arrs)

    def body(*refs):
        pm = _perm(tr, dil)
        for a_ref, o_ref, k in zip(refs[:na], refs[na:], terms):
            _store_dilated(o_ref, _permute(pm, a_ref[...], k), dil, d)

    return pl.pallas_call(body, grid=(t // tr,), in_specs=[_row(tr, d)] * na,
                          out_specs=[_dilated_spec(tr, dil, d)] * na,
                          out_shape=[SDS((t // dil, dil * d), dt) for dt in out_dtypes],
                          compiler_params=_params('parallel'), name=name)(*arrs)


def _attn_masks():
    qi = lax.broadcasted_iota(jnp.int32, (ATTN_BLOCK, ATTN_BLOCK), 0)
    kj = lax.broadcasted_iota(jnp.int32, (ATTN_BLOCK, ATTN_BLOCK), 1)
    return kj >= qi, kj <= qi


def _attn_cols(l, width):
    ncol = width // HEAD_DIM
    cpb = max(1, min(ncol, 32 // (l // ATTN_BLOCK)))
    assert ncol % cpb == 0
    return cpb


def _attn_fwd(q, k, v, name):
    l, width = q.shape
    cpb = _attn_cols(l, width)
    nb = l // ATTN_BLOCK
    scale = HEAD_DIM ** -0.5

    def body(q_ref, k_ref, v_ref, o_ref, lse_ref):
        m_prev, m_cur = _attn_masks()
        for col in range(cpb):
            cs = slice(col * HEAD_DIM, (col + 1) * HEAD_DIM)

            def step(b, carry, cs=cs):
                r0 = pl.multiple_of(b * ATTN_BLOCK, ATTN_BLOCK)
                rp = pl.multiple_of(jnp.maximum(b - 1, 0) * ATTN_BLOCK, ATTN_BLOCK)
                qb = q_ref[pl.ds(r0, ATTN_BLOCK), cs]
                s_p = lax.dot_general(qb, k_ref[pl.ds(rp, ATTN_BLOCK), cs], _DIMS['nt'], preferred_element_type=F32) * scale
                s_c = lax.dot_general(qb, k_ref[pl.ds(r0, ATTN_BLOCK), cs], _DIMS['nt'], preferred_element_type=F32) * scale
                s_p = jnp.where(jnp.logical_and(m_prev, b > 0), s_p, NEG)
                s_c = jnp.where(m_cur, s_c, NEG)
                m = jnp.maximum(jnp.max(s_p, axis=-1, keepdims=True), jnp.max(s_c, axis=-1, keepdims=True))
                p_p, p_c = jnp.exp(s_p - m), jnp.exp(s_c - m)
                den = jnp.sum(p_p, axis=-1, keepdims=True) + jnp.sum(p_c, axis=-1, keepdims=True)
                acc = jnp.dot(p_p.astype(BF16), v_ref[pl.ds(rp, ATTN_BLOCK), cs], preferred_element_type=F32)
                acc += jnp.dot(p_c.astype(BF16), v_ref[pl.ds(r0, ATTN_BLOCK), cs], preferred_element_type=F32)
                o_ref[pl.ds(r0, ATTN_BLOCK), cs] = acc / den
                lse_ref[pl.ds(r0, ATTN_BLOCK), cs] = jnp.broadcast_to(m + jnp.log(den), (ATTN_BLOCK, HEAD_DIM))
                return carry

            lax.fori_loop(0, nb, step, 0, unroll=min(nb, 4))

    spec = pl.BlockSpec((l, cpb * HEAD_DIM), lambda j: (0, j))
    return pl.pallas_call(body, grid=(width // (cpb * HEAD_DIM),), in_specs=[spec] * 3, out_specs=[spec] * 2,
                          out_shape=[SDS((l, width), F32)] * 2,
                          compiler_params=_params('parallel'), name=name)(q, k, v)


def _attn_bwd(q, k, v, do, lse, delta, name):
    l, width = q.shape
    cpb = _attn_cols(l, width)
    nb = l // ATTN_BLOCK
    scale = HEAD_DIM ** -0.5

    def body(q_ref, k_ref, v_ref, do_ref, lse_ref, dl_ref, dq_ref, dk_ref, dv_ref):
        m_prev, m_cur = _attn_masks()
        dk_ref[...] = jnp.zeros_like(dk_ref)
        dv_ref[...] = jnp.zeros_like(dv_ref)
        for col in range(cpb):
            cs = slice(col * HEAD_DIM, (col + 1) * HEAD_DIM)

            def step(b, carry, cs=cs):
                r0 = pl.multiple_of(b * ATTN_BLOCK, ATTN_BLOCK)
                rp = pl.multiple_of(jnp.maximum(b - 1, 0) * ATTN_BLOCK, ATTN_BLOCK)
                qb, dob = q_ref[pl.ds(r0, ATTN_BLOCK), cs], do_ref[pl.ds(r0, ATTN_BLOCK), cs].astype(BF16)
                kp, kc = k_ref[pl.ds(rp, ATTN_BLOCK), cs], k_ref[pl.ds(r0, ATTN_BLOCK), cs]
                vp, vc = v_ref[pl.ds(rp, ATTN_BLOCK), cs], v_ref[pl.ds(r0, ATTN_BLOCK), cs]
                lse_b = lse_ref[pl.ds(r0, ATTN_BLOCK), cs]
                dl_b = dl_ref[pl.ds(r0, ATTN_BLOCK), cs]
                s_p = lax.dot_general(qb, kp, _DIMS['nt'], preferred_element_type=F32) * scale
                s_c = lax.dot_general(qb, kc, _DIMS['nt'], preferred_element_type=F32) * scale
                p_p = jnp.exp(jnp.where(jnp.logical_and(m_prev, b > 0), s_p, NEG) - lse_b)
                p_c = jnp.exp(jnp.where(m_cur, s_c, NEG) - lse_b)
                dp_p = lax.dot_general(dob, vp, _DIMS['nt'], preferred_element_type=F32)
                dp_c = lax.dot_general(dob, vc, _DIMS['nt'], preferred_element_type=F32)
                ds_p = (p_p * (dp_p - dl_b) * scale).astype(BF16)
                ds_c = (p_c * (dp_c - dl_b) * scale).astype(BF16)
                dq_ref[pl.ds(r0, ATTN_BLOCK), cs] = (jnp.dot(ds_p, kp, preferred_element_type=F32)
                                                     + jnp.dot(ds_c, kc, preferred_element_type=F32))
                dk_ref[pl.ds(rp, ATTN_BLOCK), cs] += lax.dot_general(ds_p, qb, _DIMS['tn'], preferred_element_type=F32)
                dk_ref[pl.ds(r0, ATTN_BLOCK), cs] += lax.dot_general(ds_c, qb, _DIMS['tn'], preferred_element_type=F32)
                dv_ref[pl.ds(rp, ATTN_BLOCK), cs] += lax.dot_general(p_p.astype(BF16), dob, _DIMS['tn'], preferred_element_type=F32)
                dv_ref[pl.ds(r0, ATTN_BLOCK), cs] += lax.dot_general(p_c.astype(BF16), dob, _DIMS['tn'], preferred_element_type=F32)
                return carry

            lax.fori_loop(0, nb, step, 0, unroll=min(nb, 2))

    spec = pl.BlockSpec((l, cpb * HEAD_DIM), lambda j: (0, j))
    return pl.pallas_call(body, grid=(width // (cpb * HEAD_DIM),), in_specs=[spec] * 6, out_specs=[spec] * 3,
                          out_shape=[SDS((l, width), F32)] * 3,
                          compiler_params=_params('parallel'), name=name)(q, k, v, do, lse, delta)


def _attn_combine(os_, lses, dils, name):
    ng = len(dils)
    t = os_[0].shape[0] * dils[0]
    d = os_[0].shape[1] // dils[0]
    tr = _row_tile(t)

    def body(*refs):
        o_refs, l_refs, o_out, lse_out = refs[:ng], refs[ng:2 * ng], refs[2 * ng], refs[2 * ng + 1]
        ovs, ls = [], []
        for g, dil in enumerate(dils):
            ov, lv = _load_dilated(o_refs[g], dil, d), _load_dilated(l_refs[g], dil, d)
            if dil > 1:
                back = _perm(tr, dil, inverse=True)
                ov, lv = _permute(back, ov, 2), _permute(back, lv, 3)
            ovs.append(ov)
            ls.append(lv)
        m = functools.reduce(jnp.maximum, ls)
        ws = [jnp.exp(x - m) for x in ls]
        den = functools.reduce(lambda a, b: a + b, ws)
        acc = functools.reduce(lambda a, b: a + b, [w * o for w, o in zip(ws, ovs)])
        o_out[...] = (acc / den).astype(o_out.dtype)
        lse_out[...] = m + jnp.log(den)

    return pl.pallas_call(body, grid=(t // tr,), in_specs=[_dilated_spec(tr, dil, d) for dil in dils] * 2,
                          out_specs=[_row(tr, d)] * 2, out_shape=[SDS((t, d), BF16), SDS((t, d), F32)],
                          compiler_params=_params('parallel'), name=name)(*os_, *lses)


def _delta_epilogue(acc, o):
    prod = acc * o.astype(F32)
    segs = [jnp.broadcast_to(jnp.sum(prod[:, s:s + HEAD_DIM], axis=-1, keepdims=True), (acc.shape[0], HEAD_DIM))
            for s in range(0, acc.shape[1], HEAD_DIM)]
    return acc, jnp.concatenate(segs, axis=-1)


LRU_TILE = 128


def _lru_gates(xr, wa_ref, ba, wx_ref, bx, lam):
    nb = wa_ref.shape[0]
    xb = xr.astype(BF16)
    ra = jnp.concatenate([jnp.dot(xb[:, n * LRU_BLOCK:(n + 1) * LRU_BLOCK], wa_ref[n], preferred_element_type=F32)
                          for n in range(nb)], axis=-1) + ba
    ia = jnp.concatenate([jnp.dot(xb[:, n * LRU_BLOCK:(n + 1) * LRU_BLOCK], wx_ref[n], preferred_element_type=F32)
                          for n in range(nb)], axis=-1) + bx
    r, ig = _sigmoid(ra), _sigmoid(ia)
    sp = _softplus(-lam)
    log_a = -LRU_C * r * sp
    a = jnp.exp(log_a)
    mult = jnp.sqrt(-_expm1(2.0 * log_a))
    return xb, r, ig, sp, a, mult


def _lru_fwd(z, cw, cb, wa, ba, wx, bx, lam, name):
    t, c2 = z.shape
    c = c2 // 2
    nb = c // LRU_BLOCK
    tr = _row_tile(t, LRU_TILE)

    def body(g_ref, x_ref, xp_ref, cw_ref, cb_ref, wa_ref, ba_ref, wx_ref, bx_ref, lam_ref,
             y_ref, hs_ref, xr_ref, car_ref):
        i = pl.program_id(0)

        @pl.when(i == 0)
        def _():
            car_ref[...] = jnp.zeros_like(car_ref)

        x0 = x_ref[...]
        xp = jnp.where(i > 0, xp_ref[SUB:HALO, :], 0.0)
        cwv = cw_ref[...]
        xr = (cb_ref[...] + cwv[3:4] * x0 + cwv[2:3] * _shift_down(x0, 1, xp)
              + cwv[1:2] * _shift_down(x0, 2, xp) + cwv[0:1] * _shift_down(x0, 3, xp))
        xr_ref[...] = xr
        _, _, ig, _, a, mult = _lru_gates(xr, wa_ref, ba_ref[...], wx_ref, bx_ref[...], lam_ref[...])
        u = mult * (ig * xr)
        row = lax.broadcasted_iota(jnp.int32, (SUB, c), 0)
        car = car_ref[...]
        for j in range(tr // SUB):
            ab, ub = a[j * SUB:(j + 1) * SUB], u[j * SUB:(j + 1) * SUB]
            for s in (1, 2, 4):
                a_sh = jnp.where(row >= s, pltpu.roll(ab, s, 0), 1.0)
                u_sh = jnp.where(row >= s, pltpu.roll(ub, s, 0), 0.0)
                ub = ab * u_sh + ub
                ab = ab * a_sh
            hb = ub + ab * car
            hs_ref[j * SUB:(j + 1) * SUB, :] = hb
            car = jnp.broadcast_to(hb[SUB - 1:SUB], (SUB, c))
        car_ref[...] = car
        gl, _ = _gelu_and_grad(g_ref[...])
        y_ref[...] = (hs_ref[...] * gl).astype(y_ref.dtype)

    return pl.pallas_call(
        body, grid=(t // tr,),
        in_specs=[_row(tr, c, 0), _row(tr, c, 1), _halo_prev(tr, c, 1), _full((4, c)), _full((1, c)),
                  _full((nb, LRU_BLOCK, LRU_BLOCK)), _full((1, c)), _full((nb, LRU_BLOCK, LRU_BLOCK)), _full((1, c)), _full((1, c))],
        out_specs=[_row(tr, c)] * 3,
        out_shape=[SDS((t, c), BF16), SDS((t, c), F32), SDS((t, c), F32)],
        scratch_shapes=[pltpu.VMEM((SUB, c), F32)],
        compiler_params=_params('arbitrary'), name=name)(
            z, z, z, cw, cb.reshape(1, c), wa, ba.reshape(1, c), wx, bx.reshape(1, c), lam.reshape(1, c))


def _lru_bwd(dy, z, xr, hs, cw, wa, ba, wx, bx, lam, name):
    t, c2 = z.shape
    c = c2 // 2
    nb = c // LRU_BLOCK
    tr = _row_tile(t, LRU_TILE)
    nt = t // tr

    def rev(col=0):
        return pl.BlockSpec((tr, c), lambda i, col=col: (nt - 1 - i, col))

    def rev_prev(col=0):
        return pl.BlockSpec((HALO, c), lambda i, col=col: (jnp.maximum((nt - 1 - i) * (tr // HALO) - 1, 0), col))

    def body(dy_ref, g_ref, x_ref, xp_ref, xr_ref, hs_ref, hp_ref, cw_ref, wa_ref, ba_ref, wx_ref, bx_ref, lam_ref,
             dz_ref, dwa_ref, dwx_ref, dvec_ref, lcar_ref, ahead_ref, dxhead_ref, lam_s):
        i = pl.program_id(0)
        first_tile = i == nt - 1

        @pl.when(i == 0)
        def _():
            lcar_ref[...] = jnp.zeros_like(lcar_ref)
            ahead_ref[...] = jnp.zeros_like(ahead_ref)
            dxhead_ref[...] = jnp.zeros_like(dxhead_ref)
            dwa_ref[...] = jnp.zeros_like(dwa_ref)
            dwx_ref[...] = jnp.zeros_like(dwx_ref)
            dvec_ref[...] = jnp.zeros_like(dvec_ref)

        xrv = xr_ref[...]
        lamv = lam_ref[...]
        xb, r, ig, sp, a, mult = _lru_gates(xrv, wa_ref, ba_ref[...], wx_ref, bx_ref[...], lamv)
        hsv = hs_ref[...]
        dyv = dy_ref[...]
        gl, dgl = _gelu_and_grad(g_ref[...])
        dhs = dyv * gl
        dz_ref[:, :c] = (dyv * hsv * dgl).astype(dz_ref.dtype)

        a_next = _shift_up(a, 1, ahead_ref[...])
        row = lax.broadcasted_iota(jnp.int32, (SUB, c), 0)
        car = lcar_ref[...]
        for j in reversed(range(tr // SUB)):
            ab, ub = a_next[j * SUB:(j + 1) * SUB], dhs[j * SUB:(j + 1) * SUB]
            for s in (1, 2, 4):
                a_sh = jnp.where(row < SUB - s, pltpu.roll(ab, SUB - s, 0), 1.0)
                u_sh = jnp.where(row < SUB - s, pltpu.roll(ub, SUB - s, 0), 0.0)
                ub = ab * u_sh + ub
                ab = ab * a_sh
            lb = ub + ab * car
            lam_s[j * SUB:(j + 1) * SUB, :] = lb
            car = jnp.broadcast_to(lb[0:1], (SUB, c))
        lcar_ref[...] = car
        ahead_ref[...] = a[0:SUB]
        lmb = lam_s[...]

        hp = jnp.where(first_tile, 0.0, hp_ref[SUB:HALO, :])
        h_prev = _shift_down(hsv, 1, hp)
        d_a = lmb * h_prev
        d_mult = lmb * (ig * xrv)
        d_ixr = lmb * mult
        d_ig = d_ixr * xrv
        dxr = d_ixr * ig
        d_la = d_a * a - d_mult * (a * a) / mult
        d_r = d_la * (-LRU_C * sp)
        d_sp = jnp.sum(d_la * (-LRU_C * r), axis=0, keepdims=True)
        d_ra = d_r * r * (1.0 - r)
        d_ia = d_ig * ig * (1.0 - ig)
        d_rab, d_iab = d_ra.astype(BF16), d_ia.astype(BF16)
        parts = []
        for n in range(nb):
            cs = slice(n * LRU_BLOCK, (n + 1) * LRU_BLOCK)
            parts.append(lax.dot_general(d_rab[:, cs], wa_ref[n], _DIMS['nt'], preferred_element_type=F32)
                         + lax.dot_general(d_iab[:, cs], wx_ref[n], _DIMS['nt'], preferred_element_type=F32))
            dwa_ref[n] += lax.dot_general(xb[:, cs], d_rab[:, cs], _DIMS['tn'], preferred_element_type=F32)
            dwx_ref[n] += lax.dot_general(xb[:, cs], d_iab[:, cs], _DIMS['tn'], preferred_element_type=F32)
        dxr = dxr + jnp.concatenate(parts, axis=-1)

        cwv = cw_ref[...]
        nxt = dxhead_ref[...]
        dx0 = (cwv[3:4] * dxr + cwv[2:3] * _shift_up(dxr, 1, nxt) + cwv[1:2] * _shift_up(dxr, 2, nxt)
               + cwv[0:1] * _shift_up(dxr, 3, nxt))
        dxhead_ref[...] = dxr[0:SUB]
        dz_ref[:, c:] = dx0.astype(dz_ref.dtype)

        x0 = x_ref[...]
        xp = jnp.where(first_tile, 0.0, xp_ref[SUB:HALO, :])
        sums = [jnp.sum(d_ra, axis=0, keepdims=True), jnp.sum(d_ia, axis=0, keepdims=True),
                d_sp * (-_sigmoid(-lamv)), jnp.sum(dxr, axis=0, keepdims=True),
                jnp.sum(dxr * _shift_down(x0, 3, xp), axis=0, keepdims=True),
                jnp.sum(dxr * _shift_down(x0, 2, xp), axis=0, keepdims=True),
                jnp.sum(dxr * _shift_down(x0, 1, xp), axis=0, keepdims=True),
                jnp.sum(dxr * x0, axis=0, keepdims=True)]
        dvec_ref[...] += jnp.concatenate(sums, axis=0)

    wspec = _full((nb, LRU_BLOCK, LRU_BLOCK))
    return pl.pallas_call(
        body, grid=(nt,),
        in_specs=[rev(), rev(0), rev(1), rev_prev(1), rev(), rev(), rev_prev(), _full((4, c)),
                  wspec, _full((1, c)), wspec, _full((1, c)), _full((1, c))],
        out_specs=[pl.BlockSpec((tr, c2), lambda i: (nt - 1 - i, 0)), wspec, wspec, _full((SUB, c))],
        out_shape=[SDS((t, c2), BF16), SDS((nb, LRU_BLOCK, LRU_BLOCK), F32), SDS((nb, LRU_BLOCK, LRU_BLOCK), F32),
                   SDS((SUB, c), F32)],
        scratch_shapes=[pltpu.VMEM((SUB, c), F32), pltpu.VMEM((SUB, c), F32), pltpu.VMEM((SUB, c), F32),
                        pltpu.VMEM((tr, c), F32)],
        compiler_params=_params('arbitrary'), name=name)(
            dy, z, z, z, xr, hs, hs, cw, wa, ba.reshape(1, c), wx, bx.reshape(1, c), lam.reshape(1, c))


def _local_step(x, p, pos, target, rep, weights_for_layer, emit_grads):
    t, d = x.shape
    depth = p.shape[0]
    w = rep
    half = ROPE_DIM // 2
    invf = ROPE_THETA ** (-2.0 * jnp.arange(half, dtype=F32) / ROPE_DIM)
    invf = jnp.concatenate([invf, invf, jnp.zeros((HEAD_DIM - ROPE_DIM,), F32)]).reshape(1, HEAD_DIM)
    dils = tuple(dil for _, dil in DILATED_PATTERNS)
    saved = []
    h = x
    for i in range(depth):
        kind, j = i % N_MIXERS, i // N_MIXERS
        wl, tok = weights_for_layer(i, h)
        s = {'h0': h, 'wl': wl}
        hn = _rms_fwd(h, w['norm_mix'][i], f'rms_mix_fwd_{i}')
        s['hn'] = hn
        if kind == 0:
            z = _mm(hn, wl['w_in'], 'nn', f'sc_in_{i}', dep=tok)
            y = _sc_fwd(z, wl['small'], f'sc_conv_fwd_{i}')
            h1 = _mm(y, wl['w_out'], 'nn', f'sc_out_{i}', extras=(h,), epi=lambda acc, res: (acc + res,))
            s.update(z=z, y=y)
        elif kind == 1:
            qkv = _mm(hn, wl['w_in'], 'nn', f'attn_qkv_{i}', dep=tok)
            qs, ks, vs = _rope_fwd(qkv, pos, invf, dils, f'rope_fwd_{i}')
            views = list(zip(qs, ks, vs))
            os_, lses = zip(*[_attn_fwd(qg, kg, vg, f'attn_fwd_{i}_g{g}') for g, (qg, kg, vg) in enumerate(views)])
            o, lse = _attn_combine(os_, lses, dils, f'attn_combine_{i}')
            h1 = _mm(o, wl['w_out'], 'nn', f'attn_out_{i}', extras=(h,), epi=lambda acc, res: (acc + res,))
            s.update(views=views, o=o, lse=lse)
        else:
            z = _mm(hn, wl['w_in'], 'nn', f'lru_in_{i}', dep=tok)
            sm = wl['small']
            y, hs, xr = _lru_fwd(z, sm[0:4], sm[4:5], w['lru_w_a'][j], sm[5:6], w['lru_w_x'][j], sm[6:7], sm[7:8],
                                 f'lru_fwd_{i}')
            h1 = _mm(y, wl['w_out'], 'nn', f'lru_out_{i}', extras=(h,), epi=lambda acc, res: (acc + res,))
            s.update(z=z, y=y, hs=hs, xr=xr)
        s['h1'] = h1
        hm = _rms_fwd(h1, w['norm_mlp'][i], f'rms_mlp_fwd_{i}')
        u = _mm(hm, wl['mlp_up'], 'nn', f'mlp_up_{i}', out_dtypes=(BF16,))
        h2 = _mm(u, wl['mlp_down'], 'nn', f'mlp_down_{i}', a_pro=_relu2, extras=(h1,), epi=lambda acc, res: (acc + res,))
        hp = _rms_fwd(h2, w['norm_ple'][i], f'rms_ple_fwd_{i}')
        pp = _mm(p[i], wl['ple_proj'], 'nn', f'ple_proj_{i}')
        h3, gate = _mm(hp, wl['ple_gate'], 'nn', f'ple_gate_{i}', out_dtypes=(F32, F32), extras=(pp, h2),
                       epi=lambda acc, ppv, res: (res + _sigmoid(acc) * ppv, _sigmoid(acc)))
        s.update(hm=hm, u=u, h2=h2, hp=hp, pp=pp, gate=gate)
        saved.append(s)
        h = h3

    dh, loss, dg_final = _head(h, w['norm_final'], target, 'loss_head')
    grads = {n: [None] * depth for n in ('norm_mix', 'norm_mlp', 'norm_ple')}
    grads['norm_final'] = dg_final.reshape(d)
    started = None
    for i in reversed(range(depth)):
        kind, j = i % N_MIXERS, i // N_MIXERS
        s = saved[i]
        wl, gl = s['wl'], {}
        dpp, dgl = _ple_bwd_gate(dh, s['gate'], s['pp'], f'ple_bwd_gate_{i}')
        gl['ple_proj'] = _mm(p[i], dpp, 'tn', f'ple_dproj_{i}', out_dtypes=(BF16,), dep=started)
        gl['ple_gate'] = _mm(s['hp'], dgl, 'tn', f'ple_dgate_{i}', out_dtypes=(BF16,))
        dhp = _mm(dgl, wl['ple_gate'], 'nt', f'ple_dhp_{i}')
        dh, dg = _rms_bwd(s['h2'], w['norm_ple'][i], dhp, dh, f'rms_ple_bwd_{i}')
        grads['norm_ple'][i] = dg.reshape(d)
        du = _mm(dh, wl['mlp_down'], 'nt', f'mlp_du_{i}', out_dtypes=(BF16,), extras=(s['u'],),
                 epi=lambda acc, uv: (acc * 2.0 * jnp.maximum(uv.astype(F32), 0.0),))
        gl['mlp_down'] = _mm(s['u'], dh, 'tn', f'mlp_ddown_{i}', out_dtypes=(BF16,), a_pro=_relu2)
        gl['mlp_up'] = _mm(s['hm'], du, 'tn', f'mlp_dup_{i}', out_dtypes=(BF16,), out_stacked=True)
        dhm = _mm(du, wl['mlp_up'], 'nt', f'mlp_dhm_{i}')
        dh, dg = _rms_bwd(s['h1'], w['norm_mlp'][i], dhm, dh, f'rms_mlp_bwd_{i}')
        grads['norm_mlp'][i] = dg.reshape(d)
        started = emit_grads(i, 'mlp', gl, loss if i == depth - 1 else None)
        gl = {}
        if kind == 0:
            dy = _mm(dh, wl['w_out'], 'nt', f'sc_dy_{i}', dep=started)
            gl['w_out'] = _mm(s['y'], dh, 'tn', f'sc_dout_{i}', out_dtypes=(BF16,))
            dz, dwc = _sc_bwd(dy, s['z'], wl['small'], f'sc_conv_bwd_{i}')
            gl['small'] = dwc
            gl['w_in'] = _mm(s['hn'], dz, 'tn', f'sc_din_{i}', out_dtypes=(BF16,))
            started = emit_grads(i, 'mixer', gl)
            dhn = _mm(dz, wl['w_in'], 'nt', f'sc_dhn_{i}', dep=started)
        elif kind == 1:
            do, delta = _mm(dh, wl['w_out'], 'nt', f'attn_do_{i}', out_dtypes=(BF16, F32), extras=(s['o'],),
                            epi=_delta_epilogue, dep=started)
            gl['w_out'] = _mm(s['o'], dh, 'tn', f'attn_dwo_{i}', out_dtypes=(BF16,))
            rows_in = {1: (do, s['lse'], delta)}
            for dil in dils:
                if dil not in rows_in:
                    rows_in[dil] = _dilate_many([do, s['lse'], delta], dil, (1, 3, 3), (BF16, F32, F32),
                                                f'attn_dilate_{i}_d{dil}')
            dqs, dks, dvs = zip(*[_attn_bwd(*s['views'][g], *rows_in[dil], f'attn_bwd_{i}_g{g}')
                                  for g, dil in enumerate(dils)])
            dqkv = _rope_bwd(dqs, dks, dvs, pos, invf, dils, f'rope_bwd_{i}')
            gl['w_in'] = _mm(s['hn'], dqkv, 'tn', f'attn_dqkv_{i}', out_dtypes=(BF16,), out_stacked=True)
            started = emit_grads(i, 'mixer', gl)
            dhn = _mm(dqkv, wl['w_in'], 'nt', f'attn_dhn_{i}', dep=started)
        else:
            dy = _mm(dh, wl['w_out'], 'nt', f'lru_dy_{i}', dep=started)
            gl['w_out'] = _mm(s['y'], dh, 'tn', f'lru_dout_{i}', out_dtypes=(BF16,))
            sm = wl['small']
            dz, dwa, dwx, dvec = _lru_bwd(dy, s['z'], s['xr'], s['hs'], sm[0:4], w['lru_w_a'][j], sm[5:6],
                                          w['lru_w_x'][j], sm[6:7], sm[7:8], f'lru_bwd_{i}')
            gl['gates'], gl['small'] = (dwa, dwx), dvec
            gl['w_in'] = _mm(s['hn'], dz, 'tn', f'lru_din_{i}', out_dtypes=(BF16,))
            started = emit_grads(i, 'mixer', gl)
            dhn = _mm(dz, wl['w_in'], 'nt', f'lru_dhn_{i}', dep=started)
        dh, dg = _rms_bwd(s['h0'], w['norm_mix'][i], dhn, dh, f'rms_mix_bwd_{i}')
        grads['norm_mix'][i] = dg.reshape(d)
        started = None
    return loss, dh, grads


_MESH = pl.DeviceIdType.MESH
_ANY = pl.BlockSpec(memory_space=pl.ANY)


def _block_view(ref, kind, idx):
    if kind == 'stack':
        return ref.at[idx]
    r = ref.shape[0] // N_DEV
    return ref.at[pl.ds(idx * r, r)]


def _gather_many(arrs, kinds, name, after=None):
    n = len(arrs)
    after = [] if after is None else [after]
    out_shapes = [SDS((N_DEV,) + a.shape if kd == 'stack' else (N_DEV * a.shape[0],) + a.shape[1:], a.dtype)
                  for a, kd in zip(arrs, kinds)]

    def body(*refs):
        x_refs, out_refs = refs[:n], refs[n + len(after):2 * n + len(after)]
        send_sems, recv_sems, local_sems = refs[2 * n + len(after):]
        x, y, c = lax.axis_index('x'), lax.axis_index('y'), lax.axis_index('c')
        me, sibling = (x, y, c), (x, y, 1 - c)
        chips = [(1 - x, y), (x, 1 - y), (1 - x, 1 - y)]

        def slab(t, px, py, pc):
            return _block_view(out_refs[t], kinds[t], 4 * px + 2 * py + pc)

        def copy(t, k, block, to, src=None):
            return pltpu.make_async_remote_copy(
                src_ref=slab(t, *block) if src is None else src, dst_ref=slab(t, *block),
                send_sem=send_sems.at[7 * t + k], recv_sem=recv_sems.at[7 * t + k], device_id=to, device_id_type=_MESH)

        mine = [pltpu.make_async_copy(x_refs[t], slab(t, *me), local_sems.at[t]) for t in range(n)]
        for cp in mine:
            cp.start()
        first = [copy(t, 0, me, sibling, src=x_refs[t]) for t in range(n)]
        first += [copy(t, 1 + j, me, (*chip, c), src=x_refs[t]) for j, chip in enumerate(chips) for t in range(n)]
        for cp in first:
            cp.start()
        passed = []
        for j, chip in enumerate(chips):
            for t in range(n):
                copy(t, 1 + j, (*chip, c), me).wait_recv()
                passed.append(copy(t, 4 + j, (*chip, c), sibling))
                passed[-1].start()
        for t in range(n):
            copy(t, 0, sibling, me).wait_recv()
            for j, chip in enumerate(chips):
                copy(t, 4 + j, (*chip, 1 - c), me).wait_recv()
        for cp in first + passed:
            cp.wait_send()
        for cp in mine:
            cp.wait()

    return pl.pallas_call(
        body, out_shape=out_shapes, in_specs=[_ANY] * (n + len(after)), out_specs=[_ANY] * n,
        scratch_shapes=[pltpu.SemaphoreType.DMA((7 * n,)), pltpu.SemaphoreType.DMA((7 * n,)), pltpu.SemaphoreType.DMA((n,))],
        name=name)(*arrs, *after)


_HBM = pl.BlockSpec(memory_space=pltpu.HBM)
_SEM = pl.BlockSpec(memory_space=pltpu.SEMAPHORE)
_EFFECT = pltpu.SideEffectType.DATAFLOW_SIDE_EFFECTING


def _direct_copies(mode, kinds, src_refs, land_refs, send_sems, recv_sems):
    x, y, c = lax.axis_index('x'), lax.axis_index('y'), lax.axis_index('c')
    my_idx = 4 * x + 2 * y + c
    copies = []
    for k in range(1, N_DEV):
        px, py, pc = (1 - x if k & 4 else x, 1 - y if k & 2 else y, 1 - c if k & 1 else c)
        for t, kd in enumerate(kinds):
            if mode == 'gather':
                src, dst = src_refs[t], _block_view(land_refs[t], kd, my_idx)
            else:
                src, dst = _block_view(src_refs[t], kd, 4 * px + 2 * py + pc), land_refs[t].at[my_idx]
            copies.append(pltpu.make_async_remote_copy(
                src_ref=src, dst_ref=dst, send_sem=send_sems.at[7 * t + k - 1], recv_sem=recv_sems.at[7 * t + k - 1],
                device_id=(px, py, pc), device_id_type=_MESH))
    return copies


def _own_part(mode, kind, src, land):
    idx = 4 * lax.axis_index('x') + 2 * lax.axis_index('y') + lax.axis_index('c')
    zeros = (0,) * (src.ndim - 1)
    if mode == 'gather':
        part = src
    elif kind == 'stack':
        part = lax.dynamic_index_in_dim(src, idx, 0, keepdims=False)
    else:
        r = src.shape[0] // N_DEV
        part = lax.dynamic_slice_in_dim(src, idx * r, r, 0)
    if mode == 'gather' and kind == 'rows':
        return lax.dynamic_update_slice(land, part, (idx * part.shape[0],) + zeros)
    return lax.dynamic_update_slice(land, part[None], (idx,) + (0,) * part.ndim)


def _send_start(mode, srcs, kinds, name, after=None):
    n = len(srcs)
    after = [] if after is None else [after]
    lands = []
    for a, kd in zip(srcs, kinds):
        if mode == 'gather':
            shape = (N_DEV,) + a.shape if kd == 'stack' else (N_DEV * a.shape[0],) + a.shape[1:]
        else:
            shape = a.shape if kd == 'stack' else (N_DEV, a.shape[0] // N_DEV) + a.shape[1:]
        lands.append(_own_part(mode, kd, a, lax.empty(shape, a.dtype)))

    def body(*refs):
        src_refs, land_refs = refs[:n], refs[n:2 * n]
        send_sems, recv_sems = refs[2 * n + len(after):2 * n + len(after) + 2]
        token = refs[-1]
        for cp in _direct_copies(mode, kinds, src_refs, land_refs, send_sems, recv_sems):
            cp.start()
        token[...] = jnp.zeros_like(token)

    outs = pl.pallas_call(
        body, name=name,
        out_shape=(pltpu.SemaphoreType.DMA((7 * n,)), pltpu.SemaphoreType.DMA((7 * n,)),
                   *[pltpu.HBM(a.shape, a.dtype) for a in srcs + lands], SDS((SUB, 128), F32)),
        in_specs=[_HBM] * (2 * n) + [_ANY] * len(after),
        out_specs=(_SEM, _SEM, *[_HBM] * (2 * n), pl.BlockSpec(memory_space=pltpu.VMEM)),
        input_output_aliases={i: 2 + i for i in range(2 * n)},
        compiler_params=pltpu.CompilerParams(has_side_effects=_EFFECT),
    )(*[pltpu.with_memory_space_constraint(a, pltpu.HBM) for a in srcs + lands], *after)
    return (outs[0], outs[1], list(outs[2:2 + 2 * n])), outs[-1]


def _send_wait(mode, flight, kinds, after, name):
    send, recv, bufs = flight
    n = len(kinds)

    def body(*refs):
        src_refs, land_refs, (send_sems, recv_sems) = refs[:n], refs[n:2 * n], refs[2 * n:2 * n + 2]
        copies = _direct_copies(mode, kinds, src_refs, land_refs, send_sems, recv_sems)
        for cp in copies:
            cp.wait_send()
        for cp in copies:
            cp.wait_recv()

    outs = pl.pallas_call(
        body, name=name, out_shape=[pltpu.HBM(a.shape, a.dtype) for a in bufs],
        in_specs=[_HBM] * (2 * n) + [_SEM, _SEM, _ANY], out_specs=[_HBM] * (2 * n),
        input_output_aliases={i: i for i in range(2 * n)},
        compiler_params=pltpu.CompilerParams(has_side_effects=_EFFECT),
    )(*bufs, send, recv, after)
    return list(outs[n:])


ADAMW_BLOCK_ELEMS = 128 * 1024


def _adamw_sum(wgt, parts, m, v, name):
    nl, r, c = wgt.shape
    assert len(parts) == nl and all(q.shape == (N_DEV, r, c) for q in parts), (name, wgt.shape, [q.shape for q in parts])
    tr = next((t for t in range(min(r, 512), 0, -16) if r % t == 0 and t * c <= ADAMW_BLOCK_ELEMS and t % 16 == 0), r)
    c1 = 1.0 - ADAM_B1 ** ADAM_STEP
    c2 = 1.0 - ADAM_B2 ** ADAM_STEP

    def body(w_ref, m_ref, v_ref, *rest):
        part_refs, (g_ref, d_ref, mo_ref, vo_ref) = rest[:nl], rest[nl:]
        for q in range(nl):
            @pl.when(pl.program_id(0) == q)
            def _(q=q):
                gv = part_refs[q][0].astype(F32)
                for s in range(1, N_DEV):
                    gv = gv + part_refs[q][s].astype(F32)
                mn = ADAM_B1 * m_ref[...] + (1.0 - ADAM_B1) * gv
                vn = ADAM_B2 * v_ref[...] + (1.0 - ADAM_B2) * (gv * gv)
                g_ref[...] = gv
                d_ref[...] = -ADAM_LR * ((mn / c1) / (jnp.sqrt(vn / c2) + ADAM_EPS) + ADAM_WD * w_ref[...])
                mo_ref[...] = mn
                vo_ref[...] = vn

    spec = pl.BlockSpec((None, tr, c), lambda l, i: (l, i, 0))
    part_specs = [pl.BlockSpec((N_DEV, tr, c), lambda l, i, q=q: (0, jnp.where(l == q, i, 0), 0)) for q in range(nl)]
    return pl.pallas_call(body, grid=(nl, r // tr), in_specs=[spec] * 3 + part_specs, out_specs=[spec] * 4,
                          out_shape=[SDS((nl, r, c), F32)] * 4, compiler_params=_params('arbitrary', 'arbitrary'),
                          name=name)(wgt, m, v, *parts)


MIXER_WEIGHTS = {0: ('sc_w_in', 'sc_w_out'), 1: ('attn_w_qkv', 'attn_w_o'), 2: ('lru_w_in', 'lru_w_out')}
STACKED_OPERANDS = ('attn_w_qkv', 'mlp_w_up')
LRU_SMALL = ('lru_conv_w', 'lru_conv_b', 'lru_b_a', 'lru_b_x', 'lru_lambda')


def _layer_items(i):
    w_in, w_out = MIXER_WEIGHTS[i % N_MIXERS]
    j = i // N_MIXERS
    return [('w_in', w_in, j), ('w_out', w_out, j), ('mlp_up', 'mlp_w_up', i), ('mlp_down', 'mlp_w_down', i),
            ('ple_gate', 'ple_w_gate', i), ('ple_proj', 'ple_w_proj', i)]


def _cols_to_full(stacked):
    return jnp.moveaxis(stacked, 0, 1).reshape(stacked.shape[1], -1)


def _full_to_cols(full):
    k, n = full.shape
    return jnp.moveaxis(full.reshape(k, N_DEV, n // N_DEV), 1, 0)


def _pad_to(a, rows):
    return jnp.pad(a, ((0, rows - a.shape[0]), (0, 0)))


def _small_block(src, i):
    kind, j = i % N_MIXERS, i // N_MIXERS
    if kind == 0:
        return _pad_to(src['sc_w_conv'][j], SUB)
    if kind == 2:
        return jnp.concatenate([src[n][j].reshape(-1, src[n].shape[-1]) for n in LRU_SMALL], axis=0)
    return None


def kernel(x, p, positions, norm_mix, norm_mlp, norm_ple, norm_final, sc_w_in, sc_w_conv, sc_w_out, attn_w_qkv, attn_w_o, lru_w_in, lru_conv_w, lru_conv_b, lru_w_a, lru_b_a, lru_w_x, lru_b_x, lru_lambda, lru_w_out, mlp_w_up, mlp_w_down, ple_w_gate, ple_w_proj, loss_target, m_norm_mix, m_norm_mlp, m_norm_ple, m_norm_final, m_sc_w_in, m_sc_w_conv, m_sc_w_out, m_attn_w_qkv, m_attn_w_o, m_lru_w_in, m_lru_conv_w, m_lru_conv_b, m_lru_w_a, m_lru_b_a, m_lru_w_x, m_lru_b_x, m_lru_lambda, m_lru_w_out, m_mlp_w_up, m_mlp_w_down, m_ple_w_gate, m_ple_w_proj, v_norm_mix, v_norm_mlp, v_norm_ple, v_norm_final, v_sc_w_in, v_sc_w_conv, v_sc_w_out, v_attn_w_qkv, v_attn_w_o, v_lru_w_in, v_lru_conv_w, v_lru_conv_b, v_lru_w_a, v_lru_b_a, v_lru_w_x, v_lru_b_x, v_lru_lambda, v_lru_w_out, v_mlp_w_up, v_mlp_w_down, v_ple_w_gate, v_ple_w_proj):
    loc = dict(locals())
    shards = {n: loc[n] for n in WEIGHTS}
    moms = {n: loc['m_' + n] for n in WEIGHTS}
    vels = {n: loc['v_' + n] for n in WEIGHTS}

    depth, t, d = p.shape[0], x.shape[1], x.shape[2]

    def comm_kind(name):
        return 'stack' if SHARD_AXIS[name] == 2 else 'rows'

    def layer_shards(i):
        items = _layer_items(i)
        arrs = [shards[n][idx].astype(BF16) for _, n, idx in items]
        kinds = [comm_kind(n) for _, n, _ in items]
        small = _small_block(shards, i)
        if small is not None:
            arrs.append(small)
            kinds.append('stack')
        return items, arrs, kinds

    def layer_weights(i, items, kinds, outs):
        wl = {key: (_cols_to_full(o) if kd == 'stack' and n not in STACKED_OPERANDS else o)
              for (key, n, _), kd, o in zip(items, kinds, outs)}
        if len(outs) > len(items):
            wl['small'] = _cols_to_full(outs[-1])[:shards['sc_w_conv'].shape[1] if i % N_MIXERS == 0 else SUB]
        return wl

    items0, arrs0, kinds0 = layer_shards(0)
    outs0 = _gather_many(arrs0, kinds0, 'gather_weights_0')
    pending = {}

    def start_gather(i, after):
        if i >= depth:
            return None
        items, arrs, kinds = layer_shards(i)
        flight, token = _send_start('gather', arrs, kinds, f'gather_weights_start_{i}', after=after)
        pending[i] = (items, kinds, flight)
        return token

    first_token = start_gather(1, outs0[0])

    def weights_for_layer(i, h):
        if i == 0:
            return layer_weights(0, items0, kinds0, outs0), first_token
        items, kinds, flight = pending.pop(i)
        outs = _send_wait('gather', flight, kinds, h, f'gather_weights_wait_{i}')
        return layer_weights(i, items, kinds, outs), start_gather(i + 1, outs[0])

    part_keys = {'mlp': ('mlp_up', 'mlp_down', 'ple_gate', 'ple_proj'), 'mixer': ('w_in', 'w_out')}
    exchanges, gate_gathers, total_loss = {}, {}, []

    def gate_block(src, j):
        return jnp.concatenate([src[n][j].reshape(-1, LRU_BLOCK) for n in ('lru_w_a', 'lru_w_x')], axis=0)

    def emit_grads(i, part, gl, loss=None):
        after = None
        if loss is not None:
            total_loss.append(lax.psum(loss[0, 0], ('x', 'y', 'c')))
            after = jnp.full((SUB, 128), total_loss[0], F32)
        if 'gates' in gl:
            blk = gate_block({'lru_w_a': [gl['gates'][0]], 'lru_w_x': [gl['gates'][1]]}, 0)
            gate_gathers[i] = _send_start('gather', [blk], ['stack'], f'gather_gate_grads_start_{i}')[0]
        items = [it for it in _layer_items(i) if it[0] in part_keys[part]]
        kinds = [comm_kind(n) for _, n, _ in items]
        arrs = [_full_to_cols(gl[key]) if kd == 'stack' and gl[key].ndim == 2 else gl[key]
                for (key, _, _), kd in zip(items, kinds)]
        if part == 'mixer' and i % N_MIXERS == 0:
            arrs.append(_full_to_cols(_pad_to(gl['small'], SUB)))
        elif part == 'mixer' and i % N_MIXERS == 2:
            dv = gl['small']
            arrs.append(_full_to_cols(jnp.concatenate([dv[4:8], dv[3:4], dv[0:1], dv[1:2], dv[2:3]], axis=0)))
        kinds += ['stack'] * (len(arrs) - len(kinds))
        flight, token = _send_start('exchange', arrs, kinds, f'exchange_grads_start_{part}_{i}', after=after)
        exchanges[(i, part)] = (items, kinds, flight)
        return token

    rep = {n: shards[n] for n in ('norm_mix', 'norm_mlp', 'norm_ple', 'norm_final')}
    rep['lru_w_a'], rep['lru_w_x'] = shards['lru_w_a'].astype(BF16), shards['lru_w_x'].astype(BF16)
    loss, grad_x, rgrads = _local_step(x.reshape(t, d), p.reshape(depth, t, p.shape[3]), positions.reshape(t, 1),
                                       loss_target.reshape(t, d), rep, weights_for_layer, emit_grads)

    received = {}
    for (i, part), (items, kinds, flight) in exchanges.items():
        outs = _send_wait('exchange', flight, kinds, grad_x, f'exchange_grads_wait_{part}_{i}')
        for (_, n, idx), o in zip(items, outs):
            received[(n, idx)] = o
        if len(outs) > len(items):
            received[('small', i)] = outs[-1]

    res = {}
    for n in WEIGHTS:
        if SHARD_AXIS[n] is not None and shards[n].ndim == 3 and n not in ('sc_w_conv', 'lru_conv_w'):
            res[n] = _adamw_sum(shards[n], [received[(n, l)] for l in range(shards[n].shape[0])], moms[n], vels[n],
                                f'adamw_{n}')
    def small_adamw(layers, name):
        w_, m_, v_ = (jnp.stack([_small_block(src, i) for i in layers]) for src in (shards, moms, vels))
        return _adamw_sum(w_, [received[('small', i)] for i in layers], m_, v_, name)

    sc = small_adamw([i for i in range(depth) if i % N_MIXERS == 0], 'adamw_sc_w_conv')
    res['sc_w_conv'] = tuple(o[:, :shards['sc_w_conv'].shape[1]] for o in sc)
    lru = small_adamw([i for i in range(depth) if i % N_MIXERS == 2], 'adamw_lru_small')
    row = 0
    for n in LRU_SMALL:
        k = shards[n].size // shards[n].shape[0] // shards[n].shape[-1]
        res[n] = tuple(o[:, row:row + k].reshape(shards[n].shape) for o in lru)
        row += k

    gate_layers = sorted(gate_gathers)
    gate_parts = [_send_wait('gather', gate_gathers[i], ['stack'], grad_x, f'gather_gate_grads_wait_{i}')[0]
                  for i in gate_layers]
    gate_w, gate_m, gate_v = (jnp.stack([gate_block(src, j) for j in range(len(gate_layers))])
                              for src in (shards, moms, vels))
    gates = _adamw_sum(gate_w, gate_parts, gate_m, gate_v, 'adamw_lru_gates')
    half = gates[0].shape[1] // 2
    res['lru_w_a'] = tuple(o[:, :half].reshape(shards['lru_w_a'].shape) for o in gates)
    res['lru_w_x'] = tuple(o[:, half:].reshape(shards['lru_w_x'].shape) for o in gates)

    norm_names = ('norm_mix', 'norm_mlp', 'norm_ple', 'norm_final')

    def norm_block(src):
        cat = jnp.concatenate([src[n].reshape(-1, d) for n in norm_names], axis=0)
        return _pad_to(cat, -(-cat.shape[0] // HALO) * HALO)

    rfull = {n: (rgrads[n] if n == 'norm_final' else jnp.stack(rgrads[n], axis=0)) for n in norm_names}
    parts_norm, = _gather_many([norm_block(rfull)], ['stack'], 'gather_norm_grads', after=res['mlp_w_up'][0])
    norms = _adamw_sum(norm_block(shards)[None], [parts_norm], norm_block(moms)[None], norm_block(vels)[None],
                       'adamw_norms')
    row = 0
    for n in norm_names:
        k = shards[n].size // d
        res[n] = tuple(o[0, row:row + k].reshape(shards[n].shape) for o in norms)
        row += k

    return (total_loss[0], grad_x.reshape(x.shape), *[res[n][0] for n in WEIGHTS], *[res[n][1] for n in WEIGHTS],
            *[res[n][2] for n in WEIGHTS], *[res[n][3] for n in WEIGHTS])
```

```python
import functools
import math

import jax
import jax.numpy as jnp
from jax import lax
from jax.experimental import pallas as pl
from jax.experimental.pallas import tpu as pltpu

F32 = jnp.float32
BF16 = jnp.bfloat16
SDS = jax.ShapeDtypeStruct

N_DEV = 8
RMS_EPS = 1e-6
N_MIXERS = 3
HEAD_DIM = 128
DILATED_PATTERNS = ((128, 1), (512, 4), (2048, 16))
ATTN_BLOCK = 128
ROPE_THETA = 500000.0
ROPE_DIM = HEAD_DIM // 4
LRU_BLOCK = 128
LRU_C = 8.0
ADAM_LR, ADAM_B1, ADAM_B2, ADAM_EPS, ADAM_WD, ADAM_STEP = 0.001, 0.9, 0.999, 1e-08, 0.01, 10

HALO = 16
SUB = 8
VMEM_LIMIT = 56 * 1024 * 1024
NEG = -1e30

SHARD_AXIS = {
    'norm_mix': None, 'norm_mlp': None, 'norm_ple': None, 'norm_final': None,
    'sc_w_in': 2, 'sc_w_conv': 2, 'sc_w_out': 1, 'attn_w_qkv': 2, 'attn_w_o': 1,
    'lru_w_in': 2, 'lru_conv_w': 2, 'lru_conv_b': 1, 'lru_w_a': None, 'lru_b_a': 1,
    'lru_w_x': None, 'lru_b_x': 1, 'lru_lambda': 1, 'lru_w_out': 1,
    'mlp_w_up': 2, 'mlp_w_down': 1, 'ple_w_gate': 1, 'ple_w_proj': 2,
}
WEIGHTS = list(SHARD_AXIS)


def _params(*sem):
    return pltpu.CompilerParams(dimension_semantics=sem or None, vmem_limit_bytes=VMEM_LIMIT)


def _row_tile(t, pref=256):
    tr = min(t, pref)
    assert t % tr == 0 and tr % HALO == 0
    return tr


def _row(tr, c, col=0):
    return pl.BlockSpec((tr, c), lambda i, col=col: (i, col))


def _full(shape):
    return pl.BlockSpec(shape, lambda *_: (0,) * len(shape))


def _sigmoid(x):
    return 1.0 / (1.0 + jnp.exp(-x))


def _expm1(x):
    taylor = x * (1.0 + x * (0.5 + x * (1.0 / 6.0 + x * (1.0 / 24.0 + x * (1.0 / 120.0)))))
    return jnp.where(jnp.abs(x) < 0.1, taylor, jnp.exp(x) - 1.0)


def _softplus(x):
    z = jnp.exp(-jnp.abs(x))
    log1p = jnp.where(z < 0.01, z * (1.0 - z * (0.5 - z * (1.0 / 3.0 - z * 0.25))), jnp.log(1.0 + z))
    return jnp.maximum(x, 0.0) + log1p


_GELU_K = math.sqrt(2.0 / math.pi)


def _gelu_and_grad(x):
    inner = _GELU_K * (x + 0.044715 * x * x * x)
    th = jnp.tanh(inner)
    g = 0.5 * x * (1.0 + th)
    dg = 0.5 * (1.0 + th) + 0.5 * x * (1.0 - th * th) * _GELU_K * (1.0 + 3.0 * 0.044715 * x * x)
    return g, dg


def _shift_down(x, k, prev):
    row = lax.broadcasted_iota(jnp.int32, (SUB, x.shape[1]), 0)
    xr = pltpu.roll(x, k, 0)
    top = jnp.where(row < k, pltpu.roll(prev, k, 0), xr[0:SUB])
    return jnp.concatenate([top, xr[SUB:]], axis=0)


def _shift_up(x, k, nxt):
    r = x.shape[0]
    row = lax.broadcasted_iota(jnp.int32, (SUB, x.shape[1]), 0)
    xr = pltpu.roll(x, r - k, 0)
    bot = jnp.where(row >= SUB - k, pltpu.roll(nxt, SUB - k, 0), xr[r - SUB:r])
    return jnp.concatenate([xr[:r - SUB], bot], axis=0)


_DIMS = {'nn': (((1,), (0,)), ((), ())), 'nt': (((1,), (1,)), ((), ())), 'tn': (((0,), (0,)), ((), ()))}


MM_VMEM_BUDGET = 50 * 1024 * 1024
MM_MIN_TK = 1024
MM_MIN_TM = 1024


def _tile_options(dim):
    return [c for c in range(dim, 127, -128) if dim % c == 0] or [dim]


def _choose_tiles(m, n, k, n_span, k_span, a_size, b_size, mn_size, a_temp):
    best = None
    for tm in _tile_options(m):
        for tn in _tile_options(n_span):
            for tk in _tile_options(k_span):
                nk = k // tk
                need = (2 * (tm * tk * a_size + tk * tn * b_size + tm * tn * mn_size) + tm * tn * 4 * (1 + (nk > 1))
                        + tm * tk * 4 * a_temp)
                score = (-min(tk, MM_MIN_TK), -min(tm, MM_MIN_TM), -tm * tn, nk, -min(tm, 2 * MM_MIN_TM), -tn)
                if need <= MM_VMEM_BUDGET and (best is None or score < best[0]):
                    best = (score, (tm, tn, tk))
    return best[1]


def _mm(a, b, dims, name, out_dtypes=(F32,), a_pro=None, extras=(), epi=None, out_stacked=False, dep=None):
    deps = [] if dep is None else [dep]
    stacked = b.ndim == 3
    b_rows, b_cols = (b.shape[1], N_DEV * b.shape[2]) if stacked else b.shape
    if dims == 'nn':
        (m, k), (k2, n) = a.shape, (b_rows, b_cols)
    elif dims == 'nt':
        (m, k), (n, k2) = a.shape, (b_rows, b_cols)
    else:
        (k, m), (k2, n) = a.shape, (b_rows, b_cols)
    assert k == k2, (name, a.shape, b.shape)
    assert not (stacked and dims == 'tn') and not (out_stacked and (extras or dims != 'tn'))
    tm, tn, tk = _choose_tiles(
        m, n, k, n // N_DEV if (out_stacked or (stacked and dims == 'nn')) else n,
        k // N_DEV if (stacked and dims == 'nt') else k, a.dtype.itemsize, b.dtype.itemsize,
        sum(e.dtype.itemsize for e in extras) + sum(jnp.dtype(dt).itemsize for dt in out_dtypes),
        a_pro is not None or a.dtype != BF16)
    assert m % tm == 0 and n % tn == 0 and k % tk == 0, (name, m, n, k)
    nk = k // tk
    a_spec = pl.BlockSpec((tk, tm), lambda i, j, kk: (kk, i)) if dims == 'tn' else pl.BlockSpec((tm, tk), lambda i, j, kk: (i, kk))
    if not stacked:
        b_spec = pl.BlockSpec((tn, tk), lambda i, j, kk: (j, kk)) if dims == 'nt' else pl.BlockSpec((tk, tn), lambda i, j, kk: (kk, j))
    elif dims == 'nn':
        per = b.shape[2] // tn
        b_spec = pl.BlockSpec((None, tk, tn), lambda i, j, kk: (j // per, kk, j % per))
    else:
        per = b.shape[2] // tk
        b_spec = pl.BlockSpec((None, tn, tk), lambda i, j, kk: (kk // per, j, kk % per))
    if out_stacked:
        per_o = n // N_DEV // tn
        o_spec = pl.BlockSpec((None, tm, tn), lambda i, j, kk: (j // per_o, i, j % per_o))
        o_shape = (N_DEV, m, n // N_DEV)
    else:
        o_spec = pl.BlockSpec((tm, tn), lambda i, j, kk: (i, j))
        o_shape = (m, n)
    n_ex, n_out = len(extras), len(out_dtypes)
    for e in extras:
        assert e.shape == (m, n), (name, e.shape)

    def body(a_ref, b_ref, *rest):
        rest = rest[len(deps):]
        ex_refs, out_refs = rest[:n_ex], rest[n_ex:n_ex + n_out]
        kk = pl.program_id(2)
        av = a_ref[...]
        if a_pro is not None:
            av = a_pro(av.astype(F32))
        part = lax.dot_general(av.astype(BF16), b_ref[...].astype(BF16), _DIMS[dims], preferred_element_type=F32)

        def finish(res):
            outs = (res,) if epi is None else epi(res, *[e[...] for e in ex_refs])
            for o_ref, o in zip(out_refs, outs):
                o_ref[...] = o.astype(o_ref.dtype)

        if nk == 1:
            finish(part)
        else:
            acc = rest[-1]

            @pl.when(kk == 0)
            def _():
                acc[...] = part

            @pl.when(kk > 0)
            def _():
                acc[...] += part

            @pl.when(kk == nk - 1)
            def _():
                finish(acc[...])

    out = pl.pallas_call(
        body, grid=(m // tm, n // tn, nk),
        in_specs=[a_spec, b_spec] + [_ANY] * len(deps) + [o_spec] * n_ex,
        out_specs=[o_spec] * n_out,
        out_shape=[SDS(o_shape, d) for d in out_dtypes],
        scratch_shapes=[] if nk == 1 else [pltpu.VMEM((tm, tn), F32)],
        compiler_params=_params('parallel', 'parallel', 'arbitrary'), name=name)(a, b, *deps, *extras)
    return out[0] if n_out == 1 else out


def _relu2(u):
    r = jnp.maximum(u, 0.0)
    return r * r


def _rms_fwd(h, g, name):
    t, d = h.shape
    tr = _row_tile(t)

    def body(h_ref, g_ref, o_ref):
        x = h_ref[...]
        r = lax.rsqrt(jnp.mean(x * x, axis=-1, keepdims=True) + RMS_EPS)
        o_ref[...] = (x * r * g_ref[...]).astype(o_ref.dtype)

    return pl.pallas_call(body, grid=(t // tr,), in_specs=[_row(tr, d), _full((1, d))], out_specs=_row(tr, d),
                          out_shape=SDS((t, d), BF16), compiler_params=_params('parallel'), name=name)(h, g.reshape(1, d))


def _rms_bwd(h, g, dhn, dres, name):
    t, d = h.shape
    tr = _row_tile(t)

    def body(h_ref, g_ref, dhn_ref, dres_ref, dh_ref, dg_ref):
        @pl.when(pl.program_id(0) == 0)
        def _():
            dg_ref[...] = jnp.zeros_like(dg_ref)

        x = h_ref[...]
        r = lax.rsqrt(jnp.mean(x * x, axis=-1, keepdims=True) + RMS_EPS)
        dy = dhn_ref[...].astype(F32)
        gy = dy * g_ref[...]
        dx = r * gy - x * (r * r * r) * jnp.mean(gy * x, axis=-1, keepdims=True)
        dh_ref[...] = dres_ref[...] + dx
        dg_ref[...] += jnp.sum(dy * (x * r), axis=0, keepdims=True)

    return pl.pallas_call(body, grid=(t // tr,),
                          in_specs=[_row(tr, d), _full((1, d)), _row(tr, d), _row(tr, d)],
                          out_specs=[_row(tr, d), _full((1, d))],
                          out_shape=[SDS((t, d), F32), SDS((1, d), F32)],
                          compiler_params=_params('arbitrary'), name=name)(h, g.reshape(1, d), dhn, dres)


def _head(h, g, target, name):
    t, d = h.shape
    tr = _row_tile(t)

    def body(h_ref, g_ref, t_ref, dh_ref, loss_ref, dg_ref):
        @pl.when(pl.program_id(0) == 0)
        def _():
            dg_ref[...] = jnp.zeros_like(dg_ref)
            loss_ref[...] = jnp.zeros_like(loss_ref)

        x = h_ref[...]
        gv = g_ref[...]
        r = lax.rsqrt(jnp.mean(x * x, axis=-1, keepdims=True) + RMS_EPS)
        xh = x * r
        e = xh * gv - t_ref[...]
        per_tok = jnp.mean(e * e, axis=-1, keepdims=True)
        loss_ref[...] += jnp.broadcast_to(0.5 * jnp.sum(per_tok, axis=0, keepdims=True), loss_ref.shape)
        dy = e * (1.0 / d)
        gy = dy * gv
        dh_ref[...] = r * gy - x * (r * r * r) * jnp.mean(gy * x, axis=-1, keepdims=True)
        dg_ref[...] += jnp.sum(dy * xh, axis=0, keepdims=True)

    return pl.pallas_call(body, grid=(t // tr,),
                          in_specs=[_row(tr, d), _full((1, d)), _row(tr, d)],
                          out_specs=[_row(tr, d), _full((1, 128)), _full((1, d))],
                          out_shape=[SDS((t, d), F32), SDS((1, 128), F32), SDS((1, d), F32)],
                          compiler_params=_params('arbitrary'), name=name)(h, g.reshape(1, d), target)


def _ple_bwd_gate(dh3, gate, pp, name):
    t, d = dh3.shape
    tr = _row_tile(t)

    def body(dh_ref, g_ref, pp_ref, dpp_ref, dgl_ref):
        dh = dh_ref[...]
        gt = g_ref[...]
        dpp_ref[...] = (dh * gt).astype(dpp_ref.dtype)
        dgl_ref[...] = (dh * pp_ref[...] * gt * (1.0 - gt)).astype(dgl_ref.dtype)

    return pl.pallas_call(body, grid=(t // tr,), in_specs=[_row(tr, d)] * 3, out_specs=[_row(tr, d)] * 2,
                          out_shape=[SDS((t, d), BF16), SDS((t, d), BF16)],
                          compiler_params=_params('parallel'), name=name)(dh3, gate, pp)


def _halo_prev(tr, c, col=0):
    return pl.BlockSpec((HALO, c), lambda i, col=col: (jnp.maximum(i * (tr // HALO) - 1, 0), col))


def _halo_next(tr, c, t, col=0):
    return pl.BlockSpec((HALO, c), lambda i, col=col: (jnp.minimum((i + 1) * (tr // HALO), t // HALO - 1), col))


def _sc_fwd(z, w, name):
    t, c3 = z.shape
    c = c3 // 3
    tr = _row_tile(t)

    def body(z_ref, zp_ref, w_ref, y_ref):
        i = pl.program_id(0)
        zz = z_ref[...]
        gb, cx = zz[:, :c], zz[:, c:2 * c] * zz[:, 2 * c:]
        zp = zp_ref[SUB:HALO, :]
        cxp = jnp.where(i > 0, zp[:, c:2 * c] * zp[:, 2 * c:], 0.0)
        wv = w_ref[...]
        conv = wv[2:3] * cx + wv[1:2] * _shift_down(cx, 1, cxp) + wv[0:1] * _shift_down(cx, 2, cxp)
        y_ref[...] = (gb * conv).astype(y_ref.dtype)

    return pl.pallas_call(body, grid=(t // tr,),
                          in_specs=[_row(tr, c3), _halo_prev(tr, c3), _full((3, c))],
                          out_specs=_row(tr, c), out_shape=SDS((t, c), BF16),
                          compiler_params=_params('parallel'), name=name)(z, z, w)


def _sc_bwd(dy, z, w, name):
    t, c3 = z.shape
    c = c3 // 3
    tr = _row_tile(t)
    nt = t // tr

    def body(dy_ref, dyn_ref, z_ref, zp_ref, zn_ref, w_ref, dz_ref, dw_ref):
        i = pl.program_id(0)

        @pl.when(i == 0)
        def _():
            dw_ref[...] = jnp.zeros_like(dw_ref)

        zz = z_ref[...]
        gb, gc, xi = zz[:, :c], zz[:, c:2 * c], zz[:, 2 * c:]
        cx = gc * xi
        zp = zp_ref[SUB:HALO, :]
        cxp = jnp.where(i > 0, zp[:, c:2 * c] * zp[:, 2 * c:], 0.0)
        wv = w_ref[...]
        cx1, cx2 = _shift_down(cx, 1, cxp), _shift_down(cx, 2, cxp)
        conv = wv[2:3] * cx + wv[1:2] * cx1 + wv[0:1] * cx2
        dyv = dy_ref[...]
        dconv = dyv * gb
        dcn = jnp.where(i < nt - 1, dyn_ref[0:SUB, :] * zn_ref[0:SUB, :c], 0.0)
        dcx = wv[2:3] * dconv + wv[1:2] * _shift_up(dconv, 1, dcn) + wv[0:1] * _shift_up(dconv, 2, dcn)
        dz_ref[:, :c] = (dyv * conv).astype(dz_ref.dtype)
        dz_ref[:, c:2 * c] = (dcx * xi).astype(dz_ref.dtype)
        dz_ref[:, 2 * c:] = (dcx * gc).astype(dz_ref.dtype)
        dw_ref[...] += jnp.concatenate([jnp.sum(dconv * cx2, axis=0, keepdims=True),
                                        jnp.sum(dconv * cx1, axis=0, keepdims=True),
                                        jnp.sum(dconv * cx, axis=0, keepdims=True)], axis=0)

    return pl.pallas_call(body, grid=(nt,),
                          in_specs=[_row(tr, c), _halo_next(tr, c, t), _row(tr, c3), _halo_prev(tr, c3),
                                    _halo_next(tr, c3, t), _full((3, c))],
                          out_specs=[_row(tr, c3), _full((3, c))],
                          out_shape=[SDS((t, c3), BF16), SDS((3, c), F32)],
                          compiler_params=_params('arbitrary'), name=name)(dy, dy, z, z, z, w)


def _perm(tr, dil, inverse=False):
    n = tr // dil
    a = lax.broadcasted_iota(jnp.int32, (tr, tr), 1 if inverse else 0)
    b = lax.broadcasted_iota(jnp.int32, (tr, tr), 0 if inverse else 1)
    return (b == (a % n) * dil + a // n).astype(BF16)


def _permute(pm, x, terms):
    if x.dtype == BF16:
        return jnp.dot(pm, x, preferred_element_type=F32)
    acc = None
    for _ in range(terms):
        part = x.astype(BF16)
        y = jnp.dot(pm, part, preferred_element_type=F32)
        acc = y if acc is None else acc + y
        x = x - part.astype(F32)
    return acc


def _store_dilated(o_ref, y, dil, d):
    n = y.shape[0] // dil
    for rho in range(dil):
        o_ref[:, rho * d:(rho + 1) * d] = y[rho * n:(rho + 1) * n].astype(o_ref.dtype)


def _load_dilated(ref, dil, d):
    return jnp.concatenate([ref[:, rho * d:(rho + 1) * d] for rho in range(dil)], axis=0) if dil > 1 else ref[...]


def _rope_heads(x, lane, cos, sin):
    return jnp.concatenate([_rope_apply(x[:, s:s + HEAD_DIM], lane, cos, sin)
                            for s in range(0, x.shape[1], HEAD_DIM)], axis=1)


def _rope_tables(pos, invf, sign):
    lane = lax.broadcasted_iota(jnp.int32, (pos.shape[0], HEAD_DIM), 1)
    ang = pos.astype(F32) * invf
    half = ROPE_DIM // 2
    cos = jnp.where(lane < ROPE_DIM, jnp.cos(ang), 1.0)
    sin = jnp.sin(ang) * sign
    sin = jnp.where(lane < half, -sin, jnp.where(lane < ROPE_DIM, sin, 0.0))
    return lane, cos, sin


def _rope_apply(x, lane, cos, sin):
    half = ROPE_DIM // 2
    xs = jnp.where(lane < half, pltpu.roll(x, HEAD_DIM - half, 1), pltpu.roll(x, half, 1))
    return x * cos + xs * sin


def _dilated_spec(tr, dil, d):
    return pl.BlockSpec((tr // dil, dil * d), lambda i: (i, 0))


def _rope_fwd(qkv, pos, invf, dils, name):
    t, w3 = qkv.shape
    w, ng = w3 // 3, len(dils)
    d = w // ng
    tr = _row_tile(t)

    def body(q_ref, k_ref, v_ref, pos_ref, invf_ref, *out_refs):
        lane, cos, sin = _rope_tables(pos_ref[...], invf_ref[...], 1.0)
        for g, dil in enumerate(dils):
            cs = slice(g * d, (g + 1) * d)
            vals = [_rope_heads(q_ref[:, cs], lane, cos, sin).astype(BF16),
                    _rope_heads(k_ref[:, cs], lane, cos, sin).astype(BF16), v_ref[:, cs].astype(BF16)]
            if dil > 1:
                pm = _perm(tr, dil)
                vals = [_permute(pm, a, 1) for a in vals]
            for o_ref, a in zip(out_refs[g::ng], vals):
                _store_dilated(o_ref, a, dil, d)

    outs = pl.pallas_call(body, grid=(t // tr,),
                          in_specs=[_row(tr, w, 0), _row(tr, w, 1), _row(tr, w, 2), _row(tr, 1), _full((1, HEAD_DIM))],
                          out_specs=[_dilated_spec(tr, dil, d) for dil in dils] * 3,
                          out_shape=[SDS((t // dil, dil * d), BF16) for dil in dils] * 3,
                          compiler_params=_params('parallel'), name=name)(qkv, qkv, qkv, pos, invf)
    return outs[:ng], outs[ng:2 * ng], outs[2 * ng:]


def _rope_bwd(dqs, dks, dvs, pos, invf, dils, name):
    ng = len(dils)
    t = dqs[0].shape[0] * dils[0]
    d = dqs[0].shape[1] // dils[0]
    w = ng * d
    tr = _row_tile(t)

    def body(*refs):
        dq_refs, dk_refs, dv_refs = refs[:ng], refs[ng:2 * ng], refs[2 * ng:3 * ng]
        pos_ref, invf_ref, o_ref = refs[3 * ng:]
        pos_f = jnp.broadcast_to(pos_ref[...].astype(F32), (tr, HEAD_DIM))
        for g, dil in enumerate(dils):
            pos_g = pos_f if dil == 1 else _permute(_perm(tr, dil), pos_f, 3)
            lane, cos, sin = _rope_tables(pos_g, invf_ref[...], -1.0)
            vals = [_rope_heads(_load_dilated(dq_refs[g], dil, d), lane, cos, sin),
                    _rope_heads(_load_dilated(dk_refs[g], dil, d), lane, cos, sin), _load_dilated(dv_refs[g], dil, d)]
            back = _perm(tr, dil, inverse=True) if dil > 1 else None
            for sec, a in enumerate(vals):
                a = a.astype(BF16)
                if dil > 1:
                    a = _permute(back, a, 1)
                o_ref[:, sec * w + g * d:sec * w + (g + 1) * d] = a.astype(o_ref.dtype)

    return pl.pallas_call(body, grid=(t // tr,),
                          in_specs=[_dilated_spec(tr, dil, d) for dil in dils] * 3 + [_row(tr, 1), _full((1, HEAD_DIM))],
                          out_specs=_row(tr, 3 * w), out_shape=SDS((t, 3 * w), BF16),
                          compiler_params=_params('parallel'), name=name)(*dqs, *dks, *dvs, pos, invf)


def _dilate_many(arrs, dil, terms, out_dtypes, name):
    t, d = arrs[0].shape
    tr = _row_tile(t)
    na = len(arrs)

    def body(*refs):
        pm = _perm(tr, dil)
        for a_ref, o_ref, k in zip(refs[:na], refs[na:], terms):
            _store_dilated(o_ref, _permute(pm, a_ref[...], k), dil, d)

    return pl.pallas_call(body, grid=(t // tr,), in_specs=[_row(tr, d)] * na,
                          out_specs=[_dilated_spec(tr, dil, d)] * na,
                          out_shape=[SDS((t // dil, dil * d), dt) for dt in out_dtypes],
                          compiler_params=_params('parallel'), name=name)(*arrs)


def _attn_masks():
    qi = lax.broadcasted_iota(jnp.int32, (ATTN_BLOCK, ATTN_BLOCK), 0)
    kj = lax.broadcasted_iota(jnp.int32, (ATTN_BLOCK, ATTN_BLOCK), 1)
    return kj >= qi, kj <= qi


def _attn_cols(l, width):
    ncol = width // HEAD_DIM
    cpb = max(1, min(ncol, 32 // (l // ATTN_BLOCK)))
    assert ncol % cpb == 0
    return cpb


def _attn_fwd(q, k, v, name):
    l, width = q.shape
    cpb = _attn_cols(l, width)
    nb = l // ATTN_BLOCK
    scale = HEAD_DIM ** -0.5

    def body(q_ref, k_ref, v_ref, o_ref, lse_ref):
        m_prev, m_cur = _attn_masks()
        for col in range(cpb):
            cs = slice(col * HEAD_DIM, (col + 1) * HEAD_DIM)

            def step(b, carry, cs=cs):
                r0 = pl.multiple_of(b * ATTN_BLOCK, ATTN_BLOCK)
                rp = pl.multiple_of(jnp.maximum(b - 1, 0) * ATTN_BLOCK, ATTN_BLOCK)
                qb = q_ref[pl.ds(r0, ATTN_BLOCK), cs]
                s_p = lax.dot_general(qb, k_ref[pl.ds(rp, ATTN_BLOCK), cs], _DIMS['nt'], preferred_element_type=F32) * scale
                s_c = lax.dot_general(qb, k_ref[pl.ds(r0, ATTN_BLOCK), cs], _DIMS['nt'], preferred_element_type=F32) * scale
                s_p = jnp.where(jnp.logical_and(m_prev, b > 0), s_p, NEG)
                s_c = jnp.where(m_cur, s_c, NEG)
                m = jnp.maximum(jnp.max(s_p, axis=-1, keepdims=True), jnp.max(s_c, axis=-1, keepdims=True))
                p_p, p_c = jnp.exp(s_p - m), jnp.exp(s_c - m)
                den = jnp.sum(p_p, axis=-1, keepdims=True) + jnp.sum(p_c, axis=-1, keepdims=True)
                acc = jnp.dot(p_p.astype(BF16), v_ref[pl.ds(rp, ATTN_BLOCK), cs], preferred_element_type=F32)
                acc += jnp.dot(p_c.astype(BF16), v_ref[pl.ds(r0, ATTN_BLOCK), cs], preferred_element_type=F32)
                o_ref[pl.ds(r0, ATTN_BLOCK), cs] = acc / den
                lse_ref[pl.ds(r0, ATTN_BLOCK), cs] = jnp.broadcast_to(m + jnp.log(den), (ATTN_BLOCK, HEAD_DIM))
                return carry

            lax.fori_loop(0, nb, step, 0, unroll=min(nb, 4))

    spec = pl.BlockSpec((l, cpb * HEAD_DIM), lambda j: (0, j))
    return pl.pallas_call(body, grid=(width // (cpb * HEAD_DIM),), in_specs=[spec] * 3, out_specs=[spec] * 2,
                          out_shape=[SDS((l, width), F32)] * 2,
                          compiler_params=_params('parallel'), name=name)(q, k, v)


def _attn_bwd(q, k, v, do, lse, delta, name):
    l, width = q.shape
    cpb = _attn_cols(l, width)
    nb = l // ATTN_BLOCK
    scale = HEAD_DIM ** -0.5

    def body(q_ref, k_ref, v_ref, do_ref, lse_ref, dl_ref, dq_ref, dk_ref, dv_ref):
        m_prev, m_cur = _attn_masks()
        dk_ref[...] = jnp.zeros_like(dk_ref)
        dv_ref[...] = jnp.zeros_like(dv_ref)
        for col in range(cpb):
            cs = slice(col * HEAD_DIM, (col + 1) * HEAD_DIM)

            def step(b, carry, cs=cs):
                r0 = pl.multiple_of(b * ATTN_BLOCK, ATTN_BLOCK)
                rp = pl.multiple_of(jnp.maximum(b - 1, 0) * ATTN_BLOCK, ATTN_BLOCK)
                qb, dob = q_ref[pl.ds(r0, ATTN_BLOCK), cs], do_ref[pl.ds(r0, ATTN_BLOCK), cs].astype(BF16)
                kp, kc = k_ref[pl.ds(rp, ATTN_BLOCK), cs], k_ref[pl.ds(r0, ATTN_BLOCK), cs]
                vp, vc = v_ref[pl.ds(rp, ATTN_BLOCK), cs], v_ref[pl.ds(r0, ATTN_BLOCK), cs]
                lse_b = lse_ref[pl.ds(r0, ATTN_BLOCK), cs]
                dl_b = dl_ref[pl.ds(r0, ATTN_BLOCK), cs]
                s_p = lax.dot_general(qb, kp, _DIMS['nt'], preferred_element_type=F32) * scale
                s_c = lax.dot_general(qb, kc, _DIMS['nt'], preferred_element_type=F32) * scale
                p_p = jnp.exp(jnp.where(jnp.logical_and(m_prev, b > 0), s_p, NEG) - lse_b)
                p_c = jnp.exp(jnp.where(m_cur, s_c, NEG) - lse_b)
                dp_p = lax.dot_general(dob, vp, _DIMS['nt'], preferred_element_type=F32)
                dp_c = lax.dot_general(dob, vc, _DIMS['nt'], preferred_element_type=F32)
                ds_p = (p_p * (dp_p - dl_b) * scale).astype(BF16)
                ds_c = (p_c * (dp_c - dl_b) * scale).astype(BF16)
                dq_ref[pl.ds(r0, ATTN_BLOCK), cs] = (jnp.dot(ds_p, kp, preferred_element_type=F32)
                                                     + jnp.dot(ds_c, kc, preferred_element_type=F32))
                dk_ref[pl.ds(rp, ATTN_BLOCK), cs] += lax.dot_general(ds_p, qb, _DIMS['tn'], preferred_element_type=F32)
                dk_ref[pl.ds(r0, ATTN_BLOCK), cs] += lax.dot_general(ds_c, qb, _DIMS['tn'], preferred_element_type=F32)
                dv_ref[pl.ds(rp, ATTN_BLOCK), cs] += lax.dot_general(p_p.astype(BF16), dob, _DIMS['tn'], preferred_element_type=F32)
                dv_ref[pl.ds(r0, ATTN_BLOCK), cs] += lax.dot_general(p_c.astype(BF16), dob, _DIMS['tn'], preferred_element_type=F32)
                return carry

            lax.fori_loop(0, nb, step, 0, unroll=min(nb, 2))

    spec = pl.BlockSpec((l, cpb * HEAD_DIM), lambda j: (0, j))
    return pl.pallas_call(body, grid=(width // (cpb * HEAD_DIM),), in_specs=[spec] * 6, out_specs=[spec] * 3,
                          out_shape=[SDS((l, width), F32)] * 3,
                          compiler_params=_params('parallel'), name=name)(q, k, v, do, lse, delta)


def _attn_combine(os_, lses, dils, name):
    ng = len(dils)
    t = os_[0].shape[0] * dils[0]
    d = os_[0].shape[1] // dils[0]
    tr = _row_tile(t)

    def body(*refs):
        o_refs, l_refs, o_out, lse_out = refs[:ng], refs[ng:2 * ng], refs[2 * ng], refs[2 * ng + 1]
        ovs, ls = [], []
        for g, dil in enumerate(dils):
            ov, lv = _load_dilated(o_refs[g], dil, d), _load_dilated(l_refs[g], dil, d)
            if dil > 1:
                back = _perm(tr, dil, inverse=True)
                ov, lv = _permute(back, ov, 2), _permute(back, lv, 3)
            ovs.append(ov)
            ls.append(lv)
        m = functools.reduce(jnp.maximum, ls)
        ws = [jnp.exp(x - m) for x in ls]
        den = functools.reduce(lambda a, b: a + b, ws)
        acc = functools.reduce(lambda a, b: a + b, [w * o for w, o in zip(ws, ovs)])
        o_out[...] = (acc / den).astype(o_out.dtype)
        lse_out[...] = m + jnp.log(den)

    return pl.pallas_call(body, grid=(t // tr,), in_specs=[_dilated_spec(tr, dil, d) for dil in dils] * 2,
                          out_specs=[_row(tr, d)] * 2, out_shape=[SDS((t, d), BF16), SDS((t, d), F32)],
                          compiler_params=_params('parallel'), name=name)(*os_, *lses)


def _delta_epilogue(acc, o):
    prod = acc * o.astype(F32)
    segs = [jnp.broadcast_to(jnp.sum(prod[:, s:s + HEAD_DIM], axis=-1, keepdims=True), (acc.shape[0], HEAD_DIM))
            for s in range(0, acc.shape[1], HEAD_DIM)]
    return acc, jnp.concatenate(segs, axis=-1)


LRU_TILE = 128


def _lru_gates(xr, wa_ref, ba, wx_ref, bx, lam):
    nb = wa_ref.shape[0]
    xb = xr.astype(BF16)
    ra = jnp.concatenate([jnp.dot(xb[:, n * LRU_BLOCK:(n + 1) * LRU_BLOCK], wa_ref[n], preferred_element_type=F32)
                          for n in range(nb)], axis=-1) + ba
    ia = jnp.concatenate([jnp.dot(xb[:, n * LRU_BLOCK:(n + 1) * LRU_BLOCK], wx_ref[n], preferred_element_type=F32)
                          for n in range(nb)], axis=-1) + bx
    r, ig = _sigmoid(ra), _sigmoid(ia)
    sp = _softplus(-lam)
    log_a = -LRU_C * r * sp
    a = jnp.exp(log_a)
    mult = jnp.sqrt(-_expm1(2.0 * log_a))
    return xb, r, ig, sp, a, mult


def _lru_fwd(z, cw, cb, wa, ba, wx, bx, lam, name):
    t, c2 = z.shape
    c = c2 // 2
    nb = c // LRU_BLOCK
    tr = _row_tile(t, LRU_TILE)

    def body(g_ref, x_ref, xp_ref, cw_ref, cb_ref, wa_ref, ba_ref, wx_ref, bx_ref, lam_ref,
             y_ref, hs_ref, xr_ref, car_ref):
        i = pl.program_id(0)

        @pl.when(i == 0)
        def _():
            car_ref[...] = jnp.zeros_like(car_ref)

        x0 = x_ref[...]
        xp = jnp.where(i > 0, xp_ref[SUB:HALO, :], 0.0)
        cwv = cw_ref[...]
        xr = (cb_ref[...] + cwv[3:4] * x0 + cwv[2:3] * _shift_down(x0, 1, xp)
              + cwv[1:2] * _shift_down(x0, 2, xp) + cwv[0:1] * _shift_down(x0, 3, xp))
        xr_ref[...] = xr
        _, _, ig, _, a, mult = _lru_gates(xr, wa_ref, ba_ref[...], wx_ref, bx_ref[...], lam_ref[...])
        u = mult * (ig * xr)
        row = lax.broadcasted_iota(jnp.int32, (SUB, c), 0)
        car = car_ref[...]
        for j in range(tr // SUB):
            ab, ub = a[j * SUB:(j + 1) * SUB], u[j * SUB:(j + 1) * SUB]
            for s in (1, 2, 4):
                a_sh = jnp.where(row >= s, pltpu.roll(ab, s, 0), 1.0)
                u_sh = jnp.where(row >= s, pltpu.roll(ub, s, 0), 0.0)
                ub = ab * u_sh + ub
                ab = ab * a_sh
            hb = ub + ab * car
            hs_ref[j * SUB:(j + 1) * SUB, :] = hb
            car = jnp.broadcast_to(hb[SUB - 1:SUB], (SUB, c))
        car_ref[...] = car
        gl, _ = _gelu_and_grad(g_ref[...])
        y_ref[...] = (hs_ref[...] * gl).astype(y_ref.dtype)

    return pl.pallas_call(
        body, grid=(t // tr,),
        in_specs=[_row(tr, c, 0), _row(tr, c, 1), _halo_prev(tr, c, 1), _full((4, c)), _full((1, c)),
                  _full((nb, LRU_BLOCK, LRU_BLOCK)), _full((1, c)), _full((nb, LRU_BLOCK, LRU_BLOCK)), _full((1, c)), _full((1, c))],
        out_specs=[_row(tr, c)] * 3,
        out_shape=[SDS((t, c), BF16), SDS((t, c), F32), SDS((t, c), F32)],
        scratch_shapes=[pltpu.VMEM((SUB, c), F32)],
        compiler_params=_params('arbitrary'), name=name)(
            z, z, z, cw, cb.reshape(1, c), wa, ba.reshape(1, c), wx, bx.reshape(1, c), lam.reshape(1, c))


def _lru_bwd(dy, z, xr, hs, cw, wa, ba, wx, bx, lam, name):
    t, c2 = z.shape
    c = c2 // 2
    nb = c // LRU_BLOCK
    tr = _row_tile(t, LRU_TILE)
    nt = t // tr

    def rev(col=0):
        return pl.BlockSpec((tr, c), lambda i, col=col: (nt - 1 - i, col))

    def rev_prev(col=0):
        return pl.BlockSpec((HALO, c), lambda i, col=col: (jnp.maximum((nt - 1 - i) * (tr // HALO) - 1, 0), col))

    def body(dy_ref, g_ref, x_ref, xp_ref, xr_ref, hs_ref, hp_ref, cw_ref, wa_ref, ba_ref, wx_ref, bx_ref, lam_ref,
             dz_ref, dwa_ref, dwx_ref, dvec_ref, lcar_ref, ahead_ref, dxhead_ref, lam_s):
        i = pl.program_id(0)
        first_tile = i == nt - 1

        @pl.when(i == 0)
        def _():
            lcar_ref[...] = jnp.zeros_like(lcar_ref)
            ahead_ref[...] = jnp.zeros_like(ahead_ref)
            dxhead_ref[...] = jnp.zeros_like(dxhead_ref)
            dwa_ref[...] = jnp.zeros_like(dwa_ref)
            dwx_ref[...] = jnp.zeros_like(dwx_ref)
            dvec_ref[...] = jnp.zeros_like(dvec_ref)

        xrv = xr_ref[...]
        lamv = lam_ref[...]
        xb, r, ig, sp, a, mult = _lru_gates(xrv, wa_ref, ba_ref[...], wx_ref, bx_ref[...], lamv)
        hsv = hs_ref[...]
        dyv = dy_ref[...]
        gl, dgl = _gelu_and_grad(g_ref[...])
        dhs = dyv * gl
        dz_ref[:, :c] = (dyv * hsv * dgl).astype(dz_ref.dtype)

        a_next = _shift_up(a, 1, ahead_ref[...])
        row = lax.broadcasted_iota(jnp.int32, (SUB, c), 0)
        car = lcar_ref[...]
        for j in reversed(range(tr // SUB)):
            ab, ub = a_next[j * SUB:(j + 1) * SUB], dhs[j * SUB:(j + 1) * SUB]
            for s in (1, 2, 4):
                a_sh = jnp.where(row < SUB - s, pltpu.roll(ab, SUB - s, 0), 1.0)
                u_sh = jnp.where(row < SUB - s, pltpu.roll(ub, SUB - s, 0), 0.0)
                ub = ab * u_sh + ub
                ab = ab * a_sh
            lb = ub + ab * car
            lam_s[j * SUB:(j + 1) * SUB, :] = lb
            car = jnp.broadcast_to(lb[0:1], (SUB, c))
        lcar_ref[...] = car
        ahead_ref[...] = a[0:SUB]
        lmb = lam_s[...]

        hp = jnp.where(first_tile, 0.0, hp_ref[SUB:HALO, :])
        h_prev = _shift_down(hsv, 1, hp)
        d_a = lmb * h_prev
        d_mult = lmb * (ig * xrv)
        d_ixr = lmb * mult
        d_ig = d_ixr * xrv
        dxr = d_ixr * ig
        d_la = d_a * a - d_mult * (a * a) / mult
        d_r = d_la * (-LRU_C * sp)
        d_sp = jnp.sum(d_la * (-LRU_C * r), axis=0, keepdims=True)
        d_ra = d_r * r * (1.0 - r)
        d_ia = d_ig * ig * (1.0 - ig)
        d_rab, d_iab = d_ra.astype(BF16), d_ia.astype(BF16)
        parts = []
        for n in range(nb):
            cs = slice(n * LRU_BLOCK, (n + 1) * LRU_BLOCK)
            parts.append(lax.dot_general(d_rab[:, cs], wa_ref[n], _DIMS['nt'], preferred_element_type=F32)
                         + lax.dot_general(d_iab[:, cs], wx_ref[n], _DIMS['nt'], preferred_element_type=F32))
            dwa_ref[n] += lax.dot_general(xb[:, cs], d_rab[:, cs], _DIMS['tn'], preferred_element_type=F32)
            dwx_ref[n] += lax.dot_general(xb[:, cs], d_iab[:, cs], _DIMS['tn'], preferred_element_type=F32)
        dxr = dxr + jnp.concatenate(parts, axis=-1)

        cwv = cw_ref[...]
        nxt = dxhead_ref[...]
        dx0 = (cwv[3:4] * dxr + cwv[2:3] * _shift_up(dxr, 1, nxt) + cwv[1:2] * _shift_up(dxr, 2, nxt)
               + cwv[0:1] * _shift_up(dxr, 3, nxt))
        dxhead_ref[...] = dxr[0:SUB]
        dz_ref[:, c:] = dx0.astype(dz_ref.dtype)

        x0 = x_ref[...]
        xp = jnp.where(first_tile, 0.0, xp_ref[SUB:HALO, :])
        sums = [jnp.sum(d_ra, axis=0, keepdims=True), jnp.sum(d_ia, axis=0, keepdims=True),
                d_sp * (-_sigmoid(-lamv)), jnp.sum(dxr, axis=0, keepdims=True),
                jnp.sum(dxr * _shift_down(x0, 3, xp), axis=0, keepdims=True),
                jnp.sum(dxr * _shift_down(x0, 2, xp), axis=0, keepdims=True),
                jnp.sum(dxr * _shift_down(x0, 1, xp), axis=0, keepdims=True),
                jnp.sum(dxr * x0, axis=0, keepdims=True)]
        dvec_ref[...] += jnp.concatenate(sums, axis=0)

    wspec = _full((nb, LRU_BLOCK, LRU_BLOCK))
    return pl.pallas_call(
        body, grid=(nt,),
        in_specs=[rev(), rev(0), rev(1), rev_prev(1), rev(), rev(), rev_prev(), _full((4, c)),
                  wspec, _full((1, c)), wspec, _full((1, c)), _full((1, c))],
        out_specs=[pl.BlockSpec((tr, c2), lambda i: (nt - 1 - i, 0)), wspec, wspec, _full((SUB, c))],
        out_shape=[SDS((t, c2), BF16), SDS((nb, LRU_BLOCK, LRU_BLOCK), F32), SDS((nb, LRU_BLOCK, LRU_BLOCK), F32),
                   SDS((SUB, c), F32)],
        scratch_shapes=[pltpu.VMEM((SUB, c), F32), pltpu.VMEM((SUB, c), F32), pltpu.VMEM((SUB, c), F32),
                        pltpu.VMEM((tr, c), F32)],
        compiler_params=_params('arbitrary'), name=name)(
            dy, z, z, z, xr, hs, hs, cw, wa, ba.reshape(1, c), wx, bx.reshape(1, c), lam.reshape(1, c))


def _local_step(x, p, pos, target, rep, weights_for_layer, emit_grads):
    t, d = x.shape
    depth = p.shape[0]
    w = rep
    half = ROPE_DIM // 2
    invf = ROPE_THETA ** (-2.0 * jnp.arange(half, dtype=F32) / ROPE_DIM)
    invf = jnp.concatenate([invf, invf, jnp.zeros((HEAD_DIM - ROPE_DIM,), F32)]).reshape(1, HEAD_DIM)
    dils = tuple(dil for _, dil in DILATED_PATTERNS)
    saved = []
    h = x
    for i in range(depth):
        kind, j = i % N_MIXERS, i // N_MIXERS
        wl, tok = weights_for_layer(i, 'mixer', h)
        s = {'h0': h, 'wl': wl}
        hn = _rms_fwd(h, w['norm_mix'][i], f'rms_mix_fwd_{i}')
        s['hn'] = hn
        if kind == 0:
            z = _mm(hn, wl['w_in'], 'nn', f'sc_in_{i}', dep=tok)
            y = _sc_fwd(z, wl['small'], f'sc_conv_fwd_{i}')
            h1 = _mm(y, wl['w_out'], 'nn', f'sc_out_{i}', extras=(h,), epi=lambda acc, res: (acc + res,))
            s.update(z=z, y=y)
        elif kind == 1:
            qkv = _mm(hn, wl['w_in'], 'nn', f'attn_qkv_{i}', dep=tok)
            qs, ks, vs = _rope_fwd(qkv, pos, invf, dils, f'rope_fwd_{i}')
            views = list(zip(qs, ks, vs))
            os_, lses = zip(*[_attn_fwd(qg, kg, vg, f'attn_fwd_{i}_g{g}') for g, (qg, kg, vg) in enumerate(views)])
            o, lse = _attn_combine(os_, lses, dils, f'attn_combine_{i}')
            h1 = _mm(o, wl['w_out'], 'nn', f'attn_out_{i}', extras=(h,), epi=lambda acc, res: (acc + res,))
            s.update(views=views, o=o, lse=lse)
        else:
            z = _mm(hn, wl['w_in'], 'nn', f'lru_in_{i}', dep=tok)
            sm = wl['small']
            y, hs, xr = _lru_fwd(z, sm[0:4], sm[4:5], w['lru_w_a'][j], sm[5:6], w['lru_w_x'][j], sm[6:7], sm[7:8],
                                 f'lru_fwd_{i}')
            h1 = _mm(y, wl['w_out'], 'nn', f'lru_out_{i}', extras=(h,), epi=lambda acc, res: (acc + res,))
            s.update(z=z, y=y, hs=hs, xr=xr)
        s['h1'] = h1
        more, tok = weights_for_layer(i, 'mlp', h1)
        wl.update(more)
        hm = _rms_fwd(h1, w['norm_mlp'][i], f'rms_mlp_fwd_{i}')
        u = _mm(hm, wl['mlp_up'], 'nn', f'mlp_up_{i}', out_dtypes=(BF16,), dep=tok)
        h2 = _mm(u, wl['mlp_down'], 'nn', f'mlp_down_{i}', a_pro=_relu2, extras=(h1,), epi=lambda acc, res: (acc + res,))
        hp = _rms_fwd(h2, w['norm_ple'][i], f'rms_ple_fwd_{i}')
        pp = _mm(p[i], wl['ple_proj'], 'nn', f'ple_proj_{i}')
        h3, gate = _mm(hp, wl['ple_gate'], 'nn', f'ple_gate_{i}', out_dtypes=(F32, F32), extras=(pp, h2),
                       epi=lambda acc, ppv, res: (res + _sigmoid(acc) * ppv, _sigmoid(acc)))
        s.update(hm=hm, u=u, h2=h2, hp=hp, pp=pp, gate=gate)
        saved.append(s)
        h = h3

    dh, loss, dg_final = _head(h, w['norm_final'], target, 'loss_head')
    grads = {n: [None] * depth for n in ('norm_mix', 'norm_mlp', 'norm_ple')}
    grads['norm_final'] = dg_final.reshape(d)
    started = None
    for i in reversed(range(depth)):
        kind, j = i % N_MIXERS, i // N_MIXERS
        s = saved[i]
        wl, gl = s['wl'], {}
        dpp, dgl = _ple_bwd_gate(dh, s['gate'], s['pp'], f'ple_bwd_gate_{i}')
        gl['ple_proj'] = _mm(p[i], dpp, 'tn', f'ple_dproj_{i}', out_dtypes=(BF16,), dep=started)
        gl['ple_gate'] = _mm(s['hp'], dgl, 'tn', f'ple_dgate_{i}', out_dtypes=(BF16,))
        dhp = _mm(dgl, wl['ple_gate'], 'nt', f'ple_dhp_{i}')
        dh, dg = _rms_bwd(s['h2'], w['norm_ple'][i], dhp, dh, f'rms_ple_bwd_{i}')
        grads['norm_ple'][i] = dg.reshape(d)
        du = _mm(dh, wl['mlp_down'], 'nt', f'mlp_du_{i}', out_dtypes=(BF16,), extras=(s['u'],),
                 epi=lambda acc, uv: (acc * 2.0 * jnp.maximum(uv.astype(F32), 0.0),))
        gl['mlp_down'] = _mm(s['u'], dh, 'tn', f'mlp_ddown_{i}', out_dtypes=(BF16,), a_pro=_relu2)
        gl['mlp_up'] = _mm(s['hm'], du, 'tn', f'mlp_dup_{i}', out_dtypes=(BF16,), out_stacked=True)
        dhm = _mm(du, wl['mlp_up'], 'nt', f'mlp_dhm_{i}')
        dh, dg = _rms_bwd(s['h1'], w['norm_mlp'][i], dhm, dh, f'rms_mlp_bwd_{i}')
        grads['norm_mlp'][i] = dg.reshape(d)
        started = emit_grads(i, 'mlp', gl, loss if i == depth - 1 else None)
        gl = {}
        if kind == 0:
            dy = _mm(dh, wl['w_out'], 'nt', f'sc_dy_{i}', dep=started)
            gl['w_out'] = _mm(s['y'], dh, 'tn', f'sc_dout_{i}', out_dtypes=(BF16,))
            dz, dwc = _sc_bwd(dy, s['z'], wl['small'], f'sc_conv_bwd_{i}')
            gl['small'] = dwc
            gl['w_in'] = _mm(s['hn'], dz, 'tn', f'sc_din_{i}', out_dtypes=(BF16,))
            started = emit_grads(i, 'mixer', gl)
            dhn = _mm(dz, wl['w_in'], 'nt', f'sc_dhn_{i}', dep=started)
        elif kind == 1:
            do, delta = _mm(dh, wl['w_out'], 'nt', f'attn_do_{i}', out_dtypes=(BF16, F32), extras=(s['o'],),
                            epi=_delta_epilogue, dep=started)
            gl['w_out'] = _mm(s['o'], dh, 'tn', f'attn_dwo_{i}', out_dtypes=(BF16,))
            rows_in = {1: (do, s['lse'], delta)}
            for dil in dils:
                if dil not in rows_in:
                    rows_in[dil] = _dilate_many([do, s['lse'], delta], dil, (1, 3, 3), (BF16, F32, F32),
                                                f'attn_dilate_{i}_d{dil}')
            dqs, dks, dvs = zip(*[_attn_bwd(*s['views'][g], *rows_in[dil], f'attn_bwd_{i}_g{g}')
                                  for g, dil in enumerate(dils)])
            dqkv = _rope_bwd(dqs, dks, dvs, pos, invf, dils, f'rope_bwd_{i}')
            gl['w_in'] = _mm(s['hn'], dqkv, 'tn', f'attn_dqkv_{i}', out_dtypes=(BF16,), out_stacked=True)
            started = emit_grads(i, 'mixer', gl)
            dhn = _mm(dqkv, wl['w_in'], 'nt', f'attn_dhn_{i}', dep=started)
        else:
            dy = _mm(dh, wl['w_out'], 'nt', f'lru_dy_{i}', dep=started)
            gl['w_out'] = _mm(s['y'], dh, 'tn', f'lru_dout_{i}', out_dtypes=(BF16,))
            sm = wl['small']
            dz, dwa, dwx, dvec = _lru_bwd(dy, s['z'], s['xr'], s['hs'], sm[0:4], w['lru_w_a'][j], sm[5:6],
                                          w['lru_w_x'][j], sm[6:7], sm[7:8], f'lru_bwd_{i}')
            gl['gates'], gl['small'] = (dwa, dwx), dvec
            gl['w_in'] = _mm(s['hn'], dz, 'tn', f'lru_din_{i}', out_dtypes=(BF16,))
            started = emit_grads(i, 'mixer', gl)
            dhn = _mm(dz, wl['w_in'], 'nt', f'lru_dhn_{i}', dep=started)
        dh, dg = _rms_bwd(s['h0'], w['norm_mix'][i], dhn, dh, f'rms_mix_bwd_{i}')
        grads['norm_mix'][i] = dg.reshape(d)
        started = None
    return loss, dh, grads


_MESH = pl.DeviceIdType.MESH
_ANY = pl.BlockSpec(memory_space=pl.ANY)


def _block_view(ref, kind, idx):
    if kind == 'stack':
        return ref.at[idx]
    r = ref.shape[0] // N_DEV
    return ref.at[pl.ds(idx * r, r)]


def _gather_many(arrs, kinds, name, after=None):
    n = len(arrs)
    after = [] if after is None else [after]
    out_shapes = [SDS((N_DEV,) + a.shape if kd == 'stack' else (N_DEV * a.shape[0],) + a.shape[1:], a.dtype)
                  for a, kd in zip(arrs, kinds)]

    def body(*refs):
        x_refs, out_refs = refs[:n], refs[n + len(after):2 * n + len(after)]
        send_sems, recv_sems, local_sems = refs[2 * n + len(after):]
        x, y, c = lax.axis_index('x'), lax.axis_index('y'), lax.axis_index('c')
        me, sibling = (x, y, c), (x, y, 1 - c)
        chips = [(1 - x, y), (x, 1 - y), (1 - x, 1 - y)]

        def slab(t, px, py, pc):
            return _block_view(out_refs[t], kinds[t], 4 * px + 2 * py + pc)

        def copy(t, k, block, to, src=None):
            return pltpu.make_async_remote_copy(
                src_ref=slab(t, *block) if src is None else src, dst_ref=slab(t, *block),
                send_sem=send_sems.at[7 * t + k], recv_sem=recv_sems.at[7 * t + k], device_id=to, device_id_type=_MESH)

        mine = [pltpu.make_async_copy(x_refs[t], slab(t, *me), local_sems.at[t]) for t in range(n)]
        for cp in mine:
            cp.start()
        first = [copy(t, 0, me, sibling, src=x_refs[t]) for t in range(n)]
        first += [copy(t, 1 + j, me, (*chip, c), src=x_refs[t]) for j, chip in enumerate(chips) for t in range(n)]
        for cp in first:
            cp.start()
        passed = []
        for j, chip in enumerate(chips):
            for t in range(n):
                copy(t, 1 + j, (*chip, c), me).wait_recv()
                passed.append(copy(t, 4 + j, (*chip, c), sibling))
                passed[-1].start()
        for t in range(n):
            copy(t, 0, sibling, me).wait_recv()
            for j, chip in enumerate(chips):
                copy(t, 4 + j, (*chip, 1 - c), me).wait_recv()
        for cp in first + passed:
            cp.wait_send()
        for cp in mine:
            cp.wait()

    return pl.pallas_call(
        body, out_shape=out_shapes, in_specs=[_ANY] * (n + len(after)), out_specs=[_ANY] * n,
        scratch_shapes=[pltpu.SemaphoreType.DMA((7 * n,)), pltpu.SemaphoreType.DMA((7 * n,)), pltpu.SemaphoreType.DMA((n,))],
        name=name)(*arrs, *after)


_HBM = pl.BlockSpec(memory_space=pltpu.HBM)
_SEM = pl.BlockSpec(memory_space=pltpu.SEMAPHORE)
_EFFECT = pltpu.SideEffectType.DATAFLOW_SIDE_EFFECTING


def _direct_copies(mode, kinds, src_refs, land_refs, send_sems, recv_sems):
    x, y, c = lax.axis_index('x'), lax.axis_index('y'), lax.axis_index('c')
    my_idx = 4 * x + 2 * y + c
    copies = []
    for k in range(1, N_DEV):
        px, py, pc = (1 - x if k & 4 else x, 1 - y if k & 2 else y, 1 - c if k & 1 else c)
        for t, kd in enumerate(kinds):
            if mode == 'gather':
                src, dst = src_refs[t], _block_view(land_refs[t], kd, my_idx)
            else:
                src, dst = _block_view(src_refs[t], kd, 4 * px + 2 * py + pc), land_refs[t].at[my_idx]
            copies.append(pltpu.make_async_remote_copy(
                src_ref=src, dst_ref=dst, send_sem=send_sems.at[7 * t + k - 1], recv_sem=recv_sems.at[7 * t + k - 1],
                device_id=(px, py, pc), device_id_type=_MESH))
    return copies


def _own_part(mode, kind, src, land):
    idx = 4 * lax.axis_index('x') + 2 * lax.axis_index('y') + lax.axis_index('c')
    zeros = (0,) * (src.ndim - 1)
    if mode == 'gather':
        part = src
    elif kind == 'stack':
        part = lax.dynamic_index_in_dim(src, idx, 0, keepdims=False)
    else:
        r = src.shape[0] // N_DEV
        part = lax.dynamic_slice_in_dim(src, idx * r, r, 0)
    if mode == 'gather' and kind == 'rows':
        return lax.dynamic_update_slice(land, part, (idx * part.shape[0],) + zeros)
    return lax.dynamic_update_slice(land, part[None], (idx,) + (0,) * part.ndim)


def _send_start(mode, srcs, kinds, name, after=None):
    n = len(srcs)
    after = [] if after is None else [after]
    lands = []
    for a, kd in zip(srcs, kinds):
        if mode == 'gather':
            shape = (N_DEV,) + a.shape if kd == 'stack' else (N_DEV * a.shape[0],) + a.shape[1:]
        else:
            shape = a.shape if kd == 'stack' else (N_DEV, a.shape[0] // N_DEV) + a.shape[1:]
        lands.append(_own_part(mode, kd, a, lax.empty(shape, a.dtype)))

    def body(*refs):
        src_refs, land_refs = refs[:n], refs[n:2 * n]
        send_sems, recv_sems = refs[2 * n + len(after):2 * n + len(after) + 2]
        token = refs[-1]
        for cp in _direct_copies(mode, kinds, src_refs, land_refs, send_sems, recv_sems):
            cp.start()
        token[...] = jnp.zeros_like(token)

    outs = pl.pallas_call(
        body, name=name,
        out_shape=(pltpu.SemaphoreType.DMA((7 * n,)), pltpu.SemaphoreType.DMA((7 * n,)),
                   *[pltpu.HBM(a.shape, a.dtype) for a in srcs + lands], SDS((SUB, 128), F32)),
        in_specs=[_HBM] * (2 * n) + [_ANY] * len(after),
        out_specs=(_SEM, _SEM, *[_HBM] * (2 * n), pl.BlockSpec(memory_space=pltpu.VMEM)),
        input_output_aliases={i: 2 + i for i in range(2 * n)},
        compiler_params=pltpu.CompilerParams(has_side_effects=_EFFECT),
    )(*[pltpu.with_memory_space_constraint(a, pltpu.HBM) for a in srcs + lands], *after)
    return (outs[0], outs[1], list(outs[2:2 + 2 * n])), outs[-1]


def _send_wait(mode, flight, kinds, after, name):
    send, recv, bufs = flight
    n = len(kinds)

    def body(*refs):
        src_refs, land_refs, (send_sems, recv_sems) = refs[:n], refs[n:2 * n], refs[2 * n:2 * n + 2]
        copies = _direct_copies(mode, kinds, src_refs, land_refs, send_sems, recv_sems)
        for cp in copies:
            cp.wait_send()
        for cp in copies:
            cp.wait_recv()

    outs = pl.pallas_call(
        body, name=name, out_shape=[pltpu.HBM(a.shape, a.dtype) for a in bufs],
        in_specs=[_HBM] * (2 * n) + [_SEM, _SEM, _ANY], out_specs=[_HBM] * (2 * n),
        input_output_aliases={i: i for i in range(2 * n)},
        compiler_params=pltpu.CompilerParams(has_side_effects=_EFFECT),
    )(*bufs, send, recv, after)
    return list(outs[n:])


ADAMW_BLOCK_ELEMS = 128 * 1024


def _adamw_sum(wgt, parts, m, v, name):
    nl, r, c = wgt.shape
    assert len(parts) == nl and all(q.shape == (N_DEV, r, c) for q in parts), (name, wgt.shape, [q.shape for q in parts])
    tr = next((t for t in range(min(r, 512), 0, -16) if r % t == 0 and t * c <= ADAMW_BLOCK_ELEMS and t % 16 == 0), r)
    c1 = 1.0 - ADAM_B1 ** ADAM_STEP
    c2 = 1.0 - ADAM_B2 ** ADAM_STEP

    def body(w_ref, m_ref, v_ref, *rest):
        part_refs, (g_ref, d_ref, mo_ref, vo_ref) = rest[:nl], rest[nl:]
        for q in range(nl):
            @pl.when(pl.program_id(0) == q)
            def _(q=q):
                gv = part_refs[q][0].astype(F32)
                for s in range(1, N_DEV):
                    gv = gv + part_refs[q][s].astype(F32)
                mn = ADAM_B1 * m_ref[...] + (1.0 - ADAM_B1) * gv
                vn = ADAM_B2 * v_ref[...] + (1.0 - ADAM_B2) * (gv * gv)
                g_ref[...] = gv
                d_ref[...] = -ADAM_LR * ((mn / c1) / (jnp.sqrt(vn / c2) + ADAM_EPS) + ADAM_WD * w_ref[...])
                mo_ref[...] = mn
                vo_ref[...] = vn

    spec = pl.BlockSpec((None, tr, c), lambda l, i: (l, i, 0))
    part_specs = [pl.BlockSpec((N_DEV, tr, c), lambda l, i, q=q: (0, jnp.where(l == q, i, 0), 0)) for q in range(nl)]
    return pl.pallas_call(body, grid=(nl, r // tr), in_specs=[spec] * 3 + part_specs, out_specs=[spec] * 4,
                          out_shape=[SDS((nl, r, c), F32)] * 4, compiler_params=_params('arbitrary', 'arbitrary'),
                          name=name)(wgt, m, v, *parts)


MIXER_WEIGHTS = {0: ('sc_w_in', 'sc_w_out'), 1: ('attn_w_qkv', 'attn_w_o'), 2: ('lru_w_in', 'lru_w_out')}
STACKED_OPERANDS = ('attn_w_qkv', 'mlp_w_up')
LRU_SMALL = ('lru_conv_w', 'lru_conv_b', 'lru_b_a', 'lru_b_x', 'lru_lambda')


def _layer_items(i):
    w_in, w_out = MIXER_WEIGHTS[i % N_MIXERS]
    j = i // N_MIXERS
    return [('w_in', w_in, j), ('w_out', w_out, j), ('mlp_up', 'mlp_w_up', i), ('mlp_down', 'mlp_w_down', i),
            ('ple_gate', 'ple_w_gate', i), ('ple_proj', 'ple_w_proj', i)]


def _cols_to_full(stacked):
    return jnp.moveaxis(stacked, 0, 1).reshape(stacked.shape[1], -1)


def _full_to_cols(full):
    k, n = full.shape
    return jnp.moveaxis(full.reshape(k, N_DEV, n // N_DEV), 1, 0)


def _pad_to(a, rows):
    return jnp.pad(a, ((0, rows - a.shape[0]), (0, 0)))


def _small_block(src, i):
    kind, j = i % N_MIXERS, i // N_MIXERS
    if kind == 0:
        return _pad_to(src['sc_w_conv'][j], SUB)
    if kind == 2:
        return jnp.concatenate([src[n][j].reshape(-1, src[n].shape[-1]) for n in LRU_SMALL], axis=0)
    return None


def kernel(x, p, positions, norm_mix, norm_mlp, norm_ple, norm_final, sc_w_in, sc_w_conv, sc_w_out, attn_w_qkv, attn_w_o, lru_w_in, lru_conv_w, lru_conv_b, lru_w_a, lru_b_a, lru_w_x, lru_b_x, lru_lambda, lru_w_out, mlp_w_up, mlp_w_down, ple_w_gate, ple_w_proj, loss_target, m_norm_mix, m_norm_mlp, m_norm_ple, m_norm_final, m_sc_w_in, m_sc_w_conv, m_sc_w_out, m_attn_w_qkv, m_attn_w_o, m_lru_w_in, m_lru_conv_w, m_lru_conv_b, m_lru_w_a, m_lru_b_a, m_lru_w_x, m_lru_b_x, m_lru_lambda, m_lru_w_out, m_mlp_w_up, m_mlp_w_down, m_ple_w_gate, m_ple_w_proj, v_norm_mix, v_norm_mlp, v_norm_ple, v_norm_final, v_sc_w_in, v_sc_w_conv, v_sc_w_out, v_attn_w_qkv, v_attn_w_o, v_lru_w_in, v_lru_conv_w, v_lru_conv_b, v_lru_w_a, v_lru_b_a, v_lru_w_x, v_lru_b_x, v_lru_lambda, v_lru_w_out, v_mlp_w_up, v_mlp_w_down, v_ple_w_gate, v_ple_w_proj):
    loc = dict(locals())
    shards = {n: loc[n] for n in WEIGHTS}
    moms = {n: loc['m_' + n] for n in WEIGHTS}
    vels = {n: loc['v_' + n] for n in WEIGHTS}

    depth, t, d = p.shape[0], x.shape[1], x.shape[2]

    def comm_kind(name):
        return 'stack' if SHARD_AXIS[name] == 2 else 'rows'

    part_keys = {'mlp': ('mlp_up', 'mlp_down', 'ple_gate', 'ple_proj'), 'mixer': ('w_in', 'w_out')}
    halves = [(i, part) for i in range(depth) for part in ('mixer', 'mlp')]

    def half_shards(i, part):
        items = [it for it in _layer_items(i) if it[0] in part_keys[part]]
        arrs = [shards[n][idx].astype(BF16) for _, n, idx in items]
        kinds = [comm_kind(n) for _, n, _ in items]
        small = _small_block(shards, i) if part == 'mixer' else None
        if small is not None:
            arrs.append(small)
            kinds.append('stack')
        return items, arrs, kinds

    def half_weights(i, items, kinds, outs):
        wl = {key: (_cols_to_full(o) if kd == 'stack' and n not in STACKED_OPERANDS else o)
              for (key, n, _), kd, o in zip(items, kinds, outs)}
        if len(outs) > len(items):
            wl['small'] = _cols_to_full(outs[-1])[:shards['sc_w_conv'].shape[1] if i % N_MIXERS == 0 else SUB]
        return wl

    first = [half_shards(0, part) for part in ('mixer', 'mlp')]
    outs0 = _gather_many(first[0][1] + first[1][1], first[0][2] + first[1][2], 'gather_weights_0')
    weights0 = {**half_weights(0, first[0][0], first[0][2], outs0[:len(first[0][1])]),
                **half_weights(0, first[1][0], first[1][2], outs0[len(first[0][1]):])}
    pending = {}

    def start_gather(pos, after):
        if pos >= len(halves):
            return None
        i, part = halves[pos]
        items, arrs, kinds = half_shards(i, part)
        flight, token = _send_start('gather', arrs, kinds, f'gather_weights_start_{part}_{i}', after=after)
        pending[pos] = (items, kinds, flight)
        return token

    first_token = start_gather(2, outs0[0])

    def weights_for_layer(i, part, h):
        pos = halves.index((i, part))
        if pos == 0:
            return weights0, first_token
        if pos == 1:
            return {}, None
        items, kinds, flight = pending.pop(pos)
        outs = _send_wait('gather', flight, kinds, h, f'gather_weights_wait_{part}_{i}')
        return half_weights(i, items, kinds, outs), start_gather(pos + 1, outs[0])

    exchanges, gate_gathers, total_loss = {}, {}, []

    def gate_block(src, j):
        return jnp.concatenate([src[n][j].reshape(-1, LRU_BLOCK) for n in ('lru_w_a', 'lru_w_x')], axis=0)

    def emit_grads(i, part, gl, loss=None):
        after = None
        if loss is not None:
            total_loss.append(lax.psum(loss[0, 0], ('x', 'y', 'c')))
            after = jnp.full((SUB, 128), total_loss[0], F32)
        if 'gates' in gl:
            blk = gate_block({'lru_w_a': [gl['gates'][0]], 'lru_w_x': [gl['gates'][1]]}, 0)
            gate_gathers[i] = _send_start('gather', [blk], ['stack'], f'gather_gate_grads_start_{i}')[0]
        items = [it for it in _layer_items(i) if it[0] in part_keys[part]]
        kinds = [comm_kind(n) for _, n, _ in items]
        arrs = [_full_to_cols(gl[key]) if kd == 'stack' and gl[key].ndim == 2 else gl[key]
                for (key, _, _), kd in zip(items, kinds)]
        if part == 'mixer' and i % N_MIXERS == 0:
            arrs.append(_full_to_cols(_pad_to(gl['small'], SUB)))
        elif part == 'mixer' and i % N_MIXERS == 2:
            dv = gl['small']
            arrs.append(_full_to_cols(jnp.concatenate([dv[4:8], dv[3:4], dv[0:1], dv[1:2], dv[2:3]], axis=0)))
        kinds += ['stack'] * (len(arrs) - len(kinds))
        flight, token = _send_start('exchange', arrs, kinds, f'exchange_grads_start_{part}_{i}', after=after)
        exchanges[(i, part)] = (items, kinds, flight)
        return token

    rep = {n: shards[n] for n in ('norm_mix', 'norm_mlp', 'norm_ple', 'norm_final')}
    rep['lru_w_a'], rep['lru_w_x'] = shards['lru_w_a'].astype(BF16), shards['lru_w_x'].astype(BF16)
    loss, grad_x, rgrads = _local_step(x.reshape(t, d), p.reshape(depth, t, p.shape[3]), positions.reshape(t, 1),
                                       loss_target.reshape(t, d), rep, weights_for_layer, emit_grads)

    received = {}
    for (i, part), (items, kinds, flight) in exchanges.items():
        outs = _send_wait('exchange', flight, kinds, grad_x, f'exchange_grads_wait_{part}_{i}')
        for (_, n, idx), o in zip(items, outs):
            received[(n, idx)] = o
        if len(outs) > len(items):
            received[('small', i)] = outs[-1]

    res = {}
    for n in WEIGHTS:
        if SHARD_AXIS[n] is not None and shards[n].ndim == 3 and n not in ('sc_w_conv', 'lru_conv_w'):
            res[n] = _adamw_sum(shards[n], [received[(n, l)] for l in range(shards[n].shape[0])], moms[n], vels[n],
                                f'adamw_{n}')
    def small_adamw(layers, name):
        w_, m_, v_ = (jnp.stack([_small_block(src, i) for i in layers]) for src in (shards, moms, vels))
        return _adamw_sum(w_, [received[('small', i)] for i in layers], m_, v_, name)

    sc = small_adamw([i for i in range(depth) if i % N_MIXERS == 0], 'adamw_sc_w_conv')
    res['sc_w_conv'] = tuple(o[:, :shards['sc_w_conv'].shape[1]] for o in sc)
    lru = small_adamw([i for i in range(depth) if i % N_MIXERS == 2], 'adamw_lru_small')
    row = 0
    for n in LRU_SMALL:
        k = shards[n].size // shards[n].shape[0] // shards[n].shape[-1]
        res[n] = tuple(o[:, row:row + k].reshape(shards[n].shape) for o in lru)
        row += k

    gate_layers = sorted(gate_gathers)
    gate_parts = [_send_wait('gather', gate_gathers[i], ['stack'], grad_x, f'gather_gate_grads_wait_{i}')[0]
                  for i in gate_layers]
    gate_w, gate_m, gate_v = (jnp.stack([gate_block(src, j) for j in range(len(gate_layers))])
                              for src in (shards, moms, vels))
    gates = _adamw_sum(gate_w, gate_parts, gate_m, gate_v, 'adamw_lru_gates')
    half = gates[0].shape[1] // 2
    res['lru_w_a'] = tuple(o[:, :half].reshape(shards['lru_w_a'].shape) for o in gates)
    res['lru_w_x'] = tuple(o[:, half:].reshape(shards['lru_w_x'].shape) for o in gates)

    norm_names = ('norm_mix', 'norm_mlp', 'norm_ple', 'norm_final')

    def norm_block(src):
        cat = jnp.concatenate([src[n].reshape(-1, d) for n in norm_names], axis=0)
        return _pad_to(cat, -(-cat.shape[0] // HALO) * HALO)

    rfull = {n: (rgrads[n] if n == 'norm_final' else jnp.stack(rgrads[n], axis=0)) for n in norm_names}
    updated = jnp.full((SUB, 128), sum(r[0].reshape(-1)[0] for r in res.values()), F32)
    parts_norm, = _gather_many([norm_block(rfull)], ['stack'], 'gather_norm_grads', after=updated)
    norms = _adamw_sum(norm_block(shards)[None], [parts_norm], norm_block(moms)[None], norm_block(vels)[None],
                       'adamw_norms')
    row = 0
    for n in norm_names:
        k = shards[n].size // d
        res[n] = tuple(o[0, row:row + k].reshape(shards[n].shape) for o in norms)
        row += k

    return (total_loss[0], grad_x.reshape(x.shape), *[res[n][0] for n in WEIGHTS], *[res[n][1] for n in WEIGHTS],
            *[res[n][2] for n in WEIGHTS], *[res[n][3] for n in WEIGHTS])
```

```python
import functools
import math

import jax
import jax.numpy as jnp
from jax import lax
from jax.experimental import pallas as pl
from jax.experimental.pallas import tpu as pltpu

F32 = jnp.float32
BF16 = jnp.bfloat16
SDS = jax.ShapeDtypeStruct

N_DEV = 8
RMS_EPS = 1e-6
N_MIXERS = 3
HEAD_DIM = 128
DILATED_PATTERNS = ((128, 1), (512, 4), (2048, 16))
ATTN_BLOCK = 128
ROPE_THETA = 500000.0
ROPE_DIM = HEAD_DIM // 4
LRU_BLOCK = 128
LRU_C = 8.0
ADAM_LR, ADAM_B1, ADAM_B2, ADAM_EPS, ADAM_WD, ADAM_STEP = 0.001, 0.9, 0.999, 1e-08, 0.01, 10

HALO = 16
SUB = 8
VMEM_LIMIT = 56 * 1024 * 1024
NEG = -1e30

SHARD_AXIS = {
    'norm_mix': None, 'norm_mlp': None, 'norm_ple': None, 'norm_final': None,
    'sc_w_in': 2, 'sc_w_conv': 2, 'sc_w_out': 1, 'attn_w_qkv': 2, 'attn_w_o': 1,
    'lru_w_in': 2, 'lru_conv_w': 2, 'lru_conv_b': 1, 'lru_w_a': None, 'lru_b_a': 1,
    'lru_w_x': None, 'lru_b_x': 1, 'lru_lambda': 1, 'lru_w_out': 1,
    'mlp_w_up': 2, 'mlp_w_down': 1, 'ple_w_gate': 1, 'ple_w_proj': 2,
}
WEIGHTS = list(SHARD_AXIS)


def _params(*sem):
    return pltpu.CompilerParams(dimension_semantics=sem or None, vmem_limit_bytes=VMEM_LIMIT)


def _row_tile(t, pref=256):
    tr = min(t, pref)
    assert t % tr == 0 and tr % HALO == 0
    return tr


def _row(tr, c, col=0):
    return pl.BlockSpec((tr, c), lambda i, col=col: (i, col))


def _full(shape):
    return pl.BlockSpec(shape, lambda *_: (0,) * len(shape))


def _sigmoid(x):
    return 1.0 / (1.0 + jnp.exp(-x))


def _expm1(x):
    taylor = x * (1.0 + x * (0.5 + x * (1.0 / 6.0 + x * (1.0 / 24.0 + x * (1.0 / 120.0)))))
    return jnp.where(jnp.abs(x) < 0.1, taylor, jnp.exp(x) - 1.0)


def _softplus(x):
    z = jnp.exp(-jnp.abs(x))
    log1p = jnp.where(z < 0.01, z * (1.0 - z * (0.5 - z * (1.0 / 3.0 - z * 0.25))), jnp.log(1.0 + z))
    return jnp.maximum(x, 0.0) + log1p


_GELU_K = math.sqrt(2.0 / math.pi)


def _gelu_and_grad(x):
    inner = _GELU_K * (x + 0.044715 * x * x * x)
    th = jnp.tanh(inner)
    g = 0.5 * x * (1.0 + th)
    dg = 0.5 * (1.0 + th) + 0.5 * x * (1.0 - th * th) * _GELU_K * (1.0 + 3.0 * 0.044715 * x * x)
    return g, dg


def _shift_down(x, k, prev):
    row = lax.broadcasted_iota(jnp.int32, (SUB, x.shape[1]), 0)
    xr = pltpu.roll(x, k, 0)
    top = jnp.where(row < k, pltpu.roll(prev, k, 0), xr[0:SUB])
    return jnp.concatenate([top, xr[SUB:]], axis=0)


def _shift_up(x, k, nxt):
    r = x.shape[0]
    row = lax.broadcasted_iota(jnp.int32, (SUB, x.shape[1]), 0)
    xr = pltpu.roll(x, r - k, 0)
    bot = jnp.where(row >= SUB - k, pltpu.roll(nxt, SUB - k, 0), xr[r - SUB:r])
    return jnp.concatenate([xr[:r - SUB], bot], axis=0)


_DIMS = {'nn': (((1,), (0,)), ((), ())), 'nt': (((1,), (1,)), ((), ())), 'tn': (((0,), (0,)), ((), ()))}


MM_VMEM_BUDGET = 50 * 1024 * 1024
MM_MIN_TK = 1024
MM_MIN_TM = 1024


def _tile_options(dim):
    return [c for c in range(dim, 127, -128) if dim % c == 0] or [dim]


def _choose_tiles(m, n, k, n_span, k_span, a_size, b_size, mn_size, a_temp):
    best = None
    for tm in _tile_options(m):
        for tn in _tile_options(n_span):
            for tk in _tile_options(k_span):
                nk = k // tk
                need = (2 * (tm * tk * a_size + tk * tn * b_size + tm * tn * mn_size) + tm * tn * 4 * (1 + (nk > 1))
                        + tm * tk * 4 * a_temp)
                score = (-min(tk, MM_MIN_TK), -min(tm, MM_MIN_TM), -tm * tn, nk, -min(tm, 2 * MM_MIN_TM), -tn)
                if need <= MM_VMEM_BUDGET and (best is None or score < best[0]):
                    best = (score, (tm, tn, tk))
    return best[1]


def _mm(a, b, dims, name, out_dtypes=(F32,), a_pro=None, extras=(), epi=None, out_stacked=False, dep=None):
    deps = [] if dep is None else [dep]
    stacked = b.ndim == 3
    b_rows, b_cols = (b.shape[1], N_DEV * b.shape[2]) if stacked else b.shape
    if dims == 'nn':
        (m, k), (k2, n) = a.shape, (b_rows, b_cols)
    elif dims == 'nt':
        (m, k), (n, k2) = a.shape, (b_rows, b_cols)
    else:
        (k, m), (k2, n) = a.shape, (b_rows, b_cols)
    assert k == k2, (name, a.shape, b.shape)
    assert not (stacked and dims == 'tn') and not (out_stacked and (extras or dims != 'tn'))
    tm, tn, tk = _choose_tiles(
        m, n, k, n // N_DEV if (out_stacked or (stacked and dims == 'nn')) else n,
        k // N_DEV if (stacked and dims == 'nt') else k, a.dtype.itemsize, b.dtype.itemsize,
        sum(e.dtype.itemsize for e in extras) + sum(jnp.dtype(dt).itemsize for dt in out_dtypes),
        a_pro is not None or a.dtype != BF16)
    assert m % tm == 0 and n % tn == 0 and k % tk == 0, (name, m, n, k)
    nk = k // tk
    a_spec = pl.BlockSpec((tk, tm), lambda i, j, kk: (kk, i)) if dims == 'tn' else pl.BlockSpec((tm, tk), lambda i, j, kk: (i, kk))
    if not stacked:
        b_spec = pl.BlockSpec((tn, tk), lambda i, j, kk: (j, kk)) if dims == 'nt' else pl.BlockSpec((tk, tn), lambda i, j, kk: (kk, j))
    elif dims == 'nn':
        per = b.shape[2] // tn
        b_spec = pl.BlockSpec((None, tk, tn), lambda i, j, kk: (j // per, kk, j % per))
    else:
        per = b.shape[2] // tk
        b_spec = pl.BlockSpec((None, tn, tk), lambda i, j, kk: (kk // per, j, kk % per))
    if out_stacked:
        per_o = n // N_DEV // tn
        o_spec = pl.BlockSpec((None, tm, tn), lambda i, j, kk: (j // per_o, i, j % per_o))
        o_shape = (N_DEV, m, n // N_DEV)
    else:
        o_spec = pl.BlockSpec((tm, tn), lambda i, j, kk: (i, j))
        o_shape = (m, n)
    n_ex, n_out = len(extras), len(out_dtypes)
    for e in extras:
        assert e.shape == (m, n), (name, e.shape)

    def body(a_ref, b_ref, *rest):
        rest = rest[len(deps):]
        ex_refs, out_refs = rest[:n_ex], rest[n_ex:n_ex + n_out]
        kk = pl.program_id(2)
        av = a_ref[...]
        if a_pro is not None:
            av = a_pro(av.astype(F32))
        part = lax.dot_general(av.astype(BF16), b_ref[...].astype(BF16), _DIMS[dims], preferred_element_type=F32)

        def finish(res):
            outs = (res,) if epi is None else epi(res, *[e[...] for e in ex_refs])
            for o_ref, o in zip(out_refs, outs):
                o_ref[...] = o.astype(o_ref.dtype)

        if nk == 1:
            finish(part)
        else:
            acc = rest[-1]

            @pl.when(kk == 0)
            def _():
                acc[...] = part

            @pl.when(kk > 0)
            def _():
                acc[...] += part

            @pl.when(kk == nk - 1)
            def _():
                finish(acc[...])

    out = pl.pallas_call(
        body, grid=(m // tm, n // tn, nk),
        in_specs=[a_spec, b_spec] + [_ANY] * len(deps) + [o_spec] * n_ex,
        out_specs=[o_spec] * n_out,
        out_shape=[SDS(o_shape, d) for d in out_dtypes],
        scratch_shapes=[] if nk == 1 else [pltpu.VMEM((tm, tn), F32)],
        compiler_params=_params('parallel', 'parallel', 'arbitrary'), name=name)(a, b, *deps, *extras)
    return out[0] if n_out == 1 else out


def _relu2(u):
    r = jnp.maximum(u, 0.0)
    return r * r


def _rms_fwd(h, g, name):
    t, d = h.shape
    tr = _row_tile(t)

    def body(h_ref, g_ref, o_ref):
        x = h_ref[...]
        r = lax.rsqrt(jnp.mean(x * x, axis=-1, keepdims=True) + RMS_EPS)
        o_ref[...] = (x * r * g_ref[...]).astype(o_ref.dtype)

    return pl.pallas_call(body, grid=(t // tr,), in_specs=[_row(tr, d), _full((1, d))], out_specs=_row(tr, d),
                          out_shape=SDS((t, d), BF16), compiler_params=_params('parallel'), name=name)(h, g.reshape(1, d))


def _rms_bwd(h, g, dhn, dres, name):
    t, d = h.shape
    tr = _row_tile(t)

    def body(h_ref, g_ref, dhn_ref, dres_ref, dh_ref, dg_ref):
        @pl.when(pl.program_id(0) == 0)
        def _():
            dg_ref[...] = jnp.zeros_like(dg_ref)

        x = h_ref[...]
        r = lax.rsqrt(jnp.mean(x * x, axis=-1, keepdims=True) + RMS_EPS)
        dy = dhn_ref[...].astype(F32)
        gy = dy * g_ref[...]
        dx = r * gy - x * (r * r * r) * jnp.mean(gy * x, axis=-1, keepdims=True)
        dh_ref[...] = dres_ref[...] + dx
        dg_ref[...] += jnp.sum(dy * (x * r), axis=0, keepdims=True)

    return pl.pallas_call(body, grid=(t // tr,),
                          in_specs=[_row(tr, d), _full((1, d)), _row(tr, d), _row(tr, d)],
                          out_specs=[_row(tr, d), _full((1, d))],
                          out_shape=[SDS((t, d), F32), SDS((1, d), F32)],
                          compiler_params=_params('arbitrary'), name=name)(h, g.reshape(1, d), dhn, dres)


def _head(h, g, target, name):
    t, d = h.shape
    tr = _row_tile(t)

    def body(h_ref, g_ref, t_ref, dh_ref, loss_ref, dg_ref):
        @pl.when(pl.program_id(0) == 0)
        def _():
            dg_ref[...] = jnp.zeros_like(dg_ref)
            loss_ref[...] = jnp.zeros_like(loss_ref)

        x = h_ref[...]
        gv = g_ref[...]
        r = lax.rsqrt(jnp.mean(x * x, axis=-1, keepdims=True) + RMS_EPS)
        xh = x * r
        e = xh * gv - t_ref[...]
        per_tok = jnp.mean(e * e, axis=-1, keepdims=True)
        loss_ref[...] += jnp.broadcast_to(0.5 * jnp.sum(per_tok, axis=0, keepdims=True), loss_ref.shape)
        dy = e * (1.0 / d)
        gy = dy * gv
        dh_ref[...] = r * gy - x * (r * r * r) * jnp.mean(gy * x, axis=-1, keepdims=True)
        dg_ref[...] += jnp.sum(dy * xh, axis=0, keepdims=True)

    return pl.pallas_call(body, grid=(t // tr,),
                          in_specs=[_row(tr, d), _full((1, d)), _row(tr, d)],
                          out_specs=[_row(tr, d), _full((1, 128)), _full((1, d))],
                          out_shape=[SDS((t, d), F32), SDS((1, 128), F32), SDS((1, d), F32)],
                          compiler_params=_params('arbitrary'), name=name)(h, g.reshape(1, d), target)


def _ple_bwd_gate(dh3, gate, pp, name):
    t, d = dh3.shape
    tr = _row_tile(t)

    def body(dh_ref, g_ref, pp_ref, dpp_ref, dgl_ref):
        dh = dh_ref[...]
        gt = g_ref[...]
        dpp_ref[...] = (dh * gt).astype(dpp_ref.dtype)
        dgl_ref[...] = (dh * pp_ref[...] * gt * (1.0 - gt)).astype(dgl_ref.dtype)

    return pl.pallas_call(body, grid=(t // tr,), in_specs=[_row(tr, d)] * 3, out_specs=[_row(tr, d)] * 2,
                          out_shape=[SDS((t, d), BF16), SDS((t, d), BF16)],
                          compiler_params=_params('parallel'), name=name)(dh3, gate, pp)


def _halo_prev(tr, c, col=0):
    return pl.BlockSpec((HALO, c), lambda i, col=col: (jnp.maximum(i * (tr // HALO) - 1, 0), col))


def _halo_next(tr, c, t, col=0):
    return pl.BlockSpec((HALO, c), lambda i, col=col: (jnp.minimum((i + 1) * (tr // HALO), t // HALO - 1), col))


def _sc_fwd(z, w, name):
    t, c3 = z.shape
    c = c3 // 3
    tr = _row_tile(t)

    def body(z_ref, zp_ref, w_ref, y_ref):
        i = pl.program_id(0)
        zz = z_ref[...]
        gb, cx = zz[:, :c], zz[:, c:2 * c] * zz[:, 2 * c:]
        zp = zp_ref[SUB:HALO, :]
        cxp = jnp.where(i > 0, zp[:, c:2 * c] * zp[:, 2 * c:], 0.0)
        wv = w_ref[...]
        conv = wv[2:3] * cx + wv[1:2] * _shift_down(cx, 1, cxp) + wv[0:1] * _shift_down(cx, 2, cxp)
        y_ref[...] = (gb * conv).astype(y_ref.dtype)

    return pl.pallas_call(body, grid=(t // tr,),
                          in_specs=[_row(tr, c3), _halo_prev(tr, c3), _full((3, c))],
                          out_specs=_row(tr, c), out_shape=SDS((t, c), BF16),
                          compiler_params=_params('parallel'), name=name)(z, z, w)


def _sc_bwd(dy, z, w, name):
    t, c3 = z.shape
    c = c3 // 3
    tr = _row_tile(t)
    nt = t // tr

    def body(dy_ref, dyn_ref, z_ref, zp_ref, zn_ref, w_ref, dz_ref, dw_ref):
        i = pl.program_id(0)

        @pl.when(i == 0)
        def _():
            dw_ref[...] = jnp.zeros_like(dw_ref)

        zz = z_ref[...]
        gb, gc, xi = zz[:, :c], zz[:, c:2 * c], zz[:, 2 * c:]
        cx = gc * xi
        zp = zp_ref[SUB:HALO, :]
        cxp = jnp.where(i > 0, zp[:, c:2 * c] * zp[:, 2 * c:], 0.0)
        wv = w_ref[...]
        cx1, cx2 = _shift_down(cx, 1, cxp), _shift_down(cx, 2, cxp)
        conv = wv[2:3] * cx + wv[1:2] * cx1 + wv[0:1] * cx2
        dyv = dy_ref[...]
        dconv = dyv * gb
        dcn = jnp.where(i < nt - 1, dyn_ref[0:SUB, :] * zn_ref[0:SUB, :c], 0.0)
        dcx = wv[2:3] * dconv + wv[1:2] * _shift_up(dconv, 1, dcn) + wv[0:1] * _shift_up(dconv, 2, dcn)
        dz_ref[:, :c] = (dyv * conv).astype(dz_ref.dtype)
        dz_ref[:, c:2 * c] = (dcx * xi).astype(dz_ref.dtype)
        dz_ref[:, 2 * c:] = (dcx * gc).astype(dz_ref.dtype)
        dw_ref[...] += jnp.concatenate([jnp.sum(dconv * cx2, axis=0, keepdims=True),
                                        jnp.sum(dconv * cx1, axis=0, keepdims=True),
                                        jnp.sum(dconv * cx, axis=0, keepdims=True)], axis=0)

    return pl.pallas_call(body, grid=(nt,),
                          in_specs=[_row(tr, c), _halo_next(tr, c, t), _row(tr, c3), _halo_prev(tr, c3),
                                    _halo_next(tr, c3, t), _full((3, c))],
                          out_specs=[_row(tr, c3), _full((3, c))],
                          out_shape=[SDS((t, c3), BF16), SDS((3, c), F32)],
                          compiler_params=_params('arbitrary'), name=name)(dy, dy, z, z, z, w)


def _perm(tr, dil, inverse=False):
    n = tr // dil
    a = lax.broadcasted_iota(jnp.int32, (tr, tr), 1 if inverse else 0)
    b = lax.broadcasted_iota(jnp.int32, (tr, tr), 0 if inverse else 1)
    return (b == (a % n) * dil + a // n).astype(BF16)


def _permute(pm, x, terms):
    if x.dtype == BF16:
        return jnp.dot(pm, x, preferred_element_type=F32)
    acc = None
    for _ in range(terms):
        part = x.astype(BF16)
        y = jnp.dot(pm, part, preferred_element_type=F32)
        acc = y if acc is None else acc + y
        x = x - part.astype(F32)
    return acc


def _store_dilated(o_ref, y, dil, d):
    n = y.shape[0] // dil
    for rho in range(dil):
        o_ref[:, rho * d:(rho + 1) * d] = y[rho * n:(rho + 1) * n].astype(o_ref.dtype)


def _load_dilated(ref, dil, d):
    return jnp.concatenate([ref[:, rho * d:(rho + 1) * d] for rho in range(dil)], axis=0) if dil > 1 else ref[...]


def _rope_heads(x, lane, cos, sin):
    return jnp.concatenate([_rope_apply(x[:, s:s + HEAD_DIM], lane, cos, sin)
                            for s in range(0, x.shape[1], HEAD_DIM)], axis=1)


def _rope_tables(pos, invf, sign):
    lane = lax.broadcasted_iota(jnp.int32, (pos.shape[0], HEAD_DIM), 1)
    ang = pos.astype(F32) * invf
    half = ROPE_DIM // 2
    cos = jnp.where(lane < ROPE_DIM, jnp.cos(ang), 1.0)
    sin = jnp.sin(ang) * sign
    sin = jnp.where(lane < half, -sin, jnp.where(lane < ROPE_DIM, sin, 0.0))
    return lane, cos, sin


def _rope_apply(x, lane, cos, sin):
    half = ROPE_DIM // 2
    xs = jnp.where(lane < half, pltpu.roll(x, HEAD_DIM - half, 1), pltpu.roll(x, half, 1))
    return x * cos + xs * sin


def _dilated_spec(tr, dil, d):
    return pl.BlockSpec((tr // dil, dil * d), lambda i: (i, 0))


def _rope_fwd(qkv, pos, invf, dils, name):
    t, w3 = qkv.shape
    w, ng = w3 // 3, len(dils)
    d = w // ng
    tr = _row_tile(t)

    def body(q_ref, k_ref, v_ref, pos_ref, invf_ref, *out_refs):
        lane, cos, sin = _rope_tables(pos_ref[...], invf_ref[...], 1.0)
        for g, dil in enumerate(dils):
            cs = slice(g * d, (g + 1) * d)
            vals = [_rope_heads(q_ref[:, cs], lane, cos, sin).astype(BF16),
                    _rope_heads(k_ref[:, cs], lane, cos, sin).astype(BF16), v_ref[:, cs].astype(BF16)]
            if dil > 1:
                pm = _perm(tr, dil)
                vals = [_permute(pm, a, 1) for a in vals]
            for o_ref, a in zip(out_refs[g::ng], vals):
                _store_dilated(o_ref, a, dil, d)

    outs = pl.pallas_call(body, grid=(t // tr,),
                          in_specs=[_row(tr, w, 0), _row(tr, w, 1), _row(tr, w, 2), _row(tr, 1), _full((1, HEAD_DIM))],
                          out_specs=[_dilated_spec(tr, dil, d) for dil in dils] * 3,
                          out_shape=[SDS((t // dil, dil * d), BF16) for dil in dils] * 3,
                          compiler_params=_params('parallel'), name=name)(qkv, qkv, qkv, pos, invf)
    return outs[:ng], outs[ng:2 * ng], outs[2 * ng:]


def _rope_bwd(dqs, dks, dvs, pos, invf, dils, name):
    ng = len(dils)
    t = dqs[0].shape[0] * dils[0]
    d = dqs[0].shape[1] // dils[0]
    w = ng * d
    tr = _row_tile(t)

    def body(*refs):
        dq_refs, dk_refs, dv_refs = refs[:ng], refs[ng:2 * ng], refs[2 * ng:3 * ng]
        pos_ref, invf_ref, o_ref = refs[3 * ng:]
        pos_f = jnp.broadcast_to(pos_ref[...].astype(F32), (tr, HEAD_DIM))
        for g, dil in enumerate(dils):
            pos_g = pos_f if dil == 1 else _permute(_perm(tr, dil), pos_f, 3)
            lane, cos, sin = _rope_tables(pos_g, invf_ref[...], -1.0)
            vals = [_rope_heads(_load_dilated(dq_refs[g], dil, d), lane, cos, sin),
                    _rope_heads(_load_dilated(dk_refs[g], dil, d), lane, cos, sin), _load_dilated(dv_refs[g], dil, d)]
            back = _perm(tr, dil, inverse=True) if dil > 1 else None
            for sec, a in enumerate(vals):
                a = a.astype(BF16)
                if dil > 1:
                    a = _permute(back, a, 1)
                o_ref[:, sec * w + g * d:sec * w + (g + 1) * d] = a.astype(o_ref.dtype)

    return pl.pallas_call(body, grid=(t // tr,),
                          in_specs=[_dilated_spec(tr, dil, d) for dil in dils] * 3 + [_row(tr, 1), _full((1, HEAD_DIM))],
                          out_specs=_row(tr, 3 * w), out_shape=SDS((t, 3 * w), BF16),
                          compiler_params=_params('parallel'), name=name)(*dqs, *dks, *dvs, pos, invf)


def _dilate_many(arrs, dil, terms, out_dtypes, name):
    t, d = arrs[0].shape
    tr = _row_tile(t)
    na = len(arrs)

    def body(*refs):
        pm = _perm(tr, dil)
        for a_ref, o_ref, k in zip(refs[:na], refs[na:], terms):
            _store_dilated(o_ref, _permute(pm, a_ref[...], k), dil, d)

    return pl.pallas_call(body, grid=(t // tr,), in_specs=[_row(tr, d)] * na,
                          out_specs=[_dilated_spec(tr, dil, d)] * na,
                          out_shape=[SDS((t // dil, dil * d), dt) for dt in out_dtypes],
                          compiler_params=_params('parallel'), name=name)(*arrs)


def _attn_masks():
    qi = lax.broadcasted_iota(jnp.int32, (ATTN_BLOCK, ATTN_BLOCK), 0)
    kj = lax.broadcasted_iota(jnp.int32, (ATTN_BLOCK, ATTN_BLOCK), 1)
    return kj >= qi, kj <= qi


def _attn_cols(l, width):
    ncol = width // HEAD_DIM
    cpb = max(1, min(ncol, 32 // (l // ATTN_BLOCK)))
    assert ncol % cpb == 0
    return cpb


def _attn_fwd(q, k, v, name):
    l, width = q.shape
    cpb = _attn_cols(l, width)
    nb = l // ATTN_BLOCK
    scale = HEAD_DIM ** -0.5

    def body(q_ref, k_ref, v_ref, o_ref, lse_ref):
        m_prev, m_cur = _attn_masks()
        for col in range(cpb):
            cs = slice(col * HEAD_DIM, (col + 1) * HEAD_DIM)

            def step(b, carry, cs=cs):
                r0 = pl.multiple_of(b * ATTN_BLOCK, ATTN_BLOCK)
                rp = pl.multiple_of(jnp.maximum(b - 1, 0) * ATTN_BLOCK, ATTN_BLOCK)
                qb = q_ref[pl.ds(r0, ATTN_BLOCK), cs]
                s_p = lax.dot_general(qb, k_ref[pl.ds(rp, ATTN_BLOCK), cs], _DIMS['nt'], preferred_element_type=F32) * scale
                s_c = lax.dot_general(qb, k_ref[pl.ds(r0, ATTN_BLOCK), cs], _DIMS['nt'], preferred_element_type=F32) * scale
                s_p = jnp.where(jnp.logical_and(m_prev, b > 0), s_p, NEG)
                s_c = jnp.where(m_cur, s_c, NEG)
                m = jnp.maximum(jnp.max(s_p, axis=-1, keepdims=True), jnp.max(s_c, axis=-1, keepdims=True))
                p_p, p_c = jnp.exp(s_p - m), jnp.exp(s_c - m)
                den = jnp.sum(p_p, axis=-1, keepdims=True) + jnp.sum(p_c, axis=-1, keepdims=True)
                acc = jnp.dot(p_p.astype(BF16), v_ref[pl.ds(rp, ATTN_BLOCK), cs], preferred_element_type=F32)
                acc += jnp.dot(p_c.astype(BF16), v_ref[pl.ds(r0, ATTN_BLOCK), cs], preferred_element_type=F32)
                o_ref[pl.ds(r0, ATTN_BLOCK), cs] = acc / den
                lse_ref[pl.ds(r0, ATTN_BLOCK), cs] = jnp.broadcast_to(m + jnp.log(den), (ATTN_BLOCK, HEAD_DIM))
                return carry

            lax.fori_loop(0, nb, step, 0, unroll=min(nb, 4))

    spec = pl.BlockSpec((l, cpb * HEAD_DIM), lambda j: (0, j))
    return pl.pallas_call(body, grid=(width // (cpb * HEAD_DIM),), in_specs=[spec] * 3, out_specs=[spec] * 2,
                          out_shape=[SDS((l, width), F32)] * 2,
                          compiler_params=_params('parallel'), name=name)(q, k, v)


def _attn_bwd(q, k, v, do, lse, delta, name):
    l, width = q.shape
    cpb = _attn_cols(l, width)
    nb = l // ATTN_BLOCK
    scale = HEAD_DIM ** -0.5

    def body(q_ref, k_ref, v_ref, do_ref, lse_ref, dl_ref, dq_ref, dk_ref, dv_ref):
        m_prev, m_cur = _attn_masks()
        dk_ref[...] = jnp.zeros_like(dk_ref)
        dv_ref[...] = jnp.zeros_like(dv_ref)
        for col in range(cpb):
            cs = slice(col * HEAD_DIM, (col + 1) * HEAD_DIM)

            def step(b, carry, cs=cs):
                r0 = pl.multiple_of(b * ATTN_BLOCK, ATTN_BLOCK)
                rp = pl.multiple_of(jnp.maximum(b - 1, 0) * ATTN_BLOCK, ATTN_BLOCK)
                qb, dob = q_ref[pl.ds(r0, ATTN_BLOCK), cs], do_ref[pl.ds(r0, ATTN_BLOCK), cs].astype(BF16)
                kp, kc = k_ref[pl.ds(rp, ATTN_BLOCK), cs], k_ref[pl.ds(r0, ATTN_BLOCK), cs]
                vp, vc = v_ref[pl.ds(rp, ATTN_BLOCK), cs], v_ref[pl.ds(r0, ATTN_BLOCK), cs]
                lse_b = lse_ref[pl.ds(r0, ATTN_BLOCK), cs]
                dl_b = dl_ref[pl.ds(r0, ATTN_BLOCK), cs]
                s_p = lax.dot_general(qb, kp, _DIMS['nt'], preferred_element_type=F32) * scale
                s_c = lax.dot_general(qb, kc, _DIMS['nt'], preferred_element_type=F32) * scale
                p_p = jnp.exp(jnp.where(jnp.logical_and(m_prev, b > 0), s_p, NEG) - lse_b)
                p_c = jnp.exp(jnp.where(m_cur, s_c, NEG) - lse_b)
                dp_p = lax.dot_general(dob, vp, _DIMS['nt'], preferred_element_type=F32)
                dp_c = lax.dot_general(dob, vc, _DIMS['nt'], preferred_element_type=F32)
                ds_p = (p_p * (dp_p - dl_b) * scale).astype(BF16)
                ds_c = (p_c * (dp_c - dl_b) * scale).astype(BF16)
                dq_ref[pl.ds(r0, ATTN_BLOCK), cs] = (jnp.dot(ds_p, kp, preferred_element_type=F32)
                                                     + jnp.dot(ds_c, kc, preferred_element_type=F32))
                dk_ref[pl.ds(rp, ATTN_BLOCK), cs] += lax.dot_general(ds_p, qb, _DIMS['tn'], preferred_element_type=F32)
                dk_ref[pl.ds(r0, ATTN_BLOCK), cs] += lax.dot_general(ds_c, qb, _DIMS['tn'], preferred_element_type=F32)
                dv_ref[pl.ds(rp, ATTN_BLOCK), cs] += lax.dot_general(p_p.astype(BF16), dob, _DIMS['tn'], preferred_element_type=F32)
                dv_ref[pl.ds(r0, ATTN_BLOCK), cs] += lax.dot_general(p_c.astype(BF16), dob, _DIMS['tn'], preferred_element_type=F32)
                return carry

            lax.fori_loop(0, nb, step, 0, unroll=min(nb, 2))

    spec = pl.BlockSpec((l, cpb * HEAD_DIM), lambda j: (0, j))
    return pl.pallas_call(body, grid=(width // (cpb * HEAD_DIM),), in_specs=[spec] * 6, out_specs=[spec] * 3,
                          out_shape=[SDS((l, width), F32)] * 3,
                          compiler_params=_params('parallel'), name=name)(q, k, v, do, lse, delta)


def _attn_combine(os_, lses, dils, name):
    ng = len(dils)
    t = os_[0].shape[0] * dils[0]
    d = os_[0].shape[1] // dils[0]
    tr = _row_tile(t)

    def body(*refs):
        o_refs, l_refs, o_out, lse_out = refs[:ng], refs[ng:2 * ng], refs[2 * ng], refs[2 * ng + 1]
        ovs, ls = [], []
        for g, dil in enumerate(dils):
            ov, lv = _load_dilated(o_refs[g], dil, d), _load_dilated(l_refs[g], dil, d)
            if dil > 1:
                back = _perm(tr, dil, inverse=True)
                ov, lv = _permute(back, ov, 2), _permute(back, lv, 3)
            ovs.append(ov)
            ls.append(lv)
        m = functools.reduce(jnp.maximum, ls)
        ws = [jnp.exp(x - m) for x in ls]
        den = functools.reduce(lambda a, b: a + b, ws)
        acc = functools.reduce(lambda a, b: a + b, [w * o for w, o in zip(ws, ovs)])
        o_out[...] = (acc / den).astype(o_out.dtype)
        lse_out[...] = m + jnp.log(den)

    return pl.pallas_call(body, grid=(t // tr,), in_specs=[_dilated_spec(tr, dil, d) for dil in dils] * 2,
                          out_specs=[_row(tr, d)] * 2, out_shape=[SDS((t, d), BF16), SDS((t, d), F32)],
                          compiler_params=_params('parallel'), name=name)(*os_, *lses)


def _delta_epilogue(acc, o):
    prod = acc * o.astype(F32)
    segs = [jnp.broadcast_to(jnp.sum(prod[:, s:s + HEAD_DIM], axis=-1, keepdims=True), (acc.shape[0], HEAD_DIM))
            for s in range(0, acc.shape[1], HEAD_DIM)]
    return acc, jnp.concatenate(segs, axis=-1)


LRU_TILE = 128


def _lru_gates(xr, wa_ref, ba, wx_ref, bx, lam):
    nb = wa_ref.shape[0]
    xb = xr.astype(BF16)
    ra = jnp.concatenate([jnp.dot(xb[:, n * LRU_BLOCK:(n + 1) * LRU_BLOCK], wa_ref[n], preferred_element_type=F32)
                          for n in range(nb)], axis=-1) + ba
    ia = jnp.concatenate([jnp.dot(xb[:, n * LRU_BLOCK:(n + 1) * LRU_BLOCK], wx_ref[n], preferred_element_type=F32)
                          for n in range(nb)], axis=-1) + bx
    r, ig = _sigmoid(ra), _sigmoid(ia)
    sp = _softplus(-lam)
    log_a = -LRU_C * r * sp
    a = jnp.exp(log_a)
    mult = jnp.sqrt(-_expm1(2.0 * log_a))
    return xb, r, ig, sp, a, mult


def _lru_fwd(z, cw, cb, wa, ba, wx, bx, lam, name):
    t, c2 = z.shape
    c = c2 // 2
    nb = c // LRU_BLOCK
    tr = _row_tile(t, LRU_TILE)

    def body(g_ref, x_ref, xp_ref, cw_ref, cb_ref, wa_ref, ba_ref, wx_ref, bx_ref, lam_ref,
             y_ref, hs_ref, xr_ref, car_ref):
        i = pl.program_id(0)

        @pl.when(i == 0)
        def _():
            car_ref[...] = jnp.zeros_like(car_ref)

        x0 = x_ref[...]
        xp = jnp.where(i > 0, xp_ref[SUB:HALO, :], 0.0)
        cwv = cw_ref[...]
        xr = (cb_ref[...] + cwv[3:4] * x0 + cwv[2:3] * _shift_down(x0, 1, xp)
              + cwv[1:2] * _shift_down(x0, 2, xp) + cwv[0:1] * _shift_down(x0, 3, xp))
        xr_ref[...] = xr
        _, _, ig, _, a, mult = _lru_gates(xr, wa_ref, ba_ref[...], wx_ref, bx_ref[...], lam_ref[...])
        u = mult * (ig * xr)
        row = lax.broadcasted_iota(jnp.int32, (SUB, c), 0)
        car = car_ref[...]
        for j in range(tr // SUB):
            ab, ub = a[j * SUB:(j + 1) * SUB], u[j * SUB:(j + 1) * SUB]
            for s in (1, 2, 4):
                a_sh = jnp.where(row >= s, pltpu.roll(ab, s, 0), 1.0)
                u_sh = jnp.where(row >= s, pltpu.roll(ub, s, 0), 0.0)
                ub = ab * u_sh + ub
                ab = ab * a_sh
            hb = ub + ab * car
            hs_ref[j * SUB:(j + 1) * SUB, :] = hb
            car = jnp.broadcast_to(hb[SUB - 1:SUB], (SUB, c))
        car_ref[...] = car
        gl, _ = _gelu_and_grad(g_ref[...])
        y_ref[...] = (hs_ref[...] * gl).astype(y_ref.dtype)

    return pl.pallas_call(
        body, grid=(t // tr,),
        in_specs=[_row(tr, c, 0), _row(tr, c, 1), _halo_prev(tr, c, 1), _full((4, c)), _full((1, c)),
                  _full((nb, LRU_BLOCK, LRU_BLOCK)), _full((1, c)), _full((nb, LRU_BLOCK, LRU_BLOCK)), _full((1, c)), _full((1, c))],
        out_specs=[_row(tr, c)] * 3,
        out_shape=[SDS((t, c), BF16), SDS((t, c), F32), SDS((t, c), F32)],
        scratch_shapes=[pltpu.VMEM((SUB, c), F32)],
        compiler_params=_params('arbitrary'), name=name)(
            z, z, z, cw, cb.reshape(1, c), wa, ba.reshape(1, c), wx, bx.reshape(1, c), lam.reshape(1, c))


def _lru_bwd(dy, z, xr, hs, cw, wa, ba, wx, bx, lam, name):
    t, c2 = z.shape
    c = c2 // 2
    nb = c // LRU_BLOCK
    tr = _row_tile(t, LRU_TILE)
    nt = t // tr

    def rev(col=0):
        return pl.BlockSpec((tr, c), lambda i, col=col: (nt - 1 - i, col))

    def rev_prev(col=0):
        return pl.BlockSpec((HALO, c), lambda i, col=col: (jnp.maximum((nt - 1 - i) * (tr // HALO) - 1, 0), col))

    def body(dy_ref, g_ref, x_ref, xp_ref, xr_ref, hs_ref, hp_ref, cw_ref, wa_ref, ba_ref, wx_ref, bx_ref, lam_ref,
             dz_ref, dwa_ref, dwx_ref, dvec_ref, lcar_ref, ahead_ref, dxhead_ref, lam_s):
        i = pl.program_id(0)
        first_tile = i == nt - 1

        @pl.when(i == 0)
        def _():
            lcar_ref[...] = jnp.zeros_like(lcar_ref)
            ahead_ref[...] = jnp.zeros_like(ahead_ref)
            dxhead_ref[...] = jnp.zeros_like(dxhead_ref)
            dwa_ref[...] = jnp.zeros_like(dwa_ref)
            dwx_ref[...] = jnp.zeros_like(dwx_ref)
            dvec_ref[...] = jnp.zeros_like(dvec_ref)

        xrv = xr_ref[...]
        lamv = lam_ref[...]
        xb, r, ig, sp, a, mult = _lru_gates(xrv, wa_ref, ba_ref[...], wx_ref, bx_ref[...], lamv)
        hsv = hs_ref[...]
        dyv = dy_ref[...]
        gl, dgl = _gelu_and_grad(g_ref[...])
        dhs = dyv * gl
        dz_ref[:, :c] = (dyv * hsv * dgl).astype(dz_ref.dtype)

        a_next = _shift_up(a, 1, ahead_ref[...])
        row = lax.broadcasted_iota(jnp.int32, (SUB, c), 0)
        car = lcar_ref[...]
        for j in reversed(range(tr // SUB)):
            ab, ub = a_next[j * SUB:(j + 1) * SUB], dhs[j * SUB:(j + 1) * SUB]
            for s in (1, 2, 4):
                a_sh = jnp.where(row < SUB - s, pltpu.roll(ab, SUB - s, 0), 1.0)
                u_sh = jnp.where(row < SUB - s, pltpu.roll(ub, SUB - s, 0), 0.0)
                ub = ab * u_sh + ub
                ab = ab * a_sh
            lb = ub + ab * car
            lam_s[j * SUB:(j + 1) * SUB, :] = lb
            car = jnp.broadcast_to(lb[0:1], (SUB, c))
        lcar_ref[...] = car
        ahead_ref[...] = a[0:SUB]
        lmb = lam_s[...]

        hp = jnp.where(first_tile, 0.0, hp_ref[SUB:HALO, :])
        h_prev = _shift_down(hsv, 1, hp)
        d_a = lmb * h_prev
        d_mult = lmb * (ig * xrv)
        d_ixr = lmb * mult
        d_ig = d_ixr * xrv
        dxr = d_ixr * ig
        d_la = d_a * a - d_mult * (a * a) / mult
        d_r = d_la * (-LRU_C * sp)
        d_sp = jnp.sum(d_la * (-LRU_C * r), axis=0, keepdims=True)
        d_ra = d_r * r * (1.0 - r)
        d_ia = d_ig * ig * (1.0 - ig)
        d_rab, d_iab = d_ra.astype(BF16), d_ia.astype(BF16)
        parts = []
        for n in range(nb):
            cs = slice(n * LRU_BLOCK, (n + 1) * LRU_BLOCK)
            parts.append(lax.dot_general(d_rab[:, cs], wa_ref[n], _DIMS['nt'], preferred_element_type=F32)
                         + lax.dot_general(d_iab[:, cs], wx_ref[n], _DIMS['nt'], preferred_element_type=F32))
            dwa_ref[n] += lax.dot_general(xb[:, cs], d_rab[:, cs], _DIMS['tn'], preferred_element_type=F32)
            dwx_ref[n] += lax.dot_general(xb[:, cs], d_iab[:, cs], _DIMS['tn'], preferred_element_type=F32)
        dxr = dxr + jnp.concatenate(parts, axis=-1)

        cwv = cw_ref[...]
        nxt = dxhead_ref[...]
        dx0 = (cwv[3:4] * dxr + cwv[2:3] * _shift_up(dxr, 1, nxt) + cwv[1:2] * _shift_up(dxr, 2, nxt)
               + cwv[0:1] * _shift_up(dxr, 3, nxt))
        dxhead_ref[...] = dxr[0:SUB]
        dz_ref[:, c:] = dx0.astype(dz_ref.dtype)

        x0 = x_ref[...]
        xp = jnp.where(first_tile, 0.0, xp_ref[SUB:HALO, :])
        sums = [jnp.sum(d_ra, axis=0, keepdims=True), jnp.sum(d_ia, axis=0, keepdims=True),
                d_sp * (-_sigmoid(-lamv)), jnp.sum(dxr, axis=0, keepdims=True),
                jnp.sum(dxr * _shift_down(x0, 3, xp), axis=0, keepdims=True),
                jnp.sum(dxr * _shift_down(x0, 2, xp), axis=0, keepdims=True),
                jnp.sum(dxr * _shift_down(x0, 1, xp), axis=0, keepdims=True),
                jnp.sum(dxr * x0, axis=0, keepdims=True)]
        dvec_ref[...] += jnp.concatenate(sums, axis=0)

    wspec = _full((nb, LRU_BLOCK, LRU_BLOCK))
    return pl.pallas_call(
        body, grid=(nt,),
        in_specs=[rev(), rev(0), rev(1), rev_prev(1), rev(), rev(), rev_prev(), _full((4, c)),
                  wspec, _full((1, c)), wspec, _full((1, c)), _full((1, c))],
        out_specs=[pl.BlockSpec((tr, c2), lambda i: (nt - 1 - i, 0)), wspec, wspec, _full((SUB, c))],
        out_shape=[SDS((t, c2), BF16), SDS((nb, LRU_BLOCK, LRU_BLOCK), F32), SDS((nb, LRU_BLOCK, LRU_BLOCK), F32),
                   SDS((SUB, c), F32)],
        scratch_shapes=[pltpu.VMEM((SUB, c), F32), pltpu.VMEM((SUB, c), F32), pltpu.VMEM((SUB, c), F32),
                        pltpu.VMEM((tr, c), F32)],
        compiler_params=_params('arbitrary'), name=name)(
            dy, z, z, z, xr, hs, hs, cw, wa, ba.reshape(1, c), wx, bx.reshape(1, c), lam.reshape(1, c))


def _local_step(x, p, pos, target, rep, weights_for_layer, emit_grads):
    t, d = x.shape
    depth = p.shape[0]
    w = rep
    half = ROPE_DIM // 2
    invf = ROPE_THETA ** (-2.0 * jnp.arange(half, dtype=F32) / ROPE_DIM)
    invf = jnp.concatenate([invf, invf, jnp.zeros((HEAD_DIM - ROPE_DIM,), F32)]).reshape(1, HEAD_DIM)
    dils = tuple(dil for _, dil in DILATED_PATTERNS)
    saved = []
    h = x
    for i in range(depth):
        kind, j = i % N_MIXERS, i // N_MIXERS
        wl, tok = weights_for_layer(i, 'mixer', h)
        s = {'h0': h, 'wl': wl}
        hn = _rms_fwd(h, w['norm_mix'][i], f'rms_mix_fwd_{i}')
        s['hn'] = hn
        if kind == 0:
            z = _mm(hn, wl['w_in'], 'nn', f'sc_in_{i}', dep=tok)
            y = _sc_fwd(z, wl['small'], f'sc_conv_fwd_{i}')
            h1 = _mm(y, wl['w_out'], 'nn', f'sc_out_{i}', extras=(h,), epi=lambda acc, res: (acc + res,))
            s.update(z=z, y=y)
        elif kind == 1:
            qkv = _mm(hn, wl['w_in'], 'nn', f'attn_qkv_{i}', dep=tok)
            qs, ks, vs = _rope_fwd(qkv, pos, invf, dils, f'rope_fwd_{i}')
            views = list(zip(qs, ks, vs))
            os_, lses = zip(*[_attn_fwd(qg, kg, vg, f'attn_fwd_{i}_g{g}') for g, (qg, kg, vg) in enumerate(views)])
            o, lse = _attn_combine(os_, lses, dils, f'attn_combine_{i}')
            h1 = _mm(o, wl['w_out'], 'nn', f'attn_out_{i}', extras=(h,), epi=lambda acc, res: (acc + res,))
            s.update(views=views, o=o, lse=lse)
        else:
            z = _mm(hn, wl['w_in'], 'nn', f'lru_in_{i}', dep=tok)
            sm = wl['small']
            y, hs, xr = _lru_fwd(z, sm[0:4], sm[4:5], w['lru_w_a'][j], sm[5:6], w['lru_w_x'][j], sm[6:7], sm[7:8],
                                 f'lru_fwd_{i}')
            h1 = _mm(y, wl['w_out'], 'nn', f'lru_out_{i}', extras=(h,), epi=lambda acc, res: (acc + res,))
            s.update(z=z, y=y, hs=hs, xr=xr)
        s['h1'] = h1
        more, tok = weights_for_layer(i, 'mlp', h1)
        wl.update(more)
        hm = _rms_fwd(h1, w['norm_mlp'][i], f'rms_mlp_fwd_{i}')
        u = _mm(hm, wl['mlp_up'], 'nn', f'mlp_up_{i}', out_dtypes=(BF16,), dep=tok)
        h2 = _mm(u, wl['mlp_down'], 'nn', f'mlp_down_{i}', a_pro=_relu2, extras=(h1,), epi=lambda acc, res: (acc + res,))
        hp = _rms_fwd(h2, w['norm_ple'][i], f'rms_ple_fwd_{i}')
        pp = _mm(p[i], wl['ple_proj'], 'nn', f'ple_proj_{i}')
        h3, gate = _mm(hp, wl['ple_gate'], 'nn', f'ple_gate_{i}', out_dtypes=(F32, F32), extras=(pp, h2),
                       epi=lambda acc, ppv, res: (res + _sigmoid(acc) * ppv, _sigmoid(acc)))
        s.update(hm=hm, u=u, h2=h2, hp=hp, pp=pp, gate=gate)
        saved.append(s)
        h = h3

    dh, loss, dg_final = _head(h, w['norm_final'], target, 'loss_head')
    grads = {n: [None] * depth for n in ('norm_mix', 'norm_mlp', 'norm_ple')}
    grads['norm_final'] = dg_final.reshape(d)
    started = None
    for i in reversed(range(depth)):
        kind, j = i % N_MIXERS, i // N_MIXERS
        s = saved[i]
        wl, gl = s['wl'], {}
        dpp, dgl = _ple_bwd_gate(dh, s['gate'], s['pp'], f'ple_bwd_gate_{i}')
        gl['ple_proj'] = _mm(p[i], dpp, 'tn', f'ple_dproj_{i}', out_dtypes=(BF16,), dep=started)
        gl['ple_gate'] = _mm(s['hp'], dgl, 'tn', f'ple_dgate_{i}', out_dtypes=(BF16,))
        dhp = _mm(dgl, wl['ple_gate'], 'nt', f'ple_dhp_{i}')
        dh, dg = _rms_bwd(s['h2'], w['norm_ple'][i], dhp, dh, f'rms_ple_bwd_{i}')
        grads['norm_ple'][i] = dg.reshape(d)
        du = _mm(dh, wl['mlp_down'], 'nt', f'mlp_du_{i}', out_dtypes=(BF16,), extras=(s['u'],),
                 epi=lambda acc, uv: (acc * 2.0 * jnp.maximum(uv.astype(F32), 0.0),))
        gl['mlp_down'] = _mm(s['u'], dh, 'tn', f'mlp_ddown_{i}', out_dtypes=(BF16,), a_pro=_relu2)
        gl['mlp_up'] = _mm(s['hm'], du, 'tn', f'mlp_dup_{i}', out_dtypes=(BF16,), out_stacked=True)
        dhm = _mm(du, wl['mlp_up'], 'nt', f'mlp_dhm_{i}')
        dh, dg = _rms_bwd(s['h1'], w['norm_mlp'][i], dhm, dh, f'rms_mlp_bwd_{i}')
        grads['norm_mlp'][i] = dg.reshape(d)
        started = emit_grads(i, 'mlp', gl, loss if i == depth - 1 else None)
        gl = {}
        if kind == 0:
            dy = _mm(dh, wl['w_out'], 'nt', f'sc_dy_{i}', dep=started)
            gl['w_out'] = _mm(s['y'], dh, 'tn', f'sc_dout_{i}', out_dtypes=(BF16,))
            dz, dwc = _sc_bwd(dy, s['z'], wl['small'], f'sc_conv_bwd_{i}')
            gl['small'] = dwc
            gl['w_in'] = _mm(s['hn'], dz, 'tn', f'sc_din_{i}', out_dtypes=(BF16,))
            started = emit_grads(i, 'mixer', gl)
            dhn = _mm(dz, wl['w_in'], 'nt', f'sc_dhn_{i}', dep=started)
        elif kind == 1:
            do, delta = _mm(dh, wl['w_out'], 'nt', f'attn_do_{i}', out_dtypes=(BF16, F32), extras=(s['o'],),
                            epi=_delta_epilogue, dep=started)
            gl['w_out'] = _mm(s['o'], dh, 'tn', f'attn_dwo_{i}', out_dtypes=(BF16,))
            rows_in = {1: (do, s['lse'], delta)}
            for dil in dils:
                if dil not in rows_in:
                    rows_in[dil] = _dilate_many([do, s['lse'], delta], dil, (1, 3, 3), (BF16, F32, F32),
                                                f'attn_dilate_{i}_d{dil}')
            dqs, dks, dvs = zip(*[_attn_bwd(*s['views'][g], *rows_in[dil], f'attn_bwd_{i}_g{g}')
                                  for g, dil in enumerate(dils)])
            dqkv = _rope_bwd(dqs, dks, dvs, pos, invf, dils, f'rope_bwd_{i}')
            gl['w_in'] = _mm(s['hn'], dqkv, 'tn', f'attn_dqkv_{i}', out_dtypes=(BF16,), out_stacked=True)
            started = emit_grads(i, 'mixer', gl)
            dhn = _mm(dqkv, wl['w_in'], 'nt', f'attn_dhn_{i}', dep=started)
        else:
            dy = _mm(dh, wl['w_out'], 'nt', f'lru_dy_{i}', dep=started)
            gl['w_out'] = _mm(s['y'], dh, 'tn', f'lru_dout_{i}', out_dtypes=(BF16,))
            sm = wl['small']
            dz, dwa, dwx, dvec = _lru_bwd(dy, s['z'], s['xr'], s['hs'], sm[0:4], w['lru_w_a'][j], sm[5:6],
                                          w['lru_w_x'][j], sm[6:7], sm[7:8], f'lru_bwd_{i}')
            gl['gates'], gl['small'] = (dwa, dwx), dvec
            gl['w_in'] = _mm(s['hn'], dz, 'tn', f'lru_din_{i}', out_dtypes=(BF16,))
            started = emit_grads(i, 'mixer', gl)
            dhn = _mm(dz, wl['w_in'], 'nt', f'lru_dhn_{i}', dep=started)
        dh, dg = _rms_bwd(s['h0'], w['norm_mix'][i], dhn, dh, f'rms_mix_bwd_{i}')
        grads['norm_mix'][i] = dg.reshape(d)
        started = None
    return loss, dh, grads


_MESH = pl.DeviceIdType.MESH
_ANY = pl.BlockSpec(memory_space=pl.ANY)


def _block_view(ref, kind, idx):
    if kind == 'stack':
        return ref.at[idx]
    r = ref.shape[0] // N_DEV
    return ref.at[pl.ds(idx * r, r)]


def _gather_many(arrs, kinds, name, after=None):
    n = len(arrs)
    after = [] if after is None else [after]
    out_shapes = [SDS((N_DEV,) + a.shape if kd == 'stack' else (N_DEV * a.shape[0],) + a.shape[1:], a.dtype)
                  for a, kd in zip(arrs, kinds)]

    def body(*refs):
        x_refs, out_refs = refs[:n], refs[n + len(after):2 * n + len(after)]
        send_sems, recv_sems, local_sems = refs[2 * n + len(after):]
        x, y, c = lax.axis_index('x'), lax.axis_index('y'), lax.axis_index('c')
        me, sibling = (x, y, c), (x, y, 1 - c)
        chips = [(1 - x, y), (x, 1 - y), (1 - x, 1 - y)]

        def slab(t, px, py, pc):
            return _block_view(out_refs[t], kinds[t], 4 * px + 2 * py + pc)

        def copy(t, k, block, to, src=None):
            return pltpu.make_async_remote_copy(
                src_ref=slab(t, *block) if src is None else src, dst_ref=slab(t, *block),
                send_sem=send_sems.at[7 * t + k], recv_sem=recv_sems.at[7 * t + k], device_id=to, device_id_type=_MESH)

        mine = [pltpu.make_async_copy(x_refs[t], slab(t, *me), local_sems.at[t]) for t in range(n)]
        for cp in mine:
            cp.start()
        first = [copy(t, 0, me, sibling, src=x_refs[t]) for t in range(n)]
        first += [copy(t, 1 + j, me, (*chip, c), src=x_refs[t]) for j, chip in enumerate(chips) for t in range(n)]
        for cp in first:
            cp.start()
        passed = []
        for j, chip in enumerate(chips):
            for t in range(n):
                copy(t, 1 + j, (*chip, c), me).wait_recv()
                passed.append(copy(t, 4 + j, (*chip, c), sibling))
                passed[-1].start()
        for t in range(n):
            copy(t, 0, sibling, me).wait_recv()
            for j, chip in enumerate(chips):
                copy(t, 4 + j, (*chip, 1 - c), me).wait_recv()
        for cp in first + passed:
            cp.wait_send()
        for cp in mine:
            cp.wait()

    return pl.pallas_call(
        body, out_shape=out_shapes, in_specs=[_ANY] * (n + len(after)), out_specs=[_ANY] * n,
        scratch_shapes=[pltpu.SemaphoreType.DMA((7 * n,)), pltpu.SemaphoreType.DMA((7 * n,)), pltpu.SemaphoreType.DMA((n,))],
        name=name)(*arrs, *after)


_HBM = pl.BlockSpec(memory_space=pltpu.HBM)
_SEM = pl.BlockSpec(memory_space=pltpu.SEMAPHORE)
_EFFECT = pltpu.SideEffectType.DATAFLOW_SIDE_EFFECTING


def _direct_copies(mode, kinds, src_refs, land_refs, send_sems, recv_sems):
    x, y, c = lax.axis_index('x'), lax.axis_index('y'), lax.axis_index('c')
    my_idx = 4 * x + 2 * y + c
    copies = []
    for k in range(1, N_DEV):
        px, py, pc = (1 - x if k & 4 else x, 1 - y if k & 2 else y, 1 - c if k & 1 else c)
        for t, kd in enumerate(kinds):
            if mode == 'gather':
                src, dst = src_refs[t], _block_view(land_refs[t], kd, my_idx)
            else:
                src, dst = _block_view(src_refs[t], kd, 4 * px + 2 * py + pc), land_refs[t].at[my_idx]
            copies.append(pltpu.make_async_remote_copy(
                src_ref=src, dst_ref=dst, send_sem=send_sems.at[7 * t + k - 1], recv_sem=recv_sems.at[7 * t + k - 1],
                device_id=(px, py, pc), device_id_type=_MESH))
    return copies


def _own_part(mode, kind, src, land):
    idx = 4 * lax.axis_index('x') + 2 * lax.axis_index('y') + lax.axis_index('c')
    zeros = (0,) * (src.ndim - 1)
    if mode == 'gather':
        part = src
    elif kind == 'stack':
        part = lax.dynamic_index_in_dim(src, idx, 0, keepdims=False)
    else:
        r = src.shape[0] // N_DEV
        part = lax.dynamic_slice_in_dim(src, idx * r, r, 0)
    if mode == 'gather' and kind == 'rows':
        return lax.dynamic_update_slice(land, part, (idx * part.shape[0],) + zeros)
    return lax.dynamic_update_slice(land, part[None], (idx,) + (0,) * part.ndim)


def _send_start(mode, srcs, kinds, name, after=None):
    n = len(srcs)
    after = [] if after is None else [after]
    lands = []
    for a, kd in zip(srcs, kinds):
        if mode == 'gather':
            shape = (N_DEV,) + a.shape if kd == 'stack' else (N_DEV * a.shape[0],) + a.shape[1:]
        else:
            shape = a.shape if kd == 'stack' else (N_DEV, a.shape[0] // N_DEV) + a.shape[1:]
        lands.append(_own_part(mode, kd, a, lax.empty(shape, a.dtype)))

    def body(*refs):
        src_refs, land_refs = refs[:n], refs[n:2 * n]
        send_sems, recv_sems = refs[2 * n + len(after):2 * n + len(after) + 2]
        token = refs[-1]
        for cp in _direct_copies(mode, kinds, src_refs, land_refs, send_sems, recv_sems):
            cp.start()
        token[...] = jnp.zeros_like(token)

    outs = pl.pallas_call(
        body, name=name,
        out_shape=(pltpu.SemaphoreType.DMA((7 * n,)), pltpu.SemaphoreType.DMA((7 * n,)),
                   *[pltpu.HBM(a.shape, a.dtype) for a in srcs + lands], SDS((SUB, 128), F32)),
        in_specs=[_HBM] * (2 * n) + [_ANY] * len(after),
        out_specs=(_SEM, _SEM, *[_HBM] * (2 * n), pl.BlockSpec(memory_space=pltpu.VMEM)),
        input_output_aliases={i: 2 + i for i in range(2 * n)},
        compiler_params=pltpu.CompilerParams(has_side_effects=_EFFECT),
    )(*[pltpu.with_memory_space_constraint(a, pltpu.HBM) for a in srcs + lands], *after)
    return (outs[0], outs[1], list(outs[2:2 + 2 * n])), outs[-1]


def _send_wait(mode, flight, kinds, after, name):
    send, recv, bufs = flight
    n = len(kinds)

    def body(*refs):
        src_refs, land_refs, (send_sems, recv_sems) = refs[:n], refs[n:2 * n], refs[2 * n:2 * n + 2]
        copies = _direct_copies(mode, kinds, src_refs, land_refs, send_sems, recv_sems)
        for cp in copies:
            cp.wait_send()
        for cp in copies:
            cp.wait_recv()

    outs = pl.pallas_call(
        body, name=name, out_shape=[pltpu.HBM(a.shape, a.dtype) for a in bufs],
        in_specs=[_HBM] * (2 * n) + [_SEM, _SEM, _ANY], out_specs=[_HBM] * (2 * n),
        input_output_aliases={i: i for i in range(2 * n)},
        compiler_params=pltpu.CompilerParams(has_side_effects=_EFFECT),
    )(*bufs, send, recv, after)
    return list(outs[n:])


ADAMW_BLOCK_ELEMS = 128 * 1024


def _adamw_sum(wgt, parts, m, v, name):
    nl, r, c = wgt.shape
    assert len(parts) == nl and all(q.shape == (N_DEV, r, c) for q in parts), (name, wgt.shape, [q.shape for q in parts])
    tr = next((t for t in range(min(r, 512), 0, -16) if r % t == 0 and t * c <= ADAMW_BLOCK_ELEMS and t % 16 == 0), r)
    c1 = 1.0 - ADAM_B1 ** ADAM_STEP
    c2 = 1.0 - ADAM_B2 ** ADAM_STEP

    def body(w_ref, m_ref, v_ref, *rest):
        part_refs, (g_ref, d_ref, mo_ref, vo_ref) = rest[:nl], rest[nl:]
        for q in range(nl):
            @pl.when(pl.program_id(0) == q)
            def _(q=q):
                gv = part_refs[q][0].astype(F32)
                for s in range(1, N_DEV):
                    gv = gv + part_refs[q][s].astype(F32)
                mn = ADAM_B1 * m_ref[...] + (1.0 - ADAM_B1) * gv
                vn = ADAM_B2 * v_ref[...] + (1.0 - ADAM_B2) * (gv * gv)
                g_ref[...] = gv
                d_ref[...] = -ADAM_LR * ((mn / c1) / (jnp.sqrt(vn / c2) + ADAM_EPS) + ADAM_WD * w_ref[...])
                mo_ref[...] = mn
                vo_ref[...] = vn

    spec = pl.BlockSpec((None, tr, c), lambda l, i: (l, i, 0))
    part_specs = [pl.BlockSpec((N_DEV, tr, c), lambda l, i, q=q: (0, jnp.where(l == q, i, 0), 0)) for q in range(nl)]
    return pl.pallas_call(body, grid=(nl, r // tr), in_specs=[spec] * 3 + part_specs, out_specs=[spec] * 4,
                          out_shape=[SDS((nl, r, c), F32)] * 4, compiler_params=_params('arbitrary', 'arbitrary'),
                          name=name)(wgt, m, v, *parts)


MIXER_WEIGHTS = {0: ('sc_w_in', 'sc_w_out'), 1: ('attn_w_qkv', 'attn_w_o'), 2: ('lru_w_in', 'lru_w_out')}
STACKED_OPERANDS = ('attn_w_qkv', 'mlp_w_up')
LRU_SMALL = ('lru_conv_w', 'lru_conv_b', 'lru_b_a', 'lru_b_x', 'lru_lambda')


def _layer_items(i):
    w_in, w_out = MIXER_WEIGHTS[i % N_MIXERS]
    j = i // N_MIXERS
    return [('w_in', w_in, j), ('w_out', w_out, j), ('mlp_up', 'mlp_w_up', i), ('mlp_down', 'mlp_w_down', i),
            ('ple_gate', 'ple_w_gate', i), ('ple_proj', 'ple_w_proj', i)]


def _cols_to_full(stacked):
    return jnp.moveaxis(stacked, 0, 1).reshape(stacked.shape[1], -1)


def _full_to_cols(full):
    k, n = full.shape
    return jnp.moveaxis(full.reshape(k, N_DEV, n // N_DEV), 1, 0)


def _pad_to(a, rows):
    return jnp.pad(a, ((0, rows - a.shape[0]), (0, 0)))


def _small_block(src, i):
    kind, j = i % N_MIXERS, i // N_MIXERS
    if kind == 0:
        return _pad_to(src['sc_w_conv'][j], SUB)
    if kind == 2:
        return jnp.concatenate([src[n][j].reshape(-1, src[n].shape[-1]) for n in LRU_SMALL], axis=0)
    return None


def kernel(x, p, positions, norm_mix, norm_mlp, norm_ple, norm_final, sc_w_in, sc_w_conv, sc_w_out, attn_w_qkv, attn_w_o, lru_w_in, lru_conv_w, lru_conv_b, lru_w_a, lru_b_a, lru_w_x, lru_b_x, lru_lambda, lru_w_out, mlp_w_up, mlp_w_down, ple_w_gate, ple_w_proj, loss_target, m_norm_mix, m_norm_mlp, m_norm_ple, m_norm_final, m_sc_w_in, m_sc_w_conv, m_sc_w_out, m_attn_w_qkv, m_attn_w_o, m_lru_w_in, m_lru_conv_w, m_lru_conv_b, m_lru_w_a, m_lru_b_a, m_lru_w_x, m_lru_b_x, m_lru_lambda, m_lru_w_out, m_mlp_w_up, m_mlp_w_down, m_ple_w_gate, m_ple_w_proj, v_norm_mix, v_norm_mlp, v_norm_ple, v_norm_final, v_sc_w_in, v_sc_w_conv, v_sc_w_out, v_attn_w_qkv, v_attn_w_o, v_lru_w_in, v_lru_conv_w, v_lru_conv_b, v_lru_w_a, v_lru_b_a, v_lru_w_x, v_lru_b_x, v_lru_lambda, v_lru_w_out, v_mlp_w_up, v_mlp_w_down, v_ple_w_gate, v_ple_w_proj):
    loc = dict(locals())
    shards = {n: loc[n] for n in WEIGHTS}
    moms = {n: loc['m_' + n] for n in WEIGHTS}
    vels = {n: loc['v_' + n] for n in WEIGHTS}

    depth, t, d = p.shape[0], x.shape[1], x.shape[2]

    def comm_kind(name):
        return 'stack' if SHARD_AXIS[name] == 2 else 'rows'

    part_keys = {'mlp': ('mlp_up', 'mlp_down', 'ple_gate', 'ple_proj'), 'mixer': ('w_in', 'w_out')}
    halves = [(i, part) for i in range(depth) for part in ('mixer', 'mlp')]

    def half_shards(i, part):
        items = [it for it in _layer_items(i) if it[0] in part_keys[part]]
        arrs = [shards[n][idx].astype(BF16) for _, n, idx in items]
        kinds = [comm_kind(n) for _, n, _ in items]
        small = _small_block(shards, i) if part == 'mixer' else None
        if small is not None:
            arrs.append(small)
            kinds.append('stack')
        return items, arrs, kinds

    def half_weights(i, items, kinds, outs):
        wl = {key: (_cols_to_full(o) if kd == 'stack' and n not in STACKED_OPERANDS else o)
              for (key, n, _), kd, o in zip(items, kinds, outs)}
        if len(outs) > len(items):
            wl['small'] = _cols_to_full(outs[-1])[:shards['sc_w_conv'].shape[1] if i % N_MIXERS == 0 else SUB]
        return wl

    first = [half_shards(0, part) for part in ('mixer', 'mlp')]
    outs0 = _gather_many(first[0][1] + first[1][1], first[0][2] + first[1][2], 'gather_weights_0')
    weights0 = {**half_weights(0, first[0][0], first[0][2], outs0[:len(first[0][1])]),
                **half_weights(0, first[1][0], first[1][2], outs0[len(first[0][1]):])}
    pending = {}

    def start_gather(pos, after):
        if pos >= len(halves):
            return None
        i, part = halves[pos]
        items, arrs, kinds = half_shards(i, part)
        flight, token = _send_start('gather', arrs, kinds, f'gather_weights_start_{part}_{i}', after=after)
        pending[pos] = (items, kinds, flight)
        return token

    first_token = start_gather(2, outs0[0])
    second_token = start_gather(3, first_token)

    def weights_for_layer(i, part, h):
        pos = halves.index((i, part))
        if pos == 0:
            return weights0, second_token
        if pos == 1:
            return {}, None
        items, kinds, flight = pending.pop(pos)
        outs = _send_wait('gather', flight, kinds, h, f'gather_weights_wait_{part}_{i}')
        return half_weights(i, items, kinds, outs), start_gather(pos + 2, outs[0])

    exchanges, gate_gathers, total_loss = {}, {}, []

    def gate_block(src, j):
        return jnp.concatenate([src[n][j].reshape(-1, LRU_BLOCK) for n in ('lru_w_a', 'lru_w_x')], axis=0)

    def emit_grads(i, part, gl, loss=None):
        after = None
        if loss is not None:
            total_loss.append(lax.psum(loss[0, 0], ('x', 'y', 'c')))
            after = jnp.full((SUB, 128), total_loss[0], F32)
        if 'gates' in gl:
            blk = gate_block({'lru_w_a': [gl['gates'][0]], 'lru_w_x': [gl['gates'][1]]}, 0)
            gate_gathers[i] = _send_start('gather', [blk], ['stack'], f'gather_gate_grads_start_{i}')[0]
        items = [it for it in _layer_items(i) if it[0] in part_keys[part]]
        kinds = [comm_kind(n) for _, n, _ in items]
        arrs = [_full_to_cols(gl[key]) if kd == 'stack' and gl[key].ndim == 2 else gl[key]
                for (key, _, _), kd in zip(items, kinds)]
        if part == 'mixer' and i % N_MIXERS == 0:
            arrs.append(_full_to_cols(_pad_to(gl['small'], SUB)))
        elif part == 'mixer' and i % N_MIXERS == 2:
            dv = gl['small']
            arrs.append(_full_to_cols(jnp.concatenate([dv[4:8], dv[3:4], dv[0:1], dv[1:2], dv[2:3]], axis=0)))
        kinds += ['stack'] * (len(arrs) - len(kinds))
        flight, token = _send_start('exchange', arrs, kinds, f'exchange_grads_start_{part}_{i}', after=after)
        exchanges[(i, part)] = (items, kinds, flight)
        return token

    rep = {n: shards[n] for n in ('norm_mix', 'norm_mlp', 'norm_ple', 'norm_final')}
    rep['lru_w_a'], rep['lru_w_x'] = shards['lru_w_a'].astype(BF16), shards['lru_w_x'].astype(BF16)
    loss, grad_x, rgrads = _local_step(x.reshape(t, d), p.reshape(depth, t, p.shape[3]), positions.reshape(t, 1),
                                       loss_target.reshape(t, d), rep, weights_for_layer, emit_grads)

    received, res = {}, {}

    def finish_exchange(key, after):
        items, kinds, flight = exchanges[key]
        outs = _send_wait('exchange', flight, kinds, after, f'exchange_grads_wait_{key[1]}_{key[0]}')
        for (_, n, idx), o in zip(items, outs):
            received[(n, idx)] = o
        if len(outs) > len(items):
            received[('small', key[0])] = outs[-1]

    def big_adamw(names):
        for n in names:
            res[n] = _adamw_sum(shards[n], [received[(n, l)] for l in range(shards[n].shape[0])], moms[n], vels[n],
                                f'adamw_{n}')

    last = (0, 'mixer')
    for key in exchanges:
        if key != last:
            finish_exchange(key, grad_x)
    big = [n for n in WEIGHTS if SHARD_AXIS[n] is not None and shards[n].ndim == 3 and n not in ('sc_w_conv', 'lru_conv_w')]
    late = [n for n in big if n in MIXER_WEIGHTS[0]]
    big_adamw([n for n in big if n not in late])
    finish_exchange(last, jnp.full((SUB, 128), sum(r[0].reshape(-1)[0] for r in res.values()), F32))
    big_adamw(late)

    def small_adamw(layers, name):
        w_, m_, v_ = (jnp.stack([_small_block(src, i) for i in layers]) for src in (shards, moms, vels))
        return _adamw_sum(w_, [received[('small', i)] for i in layers], m_, v_, name)

    sc = small_adamw([i for i in range(depth) if i % N_MIXERS == 0], 'adamw_sc_w_conv')
    res['sc_w_conv'] = tuple(o[:, :shards['sc_w_conv'].shape[1]] for o in sc)
    lru = small_adamw([i for i in range(depth) if i % N_MIXERS == 2], 'adamw_lru_small')
    row = 0
    for n in LRU_SMALL:
        k = shards[n].size // shards[n].shape[0] // shards[n].shape[-1]
        res[n] = tuple(o[:, row:row + k].reshape(shards[n].shape) for o in lru)
        row += k

    gate_layers = sorted(gate_gathers)
    gate_parts = [_send_wait('gather', gate_gathers[i], ['stack'], grad_x, f'gather_gate_grads_wait_{i}')[0]
                  for i in gate_layers]
    gate_w, gate_m, gate_v = (jnp.stack([gate_block(src, j) for j in range(len(gate_layers))])
                              for src in (shards, moms, vels))
    gates = _adamw_sum(gate_w, gate_parts, gate_m, gate_v, 'adamw_lru_gates')
    half = gates[0].shape[1] // 2
    res['lru_w_a'] = tuple(o[:, :half].reshape(shards['lru_w_a'].shape) for o in gates)
    res['lru_w_x'] = tuple(o[:, half:].reshape(shards['lru_w_x'].shape) for o in gates)

    norm_names = ('norm_mix', 'norm_mlp', 'norm_ple', 'norm_final')

    def norm_block(src):
        cat = jnp.concatenate([src[n].reshape(-1, d) for n in norm_names], axis=0)
        return _pad_to(cat, -(-cat.shape[0] // HALO) * HALO)

    rfull = {n: (rgrads[n] if n == 'norm_final' else jnp.stack(rgrads[n], axis=0)) for n in norm_names}
    updated = jnp.full((SUB, 128), sum(r[0].reshape(-1)[0] for r in res.values()), F32)
    parts_norm, = _gather_many([norm_block(rfull)], ['stack'], 'gather_norm_grads', after=updated)
    norms = _adamw_sum(norm_block(shards)[None], [parts_norm], norm_block(moms)[None], norm_block(vels)[None],
                       'adamw_norms')
    row = 0
    for n in norm_names:
        k = shards[n].size // d
        res[n] = tuple(o[0, row:row + k].reshape(shards[n].shape) for o in norms)
        row += k

    return (total_loss[0], grad_x.reshape(x.shape), *[res[n][0] for n in WEIGHTS], *[res[n][1] for n in WEIGHTS],
            *[res[n][2] for n in WEIGHTS], *[res[n][3] for n in WEIGHTS])
```

```python
import functools
import math

import jax
import jax.numpy as jnp
from jax import lax
from jax.experimental import pallas as pl
from jax.experimental.pallas import tpu as pltpu

F32 = jnp.float32
BF16 = jnp.bfloat16
SDS = jax.ShapeDtypeStruct

N_DEV = 8
RMS_EPS = 1e-6
N_MIXERS = 3
HEAD_DIM = 128
DILATED_PATTERNS = ((128, 1), (512, 4), (2048, 16))
ATTN_BLOCK = 128
ROPE_THETA = 500000.0
ROPE_DIM = HEAD_DIM // 4
LRU_BLOCK = 128
LRU_C = 8.0
ADAM_LR, ADAM_B1, ADAM_B2, ADAM_EPS, ADAM_WD, ADAM_STEP = 0.001, 0.9, 0.999, 1e-08, 0.01, 10

HALO = 16
SUB = 8
VMEM_LIMIT = 56 * 1024 * 1024
NEG = -1e30

SHARD_AXIS = {
    'norm_mix': None, 'norm_mlp': None, 'norm_ple': None, 'norm_final': None,
    'sc_w_in': 2, 'sc_w_conv': 2, 'sc_w_out': 1, 'attn_w_qkv': 2, 'attn_w_o': 1,
    'lru_w_in': 2, 'lru_conv_w': 2, 'lru_conv_b': 1, 'lru_w_a': None, 'lru_b_a': 1,
    'lru_w_x': None, 'lru_b_x': 1, 'lru_lambda': 1, 'lru_w_out': 1,
    'mlp_w_up': 2, 'mlp_w_down': 1, 'ple_w_gate': 1, 'ple_w_proj': 2,
}
WEIGHTS = list(SHARD_AXIS)


def _params(*sem):
    return pltpu.CompilerParams(dimension_semantics=sem or None, vmem_limit_bytes=VMEM_LIMIT)


def _row_tile(t, pref=256):
    tr = min(t, pref)
    assert t % tr == 0 and tr % HALO == 0
    return tr


def _row(tr, c, col=0):
    return pl.BlockSpec((tr, c), lambda i, col=col: (i, col))


def _full(shape):
    return pl.BlockSpec(shape, lambda *_: (0,) * len(shape))


def _sigmoid(x):
    return 1.0 / (1.0 + jnp.exp(-x))


def _expm1(x):
    taylor = x * (1.0 + x * (0.5 + x * (1.0 / 6.0 + x * (1.0 / 24.0 + x * (1.0 / 120.0)))))
    return jnp.where(jnp.abs(x) < 0.1, taylor, jnp.exp(x) - 1.0)


def _softplus(x):
    z = jnp.exp(-jnp.abs(x))
    log1p = jnp.where(z < 0.01, z * (1.0 - z * (0.5 - z * (1.0 / 3.0 - z * 0.25))), jnp.log(1.0 + z))
    return jnp.maximum(x, 0.0) + log1p


_GELU_K = math.sqrt(2.0 / math.pi)


def _gelu_and_grad(x):
    inner = _GELU_K * (x + 0.044715 * x * x * x)
    th = jnp.tanh(inner)
    g = 0.5 * x * (1.0 + th)
    dg = 0.5 * (1.0 + th) + 0.5 * x * (1.0 - th * th) * _GELU_K * (1.0 + 3.0 * 0.044715 * x * x)
    return g, dg


def _shift_down(x, k, prev):
    row = lax.broadcasted_iota(jnp.int32, (SUB, x.shape[1]), 0)
    xr = pltpu.roll(x, k, 0)
    top = jnp.where(row < k, pltpu.roll(prev, k, 0), xr[0:SUB])
    return jnp.concatenate([top, xr[SUB:]], axis=0)


def _shift_up(x, k, nxt):
    r = x.shape[0]
    row = lax.broadcasted_iota(jnp.int32, (SUB, x.shape[1]), 0)
    xr = pltpu.roll(x, r - k, 0)
    bot = jnp.where(row >= SUB - k, pltpu.roll(nxt, SUB - k, 0), xr[r - SUB:r])
    return jnp.concatenate([xr[:r - SUB], bot], axis=0)


_DIMS = {'nn': (((1,), (0,)), ((), ())), 'nt': (((1,), (1,)), ((), ())), 'tn': (((0,), (0,)), ((), ()))}


MM_VMEM_BUDGET = 50 * 1024 * 1024
MM_MIN_TK = 1024
MM_MIN_TM = 1024


def _tile_options(dim):
    return [c for c in range(dim, 127, -128) if dim % c == 0] or [dim]


def _choose_tiles(m, n, k, n_span, k_span, a_size, b_size, mn_size, a_temp):
    best = None
    for tm in _tile_options(m):
        for tn in _tile_options(n_span):
            for tk in _tile_options(k_span):
                nk = k // tk
                need = (2 * (tm * tk * a_size + tk * tn * b_size + tm * tn * mn_size) + tm * tn * 4 * (1 + (nk > 1))
                        + tm * tk * 4 * a_temp)
                score = (-min(tk, MM_MIN_TK), -min(tm, MM_MIN_TM), -tm * tn, nk, -min(tm, 2 * MM_MIN_TM), -tn)
                if need <= MM_VMEM_BUDGET and (best is None or score < best[0]):
                    best = (score, (tm, tn, tk))
    return best[1]


def _mm(a, b, dims, name, out_dtypes=(F32,), a_pro=None, extras=(), epi=None, out_stacked=False, dep=None):
    deps = [] if dep is None else [dep]
    stacked = b.ndim == 3
    b_rows, b_cols = (b.shape[1], N_DEV * b.shape[2]) if stacked else b.shape
    if dims == 'nn':
        (m, k), (k2, n) = a.shape, (b_rows, b_cols)
    elif dims == 'nt':
        (m, k), (n, k2) = a.shape, (b_rows, b_cols)
    else:
        (k, m), (k2, n) = a.shape, (b_rows, b_cols)
    assert k == k2, (name, a.shape, b.shape)
    assert not (stacked and dims == 'tn') and not (out_stacked and (extras or dims != 'tn'))
    tm, tn, tk = _choose_tiles(
        m, n, k, n // N_DEV if (out_stacked or (stacked and dims == 'nn')) else n,
        k // N_DEV if (stacked and dims == 'nt') else k, a.dtype.itemsize, b.dtype.itemsize,
        sum(e.dtype.itemsize for e in extras) + sum(jnp.dtype(dt).itemsize for dt in out_dtypes),
        a_pro is not None or a.dtype != BF16)
    assert m % tm == 0 and n % tn == 0 and k % tk == 0, (name, m, n, k)
    nk = k // tk
    a_spec = pl.BlockSpec((tk, tm), lambda i, j, kk: (kk, i)) if dims == 'tn' else pl.BlockSpec((tm, tk), lambda i, j, kk: (i, kk))
    if not stacked:
        b_spec = pl.BlockSpec((tn, tk), lambda i, j, kk: (j, kk)) if dims == 'nt' else pl.BlockSpec((tk, tn), lambda i, j, kk: (kk, j))
    elif dims == 'nn':
        per = b.shape[2] // tn
        b_spec = pl.BlockSpec((None, tk, tn), lambda i, j, kk: (j // per, kk, j % per))
    else:
        per = b.shape[2] // tk
        b_spec = pl.BlockSpec((None, tn, tk), lambda i, j, kk: (kk // per, j, kk % per))
    if out_stacked:
        per_o = n // N_DEV // tn
        o_spec = pl.BlockSpec((None, tm, tn), lambda i, j, kk: (j // per_o, i, j % per_o))
        o_shape = (N_DEV, m, n // N_DEV)
    else:
        o_spec = pl.BlockSpec((tm, tn), lambda i, j, kk: (i, j))
        o_shape = (m, n)
    n_ex, n_out = len(extras), len(out_dtypes)
    for e in extras:
        assert e.shape == (m, n), (name, e.shape)

    def body(a_ref, b_ref, *rest):
        rest = rest[len(deps):]
        ex_refs, out_refs = rest[:n_ex], rest[n_ex:n_ex + n_out]
        kk = pl.program_id(2)
        av = a_ref[...]
        if a_pro is not None:
            av = a_pro(av.astype(F32))
        part = lax.dot_general(av.astype(BF16), b_ref[...].astype(BF16), _DIMS[dims], preferred_element_type=F32)

        def finish(res):
            outs = (res,) if epi is None else epi(res, *[e[...] for e in ex_refs])
            for o_ref, o in zip(out_refs, outs):
                o_ref[...] = o.astype(o_ref.dtype)

        if nk == 1:
            finish(part)
        else:
            acc = rest[-1]

            @pl.when(kk == 0)
            def _():
                acc[...] = part

            @pl.when(kk > 0)
            def _():
                acc[...] += part

            @pl.when(kk == nk - 1)
            def _():
                finish(acc[...])

    out = pl.pallas_call(
        body, grid=(m // tm, n // tn, nk),
        in_specs=[a_spec, b_spec] + [_ANY] * len(deps) + [o_spec] * n_ex,
        out_specs=[o_spec] * n_out,
        out_shape=[SDS(o_shape, d) for d in out_dtypes],
        scratch_shapes=[] if nk == 1 else [pltpu.VMEM((tm, tn), F32)],
        compiler_params=_params('parallel', 'parallel', 'arbitrary'), name=name)(a, b, *deps, *extras)
    return out[0] if n_out == 1 else out


def _relu2(u):
    r = jnp.maximum(u, 0.0)
    return r * r


def _rms_fwd(h, g, name):
    t, d = h.shape
    tr = _row_tile(t)

    def body(h_ref, g_ref, o_ref):
        x = h_ref[...]
        r = lax.rsqrt(jnp.mean(x * x, axis=-1, keepdims=True) + RMS_EPS)
        o_ref[...] = (x * r * g_ref[...]).astype(o_ref.dtype)

    return pl.pallas_call(body, grid=(t // tr,), in_specs=[_row(tr, d), _full((1, d))], out_specs=_row(tr, d),
                          out_shape=SDS((t, d), BF16), compiler_params=_params('parallel'), name=name)(h, g.reshape(1, d))


def _rms_bwd(h, g, dhn, dres, name):
    t, d = h.shape
    tr = _row_tile(t)

    def body(h_ref, g_ref, dhn_ref, dres_ref, dh_ref, dg_ref):
        @pl.when(pl.program_id(0) == 0)
        def _():
            dg_ref[...] = jnp.zeros_like(dg_ref)

        x = h_ref[...]
        r = lax.rsqrt(jnp.mean(x * x, axis=-1, keepdims=True) + RMS_EPS)
        dy = dhn_ref[...].astype(F32)
        gy = dy * g_ref[...]
        dx = r * gy - x * (r * r * r) * jnp.mean(gy * x, axis=-1, keepdims=True)
        dh_ref[...] = dres_ref[...] + dx
        dg_ref[...] += jnp.sum(dy * (x * r), axis=0, keepdims=True)

    return pl.pallas_call(body, grid=(t // tr,),
                          in_specs=[_row(tr, d), _full((1, d)), _row(tr, d), _row(tr, d)],
                          out_specs=[_row(tr, d), _full((1, d))],
                          out_shape=[SDS((t, d), F32), SDS((1, d), F32)],
                          compiler_params=_params('arbitrary'), name=name)(h, g.reshape(1, d), dhn, dres)


def _head(h, g, target, name):
    t, d = h.shape
    tr = _row_tile(t)

    def body(h_ref, g_ref, t_ref, dh_ref, loss_ref, dg_ref):
        @pl.when(pl.program_id(0) == 0)
        def _():
            dg_ref[...] = jnp.zeros_like(dg_ref)
            loss_ref[...] = jnp.zeros_like(loss_ref)

        x = h_ref[...]
        gv = g_ref[...]
        r = lax.rsqrt(jnp.mean(x * x, axis=-1, keepdims=True) + RMS_EPS)
        xh = x * r
        e = xh * gv - t_ref[...]
        per_tok = jnp.mean(e * e, axis=-1, keepdims=True)
        loss_ref[...] += jnp.broadcast_to(0.5 * jnp.sum(per_tok, axis=0, keepdims=True), loss_ref.shape)
        dy = e * (1.0 / d)
        gy = dy * gv
        dh_ref[...] = r * gy - x * (r * r * r) * jnp.mean(gy * x, axis=-1, keepdims=True)
        dg_ref[...] += jnp.sum(dy * xh, axis=0, keepdims=True)

    return pl.pallas_call(body, grid=(t // tr,),
                          in_specs=[_row(tr, d), _full((1, d)), _row(tr, d)],
                          out_specs=[_row(tr, d), _full((1, 128)), _full((1, d))],
                          out_shape=[SDS((t, d), F32), SDS((1, 128), F32), SDS((1, d), F32)],
                          compiler_params=_params('arbitrary'), name=name)(h, g.reshape(1, d), target)


def _ple_bwd_gate(dh3, gate, pp, name):
    t, d = dh3.shape
    tr = _row_tile(t)

    def body(dh_ref, g_ref, pp_ref, dpp_ref, dgl_ref):
        dh = dh_ref[...]
        gt = g_ref[...]
        dpp_ref[...] = (dh * gt).astype(dpp_ref.dtype)
        dgl_ref[...] = (dh * pp_ref[...] * gt * (1.0 - gt)).astype(dgl_ref.dtype)

    return pl.pallas_call(body, grid=(t // tr,), in_specs=[_row(tr, d)] * 3, out_specs=[_row(tr, d)] * 2,
                          out_shape=[SDS((t, d), BF16), SDS((t, d), BF16)],
                          compiler_params=_params('parallel'), name=name)(dh3, gate, pp)


def _halo_prev(tr, c, col=0):
    return pl.BlockSpec((HALO, c), lambda i, col=col: (jnp.maximum(i * (tr // HALO) - 1, 0), col))


def _halo_next(tr, c, t, col=0):
    return pl.BlockSpec((HALO, c), lambda i, col=col: (jnp.minimum((i + 1) * (tr // HALO), t // HALO - 1), col))


def _sc_fwd(z, w, name):
    t, c3 = z.shape
    c = c3 // 3
    tr = _row_tile(t)

    def body(z_ref, zp_ref, w_ref, y_ref):
        i = pl.program_id(0)
        zz = z_ref[...]
        gb, cx = zz[:, :c], zz[:, c:2 * c] * zz[:, 2 * c:]
        zp = zp_ref[SUB:HALO, :]
        cxp = jnp.where(i > 0, zp[:, c:2 * c] * zp[:, 2 * c:], 0.0)
        wv = w_ref[...]
        conv = wv[2:3] * cx + wv[1:2] * _shift_down(cx, 1, cxp) + wv[0:1] * _shift_down(cx, 2, cxp)
        y_ref[...] = (gb * conv).astype(y_ref.dtype)

    return pl.pallas_call(body, grid=(t // tr,),
                          in_specs=[_row(tr, c3), _halo_prev(tr, c3), _full((3, c))],
                          out_specs=_row(tr, c), out_shape=SDS((t, c), BF16),
                          compiler_params=_params('parallel'), name=name)(z, z, w)


def _sc_bwd(dy, z, w, name):
    t, c3 = z.shape
    c = c3 // 3
    tr = _row_tile(t)
    nt = t // tr

    def body(dy_ref, dyn_ref, z_ref, zp_ref, zn_ref, w_ref, dz_ref, dw_ref):
        i = pl.program_id(0)

        @pl.when(i == 0)
        def _():
            dw_ref[...] = jnp.zeros_like(dw_ref)

        zz = z_ref[...]
        gb, gc, xi = zz[:, :c], zz[:, c:2 * c], zz[:, 2 * c:]
        cx = gc * xi
        zp = zp_ref[SUB:HALO, :]
        cxp = jnp.where(i > 0, zp[:, c:2 * c] * zp[:, 2 * c:], 0.0)
        wv = w_ref[...]
        cx1, cx2 = _shift_down(cx, 1, cxp), _shift_down(cx, 2, cxp)
        conv = wv[2:3] * cx + wv[1:2] * cx1 + wv[0:1] * cx2
        dyv = dy_ref[...]
        dconv = dyv * gb
        dcn = jnp.where(i < nt - 1, dyn_ref[0:SUB, :] * zn_ref[0:SUB, :c], 0.0)
        dcx = wv[2:3] * dconv + wv[1:2] * _shift_up(dconv, 1, dcn) + wv[0:1] * _shift_up(dconv, 2, dcn)
        dz_ref[:, :c] = (dyv * conv).astype(dz_ref.dtype)
        dz_ref[:, c:2 * c] = (dcx * xi).astype(dz_ref.dtype)
        dz_ref[:, 2 * c:] = (dcx * gc).astype(dz_ref.dtype)
        dw_ref[...] += jnp.concatenate([jnp.sum(dconv * cx2, axis=0, keepdims=True),
                                        jnp.sum(dconv * cx1, axis=0, keepdims=True),
                                        jnp.sum(dconv * cx, axis=0, keepdims=True)], axis=0)

    return pl.pallas_call(body, grid=(nt,),
                          in_specs=[_row(tr, c), _halo_next(tr, c, t), _row(tr, c3), _halo_prev(tr, c3),
                                    _halo_next(tr, c3, t), _full((3, c))],
                          out_specs=[_row(tr, c3), _full((3, c))],
                          out_shape=[SDS((t, c3), BF16), SDS((3, c), F32)],
                          compiler_params=_params('arbitrary'), name=name)(dy, dy, z, z, z, w)


def _perm(tr, dil, inverse=False):
    n = tr // dil
    a = lax.broadcasted_iota(jnp.int32, (tr, tr), 1 if inverse else 0)
    b = lax.broadcasted_iota(jnp.int32, (tr, tr), 0 if inverse else 1)
    return (b == (a % n) * dil + a // n).astype(BF16)


def _permute(pm, x, terms):
    if x.dtype == BF16:
        return jnp.dot(pm, x, preferred_element_type=F32)
    acc = None
    for _ in range(terms):
        part = x.astype(BF16)
        y = jnp.dot(pm, part, preferred_element_type=F32)
        acc = y if acc is None else acc + y
        x = x - part.astype(F32)
    return acc


def _store_dilated(o_ref, y, dil, d):
    n = y.shape[0] // dil
    for rho in range(dil):
        o_ref[:, rho * d:(rho + 1) * d] = y[rho * n:(rho + 1) * n].astype(o_ref.dtype)


def _load_dilated(ref, dil, d):
    return jnp.concatenate([ref[:, rho * d:(rho + 1) * d] for rho in range(dil)], axis=0) if dil > 1 else ref[...]


def _rope_heads(x, lane, cos, sin):
    return jnp.concatenate([_rope_apply(x[:, s:s + HEAD_DIM], lane, cos, sin)
                            for s in range(0, x.shape[1], HEAD_DIM)], axis=1)


def _rope_tables(pos, invf, sign):
    lane = lax.broadcasted_iota(jnp.int32, (pos.shape[0], HEAD_DIM), 1)
    ang = pos.astype(F32) * invf
    half = ROPE_DIM // 2
    cos = jnp.where(lane < ROPE_DIM, jnp.cos(ang), 1.0)
    sin = jnp.sin(ang) * sign
    sin = jnp.where(lane < half, -sin, jnp.where(lane < ROPE_DIM, sin, 0.0))
    return lane, cos, sin


def _rope_apply(x, lane, cos, sin):
    half = ROPE_DIM // 2
    xs = jnp.where(lane < half, pltpu.roll(x, HEAD_DIM - half, 1), pltpu.roll(x, half, 1))
    return x * cos + xs * sin


def _dilated_spec(tr, dil, d):
    return pl.BlockSpec((tr // dil, dil * d), lambda i: (i, 0))


def _rope_fwd(qkv, pos, invf, dils, name):
    t, w3 = qkv.shape
    w, ng = w3 // 3, len(dils)
    d = w // ng
    tr = _row_tile(t)

    def body(q_ref, k_ref, v_ref, pos_ref, invf_ref, *out_refs):
        lane, cos, sin = _rope_tables(pos_ref[...], invf_ref[...], 1.0)
        for g, dil in enumerate(dils):
            cs = slice(g * d, (g + 1) * d)
            vals = [_rope_heads(q_ref[:, cs], lane, cos, sin).astype(BF16),
                    _rope_heads(k_ref[:, cs], lane, cos, sin).astype(BF16), v_ref[:, cs].astype(BF16)]
            if dil > 1:
                pm = _perm(tr, dil)
                vals = [_permute(pm, a, 1) for a in vals]
            for o_ref, a in zip(out_refs[g::ng], vals):
                _store_dilated(o_ref, a, dil, d)

    outs = pl.pallas_call(body, grid=(t // tr,),
                          in_specs=[_row(tr, w, 0), _row(tr, w, 1), _row(tr, w, 2), _row(tr, 1), _full((1, HEAD_DIM))],
                          out_specs=[_dilated_spec(tr, dil, d) for dil in dils] * 3,
                          out_shape=[SDS((t // dil, dil * d), BF16) for dil in dils] * 3,
                          compiler_params=_params('parallel'), name=name)(qkv, qkv, qkv, pos, invf)
    return outs[:ng], outs[ng:2 * ng], outs[2 * ng:]


def _rope_bwd(dqs, dks, dvs, pos, invf, dils, name):
    ng = len(dils)
    t = dqs[0].shape[0] * dils[0]
    d = dqs[0].shape[1] // dils[0]
    w = ng * d
    tr = _row_tile(t)

    def body(*refs):
        dq_refs, dk_refs, dv_refs = refs[:ng], refs[ng:2 * ng], refs[2 * ng:3 * ng]
        pos_ref, invf_ref, o_ref = refs[3 * ng:]
        pos_f = jnp.broadcast_to(pos_ref[...].astype(F32), (tr, HEAD_DIM))
        for g, dil in enumerate(dils):
            pos_g = pos_f if dil == 1 else _permute(_perm(tr, dil), pos_f, 3)
            lane, cos, sin = _rope_tables(pos_g, invf_ref[...], -1.0)
            vals = [_rope_heads(_load_dilated(dq_refs[g], dil, d), lane, cos, sin),
                    _rope_heads(_load_dilated(dk_refs[g], dil, d), lane, cos, sin), _load_dilated(dv_refs[g], dil, d)]
            back = _perm(tr, dil, inverse=True) if dil > 1 else None
            for sec, a in enumerate(vals):
                a = a.astype(BF16)
                if dil > 1:
                    a = _permute(back, a, 1)
                o_ref[:, sec * w + g * d:sec * w + (g + 1) * d] = a.astype(o_ref.dtype)

    return pl.pallas_call(body, grid=(t // tr,),
                          in_specs=[_dilated_spec(tr, dil, d) for dil in dils] * 3 + [_row(tr, 1), _full((1, HEAD_DIM))],
                          out_specs=_row(tr, 3 * w), out_shape=SDS((t, 3 * w), BF16),
                          compiler_params=_params('parallel'), name=name)(*dqs, *dks, *dvs, pos, invf)


def _dilate_many(arrs, dil, terms, out_dtypes, name):
    t, d = arrs[0].shape
    tr = _row_tile(t)
    na = len(arrs)

    def body(*refs):
        pm = _perm(tr, dil)
        for a_ref, o_ref, k in zip(refs[:na], refs[na:], terms):
            _store_dilated(o_ref, _permute(pm, a_ref[...], k), dil, d)

    return pl.pallas_call(body, grid=(t // tr,), in_specs=[_row(tr, d)] * na,
                          out_specs=[_dilated_spec(tr, dil, d)] * na,
                          out_shape=[SDS((t // dil, dil * d), dt) for dt in out_dtypes],
                          compiler_params=_params('parallel'), name=name)(*arrs)


def _attn_masks():
    qi = lax.broadcasted_iota(jnp.int32, (ATTN_BLOCK, ATTN_BLOCK), 0)
    kj = lax.broadcasted_iota(jnp.int32, (ATTN_BLOCK, ATTN_BLOCK), 1)
    return kj >= qi, kj <= qi


def _attn_cols(l, width):
    ncol = width // HEAD_DIM
    cpb = max(1, min(ncol, 32 // (l // ATTN_BLOCK)))
    assert ncol % cpb == 0
    return cpb


def _attn_fwd(q, k, v, name):
    l, width = q.shape
    cpb = _attn_cols(l, width)
    nb = l // ATTN_BLOCK
    scale = HEAD_DIM ** -0.5

    def body(q_ref, k_ref, v_ref, o_ref, lse_ref):
        m_prev, m_cur = _attn_masks()
        for col in range(cpb):
            cs = slice(col * HEAD_DIM, (col + 1) * HEAD_DIM)

            def step(b, carry, cs=cs):
                r0 = pl.multiple_of(b * ATTN_BLOCK, ATTN_BLOCK)
                rp = pl.multiple_of(jnp.maximum(b - 1, 0) * ATTN_BLOCK, ATTN_BLOCK)
                qb = q_ref[pl.ds(r0, ATTN_BLOCK), cs]
                s_p = lax.dot_general(qb, k_ref[pl.ds(rp, ATTN_BLOCK), cs], _DIMS['nt'], preferred_element_type=F32) * scale
                s_c = lax.dot_general(qb, k_ref[pl.ds(r0, ATTN_BLOCK), cs], _DIMS['nt'], preferred_element_type=F32) * scale
                s_p = jnp.where(jnp.logical_and(m_prev, b > 0), s_p, NEG)
                s_c = jnp.where(m_cur, s_c, NEG)
                m = jnp.maximum(jnp.max(s_p, axis=-1, keepdims=True), jnp.max(s_c, axis=-1, keepdims=True))
                p_p, p_c = jnp.exp(s_p - m), jnp.exp(s_c - m)
                den = jnp.sum(p_p, axis=-1, keepdims=True) + jnp.sum(p_c, axis=-1, keepdims=True)
                acc = jnp.dot(p_p.astype(BF16), v_ref[pl.ds(rp, ATTN_BLOCK), cs], preferred_element_type=F32)
                acc += jnp.dot(p_c.astype(BF16), v_ref[pl.ds(r0, ATTN_BLOCK), cs], preferred_element_type=F32)
                o_ref[pl.ds(r0, ATTN_BLOCK), cs] = acc / den
                lse_ref[pl.ds(r0, ATTN_BLOCK), cs] = jnp.broadcast_to(m + jnp.log(den), (ATTN_BLOCK, HEAD_DIM))
                return carry

            lax.fori_loop(0, nb, step, 0, unroll=min(nb, 4))

    spec = pl.BlockSpec((l, cpb * HEAD_DIM), lambda j: (0, j))
    return pl.pallas_call(body, grid=(width // (cpb * HEAD_DIM),), in_specs=[spec] * 3, out_specs=[spec] * 2,
                          out_shape=[SDS((l, width), F32)] * 2,
                          compiler_params=_params('parallel'), name=name)(q, k, v)


def _attn_bwd(q, k, v, do, lse, delta, name):
    l, width = q.shape
    cpb = _attn_cols(l, width)
    nb = l // ATTN_BLOCK
    scale = HEAD_DIM ** -0.5

    def body(q_ref, k_ref, v_ref, do_ref, lse_ref, dl_ref, dq_ref, dk_ref, dv_ref):
        m_prev, m_cur = _attn_masks()
        dk_ref[...] = jnp.zeros_like(dk_ref)
        dv_ref[...] = jnp.zeros_like(dv_ref)
        for col in range(cpb):
            cs = slice(col * HEAD_DIM, (col + 1) * HEAD_DIM)

            def step(b, carry, cs=cs):
                r0 = pl.multiple_of(b * ATTN_BLOCK, ATTN_BLOCK)
                rp = pl.multiple_of(jnp.maximum(b - 1, 0) * ATTN_BLOCK, ATTN_BLOCK)
                qb, dob = q_ref[pl.ds(r0, ATTN_BLOCK), cs], do_ref[pl.ds(r0, ATTN_BLOCK), cs].astype(BF16)
                kp, kc = k_ref[pl.ds(rp, ATTN_BLOCK), cs], k_ref[pl.ds(r0, ATTN_BLOCK), cs]
                vp, vc = v_ref[pl.ds(rp, ATTN_BLOCK), cs], v_ref[pl.ds(r0, ATTN_BLOCK), cs]
                lse_b = lse_ref[pl.ds(r0, ATTN_BLOCK), cs]
                dl_b = dl_ref[pl.ds(r0, ATTN_BLOCK), cs]
                s_p = lax.dot_general(qb, kp, _DIMS['nt'], preferred_element_type=F32) * scale
                s_c = lax.dot_general(qb, kc, _DIMS['nt'], preferred_element_type=F32) * scale
                p_p = jnp.exp(jnp.where(jnp.logical_and(m_prev, b > 0), s_p, NEG) - lse_b)
                p_c = jnp.exp(jnp.where(m_cur, s_c, NEG) - lse_b)
                dp_p = lax.dot_general(dob, vp, _DIMS['nt'], preferred_element_type=F32)
                dp_c = lax.dot_general(dob, vc, _DIMS['nt'], preferred_element_type=F32)
                ds_p = (p_p * (dp_p - dl_b) * scale).astype(BF16)
                ds_c = (p_c * (dp_c - dl_b) * scale).astype(BF16)
                dq_ref[pl.ds(r0, ATTN_BLOCK), cs] = (jnp.dot(ds_p, kp, preferred_element_type=F32)
                                                     + jnp.dot(ds_c, kc, preferred_element_type=F32))
                dk_ref[pl.ds(rp, ATTN_BLOCK), cs] += lax.dot_general(ds_p, qb, _DIMS['tn'], preferred_element_type=F32)
                dk_ref[pl.ds(r0, ATTN_BLOCK), cs] += lax.dot_general(ds_c, qb, _DIMS['tn'], preferred_element_type=F32)
                dv_ref[pl.ds(rp, ATTN_BLOCK), cs] += lax.dot_general(p_p.astype(BF16), dob, _DIMS['tn'], preferred_element_type=F32)
                dv_ref[pl.ds(r0, ATTN_BLOCK), cs] += lax.dot_general(p_c.astype(BF16), dob, _DIMS['tn'], preferred_element_type=F32)
                return carry

            lax.fori_loop(0, nb, step, 0, unroll=min(nb, 2))

    spec = pl.BlockSpec((l, cpb * HEAD_DIM), lambda j: (0, j))
    return pl.pallas_call(body, grid=(width // (cpb * HEAD_DIM),), in_specs=[spec] * 6, out_specs=[spec] * 3,
                          out_shape=[SDS((l, width), F32)] * 3,
                          compiler_params=_params('parallel'), name=name)(q, k, v, do, lse, delta)


def _attn_combine(os_, lses, dils, name):
    ng = len(dils)
    t = os_[0].shape[0] * dils[0]
    d = os_[0].shape[1] // dils[0]
    tr = _row_tile(t)

    def body(*refs):
        o_refs, l_refs, o_out, lse_out = refs[:ng], refs[ng:2 * ng], refs[2 * ng], refs[2 * ng + 1]
        ovs, ls = [], []
        for g, dil in enumerate(dils):
            ov, lv = _load_dilated(o_refs[g], dil, d), _load_dilated(l_refs[g], dil, d)
            if dil > 1:
                back = _perm(tr, dil, inverse=True)
                ov, lv = _permute(back, ov, 2), _permute(back, lv, 3)
            ovs.append(ov)
            ls.append(lv)
        m = functools.reduce(jnp.maximum, ls)
        ws = [jnp.exp(x - m) for x in ls]
        den = functools.reduce(lambda a, b: a + b, ws)
        acc = functools.reduce(lambda a, b: a + b, [w * o for w, o in zip(ws, ovs)])
        o_out[...] = (acc / den).astype(o_out.dtype)
        lse_out[...] = m + jnp.log(den)

    return pl.pallas_call(body, grid=(t // tr,), in_specs=[_dilated_spec(tr, dil, d) for dil in dils] * 2,
                          out_specs=[_row(tr, d)] * 2, out_shape=[SDS((t, d), BF16), SDS((t, d), F32)],
                          compiler_params=_params('parallel'), name=name)(*os_, *lses)


def _delta_epilogue(acc, o):
    prod = acc * o.astype(F32)
    segs = [jnp.broadcast_to(jnp.sum(prod[:, s:s + HEAD_DIM], axis=-1, keepdims=True), (acc.shape[0], HEAD_DIM))
            for s in range(0, acc.shape[1], HEAD_DIM)]
    return acc, jnp.concatenate(segs, axis=-1)


LRU_TILE = 128


def _lru_gates(xr, wa_ref, ba, wx_ref, bx, lam):
    nb = wa_ref.shape[0]
    xb = xr.astype(BF16)
    ra = jnp.concatenate([jnp.dot(xb[:, n * LRU_BLOCK:(n + 1) * LRU_BLOCK], wa_ref[n], preferred_element_type=F32)
                          for n in range(nb)], axis=-1) + ba
    ia = jnp.concatenate([jnp.dot(xb[:, n * LRU_BLOCK:(n + 1) * LRU_BLOCK], wx_ref[n], preferred_element_type=F32)
                          for n in range(nb)], axis=-1) + bx
    r, ig = _sigmoid(ra), _sigmoid(ia)
    sp = _softplus(-lam)
    log_a = -LRU_C * r * sp
    a = jnp.exp(log_a)
    mult = jnp.sqrt(-_expm1(2.0 * log_a))
    return xb, r, ig, sp, a, mult


def _lru_fwd(z, cw, cb, wa, ba, wx, bx, lam, name):
    t, c2 = z.shape
    c = c2 // 2
    nb = c // LRU_BLOCK
    tr = _row_tile(t, LRU_TILE)

    def body(g_ref, x_ref, xp_ref, cw_ref, cb_ref, wa_ref, ba_ref, wx_ref, bx_ref, lam_ref,
             y_ref, hs_ref, xr_ref, car_ref):
        i = pl.program_id(0)

        @pl.when(i == 0)
        def _():
            car_ref[...] = jnp.zeros_like(car_ref)

        x0 = x_ref[...]
        xp = jnp.where(i > 0, xp_ref[SUB:HALO, :], 0.0)
        cwv = cw_ref[...]
        xr = (cb_ref[...] + cwv[3:4] * x0 + cwv[2:3] * _shift_down(x0, 1, xp)
              + cwv[1:2] * _shift_down(x0, 2, xp) + cwv[0:1] * _shift_down(x0, 3, xp))
        xr_ref[...] = xr
        _, _, ig, _, a, mult = _lru_gates(xr, wa_ref, ba_ref[...], wx_ref, bx_ref[...], lam_ref[...])
        u = mult * (ig * xr)
        row = lax.broadcasted_iota(jnp.int32, (SUB, c), 0)
        car = car_ref[...]
        for j in range(tr // SUB):
            ab, ub = a[j * SUB:(j + 1) * SUB], u[j * SUB:(j + 1) * SUB]
            for s in (1, 2, 4):
                a_sh = jnp.where(row >= s, pltpu.roll(ab, s, 0), 1.0)
                u_sh = jnp.where(row >= s, pltpu.roll(ub, s, 0), 0.0)
                ub = ab * u_sh + ub
                ab = ab * a_sh
            hb = ub + ab * car
            hs_ref[j * SUB:(j + 1) * SUB, :] = hb
            car = jnp.broadcast_to(hb[SUB - 1:SUB], (SUB, c))
        car_ref[...] = car
        gl, _ = _gelu_and_grad(g_ref[...])
        y_ref[...] = (hs_ref[...] * gl).astype(y_ref.dtype)

    return pl.pallas_call(
        body, grid=(t // tr,),
        in_specs=[_row(tr, c, 0), _row(tr, c, 1), _halo_prev(tr, c, 1), _full((4, c)), _full((1, c)),
                  _full((nb, LRU_BLOCK, LRU_BLOCK)), _full((1, c)), _full((nb, LRU_BLOCK, LRU_BLOCK)), _full((1, c)), _full((1, c))],
        out_specs=[_row(tr, c)] * 3,
        out_shape=[SDS((t, c), BF16), SDS((t, c), F32), SDS((t, c), F32)],
        scratch_shapes=[pltpu.VMEM((SUB, c), F32)],
        compiler_params=_params('arbitrary'), name=name)(
            z, z, z, cw, cb.reshape(1, c), wa, ba.reshape(1, c), wx, bx.reshape(1, c), lam.reshape(1, c))


def _lru_bwd(dy, z, xr, hs, cw, wa, ba, wx, bx, lam, name):
    t, c2 = z.shape
    c = c2 // 2
    nb = c // LRU_BLOCK
    tr = _row_tile(t, LRU_TILE)
    nt = t // tr

    def rev(col=0):
        return pl.BlockSpec((tr, c), lambda i, col=col: (nt - 1 - i, col))

    def rev_prev(col=0):
        return pl.BlockSpec((HALO, c), lambda i, col=col: (jnp.maximum((nt - 1 - i) * (tr // HALO) - 1, 0), col))

    def body(dy_ref, g_ref, x_ref, xp_ref, xr_ref, hs_ref, hp_ref, cw_ref, wa_ref, ba_ref, wx_ref, bx_ref, lam_ref,
             dz_ref, dwa_ref, dwx_ref, dvec_ref, lcar_ref, ahead_ref, dxhead_ref, lam_s):
        i = pl.program_id(0)
        first_tile = i == nt - 1

        @pl.when(i == 0)
        def _():
            lcar_ref[...] = jnp.zeros_like(lcar_ref)
            ahead_ref[...] = jnp.zeros_like(ahead_ref)
            dxhead_ref[...] = jnp.zeros_like(dxhead_ref)
            dwa_ref[...] = jnp.zeros_like(dwa_ref)
            dwx_ref[...] = jnp.zeros_like(dwx_ref)
            dvec_ref[...] = jnp.zeros_like(dvec_ref)

        xrv = xr_ref[...]
        lamv = lam_ref[...]
        xb, r, ig, sp, a, mult = _lru_gates(xrv, wa_ref, ba_ref[...], wx_ref, bx_ref[...], lamv)
        hsv = hs_ref[...]
        dyv = dy_ref[...]
        gl, dgl = _gelu_and_grad(g_ref[...])
        dhs = dyv * gl
        dz_ref[:, :c] = (dyv * hsv * dgl).astype(dz_ref.dtype)

        a_next = _shift_up(a, 1, ahead_ref[...])
        row = lax.broadcasted_iota(jnp.int32, (SUB, c), 0)
        car = lcar_ref[...]
        for j in reversed(range(tr // SUB)):
            ab, ub = a_next[j * SUB:(j + 1) * SUB], dhs[j * SUB:(j + 1) * SUB]
            for s in (1, 2, 4):
                a_sh = jnp.where(row < SUB - s, pltpu.roll(ab, SUB - s, 0), 1.0)
                u_sh = jnp.where(row < SUB - s, pltpu.roll(ub, SUB - s, 0), 0.0)
                ub = ab * u_sh + ub
                ab = ab * a_sh
            lb = ub + ab * car
            lam_s[j * SUB:(j + 1) * SUB, :] = lb
            car = jnp.broadcast_to(lb[0:1], (SUB, c))
        lcar_ref[...] = car
        ahead_ref[...] = a[0:SUB]
        lmb = lam_s[...]

        hp = jnp.where(first_tile, 0.0, hp_ref[SUB:HALO, :])
        h_prev = _shift_down(hsv, 1, hp)
        d_a = lmb * h_prev
        d_mult = lmb * (ig * xrv)
        d_ixr = lmb * mult
        d_ig = d_ixr * xrv
        dxr = d_ixr * ig
        d_la = d_a * a - d_mult * (a * a) / mult
        d_r = d_la * (-LRU_C * sp)
        d_sp = jnp.sum(d_la * (-LRU_C * r), axis=0, keepdims=True)
        d_ra = d_r * r * (1.0 - r)
        d_ia = d_ig * ig * (1.0 - ig)
        d_rab, d_iab = d_ra.astype(BF16), d_ia.astype(BF16)
        parts = []
        for n in range(nb):
            cs = slice(n * LRU_BLOCK, (n + 1) * LRU_BLOCK)
            parts.append(lax.dot_general(d_rab[:, cs], wa_ref[n], _DIMS['nt'], preferred_element_type=F32)
                         + lax.dot_general(d_iab[:, cs], wx_ref[n], _DIMS['nt'], preferred_element_type=F32))
            dwa_ref[n] += lax.dot_general(xb[:, cs], d_rab[:, cs], _DIMS['tn'], preferred_element_type=F32)
            dwx_ref[n] += lax.dot_general(xb[:, cs], d_iab[:, cs], _DIMS['tn'], preferred_element_type=F32)
        dxr = dxr + jnp.concatenate(parts, axis=-1)

        cwv = cw_ref[...]
        nxt = dxhead_ref[...]
        dx0 = (cwv[3:4] * dxr + cwv[2:3] * _shift_up(dxr, 1, nxt) + cwv[1:2] * _shift_up(dxr, 2, nxt)
               + cwv[0:1] * _shift_up(dxr, 3, nxt))
        dxhead_ref[...] = dxr[0:SUB]
        dz_ref[:, c:] = dx0.astype(dz_ref.dtype)

        x0 = x_ref[...]
        xp = jnp.where(first_tile, 0.0, xp_ref[SUB:HALO, :])
        sums = [jnp.sum(d_ra, axis=0, keepdims=True), jnp.sum(d_ia, axis=0, keepdims=True),
                d_sp * (-_sigmoid(-lamv)), jnp.sum(dxr, axis=0, keepdims=True),
                jnp.sum(dxr * _shift_down(x0, 3, xp), axis=0, keepdims=True),
                jnp.sum(dxr * _shift_down(x0, 2, xp), axis=0, keepdims=True),
                jnp.sum(dxr * _shift_down(x0, 1, xp), axis=0, keepdims=True),
                jnp.sum(dxr * x0, axis=0, keepdims=True)]
        dvec_ref[...] += jnp.concatenate(sums, axis=0)

    wspec = _full((nb, LRU_BLOCK, LRU_BLOCK))
    return pl.pallas_call(
        body, grid=(nt,),
        in_specs=[rev(), rev(0), rev(1), rev_prev(1), rev(), rev(), rev_prev(), _full((4, c)),
                  wspec, _full((1, c)), wspec, _full((1, c)), _full((1, c))],
        out_specs=[pl.BlockSpec((tr, c2), lambda i: (nt - 1 - i, 0)), wspec, wspec, _full((SUB, c))],
        out_shape=[SDS((t, c2), BF16), SDS((nb, LRU_BLOCK, LRU_BLOCK), F32), SDS((nb, LRU_BLOCK, LRU_BLOCK), F32),
                   SDS((SUB, c), F32)],
        scratch_shapes=[pltpu.VMEM((SUB, c), F32), pltpu.VMEM((SUB, c), F32), pltpu.VMEM((SUB, c), F32),
                        pltpu.VMEM((tr, c), F32)],
        compiler_params=_params('arbitrary'), name=name)(
            dy, z, z, z, xr, hs, hs, cw, wa, ba.reshape(1, c), wx, bx.reshape(1, c), lam.reshape(1, c))


def _local_step(x, p, pos, target, rep, weights_for_layer, emit_grads):
    t, d = x.shape
    depth = p.shape[0]
    w = rep
    half = ROPE_DIM // 2
    invf = ROPE_THETA ** (-2.0 * jnp.arange(half, dtype=F32) / ROPE_DIM)
    invf = jnp.concatenate([invf, invf, jnp.zeros((HEAD_DIM - ROPE_DIM,), F32)]).reshape(1, HEAD_DIM)
    dils = tuple(dil for _, dil in DILATED_PATTERNS)
    saved = []
    h = x
    for i in range(depth):
        kind, j = i % N_MIXERS, i // N_MIXERS
        wl, tok = weights_for_layer(i, 'mixer', h)
        s = {'h0': h, 'wl': wl}
        hn = _rms_fwd(h, w['norm_mix'][i], f'rms_mix_fwd_{i}')
        s['hn'] = hn
        if kind == 0:
            z = _mm(hn, wl['w_in'], 'nn', f'sc_in_{i}', dep=tok)
            y = _sc_fwd(z, wl['small'], f'sc_conv_fwd_{i}')
            h1 = _mm(y, wl['w_out'], 'nn', f'sc_out_{i}', extras=(h,), epi=lambda acc, res: (acc + res,))
            s.update(z=z, y=y)
        elif kind == 1:
            qkv = _mm(hn, wl['w_in'], 'nn', f'attn_qkv_{i}', dep=tok)
            qs, ks, vs = _rope_fwd(qkv, pos, invf, dils, f'rope_fwd_{i}')
            views = list(zip(qs, ks, vs))
            os_, lses = zip(*[_attn_fwd(qg, kg, vg, f'attn_fwd_{i}_g{g}') for g, (qg, kg, vg) in enumerate(views)])
            o, lse = _attn_combine(os_, lses, dils, f'attn_combine_{i}')
            h1 = _mm(o, wl['w_out'], 'nn', f'attn_out_{i}', extras=(h,), epi=lambda acc, res: (acc + res,))
            s.update(views=views, o=o, lse=lse)
        else:
            z = _mm(hn, wl['w_in'], 'nn', f'lru_in_{i}', dep=tok)
            sm = wl['small']
            y, hs, xr = _lru_fwd(z, sm[0:4], sm[4:5], w['lru_w_a'][j], sm[5:6], w['lru_w_x'][j], sm[6:7], sm[7:8],
                                 f'lru_fwd_{i}')
            h1 = _mm(y, wl['w_out'], 'nn', f'lru_out_{i}', extras=(h,), epi=lambda acc, res: (acc + res,))
            s.update(z=z, y=y, hs=hs, xr=xr)
        s['h1'] = h1
        more, tok = weights_for_layer(i, 'mlp', h1)
        wl.update(more)
        hm = _rms_fwd(h1, w['norm_mlp'][i], f'rms_mlp_fwd_{i}')
        u = _mm(hm, wl['mlp_up'], 'nn', f'mlp_up_{i}', out_dtypes=(BF16,), dep=tok)
        h2 = _mm(u, wl['mlp_down'], 'nn', f'mlp_down_{i}', a_pro=_relu2, extras=(h1,), epi=lambda acc, res: (acc + res,))
        hp = _rms_fwd(h2, w['norm_ple'][i], f'rms_ple_fwd_{i}')
        pp = _mm(p[i], wl['ple_proj'], 'nn', f'ple_proj_{i}')
        h3, gate = _mm(hp, wl['ple_gate'], 'nn', f'ple_gate_{i}', out_dtypes=(F32, F32), extras=(pp, h2),
                       epi=lambda acc, ppv, res: (res + _sigmoid(acc) * ppv, _sigmoid(acc)))
        s.update(hm=hm, u=u, h2=h2, hp=hp, pp=pp, gate=gate)
        saved.append(s)
        h = h3

    dh, loss, dg_final = _head(h, w['norm_final'], target, 'loss_head')
    grads = {n: [None] * depth for n in ('norm_mix', 'norm_mlp', 'norm_ple')}
    grads['norm_final'] = dg_final.reshape(d)
    started = None
    for i in reversed(range(depth)):
        kind, j = i % N_MIXERS, i // N_MIXERS
        s = saved[i]
        wl, gl = s['wl'], {}
        dpp, dgl = _ple_bwd_gate(dh, s['gate'], s['pp'], f'ple_bwd_gate_{i}')
        gl['ple_proj'] = _mm(p[i], dpp, 'tn', f'ple_dproj_{i}', out_dtypes=(BF16,), dep=started)
        gl['ple_gate'] = _mm(s['hp'], dgl, 'tn', f'ple_dgate_{i}', out_dtypes=(BF16,))
        dhp = _mm(dgl, wl['ple_gate'], 'nt', f'ple_dhp_{i}')
        dh, dg = _rms_bwd(s['h2'], w['norm_ple'][i], dhp, dh, f'rms_ple_bwd_{i}')
        grads['norm_ple'][i] = dg.reshape(d)
        du = _mm(dh, wl['mlp_down'], 'nt', f'mlp_du_{i}', out_dtypes=(BF16,), extras=(s['u'],),
                 epi=lambda acc, uv: (acc * 2.0 * jnp.maximum(uv.astype(F32), 0.0),))
        gl['mlp_down'] = _mm(s['u'], dh, 'tn', f'mlp_ddown_{i}', out_dtypes=(BF16,), a_pro=_relu2)
        gl['mlp_up'] = _mm(s['hm'], du, 'tn', f'mlp_dup_{i}', out_dtypes=(BF16,), out_stacked=True)
        dhm = _mm(du, wl['mlp_up'], 'nt', f'mlp_dhm_{i}')
        dh, dg = _rms_bwd(s['h1'], w['norm_mlp'][i], dhm, dh, f'rms_mlp_bwd_{i}')
        grads['norm_mlp'][i] = dg.reshape(d)
        started = emit_grads(i, 'mlp', gl, loss if i == depth - 1 else None)
        gl = {}
        if kind == 0:
            dy = _mm(dh, wl['w_out'], 'nt', f'sc_dy_{i}', dep=started)
            gl['w_out'] = _mm(s['y'], dh, 'tn', f'sc_dout_{i}', out_dtypes=(BF16,))
            dz, dwc = _sc_bwd(dy, s['z'], wl['small'], f'sc_conv_bwd_{i}')
            gl['small'] = dwc
            gl['w_in'] = _mm(s['hn'], dz, 'tn', f'sc_din_{i}', out_dtypes=(BF16,))
            started = emit_grads(i, 'mixer', gl)
            dhn = _mm(dz, wl['w_in'], 'nt', f'sc_dhn_{i}', dep=started)
        elif kind == 1:
            do, delta = _mm(dh, wl['w_out'], 'nt', f'attn_do_{i}', out_dtypes=(BF16, F32), extras=(s['o'],),
                            epi=_delta_epilogue, dep=started)
            gl['w_out'] = _mm(s['o'], dh, 'tn', f'attn_dwo_{i}', out_dtypes=(BF16,))
            rows_in = {1: (do, s['lse'], delta)}
            for dil in dils:
                if dil not in rows_in:
                    rows_in[dil] = _dilate_many([do, s['lse'], delta], dil, (1, 3, 3), (BF16, F32, F32),
                                                f'attn_dilate_{i}_d{dil}')
            dqs, dks, dvs = zip(*[_attn_bwd(*s['views'][g], *rows_in[dil], f'attn_bwd_{i}_g{g}')
                                  for g, dil in enumerate(dils)])
            dqkv = _rope_bwd(dqs, dks, dvs, pos, invf, dils, f'rope_bwd_{i}')
            gl['w_in'] = _mm(s['hn'], dqkv, 'tn', f'attn_dqkv_{i}', out_dtypes=(BF16,), out_stacked=True)
            started = emit_grads(i, 'mixer', gl)
            dhn = _mm(dqkv, wl['w_in'], 'nt', f'attn_dhn_{i}', dep=started)
        else:
            dy = _mm(dh, wl['w_out'], 'nt', f'lru_dy_{i}', dep=started)
            gl['w_out'] = _mm(s['y'], dh, 'tn', f'lru_dout_{i}', out_dtypes=(BF16,))
            sm = wl['small']
            dz, dwa, dwx, dvec = _lru_bwd(dy, s['z'], s['xr'], s['hs'], sm[0:4], w['lru_w_a'][j], sm[5:6],
                                          w['lru_w_x'][j], sm[6:7], sm[7:8], f'lru_bwd_{i}')
            gl['gates'], gl['small'] = (dwa, dwx), dvec
            gl['w_in'] = _mm(s['hn'], dz, 'tn', f'lru_din_{i}', out_dtypes=(BF16,))
            started = emit_grads(i, 'mixer', gl)
            dhn = _mm(dz, wl['w_in'], 'nt', f'lru_dhn_{i}', dep=started)
        dh, dg = _rms_bwd(s['h0'], w['norm_mix'][i], dhn, dh, f'rms_mix_bwd_{i}')
        grads['norm_mix'][i] = dg.reshape(d)
        started = None
    return loss, dh, grads


_MESH = pl.DeviceIdType.MESH
_ANY = pl.BlockSpec(memory_space=pl.ANY)


def _block_view(ref, kind, idx):
    if kind == 'stack':
        return ref.at[idx]
    r = ref.shape[0] // N_DEV
    return ref.at[pl.ds(idx * r, r)]


def _gather_many(arrs, kinds, name, after=None):
    n = len(arrs)
    after = [] if after is None else [after]
    out_shapes = [SDS((N_DEV,) + a.shape if kd == 'stack' else (N_DEV * a.shape[0],) + a.shape[1:], a.dtype)
                  for a, kd in zip(arrs, kinds)]

    def body(*refs):
        x_refs, out_refs = refs[:n], refs[n + len(after):2 * n + len(after)]
        send_sems, recv_sems, local_sems = refs[2 * n + len(after):]
        x, y, c = lax.axis_index('x'), lax.axis_index('y'), lax.axis_index('c')
        me, sibling = (x, y, c), (x, y, 1 - c)
        chips = [(1 - x, y), (x, 1 - y), (1 - x, 1 - y)]

        def slab(t, px, py, pc):
            return _block_view(out_refs[t], kinds[t], 4 * px + 2 * py + pc)

        def copy(t, k, block, to, src=None):
            return pltpu.make_async_remote_copy(
                src_ref=slab(t, *block) if src is None else src, dst_ref=slab(t, *block),
                send_sem=send_sems.at[7 * t + k], recv_sem=recv_sems.at[7 * t + k], device_id=to, device_id_type=_MESH)

        mine = [pltpu.make_async_copy(x_refs[t], slab(t, *me), local_sems.at[t]) for t in range(n)]
        for cp in mine:
            cp.start()
        first = [copy(t, 0, me, sibling, src=x_refs[t]) for t in range(n)]
        first += [copy(t, 1 + j, me, (*chip, c), src=x_refs[t]) for j, chip in enumerate(chips) for t in range(n)]
        for cp in first:
            cp.start()
        passed = []
        for j, chip in enumerate(chips):
            for t in range(n):
                copy(t, 1 + j, (*chip, c), me).wait_recv()
                passed.append(copy(t, 4 + j, (*chip, c), sibling))
                passed[-1].start()
        for t in range(n):
            copy(t, 0, sibling, me).wait_recv()
            for j, chip in enumerate(chips):
                copy(t, 4 + j, (*chip, 1 - c), me).wait_recv()
        for cp in first + passed:
            cp.wait_send()
        for cp in mine:
            cp.wait()

    return pl.pallas_call(
        body, out_shape=out_shapes, in_specs=[_ANY] * (n + len(after)), out_specs=[_ANY] * n,
        scratch_shapes=[pltpu.SemaphoreType.DMA((7 * n,)), pltpu.SemaphoreType.DMA((7 * n,)), pltpu.SemaphoreType.DMA((n,))],
        name=name)(*arrs, *after)


_HBM = pl.BlockSpec(memory_space=pltpu.HBM)
_SEM = pl.BlockSpec(memory_space=pltpu.SEMAPHORE)
_EFFECT = pltpu.SideEffectType.DATAFLOW_SIDE_EFFECTING


def _direct_copies(mode, kinds, src_refs, land_refs, send_sems, recv_sems):
    x, y, c = lax.axis_index('x'), lax.axis_index('y'), lax.axis_index('c')
    my_idx = 4 * x + 2 * y + c
    copies = []
    for k in range(1, N_DEV):
        px, py, pc = (1 - x if k & 4 else x, 1 - y if k & 2 else y, 1 - c if k & 1 else c)
        for t, kd in enumerate(kinds):
            if mode == 'gather':
                src, dst = src_refs[t], _block_view(land_refs[t], kd, my_idx)
            else:
                src, dst = _block_view(src_refs[t], kd, 4 * px + 2 * py + pc), land_refs[t].at[my_idx]
            copies.append(pltpu.make_async_remote_copy(
                src_ref=src, dst_ref=dst, send_sem=send_sems.at[7 * t + k - 1], recv_sem=recv_sems.at[7 * t + k - 1],
                device_id=(px, py, pc), device_id_type=_MESH))
    return copies


def _own_part(mode, kind, src, land):
    idx = 4 * lax.axis_index('x') + 2 * lax.axis_index('y') + lax.axis_index('c')
    zeros = (0,) * (src.ndim - 1)
    if mode == 'gather':
        part = src
    elif kind == 'stack':
        part = lax.dynamic_index_in_dim(src, idx, 0, keepdims=False)
    else:
        r = src.shape[0] // N_DEV
        part = lax.dynamic_slice_in_dim(src, idx * r, r, 0)
    if mode == 'gather' and kind == 'rows':
        return lax.dynamic_update_slice(land, part, (idx * part.shape[0],) + zeros)
    return lax.dynamic_update_slice(land, part[None], (idx,) + (0,) * part.ndim)


def _send_start(mode, srcs, kinds, name, after=None):
    n = len(srcs)
    after = [] if after is None else [after]
    lands = []
    for a, kd in zip(srcs, kinds):
        if mode == 'gather':
            shape = (N_DEV,) + a.shape if kd == 'stack' else (N_DEV * a.shape[0],) + a.shape[1:]
        else:
            shape = a.shape if kd == 'stack' else (N_DEV, a.shape[0] // N_DEV) + a.shape[1:]
        lands.append(_own_part(mode, kd, a, lax.empty(shape, a.dtype)))

    def body(*refs):
        src_refs, land_refs = refs[:n], refs[n:2 * n]
        send_sems, recv_sems = refs[2 * n + len(after):2 * n + len(after) + 2]
        token = refs[-1]
        for cp in _direct_copies(mode, kinds, src_refs, land_refs, send_sems, recv_sems):
            cp.start()
        token[...] = jnp.zeros_like(token)

    outs = pl.pallas_call(
        body, name=name,
        out_shape=(pltpu.SemaphoreType.DMA((7 * n,)), pltpu.SemaphoreType.DMA((7 * n,)),
                   *[pltpu.HBM(a.shape, a.dtype) for a in srcs + lands], SDS((SUB, 128), F32)),
        in_specs=[_HBM] * (2 * n) + [_ANY] * len(after),
        out_specs=(_SEM, _SEM, *[_HBM] * (2 * n), pl.BlockSpec(memory_space=pltpu.VMEM)),
        input_output_aliases={i: 2 + i for i in range(2 * n)},
        compiler_params=pltpu.CompilerParams(has_side_effects=_EFFECT),
    )(*[pltpu.with_memory_space_constraint(a, pltpu.HBM) for a in srcs + lands], *after)
    return (outs[0], outs[1], list(outs[2:2 + 2 * n])), outs[-1]


def _send_wait(mode, flight, kinds, after, name):
    send, recv, bufs = flight
    n = len(kinds)

    def body(*refs):
        src_refs, land_refs, (send_sems, recv_sems) = refs[:n], refs[n:2 * n], refs[2 * n:2 * n + 2]
        copies = _direct_copies(mode, kinds, src_refs, land_refs, send_sems, recv_sems)
        for cp in copies:
            cp.wait_send()
        for cp in copies:
            cp.wait_recv()

    outs = pl.pallas_call(
        body, name=name, out_shape=[pltpu.HBM(a.shape, a.dtype) for a in bufs],
        in_specs=[_HBM] * (2 * n) + [_SEM, _SEM, _ANY], out_specs=[_HBM] * (2 * n),
        input_output_aliases={i: i for i in range(2 * n)},
        compiler_params=pltpu.CompilerParams(has_side_effects=_EFFECT),
    )(*bufs, send, recv, after)
    return list(outs[n:])


ADAMW_BLOCK_ELEMS = 128 * 1024


def _adamw_sum(wgt, parts, m, v, name):
    nl, r, c = wgt.shape
    assert len(parts) == nl and all(q.shape == (N_DEV, r, c) for q in parts), (name, wgt.shape, [q.shape for q in parts])
    tr = next((t for t in range(min(r, 512), 0, -16) if r % t == 0 and t * c <= ADAMW_BLOCK_ELEMS and t % 16 == 0), r)
    c1 = 1.0 - ADAM_B1 ** ADAM_STEP
    c2 = 1.0 - ADAM_B2 ** ADAM_STEP

    def body(w_ref, m_ref, v_ref, *rest):
        part_refs, (g_ref, d_ref, mo_ref, vo_ref) = rest[:nl], rest[nl:]
        for q in range(nl):
            @pl.when(pl.program_id(0) == q)
            def _(q=q):
                gv = part_refs[q][0].astype(F32)
                for s in range(1, N_DEV):
                    gv = gv + part_refs[q][s].astype(F32)
                mn = ADAM_B1 * m_ref[...] + (1.0 - ADAM_B1) * gv
                vn = ADAM_B2 * v_ref[...] + (1.0 - ADAM_B2) * (gv * gv)
                g_ref[...] = gv
                d_ref[...] = -ADAM_LR * ((mn / c1) / (jnp.sqrt(vn / c2) + ADAM_EPS) + ADAM_WD * w_ref[...])
                mo_ref[...] = mn
                vo_ref[...] = vn

    spec = pl.BlockSpec((None, tr, c), lambda l, i: (l, i, 0))
    part_specs = [pl.BlockSpec((N_DEV, tr, c), lambda l, i, q=q: (0, jnp.where(l == q, i, 0), 0)) for q in range(nl)]
    return pl.pallas_call(body, grid=(nl, r // tr), in_specs=[spec] * 3 + part_specs, out_specs=[spec] * 4,
                          out_shape=[SDS((nl, r, c), F32)] * 4, compiler_params=_params('arbitrary', 'arbitrary'),
                          name=name)(wgt, m, v, *parts)


MIXER_WEIGHTS = {0: ('sc_w_in', 'sc_w_out'), 1: ('attn_w_qkv', 'attn_w_o'), 2: ('lru_w_in', 'lru_w_out')}
STACKED_OPERANDS = ('attn_w_qkv', 'mlp_w_up')
LRU_SMALL = ('lru_conv_w', 'lru_conv_b', 'lru_b_a', 'lru_b_x', 'lru_lambda')


def _layer_items(i):
    w_in, w_out = MIXER_WEIGHTS[i % N_MIXERS]
    j = i // N_MIXERS
    return [('w_in', w_in, j), ('w_out', w_out, j), ('mlp_up', 'mlp_w_up', i), ('mlp_down', 'mlp_w_down', i),
            ('ple_gate', 'ple_w_gate', i), ('ple_proj', 'ple_w_proj', i)]


def _cols_to_full(stacked):
    return jnp.moveaxis(stacked, 0, 1).reshape(stacked.shape[1], -1)


def _full_to_cols(full):
    k, n = full.shape
    return jnp.moveaxis(full.reshape(k, N_DEV, n // N_DEV), 1, 0)


def _pad_to(a, rows):
    return jnp.pad(a, ((0, rows - a.shape[0]), (0, 0)))


def _small_block(src, i):
    kind, j = i % N_MIXERS, i // N_MIXERS
    if kind == 0:
        return _pad_to(src['sc_w_conv'][j], SUB)
    if kind == 2:
        return jnp.concatenate([src[n][j].reshape(-1, src[n].shape[-1]) for n in LRU_SMALL], axis=0)
    return None


def kernel(x, p, positions, norm_mix, norm_mlp, norm_ple, norm_final, sc_w_in, sc_w_conv, sc_w_out, attn_w_qkv, attn_w_o, lru_w_in, lru_conv_w, lru_conv_b, lru_w_a, lru_b_a, lru_w_x, lru_b_x, lru_lambda, lru_w_out, mlp_w_up, mlp_w_down, ple_w_gate, ple_w_proj, loss_target, m_norm_mix, m_norm_mlp, m_norm_ple, m_norm_final, m_sc_w_in, m_sc_w_conv, m_sc_w_out, m_attn_w_qkv, m_attn_w_o, m_lru_w_in, m_lru_conv_w, m_lru_conv_b, m_lru_w_a, m_lru_b_a, m_lru_w_x, m_lru_b_x, m_lru_lambda, m_lru_w_out, m_mlp_w_up, m_mlp_w_down, m_ple_w_gate, m_ple_w_proj, v_norm_mix, v_norm_mlp, v_norm_ple, v_norm_final, v_sc_w_in, v_sc_w_conv, v_sc_w_out, v_attn_w_qkv, v_attn_w_o, v_lru_w_in, v_lru_conv_w, v_lru_conv_b, v_lru_w_a, v_lru_b_a, v_lru_w_x, v_lru_b_x, v_lru_lambda, v_lru_w_out, v_mlp_w_up, v_mlp_w_down, v_ple_w_gate, v_ple_w_proj):
    loc = dict(locals())
    shards = {n: loc[n] for n in WEIGHTS}
    moms = {n: loc['m_' + n] for n in WEIGHTS}
    vels = {n: loc['v_' + n] for n in WEIGHTS}

    depth, t, d = p.shape[0], x.shape[1], x.shape[2]

    def comm_kind(name):
        return 'stack' if SHARD_AXIS[name] == 2 else 'rows'

    part_keys = {'mlp': ('mlp_up', 'mlp_down', 'ple_gate', 'ple_proj'), 'mixer': ('w_in', 'w_out')}
    halves = [(i, part) for i in range(depth) for part in ('mixer', 'mlp')]

    def half_shards(i, part):
        items = [it for it in _layer_items(i) if it[0] in part_keys[part]]
        arrs = [shards[n][idx].astype(BF16) for _, n, idx in items]
        kinds = [comm_kind(n) for _, n, _ in items]
        small = _small_block(shards, i) if part == 'mixer' else None
        if small is not None:
            arrs.append(small)
            kinds.append('stack')
        return items, arrs, kinds

    def half_weights(i, items, kinds, outs):
        wl = {key: (_cols_to_full(o) if kd == 'stack' and n not in STACKED_OPERANDS else o)
              for (key, n, _), kd, o in zip(items, kinds, outs)}
        if len(outs) > len(items):
            wl['small'] = _cols_to_full(outs[-1])[:shards['sc_w_conv'].shape[1] if i % N_MIXERS == 0 else SUB]
        return wl

    first = [half_shards(0, part) for part in ('mixer', 'mlp')]
    outs0 = _gather_many(first[0][1] + first[1][1], first[0][2] + first[1][2], 'gather_weights_0')
    weights0 = {**half_weights(0, first[0][0], first[0][2], outs0[:len(first[0][1])]),
                **half_weights(0, first[1][0], first[1][2], outs0[len(first[0][1]):])}
    pending = {}

    def start_gather(pos, after):
        if pos >= len(halves):
            return None
        i, part = halves[pos]
        items, arrs, kinds = half_shards(i, part)
        flight, token = _send_start('gather', arrs, kinds, f'gather_weights_start_{part}_{i}', after=after)
        pending[pos] = (items, kinds, flight)
        return token

    first_token = start_gather(2, outs0[0])
    second_token = start_gather(3, first_token)

    def weights_for_layer(i, part, h):
        pos = halves.index((i, part))
        if pos == 0:
            return weights0, second_token
        if pos == 1:
            return {}, None
        items, kinds, flight = pending.pop(pos)
        outs = _send_wait('gather', flight, kinds, h, f'gather_weights_wait_{part}_{i}')
        return half_weights(i, items, kinds, outs), start_gather(pos + 2, outs[0])

    exchanges, gate_gathers, total_loss = {}, {}, []

    def gate_block(src, j):
        return jnp.concatenate([src[n][j].reshape(-1, LRU_BLOCK) for n in ('lru_w_a', 'lru_w_x')], axis=0)

    def emit_grads(i, part, gl, loss=None):
        after = None
        if loss is not None:
            total_loss.append(lax.psum(loss[0, 0], ('x', 'y', 'c')))
            after = jnp.full((SUB, 128), total_loss[0], F32)
        if 'gates' in gl:
            blk = gate_block({'lru_w_a': [gl['gates'][0]], 'lru_w_x': [gl['gates'][1]]}, 0)
            gate_gathers[i] = _send_start('gather', [blk], ['stack'], f'gather_gate_grads_start_{i}')[0]
        items = [it for it in _layer_items(i) if it[0] in part_keys[part]]
        kinds = [comm_kind(n) for _, n, _ in items]
        arrs = [_full_to_cols(gl[key]) if kd == 'stack' and gl[key].ndim == 2 else gl[key]
                for (key, _, _), kd in zip(items, kinds)]
        if part == 'mixer' and i % N_MIXERS == 0:
            arrs.append(_full_to_cols(_pad_to(gl['small'], SUB)))
        elif part == 'mixer' and i % N_MIXERS == 2:
            dv = gl['small']
            arrs.append(_full_to_cols(jnp.concatenate([dv[4:8], dv[3:4], dv[0:1], dv[1:2], dv[2:3]], axis=0)))
        kinds += ['stack'] * (len(arrs) - len(kinds))
        flight, token = _send_start('exchange', arrs, kinds, f'exchange_grads_start_{part}_{i}', after=after)
        exchanges[(i, part)] = (items, kinds, flight)
        return token

    rep = {n: shards[n] for n in ('norm_mix', 'norm_mlp', 'norm_ple', 'norm_final')}
    rep['lru_w_a'], rep['lru_w_x'] = shards['lru_w_a'].astype(BF16), shards['lru_w_x'].astype(BF16)
    loss, grad_x, rgrads = _local_step(x.reshape(t, d), p.reshape(depth, t, p.shape[3]), positions.reshape(t, 1),
                                       loss_target.reshape(t, d), rep, weights_for_layer, emit_grads)

    received, res = {}, {}

    def finish_exchange(key, after):
        items, kinds, flight = exchanges[key]
        outs = _send_wait('exchange', flight, kinds, after, f'exchange_grads_wait_{key[1]}_{key[0]}')
        for (_, n, idx), o in zip(items, outs):
            received[(n, idx)] = o
        if len(outs) > len(items):
            received[('small', key[0])] = outs[-1]

    def big_adamw(names):
        for n in names:
            res[n] = _adamw_sum(shards[n], [received[(n, l)] for l in range(shards[n].shape[0])], moms[n], vels[n],
                                f'adamw_{n}')

    last = (0, 'mixer')
    for key in exchanges:
        if key != last:
            finish_exchange(key, grad_x)
    big = [n for n in WEIGHTS if SHARD_AXIS[n] is not None and shards[n].ndim == 3 and n not in ('sc_w_conv', 'lru_conv_w')]
    late = [n for n in big if n in MIXER_WEIGHTS[0]]
    big_adamw([n for n in big if n not in late])
    finish_exchange(last, jnp.full((SUB, 128), sum(r[0].reshape(-1)[0] for r in res.values()), F32))
    big_adamw(late)


    def small_adamw(layers, name):
        w_, m_, v_ = (jnp.stack([_small_block(src, i) for i in layers]) for src in (shards, moms, vels))
        return _adamw_sum(w_, [received[('small', i)] for i in layers], m_, v_, name)

    sc = small_adamw([i for i in range(depth) if i % N_MIXERS == 0], 'adamw_sc_w_conv')
    res['sc_w_conv'] = tuple(o[:, :shards['sc_w_conv'].shape[1]] for o in sc)
    lru = small_adamw([i for i in range(depth) if i % N_MIXERS == 2], 'adamw_lru_small')
    row = 0
    for n in LRU_SMALL:
        k = shards[n].size // shards[n].shape[0] // shards[n].shape[-1]
        res[n] = tuple(o[:, row:row + k].reshape(shards[n].shape) for o in lru)
        row += k

    def all_updated():
        return jnp.full((SUB, 128), sum(r[0].reshape(-1)[0] for r in res.values()), F32)

    gate_layers = sorted(gate_gathers)
    gate_parts = [_send_wait('gather', gate_gathers[i], ['stack'], all_updated(), f'gather_gate_grads_wait_{i}')[0]
                  for i in gate_layers]
    gate_w, gate_m, gate_v = (jnp.stack([gate_block(src, j) for j in range(len(gate_layers))])
                              for src in (shards, moms, vels))
    gates = _adamw_sum(gate_w, gate_parts, gate_m, gate_v, 'adamw_lru_gates')
    half = gates[0].shape[1] // 2
    res['lru_w_a'] = tuple(o[:, :half].reshape(shards['lru_w_a'].shape) for o in gates)
    res['lru_w_x'] = tuple(o[:, half:].reshape(shards['lru_w_x'].shape) for o in gates)

    norm_names = ('norm_mix', 'norm_mlp', 'norm_ple', 'norm_final')

    def norm_block(src):
        cat = jnp.concatenate([src[n].reshape(-1, d) for n in norm_names], axis=0)
        return _pad_to(cat, -(-cat.shape[0] // HALO) * HALO)

    rfull = {n: (rgrads[n] if n == 'norm_final' else jnp.stack(rgrads[n], axis=0)) for n in norm_names}
    parts_norm, = _gather_many([norm_block(rfull)], ['stack'], 'gather_norm_grads', after=all_updated())
    norms = _adamw_sum(norm_block(shards)[None], [parts_norm], norm_block(moms)[None], norm_block(vels)[None],
                       'adamw_norms')
    row = 0
    for n in norm_names:
        k = shards[n].size // d
        res[n] = tuple(o[0, row:row + k].reshape(shards[n].shape) for o in norms)
        row += k

    return (total_loss[0], grad_x.reshape(x.shape), *[res[n][0] for n in WEIGHTS], *[res[n][1] for n in WEIGHTS],
            *[res[n][2] for n in WEIGHTS], *[res[n][3] for n in WEIGHTS])
```

```python
import functools
import math

import jax
import jax.numpy as jnp
from jax import lax
from jax.experimental import pallas as pl
from jax.experimental.pallas import tpu as pltpu

F32 = jnp.float32
BF16 = jnp.bfloat16
SDS = jax.ShapeDtypeStruct

N_DEV = 8
RMS_EPS = 1e-6
N_MIXERS = 3
HEAD_DIM = 128
DILATED_PATTERNS = ((128, 1), (512, 4), (2048, 16))
ATTN_BLOCK = 128
ROPE_THETA = 500000.0
ROPE_DIM = HEAD_DIM // 4
LRU_BLOCK = 128
LRU_C = 8.0
ADAM_LR, ADAM_B1, ADAM_B2, ADAM_EPS, ADAM_WD, ADAM_STEP = 0.001, 0.9, 0.999, 1e-08, 0.01, 10

HALO = 16
SUB = 8
VMEM_LIMIT = 56 * 1024 * 1024
NEG = -1e30

SHARD_AXIS = {
    'norm_mix': None, 'norm_mlp': None, 'norm_ple': None, 'norm_final': None,
    'sc_w_in': 2, 'sc_w_conv': 2, 'sc_w_out': 1, 'attn_w_qkv': 2, 'attn_w_o': 1,
    'lru_w_in': 2, 'lru_conv_w': 2, 'lru_conv_b': 1, 'lru_w_a': None, 'lru_b_a': 1,
    'lru_w_x': None, 'lru_b_x': 1, 'lru_lambda': 1, 'lru_w_out': 1,
    'mlp_w_up': 2, 'mlp_w_down': 1, 'ple_w_gate': 1, 'ple_w_proj': 2,
}
WEIGHTS = list(SHARD_AXIS)


def _params(*sem):
    return pltpu.CompilerParams(dimension_semantics=sem or None, vmem_limit_bytes=VMEM_LIMIT)


def _row_tile(t, pref=256):
    tr = min(t, pref)
    assert t % tr == 0 and tr % HALO == 0
    return tr


def _row(tr, c, col=0):
    return pl.BlockSpec((tr, c), lambda i, col=col: (i, col))


def _full(shape):
    return pl.BlockSpec(shape, lambda *_: (0,) * len(shape))


def _sigmoid(x):
    return 1.0 / (1.0 + jnp.exp(-x))


def _expm1(x):
    taylor = x * (1.0 + x * (0.5 + x * (1.0 / 6.0 + x * (1.0 / 24.0 + x * (1.0 / 120.0)))))
    return jnp.where(jnp.abs(x) < 0.1, taylor, jnp.exp(x) - 1.0)


def _softplus(x):
    z = jnp.exp(-jnp.abs(x))
    log1p = jnp.where(z < 0.01, z * (1.0 - z * (0.5 - z * (1.0 / 3.0 - z * 0.25))), jnp.log(1.0 + z))
    return jnp.maximum(x, 0.0) + log1p


_GELU_K = math.sqrt(2.0 / math.pi)


def _gelu_and_grad(x):
    inner = _GELU_K * (x + 0.044715 * x * x * x)
    th = jnp.tanh(inner)
    g = 0.5 * x * (1.0 + th)
    dg = 0.5 * (1.0 + th) + 0.5 * x * (1.0 - th * th) * _GELU_K * (1.0 + 3.0 * 0.044715 * x * x)
    return g, dg


def _shift_down(x, k, prev):
    row = lax.broadcasted_iota(jnp.int32, (SUB, x.shape[1]), 0)
    xr = pltpu.roll(x, k, 0)
    top = jnp.where(row < k, pltpu.roll(prev, k, 0), xr[0:SUB])
    return jnp.concatenate([top, xr[SUB:]], axis=0)


def _shift_up(x, k, nxt):
    r = x.shape[0]
    row = lax.broadcasted_iota(jnp.int32, (SUB, x.shape[1]), 0)
    xr = pltpu.roll(x, r - k, 0)
    bot = jnp.where(row >= SUB - k, pltpu.roll(nxt, SUB - k, 0), xr[r - SUB:r])
    return jnp.concatenate([xr[:r - SUB], bot], axis=0)


_DIMS = {'nn': (((1,), (0,)), ((), ())), 'nt': (((1,), (1,)), ((), ())), 'tn': (((0,), (0,)), ((), ()))}


MM_VMEM_BUDGET = 50 * 1024 * 1024
MM_MIN_TK = 1024
MM_MIN_TM = 1024


def _tile_options(dim):
    return [c for c in range(dim, 127, -128) if dim % c == 0] or [dim]


def _choose_tiles(m, n, k, n_span, k_span, a_size, b_size, mn_size, a_temp):
    best = None
    for tm in _tile_options(m):
        for tn in _tile_options(n_span):
            for tk in _tile_options(k_span):
                nk = k // tk
                need = (2 * (tm * tk * a_size + tk * tn * b_size + tm * tn * mn_size) + tm * tn * 4 * (1 + (nk > 1))
                        + tm * tk * 4 * a_temp)
                score = (-min(tk, MM_MIN_TK), -min(tm, MM_MIN_TM), -tm * tn, nk, -min(tm, 2 * MM_MIN_TM), -tn)
                if need <= MM_VMEM_BUDGET and (best is None or score < best[0]):
                    best = (score, (tm, tn, tk))
    return best[1]


def _mm(a, b, dims, name, out_dtypes=(F32,), a_pro=None, extras=(), epi=None, out_stacked=False, dep=None):
    deps = [] if dep is None else [dep]
    stacked = b.ndim == 3
    b_rows, b_cols = (b.shape[1], N_DEV * b.shape[2]) if stacked else b.shape
    if dims == 'nn':
        (m, k), (k2, n) = a.shape, (b_rows, b_cols)
    elif dims == 'nt':
        (m, k), (n, k2) = a.shape, (b_rows, b_cols)
    else:
        (k, m), (k2, n) = a.shape, (b_rows, b_cols)
    assert k == k2, (name, a.shape, b.shape)
    assert not (stacked and dims == 'tn') and not (out_stacked and (extras or dims != 'tn'))
    tm, tn, tk = _choose_tiles(
        m, n, k, n // N_DEV if (out_stacked or (stacked and dims == 'nn')) else n,
        k // N_DEV if (stacked and dims == 'nt') else k, a.dtype.itemsize, b.dtype.itemsize,
        sum(e.dtype.itemsize for e in extras) + sum(jnp.dtype(dt).itemsize for dt in out_dtypes),
        a_pro is not None or a.dtype != BF16)
    assert m % tm == 0 and n % tn == 0 and k % tk == 0, (name, m, n, k)
    nk = k // tk
    a_spec = pl.BlockSpec((tk, tm), lambda i, j, kk: (kk, i)) if dims == 'tn' else pl.BlockSpec((tm, tk), lambda i, j, kk: (i, kk))
    if not stacked:
        b_spec = pl.BlockSpec((tn, tk), lambda i, j, kk: (j, kk)) if dims == 'nt' else pl.BlockSpec((tk, tn), lambda i, j, kk: (kk, j))
    elif dims == 'nn':
        per = b.shape[2] // tn
        b_spec = pl.BlockSpec((None, tk, tn), lambda i, j, kk: (j // per, kk, j % per))
    else:
        per = b.shape[2] // tk
        b_spec = pl.BlockSpec((None, tn, tk), lambda i, j, kk: (kk // per, j, kk % per))
    if out_stacked:
        per_o = n // N_DEV // tn
        o_spec = pl.BlockSpec((None, tm, tn), lambda i, j, kk: (j // per_o, i, j % per_o))
        o_shape = (N_DEV, m, n // N_DEV)
    else:
        o_spec = pl.BlockSpec((tm, tn), lambda i, j, kk: (i, j))
        o_shape = (m, n)
    n_ex, n_out = len(extras), len(out_dtypes)
    for e in extras:
        assert e.shape == (m, n), (name, e.shape)

    def body(a_ref, b_ref, *rest):
        rest = rest[len(deps):]
        ex_refs, out_refs = rest[:n_ex], rest[n_ex:n_ex + n_out]
        kk = pl.program_id(2)
        av = a_ref[...]
        if a_pro is not None:
            av = a_pro(av.astype(F32))
        part = lax.dot_general(av.astype(BF16), b_ref[...].astype(BF16), _DIMS[dims], preferred_element_type=F32)

        def finish(res):
            outs = (res,) if epi is None else epi(res, *[e[...] for e in ex_refs])
            for o_ref, o in zip(out_refs, outs):
                o_ref[...] = o.astype(o_ref.dtype)

        if nk == 1:
            finish(part)
        else:
            acc = rest[-1]

            @pl.when(kk == 0)
            def _():
                acc[...] = part

            @pl.when(kk > 0)
            def _():
                acc[...] += part

            @pl.when(kk == nk - 1)
            def _():
                finish(acc[...])

    out = pl.pallas_call(
        body, grid=(m // tm, n // tn, nk),
        in_specs=[a_spec, b_spec] + [_ANY] * len(deps) + [o_spec] * n_ex,
        out_specs=[o_spec] * n_out,
        out_shape=[SDS(o_shape, d) for d in out_dtypes],
        scratch_shapes=[] if nk == 1 else [pltpu.VMEM((tm, tn), F32)],
        compiler_params=_params('parallel', 'parallel', 'arbitrary'), name=name)(a, b, *deps, *extras)
    return out[0] if n_out == 1 else out


def _relu2(u):
    r = jnp.maximum(u, 0.0)
    return r * r


STREAM_ROWS = 512


def _rms_fwd(h, g, name):
    t, d = h.shape
    tr = _row_tile(t, STREAM_ROWS)

    def body(h_ref, g_ref, o_ref):
        x = h_ref[...]
        r = lax.rsqrt(jnp.mean(x * x, axis=-1, keepdims=True) + RMS_EPS)
        o_ref[...] = (x * r * g_ref[...]).astype(o_ref.dtype)

    return pl.pallas_call(body, grid=(t // tr,), in_specs=[_row(tr, d), _full((1, d))], out_specs=_row(tr, d),
                          out_shape=SDS((t, d), BF16), compiler_params=_params('parallel'), name=name)(h, g.reshape(1, d))


def _rms_bwd(h, g, dhn, dres, name):
    t, d = h.shape
    tr = _row_tile(t, STREAM_ROWS)

    def body(h_ref, g_ref, dhn_ref, dres_ref, dh_ref, dg_ref):
        @pl.when(pl.program_id(0) == 0)
        def _():
            dg_ref[...] = jnp.zeros_like(dg_ref)

        x = h_ref[...]
        r = lax.rsqrt(jnp.mean(x * x, axis=-1, keepdims=True) + RMS_EPS)
        dy = dhn_ref[...].astype(F32)
        gy = dy * g_ref[...]
        dx = r * gy - x * (r * r * r) * jnp.mean(gy * x, axis=-1, keepdims=True)
        dh_ref[...] = dres_ref[...] + dx
        dg_ref[...] += jnp.sum(dy * (x * r), axis=0, keepdims=True)

    return pl.pallas_call(body, grid=(t // tr,),
                          in_specs=[_row(tr, d), _full((1, d)), _row(tr, d), _row(tr, d)],
                          out_specs=[_row(tr, d), _full((1, d))],
                          out_shape=[SDS((t, d), F32), SDS((1, d), F32)],
                          compiler_params=_params('arbitrary'), name=name)(h, g.reshape(1, d), dhn, dres)


def _head(h, g, target, name):
    t, d = h.shape
    tr = _row_tile(t, STREAM_ROWS)

    def body(h_ref, g_ref, t_ref, dh_ref, loss_ref, dg_ref):
        @pl.when(pl.program_id(0) == 0)
        def _():
            dg_ref[...] = jnp.zeros_like(dg_ref)
            loss_ref[...] = jnp.zeros_like(loss_ref)

        x = h_ref[...]
        gv = g_ref[...]
        r = lax.rsqrt(jnp.mean(x * x, axis=-1, keepdims=True) + RMS_EPS)
        xh = x * r
        e = xh * gv - t_ref[...]
        per_tok = jnp.mean(e * e, axis=-1, keepdims=True)
        loss_ref[...] += jnp.broadcast_to(0.5 * jnp.sum(per_tok, axis=0, keepdims=True), loss_ref.shape)
        dy = e * (1.0 / d)
        gy = dy * gv
        dh_ref[...] = r * gy - x * (r * r * r) * jnp.mean(gy * x, axis=-1, keepdims=True)
        dg_ref[...] += jnp.sum(dy * xh, axis=0, keepdims=True)

    return pl.pallas_call(body, grid=(t // tr,),
                          in_specs=[_row(tr, d), _full((1, d)), _row(tr, d)],
                          out_specs=[_row(tr, d), _full((1, 128)), _full((1, d))],
                          out_shape=[SDS((t, d), F32), SDS((1, 128), F32), SDS((1, d), F32)],
                          compiler_params=_params('arbitrary'), name=name)(h, g.reshape(1, d), target)


def _ple_bwd_gate(dh3, gate, pp, name):
    t, d = dh3.shape
    tr = _row_tile(t, STREAM_ROWS)

    def body(dh_ref, g_ref, pp_ref, dpp_ref, dgl_ref):
        dh = dh_ref[...]
        gt = g_ref[...].astype(F32)
        dpp_ref[...] = (dh * gt).astype(dpp_ref.dtype)
        dgl_ref[...] = (dh * pp_ref[...].astype(F32) * gt * (1.0 - gt)).astype(dgl_ref.dtype)

    return pl.pallas_call(body, grid=(t // tr,), in_specs=[_row(tr, d)] * 3, out_specs=[_row(tr, d)] * 2,
                          out_shape=[SDS((t, d), BF16), SDS((t, d), BF16)],
                          compiler_params=_params('parallel'), name=name)(dh3, gate, pp)


def _halo_prev(tr, c, col=0):
    return pl.BlockSpec((HALO, c), lambda i, col=col: (jnp.maximum(i * (tr // HALO) - 1, 0), col))


def _halo_next(tr, c, t, col=0):
    return pl.BlockSpec((HALO, c), lambda i, col=col: (jnp.minimum((i + 1) * (tr // HALO), t // HALO - 1), col))


def _sc_fwd(z, w, name):
    t, c3 = z.shape
    c = c3 // 3
    tr = _row_tile(t)

    def body(z_ref, zp_ref, w_ref, y_ref):
        i = pl.program_id(0)
        zz = z_ref[...]
        gb, cx = zz[:, :c], zz[:, c:2 * c] * zz[:, 2 * c:]
        zp = zp_ref[SUB:HALO, :]
        cxp = jnp.where(i > 0, zp[:, c:2 * c] * zp[:, 2 * c:], 0.0)
        wv = w_ref[...]
        conv = wv[2:3] * cx + wv[1:2] * _shift_down(cx, 1, cxp) + wv[0:1] * _shift_down(cx, 2, cxp)
        y_ref[...] = (gb * conv).astype(y_ref.dtype)

    return pl.pallas_call(body, grid=(t // tr,),
                          in_specs=[_row(tr, c3), _halo_prev(tr, c3), _full((3, c))],
                          out_specs=_row(tr, c), out_shape=SDS((t, c), BF16),
                          compiler_params=_params('parallel'), name=name)(z, z, w)


def _sc_bwd(dy, z, w, name):
    t, c3 = z.shape
    c = c3 // 3
    tr = _row_tile(t)
    nt = t // tr

    def body(dy_ref, dyn_ref, z_ref, zp_ref, zn_ref, w_ref, dz_ref, dw_ref):
        i = pl.program_id(0)

        @pl.when(i == 0)
        def _():
            dw_ref[...] = jnp.zeros_like(dw_ref)

        zz = z_ref[...]
        gb, gc, xi = zz[:, :c], zz[:, c:2 * c], zz[:, 2 * c:]
        cx = gc * xi
        zp = zp_ref[SUB:HALO, :]
        cxp = jnp.where(i > 0, zp[:, c:2 * c] * zp[:, 2 * c:], 0.0)
        wv = w_ref[...]
        cx1, cx2 = _shift_down(cx, 1, cxp), _shift_down(cx, 2, cxp)
        conv = wv[2:3] * cx + wv[1:2] * cx1 + wv[0:1] * cx2
        dyv = dy_ref[...]
        dconv = dyv * gb
        dcn = jnp.where(i < nt - 1, dyn_ref[0:SUB, :] * zn_ref[0:SUB, :c], 0.0)
        dcx = wv[2:3] * dconv + wv[1:2] * _shift_up(dconv, 1, dcn) + wv[0:1] * _shift_up(dconv, 2, dcn)
        dz_ref[:, :c] = (dyv * conv).astype(dz_ref.dtype)
        dz_ref[:, c:2 * c] = (dcx * xi).astype(dz_ref.dtype)
        dz_ref[:, 2 * c:] = (dcx * gc).astype(dz_ref.dtype)
        dw_ref[...] += jnp.concatenate([jnp.sum(dconv * cx2, axis=0, keepdims=True),
                                        jnp.sum(dconv * cx1, axis=0, keepdims=True),
                                        jnp.sum(dconv * cx, axis=0, keepdims=True)], axis=0)

    return pl.pallas_call(body, grid=(nt,),
                          in_specs=[_row(tr, c), _halo_next(tr, c, t), _row(tr, c3), _halo_prev(tr, c3),
                                    _halo_next(tr, c3, t), _full((3, c))],
                          out_specs=[_row(tr, c3), _full((3, c))],
                          out_shape=[SDS((t, c3), BF16), SDS((3, c), F32)],
                          compiler_params=_params('arbitrary'), name=name)(dy, dy, z, z, z, w)


def _perm(tr, dil, inverse=False):
    n = tr // dil
    a = lax.broadcasted_iota(jnp.int32, (tr, tr), 1 if inverse else 0)
    b = lax.broadcasted_iota(jnp.int32, (tr, tr), 0 if inverse else 1)
    return (b == (a % n) * dil + a // n).astype(BF16)


def _permute(pm, x, terms):
    if x.dtype == BF16:
        return jnp.dot(pm, x, preferred_element_type=F32)
    acc = None
    for _ in range(terms):
        part = x.astype(BF16)
        y = jnp.dot(pm, part, preferred_element_type=F32)
        acc = y if acc is None else acc + y
        x = x - part.astype(F32)
    return acc


def _store_dilated(o_ref, y, dil, d):
    n = y.shape[0] // dil
    for rho in range(dil):
        o_ref[:, rho * d:(rho + 1) * d] = y[rho * n:(rho + 1) * n].astype(o_ref.dtype)


def _load_dilated(ref, dil, d):
    return jnp.concatenate([ref[:, rho * d:(rho + 1) * d] for rho in range(dil)], axis=0) if dil > 1 else ref[...]


def _rope_heads(x, lane, cos, sin):
    return jnp.concatenate([_rope_apply(x[:, s:s + HEAD_DIM], lane, cos, sin)
                            for s in range(0, x.shape[1], HEAD_DIM)], axis=1)


def _rope_tables(pos, invf, sign):
    lane = lax.broadcasted_iota(jnp.int32, (pos.shape[0], HEAD_DIM), 1)
    ang = pos.astype(F32) * invf
    half = ROPE_DIM // 2
    cos = jnp.where(lane < ROPE_DIM, jnp.cos(ang), 1.0)
    sin = jnp.sin(ang) * sign
    sin = jnp.where(lane < half, -sin, jnp.where(lane < ROPE_DIM, sin, 0.0))
    return lane, cos, sin


def _rope_apply(x, lane, cos, sin):
    half = ROPE_DIM // 2
    xs = jnp.where(lane < half, pltpu.roll(x, HEAD_DIM - half, 1), pltpu.roll(x, half, 1))
    return x * cos + xs * sin


def _dilated_spec(tr, dil, d):
    return pl.BlockSpec((tr // dil, dil * d), lambda i: (i, 0))


def _rope_fwd(qkv, pos, invf, dils, name):
    t, w3 = qkv.shape
    w, ng = w3 // 3, len(dils)
    d = w // ng
    tr = _row_tile(t)

    def body(q_ref, k_ref, v_ref, pos_ref, invf_ref, *out_refs):
        lane, cos, sin = _rope_tables(pos_ref[...], invf_ref[...], 1.0)
        for g, dil in enumerate(dils):
            cs = slice(g * d, (g + 1) * d)
            vals = [_rope_heads(q_ref[:, cs], lane, cos, sin).astype(BF16),
                    _rope_heads(k_ref[:, cs], lane, cos, sin).astype(BF16), v_ref[:, cs].astype(BF16)]
            if dil > 1:
                pm = _perm(tr, dil)
                vals = [_permute(pm, a, 1) for a in vals]
            for o_ref, a in zip(out_refs[g::ng], vals):
                _store_dilated(o_ref, a, dil, d)

    outs = pl.pallas_call(body, grid=(t // tr,),
                          in_specs=[_row(tr, w, 0), _row(tr, w, 1), _row(tr, w, 2), _row(tr, 1), _full((1, HEAD_DIM))],
                          out_specs=[_dilated_spec(tr, dil, d) for dil in dils] * 3,
                          out_shape=[SDS((t // dil, dil * d), BF16) for dil in dils] * 3,
                          compiler_params=_params('parallel'), name=name)(qkv, qkv, qkv, pos, invf)
    return outs[:ng], outs[ng:2 * ng], outs[2 * ng:]


def _rope_bwd(dqs, dks, dvs, pos, invf, dils, name):
    ng = len(dils)
    t = dqs[0].shape[0] * dils[0]
    d = dqs[0].shape[1] // dils[0]
    w = ng * d
    tr = _row_tile(t)

    def body(*refs):
        dq_refs, dk_refs, dv_refs = refs[:ng], refs[ng:2 * ng], refs[2 * ng:3 * ng]
        pos_ref, invf_ref, o_ref = refs[3 * ng:]
        pos_f = jnp.broadcast_to(pos_ref[...].astype(F32), (tr, HEAD_DIM))
        for g, dil in enumerate(dils):
            pos_g = pos_f if dil == 1 else _permute(_perm(tr, dil), pos_f, 3)
            lane, cos, sin = _rope_tables(pos_g, invf_ref[...], -1.0)
            vals = [_rope_heads(_load_dilated(dq_refs[g], dil, d), lane, cos, sin),
                    _rope_heads(_load_dilated(dk_refs[g], dil, d), lane, cos, sin), _load_dilated(dv_refs[g], dil, d)]
            back = _perm(tr, dil, inverse=True) if dil > 1 else None
            for sec, a in enumerate(vals):
                a = a.astype(BF16)
                if dil > 1:
                    a = _permute(back, a, 1)
                o_ref[:, sec * w + g * d:sec * w + (g + 1) * d] = a.astype(o_ref.dtype)

    return pl.pallas_call(body, grid=(t // tr,),
                          in_specs=[_dilated_spec(tr, dil, d) for dil in dils] * 3 + [_row(tr, 1), _full((1, HEAD_DIM))],
                          out_specs=_row(tr, 3 * w), out_shape=SDS((t, 3 * w), BF16),
                          compiler_params=_params('parallel'), name=name)(*dqs, *dks, *dvs, pos, invf)


def _dilate_many(arrs, dil, terms, out_dtypes, name):
    t, d = arrs[0].shape
    tr = _row_tile(t)
    na = len(arrs)

    def body(*refs):
        pm = _perm(tr, dil)
        for a_ref, o_ref, k in zip(refs[:na], refs[na:], terms):
            _store_dilated(o_ref, _permute(pm, a_ref[...], k), dil, d)

    return pl.pallas_call(body, grid=(t // tr,), in_specs=[_row(tr, d)] * na,
                          out_specs=[_dilated_spec(tr, dil, d)] * na,
                          out_shape=[SDS((t // dil, dil * d), dt) for dt in out_dtypes],
                          compiler_params=_params('parallel'), name=name)(*arrs)


def _attn_masks():
    qi = lax.broadcasted_iota(jnp.int32, (ATTN_BLOCK, ATTN_BLOCK), 0)
    kj = lax.broadcasted_iota(jnp.int32, (ATTN_BLOCK, ATTN_BLOCK), 1)
    return kj >= qi, kj <= qi


def _attn_cols(l, width):
    ncol = width // HEAD_DIM
    cpb = max(1, min(ncol, 32 // (l // ATTN_BLOCK)))
    assert ncol % cpb == 0
    return cpb


def _attn_fwd(q, k, v, name):
    l, width = q.shape
    cpb = _attn_cols(l, width)
    nb = l // ATTN_BLOCK
    scale = HEAD_DIM ** -0.5

    def body(q_ref, k_ref, v_ref, o_ref, lse_ref):
        m_prev, m_cur = _attn_masks()
        for col in range(cpb):
            cs = slice(col * HEAD_DIM, (col + 1) * HEAD_DIM)

            def step(b, carry, cs=cs):
                r0 = pl.multiple_of(b * ATTN_BLOCK, ATTN_BLOCK)
                rp = pl.multiple_of(jnp.maximum(b - 1, 0) * ATTN_BLOCK, ATTN_BLOCK)
                qb = q_ref[pl.ds(r0, ATTN_BLOCK), cs]
                s_p = lax.dot_general(qb, k_ref[pl.ds(rp, ATTN_BLOCK), cs], _DIMS['nt'], preferred_element_type=F32) * scale
                s_c = lax.dot_general(qb, k_ref[pl.ds(r0, ATTN_BLOCK), cs], _DIMS['nt'], preferred_element_type=F32) * scale
                s_p = jnp.where(jnp.logical_and(m_prev, b > 0), s_p, NEG)
                s_c = jnp.where(m_cur, s_c, NEG)
                m = jnp.maximum(jnp.max(s_p, axis=-1, keepdims=True), jnp.max(s_c, axis=-1, keepdims=True))
                p_p, p_c = jnp.exp(s_p - m), jnp.exp(s_c - m)
                den = jnp.sum(p_p, axis=-1, keepdims=True) + jnp.sum(p_c, axis=-1, keepdims=True)
                acc = jnp.dot(p_p.astype(BF16), v_ref[pl.ds(rp, ATTN_BLOCK), cs], preferred_element_type=F32)
                acc += jnp.dot(p_c.astype(BF16), v_ref[pl.ds(r0, ATTN_BLOCK), cs], preferred_element_type=F32)
                o_ref[pl.ds(r0, ATTN_BLOCK), cs] = acc / den
                lse_ref[pl.ds(r0, ATTN_BLOCK), cs] = jnp.broadcast_to(m + jnp.log(den), (ATTN_BLOCK, HEAD_DIM))
                return carry

            lax.fori_loop(0, nb, step, 0, unroll=min(nb, 4))

    spec = pl.BlockSpec((l, cpb * HEAD_DIM), lambda j: (0, j))
    return pl.pallas_call(body, grid=(width // (cpb * HEAD_DIM),), in_specs=[spec] * 3, out_specs=[spec] * 2,
                          out_shape=[SDS((l, width), F32)] * 2,
                          compiler_params=_params('parallel'), name=name)(q, k, v)


def _attn_bwd(q, k, v, do, lse, delta, name):
    l, width = q.shape
    cpb = _attn_cols(l, width)
    nb = l // ATTN_BLOCK
    scale = HEAD_DIM ** -0.5

    def body(q_ref, k_ref, v_ref, do_ref, lse_ref, dl_ref, dq_ref, dk_ref, dv_ref):
        m_prev, m_cur = _attn_masks()
        dk_ref[...] = jnp.zeros_like(dk_ref)
        dv_ref[...] = jnp.zeros_like(dv_ref)
        for col in range(cpb):
            cs = slice(col * HEAD_DIM, (col + 1) * HEAD_DIM)

            def step(b, carry, cs=cs):
                r0 = pl.multiple_of(b * ATTN_BLOCK, ATTN_BLOCK)
                rp = pl.multiple_of(jnp.maximum(b - 1, 0) * ATTN_BLOCK, ATTN_BLOCK)
                qb, dob = q_ref[pl.ds(r0, ATTN_BLOCK), cs], do_ref[pl.ds(r0, ATTN_BLOCK), cs].astype(BF16)
                kp, kc = k_ref[pl.ds(rp, ATTN_BLOCK), cs], k_ref[pl.ds(r0, ATTN_BLOCK), cs]
                vp, vc = v_ref[pl.ds(rp, ATTN_BLOCK), cs], v_ref[pl.ds(r0, ATTN_BLOCK), cs]
                lse_b = lse_ref[pl.ds(r0, ATTN_BLOCK), cs]
                dl_b = dl_ref[pl.ds(r0, ATTN_BLOCK), cs]
                s_p = lax.dot_general(qb, kp, _DIMS['nt'], preferred_element_type=F32) * scale
                s_c = lax.dot_general(qb, kc, _DIMS['nt'], preferred_element_type=F32) * scale
                p_p = jnp.exp(jnp.where(jnp.logical_and(m_prev, b > 0), s_p, NEG) - lse_b)
                p_c = jnp.exp(jnp.where(m_cur, s_c, NEG) - lse_b)
                dp_p = lax.dot_general(dob, vp, _DIMS['nt'], preferred_element_type=F32)
                dp_c = lax.dot_general(dob, vc, _DIMS['nt'], preferred_element_type=F32)
                ds_p = (p_p * (dp_p - dl_b) * scale).astype(BF16)
                ds_c = (p_c * (dp_c - dl_b) * scale).astype(BF16)
                dq_ref[pl.ds(r0, ATTN_BLOCK), cs] = (jnp.dot(ds_p, kp, preferred_element_type=F32)
                                                     + jnp.dot(ds_c, kc, preferred_element_type=F32))
                dk_ref[pl.ds(rp, ATTN_BLOCK), cs] += lax.dot_general(ds_p, qb, _DIMS['tn'], preferred_element_type=F32)
                dk_ref[pl.ds(r0, ATTN_BLOCK), cs] += lax.dot_general(ds_c, qb, _DIMS['tn'], preferred_element_type=F32)
                dv_ref[pl.ds(rp, ATTN_BLOCK), cs] += lax.dot_general(p_p.astype(BF16), dob, _DIMS['tn'], preferred_element_type=F32)
                dv_ref[pl.ds(r0, ATTN_BLOCK), cs] += lax.dot_general(p_c.astype(BF16), dob, _DIMS['tn'], preferred_element_type=F32)
                return carry

            lax.fori_loop(0, nb, step, 0, unroll=min(nb, 2))

    spec = pl.BlockSpec((l, cpb * HEAD_DIM), lambda j: (0, j))
    return pl.pallas_call(body, grid=(width // (cpb * HEAD_DIM),), in_specs=[spec] * 6, out_specs=[spec] * 3,
                          out_shape=[SDS((l, width), F32)] * 3,
                          compiler_params=_params('parallel'), name=name)(q, k, v, do, lse, delta)


def _attn_combine(os_, lses, dils, name):
    ng = len(dils)
    t = os_[0].shape[0] * dils[0]
    d = os_[0].shape[1] // dils[0]
    tr = _row_tile(t)

    def body(*refs):
        o_refs, l_refs, o_out, lse_out = refs[:ng], refs[ng:2 * ng], refs[2 * ng], refs[2 * ng + 1]
        ovs, ls = [], []
        for g, dil in enumerate(dils):
            ov, lv = _load_dilated(o_refs[g], dil, d), _load_dilated(l_refs[g], dil, d)
            if dil > 1:
                back = _perm(tr, dil, inverse=True)
                ov, lv = _permute(back, ov, 2), _permute(back, lv, 3)
            ovs.append(ov)
            ls.append(lv)
        m = functools.reduce(jnp.maximum, ls)
        ws = [jnp.exp(x - m) for x in ls]
        den = functools.reduce(lambda a, b: a + b, ws)
        acc = functools.reduce(lambda a, b: a + b, [w * o for w, o in zip(ws, ovs)])
        o_out[...] = (acc / den).astype(o_out.dtype)
        lse_out[...] = m + jnp.log(den)

    return pl.pallas_call(body, grid=(t // tr,), in_specs=[_dilated_spec(tr, dil, d) for dil in dils] * 2,
                          out_specs=[_row(tr, d)] * 2, out_shape=[SDS((t, d), BF16), SDS((t, d), F32)],
                          compiler_params=_params('parallel'), name=name)(*os_, *lses)


def _delta_epilogue(acc, o):
    prod = acc * o.astype(F32)
    segs = [jnp.broadcast_to(jnp.sum(prod[:, s:s + HEAD_DIM], axis=-1, keepdims=True), (acc.shape[0], HEAD_DIM))
            for s in range(0, acc.shape[1], HEAD_DIM)]
    return acc, jnp.concatenate(segs, axis=-1)


LRU_TILE = 128


def _lru_gates(xr, wa_ref, ba, wx_ref, bx, lam):
    nb = wa_ref.shape[0]
    xb = xr.astype(BF16)
    ra = jnp.concatenate([jnp.dot(xb[:, n * LRU_BLOCK:(n + 1) * LRU_BLOCK], wa_ref[n], preferred_element_type=F32)
                          for n in range(nb)], axis=-1) + ba
    ia = jnp.concatenate([jnp.dot(xb[:, n * LRU_BLOCK:(n + 1) * LRU_BLOCK], wx_ref[n], preferred_element_type=F32)
                          for n in range(nb)], axis=-1) + bx
    r, ig = _sigmoid(ra), _sigmoid(ia)
    sp = _softplus(-lam)
    log_a = -LRU_C * r * sp
    a = jnp.exp(log_a)
    mult = jnp.sqrt(-_expm1(2.0 * log_a))
    return xb, r, ig, sp, a, mult


def _lru_fwd(z, cw, cb, wa, ba, wx, bx, lam, name):
    t, c2 = z.shape
    c = c2 // 2
    nb = c // LRU_BLOCK
    tr = _row_tile(t, LRU_TILE)

    def body(g_ref, x_ref, xp_ref, cw_ref, cb_ref, wa_ref, ba_ref, wx_ref, bx_ref, lam_ref,
             y_ref, hs_ref, xr_ref, car_ref):
        i = pl.program_id(0)

        @pl.when(i == 0)
        def _():
            car_ref[...] = jnp.zeros_like(car_ref)

        x0 = x_ref[...]
        xp = jnp.where(i > 0, xp_ref[SUB:HALO, :], 0.0)
        cwv = cw_ref[...]
        xr = (cb_ref[...] + cwv[3:4] * x0 + cwv[2:3] * _shift_down(x0, 1, xp)
              + cwv[1:2] * _shift_down(x0, 2, xp) + cwv[0:1] * _shift_down(x0, 3, xp))
        xr_ref[...] = xr
        _, _, ig, _, a, mult = _lru_gates(xr, wa_ref, ba_ref[...], wx_ref, bx_ref[...], lam_ref[...])
        u = mult * (ig * xr)
        row = lax.broadcasted_iota(jnp.int32, (SUB, c), 0)
        car = car_ref[...]
        for j in range(tr // SUB):
            ab, ub = a[j * SUB:(j + 1) * SUB], u[j * SUB:(j + 1) * SUB]
            for s in (1, 2, 4):
                a_sh = jnp.where(row >= s, pltpu.roll(ab, s, 0), 1.0)
                u_sh = jnp.where(row >= s, pltpu.roll(ub, s, 0), 0.0)
                ub = ab * u_sh + ub
                ab = ab * a_sh
            hb = ub + ab * car
            hs_ref[j * SUB:(j + 1) * SUB, :] = hb
            car = jnp.broadcast_to(hb[SUB - 1:SUB], (SUB, c))
        car_ref[...] = car
        gl, _ = _gelu_and_grad(g_ref[...])
        y_ref[...] = (hs_ref[...] * gl).astype(y_ref.dtype)

    return pl.pallas_call(
        body, grid=(t // tr,),
        in_specs=[_row(tr, c, 0), _row(tr, c, 1), _halo_prev(tr, c, 1), _full((4, c)), _full((1, c)),
                  _full((nb, LRU_BLOCK, LRU_BLOCK)), _full((1, c)), _full((nb, LRU_BLOCK, LRU_BLOCK)), _full((1, c)), _full((1, c))],
        out_specs=[_row(tr, c)] * 3,
        out_shape=[SDS((t, c), BF16), SDS((t, c), F32), SDS((t, c), F32)],
        scratch_shapes=[pltpu.VMEM((SUB, c), F32)],
        compiler_params=_params('arbitrary'), name=name)(
            z, z, z, cw, cb.reshape(1, c), wa, ba.reshape(1, c), wx, bx.reshape(1, c), lam.reshape(1, c))


def _lru_bwd(dy, z, xr, hs, cw, wa, ba, wx, bx, lam, name):
    t, c2 = z.shape
    c = c2 // 2
    nb = c // LRU_BLOCK
    tr = _row_tile(t, LRU_TILE)
    nt = t // tr

    def rev(col=0):
        return pl.BlockSpec((tr, c), lambda i, col=col: (nt - 1 - i, col))

    def rev_prev(col=0):
        return pl.BlockSpec((HALO, c), lambda i, col=col: (jnp.maximum((nt - 1 - i) * (tr // HALO) - 1, 0), col))

    def body(dy_ref, g_ref, x_ref, xp_ref, xr_ref, hs_ref, hp_ref, cw_ref, wa_ref, ba_ref, wx_ref, bx_ref, lam_ref,
             dz_ref, dwa_ref, dwx_ref, dvec_ref, lcar_ref, ahead_ref, dxhead_ref, lam_s):
        i = pl.program_id(0)
        first_tile = i == nt - 1

        @pl.when(i == 0)
        def _():
            lcar_ref[...] = jnp.zeros_like(lcar_ref)
            ahead_ref[...] = jnp.zeros_like(ahead_ref)
            dxhead_ref[...] = jnp.zeros_like(dxhead_ref)
            dwa_ref[...] = jnp.zeros_like(dwa_ref)
            dwx_ref[...] = jnp.zeros_like(dwx_ref)
            dvec_ref[...] = jnp.zeros_like(dvec_ref)

        xrv = xr_ref[...]
        lamv = lam_ref[...]
        xb, r, ig, sp, a, mult = _lru_gates(xrv, wa_ref, ba_ref[...], wx_ref, bx_ref[...], lamv)
        hsv = hs_ref[...]
        dyv = dy_ref[...]
        gl, dgl = _gelu_and_grad(g_ref[...])
        dhs = dyv * gl
        dz_ref[:, :c] = (dyv * hsv * dgl).astype(dz_ref.dtype)

        a_next = _shift_up(a, 1, ahead_ref[...])
        row = lax.broadcasted_iota(jnp.int32, (SUB, c), 0)
        car = lcar_ref[...]
        for j in reversed(range(tr // SUB)):
            ab, ub = a_next[j * SUB:(j + 1) * SUB], dhs[j * SUB:(j + 1) * SUB]
            for s in (1, 2, 4):
                a_sh = jnp.where(row < SUB - s, pltpu.roll(ab, SUB - s, 0), 1.0)
                u_sh = jnp.where(row < SUB - s, pltpu.roll(ub, SUB - s, 0), 0.0)
                ub = ab * u_sh + ub
                ab = ab * a_sh
            lb = ub + ab * car
            lam_s[j * SUB:(j + 1) * SUB, :] = lb
            car = jnp.broadcast_to(lb[0:1], (SUB, c))
        lcar_ref[...] = car
        ahead_ref[...] = a[0:SUB]
        lmb = lam_s[...]

        hp = jnp.where(first_tile, 0.0, hp_ref[SUB:HALO, :])
        h_prev = _shift_down(hsv, 1, hp)
        d_a = lmb * h_prev
        d_mult = lmb * (ig * xrv)
        d_ixr = lmb * mult
        d_ig = d_ixr * xrv
        dxr = d_ixr * ig
        d_la = d_a * a - d_mult * (a * a) / mult
        d_r = d_la * (-LRU_C * sp)
        d_sp = jnp.sum(d_la * (-LRU_C * r), axis=0, keepdims=True)
        d_ra = d_r * r * (1.0 - r)
        d_ia = d_ig * ig * (1.0 - ig)
        d_rab, d_iab = d_ra.astype(BF16), d_ia.astype(BF16)
        parts = []
        for n in range(nb):
            cs = slice(n * LRU_BLOCK, (n + 1) * LRU_BLOCK)
            parts.append(lax.dot_general(d_rab[:, cs], wa_ref[n], _DIMS['nt'], preferred_element_type=F32)
                         + lax.dot_general(d_iab[:, cs], wx_ref[n], _DIMS['nt'], preferred_element_type=F32))
            dwa_ref[n] += lax.dot_general(xb[:, cs], d_rab[:, cs], _DIMS['tn'], preferred_element_type=F32)
            dwx_ref[n] += lax.dot_general(xb[:, cs], d_iab[:, cs], _DIMS['tn'], preferred_element_type=F32)
        dxr = dxr + jnp.concatenate(parts, axis=-1)

        cwv = cw_ref[...]
        nxt = dxhead_ref[...]
        dx0 = (cwv[3:4] * dxr + cwv[2:3] * _shift_up(dxr, 1, nxt) + cwv[1:2] * _shift_up(dxr, 2, nxt)
               + cwv[0:1] * _shift_up(dxr, 3, nxt))
        dxhead_ref[...] = dxr[0:SUB]
        dz_ref[:, c:] = dx0.astype(dz_ref.dtype)

        x0 = x_ref[...]
        xp = jnp.where(first_tile, 0.0, xp_ref[SUB:HALO, :])
        sums = [jnp.sum(d_ra, axis=0, keepdims=True), jnp.sum(d_ia, axis=0, keepdims=True),
                d_sp * (-_sigmoid(-lamv)), jnp.sum(dxr, axis=0, keepdims=True),
                jnp.sum(dxr * _shift_down(x0, 3, xp), axis=0, keepdims=True),
                jnp.sum(dxr * _shift_down(x0, 2, xp), axis=0, keepdims=True),
                jnp.sum(dxr * _shift_down(x0, 1, xp), axis=0, keepdims=True),
                jnp.sum(dxr * x0, axis=0, keepdims=True)]
        dvec_ref[...] += jnp.concatenate(sums, axis=0)

    wspec = _full((nb, LRU_BLOCK, LRU_BLOCK))
    return pl.pallas_call(
        body, grid=(nt,),
        in_specs=[rev(), rev(0), rev(1), rev_prev(1), rev(), rev(), rev_prev(), _full((4, c)),
                  wspec, _full((1, c)), wspec, _full((1, c)), _full((1, c))],
        out_specs=[pl.BlockSpec((tr, c2), lambda i: (nt - 1 - i, 0)), wspec, wspec, _full((SUB, c))],
        out_shape=[SDS((t, c2), BF16), SDS((nb, LRU_BLOCK, LRU_BLOCK), F32), SDS((nb, LRU_BLOCK, LRU_BLOCK), F32),
                   SDS((SUB, c), F32)],
        scratch_shapes=[pltpu.VMEM((SUB, c), F32), pltpu.VMEM((SUB, c), F32), pltpu.VMEM((SUB, c), F32),
                        pltpu.VMEM((tr, c), F32)],
        compiler_params=_params('arbitrary'), name=name)(
            dy, z, z, z, xr, hs, hs, cw, wa, ba.reshape(1, c), wx, bx.reshape(1, c), lam.reshape(1, c))


def _local_step(x, p, pos, target, rep, weights_for_layer, emit_grads):
    t, d = x.shape
    depth = p.shape[0]
    w = rep
    half = ROPE_DIM // 2
    invf = ROPE_THETA ** (-2.0 * jnp.arange(half, dtype=F32) / ROPE_DIM)
    invf = jnp.concatenate([invf, invf, jnp.zeros((HEAD_DIM - ROPE_DIM,), F32)]).reshape(1, HEAD_DIM)
    dils = tuple(dil for _, dil in DILATED_PATTERNS)
    saved = []
    h = x
    for i in range(depth):
        kind, j = i % N_MIXERS, i // N_MIXERS
        wl, tok = weights_for_layer(i, 'mixer', h)
        s = {'h0': h, 'wl': wl}
        hn = _rms_fwd(h, w['norm_mix'][i], f'rms_mix_fwd_{i}')
        s['hn'] = hn
        if kind == 0:
            z = _mm(hn, wl['w_in'], 'nn', f'sc_in_{i}', dep=tok)
            y = _sc_fwd(z, wl['small'], f'sc_conv_fwd_{i}')
            h1 = _mm(y, wl['w_out'], 'nn', f'sc_out_{i}', extras=(h,), epi=lambda acc, res: (acc + res,))
            s.update(z=z, y=y)
        elif kind == 1:
            qkv = _mm(hn, wl['w_in'], 'nn', f'attn_qkv_{i}', dep=tok)
            qs, ks, vs = _rope_fwd(qkv, pos, invf, dils, f'rope_fwd_{i}')
            views = list(zip(qs, ks, vs))
            os_, lses = zip(*[_attn_fwd(qg, kg, vg, f'attn_fwd_{i}_g{g}') for g, (qg, kg, vg) in enumerate(views)])
            o, lse = _attn_combine(os_, lses, dils, f'attn_combine_{i}')
            h1 = _mm(o, wl['w_out'], 'nn', f'attn_out_{i}', extras=(h,), epi=lambda acc, res: (acc + res,))
            s.update(views=views, o=o, lse=lse)
        else:
            z = _mm(hn, wl['w_in'], 'nn', f'lru_in_{i}', dep=tok)
            sm = wl['small']
            y, hs, xr = _lru_fwd(z, sm[0:4], sm[4:5], w['lru_w_a'][j], sm[5:6], w['lru_w_x'][j], sm[6:7], sm[7:8],
                                 f'lru_fwd_{i}')
            h1 = _mm(y, wl['w_out'], 'nn', f'lru_out_{i}', extras=(h,), epi=lambda acc, res: (acc + res,))
            s.update(z=z, y=y, hs=hs, xr=xr)
        s['h1'] = h1
        more, tok = weights_for_layer(i, 'mlp', h1)
        wl.update(more)
        hm = _rms_fwd(h1, w['norm_mlp'][i], f'rms_mlp_fwd_{i}')
        u = _mm(hm, wl['mlp_up'], 'nn', f'mlp_up_{i}', out_dtypes=(BF16,), dep=tok)
        h2 = _mm(u, wl['mlp_down'], 'nn', f'mlp_down_{i}', a_pro=_relu2, extras=(h1,), epi=lambda acc, res: (acc + res,))
        hp = _rms_fwd(h2, w['norm_ple'][i], f'rms_ple_fwd_{i}')
        pp = _mm(p[i], wl['ple_proj'], 'nn', f'ple_proj_{i}')
        h3, gate = _mm(hp, wl['ple_gate'], 'nn', f'ple_gate_{i}', out_dtypes=(F32, BF16), extras=(pp, h2),
                       epi=lambda acc, ppv, res: (res + _sigmoid(acc) * ppv, _sigmoid(acc)))
        s.update(hm=hm, u=u, h2=h2, hp=hp, pp=pp, gate=gate)
        saved.append(s)
        h = h3

    dh, loss, dg_final = _head(h, w['norm_final'], target, 'loss_head')
    grads = {n: [None] * depth for n in ('norm_mix', 'norm_mlp', 'norm_ple')}
    grads['norm_final'] = dg_final.reshape(d)
    started = None
    for i in reversed(range(depth)):
        kind, j = i % N_MIXERS, i // N_MIXERS
        s = saved[i]
        wl, gl = s['wl'], {}
        dpp, dgl = _ple_bwd_gate(dh, s['gate'], s['pp'], f'ple_bwd_gate_{i}')
        gl['ple_proj'] = _mm(p[i], dpp, 'tn', f'ple_dproj_{i}', out_dtypes=(BF16,), dep=started)
        gl['ple_gate'] = _mm(s['hp'], dgl, 'tn', f'ple_dgate_{i}', out_dtypes=(BF16,))
        dhp = _mm(dgl, wl['ple_gate'], 'nt', f'ple_dhp_{i}')
        dh, dg = _rms_bwd(s['h2'], w['norm_ple'][i], dhp, dh, f'rms_ple_bwd_{i}')
        grads['norm_ple'][i] = dg.reshape(d)
        du = _mm(dh, wl['mlp_down'], 'nt', f'mlp_du_{i}', out_dtypes=(BF16,), extras=(s['u'],),
                 epi=lambda acc, uv: (acc * 2.0 * jnp.maximum(uv.astype(F32), 0.0),))
        gl['mlp_down'] = _mm(s['u'], dh, 'tn', f'mlp_ddown_{i}', out_dtypes=(BF16,), a_pro=_relu2)
        gl['mlp_up'] = _mm(s['hm'], du, 'tn', f'mlp_dup_{i}', out_dtypes=(BF16,), out_stacked=True)
        dhm = _mm(du, wl['mlp_up'], 'nt', f'mlp_dhm_{i}')
        dh, dg = _rms_bwd(s['h1'], w['norm_mlp'][i], dhm, dh, f'rms_mlp_bwd_{i}')
        grads['norm_mlp'][i] = dg.reshape(d)
        started = emit_grads(i, 'mlp', gl, loss if i == depth - 1 else None)
        gl = {}
        if kind == 0:
            dy = _mm(dh, wl['w_out'], 'nt', f'sc_dy_{i}', dep=started)
            gl['w_out'] = _mm(s['y'], dh, 'tn', f'sc_dout_{i}', out_dtypes=(BF16,))
            dz, dwc = _sc_bwd(dy, s['z'], wl['small'], f'sc_conv_bwd_{i}')
            gl['small'] = dwc
            gl['w_in'] = _mm(s['hn'], dz, 'tn', f'sc_din_{i}', out_dtypes=(BF16,))
            started = emit_grads(i, 'mixer', gl)
            dhn = _mm(dz, wl['w_in'], 'nt', f'sc_dhn_{i}', dep=started)
        elif kind == 1:
            do, delta = _mm(dh, wl['w_out'], 'nt', f'attn_do_{i}', out_dtypes=(BF16, F32), extras=(s['o'],),
                            epi=_delta_epilogue, dep=started)
            gl['w_out'] = _mm(s['o'], dh, 'tn', f'attn_dwo_{i}', out_dtypes=(BF16,))
            rows_in = {1: (do, s['lse'], delta)}
            for dil in dils:
                if dil not in rows_in:
                    rows_in[dil] = _dilate_many([do, s['lse'], delta], dil, (1, 3, 3), (BF16, F32, F32),
                                                f'attn_dilate_{i}_d{dil}')
            dqs, dks, dvs = zip(*[_attn_bwd(*s['views'][g], *rows_in[dil], f'attn_bwd_{i}_g{g}')
                                  for g, dil in enumerate(dils)])
            dqkv = _rope_bwd(dqs, dks, dvs, pos, invf, dils, f'rope_bwd_{i}')
            gl['w_in'] = _mm(s['hn'], dqkv, 'tn', f'attn_dqkv_{i}', out_dtypes=(BF16,), out_stacked=True)
            started = emit_grads(i, 'mixer', gl)
            dhn = _mm(dqkv, wl['w_in'], 'nt', f'attn_dhn_{i}', dep=started)
        else:
            dy = _mm(dh, wl['w_out'], 'nt', f'lru_dy_{i}', dep=started)
            gl['w_out'] = _mm(s['y'], dh, 'tn', f'lru_dout_{i}', out_dtypes=(BF16,))
            sm = wl['small']
            dz, dwa, dwx, dvec = _lru_bwd(dy, s['z'], s['xr'], s['hs'], sm[0:4], w['lru_w_a'][j], sm[5:6],
                                          w['lru_w_x'][j], sm[6:7], sm[7:8], f'lru_bwd_{i}')
            gl['gates'], gl['small'] = (dwa, dwx), dvec
            gl['w_in'] = _mm(s['hn'], dz, 'tn', f'lru_din_{i}', out_dtypes=(BF16,))
            started = emit_grads(i, 'mixer', gl)
            dhn = _mm(dz, wl['w_in'], 'nt', f'lru_dhn_{i}', dep=started)
        dh, dg = _rms_bwd(s['h0'], w['norm_mix'][i], dhn, dh, f'rms_mix_bwd_{i}')
        grads['norm_mix'][i] = dg.reshape(d)
        started = None
    return loss, dh, grads


_MESH = pl.DeviceIdType.MESH
_ANY = pl.BlockSpec(memory_space=pl.ANY)


def _block_view(ref, kind, idx):
    if kind == 'stack':
        return ref.at[idx]
    r = ref.shape[0] // N_DEV
    return ref.at[pl.ds(idx * r, r)]


def _gather_many(arrs, kinds, name, after=None):
    n = len(arrs)
    after = [] if after is None else [after]
    out_shapes = [SDS((N_DEV,) + a.shape if kd == 'stack' else (N_DEV * a.shape[0],) + a.shape[1:], a.dtype)
                  for a, kd in zip(arrs, kinds)]

    def body(*refs):
        x_refs, out_refs = refs[:n], refs[n + len(after):2 * n + len(after)]
        send_sems, recv_sems, local_sems = refs[2 * n + len(after):]
        x, y, c = lax.axis_index('x'), lax.axis_index('y'), lax.axis_index('c')
        me, sibling = (x, y, c), (x, y, 1 - c)
        chips = [(1 - x, y), (x, 1 - y), (1 - x, 1 - y)]

        def slab(t, px, py, pc):
            return _block_view(out_refs[t], kinds[t], 4 * px + 2 * py + pc)

        def copy(t, k, block, to, src=None):
            return pltpu.make_async_remote_copy(
                src_ref=slab(t, *block) if src is None else src, dst_ref=slab(t, *block),
                send_sem=send_sems.at[7 * t + k], recv_sem=recv_sems.at[7 * t + k], device_id=to, device_id_type=_MESH)

        mine = [pltpu.make_async_copy(x_refs[t], slab(t, *me), local_sems.at[t]) for t in range(n)]
        for cp in mine:
            cp.start()
        first = [copy(t, 0, me, sibling, src=x_refs[t]) for t in range(n)]
        first += [copy(t, 1 + j, me, (*chip, c), src=x_refs[t]) for j, chip in enumerate(chips) for t in range(n)]
        for cp in first:
            cp.start()
        passed = []
        for j, chip in enumerate(chips):
            for t in range(n):
                copy(t, 1 + j, (*chip, c), me).wait_recv()
                passed.append(copy(t, 4 + j, (*chip, c), sibling))
                passed[-1].start()
        for t in range(n):
            copy(t, 0, sibling, me).wait_recv()
            for j, chip in enumerate(chips):
                copy(t, 4 + j, (*chip, 1 - c), me).wait_recv()
        for cp in first + passed:
            cp.wait_send()
        for cp in mine:
            cp.wait()

    return pl.pallas_call(
        body, out_shape=out_shapes, in_specs=[_ANY] * (n + len(after)), out_specs=[_ANY] * n,
        scratch_shapes=[pltpu.SemaphoreType.DMA((7 * n,)), pltpu.SemaphoreType.DMA((7 * n,)), pltpu.SemaphoreType.DMA((n,))],
        name=name)(*arrs, *after)


_HBM = pl.BlockSpec(memory_space=pltpu.HBM)
_SEM = pl.BlockSpec(memory_space=pltpu.SEMAPHORE)
_EFFECT = pltpu.SideEffectType.DATAFLOW_SIDE_EFFECTING


def _direct_copies(mode, kinds, src_refs, land_refs, send_sems, recv_sems):
    x, y, c = lax.axis_index('x'), lax.axis_index('y'), lax.axis_index('c')
    my_idx = 4 * x + 2 * y + c
    copies = []
    for k in range(1, N_DEV):
        px, py, pc = (1 - x if k & 4 else x, 1 - y if k & 2 else y, 1 - c if k & 1 else c)
        for t, kd in enumerate(kinds):
            if mode == 'gather':
                src, dst = src_refs[t], _block_view(land_refs[t], kd, my_idx)
            else:
                src, dst = _block_view(src_refs[t], kd, 4 * px + 2 * py + pc), land_refs[t].at[my_idx]
            copies.append(pltpu.make_async_remote_copy(
                src_ref=src, dst_ref=dst, send_sem=send_sems.at[7 * t + k - 1], recv_sem=recv_sems.at[7 * t + k - 1],
                device_id=(px, py, pc), device_id_type=_MESH))
    return copies


def _own_part(mode, kind, src, land):
    idx = 4 * lax.axis_index('x') + 2 * lax.axis_index('y') + lax.axis_index('c')
    zeros = (0,) * (src.ndim - 1)
    if mode == 'gather':
        part = src
    elif kind == 'stack':
        part = lax.dynamic_index_in_dim(src, idx, 0, keepdims=False)
    else:
        r = src.shape[0] // N_DEV
        part = lax.dynamic_slice_in_dim(src, idx * r, r, 0)
    if mode == 'gather' and kind == 'rows':
        return lax.dynamic_update_slice(land, part, (idx * part.shape[0],) + zeros)
    return lax.dynamic_update_slice(land, part[None], (idx,) + (0,) * part.ndim)


def _send_start(mode, srcs, kinds, name, after=None):
    n = len(srcs)
    after = [] if after is None else [after]
    lands = []
    for a, kd in zip(srcs, kinds):
        if mode == 'gather':
            shape = (N_DEV,) + a.shape if kd == 'stack' else (N_DEV * a.shape[0],) + a.shape[1:]
        else:
            shape = a.shape if kd == 'stack' else (N_DEV, a.shape[0] // N_DEV) + a.shape[1:]
        lands.append(_own_part(mode, kd, a, lax.empty(shape, a.dtype)))

    def body(*refs):
        src_refs, land_refs = refs[:n], refs[n:2 * n]
        send_sems, recv_sems = refs[2 * n + len(after):2 * n + len(after) + 2]
        token = refs[-1]
        for cp in _direct_copies(mode, kinds, src_refs, land_refs, send_sems, recv_sems):
            cp.start()
        token[...] = jnp.zeros_like(token)

    outs = pl.pallas_call(
        body, name=name,
        out_shape=(pltpu.SemaphoreType.DMA((7 * n,)), pltpu.SemaphoreType.DMA((7 * n,)),
                   *[pltpu.HBM(a.shape, a.dtype) for a in srcs + lands], SDS((SUB, 128), F32)),
        in_specs=[_HBM] * (2 * n) + [_ANY] * len(after),
        out_specs=(_SEM, _SEM, *[_HBM] * (2 * n), pl.BlockSpec(memory_space=pltpu.VMEM)),
        input_output_aliases={i: 2 + i for i in range(2 * n)},
        compiler_params=pltpu.CompilerParams(has_side_effects=_EFFECT),
    )(*[pltpu.with_memory_space_constraint(a, pltpu.HBM) for a in srcs + lands], *after)
    return (outs[0], outs[1], list(outs[2:2 + 2 * n])), outs[-1]


def _send_wait(mode, flight, kinds, after, name):
    send, recv, bufs = flight
    n = len(kinds)

    def body(*refs):
        src_refs, land_refs, (send_sems, recv_sems) = refs[:n], refs[n:2 * n], refs[2 * n:2 * n + 2]
        copies = _direct_copies(mode, kinds, src_refs, land_refs, send_sems, recv_sems)
        for cp in copies:
            cp.wait_send()
        for cp in copies:
            cp.wait_recv()

    outs = pl.pallas_call(
        body, name=name, out_shape=[pltpu.HBM(a.shape, a.dtype) for a in bufs],
        in_specs=[_HBM] * (2 * n) + [_SEM, _SEM, _ANY], out_specs=[_HBM] * (2 * n),
        input_output_aliases={i: i for i in range(2 * n)},
        compiler_params=pltpu.CompilerParams(has_side_effects=_EFFECT),
    )(*bufs, send, recv, after)
    return list(outs[n:])


ADAMW_BLOCK_ELEMS = 128 * 1024


def _adamw_sum(wgt, parts, m, v, name):
    nl, r, c = wgt.shape
    assert len(parts) == nl and all(q.shape == (N_DEV, r, c) for q in parts), (name, wgt.shape, [q.shape for q in parts])
    tr = next((t for t in range(min(r, 512), 0, -16) if r % t == 0 and t * c <= ADAMW_BLOCK_ELEMS and t % 16 == 0), r)
    c1 = 1.0 - ADAM_B1 ** ADAM_STEP
    c2 = 1.0 - ADAM_B2 ** ADAM_STEP

    def body(w_ref, m_ref, v_ref, *rest):
        part_refs, (g_ref, d_ref, mo_ref, vo_ref) = rest[:nl], rest[nl:]
        for q in range(nl):
            @pl.when(pl.program_id(0) == q)
            def _(q=q):
                gv = part_refs[q][0].astype(F32)
                for s in range(1, N_DEV):
                    gv = gv + part_refs[q][s].astype(F32)
                mn = ADAM_B1 * m_ref[...] + (1.0 - ADAM_B1) * gv
                vn = ADAM_B2 * v_ref[...] + (1.0 - ADAM_B2) * (gv * gv)
                g_ref[...] = gv
                d_ref[...] = -ADAM_LR * ((mn / c1) / (jnp.sqrt(vn / c2) + ADAM_EPS) + ADAM_WD * w_ref[...])
                mo_ref[...] = mn
                vo_ref[...] = vn

    spec = pl.BlockSpec((None, tr, c), lambda l, i: (l, i, 0))
    part_specs = [pl.BlockSpec((N_DEV, tr, c), lambda l, i, q=q: (0, jnp.where(l == q, i, 0), 0)) for q in range(nl)]
    return pl.pallas_call(body, grid=(nl, r // tr), in_specs=[spec] * 3 + part_specs, out_specs=[spec] * 4,
                          out_shape=[SDS((nl, r, c), F32)] * 4, compiler_params=_params('arbitrary', 'arbitrary'),
                          name=name)(wgt, m, v, *parts)


MIXER_WEIGHTS = {0: ('sc_w_in', 'sc_w_out'), 1: ('attn_w_qkv', 'attn_w_o'), 2: ('lru_w_in', 'lru_w_out')}
STACKED_OPERANDS = ('attn_w_qkv', 'mlp_w_up')
LRU_SMALL = ('lru_conv_w', 'lru_conv_b', 'lru_b_a', 'lru_b_x', 'lru_lambda')


def _layer_items(i):
    w_in, w_out = MIXER_WEIGHTS[i % N_MIXERS]
    j = i // N_MIXERS
    return [('w_in', w_in, j), ('w_out', w_out, j), ('mlp_up', 'mlp_w_up', i), ('mlp_down', 'mlp_w_down', i),
            ('ple_gate', 'ple_w_gate', i), ('ple_proj', 'ple_w_proj', i)]


def _cols_to_full(stacked):
    return jnp.moveaxis(stacked, 0, 1).reshape(stacked.shape[1], -1)


def _full_to_cols(full):
    k, n = full.shape
    return jnp.moveaxis(full.reshape(k, N_DEV, n // N_DEV), 1, 0)


def _pad_to(a, rows):
    return jnp.pad(a, ((0, rows - a.shape[0]), (0, 0)))


def _small_block(src, i):
    kind, j = i % N_MIXERS, i // N_MIXERS
    if kind == 0:
        return _pad_to(src['sc_w_conv'][j], SUB)
    if kind == 2:
        return jnp.concatenate([src[n][j].reshape(-1, src[n].shape[-1]) for n in LRU_SMALL], axis=0)
    return None


def kernel(x, p, positions, norm_mix, norm_mlp, norm_ple, norm_final, sc_w_in, sc_w_conv, sc_w_out, attn_w_qkv, attn_w_o, lru_w_in, lru_conv_w, lru_conv_b, lru_w_a, lru_b_a, lru_w_x, lru_b_x, lru_lambda, lru_w_out, mlp_w_up, mlp_w_down, ple_w_gate, ple_w_proj, loss_target, m_norm_mix, m_norm_mlp, m_norm_ple, m_norm_final, m_sc_w_in, m_sc_w_conv, m_sc_w_out, m_attn_w_qkv, m_attn_w_o, m_lru_w_in, m_lru_conv_w, m_lru_conv_b, m_lru_w_a, m_lru_b_a, m_lru_w_x, m_lru_b_x, m_lru_lambda, m_lru_w_out, m_mlp_w_up, m_mlp_w_down, m_ple_w_gate, m_ple_w_proj, v_norm_mix, v_norm_mlp, v_norm_ple, v_norm_final, v_sc_w_in, v_sc_w_conv, v_sc_w_out, v_attn_w_qkv, v_attn_w_o, v_lru_w_in, v_lru_conv_w, v_lru_conv_b, v_lru_w_a, v_lru_b_a, v_lru_w_x, v_lru_b_x, v_lru_lambda, v_lru_w_out, v_mlp_w_up, v_mlp_w_down, v_ple_w_gate, v_ple_w_proj):
    loc = dict(locals())
    shards = {n: loc[n] for n in WEIGHTS}
    moms = {n: loc['m_' + n] for n in WEIGHTS}
    vels = {n: loc['v_' + n] for n in WEIGHTS}

    depth, t, d = p.shape[0], x.shape[1], x.shape[2]

    def comm_kind(name):
        return 'stack' if SHARD_AXIS[name] == 2 else 'rows'

    part_keys = {'mlp': ('mlp_up', 'mlp_down', 'ple_gate', 'ple_proj'), 'mixer': ('w_in', 'w_out')}
    halves = [(i, part) for i in range(depth) for part in ('mixer', 'mlp')]

    def half_shards(i, part):
        items = [it for it in _layer_items(i) if it[0] in part_keys[part]]
        arrs = [shards[n][idx].astype(BF16) for _, n, idx in items]
        kinds = [comm_kind(n) for _, n, _ in items]
        small = _small_block(shards, i) if part == 'mixer' else None
        if small is not None:
            arrs.append(small)
            kinds.append('stack')
        return items, arrs, kinds

    def half_weights(i, items, kinds, outs):
        wl = {key: (_cols_to_full(o) if kd == 'stack' and n not in STACKED_OPERANDS else o)
              for (key, n, _), kd, o in zip(items, kinds, outs)}
        if len(outs) > len(items):
            wl['small'] = _cols_to_full(outs[-1])[:shards['sc_w_conv'].shape[1] if i % N_MIXERS == 0 else SUB]
        return wl

    first = [half_shards(0, part) for part in ('mixer', 'mlp')]
    outs0 = _gather_many(first[0][1] + first[1][1], first[0][2] + first[1][2], 'gather_weights_0')
    weights0 = {**half_weights(0, first[0][0], first[0][2], outs0[:len(first[0][1])]),
                **half_weights(0, first[1][0], first[1][2], outs0[len(first[0][1]):])}
    pending = {}

    def start_gather(pos, after):
        if pos >= len(halves):
            return None
        i, part = halves[pos]
        items, arrs, kinds = half_shards(i, part)
        flight, token = _send_start('gather', arrs, kinds, f'gather_weights_start_{part}_{i}', after=after)
        pending[pos] = (items, kinds, flight)
        return token

    first_token = start_gather(2, outs0[0])
    second_token = start_gather(3, first_token)

    def weights_for_layer(i, part, h):
        pos = halves.index((i, part))
        if pos == 0:
            return weights0, second_token
        if pos == 1:
            return {}, None
        items, kinds, flight = pending.pop(pos)
        outs = _send_wait('gather', flight, kinds, h, f'gather_weights_wait_{part}_{i}')
        return half_weights(i, items, kinds, outs), start_gather(pos + 2, outs[0])

    exchanges, gate_gathers, total_loss = {}, {}, []

    def gate_block(src, j):
        return jnp.concatenate([src[n][j].reshape(-1, LRU_BLOCK) for n in ('lru_w_a', 'lru_w_x')], axis=0)

    def emit_grads(i, part, gl, loss=None):
        after = None
        if loss is not None:
            total_loss.append(lax.psum(loss[0, 0], ('x', 'y', 'c')))
            after = jnp.full((SUB, 128), total_loss[0], F32)
        if 'gates' in gl:
            blk = gate_block({'lru_w_a': [gl['gates'][0]], 'lru_w_x': [gl['gates'][1]]}, 0)
            gate_gathers[i] = _send_start('gather', [blk], ['stack'], f'gather_gate_grads_start_{i}')[0]
        items = [it for it in _layer_items(i) if it[0] in part_keys[part]]
        kinds = [comm_kind(n) for _, n, _ in items]
        arrs = [_full_to_cols(gl[key]) if kd == 'stack' and gl[key].ndim == 2 else gl[key]
                for (key, _, _), kd in zip(items, kinds)]
        if part == 'mixer' and i % N_MIXERS == 0:
            arrs.append(_full_to_cols(_pad_to(gl['small'], SUB)))
        elif part == 'mixer' and i % N_MIXERS == 2:
            dv = gl['small']
            arrs.append(_full_to_cols(jnp.concatenate([dv[4:8], dv[3:4], dv[0:1], dv[1:2], dv[2:3]], axis=0)))
        kinds += ['stack'] * (len(arrs) - len(kinds))
        flight, token = _send_start('exchange', arrs, kinds, f'exchange_grads_start_{part}_{i}', after=after)
        exchanges[(i, part)] = (items, kinds, flight)
        return token

    rep = {n: shards[n] for n in ('norm_mix', 'norm_mlp', 'norm_ple', 'norm_final')}
    rep['lru_w_a'], rep['lru_w_x'] = shards['lru_w_a'].astype(BF16), shards['lru_w_x'].astype(BF16)
    loss, grad_x, rgrads = _local_step(x.reshape(t, d), p.reshape(depth, t, p.shape[3]), positions.reshape(t, 1),
                                       loss_target.reshape(t, d), rep, weights_for_layer, emit_grads)

    received, res = {}, {}

    def finish_exchange(key, after):
        items, kinds, flight = exchanges[key]
        outs = _send_wait('exchange', flight, kinds, after, f'exchange_grads_wait_{key[1]}_{key[0]}')
        for (_, n, idx), o in zip(items, outs):
            received[(n, idx)] = o
        if len(outs) > len(items):
            received[('small', key[0])] = outs[-1]

    def big_adamw(names):
        for n in names:
            res[n] = _adamw_sum(shards[n], [received[(n, l)] for l in range(shards[n].shape[0])], moms[n], vels[n],
                                f'adamw_{n}')

    last = (0, 'mixer')
    for key in exchanges:
        if key != last:
            finish_exchange(key, grad_x)
    big = [n for n in WEIGHTS if SHARD_AXIS[n] is not None and shards[n].ndim == 3 and n not in ('sc_w_conv', 'lru_conv_w')]
    late = [n for n in big if n in MIXER_WEIGHTS[0]]
    big_adamw([n for n in big if n not in late])
    finish_exchange(last, jnp.full((SUB, 128), sum(r[0].reshape(-1)[0] for r in res.values()), F32))
    big_adamw(late)


    def small_adamw(layers, name):
        w_, m_, v_ = (jnp.stack([_small_block(src, i) for i in layers]) for src in (shards, moms, vels))
        return _adamw_sum(w_, [received[('small', i)] for i in layers], m_, v_, name)

    sc = small_adamw([i for i in range(depth) if i % N_MIXERS == 0], 'adamw_sc_w_conv')
    res['sc_w_conv'] = tuple(o[:, :shards['sc_w_conv'].shape[1]] for o in sc)
    lru = small_adamw([i for i in range(depth) if i % N_MIXERS == 2], 'adamw_lru_small')
    row = 0
    for n in LRU_SMALL:
        k = shards[n].size // shards[n].shape[0] // shards[n].shape[-1]
        res[n] = tuple(o[:, row:row + k].reshape(shards[n].shape) for o in lru)
        row += k

    def all_updated():
        return jnp.full((SUB, 128), sum(r[0].reshape(-1)[0] for r in res.values()), F32)

    gate_layers = sorted(gate_gathers)
    gate_parts = [_send_wait('gather', gate_gathers[i], ['stack'], all_updated(), f'gather_gate_grads_wait_{i}')[0]
                  for i in gate_layers]
    gate_w, gate_m, gate_v = (jnp.stack([gate_block(src, j) for j in range(len(gate_layers))])
                              for src in (shards, moms, vels))
    gates = _adamw_sum(gate_w, gate_parts, gate_m, gate_v, 'adamw_lru_gates')
    half = gates[0].shape[1] // 2
    res['lru_w_a'] = tuple(o[:, :half].reshape(shards['lru_w_a'].shape) for o in gates)
    res['lru_w_x'] = tuple(o[:, half:].reshape(shards['lru_w_x'].shape) for o in gates)

    norm_names = ('norm_mix', 'norm_mlp', 'norm_ple', 'norm_final')

    def norm_block(src):
        cat = jnp.concatenate([src[n].reshape(-1, d) for n in norm_names], axis=0)
        return _pad_to(cat, -(-cat.shape[0] // HALO) * HALO)

    rfull = {n: (rgrads[n] if n == 'norm_final' else jnp.stack(rgrads[n], axis=0)) for n in norm_names}
    parts_norm, = _gather_many([norm_block(rfull)], ['stack'], 'gather_norm_grads', after=all_updated())
    norms = _adamw_sum(norm_block(shards)[None], [parts_norm], norm_block(moms)[None], norm_block(vels)[None],
                       'adamw_norms')
    row = 0
    for n in norm_names:
        k = shards[n].size // d
        res[n] = tuple(o[0, row:row + k].reshape(shards[n].shape) for o in norms)
        row += k

    return (total_loss[0], grad_x.reshape(x.shape), *[res[n][0] for n in WEIGHTS], *[res[n][1] for n in WEIGHTS],
            *[res[n][2] for n in WEIGHTS], *[res[n][3] for n in WEIGHTS])
```

```python
import functools
import math

import jax
import jax.numpy as jnp
from jax import lax
from jax.experimental import pallas as pl
from jax.experimental.pallas import tpu as pltpu

F32 = jnp.float32
BF16 = jnp.bfloat16
SDS = jax.ShapeDtypeStruct

N_DEV = 8
RMS_EPS = 1e-6
N_MIXERS = 3
HEAD_DIM = 128
DILATED_PATTERNS = ((128, 1), (512, 4), (2048, 16))
ATTN_BLOCK = 128
ROPE_THETA = 500000.0
ROPE_DIM = HEAD_DIM // 4
LRU_BLOCK = 128
LRU_C = 8.0
ADAM_LR, ADAM_B1, ADAM_B2, ADAM_EPS, ADAM_WD, ADAM_STEP = 0.001, 0.9, 0.999, 1e-08, 0.01, 10

HALO = 16
SUB = 8
VMEM_LIMIT = 56 * 1024 * 1024
NEG = -1e30

SHARD_AXIS = {
    'norm_mix': None, 'norm_mlp': None, 'norm_ple': None, 'norm_final': None,
    'sc_w_in': 2, 'sc_w_conv': 2, 'sc_w_out': 1, 'attn_w_qkv': 2, 'attn_w_o': 1,
    'lru_w_in': 2, 'lru_conv_w': 2, 'lru_conv_b': 1, 'lru_w_a': None, 'lru_b_a': 1,
    'lru_w_x': None, 'lru_b_x': 1, 'lru_lambda': 1, 'lru_w_out': 1,
    'mlp_w_up': 2, 'mlp_w_down': 1, 'ple_w_gate': 1, 'ple_w_proj': 2,
}
WEIGHTS = list(SHARD_AXIS)


def _params(*sem):
    return pltpu.CompilerParams(dimension_semantics=sem or None, vmem_limit_bytes=VMEM_LIMIT)


def _row_tile(t, pref=256):
    tr = min(t, pref)
    assert t % tr == 0 and tr % HALO == 0
    return tr


def _row(tr, c, col=0):
    return pl.BlockSpec((tr, c), lambda i, col=col: (i, col))


def _full(shape):
    return pl.BlockSpec(shape, lambda *_: (0,) * len(shape))


def _sigmoid(x):
    return 1.0 / (1.0 + jnp.exp(-x))


def _expm1(x):
    taylor = x * (1.0 + x * (0.5 + x * (1.0 / 6.0 + x * (1.0 / 24.0 + x * (1.0 / 120.0)))))
    return jnp.where(jnp.abs(x) < 0.1, taylor, jnp.exp(x) - 1.0)


def _softplus(x):
    z = jnp.exp(-jnp.abs(x))
    log1p = jnp.where(z < 0.01, z * (1.0 - z * (0.5 - z * (1.0 / 3.0 - z * 0.25))), jnp.log(1.0 + z))
    return jnp.maximum(x, 0.0) + log1p


_GELU_K = math.sqrt(2.0 / math.pi)


def _gelu_and_grad(x):
    inner = _GELU_K * (x + 0.044715 * x * x * x)
    th = jnp.tanh(inner)
    g = 0.5 * x * (1.0 + th)
    dg = 0.5 * (1.0 + th) + 0.5 * x * (1.0 - th * th) * _GELU_K * (1.0 + 3.0 * 0.044715 * x * x)
    return g, dg


def _shift_down(x, k, prev):
    row = lax.broadcasted_iota(jnp.int32, (SUB, x.shape[1]), 0)
    xr = pltpu.roll(x, k, 0)
    top = jnp.where(row < k, pltpu.roll(prev, k, 0), xr[0:SUB])
    return jnp.concatenate([top, xr[SUB:]], axis=0)


def _shift_up(x, k, nxt):
    r = x.shape[0]
    row = lax.broadcasted_iota(jnp.int32, (SUB, x.shape[1]), 0)
    xr = pltpu.roll(x, r - k, 0)
    bot = jnp.where(row >= SUB - k, pltpu.roll(nxt, SUB - k, 0), xr[r - SUB:r])
    return jnp.concatenate([xr[:r - SUB], bot], axis=0)


_DIMS = {'nn': (((1,), (0,)), ((), ())), 'nt': (((1,), (1,)), ((), ())), 'tn': (((0,), (0,)), ((), ()))}


MM_VMEM_BUDGET = 50 * 1024 * 1024
MM_MIN_TK = 1024
MM_MIN_TM = 1024


def _tile_options(dim):
    return [c for c in range(dim, 127, -128) if dim % c == 0] or [dim]


def _choose_tiles(m, n, k, n_span, k_span, a_size, b_size, mn_size, a_temp):
    best = None
    for tm in _tile_options(m):
        for tn in _tile_options(n_span):
            for tk in _tile_options(k_span):
                nk = k // tk
                need = (2 * (tm * tk * a_size + tk * tn * b_size + tm * tn * mn_size) + tm * tn * 4 * (1 + (nk > 1))
                        + tm * tk * 4 * a_temp)
                score = (-min(tk, MM_MIN_TK), -min(tm, MM_MIN_TM), -tm * tn, nk, -min(tm, 2 * MM_MIN_TM), -tn)
                if need <= MM_VMEM_BUDGET and (best is None or score < best[0]):
                    best = (score, (tm, tn, tk))
    return best[1]


def _mm(a, b, dims, name, out_dtypes=(F32,), a_pro=None, extras=(), epi=None, out_stacked=False, dep=None):
    deps = [] if dep is None else [dep]
    stacked = b.ndim == 3
    b_rows, b_cols = (b.shape[1], N_DEV * b.shape[2]) if stacked else b.shape
    if dims == 'nn':
        (m, k), (k2, n) = a.shape, (b_rows, b_cols)
    elif dims == 'nt':
        (m, k), (n, k2) = a.shape, (b_rows, b_cols)
    else:
        (k, m), (k2, n) = a.shape, (b_rows, b_cols)
    assert k == k2, (name, a.shape, b.shape)
    assert not (stacked and dims == 'tn') and not (out_stacked and (extras or dims != 'tn'))
    tm, tn, tk = _choose_tiles(
        m, n, k, n // N_DEV if (out_stacked or (stacked and dims == 'nn')) else n,
        k // N_DEV if (stacked and dims == 'nt') else k, a.dtype.itemsize, b.dtype.itemsize,
        sum(e.dtype.itemsize for e in extras) + sum(jnp.dtype(dt).itemsize for dt in out_dtypes),
        a_pro is not None or a.dtype != BF16)
    assert m % tm == 0 and n % tn == 0 and k % tk == 0, (name, m, n, k)
    nk = k // tk
    a_spec = pl.BlockSpec((tk, tm), lambda i, j, kk: (kk, i)) if dims == 'tn' else pl.BlockSpec((tm, tk), lambda i, j, kk: (i, kk))
    if not stacked:
        b_spec = pl.BlockSpec((tn, tk), lambda i, j, kk: (j, kk)) if dims == 'nt' else pl.BlockSpec((tk, tn), lambda i, j, kk: (kk, j))
    elif dims == 'nn':
        per = b.shape[2] // tn
        b_spec = pl.BlockSpec((None, tk, tn), lambda i, j, kk: (j // per, kk, j % per))
    else:
        per = b.shape[2] // tk
        b_spec = pl.BlockSpec((None, tn, tk), lambda i, j, kk: (kk // per, j, kk % per))
    if out_stacked:
        per_o = n // N_DEV // tn
        o_spec = pl.BlockSpec((None, tm, tn), lambda i, j, kk: (j // per_o, i, j % per_o))
        o_shape = (N_DEV, m, n // N_DEV)
    else:
        o_spec = pl.BlockSpec((tm, tn), lambda i, j, kk: (i, j))
        o_shape = (m, n)
    n_ex, n_out = len(extras), len(out_dtypes)
    for e in extras:
        assert e.shape == (m, n), (name, e.shape)

    def body(a_ref, b_ref, *rest):
        rest = rest[len(deps):]
        ex_refs, out_refs = rest[:n_ex], rest[n_ex:n_ex + n_out]
        kk = pl.program_id(2)
        av = a_ref[...]
        if a_pro is not None:
            av = a_pro(av.astype(F32))
        part = lax.dot_general(av.astype(BF16), b_ref[...].astype(BF16), _DIMS[dims], preferred_element_type=F32)

        def finish(res):
            outs = (res,) if epi is None else epi(res, *[e[...] for e in ex_refs])
            for o_ref, o in zip(out_refs, outs):
                o_ref[...] = o.astype(o_ref.dtype)

        if nk == 1:
            finish(part)
        else:
            acc = rest[-1]

            @pl.when(kk == 0)
            def _():
                acc[...] = part

            @pl.when(kk > 0)
            def _():
                acc[...] += part

            @pl.when(kk == nk - 1)
            def _():
                finish(acc[...])

    out = pl.pallas_call(
        body, grid=(m // tm, n // tn, nk),
        in_specs=[a_spec, b_spec] + [_ANY] * len(deps) + [o_spec] * n_ex,
        out_specs=[o_spec] * n_out,
        out_shape=[SDS(o_shape, d) for d in out_dtypes],
        scratch_shapes=[] if nk == 1 else [pltpu.VMEM((tm, tn), F32)],
        compiler_params=_params('parallel', 'parallel', 'arbitrary'), name=name)(a, b, *deps, *extras)
    return out[0] if n_out == 1 else out


def _relu2(u):
    r = jnp.maximum(u, 0.0)
    return r * r


STREAM_ROWS = 512


def _rms_fwd(h, g, name):
    t, d = h.shape
    tr = _row_tile(t, STREAM_ROWS)

    def body(h_ref, g_ref, o_ref):
        x = h_ref[...]
        r = lax.rsqrt(jnp.mean(x * x, axis=-1, keepdims=True) + RMS_EPS)
        o_ref[...] = (x * r * g_ref[...]).astype(o_ref.dtype)

    return pl.pallas_call(body, grid=(t // tr,), in_specs=[_row(tr, d), _full((1, d))], out_specs=_row(tr, d),
                          out_shape=SDS((t, d), BF16), compiler_params=_params('parallel'), name=name)(h, g.reshape(1, d))


def _rms_bwd(h, g, dhn, dres, name):
    t, d = h.shape
    tr = _row_tile(t, STREAM_ROWS)

    def body(h_ref, g_ref, dhn_ref, dres_ref, dh_ref, dg_ref):
        @pl.when(pl.program_id(0) == 0)
        def _():
            dg_ref[...] = jnp.zeros_like(dg_ref)

        x = h_ref[...]
        r = lax.rsqrt(jnp.mean(x * x, axis=-1, keepdims=True) + RMS_EPS)
        dy = dhn_ref[...].astype(F32)
        gy = dy * g_ref[...]
        dx = r * gy - x * (r * r * r) * jnp.mean(gy * x, axis=-1, keepdims=True)
        dh_ref[...] = dres_ref[...] + dx
        dg_ref[...] += jnp.sum(dy * (x * r), axis=0, keepdims=True)

    return pl.pallas_call(body, grid=(t // tr,),
                          in_specs=[_row(tr, d), _full((1, d)), _row(tr, d), _row(tr, d)],
                          out_specs=[_row(tr, d), _full((1, d))],
                          out_shape=[SDS((t, d), F32), SDS((1, d), F32)],
                          compiler_params=_params('arbitrary'), name=name)(h, g.reshape(1, d), dhn, dres)


def _head(h, g, target, name):
    t, d = h.shape
    tr = _row_tile(t, STREAM_ROWS)

    def body(h_ref, g_ref, t_ref, dh_ref, loss_ref, dg_ref):
        @pl.when(pl.program_id(0) == 0)
        def _():
            dg_ref[...] = jnp.zeros_like(dg_ref)
            loss_ref[...] = jnp.zeros_like(loss_ref)

        x = h_ref[...]
        gv = g_ref[...]
        r = lax.rsqrt(jnp.mean(x * x, axis=-1, keepdims=True) + RMS_EPS)
        xh = x * r
        e = xh * gv - t_ref[...]
        per_tok = jnp.mean(e * e, axis=-1, keepdims=True)
        loss_ref[...] += jnp.broadcast_to(0.5 * jnp.sum(per_tok, axis=0, keepdims=True), loss_ref.shape)
        dy = e * (1.0 / d)
        gy = dy * gv
        dh_ref[...] = r * gy - x * (r * r * r) * jnp.mean(gy * x, axis=-1, keepdims=True)
        dg_ref[...] += jnp.sum(dy * xh, axis=0, keepdims=True)

    return pl.pallas_call(body, grid=(t // tr,),
                          in_specs=[_row(tr, d), _full((1, d)), _row(tr, d)],
                          out_specs=[_row(tr, d), _full((1, 128)), _full((1, d))],
                          out_shape=[SDS((t, d), F32), SDS((1, 128), F32), SDS((1, d), F32)],
                          compiler_params=_params('arbitrary'), name=name)(h, g.reshape(1, d), target)


def _ple_bwd_gate(dh3, gate, pp, name):
    t, d = dh3.shape
    tr = _row_tile(t, STREAM_ROWS)

    def body(dh_ref, g_ref, pp_ref, dpp_ref, dgl_ref):
        dh = dh_ref[...]
        gt = g_ref[...].astype(F32)
        dpp_ref[...] = (dh * gt).astype(dpp_ref.dtype)
        dgl_ref[...] = (dh * pp_ref[...].astype(F32) * gt * (1.0 - gt)).astype(dgl_ref.dtype)

    return pl.pallas_call(body, grid=(t // tr,), in_specs=[_row(tr, d)] * 3, out_specs=[_row(tr, d)] * 2,
                          out_shape=[SDS((t, d), BF16), SDS((t, d), BF16)],
                          compiler_params=_params('parallel'), name=name)(dh3, gate, pp)


def _halo_prev(tr, c, col=0):
    return pl.BlockSpec((HALO, c), lambda i, col=col: (jnp.maximum(i * (tr // HALO) - 1, 0), col))


def _halo_next(tr, c, t, col=0):
    return pl.BlockSpec((HALO, c), lambda i, col=col: (jnp.minimum((i + 1) * (tr // HALO), t // HALO - 1), col))


def _sc_fwd(z, w, name):
    t, c3 = z.shape
    c = c3 // 3
    tr = _row_tile(t)

    def body(z_ref, zp_ref, w_ref, y_ref):
        i = pl.program_id(0)
        zz = z_ref[...]
        gb, cx = zz[:, :c], zz[:, c:2 * c] * zz[:, 2 * c:]
        zp = zp_ref[SUB:HALO, :]
        cxp = jnp.where(i > 0, zp[:, c:2 * c] * zp[:, 2 * c:], 0.0)
        wv = w_ref[...]
        conv = wv[2:3] * cx + wv[1:2] * _shift_down(cx, 1, cxp) + wv[0:1] * _shift_down(cx, 2, cxp)
        y_ref[...] = (gb * conv).astype(y_ref.dtype)

    return pl.pallas_call(body, grid=(t // tr,),
                          in_specs=[_row(tr, c3), _halo_prev(tr, c3), _full((3, c))],
                          out_specs=_row(tr, c), out_shape=SDS((t, c), BF16),
                          compiler_params=_params('parallel'), name=name)(z, z, w)


def _sc_bwd(dy, z, w, name):
    t, c3 = z.shape
    c = c3 // 3
    tr = _row_tile(t)
    nt = t // tr

    def body(dy_ref, dyn_ref, z_ref, zp_ref, zn_ref, w_ref, dz_ref, dw_ref):
        i = pl.program_id(0)

        @pl.when(i == 0)
        def _():
            dw_ref[...] = jnp.zeros_like(dw_ref)

        zz = z_ref[...]
        gb, gc, xi = zz[:, :c], zz[:, c:2 * c], zz[:, 2 * c:]
        cx = gc * xi
        zp = zp_ref[SUB:HALO, :]
        cxp = jnp.where(i > 0, zp[:, c:2 * c] * zp[:, 2 * c:], 0.0)
        wv = w_ref[...]
        cx1, cx2 = _shift_down(cx, 1, cxp), _shift_down(cx, 2, cxp)
        conv = wv[2:3] * cx + wv[1:2] * cx1 + wv[0:1] * cx2
        dyv = dy_ref[...]
        dconv = dyv * gb
        dcn = jnp.where(i < nt - 1, dyn_ref[0:SUB, :] * zn_ref[0:SUB, :c], 0.0)
        dcx = wv[2:3] * dconv + wv[1:2] * _shift_up(dconv, 1, dcn) + wv[0:1] * _shift_up(dconv, 2, dcn)
        dz_ref[:, :c] = (dyv * conv).astype(dz_ref.dtype)
        dz_ref[:, c:2 * c] = (dcx * xi).astype(dz_ref.dtype)
        dz_ref[:, 2 * c:] = (dcx * gc).astype(dz_ref.dtype)
        dw_ref[...] += jnp.concatenate([jnp.sum(dconv * cx2, axis=0, keepdims=True),
                                        jnp.sum(dconv * cx1, axis=0, keepdims=True),
                                        jnp.sum(dconv * cx, axis=0, keepdims=True)], axis=0)

    return pl.pallas_call(body, grid=(nt,),
                          in_specs=[_row(tr, c), _halo_next(tr, c, t), _row(tr, c3), _halo_prev(tr, c3),
                                    _halo_next(tr, c3, t), _full((3, c))],
                          out_specs=[_row(tr, c3), _full((3, c))],
                          out_shape=[SDS((t, c3), BF16), SDS((3, c), F32)],
                          compiler_params=_params('arbitrary'), name=name)(dy, dy, z, z, z, w)


def _perm(tr, dil, inverse=False):
    n = tr // dil
    a = lax.broadcasted_iota(jnp.int32, (tr, tr), 1 if inverse else 0)
    b = lax.broadcasted_iota(jnp.int32, (tr, tr), 0 if inverse else 1)
    return (b == (a % n) * dil + a // n).astype(BF16)


def _permute(pm, x, terms):
    if x.dtype == BF16:
        return jnp.dot(pm, x, preferred_element_type=F32)
    acc = None
    for _ in range(terms):
        part = x.astype(BF16)
        y = jnp.dot(pm, part, preferred_element_type=F32)
        acc = y if acc is None else acc + y
        x = x - part.astype(F32)
    return acc


def _store_dilated(o_ref, y, dil, d):
    n = y.shape[0] // dil
    for rho in range(dil):
        o_ref[:, rho * d:(rho + 1) * d] = y[rho * n:(rho + 1) * n].astype(o_ref.dtype)


def _load_dilated(ref, dil, d):
    return jnp.concatenate([ref[:, rho * d:(rho + 1) * d] for rho in range(dil)], axis=0) if dil > 1 else ref[...]


def _rope_heads(x, lane, cos, sin):
    return jnp.concatenate([_rope_apply(x[:, s:s + HEAD_DIM], lane, cos, sin)
                            for s in range(0, x.shape[1], HEAD_DIM)], axis=1)


def _rope_tables(pos, invf, sign):
    lane = lax.broadcasted_iota(jnp.int32, (pos.shape[0], HEAD_DIM), 1)
    ang = pos.astype(F32) * invf
    half = ROPE_DIM // 2
    cos = jnp.where(lane < ROPE_DIM, jnp.cos(ang), 1.0)
    sin = jnp.sin(ang) * sign
    sin = jnp.where(lane < half, -sin, jnp.where(lane < ROPE_DIM, sin, 0.0))
    return lane, cos, sin


def _rope_apply(x, lane, cos, sin):
    half = ROPE_DIM // 2
    xs = jnp.where(lane < half, pltpu.roll(x, HEAD_DIM - half, 1), pltpu.roll(x, half, 1))
    return x * cos + xs * sin


def _dilated_spec(tr, dil, d):
    return pl.BlockSpec((tr // dil, dil * d), lambda i: (i, 0))


def _rope_fwd(qkv, pos, invf, dils, name):
    t, w3 = qkv.shape
    w, ng = w3 // 3, len(dils)
    d = w // ng
    tr = _row_tile(t)

    def body(q_ref, k_ref, v_ref, pos_ref, invf_ref, *out_refs):
        lane, cos, sin = _rope_tables(pos_ref[...], invf_ref[...], 1.0)
        for g, dil in enumerate(dils):
            cs = slice(g * d, (g + 1) * d)
            vals = [_rope_heads(q_ref[:, cs], lane, cos, sin).astype(BF16),
                    _rope_heads(k_ref[:, cs], lane, cos, sin).astype(BF16), v_ref[:, cs].astype(BF16)]
            if dil > 1:
                pm = _perm(tr, dil)
                vals = [_permute(pm, a, 1) for a in vals]
            for o_ref, a in zip(out_refs[g::ng], vals):
                _store_dilated(o_ref, a, dil, d)

    outs = pl.pallas_call(body, grid=(t // tr,),
                          in_specs=[_row(tr, w, 0), _row(tr, w, 1), _row(tr, w, 2), _row(tr, 1), _full((1, HEAD_DIM))],
                          out_specs=[_dilated_spec(tr, dil, d) for dil in dils] * 3,
                          out_shape=[SDS((t // dil, dil * d), BF16) for dil in dils] * 3,
                          compiler_params=_params('parallel'), name=name)(qkv, qkv, qkv, pos, invf)
    return outs[:ng], outs[ng:2 * ng], outs[2 * ng:]


def _rope_bwd(dqs, dks, dvs, pos, invf, dils, name):
    ng = len(dils)
    t = dqs[0].shape[0] * dils[0]
    d = dqs[0].shape[1] // dils[0]
    w = ng * d
    tr = _row_tile(t)

    def body(*refs):
        dq_refs, dk_refs, dv_refs = refs[:ng], refs[ng:2 * ng], refs[2 * ng:3 * ng]
        pos_ref, invf_ref, o_ref = refs[3 * ng:]
        pos_f = jnp.broadcast_to(pos_ref[...].astype(F32), (tr, HEAD_DIM))
        for g, dil in enumerate(dils):
            pos_g = pos_f if dil == 1 else _permute(_perm(tr, dil), pos_f, 3)
            lane, cos, sin = _rope_tables(pos_g, invf_ref[...], -1.0)
            vals = [_rope_heads(_load_dilated(dq_refs[g], dil, d), lane, cos, sin),
                    _rope_heads(_load_dilated(dk_refs[g], dil, d), lane, cos, sin), _load_dilated(dv_refs[g], dil, d)]
            back = _perm(tr, dil, inverse=True) if dil > 1 else None
            for sec, a in enumerate(vals):
                a = a.astype(BF16)
                if dil > 1:
                    a = _permute(back, a, 1)
                o_ref[:, sec * w + g * d:sec * w + (g + 1) * d] = a.astype(o_ref.dtype)

    return pl.pallas_call(body, grid=(t // tr,),
                          in_specs=[_dilated_spec(tr, dil, d) for dil in dils] * 3 + [_row(tr, 1), _full((1, HEAD_DIM))],
                          out_specs=_row(tr, 3 * w), out_shape=SDS((t, 3 * w), BF16),
                          compiler_params=_params('parallel'), name=name)(*dqs, *dks, *dvs, pos, invf)


def _dilate_many(arrs, dil, terms, out_dtypes, name):
    t, d = arrs[0].shape
    tr = _row_tile(t)
    na = len(arrs)

    def body(*refs):
        pm = _perm(tr, dil)
        for a_ref, o_ref, k in zip(refs[:na], refs[na:], terms):
            _store_dilated(o_ref, _permute(pm, a_ref[...], k), dil, d)

    return pl.pallas_call(body, grid=(t // tr,), in_specs=[_row(tr, d)] * na,
                          out_specs=[_dilated_spec(tr, dil, d)] * na,
                          out_shape=[SDS((t // dil, dil * d), dt) for dt in out_dtypes],
                          compiler_params=_params('parallel'), name=name)(*arrs)


def _attn_masks():
    qi = lax.broadcasted_iota(jnp.int32, (ATTN_BLOCK, ATTN_BLOCK), 0)
    kj = lax.broadcasted_iota(jnp.int32, (ATTN_BLOCK, ATTN_BLOCK), 1)
    return kj >= qi, kj <= qi


def _attn_cols(l, width):
    ncol = width // HEAD_DIM
    cpb = max(1, min(ncol, 32 // (l // ATTN_BLOCK)))
    assert ncol % cpb == 0
    return cpb


def _attn_fwd(q, k, v, name):
    l, width = q.shape
    cpb = _attn_cols(l, width)
    nb = l // ATTN_BLOCK
    scale = HEAD_DIM ** -0.5

    def body(q_ref, k_ref, v_ref, o_ref, lse_ref):
        m_prev, m_cur = _attn_masks()
        for col in range(cpb):
            cs = slice(col * HEAD_DIM, (col + 1) * HEAD_DIM)

            def step(b, carry, cs=cs):
                r0 = pl.multiple_of(b * ATTN_BLOCK, ATTN_BLOCK)
                rp = pl.multiple_of(jnp.maximum(b - 1, 0) * ATTN_BLOCK, ATTN_BLOCK)
                qb = q_ref[pl.ds(r0, ATTN_BLOCK), cs]
                s_p = lax.dot_general(qb, k_ref[pl.ds(rp, ATTN_BLOCK), cs], _DIMS['nt'], preferred_element_type=F32) * scale
                s_c = lax.dot_general(qb, k_ref[pl.ds(r0, ATTN_BLOCK), cs], _DIMS['nt'], preferred_element_type=F32) * scale
                s_p = jnp.where(jnp.logical_and(m_prev, b > 0), s_p, NEG)
                s_c = jnp.where(m_cur, s_c, NEG)
                m = jnp.maximum(jnp.max(s_p, axis=-1, keepdims=True), jnp.max(s_c, axis=-1, keepdims=True))
                p_p, p_c = jnp.exp(s_p - m), jnp.exp(s_c - m)
                den = jnp.sum(p_p, axis=-1, keepdims=True) + jnp.sum(p_c, axis=-1, keepdims=True)
                acc = jnp.dot(p_p.astype(BF16), v_ref[pl.ds(rp, ATTN_BLOCK), cs], preferred_element_type=F32)
                acc += jnp.dot(p_c.astype(BF16), v_ref[pl.ds(r0, ATTN_BLOCK), cs], preferred_element_type=F32)
                o_ref[pl.ds(r0, ATTN_BLOCK), cs] = acc / den
                lse_ref[pl.ds(r0, ATTN_BLOCK), cs] = jnp.broadcast_to(m + jnp.log(den), (ATTN_BLOCK, HEAD_DIM))
                return carry

            lax.fori_loop(0, nb, step, 0, unroll=min(nb, 4))

    spec = pl.BlockSpec((l, cpb * HEAD_DIM), lambda j: (0, j))
    return pl.pallas_call(body, grid=(width // (cpb * HEAD_DIM),), in_specs=[spec] * 3, out_specs=[spec] * 2,
                          out_shape=[SDS((l, width), F32)] * 2,
                          compiler_params=_params('parallel'), name=name)(q, k, v)


def _attn_bwd(q, k, v, do, lse, delta, name):
    l, width = q.shape
    cpb = _attn_cols(l, width)
    nb = l // ATTN_BLOCK
    scale = HEAD_DIM ** -0.5

    def body(q_ref, k_ref, v_ref, do_ref, lse_ref, dl_ref, dq_ref, dk_ref, dv_ref):
        m_prev, m_cur = _attn_masks()
        dk_ref[...] = jnp.zeros_like(dk_ref)
        dv_ref[...] = jnp.zeros_like(dv_ref)
        for col in range(cpb):
            cs = slice(col * HEAD_DIM, (col + 1) * HEAD_DIM)

            def step(b, carry, cs=cs):
                r0 = pl.multiple_of(b * ATTN_BLOCK, ATTN_BLOCK)
                rp = pl.multiple_of(jnp.maximum(b - 1, 0) * ATTN_BLOCK, ATTN_BLOCK)
                qb, dob = q_ref[pl.ds(r0, ATTN_BLOCK), cs], do_ref[pl.ds(r0, ATTN_BLOCK), cs].astype(BF16)
                kp, kc = k_ref[pl.ds(rp, ATTN_BLOCK), cs], k_ref[pl.ds(r0, ATTN_BLOCK), cs]
                vp, vc = v_ref[pl.ds(rp, ATTN_BLOCK), cs], v_ref[pl.ds(r0, ATTN_BLOCK), cs]
                lse_b = lse_ref[pl.ds(r0, ATTN_BLOCK), cs]
                dl_b = dl_ref[pl.ds(r0, ATTN_BLOCK), cs]
                s_p = lax.dot_general(qb, kp, _DIMS['nt'], preferred_element_type=F32) * scale
                s_c = lax.dot_general(qb, kc, _DIMS['nt'], preferred_element_type=F32) * scale
                p_p = jnp.exp(jnp.where(jnp.logical_and(m_prev, b > 0), s_p, NEG) - lse_b)
                p_c = jnp.exp(jnp.where(m_cur, s_c, NEG) - lse_b)
                dp_p = lax.dot_general(dob, vp, _DIMS['nt'], preferred_element_type=F32)
                dp_c = lax.dot_general(dob, vc, _DIMS['nt'], preferred_element_type=F32)
                ds_p = (p_p * (dp_p - dl_b) * scale).astype(BF16)
                ds_c = (p_c * (dp_c - dl_b) * scale).astype(BF16)
                dq_ref[pl.ds(r0, ATTN_BLOCK), cs] = (jnp.dot(ds_p, kp, preferred_element_type=F32)
                                                     + jnp.dot(ds_c, kc, preferred_element_type=F32))
                dk_ref[pl.ds(rp, ATTN_BLOCK), cs] += lax.dot_general(ds_p, qb, _DIMS['tn'], preferred_element_type=F32)
                dk_ref[pl.ds(r0, ATTN_BLOCK), cs] += lax.dot_general(ds_c, qb, _DIMS['tn'], preferred_element_type=F32)
                dv_ref[pl.ds(rp, ATTN_BLOCK), cs] += lax.dot_general(p_p.astype(BF16), dob, _DIMS['tn'], preferred_element_type=F32)
                dv_ref[pl.ds(r0, ATTN_BLOCK), cs] += lax.dot_general(p_c.astype(BF16), dob, _DIMS['tn'], preferred_element_type=F32)
                return carry

            lax.fori_loop(0, nb, step, 0, unroll=min(nb, 2))

    spec = pl.BlockSpec((l, cpb * HEAD_DIM), lambda j: (0, j))
    return pl.pallas_call(body, grid=(width // (cpb * HEAD_DIM),), in_specs=[spec] * 6, out_specs=[spec] * 3,
                          out_shape=[SDS((l, width), F32)] * 3,
                          compiler_params=_params('parallel'), name=name)(q, k, v, do, lse, delta)


def _attn_combine(os_, lses, dils, name):
    ng = len(dils)
    t = os_[0].shape[0] * dils[0]
    d = os_[0].shape[1] // dils[0]
    tr = _row_tile(t)

    def body(*refs):
        o_refs, l_refs, o_out, lse_out = refs[:ng], refs[ng:2 * ng], refs[2 * ng], refs[2 * ng + 1]
        ovs, ls = [], []
        for g, dil in enumerate(dils):
            ov, lv = _load_dilated(o_refs[g], dil, d), _load_dilated(l_refs[g], dil, d)
            if dil > 1:
                back = _perm(tr, dil, inverse=True)
                ov, lv = _permute(back, ov, 2), _permute(back, lv, 3)
            ovs.append(ov)
            ls.append(lv)
        m = functools.reduce(jnp.maximum, ls)
        ws = [jnp.exp(x - m) for x in ls]
        den = functools.reduce(lambda a, b: a + b, ws)
        acc = functools.reduce(lambda a, b: a + b, [w * o for w, o in zip(ws, ovs)])
        o_out[...] = (acc / den).astype(o_out.dtype)
        lse_out[...] = m + jnp.log(den)

    return pl.pallas_call(body, grid=(t // tr,), in_specs=[_dilated_spec(tr, dil, d) for dil in dils] * 2,
                          out_specs=[_row(tr, d)] * 2, out_shape=[SDS((t, d), BF16), SDS((t, d), F32)],
                          compiler_params=_params('parallel'), name=name)(*os_, *lses)


def _delta_epilogue(acc, o):
    prod = acc * o.astype(F32)
    segs = [jnp.broadcast_to(jnp.sum(prod[:, s:s + HEAD_DIM], axis=-1, keepdims=True), (acc.shape[0], HEAD_DIM))
            for s in range(0, acc.shape[1], HEAD_DIM)]
    return acc, jnp.concatenate(segs, axis=-1)


LRU_TILE = 128


def _lru_gates(xr, wa_ref, ba, wx_ref, bx, lam):
    nb = wa_ref.shape[0]
    xb = xr.astype(BF16)
    ra = jnp.concatenate([jnp.dot(xb[:, n * LRU_BLOCK:(n + 1) * LRU_BLOCK], wa_ref[n], preferred_element_type=F32)
                          for n in range(nb)], axis=-1) + ba
    ia = jnp.concatenate([jnp.dot(xb[:, n * LRU_BLOCK:(n + 1) * LRU_BLOCK], wx_ref[n], preferred_element_type=F32)
                          for n in range(nb)], axis=-1) + bx
    r, ig = _sigmoid(ra), _sigmoid(ia)
    sp = _softplus(-lam)
    log_a = -LRU_C * r * sp
    a = jnp.exp(log_a)
    mult = jnp.sqrt(-_expm1(2.0 * log_a))
    return xb, r, ig, sp, a, mult


def _lru_fwd(z, cw, cb, wa, ba, wx, bx, lam, name):
    t, c2 = z.shape
    c = c2 // 2
    nb = c // LRU_BLOCK
    tr = _row_tile(t, LRU_TILE)

    def body(g_ref, x_ref, xp_ref, cw_ref, cb_ref, wa_ref, ba_ref, wx_ref, bx_ref, lam_ref,
             y_ref, hs_ref, xr_ref, car_ref):
        i = pl.program_id(0)

        @pl.when(i == 0)
        def _():
            car_ref[...] = jnp.zeros_like(car_ref)

        x0 = x_ref[...]
        xp = jnp.where(i > 0, xp_ref[SUB:HALO, :], 0.0)
        cwv = cw_ref[...]
        xr = (cb_ref[...] + cwv[3:4] * x0 + cwv[2:3] * _shift_down(x0, 1, xp)
              + cwv[1:2] * _shift_down(x0, 2, xp) + cwv[0:1] * _shift_down(x0, 3, xp))
        xr_ref[...] = xr
        _, _, ig, _, a, mult = _lru_gates(xr, wa_ref, ba_ref[...], wx_ref, bx_ref[...], lam_ref[...])
        u = mult * (ig * xr)
        row = lax.broadcasted_iota(jnp.int32, (SUB, c), 0)
        car = car_ref[...]
        for j in range(tr // SUB):
            ab, ub = a[j * SUB:(j + 1) * SUB], u[j * SUB:(j + 1) * SUB]
            for s in (1, 2, 4):
                a_sh = jnp.where(row >= s, pltpu.roll(ab, s, 0), 1.0)
                u_sh = jnp.where(row >= s, pltpu.roll(ub, s, 0), 0.0)
                ub = ab * u_sh + ub
                ab = ab * a_sh
            hb = ub + ab * car
            hs_ref[j * SUB:(j + 1) * SUB, :] = hb
            car = jnp.broadcast_to(hb[SUB - 1:SUB], (SUB, c))
        car_ref[...] = car
        gl, _ = _gelu_and_grad(g_ref[...])
        y_ref[...] = (hs_ref[...] * gl).astype(y_ref.dtype)

    return pl.pallas_call(
        body, grid=(t // tr,),
        in_specs=[_row(tr, c, 0), _row(tr, c, 1), _halo_prev(tr, c, 1), _full((4, c)), _full((1, c)),
                  _full((nb, LRU_BLOCK, LRU_BLOCK)), _full((1, c)), _full((nb, LRU_BLOCK, LRU_BLOCK)), _full((1, c)), _full((1, c))],
        out_specs=[_row(tr, c)] * 3,
        out_shape=[SDS((t, c), BF16), SDS((t, c), F32), SDS((t, c), F32)],
        scratch_shapes=[pltpu.VMEM((SUB, c), F32)],
        compiler_params=_params('arbitrary'), name=name)(
            z, z, z, cw, cb.reshape(1, c), wa, ba.reshape(1, c), wx, bx.reshape(1, c), lam.reshape(1, c))


def _lru_bwd(dy, z, xr, hs, cw, wa, ba, wx, bx, lam, name):
    t, c2 = z.shape
    c = c2 // 2
    nb = c // LRU_BLOCK
    tr = _row_tile(t, LRU_TILE)
    nt = t // tr

    def rev(col=0):
        return pl.BlockSpec((tr, c), lambda i, col=col: (nt - 1 - i, col))

    def rev_prev(col=0):
        return pl.BlockSpec((HALO, c), lambda i, col=col: (jnp.maximum((nt - 1 - i) * (tr // HALO) - 1, 0), col))

    def body(dy_ref, g_ref, x_ref, xp_ref, xr_ref, hs_ref, hp_ref, cw_ref, wa_ref, ba_ref, wx_ref, bx_ref, lam_ref,
             dz_ref, dwa_ref, dwx_ref, dvec_ref, lcar_ref, ahead_ref, dxhead_ref, lam_s):
        i = pl.program_id(0)
        first_tile = i == nt - 1

        @pl.when(i == 0)
        def _():
            lcar_ref[...] = jnp.zeros_like(lcar_ref)
            ahead_ref[...] = jnp.zeros_like(ahead_ref)
            dxhead_ref[...] = jnp.zeros_like(dxhead_ref)
            dwa_ref[...] = jnp.zeros_like(dwa_ref)
            dwx_ref[...] = jnp.zeros_like(dwx_ref)
            dvec_ref[...] = jnp.zeros_like(dvec_ref)

        xrv = xr_ref[...]
        lamv = lam_ref[...]
        xb, r, ig, sp, a, mult = _lru_gates(xrv, wa_ref, ba_ref[...], wx_ref, bx_ref[...], lamv)
        hsv = hs_ref[...]
        dyv = dy_ref[...]
        gl, dgl = _gelu_and_grad(g_ref[...])
        dhs = dyv * gl
        dz_ref[:, :c] = (dyv * hsv * dgl).astype(dz_ref.dtype)

        a_next = _shift_up(a, 1, ahead_ref[...])
        row = lax.broadcasted_iota(jnp.int32, (SUB, c), 0)
        car = lcar_ref[...]
        for j in reversed(range(tr // SUB)):
            ab, ub = a_next[j * SUB:(j + 1) * SUB], dhs[j * SUB:(j + 1) * SUB]
            for s in (1, 2, 4):
                a_sh = jnp.where(row < SUB - s, pltpu.roll(ab, SUB - s, 0), 1.0)
                u_sh = jnp.where(row < SUB - s, pltpu.roll(ub, SUB - s, 0), 0.0)
                ub = ab * u_sh + ub
                ab = ab * a_sh
            lb = ub + ab * car
            lam_s[j * SUB:(j + 1) * SUB, :] = lb
            car = jnp.broadcast_to(lb[0:1], (SUB, c))
        lcar_ref[...] = car
        ahead_ref[...] = a[0:SUB]
        lmb = lam_s[...]

        hp = jnp.where(first_tile, 0.0, hp_ref[SUB:HALO, :])
        h_prev = _shift_down(hsv, 1, hp)
        d_a = lmb * h_prev
        d_mult = lmb * (ig * xrv)
        d_ixr = lmb * mult
        d_ig = d_ixr * xrv
        dxr = d_ixr * ig
        d_la = d_a * a - d_mult * (a * a) / mult
        d_r = d_la * (-LRU_C * sp)
        d_sp = jnp.sum(d_la * (-LRU_C * r), axis=0, keepdims=True)
        d_ra = d_r * r * (1.0 - r)
        d_ia = d_ig * ig * (1.0 - ig)
        d_rab, d_iab = d_ra.astype(BF16), d_ia.astype(BF16)
        parts = []
        for n in range(nb):
            cs = slice(n * LRU_BLOCK, (n + 1) * LRU_BLOCK)
            parts.append(lax.dot_general(d_rab[:, cs], wa_ref[n], _DIMS['nt'], preferred_element_type=F32)
                         + lax.dot_general(d_iab[:, cs], wx_ref[n], _DIMS['nt'], preferred_element_type=F32))
            dwa_ref[n] += lax.dot_general(xb[:, cs], d_rab[:, cs], _DIMS['tn'], preferred_element_type=F32)
            dwx_ref[n] += lax.dot_general(xb[:, cs], d_iab[:, cs], _DIMS['tn'], preferred_element_type=F32)
        dxr = dxr + jnp.concatenate(parts, axis=-1)

        cwv = cw_ref[...]
        nxt = dxhead_ref[...]
        dx0 = (cwv[3:4] * dxr + cwv[2:3] * _shift_up(dxr, 1, nxt) + cwv[1:2] * _shift_up(dxr, 2, nxt)
               + cwv[0:1] * _shift_up(dxr, 3, nxt))
        dxhead_ref[...] = dxr[0:SUB]
        dz_ref[:, c:] = dx0.astype(dz_ref.dtype)

        x0 = x_ref[...]
        xp = jnp.where(first_tile, 0.0, xp_ref[SUB:HALO, :])
        sums = [jnp.sum(d_ra, axis=0, keepdims=True), jnp.sum(d_ia, axis=0, keepdims=True),
                d_sp * (-_sigmoid(-lamv)), jnp.sum(dxr, axis=0, keepdims=True),
                jnp.sum(dxr * _shift_down(x0, 3, xp), axis=0, keepdims=True),
                jnp.sum(dxr * _shift_down(x0, 2, xp), axis=0, keepdims=True),
                jnp.sum(dxr * _shift_down(x0, 1, xp), axis=0, keepdims=True),
                jnp.sum(dxr * x0, axis=0, keepdims=True)]
        dvec_ref[...] += jnp.concatenate(sums, axis=0)

    wspec = _full((nb, LRU_BLOCK, LRU_BLOCK))
    return pl.pallas_call(
        body, grid=(nt,),
        in_specs=[rev(), rev(0), rev(1), rev_prev(1), rev(), rev(), rev_prev(), _full((4, c)),
                  wspec, _full((1, c)), wspec, _full((1, c)), _full((1, c))],
        out_specs=[pl.BlockSpec((tr, c2), lambda i: (nt - 1 - i, 0)), wspec, wspec, _full((SUB, c))],
        out_shape=[SDS((t, c2), BF16), SDS((nb, LRU_BLOCK, LRU_BLOCK), F32), SDS((nb, LRU_BLOCK, LRU_BLOCK), F32),
                   SDS((SUB, c), F32)],
        scratch_shapes=[pltpu.VMEM((SUB, c), F32), pltpu.VMEM((SUB, c), F32), pltpu.VMEM((SUB, c), F32),
                        pltpu.VMEM((tr, c), F32)],
        compiler_params=_params('arbitrary'), name=name)(
            dy, z, z, z, xr, hs, hs, cw, wa, ba.reshape(1, c), wx, bx.reshape(1, c), lam.reshape(1, c))


def _local_step(x, p, pos, target, rep, weights_for_layer, emit_grads):
    t, d = x.shape
    depth = p.shape[0]
    w = rep
    half = ROPE_DIM // 2
    invf = ROPE_THETA ** (-2.0 * jnp.arange(half, dtype=F32) / ROPE_DIM)
    invf = jnp.concatenate([invf, invf, jnp.zeros((HEAD_DIM - ROPE_DIM,), F32)]).reshape(1, HEAD_DIM)
    dils = tuple(dil for _, dil in DILATED_PATTERNS)
    saved = []
    h = x
    for i in range(depth):
        kind, j = i % N_MIXERS, i // N_MIXERS
        wl, tok = weights_for_layer(i, 'mixer', h)
        s = {'h0': h, 'wl': wl}
        hn = _rms_fwd(h, w['norm_mix'][i], f'rms_mix_fwd_{i}')
        s['hn'] = hn
        if kind == 0:
            z = _mm(hn, wl['w_in'], 'nn', f'sc_in_{i}', dep=tok)
            y = _sc_fwd(z, wl['small'], f'sc_conv_fwd_{i}')
            h1 = _mm(y, wl['w_out'], 'nn', f'sc_out_{i}', extras=(h,), epi=lambda acc, res: (acc + res,))
            s.update(z=z, y=y)
        elif kind == 1:
            qkv = _mm(hn, wl['w_in'], 'nn', f'attn_qkv_{i}', dep=tok)
            qs, ks, vs = _rope_fwd(qkv, pos, invf, dils, f'rope_fwd_{i}')
            views = list(zip(qs, ks, vs))
            os_, lses = zip(*[_attn_fwd(qg, kg, vg, f'attn_fwd_{i}_g{g}') for g, (qg, kg, vg) in enumerate(views)])
            o, lse = _attn_combine(os_, lses, dils, f'attn_combine_{i}')
            h1 = _mm(o, wl['w_out'], 'nn', f'attn_out_{i}', extras=(h,), epi=lambda acc, res: (acc + res,))
            s.update(views=views, o=o, lse=lse)
        else:
            z = _mm(hn, wl['w_in'], 'nn', f'lru_in_{i}', dep=tok)
            sm = wl['small']
            y, hs, xr = _lru_fwd(z, sm[0:4], sm[4:5], w['lru_w_a'][j], sm[5:6], w['lru_w_x'][j], sm[6:7], sm[7:8],
                                 f'lru_fwd_{i}')
            h1 = _mm(y, wl['w_out'], 'nn', f'lru_out_{i}', extras=(h,), epi=lambda acc, res: (acc + res,))
            s.update(z=z, y=y, hs=hs, xr=xr)
        s['h1'] = h1
        more, tok = weights_for_layer(i, 'mlp', h1)
        wl.update(more)
        hm = _rms_fwd(h1, w['norm_mlp'][i], f'rms_mlp_fwd_{i}')
        u = _mm(hm, wl['mlp_up'], 'nn', f'mlp_up_{i}', out_dtypes=(BF16,), dep=tok)
        h2 = _mm(u, wl['mlp_down'], 'nn', f'mlp_down_{i}', a_pro=_relu2, extras=(h1,), epi=lambda acc, res: (acc + res,))
        hp = _rms_fwd(h2, w['norm_ple'][i], f'rms_ple_fwd_{i}')
        pp = _mm(p[i], wl['ple_proj'], 'nn', f'ple_proj_{i}')
        h3, gate = _mm(hp, wl['ple_gate'], 'nn', f'ple_gate_{i}', out_dtypes=(F32, BF16), extras=(pp, h2),
                       epi=lambda acc, ppv, res: (res + _sigmoid(acc) * ppv, _sigmoid(acc)))
        s.update(hm=hm, u=u, h2=h2, hp=hp, pp=pp, gate=gate)
        saved.append(s)
        h = h3

    dh, loss, dg_final = _head(h, w['norm_final'], target, 'loss_head')
    grads = {n: [None] * depth for n in ('norm_mix', 'norm_mlp', 'norm_ple')}
    grads['norm_final'] = dg_final.reshape(d)
    started = None
    for i in reversed(range(depth)):
        kind, j = i % N_MIXERS, i // N_MIXERS
        s = saved[i]
        wl, gl = s['wl'], {}
        dpp, dgl = _ple_bwd_gate(dh, s['gate'], s['pp'], f'ple_bwd_gate_{i}')
        gl['ple_proj'] = _mm(p[i], dpp, 'tn', f'ple_dproj_{i}', out_dtypes=(BF16,), dep=started)
        gl['ple_gate'] = _mm(s['hp'], dgl, 'tn', f'ple_dgate_{i}', out_dtypes=(BF16,))
        dhp = _mm(dgl, wl['ple_gate'], 'nt', f'ple_dhp_{i}', out_dtypes=(BF16,))
        dh, dg = _rms_bwd(s['h2'], w['norm_ple'][i], dhp, dh, f'rms_ple_bwd_{i}')
        grads['norm_ple'][i] = dg.reshape(d)
        du = _mm(dh, wl['mlp_down'], 'nt', f'mlp_du_{i}', out_dtypes=(BF16,), extras=(s['u'],),
                 epi=lambda acc, uv: (acc * 2.0 * jnp.maximum(uv.astype(F32), 0.0),))
        gl['mlp_down'] = _mm(s['u'], dh, 'tn', f'mlp_ddown_{i}', out_dtypes=(BF16,), a_pro=_relu2)
        gl['mlp_up'] = _mm(s['hm'], du, 'tn', f'mlp_dup_{i}', out_dtypes=(BF16,), out_stacked=True)
        dhm = _mm(du, wl['mlp_up'], 'nt', f'mlp_dhm_{i}', out_dtypes=(BF16,))
        dh, dg = _rms_bwd(s['h1'], w['norm_mlp'][i], dhm, dh, f'rms_mlp_bwd_{i}')
        grads['norm_mlp'][i] = dg.reshape(d)
        started = emit_grads(i, 'mlp', gl, loss if i == depth - 1 else None)
        gl = {}
        if kind == 0:
            dy = _mm(dh, wl['w_out'], 'nt', f'sc_dy_{i}', dep=started)
            gl['w_out'] = _mm(s['y'], dh, 'tn', f'sc_dout_{i}', out_dtypes=(BF16,))
            dz, dwc = _sc_bwd(dy, s['z'], wl['small'], f'sc_conv_bwd_{i}')
            gl['small'] = dwc
            gl['w_in'] = _mm(s['hn'], dz, 'tn', f'sc_din_{i}', out_dtypes=(BF16,))
            started = emit_grads(i, 'mixer', gl)
            dhn = _mm(dz, wl['w_in'], 'nt', f'sc_dhn_{i}', out_dtypes=(BF16,), dep=started)
        elif kind == 1:
            do, delta = _mm(dh, wl['w_out'], 'nt', f'attn_do_{i}', out_dtypes=(BF16, F32), extras=(s['o'],),
                            epi=_delta_epilogue, dep=started)
            gl['w_out'] = _mm(s['o'], dh, 'tn', f'attn_dwo_{i}', out_dtypes=(BF16,))
            rows_in = {1: (do, s['lse'], delta)}
            for dil in dils:
                if dil not in rows_in:
                    rows_in[dil] = _dilate_many([do, s['lse'], delta], dil, (1, 3, 3), (BF16, F32, F32),
                                                f'attn_dilate_{i}_d{dil}')
            dqs, dks, dvs = zip(*[_attn_bwd(*s['views'][g], *rows_in[dil], f'attn_bwd_{i}_g{g}')
                                  for g, dil in enumerate(dils)])
            dqkv = _rope_bwd(dqs, dks, dvs, pos, invf, dils, f'rope_bwd_{i}')
            gl['w_in'] = _mm(s['hn'], dqkv, 'tn', f'attn_dqkv_{i}', out_dtypes=(BF16,), out_stacked=True)
            started = emit_grads(i, 'mixer', gl)
            dhn = _mm(dqkv, wl['w_in'], 'nt', f'attn_dhn_{i}', out_dtypes=(BF16,), dep=started)
        else:
            dy = _mm(dh, wl['w_out'], 'nt', f'lru_dy_{i}', dep=started)
            gl['w_out'] = _mm(s['y'], dh, 'tn', f'lru_dout_{i}', out_dtypes=(BF16,))
            sm = wl['small']
            dz, dwa, dwx, dvec = _lru_bwd(dy, s['z'], s['xr'], s['hs'], sm[0:4], w['lru_w_a'][j], sm[5:6],
                                          w['lru_w_x'][j], sm[6:7], sm[7:8], f'lru_bwd_{i}')
            gl['gates'], gl['small'] = (dwa, dwx), dvec
            gl['w_in'] = _mm(s['hn'], dz, 'tn', f'lru_din_{i}', out_dtypes=(BF16,))
            started = emit_grads(i, 'mixer', gl)
            dhn = _mm(dz, wl['w_in'], 'nt', f'lru_dhn_{i}', out_dtypes=(BF16,), dep=started)
        dh, dg = _rms_bwd(s['h0'], w['norm_mix'][i], dhn, dh, f'rms_mix_bwd_{i}')
        grads['norm_mix'][i] = dg.reshape(d)
        started = None
    return loss, dh, grads


_MESH = pl.DeviceIdType.MESH
_ANY = pl.BlockSpec(memory_space=pl.ANY)


def _block_view(ref, kind, idx):
    if kind == 'stack':
        return ref.at[idx]
    r = ref.shape[0] // N_DEV
    return ref.at[pl.ds(idx * r, r)]


def _gather_many(arrs, kinds, name, after=None):
    n = len(arrs)
    after = [] if after is None else [after]
    out_shapes = [SDS((N_DEV,) + a.shape if kd == 'stack' else (N_DEV * a.shape[0],) + a.shape[1:], a.dtype)
                  for a, kd in zip(arrs, kinds)]

    def body(*refs):
        x_refs, out_refs = refs[:n], refs[n + len(after):2 * n + len(after)]
        send_sems, recv_sems, local_sems = refs[2 * n + len(after):]
        x, y, c = lax.axis_index('x'), lax.axis_index('y'), lax.axis_index('c')
        me, sibling = (x, y, c), (x, y, 1 - c)
        chips = [(1 - x, y), (x, 1 - y), (1 - x, 1 - y)]

        def slab(t, px, py, pc):
            return _block_view(out_refs[t], kinds[t], 4 * px + 2 * py + pc)

        def copy(t, k, block, to, src=None):
            return pltpu.make_async_remote_copy(
                src_ref=slab(t, *block) if src is None else src, dst_ref=slab(t, *block),
                send_sem=send_sems.at[7 * t + k], recv_sem=recv_sems.at[7 * t + k], device_id=to, device_id_type=_MESH)

        mine = [pltpu.make_async_copy(x_refs[t], slab(t, *me), local_sems.at[t]) for t in range(n)]
        for cp in mine:
            cp.start()
        first = [copy(t, 0, me, sibling, src=x_refs[t]) for t in range(n)]
        first += [copy(t, 1 + j, me, (*chip, c), src=x_refs[t]) for j, chip in enumerate(chips) for t in range(n)]
        for cp in first:
            cp.start()
        passed = []
        for j, chip in enumerate(chips):
            for t in range(n):
                copy(t, 1 + j, (*chip, c), me).wait_recv()
                passed.append(copy(t, 4 + j, (*chip, c), sibling))
                passed[-1].start()
        for t in range(n):
            copy(t, 0, sibling, me).wait_recv()
            for j, chip in enumerate(chips):
                copy(t, 4 + j, (*chip, 1 - c), me).wait_recv()
        for cp in first + passed:
            cp.wait_send()
        for cp in mine:
            cp.wait()

    return pl.pallas_call(
        body, out_shape=out_shapes, in_specs=[_ANY] * (n + len(after)), out_specs=[_ANY] * n,
        scratch_shapes=[pltpu.SemaphoreType.DMA((7 * n,)), pltpu.SemaphoreType.DMA((7 * n,)), pltpu.SemaphoreType.DMA((n,))],
        name=name)(*arrs, *after)


_HBM = pl.BlockSpec(memory_space=pltpu.HBM)
_SEM = pl.BlockSpec(memory_space=pltpu.SEMAPHORE)
_EFFECT = pltpu.SideEffectType.DATAFLOW_SIDE_EFFECTING


def _direct_copies(mode, kinds, src_refs, land_refs, send_sems, recv_sems):
    x, y, c = lax.axis_index('x'), lax.axis_index('y'), lax.axis_index('c')
    my_idx = 4 * x + 2 * y + c
    copies = []
    for k in range(1, N_DEV):
        px, py, pc = (1 - x if k & 4 else x, 1 - y if k & 2 else y, 1 - c if k & 1 else c)
        for t, kd in enumerate(kinds):
            if mode == 'gather':
                src, dst = src_refs[t], _block_view(land_refs[t], kd, my_idx)
            else:
                src, dst = _block_view(src_refs[t], kd, 4 * px + 2 * py + pc), land_refs[t].at[my_idx]
            copies.append(pltpu.make_async_remote_copy(
                src_ref=src, dst_ref=dst, send_sem=send_sems.at[7 * t + k - 1], recv_sem=recv_sems.at[7 * t + k - 1],
                device_id=(px, py, pc), device_id_type=_MESH))
    return copies


def _own_part(mode, kind, src, land):
    idx = 4 * lax.axis_index('x') + 2 * lax.axis_index('y') + lax.axis_index('c')
    zeros = (0,) * (src.ndim - 1)
    if mode == 'gather':
        part = src
    elif kind == 'stack':
        part = lax.dynamic_index_in_dim(src, idx, 0, keepdims=False)
    else:
        r = src.shape[0] // N_DEV
        part = lax.dynamic_slice_in_dim(src, idx * r, r, 0)
    if mode == 'gather' and kind == 'rows':
        return lax.dynamic_update_slice(land, part, (idx * part.shape[0],) + zeros)
    return lax.dynamic_update_slice(land, part[None], (idx,) + (0,) * part.ndim)


def _send_start(mode, srcs, kinds, name, after=None):
    n = len(srcs)
    after = [] if after is None else [after]
    lands = []
    for a, kd in zip(srcs, kinds):
        if mode == 'gather':
            shape = (N_DEV,) + a.shape if kd == 'stack' else (N_DEV * a.shape[0],) + a.shape[1:]
        else:
            shape = a.shape if kd == 'stack' else (N_DEV, a.shape[0] // N_DEV) + a.shape[1:]
        lands.append(_own_part(mode, kd, a, lax.empty(shape, a.dtype)))

    def body(*refs):
        src_refs, land_refs = refs[:n], refs[n:2 * n]
        send_sems, recv_sems = refs[2 * n + len(after):2 * n + len(after) + 2]
        token = refs[-1]
        for cp in _direct_copies(mode, kinds, src_refs, land_refs, send_sems, recv_sems):
            cp.start()
        token[...] = jnp.zeros_like(token)

    outs = pl.pallas_call(
        body, name=name,
        out_shape=(pltpu.SemaphoreType.DMA((7 * n,)), pltpu.SemaphoreType.DMA((7 * n,)),
                   *[pltpu.HBM(a.shape, a.dtype) for a in srcs + lands], SDS((SUB, 128), F32)),
        in_specs=[_HBM] * (2 * n) + [_ANY] * len(after),
        out_specs=(_SEM, _SEM, *[_HBM] * (2 * n), pl.BlockSpec(memory_space=pltpu.VMEM)),
        input_output_aliases={i: 2 + i for i in range(2 * n)},
        compiler_params=pltpu.CompilerParams(has_side_effects=_EFFECT),
    )(*[pltpu.with_memory_space_constraint(a, pltpu.HBM) for a in srcs + lands], *after)
    return (outs[0], outs[1], list(outs[2:2 + 2 * n])), outs[-1]


def _send_wait(mode, flight, kinds, after, name):
    send, recv, bufs = flight
    n = len(kinds)

    def body(*refs):
        src_refs, land_refs, (send_sems, recv_sems) = refs[:n], refs[n:2 * n], refs[2 * n:2 * n + 2]
        copies = _direct_copies(mode, kinds, src_refs, land_refs, send_sems, recv_sems)
        for cp in copies:
            cp.wait_send()
        for cp in copies:
            cp.wait_recv()

    outs = pl.pallas_call(
        body, name=name, out_shape=[pltpu.HBM(a.shape, a.dtype) for a in bufs],
        in_specs=[_HBM] * (2 * n) + [_SEM, _SEM, _ANY], out_specs=[_HBM] * (2 * n),
        input_output_aliases={i: i for i in range(2 * n)},
        compiler_params=pltpu.CompilerParams(has_side_effects=_EFFECT),
    )(*bufs, send, recv, after)
    return list(outs[n:])


ADAMW_BLOCK_ELEMS = 128 * 1024


def _adamw_sum(wgt, parts, m, v, name):
    nl, r, c = wgt.shape
    assert len(parts) == nl and all(q.shape == (N_DEV, r, c) for q in parts), (name, wgt.shape, [q.shape for q in parts])
    tr = next((t for t in range(min(r, 512), 0, -16) if r % t == 0 and t * c <= ADAMW_BLOCK_ELEMS and t % 16 == 0), r)
    c1 = 1.0 - ADAM_B1 ** ADAM_STEP
    c2 = 1.0 - ADAM_B2 ** ADAM_STEP

    def body(w_ref, m_ref, v_ref, *rest):
        part_refs, (g_ref, d_ref, mo_ref, vo_ref) = rest[:nl], rest[nl:]
        for q in range(nl):
            @pl.when(pl.program_id(0) == q)
            def _(q=q):
                gv = part_refs[q][0].astype(F32)
                for s in range(1, N_DEV):
                    gv = gv + part_refs[q][s].astype(F32)
                mn = ADAM_B1 * m_ref[...] + (1.0 - ADAM_B1) * gv
                vn = ADAM_B2 * v_ref[...] + (1.0 - ADAM_B2) * (gv * gv)
                g_ref[...] = gv
                d_ref[...] = -ADAM_LR * ((mn / c1) / (jnp.sqrt(vn / c2) + ADAM_EPS) + ADAM_WD * w_ref[...])
                mo_ref[...] = mn
                vo_ref[...] = vn

    spec = pl.BlockSpec((None, tr, c), lambda l, i: (l, i, 0))
    part_specs = [pl.BlockSpec((N_DEV, tr, c), lambda l, i, q=q: (0, jnp.where(l == q, i, 0), 0)) for q in range(nl)]
    return pl.pallas_call(body, grid=(nl, r // tr), in_specs=[spec] * 3 + part_specs, out_specs=[spec] * 4,
                          out_shape=[SDS((nl, r, c), F32)] * 4, compiler_params=_params('arbitrary', 'arbitrary'),
                          name=name)(wgt, m, v, *parts)


MIXER_WEIGHTS = {0: ('sc_w_in', 'sc_w_out'), 1: ('attn_w_qkv', 'attn_w_o'), 2: ('lru_w_in', 'lru_w_out')}
STACKED_OPERANDS = ('attn_w_qkv', 'mlp_w_up')
LRU_SMALL = ('lru_conv_w', 'lru_conv_b', 'lru_b_a', 'lru_b_x', 'lru_lambda')


def _layer_items(i):
    w_in, w_out = MIXER_WEIGHTS[i % N_MIXERS]
    j = i // N_MIXERS
    return [('w_in', w_in, j), ('w_out', w_out, j), ('mlp_up', 'mlp_w_up', i), ('mlp_down', 'mlp_w_down', i),
            ('ple_gate', 'ple_w_gate', i), ('ple_proj', 'ple_w_proj', i)]


def _cols_to_full(stacked):
    return jnp.moveaxis(stacked, 0, 1).reshape(stacked.shape[1], -1)


def _full_to_cols(full):
    k, n = full.shape
    return jnp.moveaxis(full.reshape(k, N_DEV, n // N_DEV), 1, 0)


def _pad_to(a, rows):
    return jnp.pad(a, ((0, rows - a.shape[0]), (0, 0)))


def _small_block(src, i):
    kind, j = i % N_MIXERS, i // N_MIXERS
    if kind == 0:
        return _pad_to(src['sc_w_conv'][j], SUB)
    if kind == 2:
        return jnp.concatenate([src[n][j].reshape(-1, src[n].shape[-1]) for n in LRU_SMALL], axis=0)
    return None


def kernel(x, p, positions, norm_mix, norm_mlp, norm_ple, norm_final, sc_w_in, sc_w_conv, sc_w_out, attn_w_qkv, attn_w_o, lru_w_in, lru_conv_w, lru_conv_b, lru_w_a, lru_b_a, lru_w_x, lru_b_x, lru_lambda, lru_w_out, mlp_w_up, mlp_w_down, ple_w_gate, ple_w_proj, loss_target, m_norm_mix, m_norm_mlp, m_norm_ple, m_norm_final, m_sc_w_in, m_sc_w_conv, m_sc_w_out, m_attn_w_qkv, m_attn_w_o, m_lru_w_in, m_lru_conv_w, m_lru_conv_b, m_lru_w_a, m_lru_b_a, m_lru_w_x, m_lru_b_x, m_lru_lambda, m_lru_w_out, m_mlp_w_up, m_mlp_w_down, m_ple_w_gate, m_ple_w_proj, v_norm_mix, v_norm_mlp, v_norm_ple, v_norm_final, v_sc_w_in, v_sc_w_conv, v_sc_w_out, v_attn_w_qkv, v_attn_w_o, v_lru_w_in, v_lru_conv_w, v_lru_conv_b, v_lru_w_a, v_lru_b_a, v_lru_w_x, v_lru_b_x, v_lru_lambda, v_lru_w_out, v_mlp_w_up, v_mlp_w_down, v_ple_w_gate, v_ple_w_proj):
    loc = dict(locals())
    shards = {n: loc[n] for n in WEIGHTS}
    moms = {n: loc['m_' + n] for n in WEIGHTS}
    vels = {n: loc['v_' + n] for n in WEIGHTS}

    depth, t, d = p.shape[0], x.shape[1], x.shape[2]

    def comm_kind(name):
        return 'stack' if SHARD_AXIS[name] == 2 else 'rows'

    part_keys = {'mlp': ('mlp_up', 'mlp_down', 'ple_gate', 'ple_proj'), 'mixer': ('w_in', 'w_out')}
    halves = [(i, part) for i in range(depth) for part in ('mixer', 'mlp')]

    def half_shards(i, part):
        items = [it for it in _layer_items(i) if it[0] in part_keys[part]]
        arrs = [shards[n][idx].astype(BF16) for _, n, idx in items]
        kinds = [comm_kind(n) for _, n, _ in items]
        small = _small_block(shards, i) if part == 'mixer' else None
        if small is not None:
            arrs.append(small)
            kinds.append('stack')
        return items, arrs, kinds

    def half_weights(i, items, kinds, outs):
        wl = {key: (_cols_to_full(o) if kd == 'stack' and n not in STACKED_OPERANDS else o)
              for (key, n, _), kd, o in zip(items, kinds, outs)}
        if len(outs) > len(items):
            wl['small'] = _cols_to_full(outs[-1])[:shards['sc_w_conv'].shape[1] if i % N_MIXERS == 0 else SUB]
        return wl

    first = [half_shards(0, part) for part in ('mixer', 'mlp')]
    outs0 = _gather_many(first[0][1] + first[1][1], first[0][2] + first[1][2], 'gather_weights_0')
    weights0 = {**half_weights(0, first[0][0], first[0][2], outs0[:len(first[0][1])]),
                **half_weights(0, first[1][0], first[1][2], outs0[len(first[0][1]):])}
    pending = {}

    def start_gather(pos, after):
        if pos >= len(halves):
            return None
        i, part = halves[pos]
        items, arrs, kinds = half_shards(i, part)
        flight, token = _send_start('gather', arrs, kinds, f'gather_weights_start_{part}_{i}', after=after)
        pending[pos] = (items, kinds, flight)
        return token

    first_token = start_gather(2, outs0[0])
    second_token = start_gather(3, first_token)

    def weights_for_layer(i, part, h):
        pos = halves.index((i, part))
        if pos == 0:
            return weights0, second_token
        if pos == 1:
            return {}, None
        items, kinds, flight = pending.pop(pos)
        outs = _send_wait('gather', flight, kinds, h, f'gather_weights_wait_{part}_{i}')
        return half_weights(i, items, kinds, outs), start_gather(pos + 2, outs[0])

    exchanges, gate_gathers, total_loss = {}, {}, []

    def gate_block(src, j):
        return jnp.concatenate([src[n][j].reshape(-1, LRU_BLOCK) for n in ('lru_w_a', 'lru_w_x')], axis=0)

    def emit_grads(i, part, gl, loss=None):
        after = None
        if loss is not None:
            total_loss.append(lax.psum(loss[0, 0], ('x', 'y', 'c')))
            after = jnp.full((SUB, 128), total_loss[0], F32)
        if 'gates' in gl:
            blk = gate_block({'lru_w_a': [gl['gates'][0]], 'lru_w_x': [gl['gates'][1]]}, 0)
            gate_gathers[i] = _send_start('gather', [blk], ['stack'], f'gather_gate_grads_start_{i}')[0]
        items = [it for it in _layer_items(i) if it[0] in part_keys[part]]
        kinds = [comm_kind(n) for _, n, _ in items]
        arrs = [_full_to_cols(gl[key]) if kd == 'stack' and gl[key].ndim == 2 else gl[key]
                for (key, _, _), kd in zip(items, kinds)]
        if part == 'mixer' and i % N_MIXERS == 0:
            arrs.append(_full_to_cols(_pad_to(gl['small'], SUB)))
        elif part == 'mixer' and i % N_MIXERS == 2:
            dv = gl['small']
            arrs.append(_full_to_cols(jnp.concatenate([dv[4:8], dv[3:4], dv[0:1], dv[1:2], dv[2:3]], axis=0)))
        kinds += ['stack'] * (len(arrs) - len(kinds))
        flight, token = _send_start('exchange', arrs, kinds, f'exchange_grads_start_{part}_{i}', after=after)
        exchanges[(i, part)] = (items, kinds, flight)
        return token

    rep = {n: shards[n] for n in ('norm_mix', 'norm_mlp', 'norm_ple', 'norm_final')}
    rep['lru_w_a'], rep['lru_w_x'] = shards['lru_w_a'].astype(BF16), shards['lru_w_x'].astype(BF16)
    loss, grad_x, rgrads = _local_step(x.reshape(t, d), p.reshape(depth, t, p.shape[3]), positions.reshape(t, 1),
                                       loss_target.reshape(t, d), rep, weights_for_layer, emit_grads)

    received, res = {}, {}

    def finish_exchange(key, after):
        items, kinds, flight = exchanges[key]
        outs = _send_wait('exchange', flight, kinds, after, f'exchange_grads_wait_{key[1]}_{key[0]}')
        for (_, n, idx), o in zip(items, outs):
            received[(n, idx)] = o
        if len(outs) > len(items):
            received[('small', key[0])] = outs[-1]

    def big_adamw(names):
        for n in names:
            res[n] = _adamw_sum(shards[n], [received[(n, l)] for l in range(shards[n].shape[0])], moms[n], vels[n],
                                f'adamw_{n}')

    last = (0, 'mixer')
    for key in exchanges:
        if key != last:
            finish_exchange(key, grad_x)
    big = [n for n in WEIGHTS if SHARD_AXIS[n] is not None and shards[n].ndim == 3 and n not in ('sc_w_conv', 'lru_conv_w')]
    late = [n for n in big if n in MIXER_WEIGHTS[0]]
    big_adamw([n for n in big if n not in late])
    finish_exchange(last, jnp.full((SUB, 128), sum(r[0].reshape(-1)[0] for r in res.values()), F32))
    big_adamw(late)


    def small_adamw(layers, name):
        w_, m_, v_ = (jnp.stack([_small_block(src, i) for i in layers]) for src in (shards, moms, vels))
        return _adamw_sum(w_, [received[('small', i)] for i in layers], m_, v_, name)

    sc = small_adamw([i for i in range(depth) if i % N_MIXERS == 0], 'adamw_sc_w_conv')
    res['sc_w_conv'] = tuple(o[:, :shards['sc_w_conv'].shape[1]] for o in sc)
    lru = small_adamw([i for i in range(depth) if i % N_MIXERS == 2], 'adamw_lru_small')
    row = 0
    for n in LRU_SMALL:
        k = shards[n].size // shards[n].shape[0] // shards[n].shape[-1]
        res[n] = tuple(o[:, row:row + k].reshape(shards[n].shape) for o in lru)
        row += k

    def all_updated():
        return jnp.full((SUB, 128), sum(r[0].reshape(-1)[0] for r in res.values()), F32)

    gate_layers = sorted(gate_gathers)
    gate_parts = [_send_wait('gather', gate_gathers[i], ['stack'], all_updated(), f'gather_gate_grads_wait_{i}')[0]
                  for i in gate_layers]
    gate_w, gate_m, gate_v = (jnp.stack([gate_block(src, j) for j in range(len(gate_layers))])
                              for src in (shards, moms, vels))
    gates = _adamw_sum(gate_w, gate_parts, gate_m, gate_v, 'adamw_lru_gates')
    half = gates[0].shape[1] // 2
    res['lru_w_a'] = tuple(o[:, :half].reshape(shards['lru_w_a'].shape) for o in gates)
    res['lru_w_x'] = tuple(o[:, half:].reshape(shards['lru_w_x'].shape) for o in gates)

    norm_names = ('norm_mix', 'norm_mlp', 'norm_ple', 'norm_final')

    def norm_block(src):
        cat = jnp.concatenate([src[n].reshape(-1, d) for n in norm_names], axis=0)
        return _pad_to(cat, -(-cat.shape[0] // HALO) * HALO)

    rfull = {n: (rgrads[n] if n == 'norm_final' else jnp.stack(rgrads[n], axis=0)) for n in norm_names}
    parts_norm, = _gather_many([norm_block(rfull)], ['stack'], 'gather_norm_grads', after=all_updated())
    norms = _adamw_sum(norm_block(shards)[None], [parts_norm], norm_block(moms)[None], norm_block(vels)[None],
                       'adamw_norms')
    row = 0
    for n in norm_names:
        k = shards[n].size // d
        res[n] = tuple(o[0, row:row + k].reshape(shards[n].shape) for o in norms)
        row += k

    return (total_loss[0], grad_x.reshape(x.shape), *[res[n][0] for n in WEIGHTS], *[res[n][1] for n in WEIGHTS],
            *[res[n][2] for n in WEIGHTS], *[res[n][3] for n in WEIGHTS])
```

```python
import functools
import math

import jax
import jax.numpy as jnp
from jax import lax
from jax.experimental import pallas as pl
from jax.experimental.pallas import tpu as pltpu

F32 = jnp.float32
BF16 = jnp.bfloat16
SDS = jax.ShapeDtypeStruct

N_DEV = 8
RMS_EPS = 1e-6
N_MIXERS = 3
HEAD_DIM = 128
DILATED_PATTERNS = ((128, 1), (512, 4), (2048, 16))
ATTN_BLOCK = 128
ROPE_THETA = 500000.0
ROPE_DIM = HEAD_DIM // 4
LRU_BLOCK = 128
LRU_C = 8.0
ADAM_LR, ADAM_B1, ADAM_B2, ADAM_EPS, ADAM_WD, ADAM_STEP = 0.001, 0.9, 0.999, 1e-08, 0.01, 10

HALO = 16
SUB = 8
VMEM_LIMIT = 56 * 1024 * 1024
NEG = -1e30

SHARD_AXIS = {
    'norm_mix': None, 'norm_mlp': None, 'norm_ple': None, 'norm_final': None,
    'sc_w_in': 2, 'sc_w_conv': 2, 'sc_w_out': 1, 'attn_w_qkv': 2, 'attn_w_o': 1,
    'lru_w_in': 2, 'lru_conv_w': 2, 'lru_conv_b': 1, 'lru_w_a': None, 'lru_b_a': 1,
    'lru_w_x': None, 'lru_b_x': 1, 'lru_lambda': 1, 'lru_w_out': 1,
    'mlp_w_up': 2, 'mlp_w_down': 1, 'ple_w_gate': 1, 'ple_w_proj': 2,
}
WEIGHTS = list(SHARD_AXIS)


def _params(*sem):
    return pltpu.CompilerParams(dimension_semantics=sem or None, vmem_limit_bytes=VMEM_LIMIT)


def _row_tile(t, pref=256):
    tr = min(t, pref)
    assert t % tr == 0 and tr % HALO == 0
    return tr


def _row(tr, c, col=0):
    return pl.BlockSpec((tr, c), lambda i, col=col: (i, col))


def _full(shape):
    return pl.BlockSpec(shape, lambda *_: (0,) * len(shape))


def _sigmoid(x):
    return 1.0 / (1.0 + jnp.exp(-x))


def _expm1(x):
    taylor = x * (1.0 + x * (0.5 + x * (1.0 / 6.0 + x * (1.0 / 24.0 + x * (1.0 / 120.0)))))
    return jnp.where(jnp.abs(x) < 0.1, taylor, jnp.exp(x) - 1.0)


def _softplus(x):
    z = jnp.exp(-jnp.abs(x))
    log1p = jnp.where(z < 0.01, z * (1.0 - z * (0.5 - z * (1.0 / 3.0 - z * 0.25))), jnp.log(1.0 + z))
    return jnp.maximum(x, 0.0) + log1p


_GELU_K = math.sqrt(2.0 / math.pi)


def _gelu_and_grad(x):
    inner = _GELU_K * (x + 0.044715 * x * x * x)
    th = jnp.tanh(inner)
    g = 0.5 * x * (1.0 + th)
    dg = 0.5 * (1.0 + th) + 0.5 * x * (1.0 - th * th) * _GELU_K * (1.0 + 3.0 * 0.044715 * x * x)
    return g, dg


def _shift_down(x, k, prev):
    row = lax.broadcasted_iota(jnp.int32, (SUB, x.shape[1]), 0)
    xr = pltpu.roll(x, k, 0)
    top = jnp.where(row < k, pltpu.roll(prev, k, 0), xr[0:SUB])
    return jnp.concatenate([top, xr[SUB:]], axis=0)


def _shift_up(x, k, nxt):
    r = x.shape[0]
    row = lax.broadcasted_iota(jnp.int32, (SUB, x.shape[1]), 0)
    xr = pltpu.roll(x, r - k, 0)
    bot = jnp.where(row >= SUB - k, pltpu.roll(nxt, SUB - k, 0), xr[r - SUB:r])
    return jnp.concatenate([xr[:r - SUB], bot], axis=0)


_DIMS = {'nn': (((1,), (0,)), ((), ())), 'nt': (((1,), (1,)), ((), ())), 'tn': (((0,), (0,)), ((), ()))}


MM_VMEM_BUDGET = 50 * 1024 * 1024
MM_MIN_TK = 1024
MM_MIN_TM = 1024


def _tile_options(dim):
    return [c for c in range(dim, 127, -128) if dim % c == 0] or [dim]


def _choose_tiles(m, n, k, n_span, k_span, a_size, b_size, mn_size, a_temp):
    best = None
    for tm in _tile_options(m):
        for tn in _tile_options(n_span):
            for tk in _tile_options(k_span):
                nk = k // tk
                need = (2 * (tm * tk * a_size + tk * tn * b_size + tm * tn * mn_size) + tm * tn * 4 * (1 + (nk > 1))
                        + tm * tk * 4 * a_temp)
                score = (-min(tk, MM_MIN_TK), -min(tm, MM_MIN_TM), -tm * tn, nk, -min(tm, 2 * MM_MIN_TM), -tn)
                if need <= MM_VMEM_BUDGET and (best is None or score < best[0]):
                    best = (score, (tm, tn, tk))
    return best[1]


def _mm(a, b, dims, name, out_dtypes=(F32,), a_pro=None, extras=(), epi=None, out_stacked=False, dep=None):
    deps = [] if dep is None else [dep]
    stacked = b.ndim == 3
    b_rows, b_cols = (b.shape[1], N_DEV * b.shape[2]) if stacked else b.shape
    if dims == 'nn':
        (m, k), (k2, n) = a.shape, (b_rows, b_cols)
    elif dims == 'nt':
        (m, k), (n, k2) = a.shape, (b_rows, b_cols)
    else:
        (k, m), (k2, n) = a.shape, (b_rows, b_cols)
    assert k == k2, (name, a.shape, b.shape)
    assert not (stacked and dims == 'tn') and not (out_stacked and (extras or dims != 'tn'))
    tm, tn, tk = _choose_tiles(
        m, n, k, n // N_DEV if (out_stacked or (stacked and dims == 'nn')) else n,
        k // N_DEV if (stacked and dims == 'nt') else k, a.dtype.itemsize, b.dtype.itemsize,
        sum(e.dtype.itemsize for e in extras) + sum(jnp.dtype(dt).itemsize for dt in out_dtypes),
        a_pro is not None or a.dtype != BF16)
    assert m % tm == 0 and n % tn == 0 and k % tk == 0, (name, m, n, k)
    nk = k // tk
    a_spec = pl.BlockSpec((tk, tm), lambda i, j, kk: (kk, i)) if dims == 'tn' else pl.BlockSpec((tm, tk), lambda i, j, kk: (i, kk))
    if not stacked:
        b_spec = pl.BlockSpec((tn, tk), lambda i, j, kk: (j, kk)) if dims == 'nt' else pl.BlockSpec((tk, tn), lambda i, j, kk: (kk, j))
    elif dims == 'nn':
        per = b.shape[2] // tn
        b_spec = pl.BlockSpec((None, tk, tn), lambda i, j, kk: (j // per, kk, j % per))
    else:
        per = b.shape[2] // tk
        b_spec = pl.BlockSpec((None, tn, tk), lambda i, j, kk: (kk // per, j, kk % per))
    if out_stacked:
        per_o = n // N_DEV // tn
        o_spec = pl.BlockSpec((None, tm, tn), lambda i, j, kk: (j // per_o, i, j % per_o))
        o_shape = (N_DEV, m, n // N_DEV)
    else:
        o_spec = pl.BlockSpec((tm, tn), lambda i, j, kk: (i, j))
        o_shape = (m, n)
    n_ex, n_out = len(extras), len(out_dtypes)
    for e in extras:
        assert e.shape == (m, n), (name, e.shape)

    def body(a_ref, b_ref, *rest):
        rest = rest[len(deps):]
        ex_refs, out_refs = rest[:n_ex], rest[n_ex:n_ex + n_out]
        kk = pl.program_id(2)
        av = a_ref[...]
        if a_pro is not None:
            av = a_pro(av.astype(F32))
        part = lax.dot_general(av.astype(BF16), b_ref[...].astype(BF16), _DIMS[dims], preferred_element_type=F32)

        def finish(res):
            outs = (res,) if epi is None else epi(res, *[e[...] for e in ex_refs])
            for o_ref, o in zip(out_refs, outs):
                o_ref[...] = o.astype(o_ref.dtype)

        if nk == 1:
            finish(part)
        else:
            acc = rest[-1]

            @pl.when(kk == 0)
            def _():
                acc[...] = part

            @pl.when(kk > 0)
            def _():
                acc[...] += part

            @pl.when(kk == nk - 1)
            def _():
                finish(acc[...])

    out = pl.pallas_call(
        body, grid=(m // tm, n // tn, nk),
        in_specs=[a_spec, b_spec] + [_ANY] * len(deps) + [o_spec] * n_ex,
        out_specs=[o_spec] * n_out,
        out_shape=[SDS(o_shape, d) for d in out_dtypes],
        scratch_shapes=[] if nk == 1 else [pltpu.VMEM((tm, tn), F32)],
        compiler_params=_params('parallel', 'parallel', 'arbitrary'), name=name)(a, b, *deps, *extras)
    return out[0] if n_out == 1 else out


def _relu2(u):
    r = jnp.maximum(u, 0.0)
    return r * r


STREAM_ROWS = 512


def _rms_fwd(h, g, name):
    t, d = h.shape
    tr = _row_tile(t, STREAM_ROWS)

    def body(h_ref, g_ref, o_ref):
        x = h_ref[...]
        r = lax.rsqrt(jnp.mean(x * x, axis=-1, keepdims=True) + RMS_EPS)
        o_ref[...] = (x * r * g_ref[...]).astype(o_ref.dtype)

    return pl.pallas_call(body, grid=(t // tr,), in_specs=[_row(tr, d), _full((1, d))], out_specs=_row(tr, d),
                          out_shape=SDS((t, d), BF16), compiler_params=_params('parallel'), name=name)(h, g.reshape(1, d))


def _rms_bwd(h, g, dhn, dres, name):
    t, d = h.shape
    tr = _row_tile(t, STREAM_ROWS)

    def body(h_ref, g_ref, dhn_ref, dres_ref, dh_ref, dg_ref):
        @pl.when(pl.program_id(0) == 0)
        def _():
            dg_ref[...] = jnp.zeros_like(dg_ref)

        x = h_ref[...]
        r = lax.rsqrt(jnp.mean(x * x, axis=-1, keepdims=True) + RMS_EPS)
        dy = dhn_ref[...].astype(F32)
        gy = dy * g_ref[...]
        dx = r * gy - x * (r * r * r) * jnp.mean(gy * x, axis=-1, keepdims=True)
        dh_ref[...] = dres_ref[...] + dx
        dg_ref[...] += jnp.sum(dy * (x * r), axis=0, keepdims=True)

    return pl.pallas_call(body, grid=(t // tr,),
                          in_specs=[_row(tr, d), _full((1, d)), _row(tr, d), _row(tr, d)],
                          out_specs=[_row(tr, d), _full((1, d))],
                          out_shape=[SDS((t, d), F32), SDS((1, d), F32)],
                          compiler_params=_params('arbitrary'), name=name)(h, g.reshape(1, d), dhn, dres)


def _head(h, g, target, name):
    t, d = h.shape
    tr = _row_tile(t, STREAM_ROWS)

    def body(h_ref, g_ref, t_ref, dh_ref, loss_ref, dg_ref):
        @pl.when(pl.program_id(0) == 0)
        def _():
            dg_ref[...] = jnp.zeros_like(dg_ref)
            loss_ref[...] = jnp.zeros_like(loss_ref)

        x = h_ref[...]
        gv = g_ref[...]
        r = lax.rsqrt(jnp.mean(x * x, axis=-1, keepdims=True) + RMS_EPS)
        xh = x * r
        e = xh * gv - t_ref[...]
        per_tok = jnp.mean(e * e, axis=-1, keepdims=True)
        loss_ref[...] += jnp.broadcast_to(0.5 * jnp.sum(per_tok, axis=0, keepdims=True), loss_ref.shape)
        dy = e * (1.0 / d)
        gy = dy * gv
        dh_ref[...] = r * gy - x * (r * r * r) * jnp.mean(gy * x, axis=-1, keepdims=True)
        dg_ref[...] += jnp.sum(dy * xh, axis=0, keepdims=True)

    return pl.pallas_call(body, grid=(t // tr,),
                          in_specs=[_row(tr, d), _full((1, d)), _row(tr, d)],
                          out_specs=[_row(tr, d), _full((1, 128)), _full((1, d))],
                          out_shape=[SDS((t, d), F32), SDS((1, 128), F32), SDS((1, d), F32)],
                          compiler_params=_params('arbitrary'), name=name)(h, g.reshape(1, d), target)


def _ple_bwd_gate(dh3, gate, pp, name):
    t, d = dh3.shape
    tr = _row_tile(t, STREAM_ROWS)

    def body(dh_ref, g_ref, pp_ref, dpp_ref, dgl_ref):
        dh = dh_ref[...]
        gt = g_ref[...].astype(F32)
        dpp_ref[...] = (dh * gt).astype(dpp_ref.dtype)
        dgl_ref[...] = (dh * pp_ref[...].astype(F32) * gt * (1.0 - gt)).astype(dgl_ref.dtype)

    return pl.pallas_call(body, grid=(t // tr,), in_specs=[_row(tr, d)] * 3, out_specs=[_row(tr, d)] * 2,
                          out_shape=[SDS((t, d), BF16), SDS((t, d), BF16)],
                          compiler_params=_params('parallel'), name=name)(dh3, gate, pp)


def _halo_prev(tr, c, col=0):
    return pl.BlockSpec((HALO, c), lambda i, col=col: (jnp.maximum(i * (tr // HALO) - 1, 0), col))


def _halo_next(tr, c, t, col=0):
    return pl.BlockSpec((HALO, c), lambda i, col=col: (jnp.minimum((i + 1) * (tr // HALO), t // HALO - 1), col))


def _sc_fwd(z, w, name):
    t, c3 = z.shape
    c = c3 // 3
    tr = _row_tile(t)

    def body(z_ref, zp_ref, w_ref, y_ref):
        i = pl.program_id(0)
        zz = z_ref[...]
        gb, cx = zz[:, :c], zz[:, c:2 * c] * zz[:, 2 * c:]
        zp = zp_ref[SUB:HALO, :]
        cxp = jnp.where(i > 0, zp[:, c:2 * c] * zp[:, 2 * c:], 0.0)
        wv = w_ref[...]
        conv = wv[2:3] * cx + wv[1:2] * _shift_down(cx, 1, cxp) + wv[0:1] * _shift_down(cx, 2, cxp)
        y_ref[...] = (gb * conv).astype(y_ref.dtype)

    return pl.pallas_call(body, grid=(t // tr,),
                          in_specs=[_row(tr, c3), _halo_prev(tr, c3), _full((3, c))],
                          out_specs=_row(tr, c), out_shape=SDS((t, c), BF16),
                          compiler_params=_params('parallel'), name=name)(z, z, w)


def _sc_bwd(dy, z, w, name):
    t, c3 = z.shape
    c = c3 // 3
    tr = _row_tile(t)
    nt = t // tr

    def body(dy_ref, dyn_ref, z_ref, zp_ref, zn_ref, w_ref, dz_ref, dw_ref):
        i = pl.program_id(0)

        @pl.when(i == 0)
        def _():
            dw_ref[...] = jnp.zeros_like(dw_ref)

        zz = z_ref[...]
        gb, gc, xi = zz[:, :c], zz[:, c:2 * c], zz[:, 2 * c:]
        cx = gc * xi
        zp = zp_ref[SUB:HALO, :]
        cxp = jnp.where(i > 0, zp[:, c:2 * c] * zp[:, 2 * c:], 0.0)
        wv = w_ref[...]
        cx1, cx2 = _shift_down(cx, 1, cxp), _shift_down(cx, 2, cxp)
        conv = wv[2:3] * cx + wv[1:2] * cx1 + wv[0:1] * cx2
        dyv = dy_ref[...]
        dconv = dyv * gb
        dcn = jnp.where(i < nt - 1, dyn_ref[0:SUB, :] * zn_ref[0:SUB, :c], 0.0)
        dcx = wv[2:3] * dconv + wv[1:2] * _shift_up(dconv, 1, dcn) + wv[0:1] * _shift_up(dconv, 2, dcn)
        dz_ref[:, :c] = (dyv * conv).astype(dz_ref.dtype)
        dz_ref[:, c:2 * c] = (dcx * xi).astype(dz_ref.dtype)
        dz_ref[:, 2 * c:] = (dcx * gc).astype(dz_ref.dtype)
        dw_ref[...] += jnp.concatenate([jnp.sum(dconv * cx2, axis=0, keepdims=True),
                                        jnp.sum(dconv * cx1, axis=0, keepdims=True),
                                        jnp.sum(dconv * cx, axis=0, keepdims=True)], axis=0)

    return pl.pallas_call(body, grid=(nt,),
                          in_specs=[_row(tr, c), _halo_next(tr, c, t), _row(tr, c3), _halo_prev(tr, c3),
                                    _halo_next(tr, c3, t), _full((3, c))],
                          out_specs=[_row(tr, c3), _full((3, c))],
                          out_shape=[SDS((t, c3), BF16), SDS((3, c), F32)],
                          compiler_params=_params('arbitrary'), name=name)(dy, dy, z, z, z, w)


def _perm(tr, dil, inverse=False):
    n = tr // dil
    a = lax.broadcasted_iota(jnp.int32, (tr, tr), 1 if inverse else 0)
    b = lax.broadcasted_iota(jnp.int32, (tr, tr), 0 if inverse else 1)
    return (b == (a % n) * dil + a // n).astype(BF16)


def _permute(pm, x, terms):
    if x.dtype == BF16:
        return jnp.dot(pm, x, preferred_element_type=F32)
    acc = None
    for _ in range(terms):
        part = x.astype(BF16)
        y = jnp.dot(pm, part, preferred_element_type=F32)
        acc = y if acc is None else acc + y
        x = x - part.astype(F32)
    return acc


def _store_dilated(o_ref, y, dil, d):
    n = y.shape[0] // dil
    for rho in range(dil):
        o_ref[:, rho * d:(rho + 1) * d] = y[rho * n:(rho + 1) * n].astype(o_ref.dtype)


def _load_dilated(ref, dil, d):
    return jnp.concatenate([ref[:, rho * d:(rho + 1) * d] for rho in range(dil)], axis=0) if dil > 1 else ref[...]


def _rope_heads(x, lane, cos, sin):
    return jnp.concatenate([_rope_apply(x[:, s:s + HEAD_DIM], lane, cos, sin)
                            for s in range(0, x.shape[1], HEAD_DIM)], axis=1)


def _rope_tables(pos, invf, sign):
    lane = lax.broadcasted_iota(jnp.int32, (pos.shape[0], HEAD_DIM), 1)
    ang = pos.astype(F32) * invf
    half = ROPE_DIM // 2
    cos = jnp.where(lane < ROPE_DIM, jnp.cos(ang), 1.0)
    sin = jnp.sin(ang) * sign
    sin = jnp.where(lane < half, -sin, jnp.where(lane < ROPE_DIM, sin, 0.0))
    return lane, cos, sin


def _rope_apply(x, lane, cos, sin):
    half = ROPE_DIM // 2
    xs = jnp.where(lane < half, pltpu.roll(x, HEAD_DIM - half, 1), pltpu.roll(x, half, 1))
    return x * cos + xs * sin


def _dilated_spec(tr, dil, d):
    return pl.BlockSpec((tr // dil, dil * d), lambda i: (i, 0))


def _rope_fwd(qkv, pos, invf, dils, name):
    t, w3 = qkv.shape
    w, ng = w3 // 3, len(dils)
    d = w // ng
    tr = _row_tile(t)

    def body(q_ref, k_ref, v_ref, pos_ref, invf_ref, *out_refs):
        lane, cos, sin = _rope_tables(pos_ref[...], invf_ref[...], 1.0)
        for g, dil in enumerate(dils):
            cs = slice(g * d, (g + 1) * d)
            vals = [_rope_heads(q_ref[:, cs], lane, cos, sin).astype(BF16),
                    _rope_heads(k_ref[:, cs], lane, cos, sin).astype(BF16), v_ref[:, cs].astype(BF16)]
            if dil > 1:
                pm = _perm(tr, dil)
                vals = [_permute(pm, a, 1) for a in vals]
            for o_ref, a in zip(out_refs[g::ng], vals):
                _store_dilated(o_ref, a, dil, d)

    outs = pl.pallas_call(body, grid=(t // tr,),
                          in_specs=[_row(tr, w, 0), _row(tr, w, 1), _row(tr, w, 2), _row(tr, 1), _full((1, HEAD_DIM))],
                          out_specs=[_dilated_spec(tr, dil, d) for dil in dils] * 3,
                          out_shape=[SDS((t // dil, dil * d), BF16) for dil in dils] * 3,
                          compiler_params=_params('parallel'), name=name)(qkv, qkv, qkv, pos, invf)
    return outs[:ng], outs[ng:2 * ng], outs[2 * ng:]


def _rope_bwd(dqs, dks, dvs, pos, invf, dils, name):
    ng = len(dils)
    t = dqs[0].shape[0] * dils[0]
    d = dqs[0].shape[1] // dils[0]
    w = ng * d
    tr = _row_tile(t)

    def body(*refs):
        dq_refs, dk_refs, dv_refs = refs[:ng], refs[ng:2 * ng], refs[2 * ng:3 * ng]
        pos_ref, invf_ref, o_ref = refs[3 * ng:]
        pos_f = jnp.broadcast_to(pos_ref[...].astype(F32), (tr, HEAD_DIM))
        for g, dil in enumerate(dils):
            pos_g = pos_f if dil == 1 else _permute(_perm(tr, dil), pos_f, 3)
            lane, cos, sin = _rope_tables(pos_g, invf_ref[...], -1.0)
            vals = [_rope_heads(_load_dilated(dq_refs[g], dil, d), lane, cos, sin),
                    _rope_heads(_load_dilated(dk_refs[g], dil, d), lane, cos, sin), _load_dilated(dv_refs[g], dil, d)]
            back = _perm(tr, dil, inverse=True) if dil > 1 else None
            for sec, a in enumerate(vals):
                a = a.astype(BF16)
                if dil > 1:
                    a = _permute(back, a, 1)
                o_ref[:, sec * w + g * d:sec * w + (g + 1) * d] = a.astype(o_ref.dtype)

    return pl.pallas_call(body, grid=(t // tr,),
                          in_specs=[_dilated_spec(tr, dil, d) for dil in dils] * 3 + [_row(tr, 1), _full((1, HEAD_DIM))],
                          out_specs=_row(tr, 3 * w), out_shape=SDS((t, 3 * w), BF16),
                          compiler_params=_params('parallel'), name=name)(*dqs, *dks, *dvs, pos, invf)


def _dilate_many(arrs, dil, terms, out_dtypes, name):
    t, d = arrs[0].shape
    tr = _row_tile(t)
    na = len(arrs)

    def body(*refs):
        pm = _perm(tr, dil)
        for a_ref, o_ref, k in zip(refs[:na], refs[na:], terms):
            _store_dilated(o_ref, _permute(pm, a_ref[...], k), dil, d)

    return pl.pallas_call(body, grid=(t // tr,), in_specs=[_row(tr, d)] * na,
                          out_specs=[_dilated_spec(tr, dil, d)] * na,
                          out_shape=[SDS((t // dil, dil * d), dt) for dt in out_dtypes],
                          compiler_params=_params('parallel'), name=name)(*arrs)


def _attn_masks():
    qi = lax.broadcasted_iota(jnp.int32, (ATTN_BLOCK, ATTN_BLOCK), 0)
    kj = lax.broadcasted_iota(jnp.int32, (ATTN_BLOCK, ATTN_BLOCK), 1)
    return kj >= qi, kj <= qi


def _attn_cols(l, width):
    ncol = width // HEAD_DIM
    cpb = max(1, min(ncol, 32 // (l // ATTN_BLOCK)))
    assert ncol % cpb == 0
    return cpb


def _attn_fwd(q, k, v, name):
    l, width = q.shape
    cpb = _attn_cols(l, width)
    nb = l // ATTN_BLOCK
    scale = HEAD_DIM ** -0.5

    def body(q_ref, k_ref, v_ref, o_ref, lse_ref):
        m_prev, m_cur = _attn_masks()
        for col in range(cpb):
            cs = slice(col * HEAD_DIM, (col + 1) * HEAD_DIM)

            def step(b, carry, cs=cs):
                r0 = pl.multiple_of(b * ATTN_BLOCK, ATTN_BLOCK)
                rp = pl.multiple_of(jnp.maximum(b - 1, 0) * ATTN_BLOCK, ATTN_BLOCK)
                qb = q_ref[pl.ds(r0, ATTN_BLOCK), cs]
                s_p = lax.dot_general(qb, k_ref[pl.ds(rp, ATTN_BLOCK), cs], _DIMS['nt'], preferred_element_type=F32) * scale
                s_c = lax.dot_general(qb, k_ref[pl.ds(r0, ATTN_BLOCK), cs], _DIMS['nt'], preferred_element_type=F32) * scale
                s_p = jnp.where(jnp.logical_and(m_prev, b > 0), s_p, NEG)
                s_c = jnp.where(m_cur, s_c, NEG)
                m = jnp.maximum(jnp.max(s_p, axis=-1, keepdims=True), jnp.max(s_c, axis=-1, keepdims=True))
                p_p, p_c = jnp.exp(s_p - m), jnp.exp(s_c - m)
                den = jnp.sum(p_p, axis=-1, keepdims=True) + jnp.sum(p_c, axis=-1, keepdims=True)
                acc = jnp.dot(p_p.astype(BF16), v_ref[pl.ds(rp, ATTN_BLOCK), cs], preferred_element_type=F32)
                acc += jnp.dot(p_c.astype(BF16), v_ref[pl.ds(r0, ATTN_BLOCK), cs], preferred_element_type=F32)
                o_ref[pl.ds(r0, ATTN_BLOCK), cs] = acc / den
                lse_ref[pl.ds(r0, ATTN_BLOCK), cs] = jnp.broadcast_to(m + jnp.log(den), (ATTN_BLOCK, HEAD_DIM))
                return carry

            lax.fori_loop(0, nb, step, 0, unroll=min(nb, 8))

    spec = pl.BlockSpec((l, cpb * HEAD_DIM), lambda j: (0, j))
    return pl.pallas_call(body, grid=(width // (cpb * HEAD_DIM),), in_specs=[spec] * 3, out_specs=[spec] * 2,
                          out_shape=[SDS((l, width), F32)] * 2,
                          compiler_params=_params('parallel'), name=name)(q, k, v)


def _attn_bwd(q, k, v, do, lse, delta, name):
    l, width = q.shape
    cpb = _attn_cols(l, width)
    nb = l // ATTN_BLOCK
    scale = HEAD_DIM ** -0.5

    def body(q_ref, k_ref, v_ref, do_ref, lse_ref, dl_ref, dq_ref, dk_ref, dv_ref):
        m_prev, m_cur = _attn_masks()
        dk_ref[...] = jnp.zeros_like(dk_ref)
        dv_ref[...] = jnp.zeros_like(dv_ref)
        for col in range(cpb):
            cs = slice(col * HEAD_DIM, (col + 1) * HEAD_DIM)

            def step(b, carry, cs=cs):
                r0 = pl.multiple_of(b * ATTN_BLOCK, ATTN_BLOCK)
                rp = pl.multiple_of(jnp.maximum(b - 1, 0) * ATTN_BLOCK, ATTN_BLOCK)
                qb, dob = q_ref[pl.ds(r0, ATTN_BLOCK), cs], do_ref[pl.ds(r0, ATTN_BLOCK), cs].astype(BF16)
                kp, kc = k_ref[pl.ds(rp, ATTN_BLOCK), cs], k_ref[pl.ds(r0, ATTN_BLOCK), cs]
                vp, vc = v_ref[pl.ds(rp, ATTN_BLOCK), cs], v_ref[pl.ds(r0, ATTN_BLOCK), cs]
                lse_b = lse_ref[pl.ds(r0, ATTN_BLOCK), cs]
                dl_b = dl_ref[pl.ds(r0, ATTN_BLOCK), cs]
                s_p = lax.dot_general(qb, kp, _DIMS['nt'], preferred_element_type=F32) * scale
                s_c = lax.dot_general(qb, kc, _DIMS['nt'], preferred_element_type=F32) * scale
                p_p = jnp.exp(jnp.where(jnp.logical_and(m_prev, b > 0), s_p, NEG) - lse_b)
                p_c = jnp.exp(jnp.where(m_cur, s_c, NEG) - lse_b)
                dp_p = lax.dot_general(dob, vp, _DIMS['nt'], preferred_element_type=F32)
                dp_c = lax.dot_general(dob, vc, _DIMS['nt'], preferred_element_type=F32)
                ds_p = (p_p * (dp_p - dl_b) * scale).astype(BF16)
                ds_c = (p_c * (dp_c - dl_b) * scale).astype(BF16)
                dq_ref[pl.ds(r0, ATTN_BLOCK), cs] = (jnp.dot(ds_p, kp, preferred_element_type=F32)
                                                     + jnp.dot(ds_c, kc, preferred_element_type=F32))
                dk_ref[pl.ds(rp, ATTN_BLOCK), cs] += lax.dot_general(ds_p, qb, _DIMS['tn'], preferred_element_type=F32)
                dk_ref[pl.ds(r0, ATTN_BLOCK), cs] += lax.dot_general(ds_c, qb, _DIMS['tn'], preferred_element_type=F32)
                dv_ref[pl.ds(rp, ATTN_BLOCK), cs] += lax.dot_general(p_p.astype(BF16), dob, _DIMS['tn'], preferred_element_type=F32)
                dv_ref[pl.ds(r0, ATTN_BLOCK), cs] += lax.dot_general(p_c.astype(BF16), dob, _DIMS['tn'], preferred_element_type=F32)
                return carry

            lax.fori_loop(0, nb, step, 0, unroll=min(nb, 4))

    spec = pl.BlockSpec((l, cpb * HEAD_DIM), lambda j: (0, j))
    return pl.pallas_call(body, grid=(width // (cpb * HEAD_DIM),), in_specs=[spec] * 6, out_specs=[spec] * 3,
                          out_shape=[SDS((l, width), F32)] * 3,
                          compiler_params=_params('parallel'), name=name)(q, k, v, do, lse, delta)


def _attn_combine(os_, lses, dils, name):
    ng = len(dils)
    t = os_[0].shape[0] * dils[0]
    d = os_[0].shape[1] // dils[0]
    tr = _row_tile(t)

    def body(*refs):
        o_refs, l_refs, o_out, lse_out = refs[:ng], refs[ng:2 * ng], refs[2 * ng], refs[2 * ng + 1]
        ovs, ls = [], []
        for g, dil in enumerate(dils):
            ov, lv = _load_dilated(o_refs[g], dil, d), _load_dilated(l_refs[g], dil, d)
            if dil > 1:
                back = _perm(tr, dil, inverse=True)
                ov, lv = _permute(back, ov, 2), _permute(back, lv, 3)
            ovs.append(ov)
            ls.append(lv)
        m = functools.reduce(jnp.maximum, ls)
        ws = [jnp.exp(x - m) for x in ls]
        den = functools.reduce(lambda a, b: a + b, ws)
        acc = functools.reduce(lambda a, b: a + b, [w * o for w, o in zip(ws, ovs)])
        o_out[...] = (acc / den).astype(o_out.dtype)
        lse_out[...] = m + jnp.log(den)

    return pl.pallas_call(body, grid=(t // tr,), in_specs=[_dilated_spec(tr, dil, d) for dil in dils] * 2,
                          out_specs=[_row(tr, d)] * 2, out_shape=[SDS((t, d), BF16), SDS((t, d), F32)],
                          compiler_params=_params('parallel'), name=name)(*os_, *lses)


def _delta_epilogue(acc, o):
    prod = acc * o.astype(F32)
    segs = [jnp.broadcast_to(jnp.sum(prod[:, s:s + HEAD_DIM], axis=-1, keepdims=True), (acc.shape[0], HEAD_DIM))
            for s in range(0, acc.shape[1], HEAD_DIM)]
    return acc, jnp.concatenate(segs, axis=-1)


LRU_TILE = 128


def _lru_gates(xr, wa_ref, ba, wx_ref, bx, lam):
    nb = wa_ref.shape[0]
    xb = xr.astype(BF16)
    ra = jnp.concatenate([jnp.dot(xb[:, n * LRU_BLOCK:(n + 1) * LRU_BLOCK], wa_ref[n], preferred_element_type=F32)
                          for n in range(nb)], axis=-1) + ba
    ia = jnp.concatenate([jnp.dot(xb[:, n * LRU_BLOCK:(n + 1) * LRU_BLOCK], wx_ref[n], preferred_element_type=F32)
                          for n in range(nb)], axis=-1) + bx
    r, ig = _sigmoid(ra), _sigmoid(ia)
    sp = _softplus(-lam)
    log_a = -LRU_C * r * sp
    a = jnp.exp(log_a)
    mult = jnp.sqrt(-_expm1(2.0 * log_a))
    return xb, r, ig, sp, a, mult


def _lru_fwd(z, cw, cb, wa, ba, wx, bx, lam, name):
    t, c2 = z.shape
    c = c2 // 2
    nb = c // LRU_BLOCK
    tr = _row_tile(t, LRU_TILE)

    def body(g_ref, x_ref, xp_ref, cw_ref, cb_ref, wa_ref, ba_ref, wx_ref, bx_ref, lam_ref,
             y_ref, hs_ref, xr_ref, car_ref):
        i = pl.program_id(0)

        @pl.when(i == 0)
        def _():
            car_ref[...] = jnp.zeros_like(car_ref)

        x0 = x_ref[...]
        xp = jnp.where(i > 0, xp_ref[SUB:HALO, :], 0.0)
        cwv = cw_ref[...]
        xr = (cb_ref[...] + cwv[3:4] * x0 + cwv[2:3] * _shift_down(x0, 1, xp)
              + cwv[1:2] * _shift_down(x0, 2, xp) + cwv[0:1] * _shift_down(x0, 3, xp))
        xr_ref[...] = xr
        _, _, ig, _, a, mult = _lru_gates(xr, wa_ref, ba_ref[...], wx_ref, bx_ref[...], lam_ref[...])
        u = mult * (ig * xr)
        row = lax.broadcasted_iota(jnp.int32, (SUB, c), 0)
        car = car_ref[...]
        for j in range(tr // SUB):
            ab, ub = a[j * SUB:(j + 1) * SUB], u[j * SUB:(j + 1) * SUB]
            for s in (1, 2, 4):
                a_sh = jnp.where(row >= s, pltpu.roll(ab, s, 0), 1.0)
                u_sh = jnp.where(row >= s, pltpu.roll(ub, s, 0), 0.0)
                ub = ab * u_sh + ub
                ab = ab * a_sh
            hb = ub + ab * car
            hs_ref[j * SUB:(j + 1) * SUB, :] = hb
            car = jnp.broadcast_to(hb[SUB - 1:SUB], (SUB, c))
        car_ref[...] = car
        gl, _ = _gelu_and_grad(g_ref[...])
        y_ref[...] = (hs_ref[...] * gl).astype(y_ref.dtype)

    return pl.pallas_call(
        body, grid=(t // tr,),
        in_specs=[_row(tr, c, 0), _row(tr, c, 1), _halo_prev(tr, c, 1), _full((4, c)), _full((1, c)),
                  _full((nb, LRU_BLOCK, LRU_BLOCK)), _full((1, c)), _full((nb, LRU_BLOCK, LRU_BLOCK)), _full((1, c)), _full((1, c))],
        out_specs=[_row(tr, c)] * 3,
        out_shape=[SDS((t, c), BF16), SDS((t, c), F32), SDS((t, c), F32)],
        scratch_shapes=[pltpu.VMEM((SUB, c), F32)],
        compiler_params=_params('arbitrary'), name=name)(
            z, z, z, cw, cb.reshape(1, c), wa, ba.reshape(1, c), wx, bx.reshape(1, c), lam.reshape(1, c))


def _lru_bwd(dy, z, xr, hs, cw, wa, ba, wx, bx, lam, name):
    t, c2 = z.shape
    c = c2 // 2
    nb = c // LRU_BLOCK
    tr = _row_tile(t, LRU_TILE)
    nt = t // tr

    def rev(col=0):
        return pl.BlockSpec((tr, c), lambda i, col=col: (nt - 1 - i, col))

    def rev_prev(col=0):
        return pl.BlockSpec((HALO, c), lambda i, col=col: (jnp.maximum((nt - 1 - i) * (tr // HALO) - 1, 0), col))

    def body(dy_ref, g_ref, x_ref, xp_ref, xr_ref, hs_ref, hp_ref, cw_ref, wa_ref, ba_ref, wx_ref, bx_ref, lam_ref,
             dz_ref, dwa_ref, dwx_ref, dvec_ref, lcar_ref, ahead_ref, dxhead_ref, lam_s):
        i = pl.program_id(0)
        first_tile = i == nt - 1

        @pl.when(i == 0)
        def _():
            lcar_ref[...] = jnp.zeros_like(lcar_ref)
            ahead_ref[...] = jnp.zeros_like(ahead_ref)
            dxhead_ref[...] = jnp.zeros_like(dxhead_ref)
            dwa_ref[...] = jnp.zeros_like(dwa_ref)
            dwx_ref[...] = jnp.zeros_like(dwx_ref)
            dvec_ref[...] = jnp.zeros_like(dvec_ref)

        xrv = xr_ref[...]
        lamv = lam_ref[...]
        xb, r, ig, sp, a, mult = _lru_gates(xrv, wa_ref, ba_ref[...], wx_ref, bx_ref[...], lamv)
        hsv = hs_ref[...]
        dyv = dy_ref[...]
        gl, dgl = _gelu_and_grad(g_ref[...])
        dhs = dyv * gl
        dz_ref[:, :c] = (dyv * hsv * dgl).astype(dz_ref.dtype)

        a_next = _shift_up(a, 1, ahead_ref[...])
        row = lax.broadcasted_iota(jnp.int32, (SUB, c), 0)
        car = lcar_ref[...]
        for j in reversed(range(tr // SUB)):
            ab, ub = a_next[j * SUB:(j + 1) * SUB], dhs[j * SUB:(j + 1) * SUB]
            for s in (1, 2, 4):
                a_sh = jnp.where(row < SUB - s, pltpu.roll(ab, SUB - s, 0), 1.0)
                u_sh = jnp.where(row < SUB - s, pltpu.roll(ub, SUB - s, 0), 0.0)
                ub = ab * u_sh + ub
                ab = ab * a_sh
            lb = ub + ab * car
            lam_s[j * SUB:(j + 1) * SUB, :] = lb
            car = jnp.broadcast_to(lb[0:1], (SUB, c))
        lcar_ref[...] = car
        ahead_ref[...] = a[0:SUB]
        lmb = lam_s[...]

        hp = jnp.where(first_tile, 0.0, hp_ref[SUB:HALO, :])
        h_prev = _shift_down(hsv, 1, hp)
        d_a = lmb * h_prev
        d_mult = lmb * (ig * xrv)
        d_ixr = lmb * mult
        d_ig = d_ixr * xrv
        dxr = d_ixr * ig
        d_la = d_a * a - d_mult * (a * a) / mult
        d_r = d_la * (-LRU_C * sp)
        d_sp = jnp.sum(d_la * (-LRU_C * r), axis=0, keepdims=True)
        d_ra = d_r * r * (1.0 - r)
        d_ia = d_ig * ig * (1.0 - ig)
        d_rab, d_iab = d_ra.astype(BF16), d_ia.astype(BF16)
        parts = []
        for n in range(nb):
            cs = slice(n * LRU_BLOCK, (n + 1) * LRU_BLOCK)
            parts.append(lax.dot_general(d_rab[:, cs], wa_ref[n], _DIMS['nt'], preferred_element_type=F32)
                         + lax.dot_general(d_iab[:, cs], wx_ref[n], _DIMS['nt'], preferred_element_type=F32))
            dwa_ref[n] += lax.dot_general(xb[:, cs], d_rab[:, cs], _DIMS['tn'], preferred_element_type=F32)
            dwx_ref[n] += lax.dot_general(xb[:, cs], d_iab[:, cs], _DIMS['tn'], preferred_element_type=F32)
        dxr = dxr + jnp.concatenate(parts, axis=-1)

        cwv = cw_ref[...]
        nxt = dxhead_ref[...]
        dx0 = (cwv[3:4] * dxr + cwv[2:3] * _shift_up(dxr, 1, nxt) + cwv[1:2] * _shift_up(dxr, 2, nxt)
               + cwv[0:1] * _shift_up(dxr, 3, nxt))
        dxhead_ref[...] = dxr[0:SUB]
        dz_ref[:, c:] = dx0.astype(dz_ref.dtype)

        x0 = x_ref[...]
        xp = jnp.where(first_tile, 0.0, xp_ref[SUB:HALO, :])
        sums = [jnp.sum(d_ra, axis=0, keepdims=True), jnp.sum(d_ia, axis=0, keepdims=True),
                d_sp * (-_sigmoid(-lamv)), jnp.sum(dxr, axis=0, keepdims=True),
                jnp.sum(dxr * _shift_down(x0, 3, xp), axis=0, keepdims=True),
                jnp.sum(dxr * _shift_down(x0, 2, xp), axis=0, keepdims=True),
                jnp.sum(dxr * _shift_down(x0, 1, xp), axis=0, keepdims=True),
                jnp.sum(dxr * x0, axis=0, keepdims=True)]
        dvec_ref[...] += jnp.concatenate(sums, axis=0)

    wspec = _full((nb, LRU_BLOCK, LRU_BLOCK))
    return pl.pallas_call(
        body, grid=(nt,),
        in_specs=[rev(), rev(0), rev(1), rev_prev(1), rev(), rev(), rev_prev(), _full((4, c)),
                  wspec, _full((1, c)), wspec, _full((1, c)), _full((1, c))],
        out_specs=[pl.BlockSpec((tr, c2), lambda i: (nt - 1 - i, 0)), wspec, wspec, _full((SUB, c))],
        out_shape=[SDS((t, c2), BF16), SDS((nb, LRU_BLOCK, LRU_BLOCK), F32), SDS((nb, LRU_BLOCK, LRU_BLOCK), F32),
                   SDS((SUB, c), F32)],
        scratch_shapes=[pltpu.VMEM((SUB, c), F32), pltpu.VMEM((SUB, c), F32), pltpu.VMEM((SUB, c), F32),
                        pltpu.VMEM((tr, c), F32)],
        compiler_params=_params('arbitrary'), name=name)(
            dy, z, z, z, xr, hs, hs, cw, wa, ba.reshape(1, c), wx, bx.reshape(1, c), lam.reshape(1, c))


def _local_step(x, p, pos, target, rep, weights_for_layer, emit_grads):
    t, d = x.shape
    depth = p.shape[0]
    w = rep
    half = ROPE_DIM // 2
    invf = ROPE_THETA ** (-2.0 * jnp.arange(half, dtype=F32) / ROPE_DIM)
    invf = jnp.concatenate([invf, invf, jnp.zeros((HEAD_DIM - ROPE_DIM,), F32)]).reshape(1, HEAD_DIM)
    dils = tuple(dil for _, dil in DILATED_PATTERNS)
    saved = []
    h = x
    for i in range(depth):
        kind, j = i % N_MIXERS, i // N_MIXERS
        wl, tok = weights_for_layer(i, 'mixer', h)
        s = {'h0': h, 'wl': wl}
        hn = _rms_fwd(h, w['norm_mix'][i], f'rms_mix_fwd_{i}')
        s['hn'] = hn
        if kind == 0:
            z = _mm(hn, wl['w_in'], 'nn', f'sc_in_{i}', dep=tok)
            y = _sc_fwd(z, wl['small'], f'sc_conv_fwd_{i}')
            h1 = _mm(y, wl['w_out'], 'nn', f'sc_out_{i}', extras=(h,), epi=lambda acc, res: (acc + res,))
            s.update(z=z, y=y)
        elif kind == 1:
            qkv = _mm(hn, wl['w_in'], 'nn', f'attn_qkv_{i}', dep=tok)
            qs, ks, vs = _rope_fwd(qkv, pos, invf, dils, f'rope_fwd_{i}')
            views = list(zip(qs, ks, vs))
            os_, lses = zip(*[_attn_fwd(qg, kg, vg, f'attn_fwd_{i}_g{g}') for g, (qg, kg, vg) in enumerate(views)])
            o, lse = _attn_combine(os_, lses, dils, f'attn_combine_{i}')
            h1 = _mm(o, wl['w_out'], 'nn', f'attn_out_{i}', extras=(h,), epi=lambda acc, res: (acc + res,))
            s.update(views=views, o=o, lse=lse)
        else:
            z = _mm(hn, wl['w_in'], 'nn', f'lru_in_{i}', dep=tok)
            sm = wl['small']
            y, hs, xr = _lru_fwd(z, sm[0:4], sm[4:5], w['lru_w_a'][j], sm[5:6], w['lru_w_x'][j], sm[6:7], sm[7:8],
                                 f'lru_fwd_{i}')
            h1 = _mm(y, wl['w_out'], 'nn', f'lru_out_{i}', extras=(h,), epi=lambda acc, res: (acc + res,))
            s.update(z=z, y=y, hs=hs, xr=xr)
        s['h1'] = h1
        more, tok = weights_for_layer(i, 'mlp', h1)
        wl.update(more)
        hm = _rms_fwd(h1, w['norm_mlp'][i], f'rms_mlp_fwd_{i}')
        u = _mm(hm, wl['mlp_up'], 'nn', f'mlp_up_{i}', out_dtypes=(BF16,), dep=tok)
        h2 = _mm(u, wl['mlp_down'], 'nn', f'mlp_down_{i}', a_pro=_relu2, extras=(h1,), epi=lambda acc, res: (acc + res,))
        hp = _rms_fwd(h2, w['norm_ple'][i], f'rms_ple_fwd_{i}')
        pp = _mm(p[i], wl['ple_proj'], 'nn', f'ple_proj_{i}', out_dtypes=(BF16,))
        h3, gate = _mm(hp, wl['ple_gate'], 'nn', f'ple_gate_{i}', out_dtypes=(F32, BF16), extras=(pp, h2),
                       epi=lambda acc, ppv, res: (res + _sigmoid(acc) * ppv, _sigmoid(acc)))
        s.update(hm=hm, u=u, h2=h2, hp=hp, pp=pp, gate=gate)
        saved.append(s)
        h = h3

    dh, loss, dg_final = _head(h, w['norm_final'], target, 'loss_head')
    grads = {n: [None] * depth for n in ('norm_mix', 'norm_mlp', 'norm_ple')}
    grads['norm_final'] = dg_final.reshape(d)
    started = None
    for i in reversed(range(depth)):
        kind, j = i % N_MIXERS, i // N_MIXERS
        s = saved[i]
        wl, gl = s['wl'], {}
        dpp, dgl = _ple_bwd_gate(dh, s['gate'], s['pp'], f'ple_bwd_gate_{i}')
        gl['ple_proj'] = _mm(p[i], dpp, 'tn', f'ple_dproj_{i}', out_dtypes=(BF16,), dep=started)
        gl['ple_gate'] = _mm(s['hp'], dgl, 'tn', f'ple_dgate_{i}', out_dtypes=(BF16,))
        dhp = _mm(dgl, wl['ple_gate'], 'nt', f'ple_dhp_{i}', out_dtypes=(BF16,))
        dh, dg = _rms_bwd(s['h2'], w['norm_ple'][i], dhp, dh, f'rms_ple_bwd_{i}')
        grads['norm_ple'][i] = dg.reshape(d)
        du = _mm(dh, wl['mlp_down'], 'nt', f'mlp_du_{i}', out_dtypes=(BF16,), extras=(s['u'],),
                 epi=lambda acc, uv: (acc * 2.0 * jnp.maximum(uv.astype(F32), 0.0),))
        gl['mlp_down'] = _mm(s['u'], dh, 'tn', f'mlp_ddown_{i}', out_dtypes=(BF16,), a_pro=_relu2)
        gl['mlp_up'] = _mm(s['hm'], du, 'tn', f'mlp_dup_{i}', out_dtypes=(BF16,), out_stacked=True)
        dhm = _mm(du, wl['mlp_up'], 'nt', f'mlp_dhm_{i}', out_dtypes=(BF16,))
        dh, dg = _rms_bwd(s['h1'], w['norm_mlp'][i], dhm, dh, f'rms_mlp_bwd_{i}')
        grads['norm_mlp'][i] = dg.reshape(d)
        started = emit_grads(i, 'mlp', gl, loss if i == depth - 1 else None)
        gl = {}
        if kind == 0:
            dy = _mm(dh, wl['w_out'], 'nt', f'sc_dy_{i}', dep=started)
            gl['w_out'] = _mm(s['y'], dh, 'tn', f'sc_dout_{i}', out_dtypes=(BF16,))
            dz, dwc = _sc_bwd(dy, s['z'], wl['small'], f'sc_conv_bwd_{i}')
            gl['small'] = dwc
            gl['w_in'] = _mm(s['hn'], dz, 'tn', f'sc_din_{i}', out_dtypes=(BF16,))
            started = emit_grads(i, 'mixer', gl)
            dhn = _mm(dz, wl['w_in'], 'nt', f'sc_dhn_{i}', out_dtypes=(BF16,), dep=started)
        elif kind == 1:
            do, delta = _mm(dh, wl['w_out'], 'nt', f'attn_do_{i}', out_dtypes=(BF16, F32), extras=(s['o'],),
                            epi=_delta_epilogue, dep=started)
            gl['w_out'] = _mm(s['o'], dh, 'tn', f'attn_dwo_{i}', out_dtypes=(BF16,))
            rows_in = {1: (do, s['lse'], delta)}
            for dil in dils:
                if dil not in rows_in:
                    rows_in[dil] = _dilate_many([do, s['lse'], delta], dil, (1, 3, 3), (BF16, F32, F32),
                                                f'attn_dilate_{i}_d{dil}')
            dqs, dks, dvs = zip(*[_attn_bwd(*s['views'][g], *rows_in[dil], f'attn_bwd_{i}_g{g}')
                                  for g, dil in enumerate(dils)])
            dqkv = _rope_bwd(dqs, dks, dvs, pos, invf, dils, f'rope_bwd_{i}')
            gl['w_in'] = _mm(s['hn'], dqkv, 'tn', f'attn_dqkv_{i}', out_dtypes=(BF16,), out_stacked=True)
            started = emit_grads(i, 'mixer', gl)
            dhn = _mm(dqkv, wl['w_in'], 'nt', f'attn_dhn_{i}', out_dtypes=(BF16,), dep=started)
        else:
            dy = _mm(dh, wl['w_out'], 'nt', f'lru_dy_{i}', dep=started)
            gl['w_out'] = _mm(s['y'], dh, 'tn', f'lru_dout_{i}', out_dtypes=(BF16,))
            sm = wl['small']
            dz, dwa, dwx, dvec = _lru_bwd(dy, s['z'], s['xr'], s['hs'], sm[0:4], w['lru_w_a'][j], sm[5:6],
                                          w['lru_w_x'][j], sm[6:7], sm[7:8], f'lru_bwd_{i}')
            gl['gates'], gl['small'] = (dwa, dwx), dvec
            gl['w_in'] = _mm(s['hn'], dz, 'tn', f'lru_din_{i}', out_dtypes=(BF16,))
            started = emit_grads(i, 'mixer', gl)
            dhn = _mm(dz, wl['w_in'], 'nt', f'lru_dhn_{i}', out_dtypes=(BF16,), dep=started)
        dh, dg = _rms_bwd(s['h0'], w['norm_mix'][i], dhn, dh, f'rms_mix_bwd_{i}')
        grads['norm_mix'][i] = dg.reshape(d)
        started = None
    return loss, dh, grads


_MESH = pl.DeviceIdType.MESH
_ANY = pl.BlockSpec(memory_space=pl.ANY)


def _block_view(ref, kind, idx):
    if kind == 'stack':
        return ref.at[idx]
    r = ref.shape[0] // N_DEV
    return ref.at[pl.ds(idx * r, r)]


def _gather_many(arrs, kinds, name, after=None):
    n = len(arrs)
    after = [] if after is None else [after]
    out_shapes = [SDS((N_DEV,) + a.shape if kd == 'stack' else (N_DEV * a.shape[0],) + a.shape[1:], a.dtype)
                  for a, kd in zip(arrs, kinds)]

    def body(*refs):
        x_refs, out_refs = refs[:n], refs[n + len(after):2 * n + len(after)]
        send_sems, recv_sems, local_sems = refs[2 * n + len(after):]
        x, y, c = lax.axis_index('x'), lax.axis_index('y'), lax.axis_index('c')
        me, sibling = (x, y, c), (x, y, 1 - c)
        chips = [(1 - x, y), (x, 1 - y), (1 - x, 1 - y)]

        def slab(t, px, py, pc):
            return _block_view(out_refs[t], kinds[t], 4 * px + 2 * py + pc)

        def copy(t, k, block, to, src=None):
            return pltpu.make_async_remote_copy(
                src_ref=slab(t, *block) if src is None else src, dst_ref=slab(t, *block),
                send_sem=send_sems.at[7 * t + k], recv_sem=recv_sems.at[7 * t + k], device_id=to, device_id_type=_MESH)

        mine = [pltpu.make_async_copy(x_refs[t], slab(t, *me), local_sems.at[t]) for t in range(n)]
        for cp in mine:
            cp.start()
        first = [copy(t, 0, me, sibling, src=x_refs[t]) for t in range(n)]
        first += [copy(t, 1 + j, me, (*chip, c), src=x_refs[t]) for j, chip in enumerate(chips) for t in range(n)]
        for cp in first:
            cp.start()
        passed = []
        for j, chip in enumerate(chips):
            for t in range(n):
                copy(t, 1 + j, (*chip, c), me).wait_recv()
                passed.append(copy(t, 4 + j, (*chip, c), sibling))
                passed[-1].start()
        for t in range(n):
            copy(t, 0, sibling, me).wait_recv()
            for j, chip in enumerate(chips):
                copy(t, 4 + j, (*chip, 1 - c), me).wait_recv()
        for cp in first + passed:
            cp.wait_send()
        for cp in mine:
            cp.wait()

    return pl.pallas_call(
        body, out_shape=out_shapes, in_specs=[_ANY] * (n + len(after)), out_specs=[_ANY] * n,
        scratch_shapes=[pltpu.SemaphoreType.DMA((7 * n,)), pltpu.SemaphoreType.DMA((7 * n,)), pltpu.SemaphoreType.DMA((n,))],
        name=name)(*arrs, *after)


_HBM = pl.BlockSpec(memory_space=pltpu.HBM)
_SEM = pl.BlockSpec(memory_space=pltpu.SEMAPHORE)
_EFFECT = pltpu.SideEffectType.DATAFLOW_SIDE_EFFECTING


def _direct_copies(mode, kinds, src_refs, land_refs, send_sems, recv_sems):
    x, y, c = lax.axis_index('x'), lax.axis_index('y'), lax.axis_index('c')
    my_idx = 4 * x + 2 * y + c
    copies = []
    for k in range(1, N_DEV):
        px, py, pc = (1 - x if k & 4 else x, 1 - y if k & 2 else y, 1 - c if k & 1 else c)
        for t, kd in enumerate(kinds):
            if mode == 'gather':
                src, dst = src_refs[t], _block_view(land_refs[t], kd, my_idx)
            else:
                src, dst = _block_view(src_refs[t], kd, 4 * px + 2 * py + pc), land_refs[t].at[my_idx]
            copies.append(pltpu.make_async_remote_copy(
                src_ref=src, dst_ref=dst, send_sem=send_sems.at[7 * t + k - 1], recv_sem=recv_sems.at[7 * t + k - 1],
                device_id=(px, py, pc), device_id_type=_MESH))
    return copies


def _own_part(mode, kind, src, land):
    idx = 4 * lax.axis_index('x') + 2 * lax.axis_index('y') + lax.axis_index('c')
    zeros = (0,) * (src.ndim - 1)
    if mode == 'gather':
        part = src
    elif kind == 'stack':
        part = lax.dynamic_index_in_dim(src, idx, 0, keepdims=False)
    else:
        r = src.shape[0] // N_DEV
        part = lax.dynamic_slice_in_dim(src, idx * r, r, 0)
    if mode == 'gather' and kind == 'rows':
        return lax.dynamic_update_slice(land, part, (idx * part.shape[0],) + zeros)
    return lax.dynamic_update_slice(land, part[None], (idx,) + (0,) * part.ndim)


def _send_start(mode, srcs, kinds, name, after=None):
    n = len(srcs)
    after = [] if after is None else [after]
    lands = []
    for a, kd in zip(srcs, kinds):
        if mode == 'gather':
            shape = (N_DEV,) + a.shape if kd == 'stack' else (N_DEV * a.shape[0],) + a.shape[1:]
        else:
            shape = a.shape if kd == 'stack' else (N_DEV, a.shape[0] // N_DEV) + a.shape[1:]
        lands.append(_own_part(mode, kd, a, lax.empty(shape, a.dtype)))

    def body(*refs):
        src_refs, land_refs = refs[:n], refs[n:2 * n]
        send_sems, recv_sems = refs[2 * n + len(after):2 * n + len(after) + 2]
        token = refs[-1]
        for cp in _direct_copies(mode, kinds, src_refs, land_refs, send_sems, recv_sems):
            cp.start()
        token[...] = jnp.zeros_like(token)

    outs = pl.pallas_call(
        body, name=name,
        out_shape=(pltpu.SemaphoreType.DMA((7 * n,)), pltpu.SemaphoreType.DMA((7 * n,)),
                   *[pltpu.HBM(a.shape, a.dtype) for a in srcs + lands], SDS((SUB, 128), F32)),
        in_specs=[_HBM] * (2 * n) + [_ANY] * len(after),
        out_specs=(_SEM, _SEM, *[_HBM] * (2 * n), pl.BlockSpec(memory_space=pltpu.VMEM)),
        input_output_aliases={i: 2 + i for i in range(2 * n)},
        compiler_params=pltpu.CompilerParams(has_side_effects=_EFFECT),
    )(*[pltpu.with_memory_space_constraint(a, pltpu.HBM) for a in srcs + lands], *after)
    return (outs[0], outs[1], list(outs[2:2 + 2 * n])), outs[-1]


def _send_wait(mode, flight, kinds, after, name):
    send, recv, bufs = flight
    n = len(kinds)

    def body(*refs):
        src_refs, land_refs, (send_sems, recv_sems) = refs[:n], refs[n:2 * n], refs[2 * n:2 * n + 2]
        copies = _direct_copies(mode, kinds, src_refs, land_refs, send_sems, recv_sems)
        for cp in copies:
            cp.wait_send()
        for cp in copies:
            cp.wait_recv()

    outs = pl.pallas_call(
        body, name=name, out_shape=[pltpu.HBM(a.shape, a.dtype) for a in bufs],
        in_specs=[_HBM] * (2 * n) + [_SEM, _SEM, _ANY], out_specs=[_HBM] * (2 * n),
        input_output_aliases={i: i for i in range(2 * n)},
        compiler_params=pltpu.CompilerParams(has_side_effects=_EFFECT),
    )(*bufs, send, recv, after)
    return list(outs[n:])


ADAMW_BLOCK_ELEMS = 128 * 1024


def _adamw_sum(wgt, parts, m, v, name):
    nl, r, c = wgt.shape
    assert len(parts) == nl and all(q.shape == (N_DEV, r, c) for q in parts), (name, wgt.shape, [q.shape for q in parts])
    tr = next((t for t in range(min(r, 512), 0, -16) if r % t == 0 and t * c <= ADAMW_BLOCK_ELEMS and t % 16 == 0), r)
    c1 = 1.0 - ADAM_B1 ** ADAM_STEP
    c2 = 1.0 - ADAM_B2 ** ADAM_STEP

    def body(w_ref, m_ref, v_ref, *rest):
        part_refs, (g_ref, d_ref, mo_ref, vo_ref) = rest[:nl], rest[nl:]
        for q in range(nl):
            @pl.when(pl.program_id(0) == q)
            def _(q=q):
                gv = part_refs[q][0].astype(F32)
                for s in range(1, N_DEV):
                    gv = gv + part_refs[q][s].astype(F32)
                mn = ADAM_B1 * m_ref[...] + (1.0 - ADAM_B1) * gv
                vn = ADAM_B2 * v_ref[...] + (1.0 - ADAM_B2) * (gv * gv)
                g_ref[...] = gv
                d_ref[...] = -ADAM_LR * ((mn / c1) / (jnp.sqrt(vn / c2) + ADAM_EPS) + ADAM_WD * w_ref[...])
                mo_ref[...] = mn
                vo_ref[...] = vn

    spec = pl.BlockSpec((None, tr, c), lambda l, i: (l, i, 0))
    part_specs = [pl.BlockSpec((N_DEV, tr, c), lambda l, i, q=q: (0, jnp.where(l == q, i, 0), 0)) for q in range(nl)]
    return pl.pallas_call(body, grid=(nl, r // tr), in_specs=[spec] * 3 + part_specs, out_specs=[spec] * 4,
                          out_shape=[SDS((nl, r, c), F32)] * 4, compiler_params=_params('arbitrary', 'arbitrary'),
                          name=name)(wgt, m, v, *parts)


MIXER_WEIGHTS = {0: ('sc_w_in', 'sc_w_out'), 1: ('attn_w_qkv', 'attn_w_o'), 2: ('lru_w_in', 'lru_w_out')}
STACKED_OPERANDS = ('attn_w_qkv', 'mlp_w_up')
LRU_SMALL = ('lru_conv_w', 'lru_conv_b', 'lru_b_a', 'lru_b_x', 'lru_lambda')


def _layer_items(i):
    w_in, w_out = MIXER_WEIGHTS[i % N_MIXERS]
    j = i // N_MIXERS
    return [('w_in', w_in, j), ('w_out', w_out, j), ('mlp_up', 'mlp_w_up', i), ('mlp_down', 'mlp_w_down', i),
            ('ple_gate', 'ple_w_gate', i), ('ple_proj', 'ple_w_proj', i)]


def _cols_to_full(stacked):
    return jnp.moveaxis(stacked, 0, 1).reshape(stacked.shape[1], -1)


def _full_to_cols(full):
    k, n = full.shape
    return jnp.moveaxis(full.reshape(k, N_DEV, n // N_DEV), 1, 0)


def _pad_to(a, rows):
    return jnp.pad(a, ((0, rows - a.shape[0]), (0, 0)))


def _small_block(src, i):
    kind, j = i % N_MIXERS, i // N_MIXERS
    if kind == 0:
        return _pad_to(src['sc_w_conv'][j], SUB)
    if kind == 2:
        return jnp.concatenate([src[n][j].reshape(-1, src[n].shape[-1]) for n in LRU_SMALL], axis=0)
    return None


def kernel(x, p, positions, norm_mix, norm_mlp, norm_ple, norm_final, sc_w_in, sc_w_conv, sc_w_out, attn_w_qkv, attn_w_o, lru_w_in, lru_conv_w, lru_conv_b, lru_w_a, lru_b_a, lru_w_x, lru_b_x, lru_lambda, lru_w_out, mlp_w_up, mlp_w_down, ple_w_gate, ple_w_proj, loss_target, m_norm_mix, m_norm_mlp, m_norm_ple, m_norm_final, m_sc_w_in, m_sc_w_conv, m_sc_w_out, m_attn_w_qkv, m_attn_w_o, m_lru_w_in, m_lru_conv_w, m_lru_conv_b, m_lru_w_a, m_lru_b_a, m_lru_w_x, m_lru_b_x, m_lru_lambda, m_lru_w_out, m_mlp_w_up, m_mlp_w_down, m_ple_w_gate, m_ple_w_proj, v_norm_mix, v_norm_mlp, v_norm_ple, v_norm_final, v_sc_w_in, v_sc_w_conv, v_sc_w_out, v_attn_w_qkv, v_attn_w_o, v_lru_w_in, v_lru_conv_w, v_lru_conv_b, v_lru_w_a, v_lru_b_a, v_lru_w_x, v_lru_b_x, v_lru_lambda, v_lru_w_out, v_mlp_w_up, v_mlp_w_down, v_ple_w_gate, v_ple_w_proj):
    loc = dict(locals())
    shards = {n: loc[n] for n in WEIGHTS}
    moms = {n: loc['m_' + n] for n in WEIGHTS}
    vels = {n: loc['v_' + n] for n in WEIGHTS}

    depth, t, d = p.shape[0], x.shape[1], x.shape[2]

    def comm_kind(name):
        return 'stack' if SHARD_AXIS[name] == 2 else 'rows'

    part_keys = {'mlp': ('mlp_up', 'mlp_down', 'ple_gate', 'ple_proj'), 'mixer': ('w_in', 'w_out')}
    halves = [(i, part) for i in range(depth) for part in ('mixer', 'mlp')]

    def half_shards(i, part):
        items = [it for it in _layer_items(i) if it[0] in part_keys[part]]
        arrs = [shards[n][idx].astype(BF16) for _, n, idx in items]
        kinds = [comm_kind(n) for _, n, _ in items]
        small = _small_block(shards, i) if part == 'mixer' else None
        if small is not None:
            arrs.append(small)
            kinds.append('stack')
        return items, arrs, kinds

    def half_weights(i, items, kinds, outs):
        wl = {key: (_cols_to_full(o) if kd == 'stack' and n not in STACKED_OPERANDS else o)
              for (key, n, _), kd, o in zip(items, kinds, outs)}
        if len(outs) > len(items):
            wl['small'] = _cols_to_full(outs[-1])[:shards['sc_w_conv'].shape[1] if i % N_MIXERS == 0 else SUB]
        return wl

    first = [half_shards(0, part) for part in ('mixer', 'mlp')]
    outs0 = _gather_many(first[0][1] + first[1][1], first[0][2] + first[1][2], 'gather_weights_0')
    weights0 = {**half_weights(0, first[0][0], first[0][2], outs0[:len(first[0][1])]),
                **half_weights(0, first[1][0], first[1][2], outs0[len(first[0][1]):])}
    pending = {}

    def start_gather(pos, after):
        if pos >= len(halves):
            return None
        i, part = halves[pos]
        items, arrs, kinds = half_shards(i, part)
        flight, token = _send_start('gather', arrs, kinds, f'gather_weights_start_{part}_{i}', after=after)
        pending[pos] = (items, kinds, flight)
        return token

    first_token = start_gather(2, outs0[0])
    second_token = start_gather(3, first_token)

    def weights_for_layer(i, part, h):
        pos = halves.index((i, part))
        if pos == 0:
            return weights0, second_token
        if pos == 1:
            return {}, None
        items, kinds, flight = pending.pop(pos)
        outs = _send_wait('gather', flight, kinds, h, f'gather_weights_wait_{part}_{i}')
        return half_weights(i, items, kinds, outs), start_gather(pos + 2, outs[0])

    exchanges, gate_gathers, total_loss = {}, {}, []

    def gate_block(src, j):
        return jnp.concatenate([src[n][j].reshape(-1, LRU_BLOCK) for n in ('lru_w_a', 'lru_w_x')], axis=0)

    def emit_grads(i, part, gl, loss=None):
        after = None
        if loss is not None:
            total_loss.append(lax.psum(loss[0, 0], ('x', 'y', 'c')))
            after = jnp.full((SUB, 128), total_loss[0], F32)
        if 'gates' in gl:
            blk = gate_block({'lru_w_a': [gl['gates'][0]], 'lru_w_x': [gl['gates'][1]]}, 0)
            gate_gathers[i] = _send_start('gather', [blk], ['stack'], f'gather_gate_grads_start_{i}')[0]
        items = [it for it in _layer_items(i) if it[0] in part_keys[part]]
        kinds = [comm_kind(n) for _, n, _ in items]
        arrs = [_full_to_cols(gl[key]) if kd == 'stack' and gl[key].ndim == 2 else gl[key]
                for (key, _, _), kd in zip(items, kinds)]
        if part == 'mixer' and i % N_MIXERS == 0:
            arrs.append(_full_to_cols(_pad_to(gl['small'], SUB)))
        elif part == 'mixer' and i % N_MIXERS == 2:
            dv = gl['small']
            arrs.append(_full_to_cols(jnp.concatenate([dv[4:8], dv[3:4], dv[0:1], dv[1:2], dv[2:3]], axis=0)))
        kinds += ['stack'] * (len(arrs) - len(kinds))
        flight, token = _send_start('exchange', arrs, kinds, f'exchange_grads_start_{part}_{i}', after=after)
        exchanges[(i, part)] = (items, kinds, flight)
        return token

    rep = {n: shards[n] for n in ('norm_mix', 'norm_mlp', 'norm_ple', 'norm_final')}
    rep['lru_w_a'], rep['lru_w_x'] = shards['lru_w_a'].astype(BF16), shards['lru_w_x'].astype(BF16)
    loss, grad_x, rgrads = _local_step(x.reshape(t, d), p.reshape(depth, t, p.shape[3]), positions.reshape(t, 1),
                                       loss_target.reshape(t, d), rep, weights_for_layer, emit_grads)

    received, res = {}, {}

    def finish_exchange(key, after):
        items, kinds, flight = exchanges[key]
        outs = _send_wait('exchange', flight, kinds, after, f'exchange_grads_wait_{key[1]}_{key[0]}')
        for (_, n, idx), o in zip(items, outs):
            received[(n, idx)] = o
        if len(outs) > len(items):
            received[('small', key[0])] = outs[-1]

    def big_adamw(names):
        for n in names:
            res[n] = _adamw_sum(shards[n], [received[(n, l)] for l in range(shards[n].shape[0])], moms[n], vels[n],
                                f'adamw_{n}')

    last = (0, 'mixer')
    for key in exchanges:
        if key != last:
            finish_exchange(key, grad_x)
    big = [n for n in WEIGHTS if SHARD_AXIS[n] is not None and shards[n].ndim == 3 and n not in ('sc_w_conv', 'lru_conv_w')]
    late = [n for n in big if n in MIXER_WEIGHTS[0]]
    big_adamw([n for n in big if n not in late])
    finish_exchange(last, jnp.full((SUB, 128), sum(r[0].reshape(-1)[0] for r in res.values()), F32))
    big_adamw(late)


    def small_adamw(layers, name):
        w_, m_, v_ = (jnp.stack([_small_block(src, i) for i in layers]) for src in (shards, moms, vels))
        return _adamw_sum(w_, [received[('small', i)] for i in layers], m_, v_, name)

    sc = small_adamw([i for i in range(depth) if i % N_MIXERS == 0], 'adamw_sc_w_conv')
    res['sc_w_conv'] = tuple(o[:, :shards['sc_w_conv'].shape[1]] for o in sc)
    lru = small_adamw([i for i in range(depth) if i % N_MIXERS == 2], 'adamw_lru_small')
    row = 0
    for n in LRU_SMALL:
        k = shards[n].size // shards[n].shape[0] // shards[n].shape[-1]
        res[n] = tuple(o[:, row:row + k].reshape(shards[n].shape) for o in lru)
        row += k

    def all_updated():
        return jnp.full((SUB, 128), sum(r[0].reshape(-1)[0] for r in res.values()), F32)

    gate_layers = sorted(gate_gathers)
    gate_parts = [_send_wait('gather', gate_gathers[i], ['stack'], all_updated(), f'gather_gate_grads_wait_{i}')[0]
                  for i in gate_layers]
    gate_w, gate_m, gate_v = (jnp.stack([gate_block(src, j) for j in range(len(gate_layers))])
                              for src in (shards, moms, vels))
    gates = _adamw_sum(gate_w, gate_parts, gate_m, gate_v, 'adamw_lru_gates')
    half = gates[0].shape[1] // 2
    res['lru_w_a'] = tuple(o[:, :half].reshape(shards['lru_w_a'].shape) for o in gates)
    res['lru_w_x'] = tuple(o[:, half:].reshape(shards['lru_w_x'].shape) for o in gates)

    norm_names = ('norm_mix', 'norm_mlp', 'norm_ple', 'norm_final')

    def norm_block(src):
        cat = jnp.concatenate([src[n].reshape(-1, d) for n in norm_names], axis=0)
        return _pad_to(cat, -(-cat.shape[0] // HALO) * HALO)

    rfull = {n: (rgrads[n] if n == 'norm_final' else jnp.stack(rgrads[n], axis=0)) for n in norm_names}
    parts_norm, = _gather_many([norm_block(rfull)], ['stack'], 'gather_norm_grads', after=all_updated())
    norms = _adamw_sum(norm_block(shards)[None], [parts_norm], norm_block(moms)[None], norm_block(vels)[None],
                       'adamw_norms')
    row = 0
    for n in norm_names:
        k = shards[n].size // d
        res[n] = tuple(o[0, row:row + k].reshape(shards[n].shape) for o in norms)
        row += k

    return (total_loss[0], grad_x.reshape(x.shape), *[res[n][0] for n in WEIGHTS], *[res[n][1] for n in WEIGHTS],
            *[res[n][2] for n in WEIGHTS], *[res[n][3] for n in WEIGHTS])
```

```python
import functools
import math

import jax
import jax.numpy as jnp
from jax import lax
from jax.experimental import pallas as pl
from jax.experimental.pallas import tpu as pltpu

F32 = jnp.float32
BF16 = jnp.bfloat16
SDS = jax.ShapeDtypeStruct

N_DEV = 8
RMS_EPS = 1e-6
N_MIXERS = 3
HEAD_DIM = 128
DILATED_PATTERNS = ((128, 1), (512, 4), (2048, 16))
ATTN_BLOCK = 128
ROPE_THETA = 500000.0
ROPE_DIM = HEAD_DIM // 4
LRU_BLOCK = 128
LRU_C = 8.0
ADAM_LR, ADAM_B1, ADAM_B2, ADAM_EPS, ADAM_WD, ADAM_STEP = 0.001, 0.9, 0.999, 1e-08, 0.01, 10

HALO = 16
SUB = 8
VMEM_LIMIT = 56 * 1024 * 1024
NEG = -1e30

SHARD_AXIS = {
    'norm_mix': None, 'norm_mlp': None, 'norm_ple': None, 'norm_final': None,
    'sc_w_in': 2, 'sc_w_conv': 2, 'sc_w_out': 1, 'attn_w_qkv': 2, 'attn_w_o': 1,
    'lru_w_in': 2, 'lru_conv_w': 2, 'lru_conv_b': 1, 'lru_w_a': None, 'lru_b_a': 1,
    'lru_w_x': None, 'lru_b_x': 1, 'lru_lambda': 1, 'lru_w_out': 1,
    'mlp_w_up': 2, 'mlp_w_down': 1, 'ple_w_gate': 1, 'ple_w_proj': 2,
}
WEIGHTS = list(SHARD_AXIS)


def _params(*sem):
    return pltpu.CompilerParams(dimension_semantics=sem or None, vmem_limit_bytes=VMEM_LIMIT)


def _row_tile(t, pref=256):
    tr = min(t, pref)
    assert t % tr == 0 and tr % HALO == 0
    return tr


def _row(tr, c, col=0):
    return pl.BlockSpec((tr, c), lambda i, col=col: (i, col))


def _full(shape):
    return pl.BlockSpec(shape, lambda *_: (0,) * len(shape))


def _sigmoid(x):
    return 1.0 / (1.0 + jnp.exp(-x))


def _expm1(x):
    taylor = x * (1.0 + x * (0.5 + x * (1.0 / 6.0 + x * (1.0 / 24.0 + x * (1.0 / 120.0)))))
    return jnp.where(jnp.abs(x) < 0.1, taylor, jnp.exp(x) - 1.0)


def _softplus(x):
    z = jnp.exp(-jnp.abs(x))
    log1p = jnp.where(z < 0.01, z * (1.0 - z * (0.5 - z * (1.0 / 3.0 - z * 0.25))), jnp.log(1.0 + z))
    return jnp.maximum(x, 0.0) + log1p


_GELU_K = math.sqrt(2.0 / math.pi)


def _gelu_and_grad(x):
    inner = _GELU_K * (x + 0.044715 * x * x * x)
    th = jnp.tanh(inner)
    g = 0.5 * x * (1.0 + th)
    dg = 0.5 * (1.0 + th) + 0.5 * x * (1.0 - th * th) * _GELU_K * (1.0 + 3.0 * 0.044715 * x * x)
    return g, dg


def _shift_down(x, k, prev):
    row = lax.broadcasted_iota(jnp.int32, (SUB, x.shape[1]), 0)
    xr = pltpu.roll(x, k, 0)
    top = jnp.where(row < k, pltpu.roll(prev, k, 0), xr[0:SUB])
    return jnp.concatenate([top, xr[SUB:]], axis=0)


def _shift_up(x, k, nxt):
    r = x.shape[0]
    row = lax.broadcasted_iota(jnp.int32, (SUB, x.shape[1]), 0)
    xr = pltpu.roll(x, r - k, 0)
    bot = jnp.where(row >= SUB - k, pltpu.roll(nxt, SUB - k, 0), xr[r - SUB:r])
    return jnp.concatenate([xr[:r - SUB], bot], axis=0)


_DIMS = {'nn': (((1,), (0,)), ((), ())), 'nt': (((1,), (1,)), ((), ())), 'tn': (((0,), (0,)), ((), ()))}


MM_VMEM_BUDGET = 50 * 1024 * 1024
MM_MIN_TK = 1024
MM_MIN_TM = 1024


def _tile_options(dim):
    return [c for c in range(dim, 127, -128) if dim % c == 0] or [dim]


def _choose_tiles(m, n, k, n_span, k_span, a_size, b_size, mn_size, a_temp):
    best = None
    for tm in _tile_options(m):
        for tn in _tile_options(n_span):
            for tk in _tile_options(k_span):
                nk = k // tk
                need = (2 * (tm * tk * a_size + tk * tn * b_size + tm * tn * mn_size) + tm * tn * 4 * (1 + (nk > 1))
                        + tm * tk * 4 * a_temp)
                score = (-min(tk, MM_MIN_TK), -min(tm, MM_MIN_TM), -tm * tn, nk, -min(tm, 2 * MM_MIN_TM), -tn)
                if need <= MM_VMEM_BUDGET and (best is None or score < best[0]):
                    best = (score, (tm, tn, tk))
    return best[1]


def _mm(a, b, dims, name, out_dtypes=(F32,), a_pro=None, extras=(), epi=None, out_stacked=False, dep=None):
    deps = [] if dep is None else [dep]
    stacked = b.ndim == 3
    b_rows, b_cols = (b.shape[1], N_DEV * b.shape[2]) if stacked else b.shape
    if dims == 'nn':
        (m, k), (k2, n) = a.shape, (b_rows, b_cols)
    elif dims == 'nt':
        (m, k), (n, k2) = a.shape, (b_rows, b_cols)
    else:
        (k, m), (k2, n) = a.shape, (b_rows, b_cols)
    assert k == k2, (name, a.shape, b.shape)
    assert not (stacked and dims == 'tn') and not (out_stacked and (extras or dims != 'tn'))
    tm, tn, tk = _choose_tiles(
        m, n, k, n // N_DEV if (out_stacked or (stacked and dims == 'nn')) else n,
        k // N_DEV if (stacked and dims == 'nt') else k, a.dtype.itemsize, b.dtype.itemsize,
        sum(e.dtype.itemsize for e in extras) + sum(jnp.dtype(dt).itemsize for dt in out_dtypes),
        a_pro is not None or a.dtype != BF16)
    assert m % tm == 0 and n % tn == 0 and k % tk == 0, (name, m, n, k)
    nk = k // tk
    a_spec = pl.BlockSpec((tk, tm), lambda i, j, kk: (kk, i)) if dims == 'tn' else pl.BlockSpec((tm, tk), lambda i, j, kk: (i, kk))
    if not stacked:
        b_spec = pl.BlockSpec((tn, tk), lambda i, j, kk: (j, kk)) if dims == 'nt' else pl.BlockSpec((tk, tn), lambda i, j, kk: (kk, j))
    elif dims == 'nn':
        per = b.shape[2] // tn
        b_spec = pl.BlockSpec((None, tk, tn), lambda i, j, kk: (j // per, kk, j % per))
    else:
        per = b.shape[2] // tk
        b_spec = pl.BlockSpec((None, tn, tk), lambda i, j, kk: (kk // per, j, kk % per))
    if out_stacked:
        per_o = n // N_DEV // tn
        o_spec = pl.BlockSpec((None, tm, tn), lambda i, j, kk: (j // per_o, i, j % per_o))
        o_shape = (N_DEV, m, n // N_DEV)
    else:
        o_spec = pl.BlockSpec((tm, tn), lambda i, j, kk: (i, j))
        o_shape = (m, n)
    n_ex, n_out = len(extras), len(out_dtypes)
    for e in extras:
        assert e.shape == (m, n), (name, e.shape)

    def body(a_ref, b_ref, *rest):
        rest = rest[len(deps):]
        ex_refs, out_refs = rest[:n_ex], rest[n_ex:n_ex + n_out]
        kk = pl.program_id(2)
        av = a_ref[...]
        if a_pro is not None:
            av = a_pro(av.astype(F32))
        part = lax.dot_general(av.astype(BF16), b_ref[...].astype(BF16), _DIMS[dims], preferred_element_type=F32)

        def finish(res):
            outs = (res,) if epi is None else epi(res, *[e[...] for e in ex_refs])
            for o_ref, o in zip(out_refs, outs):
                o_ref[...] = o.astype(o_ref.dtype)

        if nk == 1:
            finish(part)
        else:
            acc = rest[-1]

            @pl.when(kk == 0)
            def _():
                acc[...] = part

            @pl.when(kk > 0)
            def _():
                acc[...] += part

            @pl.when(kk == nk - 1)
            def _():
                finish(acc[...])

    out = pl.pallas_call(
        body, grid=(m // tm, n // tn, nk),
        in_specs=[a_spec, b_spec] + [_ANY] * len(deps) + [o_spec] * n_ex,
        out_specs=[o_spec] * n_out,
        out_shape=[SDS(o_shape, d) for d in out_dtypes],
        scratch_shapes=[] if nk == 1 else [pltpu.VMEM((tm, tn), F32)],
        compiler_params=_params('parallel', 'parallel', 'arbitrary'), name=name)(a, b, *deps, *extras)
    return out[0] if n_out == 1 else out


def _relu2(u):
    r = jnp.maximum(u, 0.0)
    return r * r


STREAM_ROWS = 512


def _rms_fwd(h, g, name):
    t, d = h.shape
    tr = _row_tile(t, STREAM_ROWS)

    def body(h_ref, g_ref, o_ref):
        x = h_ref[...]
        r = lax.rsqrt(jnp.mean(x * x, axis=-1, keepdims=True) + RMS_EPS)
        o_ref[...] = (x * r * g_ref[...]).astype(o_ref.dtype)

    return pl.pallas_call(body, grid=(t // tr,), in_specs=[_row(tr, d), _full((1, d))], out_specs=_row(tr, d),
                          out_shape=SDS((t, d), BF16), compiler_params=_params('parallel'), name=name)(h, g.reshape(1, d))


def _rms_bwd(h, g, dhn, dres, name):
    t, d = h.shape
    tr = _row_tile(t, STREAM_ROWS)

    def body(h_ref, g_ref, dhn_ref, dres_ref, dh_ref, dg_ref):
        @pl.when(pl.program_id(0) == 0)
        def _():
            dg_ref[...] = jnp.zeros_like(dg_ref)

        x = h_ref[...]
        r = lax.rsqrt(jnp.mean(x * x, axis=-1, keepdims=True) + RMS_EPS)
        dy = dhn_ref[...].astype(F32)
        gy = dy * g_ref[...]
        dx = r * gy - x * (r * r * r) * jnp.mean(gy * x, axis=-1, keepdims=True)
        dh_ref[...] = dres_ref[...] + dx
        dg_ref[...] += jnp.sum(dy * (x * r), axis=0, keepdims=True)

    return pl.pallas_call(body, grid=(t // tr,),
                          in_specs=[_row(tr, d), _full((1, d)), _row(tr, d), _row(tr, d)],
                          out_specs=[_row(tr, d), _full((1, d))],
                          out_shape=[SDS((t, d), F32), SDS((1, d), F32)],
                          compiler_params=_params('arbitrary'), name=name)(h, g.reshape(1, d), dhn, dres)


def _head(h, g, target, name):
    t, d = h.shape
    tr = _row_tile(t, STREAM_ROWS)

    def body(h_ref, g_ref, t_ref, dh_ref, loss_ref, dg_ref):
        @pl.when(pl.program_id(0) == 0)
        def _():
            dg_ref[...] = jnp.zeros_like(dg_ref)
            loss_ref[...] = jnp.zeros_like(loss_ref)

        x = h_ref[...]
        gv = g_ref[...]
        r = lax.rsqrt(jnp.mean(x * x, axis=-1, keepdims=True) + RMS_EPS)
        xh = x * r
        e = xh * gv - t_ref[...]
        per_tok = jnp.mean(e * e, axis=-1, keepdims=True)
        loss_ref[...] += jnp.broadcast_to(0.5 * jnp.sum(per_tok, axis=0, keepdims=True), loss_ref.shape)
        dy = e * (1.0 / d)
        gy = dy * gv
        dh_ref[...] = r * gy - x * (r * r * r) * jnp.mean(gy * x, axis=-1, keepdims=True)
        dg_ref[...] += jnp.sum(dy * xh, axis=0, keepdims=True)

    return pl.pallas_call(body, grid=(t // tr,),
                          in_specs=[_row(tr, d), _full((1, d)), _row(tr, d)],
                          out_specs=[_row(tr, d), _full((1, 128)), _full((1, d))],
                          out_shape=[SDS((t, d), F32), SDS((1, 128), F32), SDS((1, d), F32)],
                          compiler_params=_params('arbitrary'), name=name)(h, g.reshape(1, d), target)


def _ple_bwd_gate(dh3, gate, pp, name):
    t, d = dh3.shape
    tr = _row_tile(t, STREAM_ROWS)

    def body(dh_ref, g_ref, pp_ref, dpp_ref, dgl_ref):
        dh = dh_ref[...]
        gt = g_ref[...].astype(F32)
        dpp_ref[...] = (dh * gt).astype(dpp_ref.dtype)
        dgl_ref[...] = (dh * pp_ref[...].astype(F32) * gt * (1.0 - gt)).astype(dgl_ref.dtype)

    return pl.pallas_call(body, grid=(t // tr,), in_specs=[_row(tr, d)] * 3, out_specs=[_row(tr, d)] * 2,
                          out_shape=[SDS((t, d), BF16), SDS((t, d), BF16)],
                          compiler_params=_params('parallel'), name=name)(dh3, gate, pp)


def _halo_prev(tr, c, col=0):
    return pl.BlockSpec((HALO, c), lambda i, col=col: (jnp.maximum(i * (tr // HALO) - 1, 0), col))


def _halo_next(tr, c, t, col=0):
    return pl.BlockSpec((HALO, c), lambda i, col=col: (jnp.minimum((i + 1) * (tr // HALO), t // HALO - 1), col))


def _sc_fwd(z, w, name):
    t, c3 = z.shape
    c = c3 // 3
    tr = _row_tile(t)

    def body(z_ref, zp_ref, w_ref, y_ref):
        i = pl.program_id(0)
        zz = z_ref[...]
        gb, cx = zz[:, :c], zz[:, c:2 * c] * zz[:, 2 * c:]
        zp = zp_ref[SUB:HALO, :]
        cxp = jnp.where(i > 0, zp[:, c:2 * c] * zp[:, 2 * c:], 0.0)
        wv = w_ref[...]
        conv = wv[2:3] * cx + wv[1:2] * _shift_down(cx, 1, cxp) + wv[0:1] * _shift_down(cx, 2, cxp)
        y_ref[...] = (gb * conv).astype(y_ref.dtype)

    return pl.pallas_call(body, grid=(t // tr,),
                          in_specs=[_row(tr, c3), _halo_prev(tr, c3), _full((3, c))],
                          out_specs=_row(tr, c), out_shape=SDS((t, c), BF16),
                          compiler_params=_params('parallel'), name=name)(z, z, w)


def _sc_bwd(dy, z, w, name):
    t, c3 = z.shape
    c = c3 // 3
    tr = _row_tile(t)
    nt = t // tr

    def body(dy_ref, dyn_ref, z_ref, zp_ref, zn_ref, w_ref, dz_ref, dw_ref):
        i = pl.program_id(0)

        @pl.when(i == 0)
        def _():
            dw_ref[...] = jnp.zeros_like(dw_ref)

        zz = z_ref[...]
        gb, gc, xi = zz[:, :c], zz[:, c:2 * c], zz[:, 2 * c:]
        cx = gc * xi
        zp = zp_ref[SUB:HALO, :]
        cxp = jnp.where(i > 0, zp[:, c:2 * c] * zp[:, 2 * c:], 0.0)
        wv = w_ref[...]
        cx1, cx2 = _shift_down(cx, 1, cxp), _shift_down(cx, 2, cxp)
        conv = wv[2:3] * cx + wv[1:2] * cx1 + wv[0:1] * cx2
        dyv = dy_ref[...]
        dconv = dyv * gb
        dcn = jnp.where(i < nt - 1, dyn_ref[0:SUB, :] * zn_ref[0:SUB, :c], 0.0)
        dcx = wv[2:3] * dconv + wv[1:2] * _shift_up(dconv, 1, dcn) + wv[0:1] * _shift_up(dconv, 2, dcn)
        dz_ref[:, :c] = (dyv * conv).astype(dz_ref.dtype)
        dz_ref[:, c:2 * c] = (dcx * xi).astype(dz_ref.dtype)
        dz_ref[:, 2 * c:] = (dcx * gc).astype(dz_ref.dtype)
        dw_ref[...] += jnp.concatenate([jnp.sum(dconv * cx2, axis=0, keepdims=True),
                                        jnp.sum(dconv * cx1, axis=0, keepdims=True),
                                        jnp.sum(dconv * cx, axis=0, keepdims=True)], axis=0)

    return pl.pallas_call(body, grid=(nt,),
                          in_specs=[_row(tr, c), _halo_next(tr, c, t), _row(tr, c3), _halo_prev(tr, c3),
                                    _halo_next(tr, c3, t), _full((3, c))],
                          out_specs=[_row(tr, c3), _full((3, c))],
                          out_shape=[SDS((t, c3), BF16), SDS((3, c), F32)],
                          compiler_params=_params('arbitrary'), name=name)(dy, dy, z, z, z, w)


def _perm(tr, dil, inverse=False):
    n = tr // dil
    a = lax.broadcasted_iota(jnp.int32, (tr, tr), 1 if inverse else 0)
    b = lax.broadcasted_iota(jnp.int32, (tr, tr), 0 if inverse else 1)
    return (b == (a % n) * dil + a // n).astype(BF16)


def _permute(pm, x, terms):
    if x.dtype == BF16:
        return jnp.dot(pm, x, preferred_element_type=F32)
    acc = None
    for _ in range(terms):
        part = x.astype(BF16)
        y = jnp.dot(pm, part, preferred_element_type=F32)
        acc = y if acc is None else acc + y
        x = x - part.astype(F32)
    return acc


def _store_dilated(o_ref, y, dil, d):
    n = y.shape[0] // dil
    for rho in range(dil):
        o_ref[:, rho * d:(rho + 1) * d] = y[rho * n:(rho + 1) * n].astype(o_ref.dtype)


def _load_dilated(ref, dil, d):
    return jnp.concatenate([ref[:, rho * d:(rho + 1) * d] for rho in range(dil)], axis=0) if dil > 1 else ref[...]


def _rope_heads(x, lane, cos, sin):
    return jnp.concatenate([_rope_apply(x[:, s:s + HEAD_DIM], lane, cos, sin)
                            for s in range(0, x.shape[1], HEAD_DIM)], axis=1)


def _rope_tables(pos, invf, sign):
    lane = lax.broadcasted_iota(jnp.int32, (pos.shape[0], HEAD_DIM), 1)
    ang = pos.astype(F32) * invf
    half = ROPE_DIM // 2
    cos = jnp.where(lane < ROPE_DIM, jnp.cos(ang), 1.0)
    sin = jnp.sin(ang) * sign
    sin = jnp.where(lane < half, -sin, jnp.where(lane < ROPE_DIM, sin, 0.0))
    return lane, cos, sin


def _rope_apply(x, lane, cos, sin):
    half = ROPE_DIM // 2
    xs = jnp.where(lane < half, pltpu.roll(x, HEAD_DIM - half, 1), pltpu.roll(x, half, 1))
    return x * cos + xs * sin


def _dilated_spec(tr, dil, d):
    return pl.BlockSpec((tr // dil, dil * d), lambda i: (i, 0))


def _rope_fwd(qkv, pos, invf, dils, name):
    t, w3 = qkv.shape
    w, ng = w3 // 3, len(dils)
    d = w // ng
    tr = _row_tile(t)

    def body(q_ref, k_ref, v_ref, pos_ref, invf_ref, *out_refs):
        lane, cos, sin = _rope_tables(pos_ref[...], invf_ref[...], 1.0)
        for g, dil in enumerate(dils):
            cs = slice(g * d, (g + 1) * d)
            vals = [_rope_heads(q_ref[:, cs], lane, cos, sin).astype(BF16),
                    _rope_heads(k_ref[:, cs], lane, cos, sin).astype(BF16), v_ref[:, cs].astype(BF16)]
            if dil > 1:
                pm = _perm(tr, dil)
                vals = [_permute(pm, a, 1) for a in vals]
            for o_ref, a in zip(out_refs[g::ng], vals):
                _store_dilated(o_ref, a, dil, d)

    outs = pl.pallas_call(body, grid=(t // tr,),
                          in_specs=[_row(tr, w, 0), _row(tr, w, 1), _row(tr, w, 2), _row(tr, 1), _full((1, HEAD_DIM))],
                          out_specs=[_dilated_spec(tr, dil, d) for dil in dils] * 3,
                          out_shape=[SDS((t // dil, dil * d), BF16) for dil in dils] * 3,
                          compiler_params=_params('parallel'), name=name)(qkv, qkv, qkv, pos, invf)
    return outs[:ng], outs[ng:2 * ng], outs[2 * ng:]


def _rope_bwd(dqs, dks, dvs, pos, invf, dils, name):
    ng = len(dils)
    t = dqs[0].shape[0] * dils[0]
    d = dqs[0].shape[1] // dils[0]
    w = ng * d
    tr = _row_tile(t)

    def body(*refs):
        dq_refs, dk_refs, dv_refs = refs[:ng], refs[ng:2 * ng], refs[2 * ng:3 * ng]
        pos_ref, invf_ref, o_ref = refs[3 * ng:]
        pos_f = jnp.broadcast_to(pos_ref[...].astype(F32), (tr, HEAD_DIM))
        for g, dil in enumerate(dils):
            pos_g = pos_f if dil == 1 else _permute(_perm(tr, dil), pos_f, 3)
            lane, cos, sin = _rope_tables(pos_g, invf_ref[...], -1.0)
            vals = [_rope_heads(_load_dilated(dq_refs[g], dil, d), lane, cos, sin),
                    _rope_heads(_load_dilated(dk_refs[g], dil, d), lane, cos, sin), _load_dilated(dv_refs[g], dil, d)]
            back = _perm(tr, dil, inverse=True) if dil > 1 else None
            for sec, a in enumerate(vals):
                a = a.astype(BF16)
                if dil > 1:
                    a = _permute(back, a, 1)
                o_ref[:, sec * w + g * d:sec * w + (g + 1) * d] = a.astype(o_ref.dtype)

    return pl.pallas_call(body, grid=(t // tr,),
                          in_specs=[_dilated_spec(tr, dil, d) for dil in dils] * 3 + [_row(tr, 1), _full((1, HEAD_DIM))],
                          out_specs=_row(tr, 3 * w), out_shape=SDS((t, 3 * w), BF16),
                          compiler_params=_params('parallel'), name=name)(*dqs, *dks, *dvs, pos, invf)


def _dilate_many(arrs, dil, terms, out_dtypes, name):
    t, d = arrs[0].shape
    tr = _row_tile(t)
    na = len(arrs)

    def body(*refs):
        pm = _perm(tr, dil)
        for a_ref, o_ref, k in zip(refs[:na], refs[na:], terms):
            _store_dilated(o_ref, _permute(pm, a_ref[...], k), dil, d)

    return pl.pallas_call(body, grid=(t // tr,), in_specs=[_row(tr, d)] * na,
                          out_specs=[_dilated_spec(tr, dil, d)] * na,
                          out_shape=[SDS((t // dil, dil * d), dt) for dt in out_dtypes],
                          compiler_params=_params('parallel'), name=name)(*arrs)


def _attn_masks():
    qi = lax.broadcasted_iota(jnp.int32, (ATTN_BLOCK, ATTN_BLOCK), 0)
    kj = lax.broadcasted_iota(jnp.int32, (ATTN_BLOCK, ATTN_BLOCK), 1)
    return kj >= qi, kj <= qi


def _attn_cols(l, width):
    ncol = width // HEAD_DIM
    cpb = max(1, min(ncol, 32 // (l // ATTN_BLOCK)))
    assert ncol % cpb == 0
    return cpb


def _attn_fwd(q, k, v, name):
    l, width = q.shape
    cpb = _attn_cols(l, width)
    nb = l // ATTN_BLOCK
    scale = HEAD_DIM ** -0.5

    def body(q_ref, k_ref, v_ref, o_ref, lse_ref):
        m_prev, m_cur = _attn_masks()
        for col in range(cpb):
            cs = slice(col * HEAD_DIM, (col + 1) * HEAD_DIM)

            def step(b, carry, cs=cs):
                r0, rp = b * ATTN_BLOCK, max(b - 1, 0) * ATTN_BLOCK
                qb = q_ref[pl.ds(r0, ATTN_BLOCK), cs]
                s_p = lax.dot_general(qb, k_ref[pl.ds(rp, ATTN_BLOCK), cs], _DIMS['nt'], preferred_element_type=F32) * scale
                s_c = lax.dot_general(qb, k_ref[pl.ds(r0, ATTN_BLOCK), cs], _DIMS['nt'], preferred_element_type=F32) * scale
                s_p = jnp.where(jnp.logical_and(m_prev, b > 0), s_p, NEG)
                s_c = jnp.where(m_cur, s_c, NEG)
                m = jnp.maximum(jnp.max(s_p, axis=-1, keepdims=True), jnp.max(s_c, axis=-1, keepdims=True))
                p_p, p_c = jnp.exp(s_p - m), jnp.exp(s_c - m)
                den = jnp.sum(p_p, axis=-1, keepdims=True) + jnp.sum(p_c, axis=-1, keepdims=True)
                acc = jnp.dot(p_p.astype(BF16), v_ref[pl.ds(rp, ATTN_BLOCK), cs], preferred_element_type=F32)
                acc += jnp.dot(p_c.astype(BF16), v_ref[pl.ds(r0, ATTN_BLOCK), cs], preferred_element_type=F32)
                o_ref[pl.ds(r0, ATTN_BLOCK), cs] = acc / den
                lse_ref[pl.ds(r0, ATTN_BLOCK), cs] = jnp.broadcast_to(m + jnp.log(den), (ATTN_BLOCK, HEAD_DIM))
                return carry

            for b in range(nb):
                step(b, 0)

    spec = pl.BlockSpec((l, cpb * HEAD_DIM), lambda j: (0, j))
    return pl.pallas_call(body, grid=(width // (cpb * HEAD_DIM),), in_specs=[spec] * 3, out_specs=[spec] * 2,
                          out_shape=[SDS((l, width), F32)] * 2,
                          compiler_params=_params('parallel'), name=name)(q, k, v)


def _attn_bwd(q, k, v, do, lse, delta, name):
    l, width = q.shape
    cpb = _attn_cols(l, width)
    nb = l // ATTN_BLOCK
    scale = HEAD_DIM ** -0.5

    def body(q_ref, k_ref, v_ref, do_ref, lse_ref, dl_ref, dq_ref, dk_ref, dv_ref):
        m_prev, m_cur = _attn_masks()
        dk_ref[...] = jnp.zeros_like(dk_ref)
        dv_ref[...] = jnp.zeros_like(dv_ref)
        for col in range(cpb):
            cs = slice(col * HEAD_DIM, (col + 1) * HEAD_DIM)

            def step(b, carry, cs=cs):
                r0, rp = b * ATTN_BLOCK, max(b - 1, 0) * ATTN_BLOCK
                qb, dob = q_ref[pl.ds(r0, ATTN_BLOCK), cs], do_ref[pl.ds(r0, ATTN_BLOCK), cs].astype(BF16)
                kp, kc = k_ref[pl.ds(rp, ATTN_BLOCK), cs], k_ref[pl.ds(r0, ATTN_BLOCK), cs]
                vp, vc = v_ref[pl.ds(rp, ATTN_BLOCK), cs], v_ref[pl.ds(r0, ATTN_BLOCK), cs]
                lse_b = lse_ref[pl.ds(r0, ATTN_BLOCK), cs]
                dl_b = dl_ref[pl.ds(r0, ATTN_BLOCK), cs]
                s_p = lax.dot_general(qb, kp, _DIMS['nt'], preferred_element_type=F32) * scale
                s_c = lax.dot_general(qb, kc, _DIMS['nt'], preferred_element_type=F32) * scale
                p_p = jnp.exp(jnp.where(jnp.logical_and(m_prev, b > 0), s_p, NEG) - lse_b)
                p_c = jnp.exp(jnp.where(m_cur, s_c, NEG) - lse_b)
                dp_p = lax.dot_general(dob, vp, _DIMS['nt'], preferred_element_type=F32)
                dp_c = lax.dot_general(dob, vc, _DIMS['nt'], preferred_element_type=F32)
                ds_p = (p_p * (dp_p - dl_b) * scale).astype(BF16)
                ds_c = (p_c * (dp_c - dl_b) * scale).astype(BF16)
                dq_ref[pl.ds(r0, ATTN_BLOCK), cs] = (jnp.dot(ds_p, kp, preferred_element_type=F32)
                                                     + jnp.dot(ds_c, kc, preferred_element_type=F32))
                dk_ref[pl.ds(rp, ATTN_BLOCK), cs] += lax.dot_general(ds_p, qb, _DIMS['tn'], preferred_element_type=F32)
                dk_ref[pl.ds(r0, ATTN_BLOCK), cs] += lax.dot_general(ds_c, qb, _DIMS['tn'], preferred_element_type=F32)
                dv_ref[pl.ds(rp, ATTN_BLOCK), cs] += lax.dot_general(p_p.astype(BF16), dob, _DIMS['tn'], preferred_element_type=F32)
                dv_ref[pl.ds(r0, ATTN_BLOCK), cs] += lax.dot_general(p_c.astype(BF16), dob, _DIMS['tn'], preferred_element_type=F32)
                return carry

            for b in range(nb):
                step(b, 0)

    spec = pl.BlockSpec((l, cpb * HEAD_DIM), lambda j: (0, j))
    return pl.pallas_call(body, grid=(width // (cpb * HEAD_DIM),), in_specs=[spec] * 6, out_specs=[spec] * 3,
                          out_shape=[SDS((l, width), F32)] * 3,
                          compiler_params=_params('parallel'), name=name)(q, k, v, do, lse, delta)


def _attn_combine(os_, lses, dils, name):
    ng = len(dils)
    t = os_[0].shape[0] * dils[0]
    d = os_[0].shape[1] // dils[0]
    tr = _row_tile(t)

    def body(*refs):
        o_refs, l_refs, o_out, lse_out = refs[:ng], refs[ng:2 * ng], refs[2 * ng], refs[2 * ng + 1]
        ovs, ls = [], []
        for g, dil in enumerate(dils):
            ov, lv = _load_dilated(o_refs[g], dil, d), _load_dilated(l_refs[g], dil, d)
            if dil > 1:
                back = _perm(tr, dil, inverse=True)
                ov, lv = _permute(back, ov, 2), _permute(back, lv, 3)
            ovs.append(ov)
            ls.append(lv)
        m = functools.reduce(jnp.maximum, ls)
        ws = [jnp.exp(x - m) for x in ls]
        den = functools.reduce(lambda a, b: a + b, ws)
        acc = functools.reduce(lambda a, b: a + b, [w * o for w, o in zip(ws, ovs)])
        o_out[...] = (acc / den).astype(o_out.dtype)
        lse_out[...] = m + jnp.log(den)

    return pl.pallas_call(body, grid=(t // tr,), in_specs=[_dilated_spec(tr, dil, d) for dil in dils] * 2,
                          out_specs=[_row(tr, d)] * 2, out_shape=[SDS((t, d), BF16), SDS((t, d), F32)],
                          compiler_params=_params('parallel'), name=name)(*os_, *lses)


def _delta_epilogue(acc, o):
    prod = acc * o.astype(F32)
    segs = [jnp.broadcast_to(jnp.sum(prod[:, s:s + HEAD_DIM], axis=-1, keepdims=True), (acc.shape[0], HEAD_DIM))
            for s in range(0, acc.shape[1], HEAD_DIM)]
    return acc, jnp.concatenate(segs, axis=-1)


LRU_TILE = 128


def _lru_gates(xr, wa_ref, ba, wx_ref, bx, lam):
    nb = wa_ref.shape[0]
    xb = xr.astype(BF16)
    ra = jnp.concatenate([jnp.dot(xb[:, n * LRU_BLOCK:(n + 1) * LRU_BLOCK], wa_ref[n], preferred_element_type=F32)
                          for n in range(nb)], axis=-1) + ba
    ia = jnp.concatenate([jnp.dot(xb[:, n * LRU_BLOCK:(n + 1) * LRU_BLOCK], wx_ref[n], preferred_element_type=F32)
                          for n in range(nb)], axis=-1) + bx
    r, ig = _sigmoid(ra), _sigmoid(ia)
    sp = _softplus(-lam)
    log_a = -LRU_C * r * sp
    a = jnp.exp(log_a)
    mult = jnp.sqrt(-_expm1(2.0 * log_a))
    return xb, r, ig, sp, a, mult


def _lru_fwd(z, cw, cb, wa, ba, wx, bx, lam, name):
    t, c2 = z.shape
    c = c2 // 2
    nb = c // LRU_BLOCK
    tr = _row_tile(t, LRU_TILE)

    def body(g_ref, x_ref, xp_ref, cw_ref, cb_ref, wa_ref, ba_ref, wx_ref, bx_ref, lam_ref,
             y_ref, hs_ref, xr_ref, car_ref):
        i = pl.program_id(0)

        @pl.when(i == 0)
        def _():
            car_ref[...] = jnp.zeros_like(car_ref)

        x0 = x_ref[...]
        xp = jnp.where(i > 0, xp_ref[SUB:HALO, :], 0.0)
        cwv = cw_ref[...]
        xr = (cb_ref[...] + cwv[3:4] * x0 + cwv[2:3] * _shift_down(x0, 1, xp)
              + cwv[1:2] * _shift_down(x0, 2, xp) + cwv[0:1] * _shift_down(x0, 3, xp))
        xr_ref[...] = xr
        _, _, ig, _, a, mult = _lru_gates(xr, wa_ref, ba_ref[...], wx_ref, bx_ref[...], lam_ref[...])
        u = mult * (ig * xr)
        row = lax.broadcasted_iota(jnp.int32, (SUB, c), 0)
        car = car_ref[...]
        for j in range(tr // SUB):
            ab, ub = a[j * SUB:(j + 1) * SUB], u[j * SUB:(j + 1) * SUB]
            for s in (1, 2, 4):
                a_sh = jnp.where(row >= s, pltpu.roll(ab, s, 0), 1.0)
                u_sh = jnp.where(row >= s, pltpu.roll(ub, s, 0), 0.0)
                ub = ab * u_sh + ub
                ab = ab * a_sh
            hb = ub + ab * car
            hs_ref[j * SUB:(j + 1) * SUB, :] = hb
            car = jnp.broadcast_to(hb[SUB - 1:SUB], (SUB, c))
        car_ref[...] = car
        gl, _ = _gelu_and_grad(g_ref[...])
        y_ref[...] = (hs_ref[...] * gl).astype(y_ref.dtype)

    return pl.pallas_call(
        body, grid=(t // tr,),
        in_specs=[_row(tr, c, 0), _row(tr, c, 1), _halo_prev(tr, c, 1), _full((4, c)), _full((1, c)),
                  _full((nb, LRU_BLOCK, LRU_BLOCK)), _full((1, c)), _full((nb, LRU_BLOCK, LRU_BLOCK)), _full((1, c)), _full((1, c))],
        out_specs=[_row(tr, c)] * 3,
        out_shape=[SDS((t, c), BF16), SDS((t, c), F32), SDS((t, c), F32)],
        scratch_shapes=[pltpu.VMEM((SUB, c), F32)],
        compiler_params=_params('arbitrary'), name=name)(
            z, z, z, cw, cb.reshape(1, c), wa, ba.reshape(1, c), wx, bx.reshape(1, c), lam.reshape(1, c))


def _lru_bwd(dy, z, xr, hs, cw, wa, ba, wx, bx, lam, name):
    t, c2 = z.shape
    c = c2 // 2
    nb = c // LRU_BLOCK
    tr = _row_tile(t, LRU_TILE)
    nt = t // tr

    def rev(col=0):
        return pl.BlockSpec((tr, c), lambda i, col=col: (nt - 1 - i, col))

    def rev_prev(col=0):
        return pl.BlockSpec((HALO, c), lambda i, col=col: (jnp.maximum((nt - 1 - i) * (tr // HALO) - 1, 0), col))

    def body(dy_ref, g_ref, x_ref, xp_ref, xr_ref, hs_ref, hp_ref, cw_ref, wa_ref, ba_ref, wx_ref, bx_ref, lam_ref,
             dz_ref, dwa_ref, dwx_ref, dvec_ref, lcar_ref, ahead_ref, dxhead_ref, lam_s):
        i = pl.program_id(0)
        first_tile = i == nt - 1

        @pl.when(i == 0)
        def _():
            lcar_ref[...] = jnp.zeros_like(lcar_ref)
            ahead_ref[...] = jnp.zeros_like(ahead_ref)
            dxhead_ref[...] = jnp.zeros_like(dxhead_ref)
            dwa_ref[...] = jnp.zeros_like(dwa_ref)
            dwx_ref[...] = jnp.zeros_like(dwx_ref)
            dvec_ref[...] = jnp.zeros_like(dvec_ref)

        xrv = xr_ref[...]
        lamv = lam_ref[...]
        xb, r, ig, sp, a, mult = _lru_gates(xrv, wa_ref, ba_ref[...], wx_ref, bx_ref[...], lamv)
        hsv = hs_ref[...]
        dyv = dy_ref[...]
        gl, dgl = _gelu_and_grad(g_ref[...])
        dhs = dyv * gl
        dz_ref[:, :c] = (dyv * hsv * dgl).astype(dz_ref.dtype)

        a_next = _shift_up(a, 1, ahead_ref[...])
        row = lax.broadcasted_iota(jnp.int32, (SUB, c), 0)
        car = lcar_ref[...]
        for j in reversed(range(tr // SUB)):
            ab, ub = a_next[j * SUB:(j + 1) * SUB], dhs[j * SUB:(j + 1) * SUB]
            for s in (1, 2, 4):
                a_sh = jnp.where(row < SUB - s, pltpu.roll(ab, SUB - s, 0), 1.0)
                u_sh = jnp.where(row < SUB - s, pltpu.roll(ub, SUB - s, 0), 0.0)
                ub = ab * u_sh + ub
                ab = ab * a_sh
            lb = ub + ab * car
            lam_s[j * SUB:(j + 1) * SUB, :] = lb
            car = jnp.broadcast_to(lb[0:1], (SUB, c))
        lcar_ref[...] = car
        ahead_ref[...] = a[0:SUB]
        lmb = lam_s[...]

        hp = jnp.where(first_tile, 0.0, hp_ref[SUB:HALO, :])
        h_prev = _shift_down(hsv, 1, hp)
        d_a = lmb * h_prev
        d_mult = lmb * (ig * xrv)
        d_ixr = lmb * mult
        d_ig = d_ixr * xrv
        dxr = d_ixr * ig
        d_la = d_a * a - d_mult * (a * a) / mult
        d_r = d_la * (-LRU_C * sp)
        d_sp = jnp.sum(d_la * (-LRU_C * r), axis=0, keepdims=True)
        d_ra = d_r * r * (1.0 - r)
        d_ia = d_ig * ig * (1.0 - ig)
        d_rab, d_iab = d_ra.astype(BF16), d_ia.astype(BF16)
        parts = []
        for n in range(nb):
            cs = slice(n * LRU_BLOCK, (n + 1) * LRU_BLOCK)
            parts.append(lax.dot_general(d_rab[:, cs], wa_ref[n], _DIMS['nt'], preferred_element_type=F32)
                         + lax.dot_general(d_iab[:, cs], wx_ref[n], _DIMS['nt'], preferred_element_type=F32))
            dwa_ref[n] += lax.dot_general(xb[:, cs], d_rab[:, cs], _DIMS['tn'], preferred_element_type=F32)
            dwx_ref[n] += lax.dot_general(xb[:, cs], d_iab[:, cs], _DIMS['tn'], preferred_element_type=F32)
        dxr = dxr + jnp.concatenate(parts, axis=-1)

        cwv = cw_ref[...]
        nxt = dxhead_ref[...]
        dx0 = (cwv[3:4] * dxr + cwv[2:3] * _shift_up(dxr, 1, nxt) + cwv[1:2] * _shift_up(dxr, 2, nxt)
               + cwv[0:1] * _shift_up(dxr, 3, nxt))
        dxhead_ref[...] = dxr[0:SUB]
        dz_ref[:, c:] = dx0.astype(dz_ref.dtype)

        x0 = x_ref[...]
        xp = jnp.where(first_tile, 0.0, xp_ref[SUB:HALO, :])
        sums = [jnp.sum(d_ra, axis=0, keepdims=True), jnp.sum(d_ia, axis=0, keepdims=True),
                d_sp * (-_sigmoid(-lamv)), jnp.sum(dxr, axis=0, keepdims=True),
                jnp.sum(dxr * _shift_down(x0, 3, xp), axis=0, keepdims=True),
                jnp.sum(dxr * _shift_down(x0, 2, xp), axis=0, keepdims=True),
                jnp.sum(dxr * _shift_down(x0, 1, xp), axis=0, keepdims=True),
                jnp.sum(dxr * x0, axis=0, keepdims=True)]
        dvec_ref[...] += jnp.concatenate(sums, axis=0)

    wspec = _full((nb, LRU_BLOCK, LRU_BLOCK))
    return pl.pallas_call(
        body, grid=(nt,),
        in_specs=[rev(), rev(0), rev(1), rev_prev(1), rev(), rev(), rev_prev(), _full((4, c)),
                  wspec, _full((1, c)), wspec, _full((1, c)), _full((1, c))],
        out_specs=[pl.BlockSpec((tr, c2), lambda i: (nt - 1 - i, 0)), wspec, wspec, _full((SUB, c))],
        out_shape=[SDS((t, c2), BF16), SDS((nb, LRU_BLOCK, LRU_BLOCK), F32), SDS((nb, LRU_BLOCK, LRU_BLOCK), F32),
                   SDS((SUB, c), F32)],
        scratch_shapes=[pltpu.VMEM((SUB, c), F32), pltpu.VMEM((SUB, c), F32), pltpu.VMEM((SUB, c), F32),
                        pltpu.VMEM((tr, c), F32)],
        compiler_params=_params('arbitrary'), name=name)(
            dy, z, z, z, xr, hs, hs, cw, wa, ba.reshape(1, c), wx, bx.reshape(1, c), lam.reshape(1, c))


def _local_step(x, p, pos, target, rep, weights_for_layer, emit_grads):
    t, d = x.shape
    depth = p.shape[0]
    w = rep
    half = ROPE_DIM // 2
    invf = ROPE_THETA ** (-2.0 * jnp.arange(half, dtype=F32) / ROPE_DIM)
    invf = jnp.concatenate([invf, invf, jnp.zeros((HEAD_DIM - ROPE_DIM,), F32)]).reshape(1, HEAD_DIM)
    dils = tuple(dil for _, dil in DILATED_PATTERNS)
    saved = []
    h = x
    for i in range(depth):
        kind, j = i % N_MIXERS, i // N_MIXERS
        wl, tok = weights_for_layer(i, 'mixer', h)
        s = {'h0': h, 'wl': wl}
        hn = _rms_fwd(h, w['norm_mix'][i], f'rms_mix_fwd_{i}')
        s['hn'] = hn
        if kind == 0:
            z = _mm(hn, wl['w_in'], 'nn', f'sc_in_{i}', dep=tok)
            y = _sc_fwd(z, wl['small'], f'sc_conv_fwd_{i}')
            h1 = _mm(y, wl['w_out'], 'nn', f'sc_out_{i}', extras=(h,), epi=lambda acc, res: (acc + res,))
            s.update(z=z, y=y)
        elif kind == 1:
            qkv = _mm(hn, wl['w_in'], 'nn', f'attn_qkv_{i}', dep=tok)
            qs, ks, vs = _rope_fwd(qkv, pos, invf, dils, f'rope_fwd_{i}')
            views = list(zip(qs, ks, vs))
            os_, lses = zip(*[_attn_fwd(qg, kg, vg, f'attn_fwd_{i}_g{g}') for g, (qg, kg, vg) in enumerate(views)])
            o, lse = _attn_combine(os_, lses, dils, f'attn_combine_{i}')
            h1 = _mm(o, wl['w_out'], 'nn', f'attn_out_{i}', extras=(h,), epi=lambda acc, res: (acc + res,))
            s.update(views=views, o=o, lse=lse)
        else:
            z = _mm(hn, wl['w_in'], 'nn', f'lru_in_{i}', dep=tok)
            sm = wl['small']
            y, hs, xr = _lru_fwd(z, sm[0:4], sm[4:5], w['lru_w_a'][j], sm[5:6], w['lru_w_x'][j], sm[6:7], sm[7:8],
                                 f'lru_fwd_{i}')
            h1 = _mm(y, wl['w_out'], 'nn', f'lru_out_{i}', extras=(h,), epi=lambda acc, res: (acc + res,))
            s.update(z=z, y=y, hs=hs, xr=xr)
        s['h1'] = h1
        more, tok = weights_for_layer(i, 'mlp', h1)
        wl.update(more)
        hm = _rms_fwd(h1, w['norm_mlp'][i], f'rms_mlp_fwd_{i}')
        u = _mm(hm, wl['mlp_up'], 'nn', f'mlp_up_{i}', out_dtypes=(BF16,), dep=tok)
        h2 = _mm(u, wl['mlp_down'], 'nn', f'mlp_down_{i}', a_pro=_relu2, extras=(h1,), epi=lambda acc, res: (acc + res,))
        hp = _rms_fwd(h2, w['norm_ple'][i], f'rms_ple_fwd_{i}')
        pp = _mm(p[i], wl['ple_proj'], 'nn', f'ple_proj_{i}', out_dtypes=(BF16,))
        h3, gate = _mm(hp, wl['ple_gate'], 'nn', f'ple_gate_{i}', out_dtypes=(F32, BF16), extras=(pp, h2),
                       epi=lambda acc, ppv, res: (res + _sigmoid(acc) * ppv, _sigmoid(acc)))
        s.update(hm=hm, u=u, h2=h2, hp=hp, pp=pp, gate=gate)
        saved.append(s)
        h = h3

    dh, loss, dg_final = _head(h, w['norm_final'], target, 'loss_head')
    grads = {n: [None] * depth for n in ('norm_mix', 'norm_mlp', 'norm_ple')}
    grads['norm_final'] = dg_final.reshape(d)
    started = None
    for i in reversed(range(depth)):
        kind, j = i % N_MIXERS, i // N_MIXERS
        s = saved[i]
        wl, gl = s['wl'], {}
        dpp, dgl = _ple_bwd_gate(dh, s['gate'], s['pp'], f'ple_bwd_gate_{i}')
        gl['ple_proj'] = _mm(p[i], dpp, 'tn', f'ple_dproj_{i}', out_dtypes=(BF16,), dep=started)
        gl['ple_gate'] = _mm(s['hp'], dgl, 'tn', f'ple_dgate_{i}', out_dtypes=(BF16,))
        dhp = _mm(dgl, wl['ple_gate'], 'nt', f'ple_dhp_{i}', out_dtypes=(BF16,))
        dh, dg = _rms_bwd(s['h2'], w['norm_ple'][i], dhp, dh, f'rms_ple_bwd_{i}')
        grads['norm_ple'][i] = dg.reshape(d)
        du = _mm(dh, wl['mlp_down'], 'nt', f'mlp_du_{i}', out_dtypes=(BF16,), extras=(s['u'],),
                 epi=lambda acc, uv: (acc * 2.0 * jnp.maximum(uv.astype(F32), 0.0),))
        gl['mlp_down'] = _mm(s['u'], dh, 'tn', f'mlp_ddown_{i}', out_dtypes=(BF16,), a_pro=_relu2)
        gl['mlp_up'] = _mm(s['hm'], du, 'tn', f'mlp_dup_{i}', out_dtypes=(BF16,), out_stacked=True)
        dhm = _mm(du, wl['mlp_up'], 'nt', f'mlp_dhm_{i}', out_dtypes=(BF16,))
        dh, dg = _rms_bwd(s['h1'], w['norm_mlp'][i], dhm, dh, f'rms_mlp_bwd_{i}')
        grads['norm_mlp'][i] = dg.reshape(d)
        started = emit_grads(i, 'mlp', gl, loss if i == depth - 1 else None)
        gl = {}
        if kind == 0:
            dy = _mm(dh, wl['w_out'], 'nt', f'sc_dy_{i}', dep=started)
            gl['w_out'] = _mm(s['y'], dh, 'tn', f'sc_dout_{i}', out_dtypes=(BF16,))
            dz, dwc = _sc_bwd(dy, s['z'], wl['small'], f'sc_conv_bwd_{i}')
            gl['small'] = dwc
            gl['w_in'] = _mm(s['hn'], dz, 'tn', f'sc_din_{i}', out_dtypes=(BF16,))
            started = emit_grads(i, 'mixer', gl)
            dhn = _mm(dz, wl['w_in'], 'nt', f'sc_dhn_{i}', out_dtypes=(BF16,), dep=started)
        elif kind == 1:
            do, delta = _mm(dh, wl['w_out'], 'nt', f'attn_do_{i}', out_dtypes=(BF16, F32), extras=(s['o'],),
                            epi=_delta_epilogue, dep=started)
            gl['w_out'] = _mm(s['o'], dh, 'tn', f'attn_dwo_{i}', out_dtypes=(BF16,))
            rows_in = {1: (do, s['lse'], delta)}
            for dil in dils:
                if dil not in rows_in:
                    rows_in[dil] = _dilate_many([do, s['lse'], delta], dil, (1, 3, 3), (BF16, F32, F32),
                                                f'attn_dilate_{i}_d{dil}')
            dqs, dks, dvs = zip(*[_attn_bwd(*s['views'][g], *rows_in[dil], f'attn_bwd_{i}_g{g}')
                                  for g, dil in enumerate(dils)])
            dqkv = _rope_bwd(dqs, dks, dvs, pos, invf, dils, f'rope_bwd_{i}')
            gl['w_in'] = _mm(s['hn'], dqkv, 'tn', f'attn_dqkv_{i}', out_dtypes=(BF16,), out_stacked=True)
            started = emit_grads(i, 'mixer', gl)
            dhn = _mm(dqkv, wl['w_in'], 'nt', f'attn_dhn_{i}', out_dtypes=(BF16,), dep=started)
        else:
            dy = _mm(dh, wl['w_out'], 'nt', f'lru_dy_{i}', dep=started)
            gl['w_out'] = _mm(s['y'], dh, 'tn', f'lru_dout_{i}', out_dtypes=(BF16,))
            sm = wl['small']
            dz, dwa, dwx, dvec = _lru_bwd(dy, s['z'], s['xr'], s['hs'], sm[0:4], w['lru_w_a'][j], sm[5:6],
                                          w['lru_w_x'][j], sm[6:7], sm[7:8], f'lru_bwd_{i}')
            gl['gates'], gl['small'] = (dwa, dwx), dvec
            gl['w_in'] = _mm(s['hn'], dz, 'tn', f'lru_din_{i}', out_dtypes=(BF16,))
            started = emit_grads(i, 'mixer', gl)
            dhn = _mm(dz, wl['w_in'], 'nt', f'lru_dhn_{i}', out_dtypes=(BF16,), dep=started)
        dh, dg = _rms_bwd(s['h0'], w['norm_mix'][i], dhn, dh, f'rms_mix_bwd_{i}')
        grads['norm_mix'][i] = dg.reshape(d)
        started = None
    return loss, dh, grads


_MESH = pl.DeviceIdType.MESH
_ANY = pl.BlockSpec(memory_space=pl.ANY)


def _block_view(ref, kind, idx):
    if kind == 'stack':
        return ref.at[idx]
    r = ref.shape[0] // N_DEV
    return ref.at[pl.ds(idx * r, r)]


def _gather_many(arrs, kinds, name, after=None):
    n = len(arrs)
    after = [] if after is None else [after]
    out_shapes = [SDS((N_DEV,) + a.shape if kd == 'stack' else (N_DEV * a.shape[0],) + a.shape[1:], a.dtype)
                  for a, kd in zip(arrs, kinds)]

    def body(*refs):
        x_refs, out_refs = refs[:n], refs[n + len(after):2 * n + len(after)]
        send_sems, recv_sems, local_sems = refs[2 * n + len(after):]
        x, y, c = lax.axis_index('x'), lax.axis_index('y'), lax.axis_index('c')
        me, sibling = (x, y, c), (x, y, 1 - c)
        chips = [(1 - x, y), (x, 1 - y), (1 - x, 1 - y)]

        def slab(t, px, py, pc):
            return _block_view(out_refs[t], kinds[t], 4 * px + 2 * py + pc)

        def copy(t, k, block, to, src=None):
            return pltpu.make_async_remote_copy(
                src_ref=slab(t, *block) if src is None else src, dst_ref=slab(t, *block),
                send_sem=send_sems.at[7 * t + k], recv_sem=recv_sems.at[7 * t + k], device_id=to, device_id_type=_MESH)

        mine = [pltpu.make_async_copy(x_refs[t], slab(t, *me), local_sems.at[t]) for t in range(n)]
        for cp in mine:
            cp.start()
        first = [copy(t, 0, me, sibling, src=x_refs[t]) for t in range(n)]
        first += [copy(t, 1 + j, me, (*chip, c), src=x_refs[t]) for j, chip in enumerate(chips) for t in range(n)]
        for cp in first:
            cp.start()
        passed = []
        for j, chip in enumerate(chips):
            for t in range(n):
                copy(t, 1 + j, (*chip, c), me).wait_recv()
                passed.append(copy(t, 4 + j, (*chip, c), sibling))
                passed[-1].start()
        for t in range(n):
            copy(t, 0, sibling, me).wait_recv()
            for j, chip in enumerate(chips):
                copy(t, 4 + j, (*chip, 1 - c), me).wait_recv()
        for cp in first + passed:
            cp.wait_send()
        for cp in mine:
            cp.wait()

    return pl.pallas_call(
        body, out_shape=out_shapes, in_specs=[_ANY] * (n + len(after)), out_specs=[_ANY] * n,
        scratch_shapes=[pltpu.SemaphoreType.DMA((7 * n,)), pltpu.SemaphoreType.DMA((7 * n,)), pltpu.SemaphoreType.DMA((n,))],
        name=name)(*arrs, *after)


_HBM = pl.BlockSpec(memory_space=pltpu.HBM)
_SEM = pl.BlockSpec(memory_space=pltpu.SEMAPHORE)
_EFFECT = pltpu.SideEffectType.DATAFLOW_SIDE_EFFECTING


def _direct_copies(mode, kinds, src_refs, land_refs, send_sems, recv_sems):
    x, y, c = lax.axis_index('x'), lax.axis_index('y'), lax.axis_index('c')
    my_idx = 4 * x + 2 * y + c
    copies = []
    for k in range(1, N_DEV):
        px, py, pc = (1 - x if k & 4 else x, 1 - y if k & 2 else y, 1 - c if k & 1 else c)
        for t, kd in enumerate(kinds):
            if mode == 'gather':
                src, dst = src_refs[t], _block_view(land_refs[t], kd, my_idx)
            else:
                src, dst = _block_view(src_refs[t], kd, 4 * px + 2 * py + pc), land_refs[t].at[my_idx]
            copies.append(pltpu.make_async_remote_copy(
                src_ref=src, dst_ref=dst, send_sem=send_sems.at[7 * t + k - 1], recv_sem=recv_sems.at[7 * t + k - 1],
                device_id=(px, py, pc), device_id_type=_MESH))
    return copies


def _own_part(mode, kind, src, land):
    idx = 4 * lax.axis_index('x') + 2 * lax.axis_index('y') + lax.axis_index('c')
    zeros = (0,) * (src.ndim - 1)
    if mode == 'gather':
        part = src
    elif kind == 'stack':
        part = lax.dynamic_index_in_dim(src, idx, 0, keepdims=False)
    else:
        r = src.shape[0] // N_DEV
        part = lax.dynamic_slice_in_dim(src, idx * r, r, 0)
    if mode == 'gather' and kind == 'rows':
        return lax.dynamic_update_slice(land, part, (idx * part.shape[0],) + zeros)
    return lax.dynamic_update_slice(land, part[None], (idx,) + (0,) * part.ndim)


def _send_start(mode, srcs, kinds, name, after=None):
    n = len(srcs)
    after = [] if after is None else [after]
    lands = []
    for a, kd in zip(srcs, kinds):
        if mode == 'gather':
            shape = (N_DEV,) + a.shape if kd == 'stack' else (N_DEV * a.shape[0],) + a.shape[1:]
        else:
            shape = a.shape if kd == 'stack' else (N_DEV, a.shape[0] // N_DEV) + a.shape[1:]
        lands.append(_own_part(mode, kd, a, lax.empty(shape, a.dtype)))

    def body(*refs):
        src_refs, land_refs = refs[:n], refs[n:2 * n]
        send_sems, recv_sems = refs[2 * n + len(after):2 * n + len(after) + 2]
        token = refs[-1]
        for cp in _direct_copies(mode, kinds, src_refs, land_refs, send_sems, recv_sems):
            cp.start()
        token[...] = jnp.zeros_like(token)

    outs = pl.pallas_call(
        body, name=name,
        out_shape=(pltpu.SemaphoreType.DMA((7 * n,)), pltpu.SemaphoreType.DMA((7 * n,)),
                   *[pltpu.HBM(a.shape, a.dtype) for a in srcs + lands], SDS((SUB, 128), F32)),
        in_specs=[_HBM] * (2 * n) + [_ANY] * len(after),
        out_specs=(_SEM, _SEM, *[_HBM] * (2 * n), pl.BlockSpec(memory_space=pltpu.VMEM)),
        input_output_aliases={i: 2 + i for i in range(2 * n)},
        compiler_params=pltpu.CompilerParams(has_side_effects=_EFFECT),
    )(*[pltpu.with_memory_space_constraint(a, pltpu.HBM) for a in srcs + lands], *after)
    return (outs[0], outs[1], list(outs[2:2 + 2 * n])), outs[-1]


def _send_wait(mode, flight, kinds, after, name):
    send, recv, bufs = flight
    n = len(kinds)

    def body(*refs):
        src_refs, land_refs, (send_sems, recv_sems) = refs[:n], refs[n:2 * n], refs[2 * n:2 * n + 2]
        copies = _direct_copies(mode, kinds, src_refs, land_refs, send_sems, recv_sems)
        for cp in copies:
            cp.wait_send()
        for cp in copies:
            cp.wait_recv()

    outs = pl.pallas_call(
        body, name=name, out_shape=[pltpu.HBM(a.shape, a.dtype) for a in bufs],
        in_specs=[_HBM] * (2 * n) + [_SEM, _SEM, _ANY], out_specs=[_HBM] * (2 * n),
        input_output_aliases={i: i for i in range(2 * n)},
        compiler_params=pltpu.CompilerParams(has_side_effects=_EFFECT),
    )(*bufs, send, recv, after)
    return list(outs[n:])


ADAMW_BLOCK_ELEMS = 128 * 1024


def _adamw_sum(wgt, parts, m, v, name):
    nl, r, c = wgt.shape
    assert len(parts) == nl and all(q.shape == (N_DEV, r, c) for q in parts), (name, wgt.shape, [q.shape for q in parts])
    tr = next((t for t in range(min(r, 512), 0, -16) if r % t == 0 and t * c <= ADAMW_BLOCK_ELEMS and t % 16 == 0), r)
    c1 = 1.0 - ADAM_B1 ** ADAM_STEP
    c2 = 1.0 - ADAM_B2 ** ADAM_STEP

    def body(w_ref, m_ref, v_ref, *rest):
        part_refs, (g_ref, d_ref, mo_ref, vo_ref) = rest[:nl], rest[nl:]
        for q in range(nl):
            @pl.when(pl.program_id(0) == q)
            def _(q=q):
                gv = part_refs[q][0].astype(F32)
                for s in range(1, N_DEV):
                    gv = gv + part_refs[q][s].astype(F32)
                mn = ADAM_B1 * m_ref[...] + (1.0 - ADAM_B1) * gv
                vn = ADAM_B2 * v_ref[...] + (1.0 - ADAM_B2) * (gv * gv)
                g_ref[...] = gv
                d_ref[...] = -ADAM_LR * ((mn / c1) / (jnp.sqrt(vn / c2) + ADAM_EPS) + ADAM_WD * w_ref[...])
                mo_ref[...] = mn
                vo_ref[...] = vn

    spec = pl.BlockSpec((None, tr, c), lambda l, i: (l, i, 0))
    part_specs = [pl.BlockSpec((N_DEV, tr, c), lambda l, i, q=q: (0, jnp.where(l == q, i, 0), 0)) for q in range(nl)]
    return pl.pallas_call(body, grid=(nl, r // tr), in_specs=[spec] * 3 + part_specs, out_specs=[spec] * 4,
                          out_shape=[SDS((nl, r, c), F32)] * 4, compiler_params=_params('arbitrary', 'arbitrary'),
                          name=name)(wgt, m, v, *parts)


MIXER_WEIGHTS = {0: ('sc_w_in', 'sc_w_out'), 1: ('attn_w_qkv', 'attn_w_o'), 2: ('lru_w_in', 'lru_w_out')}
STACKED_OPERANDS = ('attn_w_qkv', 'mlp_w_up')
LRU_SMALL = ('lru_conv_w', 'lru_conv_b', 'lru_b_a', 'lru_b_x', 'lru_lambda')


def _layer_items(i):
    w_in, w_out = MIXER_WEIGHTS[i % N_MIXERS]
    j = i // N_MIXERS
    return [('w_in', w_in, j), ('w_out', w_out, j), ('mlp_up', 'mlp_w_up', i), ('mlp_down', 'mlp_w_down', i),
            ('ple_gate', 'ple_w_gate', i), ('ple_proj', 'ple_w_proj', i)]


def _cols_to_full(stacked):
    return jnp.moveaxis(stacked, 0, 1).reshape(stacked.shape[1], -1)


def _full_to_cols(full):
    k, n = full.shape
    return jnp.moveaxis(full.reshape(k, N_DEV, n // N_DEV), 1, 0)


def _pad_to(a, rows):
    return jnp.pad(a, ((0, rows - a.shape[0]), (0, 0)))


def _small_block(src, i):
    kind, j = i % N_MIXERS, i // N_MIXERS
    if kind == 0:
        return _pad_to(src['sc_w_conv'][j], SUB)
    if kind == 2:
        return jnp.concatenate([src[n][j].reshape(-1, src[n].shape[-1]) for n in LRU_SMALL], axis=0)
    return None


def kernel(x, p, positions, norm_mix, norm_mlp, norm_ple, norm_final, sc_w_in, sc_w_conv, sc_w_out, attn_w_qkv, attn_w_o, lru_w_in, lru_conv_w, lru_conv_b, lru_w_a, lru_b_a, lru_w_x, lru_b_x, lru_lambda, lru_w_out, mlp_w_up, mlp_w_down, ple_w_gate, ple_w_proj, loss_target, m_norm_mix, m_norm_mlp, m_norm_ple, m_norm_final, m_sc_w_in, m_sc_w_conv, m_sc_w_out, m_attn_w_qkv, m_attn_w_o, m_lru_w_in, m_lru_conv_w, m_lru_conv_b, m_lru_w_a, m_lru_b_a, m_lru_w_x, m_lru_b_x, m_lru_lambda, m_lru_w_out, m_mlp_w_up, m_mlp_w_down, m_ple_w_gate, m_ple_w_proj, v_norm_mix, v_norm_mlp, v_norm_ple, v_norm_final, v_sc_w_in, v_sc_w_conv, v_sc_w_out, v_attn_w_qkv, v_attn_w_o, v_lru_w_in, v_lru_conv_w, v_lru_conv_b, v_lru_w_a, v_lru_b_a, v_lru_w_x, v_lru_b_x, v_lru_lambda, v_lru_w_out, v_mlp_w_up, v_mlp_w_down, v_ple_w_gate, v_ple_w_proj):
    loc = dict(locals())
    shards = {n: loc[n] for n in WEIGHTS}
    moms = {n: loc['m_' + n] for n in WEIGHTS}
    vels = {n: loc['v_' + n] for n in WEIGHTS}

    depth, t, d = p.shape[0], x.shape[1], x.shape[2]

    def comm_kind(name):
        return 'stack' if SHARD_AXIS[name] == 2 else 'rows'

    part_keys = {'mlp': ('mlp_up', 'mlp_down', 'ple_gate', 'ple_proj'), 'mixer': ('w_in', 'w_out')}
    halves = [(i, part) for i in range(depth) for part in ('mixer', 'mlp')]

    def half_shards(i, part):
        items = [it for it in _layer_items(i) if it[0] in part_keys[part]]
        arrs = [shards[n][idx].astype(BF16) for _, n, idx in items]
        kinds = [comm_kind(n) for _, n, _ in items]
        small = _small_block(shards, i) if part == 'mixer' else None
        if small is not None:
            arrs.append(small)
            kinds.append('stack')
        return items, arrs, kinds

    def half_weights(i, items, kinds, outs):
        wl = {key: (_cols_to_full(o) if kd == 'stack' and n not in STACKED_OPERANDS else o)
              for (key, n, _), kd, o in zip(items, kinds, outs)}
        if len(outs) > len(items):
            wl['small'] = _cols_to_full(outs[-1])[:shards['sc_w_conv'].shape[1] if i % N_MIXERS == 0 else SUB]
        return wl

    first = [half_shards(0, part) for part in ('mixer', 'mlp')]
    outs0 = _gather_many(first[0][1] + first[1][1], first[0][2] + first[1][2], 'gather_weights_0')
    weights0 = {**half_weights(0, first[0][0], first[0][2], outs0[:len(first[0][1])]),
                **half_weights(0, first[1][0], first[1][2], outs0[len(first[0][1]):])}
    pending = {}

    def start_gather(pos, after):
        if pos >= len(halves):
            return None
        i, part = halves[pos]
        items, arrs, kinds = half_shards(i, part)
        flight, token = _send_start('gather', arrs, kinds, f'gather_weights_start_{part}_{i}', after=after)
        pending[pos] = (items, kinds, flight)
        return token

    first_token = start_gather(2, outs0[0])
    second_token = start_gather(3, first_token)

    def weights_for_layer(i, part, h):
        pos = halves.index((i, part))
        if pos == 0:
            return weights0, second_token
        if pos == 1:
            return {}, None
        items, kinds, flight = pending.pop(pos)
        outs = _send_wait('gather', flight, kinds, h, f'gather_weights_wait_{part}_{i}')
        return half_weights(i, items, kinds, outs), start_gather(pos + 2, outs[0])

    exchanges, gate_gathers, total_loss = {}, {}, []

    def gate_block(src, j):
        return jnp.concatenate([src[n][j].reshape(-1, LRU_BLOCK) for n in ('lru_w_a', 'lru_w_x')], axis=0)

    def emit_grads(i, part, gl, loss=None):
        after = None
        if loss is not None:
            total_loss.append(lax.psum(loss[0, 0], ('x', 'y', 'c')))
            after = jnp.full((SUB, 128), total_loss[0], F32)
        if 'gates' in gl:
            blk = gate_block({'lru_w_a': [gl['gates'][0]], 'lru_w_x': [gl['gates'][1]]}, 0)
            gate_gathers[i] = _send_start('gather', [blk], ['stack'], f'gather_gate_grads_start_{i}')[0]
        items = [it for it in _layer_items(i) if it[0] in part_keys[part]]
        kinds = [comm_kind(n) for _, n, _ in items]
        arrs = [_full_to_cols(gl[key]) if kd == 'stack' and gl[key].ndim == 2 else gl[key]
                for (key, _, _), kd in zip(items, kinds)]
        if part == 'mixer' and i % N_MIXERS == 0:
            arrs.append(_full_to_cols(_pad_to(gl['small'], SUB)))
        elif part == 'mixer' and i % N_MIXERS == 2:
            dv = gl['small']
            arrs.append(_full_to_cols(jnp.concatenate([dv[4:8], dv[3:4], dv[0:1], dv[1:2], dv[2:3]], axis=0)))
        kinds += ['stack'] * (len(arrs) - len(kinds))
        flight, token = _send_start('exchange', arrs, kinds, f'exchange_grads_start_{part}_{i}', after=after)
        exchanges[(i, part)] = (items, kinds, flight)
        return token

    rep = {n: shards[n] for n in ('norm_mix', 'norm_mlp', 'norm_ple', 'norm_final')}
    rep['lru_w_a'], rep['lru_w_x'] = shards['lru_w_a'].astype(BF16), shards['lru_w_x'].astype(BF16)
    loss, grad_x, rgrads = _local_step(x.reshape(t, d), p.reshape(depth, t, p.shape[3]), positions.reshape(t, 1),
                                       loss_target.reshape(t, d), rep, weights_for_layer, emit_grads)

    received, res = {}, {}

    def finish_exchange(key, after):
        items, kinds, flight = exchanges[key]
        outs = _send_wait('exchange', flight, kinds, after, f'exchange_grads_wait_{key[1]}_{key[0]}')
        for (_, n, idx), o in zip(items, outs):
            received[(n, idx)] = o
        if len(outs) > len(items):
            received[('small', key[0])] = outs[-1]

    def big_adamw(names):
        for n in names:
            res[n] = _adamw_sum(shards[n], [received[(n, l)] for l in range(shards[n].shape[0])], moms[n], vels[n],
                                f'adamw_{n}')

    last = (0, 'mixer')
    for key in exchanges:
        if key != last:
            finish_exchange(key, grad_x)
    big = [n for n in WEIGHTS if SHARD_AXIS[n] is not None and shards[n].ndim == 3 and n not in ('sc_w_conv', 'lru_conv_w')]
    late = [n for n in big if n in MIXER_WEIGHTS[0]]
    big_adamw([n for n in big if n not in late])
    finish_exchange(last, jnp.full((SUB, 128), sum(r[0].reshape(-1)[0] for r in res.values()), F32))
    big_adamw(late)


    def small_adamw(layers, name):
        w_, m_, v_ = (jnp.stack([_small_block(src, i) for i in layers]) for src in (shards, moms, vels))
        return _adamw_sum(w_, [received[('small', i)] for i in layers], m_, v_, name)

    sc = small_adamw([i for i in range(depth) if i % N_MIXERS == 0], 'adamw_sc_w_conv')
    res['sc_w_conv'] = tuple(o[:, :shards['sc_w_conv'].shape[1]] for o in sc)
    lru = small_adamw([i for i in range(depth) if i % N_MIXERS == 2], 'adamw_lru_small')
    row = 0
    for n in LRU_SMALL:
        k = shards[n].size // shards[n].shape[0] // shards[n].shape[-1]
        res[n] = tuple(o[:, row:row + k].reshape(shards[n].shape) for o in lru)
        row += k

    def all_updated():
        return jnp.full((SUB, 128), sum(r[0].reshape(-1)[0] for r in res.values()), F32)

    gate_layers = sorted(gate_gathers)
    gate_parts = [_send_wait('gather', gate_gathers[i], ['stack'], all_updated(), f'gather_gate_grads_wait_{i}')[0]
                  for i in gate_layers]
    gate_w, gate_m, gate_v = (jnp.stack([gate_block(src, j) for j in range(len(gate_layers))])
                              for src in (shards, moms, vels))
    gates = _adamw_sum(gate_w, gate_parts, gate_m, gate_v, 'adamw_lru_gates')
    half = gates[0].shape[1] // 2
    res['lru_w_a'] = tuple(o[:, :half].reshape(shards['lru_w_a'].shape) for o in gates)
    res['lru_w_x'] = tuple(o[:, half:].reshape(shards['lru_w_x'].shape) for o in gates)

    norm_names = ('norm_mix', 'norm_mlp', 'norm_ple', 'norm_final')

    def norm_block(src):
        cat = jnp.concatenate([src[n].reshape(-1, d) for n in norm_names], axis=0)
        return _pad_to(cat, -(-cat.shape[0] // HALO) * HALO)

    rfull = {n: (rgrads[n] if n == 'norm_final' else jnp.stack(rgrads[n], axis=0)) for n in norm_names}
    parts_norm, = _gather_many([norm_block(rfull)], ['stack'], 'gather_norm_grads', after=all_updated())
    norms = _adamw_sum(norm_block(shards)[None], [parts_norm], norm_block(moms)[None], norm_block(vels)[None],
                       'adamw_norms')
    row = 0
    for n in norm_names:
        k = shards[n].size // d
        res[n] = tuple(o[0, row:row + k].reshape(shards[n].shape) for o in norms)
        row += k

    return (total_loss[0], grad_x.reshape(x.shape), *[res[n][0] for n in WEIGHTS], *[res[n][1] for n in WEIGHTS],
            *[res[n][2] for n in WEIGHTS], *[res[n][3] for n in WEIGHTS])
```

```python
import functools
import math

import jax
import jax.numpy as jnp
from jax import lax
from jax.experimental import pallas as pl
from jax.experimental.pallas import tpu as pltpu

F32 = jnp.float32
BF16 = jnp.bfloat16
SDS = jax.ShapeDtypeStruct

N_DEV = 8
RMS_EPS = 1e-6
N_MIXERS = 3
HEAD_DIM = 128
DILATED_PATTERNS = ((128, 1), (512, 4), (2048, 16))
ATTN_BLOCK = 128
ROPE_THETA = 500000.0
ROPE_DIM = HEAD_DIM // 4
LRU_BLOCK = 128
LRU_C = 8.0
ADAM_LR, ADAM_B1, ADAM_B2, ADAM_EPS, ADAM_WD, ADAM_STEP = 0.001, 0.9, 0.999, 1e-08, 0.01, 10

HALO = 16
SUB = 8
VMEM_LIMIT = 56 * 1024 * 1024
NEG = -1e30

SHARD_AXIS = {
    'norm_mix': None, 'norm_mlp': None, 'norm_ple': None, 'norm_final': None,
    'sc_w_in': 2, 'sc_w_conv': 2, 'sc_w_out': 1, 'attn_w_qkv': 2, 'attn_w_o': 1,
    'lru_w_in': 2, 'lru_conv_w': 2, 'lru_conv_b': 1, 'lru_w_a': None, 'lru_b_a': 1,
    'lru_w_x': None, 'lru_b_x': 1, 'lru_lambda': 1, 'lru_w_out': 1,
    'mlp_w_up': 2, 'mlp_w_down': 1, 'ple_w_gate': 1, 'ple_w_proj': 2,
}
WEIGHTS = list(SHARD_AXIS)


def _params(*sem):
    return pltpu.CompilerParams(dimension_semantics=sem or None, vmem_limit_bytes=VMEM_LIMIT)


def _row_tile(t, pref=256):
    tr = min(t, pref)
    assert t % tr == 0 and tr % HALO == 0
    return tr


def _row(tr, c, col=0):
    return pl.BlockSpec((tr, c), lambda i, col=col: (i, col))


def _full(shape):
    return pl.BlockSpec(shape, lambda *_: (0,) * len(shape))


def _sigmoid(x):
    return 1.0 / (1.0 + jnp.exp(-x))


def _expm1(x):
    taylor = x * (1.0 + x * (0.5 + x * (1.0 / 6.0 + x * (1.0 / 24.0 + x * (1.0 / 120.0)))))
    return jnp.where(jnp.abs(x) < 0.1, taylor, jnp.exp(x) - 1.0)


def _softplus(x):
    z = jnp.exp(-jnp.abs(x))
    log1p = jnp.where(z < 0.01, z * (1.0 - z * (0.5 - z * (1.0 / 3.0 - z * 0.25))), jnp.log(1.0 + z))
    return jnp.maximum(x, 0.0) + log1p


_GELU_K = math.sqrt(2.0 / math.pi)


def _gelu_and_grad(x):
    inner = _GELU_K * (x + 0.044715 * x * x * x)
    th = jnp.tanh(inner)
    g = 0.5 * x * (1.0 + th)
    dg = 0.5 * (1.0 + th) + 0.5 * x * (1.0 - th * th) * _GELU_K * (1.0 + 3.0 * 0.044715 * x * x)
    return g, dg


def _shift_down(x, k, prev):
    row = lax.broadcasted_iota(jnp.int32, (SUB, x.shape[1]), 0)
    xr = pltpu.roll(x, k, 0)
    top = jnp.where(row < k, pltpu.roll(prev, k, 0), xr[0:SUB])
    return jnp.concatenate([top, xr[SUB:]], axis=0)


def _shift_up(x, k, nxt):
    r = x.shape[0]
    row = lax.broadcasted_iota(jnp.int32, (SUB, x.shape[1]), 0)
    xr = pltpu.roll(x, r - k, 0)
    bot = jnp.where(row >= SUB - k, pltpu.roll(nxt, SUB - k, 0), xr[r - SUB:r])
    return jnp.concatenate([xr[:r - SUB], bot], axis=0)


_DIMS = {'nn': (((1,), (0,)), ((), ())), 'nt': (((1,), (1,)), ((), ())), 'tn': (((0,), (0,)), ((), ()))}


MM_VMEM_BUDGET = 50 * 1024 * 1024
MM_MIN_TK = 1024
MM_MIN_TM = 1024


def _tile_options(dim):
    return [c for c in range(dim, 127, -128) if dim % c == 0] or [dim]


def _choose_tiles(m, n, k, n_span, k_span, a_size, b_size, mn_size, a_temp):
    best = None
    for tm in _tile_options(m):
        for tn in _tile_options(n_span):
            for tk in _tile_options(k_span):
                nk = k // tk
                need = (2 * (tm * tk * a_size + tk * tn * b_size + tm * tn * mn_size) + tm * tn * 4 * (1 + (nk > 1))
                        + tm * tk * 4 * a_temp)
                score = (-min(tk, MM_MIN_TK), -min(tm, MM_MIN_TM), -tm * tn, nk, -min(tm, 2 * MM_MIN_TM), -tn)
                if need <= MM_VMEM_BUDGET and (best is None or score < best[0]):
                    best = (score, (tm, tn, tk))
    return best[1]


def _mm(a, b, dims, name, out_dtypes=(F32,), a_pro=None, extras=(), epi=None, out_stacked=False, dep=None):
    deps = [] if dep is None else [dep]
    stacked = b.ndim == 3
    b_rows, b_cols = (b.shape[1], N_DEV * b.shape[2]) if stacked else b.shape
    if dims == 'nn':
        (m, k), (k2, n) = a.shape, (b_rows, b_cols)
    elif dims == 'nt':
        (m, k), (n, k2) = a.shape, (b_rows, b_cols)
    else:
        (k, m), (k2, n) = a.shape, (b_rows, b_cols)
    assert k == k2, (name, a.shape, b.shape)
    assert not (stacked and dims == 'tn') and not (out_stacked and (extras or dims != 'tn'))
    tm, tn, tk = _choose_tiles(
        m, n, k, n // N_DEV if (out_stacked or (stacked and dims == 'nn')) else n,
        k // N_DEV if (stacked and dims == 'nt') else k, a.dtype.itemsize, b.dtype.itemsize,
        sum(e.dtype.itemsize for e in extras) + sum(jnp.dtype(dt).itemsize for dt in out_dtypes),
        a_pro is not None or a.dtype != BF16)
    assert m % tm == 0 and n % tn == 0 and k % tk == 0, (name, m, n, k)
    nk = k // tk
    a_spec = pl.BlockSpec((tk, tm), lambda i, j, kk: (kk, i)) if dims == 'tn' else pl.BlockSpec((tm, tk), lambda i, j, kk: (i, kk))
    if not stacked:
        b_spec = pl.BlockSpec((tn, tk), lambda i, j, kk: (j, kk)) if dims == 'nt' else pl.BlockSpec((tk, tn), lambda i, j, kk: (kk, j))
    elif dims == 'nn':
        per = b.shape[2] // tn
        b_spec = pl.BlockSpec((None, tk, tn), lambda i, j, kk: (j // per, kk, j % per))
    else:
        per = b.shape[2] // tk
        b_spec = pl.BlockSpec((None, tn, tk), lambda i, j, kk: (kk // per, j, kk % per))
    if out_stacked:
        per_o = n // N_DEV // tn
        o_spec = pl.BlockSpec((None, tm, tn), lambda i, j, kk: (j // per_o, i, j % per_o))
        o_shape = (N_DEV, m, n // N_DEV)
    else:
        o_spec = pl.BlockSpec((tm, tn), lambda i, j, kk: (i, j))
        o_shape = (m, n)
    n_ex, n_out = len(extras), len(out_dtypes)
    for e in extras:
        assert e.shape == (m, n), (name, e.shape)

    def body(a_ref, b_ref, *rest):
        rest = rest[len(deps):]
        ex_refs, out_refs = rest[:n_ex], rest[n_ex:n_ex + n_out]
        kk = pl.program_id(2)
        av = a_ref[...]
        if a_pro is not None:
            av = a_pro(av.astype(F32))
        part = lax.dot_general(av.astype(BF16), b_ref[...].astype(BF16), _DIMS[dims], preferred_element_type=F32)

        def finish(res):
            outs = (res,) if epi is None else epi(res, *[e[...] for e in ex_refs])
            for o_ref, o in zip(out_refs, outs):
                o_ref[...] = o.astype(o_ref.dtype)

        if nk == 1:
            finish(part)
        else:
            acc = rest[-1]

            @pl.when(kk == 0)
            def _():
                acc[...] = part

            @pl.when(kk > 0)
            def _():
                acc[...] += part

            @pl.when(kk == nk - 1)
            def _():
                finish(acc[...])

    out = pl.pallas_call(
        body, grid=(m // tm, n // tn, nk),
        in_specs=[a_spec, b_spec] + [_ANY] * len(deps) + [o_spec] * n_ex,
        out_specs=[o_spec] * n_out,
        out_shape=[SDS(o_shape, d) for d in out_dtypes],
        scratch_shapes=[] if nk == 1 else [pltpu.VMEM((tm, tn), F32)],
        compiler_params=_params('parallel', 'parallel', 'arbitrary'), name=name)(a, b, *deps, *extras)
    return out[0] if n_out == 1 else out


def _relu2(u):
    r = jnp.maximum(u, 0.0)
    return r * r


STREAM_ROWS = 512


def _rms_fwd(h, g, name):
    t, d = h.shape
    tr = _row_tile(t, STREAM_ROWS)

    def body(h_ref, g_ref, o_ref):
        x = h_ref[...]
        r = lax.rsqrt(jnp.mean(x * x, axis=-1, keepdims=True) + RMS_EPS)
        o_ref[...] = (x * r * g_ref[...]).astype(o_ref.dtype)

    return pl.pallas_call(body, grid=(t // tr,), in_specs=[_row(tr, d), _full((1, d))], out_specs=_row(tr, d),
                          out_shape=SDS((t, d), BF16), compiler_params=_params('parallel'), name=name)(h, g.reshape(1, d))


def _rms_bwd(h, g, dhn, dres, name):
    t, d = h.shape
    tr = _row_tile(t, STREAM_ROWS)

    def body(h_ref, g_ref, dhn_ref, dres_ref, dh_ref, dg_ref):
        @pl.when(pl.program_id(0) == 0)
        def _():
            dg_ref[...] = jnp.zeros_like(dg_ref)

        x = h_ref[...]
        r = lax.rsqrt(jnp.mean(x * x, axis=-1, keepdims=True) + RMS_EPS)
        dy = dhn_ref[...].astype(F32)
        gy = dy * g_ref[...]
        dx = r * gy - x * (r * r * r) * jnp.mean(gy * x, axis=-1, keepdims=True)
        dh_ref[...] = dres_ref[...] + dx
        dg_ref[...] += jnp.sum(dy * (x * r), axis=0, keepdims=True)

    return pl.pallas_call(body, grid=(t // tr,),
                          in_specs=[_row(tr, d), _full((1, d)), _row(tr, d), _row(tr, d)],
                          out_specs=[_row(tr, d), _full((1, d))],
                          out_shape=[SDS((t, d), F32), SDS((1, d), F32)],
                          compiler_params=_params('arbitrary'), name=name)(h, g.reshape(1, d), dhn, dres)


def _head(h, g, target, name):
    t, d = h.shape
    tr = _row_tile(t, STREAM_ROWS)

    def body(h_ref, g_ref, t_ref, dh_ref, loss_ref, dg_ref):
        @pl.when(pl.program_id(0) == 0)
        def _():
            dg_ref[...] = jnp.zeros_like(dg_ref)
            loss_ref[...] = jnp.zeros_like(loss_ref)

        x = h_ref[...]
        gv = g_ref[...]
        r = lax.rsqrt(jnp.mean(x * x, axis=-1, keepdims=True) + RMS_EPS)
        xh = x * r
        e = xh * gv - t_ref[...]
        per_tok = jnp.mean(e * e, axis=-1, keepdims=True)
        loss_ref[...] += jnp.broadcast_to(0.5 * jnp.sum(per_tok, axis=0, keepdims=True), loss_ref.shape)
        dy = e * (1.0 / d)
        gy = dy * gv
        dh_ref[...] = r * gy - x * (r * r * r) * jnp.mean(gy * x, axis=-1, keepdims=True)
        dg_ref[...] += jnp.sum(dy * xh, axis=0, keepdims=True)

    return pl.pallas_call(body, grid=(t // tr,),
                          in_specs=[_row(tr, d), _full((1, d)), _row(tr, d)],
                          out_specs=[_row(tr, d), _full((1, 128)), _full((1, d))],
                          out_shape=[SDS((t, d), F32), SDS((1, 128), F32), SDS((1, d), F32)],
                          compiler_params=_params('arbitrary'), name=name)(h, g.reshape(1, d), target)


def _ple_bwd_gate(dh3, gate, pp, name):
    t, d = dh3.shape
    tr = _row_tile(t, STREAM_ROWS)

    def body(dh_ref, g_ref, pp_ref, dpp_ref, dgl_ref):
        dh = dh_ref[...]
        gt = g_ref[...].astype(F32)
        dpp_ref[...] = (dh * gt).astype(dpp_ref.dtype)
        dgl_ref[...] = (dh * pp_ref[...].astype(F32) * gt * (1.0 - gt)).astype(dgl_ref.dtype)

    return pl.pallas_call(body, grid=(t // tr,), in_specs=[_row(tr, d)] * 3, out_specs=[_row(tr, d)] * 2,
                          out_shape=[SDS((t, d), BF16), SDS((t, d), BF16)],
                          compiler_params=_params('parallel'), name=name)(dh3, gate, pp)


def _halo_prev(tr, c, col=0):
    return pl.BlockSpec((HALO, c), lambda i, col=col: (jnp.maximum(i * (tr // HALO) - 1, 0), col))


def _halo_next(tr, c, t, col=0):
    return pl.BlockSpec((HALO, c), lambda i, col=col: (jnp.minimum((i + 1) * (tr // HALO), t // HALO - 1), col))


def _sc_fwd(z, w, name):
    t, c3 = z.shape
    c = c3 // 3
    tr = _row_tile(t)

    def body(z_ref, zp_ref, w_ref, y_ref):
        i = pl.program_id(0)
        zz = z_ref[...]
        gb, cx = zz[:, :c], zz[:, c:2 * c] * zz[:, 2 * c:]
        zp = zp_ref[SUB:HALO, :]
        cxp = jnp.where(i > 0, zp[:, c:2 * c] * zp[:, 2 * c:], 0.0)
        wv = w_ref[...]
        conv = wv[2:3] * cx + wv[1:2] * _shift_down(cx, 1, cxp) + wv[0:1] * _shift_down(cx, 2, cxp)
        y_ref[...] = (gb * conv).astype(y_ref.dtype)

    return pl.pallas_call(body, grid=(t // tr,),
                          in_specs=[_row(tr, c3), _halo_prev(tr, c3), _full((3, c))],
                          out_specs=_row(tr, c), out_shape=SDS((t, c), BF16),
                          compiler_params=_params('parallel'), name=name)(z, z, w)


def _sc_bwd(dy, z, w, name):
    t, c3 = z.shape
    c = c3 // 3
    tr = _row_tile(t)
    nt = t // tr

    def body(dy_ref, dyn_ref, z_ref, zp_ref, zn_ref, w_ref, dz_ref, dw_ref):
        i = pl.program_id(0)

        @pl.when(i == 0)
        def _():
            dw_ref[...] = jnp.zeros_like(dw_ref)

        zz = z_ref[...]
        gb, gc, xi = zz[:, :c], zz[:, c:2 * c], zz[:, 2 * c:]
        cx = gc * xi
        zp = zp_ref[SUB:HALO, :]
        cxp = jnp.where(i > 0, zp[:, c:2 * c] * zp[:, 2 * c:], 0.0)
        wv = w_ref[...]
        cx1, cx2 = _shift_down(cx, 1, cxp), _shift_down(cx, 2, cxp)
        conv = wv[2:3] * cx + wv[1:2] * cx1 + wv[0:1] * cx2
        dyv = dy_ref[...]
        dconv = dyv * gb
        dcn = jnp.where(i < nt - 1, dyn_ref[0:SUB, :] * zn_ref[0:SUB, :c], 0.0)
        dcx = wv[2:3] * dconv + wv[1:2] * _shift_up(dconv, 1, dcn) + wv[0:1] * _shift_up(dconv, 2, dcn)
        dz_ref[:, :c] = (dyv * conv).astype(dz_ref.dtype)
        dz_ref[:, c:2 * c] = (dcx * xi).astype(dz_ref.dtype)
        dz_ref[:, 2 * c:] = (dcx * gc).astype(dz_ref.dtype)
        dw_ref[...] += jnp.concatenate([jnp.sum(dconv * cx2, axis=0, keepdims=True),
                                        jnp.sum(dconv * cx1, axis=0, keepdims=True),
                                        jnp.sum(dconv * cx, axis=0, keepdims=True)], axis=0)

    return pl.pallas_call(body, grid=(nt,),
                          in_specs=[_row(tr, c), _halo_next(tr, c, t), _row(tr, c3), _halo_prev(tr, c3),
                                    _halo_next(tr, c3, t), _full((3, c))],
                          out_specs=[_row(tr, c3), _full((3, c))],
                          out_shape=[SDS((t, c3), BF16), SDS((3, c), F32)],
                          compiler_params=_params('arbitrary'), name=name)(dy, dy, z, z, z, w)


def _perm(tr, dil, inverse=False):
    n = tr // dil
    a = lax.broadcasted_iota(jnp.int32, (tr, tr), 1 if inverse else 0)
    b = lax.broadcasted_iota(jnp.int32, (tr, tr), 0 if inverse else 1)
    return (b == (a % n) * dil + a // n).astype(BF16)


def _permute(pm, x, terms):
    if x.dtype == BF16:
        return jnp.dot(pm, x, preferred_element_type=F32)
    acc = None
    for _ in range(terms):
        part = x.astype(BF16)
        y = jnp.dot(pm, part, preferred_element_type=F32)
        acc = y if acc is None else acc + y
        x = x - part.astype(F32)
    return acc


def _store_dilated(o_ref, y, dil, d):
    n = y.shape[0] // dil
    for rho in range(dil):
        o_ref[:, rho * d:(rho + 1) * d] = y[rho * n:(rho + 1) * n].astype(o_ref.dtype)


def _load_dilated(ref, dil, d):
    return jnp.concatenate([ref[:, rho * d:(rho + 1) * d] for rho in range(dil)], axis=0) if dil > 1 else ref[...]


def _rope_heads(x, lane, cos, sin):
    return jnp.concatenate([_rope_apply(x[:, s:s + HEAD_DIM], lane, cos, sin)
                            for s in range(0, x.shape[1], HEAD_DIM)], axis=1)


def _rope_tables(pos, invf, sign):
    lane = lax.broadcasted_iota(jnp.int32, (pos.shape[0], HEAD_DIM), 1)
    ang = pos.astype(F32) * invf
    half = ROPE_DIM // 2
    cos = jnp.where(lane < ROPE_DIM, jnp.cos(ang), 1.0)
    sin = jnp.sin(ang) * sign
    sin = jnp.where(lane < half, -sin, jnp.where(lane < ROPE_DIM, sin, 0.0))
    return lane, cos, sin


def _rope_apply(x, lane, cos, sin):
    half = ROPE_DIM // 2
    xs = jnp.where(lane < half, pltpu.roll(x, HEAD_DIM - half, 1), pltpu.roll(x, half, 1))
    return x * cos + xs * sin


def _dilated_spec(tr, dil, d):
    return pl.BlockSpec((tr // dil, dil * d), lambda i: (i, 0))


def _rope_fwd(qkv, pos, invf, dils, name):
    t, w3 = qkv.shape
    w, ng = w3 // 3, len(dils)
    d = w // ng
    tr = _row_tile(t)

    def body(q_ref, k_ref, v_ref, pos_ref, invf_ref, *out_refs):
        lane, cos, sin = _rope_tables(pos_ref[...], invf_ref[...], 1.0)
        for g, dil in enumerate(dils):
            cs = slice(g * d, (g + 1) * d)
            vals = [_rope_heads(q_ref[:, cs], lane, cos, sin).astype(BF16),
                    _rope_heads(k_ref[:, cs], lane, cos, sin).astype(BF16), v_ref[:, cs].astype(BF16)]
            if dil > 1:
                pm = _perm(tr, dil)
                vals = [_permute(pm, a, 1) for a in vals]
            for o_ref, a in zip(out_refs[g::ng], vals):
                _store_dilated(o_ref, a, dil, d)

    outs = pl.pallas_call(body, grid=(t // tr,),
                          in_specs=[_row(tr, w, 0), _row(tr, w, 1), _row(tr, w, 2), _row(tr, 1), _full((1, HEAD_DIM))],
                          out_specs=[_dilated_spec(tr, dil, d) for dil in dils] * 3,
                          out_shape=[SDS((t // dil, dil * d), BF16) for dil in dils] * 3,
                          compiler_params=_params('parallel'), name=name)(qkv, qkv, qkv, pos, invf)
    return outs[:ng], outs[ng:2 * ng], outs[2 * ng:]


def _rope_bwd(dqs, dks, dvs, pos, invf, dils, name):
    ng = len(dils)
    t = dqs[0].shape[0] * dils[0]
    d = dqs[0].shape[1] // dils[0]
    w = ng * d
    tr = _row_tile(t)

    def body(*refs):
        dq_refs, dk_refs, dv_refs = refs[:ng], refs[ng:2 * ng], refs[2 * ng:3 * ng]
        pos_ref, invf_ref, o_ref = refs[3 * ng:]
        pos_f = jnp.broadcast_to(pos_ref[...].astype(F32), (tr, HEAD_DIM))
        for g, dil in enumerate(dils):
            pos_g = pos_f if dil == 1 else _permute(_perm(tr, dil), pos_f, 3)
            lane, cos, sin = _rope_tables(pos_g, invf_ref[...], -1.0)
            vals = [_rope_heads(_load_dilated(dq_refs[g], dil, d), lane, cos, sin),
                    _rope_heads(_load_dilated(dk_refs[g], dil, d), lane, cos, sin), _load_dilated(dv_refs[g], dil, d)]
            back = _perm(tr, dil, inverse=True) if dil > 1 else None
            for sec, a in enumerate(vals):
                a = a.astype(BF16)
                if dil > 1:
                    a = _permute(back, a, 1)
                o_ref[:, sec * w + g * d:sec * w + (g + 1) * d] = a.astype(o_ref.dtype)

    return pl.pallas_call(body, grid=(t // tr,),
                          in_specs=[_dilated_spec(tr, dil, d) for dil in dils] * 3 + [_row(tr, 1), _full((1, HEAD_DIM))],
                          out_specs=_row(tr, 3 * w), out_shape=SDS((t, 3 * w), BF16),
                          compiler_params=_params('parallel'), name=name)(*dqs, *dks, *dvs, pos, invf)


def _dilate_many(arrs, dil, terms, out_dtypes, name):
    t, d = arrs[0].shape
    tr = _row_tile(t)
    na = len(arrs)

    def body(*refs):
        pm = _perm(tr, dil)
        for a_ref, o_ref, k in zip(refs[:na], refs[na:], terms):
            _store_dilated(o_ref, _permute(pm, a_ref[...], k), dil, d)

    return pl.pallas_call(body, grid=(t // tr,), in_specs=[_row(tr, d)] * na,
                          out_specs=[_dilated_spec(tr, dil, d)] * na,
                          out_shape=[SDS((t // dil, dil * d), dt) for dt in out_dtypes],
                          compiler_params=_params('parallel'), name=name)(*arrs)


def _attn_masks():
    qi = lax.broadcasted_iota(jnp.int32, (ATTN_BLOCK, ATTN_BLOCK), 0)
    kj = lax.broadcasted_iota(jnp.int32, (ATTN_BLOCK, ATTN_BLOCK), 1)
    return kj >= qi, kj <= qi


def _attn_cols(l, width):
    ncol = width // HEAD_DIM
    cpb = max(1, min(ncol, 32 // (l // ATTN_BLOCK)))
    assert ncol % cpb == 0
    return cpb


def _attn_fwd(q, k, v, name):
    l, width = q.shape
    cpb = _attn_cols(l, width)
    nb = l // ATTN_BLOCK
    scale = HEAD_DIM ** -0.5

    def body(q_ref, k_ref, v_ref, o_ref, lse_ref):
        m_prev, m_cur = _attn_masks()
        for col in range(cpb):
            cs = slice(col * HEAD_DIM, (col + 1) * HEAD_DIM)

            def step(b, carry, cs=cs):
                r0, rp = b * ATTN_BLOCK, max(b - 1, 0) * ATTN_BLOCK
                qb = q_ref[pl.ds(r0, ATTN_BLOCK), cs]
                s_p = lax.dot_general(qb, k_ref[pl.ds(rp, ATTN_BLOCK), cs], _DIMS['nt'], preferred_element_type=F32) * scale
                s_c = lax.dot_general(qb, k_ref[pl.ds(r0, ATTN_BLOCK), cs], _DIMS['nt'], preferred_element_type=F32) * scale
                s_p = jnp.where(jnp.logical_and(m_prev, b > 0), s_p, NEG)
                s_c = jnp.where(m_cur, s_c, NEG)
                m = jnp.maximum(jnp.max(s_p, axis=-1, keepdims=True), jnp.max(s_c, axis=-1, keepdims=True))
                p_p, p_c = jnp.exp(s_p - m), jnp.exp(s_c - m)
                den = jnp.sum(p_p, axis=-1, keepdims=True) + jnp.sum(p_c, axis=-1, keepdims=True)
                acc = jnp.dot(p_p.astype(BF16), v_ref[pl.ds(rp, ATTN_BLOCK), cs], preferred_element_type=F32)
                acc += jnp.dot(p_c.astype(BF16), v_ref[pl.ds(r0, ATTN_BLOCK), cs], preferred_element_type=F32)
                o_ref[pl.ds(r0, ATTN_BLOCK), cs] = acc / den
                lse_ref[pl.ds(r0, ATTN_BLOCK), cs] = jnp.broadcast_to(m + jnp.log(den), (ATTN_BLOCK, HEAD_DIM))
                return carry

            for b in range(nb):
                step(b, 0)

    spec = pl.BlockSpec((l, cpb * HEAD_DIM), lambda j: (0, j))
    return pl.pallas_call(body, grid=(width // (cpb * HEAD_DIM),), in_specs=[spec] * 3, out_specs=[spec] * 2,
                          out_shape=[SDS((l, width), F32)] * 2,
                          compiler_params=_params('parallel'), name=name)(q, k, v)


def _attn_bwd(q, k, v, do, lse, delta, name):
    l, width = q.shape
    cpb = _attn_cols(l, width)
    nb = l // ATTN_BLOCK
    scale = HEAD_DIM ** -0.5

    def body(q_ref, k_ref, v_ref, do_ref, lse_ref, dl_ref, dq_ref, dk_ref, dv_ref):
        m_prev, m_cur = _attn_masks()
        dk_ref[...] = jnp.zeros_like(dk_ref)
        dv_ref[...] = jnp.zeros_like(dv_ref)
        for col in range(cpb):
            cs = slice(col * HEAD_DIM, (col + 1) * HEAD_DIM)

            def step(b, carry, cs=cs):
                r0, rp = b * ATTN_BLOCK, max(b - 1, 0) * ATTN_BLOCK
                qb, dob = q_ref[pl.ds(r0, ATTN_BLOCK), cs], do_ref[pl.ds(r0, ATTN_BLOCK), cs].astype(BF16)
                kp, kc = k_ref[pl.ds(rp, ATTN_BLOCK), cs], k_ref[pl.ds(r0, ATTN_BLOCK), cs]
                vp, vc = v_ref[pl.ds(rp, ATTN_BLOCK), cs], v_ref[pl.ds(r0, ATTN_BLOCK), cs]
                lse_b = lse_ref[pl.ds(r0, ATTN_BLOCK), cs]
                dl_b = dl_ref[pl.ds(r0, ATTN_BLOCK), cs]
                s_p = lax.dot_general(qb, kp, _DIMS['nt'], preferred_element_type=F32) * scale
                s_c = lax.dot_general(qb, kc, _DIMS['nt'], preferred_element_type=F32) * scale
                p_p = jnp.exp(jnp.where(jnp.logical_and(m_prev, b > 0), s_p, NEG) - lse_b)
                p_c = jnp.exp(jnp.where(m_cur, s_c, NEG) - lse_b)
                dp_p = lax.dot_general(dob, vp, _DIMS['nt'], preferred_element_type=F32)
                dp_c = lax.dot_general(dob, vc, _DIMS['nt'], preferred_element_type=F32)
                ds_p = (p_p * (dp_p - dl_b) * scale).astype(BF16)
                ds_c = (p_c * (dp_c - dl_b) * scale).astype(BF16)
                dq_ref[pl.ds(r0, ATTN_BLOCK), cs] = (jnp.dot(ds_p, kp, preferred_element_type=F32)
                                                     + jnp.dot(ds_c, kc, preferred_element_type=F32))
                dk_ref[pl.ds(rp, ATTN_BLOCK), cs] += lax.dot_general(ds_p, qb, _DIMS['tn'], preferred_element_type=F32)
                dk_ref[pl.ds(r0, ATTN_BLOCK), cs] += lax.dot_general(ds_c, qb, _DIMS['tn'], preferred_element_type=F32)
                dv_ref[pl.ds(rp, ATTN_BLOCK), cs] += lax.dot_general(p_p.astype(BF16), dob, _DIMS['tn'], preferred_element_type=F32)
                dv_ref[pl.ds(r0, ATTN_BLOCK), cs] += lax.dot_general(p_c.astype(BF16), dob, _DIMS['tn'], preferred_element_type=F32)
                return carry

            for b in range(nb):
                step(b, 0)

    spec = pl.BlockSpec((l, cpb * HEAD_DIM), lambda j: (0, j))
    return pl.pallas_call(body, grid=(width // (cpb * HEAD_DIM),), in_specs=[spec] * 6, out_specs=[spec] * 3,
                          out_shape=[SDS((l, width), F32)] * 3,
                          compiler_params=_params('parallel'), name=name)(q, k, v, do, lse, delta)


def _attn_combine(os_, lses, dils, name):
    ng = len(dils)
    t = os_[0].shape[0] * dils[0]
    d = os_[0].shape[1] // dils[0]
    tr = _row_tile(t)

    def body(*refs):
        o_refs, l_refs, o_out, lse_out = refs[:ng], refs[ng:2 * ng], refs[2 * ng], refs[2 * ng + 1]
        ovs, ls = [], []
        for g, dil in enumerate(dils):
            ov, lv = _load_dilated(o_refs[g], dil, d), _load_dilated(l_refs[g], dil, d)
            if dil > 1:
                back = _perm(tr, dil, inverse=True)
                ov, lv = _permute(back, ov, 2), _permute(back, lv, 3)
            ovs.append(ov)
            ls.append(lv)
        m = functools.reduce(jnp.maximum, ls)
        ws = [jnp.exp(x - m) for x in ls]
        den = functools.reduce(lambda a, b: a + b, ws)
        acc = functools.reduce(lambda a, b: a + b, [w * o for w, o in zip(ws, ovs)])
        o_out[...] = (acc / den).astype(o_out.dtype)
        lse_out[...] = m + jnp.log(den)

    return pl.pallas_call(body, grid=(t // tr,), in_specs=[_dilated_spec(tr, dil, d) for dil in dils] * 2,
                          out_specs=[_row(tr, d)] * 2, out_shape=[SDS((t, d), BF16), SDS((t, d), F32)],
                          compiler_params=_params('parallel'), name=name)(*os_, *lses)


def _delta_epilogue(acc, o):
    prod = acc * o.astype(F32)
    segs = [jnp.broadcast_to(jnp.sum(prod[:, s:s + HEAD_DIM], axis=-1, keepdims=True), (acc.shape[0], HEAD_DIM))
            for s in range(0, acc.shape[1], HEAD_DIM)]
    return acc, jnp.concatenate(segs, axis=-1)


LRU_TILE = 128


def _lru_gates(xr, wa_ref, ba, wx_ref, bx, lam):
    nb = wa_ref.shape[0]
    xb = xr.astype(BF16)
    ra = jnp.concatenate([jnp.dot(xb[:, n * LRU_BLOCK:(n + 1) * LRU_BLOCK], wa_ref[n], preferred_element_type=F32)
                          for n in range(nb)], axis=-1) + ba
    ia = jnp.concatenate([jnp.dot(xb[:, n * LRU_BLOCK:(n + 1) * LRU_BLOCK], wx_ref[n], preferred_element_type=F32)
                          for n in range(nb)], axis=-1) + bx
    r, ig = _sigmoid(ra), _sigmoid(ia)
    sp = _softplus(-lam)
    log_a = -LRU_C * r * sp
    a = jnp.exp(log_a)
    mult = jnp.sqrt(-_expm1(2.0 * log_a))
    return xb, r, ig, sp, a, mult


def _lru_fwd(z, cw, cb, wa, ba, wx, bx, lam, name):
    t, c2 = z.shape
    c = c2 // 2
    nb = c // LRU_BLOCK
    tr = _row_tile(t, LRU_TILE)

    def body(g_ref, x_ref, xp_ref, cw_ref, cb_ref, wa_ref, ba_ref, wx_ref, bx_ref, lam_ref,
             y_ref, hs_ref, xr_ref, car_ref):
        i = pl.program_id(0)

        @pl.when(i == 0)
        def _():
            car_ref[...] = jnp.zeros_like(car_ref)

        x0 = x_ref[...]
        xp = jnp.where(i > 0, xp_ref[SUB:HALO, :], 0.0)
        cwv = cw_ref[...]
        xr = (cb_ref[...] + cwv[3:4] * x0 + cwv[2:3] * _shift_down(x0, 1, xp)
              + cwv[1:2] * _shift_down(x0, 2, xp) + cwv[0:1] * _shift_down(x0, 3, xp))
        xr_ref[...] = xr
        _, _, ig, _, a, mult = _lru_gates(xr, wa_ref, ba_ref[...], wx_ref, bx_ref[...], lam_ref[...])
        u = mult * (ig * xr)
        row = lax.broadcasted_iota(jnp.int32, (SUB, c), 0)
        car = car_ref[...]
        for j in range(tr // SUB):
            ab, ub = a[j * SUB:(j + 1) * SUB], u[j * SUB:(j + 1) * SUB]
            for s in (1, 2, 4):
                a_sh = jnp.where(row >= s, pltpu.roll(ab, s, 0), 1.0)
                u_sh = jnp.where(row >= s, pltpu.roll(ub, s, 0), 0.0)
                ub = ab * u_sh + ub
                ab = ab * a_sh
            hb = ub + ab * car
            hs_ref[j * SUB:(j + 1) * SUB, :] = hb
            car = jnp.broadcast_to(hb[SUB - 1:SUB], (SUB, c))
        car_ref[...] = car
        gl, _ = _gelu_and_grad(g_ref[...])
        y_ref[...] = (hs_ref[...] * gl).astype(y_ref.dtype)

    return pl.pallas_call(
        body, grid=(t // tr,),
        in_specs=[_row(tr, c, 0), _row(tr, c, 1), _halo_prev(tr, c, 1), _full((4, c)), _full((1, c)),
                  _full((nb, LRU_BLOCK, LRU_BLOCK)), _full((1, c)), _full((nb, LRU_BLOCK, LRU_BLOCK)), _full((1, c)), _full((1, c))],
        out_specs=[_row(tr, c)] * 3,
        out_shape=[SDS((t, c), BF16), SDS((t, c), F32), SDS((t, c), F32)],
        scratch_shapes=[pltpu.VMEM((SUB, c), F32)],
        compiler_params=_params('arbitrary'), name=name)(
            z, z, z, cw, cb.reshape(1, c), wa, ba.reshape(1, c), wx, bx.reshape(1, c), lam.reshape(1, c))


def _lru_bwd(dy, z, xr, hs, cw, wa, ba, wx, bx, lam, name):
    t, c2 = z.shape
    c = c2 // 2
    nb = c // LRU_BLOCK
    tr = _row_tile(t, LRU_TILE)
    nt = t // tr

    def rev(col=0):
        return pl.BlockSpec((tr, c), lambda i, col=col: (nt - 1 - i, col))

    def rev_prev(col=0):
        return pl.BlockSpec((HALO, c), lambda i, col=col: (jnp.maximum((nt - 1 - i) * (tr // HALO) - 1, 0), col))

    def body(dy_ref, g_ref, x_ref, xp_ref, xr_ref, hs_ref, hp_ref, cw_ref, wa_ref, ba_ref, wx_ref, bx_ref, lam_ref,
             dz_ref, dwa_ref, dwx_ref, dvec_ref, lcar_ref, ahead_ref, dxhead_ref, lam_s):
        i = pl.program_id(0)
        first_tile = i == nt - 1

        @pl.when(i == 0)
        def _():
            lcar_ref[...] = jnp.zeros_like(lcar_ref)
            ahead_ref[...] = jnp.zeros_like(ahead_ref)
            dxhead_ref[...] = jnp.zeros_like(dxhead_ref)
            dwa_ref[...] = jnp.zeros_like(dwa_ref)
            dwx_ref[...] = jnp.zeros_like(dwx_ref)
            dvec_ref[...] = jnp.zeros_like(dvec_ref)

        xrv = xr_ref[...]
        lamv = lam_ref[...]
        xb, r, ig, sp, a, mult = _lru_gates(xrv, wa_ref, ba_ref[...], wx_ref, bx_ref[...], lamv)
        hsv = hs_ref[...]
        dyv = dy_ref[...]
        gl, dgl = _gelu_and_grad(g_ref[...])
        dhs = dyv * gl
        dz_ref[:, :c] = (dyv * hsv * dgl).astype(dz_ref.dtype)

        a_next = _shift_up(a, 1, ahead_ref[...])
        row = lax.broadcasted_iota(jnp.int32, (SUB, c), 0)
        car = lcar_ref[...]
        for j in reversed(range(tr // SUB)):
            ab, ub = a_next[j * SUB:(j + 1) * SUB], dhs[j * SUB:(j + 1) * SUB]
            for s in (1, 2, 4):
                a_sh = jnp.where(row < SUB - s, pltpu.roll(ab, SUB - s, 0), 1.0)
                u_sh = jnp.where(row < SUB - s, pltpu.roll(ub, SUB - s, 0), 0.0)
                ub = ab * u_sh + ub
                ab = ab * a_sh
            lb = ub + ab * car
            lam_s[j * SUB:(j + 1) * SUB, :] = lb
            car = jnp.broadcast_to(lb[0:1], (SUB, c))
        lcar_ref[...] = car
        ahead_ref[...] = a[0:SUB]
        lmb = lam_s[...]

        hp = jnp.where(first_tile, 0.0, hp_ref[SUB:HALO, :])
        h_prev = _shift_down(hsv, 1, hp)
        d_a = lmb * h_prev
        d_mult = lmb * (ig * xrv)
        d_ixr = lmb * mult
        d_ig = d_ixr * xrv
        dxr = d_ixr * ig
        d_la = d_a * a - d_mult * (a * a) / mult
        d_r = d_la * (-LRU_C * sp)
        d_sp = jnp.sum(d_la * (-LRU_C * r), axis=0, keepdims=True)
        d_ra = d_r * r * (1.0 - r)
        d_ia = d_ig * ig * (1.0 - ig)
        d_rab, d_iab = d_ra.astype(BF16), d_ia.astype(BF16)
        parts = []
        for n in range(nb):
            cs = slice(n * LRU_BLOCK, (n + 1) * LRU_BLOCK)
            parts.append(lax.dot_general(d_rab[:, cs], wa_ref[n], _DIMS['nt'], preferred_element_type=F32)
                         + lax.dot_general(d_iab[:, cs], wx_ref[n], _DIMS['nt'], preferred_element_type=F32))
            dwa_ref[n] += lax.dot_general(xb[:, cs], d_rab[:, cs], _DIMS['tn'], preferred_element_type=F32)
            dwx_ref[n] += lax.dot_general(xb[:, cs], d_iab[:, cs], _DIMS['tn'], preferred_element_type=F32)
        dxr = dxr + jnp.concatenate(parts, axis=-1)

        cwv = cw_ref[...]
        nxt = dxhead_ref[...]
        dx0 = (cwv[3:4] * dxr + cwv[2:3] * _shift_up(dxr, 1, nxt) + cwv[1:2] * _shift_up(dxr, 2, nxt)
               + cwv[0:1] * _shift_up(dxr, 3, nxt))
        dxhead_ref[...] = dxr[0:SUB]
        dz_ref[:, c:] = dx0.astype(dz_ref.dtype)

        x0 = x_ref[...]
        xp = jnp.where(first_tile, 0.0, xp_ref[SUB:HALO, :])
        sums = [jnp.sum(d_ra, axis=0, keepdims=True), jnp.sum(d_ia, axis=0, keepdims=True),
                d_sp * (-_sigmoid(-lamv)), jnp.sum(dxr, axis=0, keepdims=True),
                jnp.sum(dxr * _shift_down(x0, 3, xp), axis=0, keepdims=True),
                jnp.sum(dxr * _shift_down(x0, 2, xp), axis=0, keepdims=True),
                jnp.sum(dxr * _shift_down(x0, 1, xp), axis=0, keepdims=True),
                jnp.sum(dxr * x0, axis=0, keepdims=True)]
        dvec_ref[...] += jnp.concatenate(sums, axis=0)

    wspec = _full((nb, LRU_BLOCK, LRU_BLOCK))
    return pl.pallas_call(
        body, grid=(nt,),
        in_specs=[rev(), rev(0), rev(1), rev_prev(1), rev(), rev(), rev_prev(), _full((4, c)),
                  wspec, _full((1, c)), wspec, _full((1, c)), _full((1, c))],
        out_specs=[pl.BlockSpec((tr, c2), lambda i: (nt - 1 - i, 0)), wspec, wspec, _full((SUB, c))],
        out_shape=[SDS((t, c2), BF16), SDS((nb, LRU_BLOCK, LRU_BLOCK), F32), SDS((nb, LRU_BLOCK, LRU_BLOCK), F32),
                   SDS((SUB, c), F32)],
        scratch_shapes=[pltpu.VMEM((SUB, c), F32), pltpu.VMEM((SUB, c), F32), pltpu.VMEM((SUB, c), F32),
                        pltpu.VMEM((tr, c), F32)],
        compiler_params=_params('arbitrary'), name=name)(
            dy, z, z, z, xr, hs, hs, cw, wa, ba.reshape(1, c), wx, bx.reshape(1, c), lam.reshape(1, c))


def _local_step(x, p, pos, target, rep, weights_for_layer, emit_grads):
    t, d = x.shape
    depth = p.shape[0]
    w = rep
    half = ROPE_DIM // 2
    invf = ROPE_THETA ** (-2.0 * jnp.arange(half, dtype=F32) / ROPE_DIM)
    invf = jnp.concatenate([invf, invf, jnp.zeros((HEAD_DIM - ROPE_DIM,), F32)]).reshape(1, HEAD_DIM)
    dils = tuple(dil for _, dil in DILATED_PATTERNS)
    saved = []
    h = x
    for i in range(depth):
        kind, j = i % N_MIXERS, i // N_MIXERS
        wl, tok = weights_for_layer(i, 'mixer', h)
        s = {'h0': h, 'wl': wl}
        hn = _rms_fwd(h, w['norm_mix'][i], f'rms_mix_fwd_{i}')
        s['hn'] = hn
        if kind == 0:
            z = _mm(hn, wl['w_in'], 'nn', f'sc_in_{i}', dep=tok)
            y = _sc_fwd(z, wl['small'], f'sc_conv_fwd_{i}')
            h1 = _mm(y, wl['w_out'], 'nn', f'sc_out_{i}', extras=(h,), epi=lambda acc, res: (acc + res,))
            s.update(z=z, y=y)
        elif kind == 1:
            qkv = _mm(hn, wl['w_in'], 'nn', f'attn_qkv_{i}', dep=tok)
            qs, ks, vs = _rope_fwd(qkv, pos, invf, dils, f'rope_fwd_{i}')
            views = list(zip(qs, ks, vs))
            os_, lses = zip(*[_attn_fwd(qg, kg, vg, f'attn_fwd_{i}_g{g}') for g, (qg, kg, vg) in enumerate(views)])
            o, lse = _attn_combine(os_, lses, dils, f'attn_combine_{i}')
            h1 = _mm(o, wl['w_out'], 'nn', f'attn_out_{i}', extras=(h,), epi=lambda acc, res: (acc + res,))
            s.update(views=views, o=o, lse=lse)
        else:
            z = _mm(hn, wl['w_in'], 'nn', f'lru_in_{i}', dep=tok)
            sm = wl['small']
            y, hs, xr = _lru_fwd(z, sm[0:4], sm[4:5], w['lru_w_a'][j], sm[5:6], w['lru_w_x'][j], sm[6:7], sm[7:8],
                                 f'lru_fwd_{i}')
            h1 = _mm(y, wl['w_out'], 'nn', f'lru_out_{i}', extras=(h,), epi=lambda acc, res: (acc + res,))
            s.update(z=z, y=y, hs=hs, xr=xr)
        s['h1'] = h1
        more, tok = weights_for_layer(i, 'mlp', h1)
        wl.update(more)
        hm = _rms_fwd(h1, w['norm_mlp'][i], f'rms_mlp_fwd_{i}')
        u = _mm(hm, wl['mlp_up'], 'nn', f'mlp_up_{i}', out_dtypes=(BF16,), dep=tok)
        h2 = _mm(u, wl['mlp_down'], 'nn', f'mlp_down_{i}', a_pro=_relu2, extras=(h1,), epi=lambda acc, res: (acc + res,))
        hp = _rms_fwd(h2, w['norm_ple'][i], f'rms_ple_fwd_{i}')
        pp = _mm(p[i], wl['ple_proj'], 'nn', f'ple_proj_{i}', out_dtypes=(BF16,))
        h3, gate = _mm(hp, wl['ple_gate'], 'nn', f'ple_gate_{i}', out_dtypes=(F32, BF16), extras=(pp, h2),
                       epi=lambda acc, ppv, res: (res + _sigmoid(acc) * ppv, _sigmoid(acc)))
        s.update(hm=hm, u=u, h2=h2, hp=hp, pp=pp, gate=gate)
        saved.append(s)
        h = h3

    dh, loss, dg_final = _head(h, w['norm_final'], target, 'loss_head')
    grads = {n: [None] * depth for n in ('norm_mix', 'norm_mlp', 'norm_ple')}
    grads['norm_final'] = dg_final.reshape(d)
    started = None
    for i in reversed(range(depth)):
        kind, j = i % N_MIXERS, i // N_MIXERS
        s = saved[i]
        wl, gl = s['wl'], {}
        dpp, dgl = _ple_bwd_gate(dh, s['gate'], s['pp'], f'ple_bwd_gate_{i}')
        gl['ple_proj'] = _mm(p[i], dpp, 'tn', f'ple_dproj_{i}', out_dtypes=(BF16,), dep=started)
        gl['ple_gate'] = _mm(s['hp'], dgl, 'tn', f'ple_dgate_{i}', out_dtypes=(BF16,))
        dhp = _mm(dgl, wl['ple_gate'], 'nt', f'ple_dhp_{i}', out_dtypes=(BF16,))
        dh, dg = _rms_bwd(s['h2'], w['norm_ple'][i], dhp, dh, f'rms_ple_bwd_{i}')
        grads['norm_ple'][i] = dg.reshape(d)
        du = _mm(dh, wl['mlp_down'], 'nt', f'mlp_du_{i}', out_dtypes=(BF16,), extras=(s['u'],),
                 epi=lambda acc, uv: (acc * 2.0 * jnp.maximum(uv.astype(F32), 0.0),))
        gl['mlp_down'] = _mm(s['u'], dh, 'tn', f'mlp_ddown_{i}', out_dtypes=(BF16,), a_pro=_relu2)
        gl['mlp_up'] = _mm(s['hm'], du, 'tn', f'mlp_dup_{i}', out_dtypes=(BF16,), out_stacked=True)
        started = emit_grads(i, 'mlp', gl, loss if i == depth - 1 else None)
        dhm = _mm(du, wl['mlp_up'], 'nt', f'mlp_dhm_{i}', out_dtypes=(BF16,), dep=started)
        dh, dg = _rms_bwd(s['h1'], w['norm_mlp'][i], dhm, dh, f'rms_mlp_bwd_{i}')
        grads['norm_mlp'][i] = dg.reshape(d)
        gl = {}
        if kind == 0:
            dy = _mm(dh, wl['w_out'], 'nt', f'sc_dy_{i}', dep=started)
            gl['w_out'] = _mm(s['y'], dh, 'tn', f'sc_dout_{i}', out_dtypes=(BF16,))
            dz, dwc = _sc_bwd(dy, s['z'], wl['small'], f'sc_conv_bwd_{i}')
            gl['small'] = dwc
            gl['w_in'] = _mm(s['hn'], dz, 'tn', f'sc_din_{i}', out_dtypes=(BF16,))
            started = emit_grads(i, 'mixer', gl)
            dhn = _mm(dz, wl['w_in'], 'nt', f'sc_dhn_{i}', out_dtypes=(BF16,), dep=started)
        elif kind == 1:
            do, delta = _mm(dh, wl['w_out'], 'nt', f'attn_do_{i}', out_dtypes=(BF16, F32), extras=(s['o'],),
                            epi=_delta_epilogue, dep=started)
            gl['w_out'] = _mm(s['o'], dh, 'tn', f'attn_dwo_{i}', out_dtypes=(BF16,))
            rows_in = {1: (do, s['lse'], delta)}
            for dil in dils:
                if dil not in rows_in:
                    rows_in[dil] = _dilate_many([do, s['lse'], delta], dil, (1, 3, 3), (BF16, F32, F32),
                                                f'attn_dilate_{i}_d{dil}')
            dqs, dks, dvs = zip(*[_attn_bwd(*s['views'][g], *rows_in[dil], f'attn_bwd_{i}_g{g}')
                                  for g, dil in enumerate(dils)])
            dqkv = _rope_bwd(dqs, dks, dvs, pos, invf, dils, f'rope_bwd_{i}')
            gl['w_in'] = _mm(s['hn'], dqkv, 'tn', f'attn_dqkv_{i}', out_dtypes=(BF16,), out_stacked=True)
            started = emit_grads(i, 'mixer', gl)
            dhn = _mm(dqkv, wl['w_in'], 'nt', f'attn_dhn_{i}', out_dtypes=(BF16,), dep=started)
        else:
            dy = _mm(dh, wl['w_out'], 'nt', f'lru_dy_{i}', dep=started)
            gl['w_out'] = _mm(s['y'], dh, 'tn', f'lru_dout_{i}', out_dtypes=(BF16,))
            sm = wl['small']
            dz, dwa, dwx, dvec = _lru_bwd(dy, s['z'], s['xr'], s['hs'], sm[0:4], w['lru_w_a'][j], sm[5:6],
                                          w['lru_w_x'][j], sm[6:7], sm[7:8], f'lru_bwd_{i}')
            gl['gates'], gl['small'] = (dwa, dwx), dvec
            gl['w_in'] = _mm(s['hn'], dz, 'tn', f'lru_din_{i}', out_dtypes=(BF16,))
            started = emit_grads(i, 'mixer', gl)
            dhn = _mm(dz, wl['w_in'], 'nt', f'lru_dhn_{i}', out_dtypes=(BF16,), dep=started)
        dh, dg = _rms_bwd(s['h0'], w['norm_mix'][i], dhn, dh, f'rms_mix_bwd_{i}')
        grads['norm_mix'][i] = dg.reshape(d)
        started = None
    return loss, dh, grads


_MESH = pl.DeviceIdType.MESH
_ANY = pl.BlockSpec(memory_space=pl.ANY)


def _block_view(ref, kind, idx):
    if kind == 'stack':
        return ref.at[idx]
    r = ref.shape[0] // N_DEV
    return ref.at[pl.ds(idx * r, r)]


def _gather_many(arrs, kinds, name, after=None):
    n = len(arrs)
    after = [] if after is None else [after]
    out_shapes = [SDS((N_DEV,) + a.shape if kd == 'stack' else (N_DEV * a.shape[0],) + a.shape[1:], a.dtype)
                  for a, kd in zip(arrs, kinds)]

    def body(*refs):
        x_refs, out_refs = refs[:n], refs[n + len(after):2 * n + len(after)]
        send_sems, recv_sems, local_sems = refs[2 * n + len(after):]
        x, y, c = lax.axis_index('x'), lax.axis_index('y'), lax.axis_index('c')
        me, sibling = (x, y, c), (x, y, 1 - c)
        chips = [(1 - x, y), (x, 1 - y), (1 - x, 1 - y)]

        def slab(t, px, py, pc):
            return _block_view(out_refs[t], kinds[t], 4 * px + 2 * py + pc)

        def copy(t, k, block, to, src=None):
            return pltpu.make_async_remote_copy(
                src_ref=slab(t, *block) if src is None else src, dst_ref=slab(t, *block),
                send_sem=send_sems.at[7 * t + k], recv_sem=recv_sems.at[7 * t + k], device_id=to, device_id_type=_MESH)

        mine = [pltpu.make_async_copy(x_refs[t], slab(t, *me), local_sems.at[t]) for t in range(n)]
        for cp in mine:
            cp.start()
        first = [copy(t, 0, me, sibling, src=x_refs[t]) for t in range(n)]
        first += [copy(t, 1 + j, me, (*chip, c), src=x_refs[t]) for j, chip in enumerate(chips) for t in range(n)]
        for cp in first:
            cp.start()
        passed = []
        for j, chip in enumerate(chips):
            for t in range(n):
                copy(t, 1 + j, (*chip, c), me).wait_recv()
                passed.append(copy(t, 4 + j, (*chip, c), sibling))
                passed[-1].start()
        for t in range(n):
            copy(t, 0, sibling, me).wait_recv()
            for j, chip in enumerate(chips):
                copy(t, 4 + j, (*chip, 1 - c), me).wait_recv()
        for cp in first + passed:
            cp.wait_send()
        for cp in mine:
            cp.wait()

    return pl.pallas_call(
        body, out_shape=out_shapes, in_specs=[_ANY] * (n + len(after)), out_specs=[_ANY] * n,
        scratch_shapes=[pltpu.SemaphoreType.DMA((7 * n,)), pltpu.SemaphoreType.DMA((7 * n,)), pltpu.SemaphoreType.DMA((n,))],
        name=name)(*arrs, *after)


_HBM = pl.BlockSpec(memory_space=pltpu.HBM)
_SEM = pl.BlockSpec(memory_space=pltpu.SEMAPHORE)
_EFFECT = pltpu.SideEffectType.DATAFLOW_SIDE_EFFECTING


def _direct_copies(mode, kinds, src_refs, land_refs, send_sems, recv_sems):
    x, y, c = lax.axis_index('x'), lax.axis_index('y'), lax.axis_index('c')
    my_idx = 4 * x + 2 * y + c
    copies = []
    for k in range(1, N_DEV):
        px, py, pc = (1 - x if k & 4 else x, 1 - y if k & 2 else y, 1 - c if k & 1 else c)
        for t, kd in enumerate(kinds):
            if mode == 'gather':
                src, dst = src_refs[t], _block_view(land_refs[t], kd, my_idx)
            else:
                src, dst = _block_view(src_refs[t], kd, 4 * px + 2 * py + pc), land_refs[t].at[my_idx]
            copies.append(pltpu.make_async_remote_copy(
                src_ref=src, dst_ref=dst, send_sem=send_sems.at[7 * t + k - 1], recv_sem=recv_sems.at[7 * t + k - 1],
                device_id=(px, py, pc), device_id_type=_MESH))
    return copies


def _own_part(mode, kind, src, land):
    idx = 4 * lax.axis_index('x') + 2 * lax.axis_index('y') + lax.axis_index('c')
    zeros = (0,) * (src.ndim - 1)
    if mode == 'gather':
        part = src
    elif kind == 'stack':
        part = lax.dynamic_index_in_dim(src, idx, 0, keepdims=False)
    else:
        r = src.shape[0] // N_DEV
        part = lax.dynamic_slice_in_dim(src, idx * r, r, 0)
    if mode == 'gather' and kind == 'rows':
        return lax.dynamic_update_slice(land, part, (idx * part.shape[0],) + zeros)
    return lax.dynamic_update_slice(land, part[None], (idx,) + (0,) * part.ndim)


def _send_start(mode, srcs, kinds, name, after=None):
    n = len(srcs)
    after = [] if after is None else [after]
    lands = []
    for a, kd in zip(srcs, kinds):
        if mode == 'gather':
            shape = (N_DEV,) + a.shape if kd == 'stack' else (N_DEV * a.shape[0],) + a.shape[1:]
        else:
            shape = a.shape if kd == 'stack' else (N_DEV, a.shape[0] // N_DEV) + a.shape[1:]
        lands.append(_own_part(mode, kd, a, lax.empty(shape, a.dtype)))

    def body(*refs):
        src_refs, land_refs = refs[:n], refs[n:2 * n]
        send_sems, recv_sems = refs[2 * n + len(after):2 * n + len(after) + 2]
        token = refs[-1]
        for cp in _direct_copies(mode, kinds, src_refs, land_refs, send_sems, recv_sems):
            cp.start()
        token[...] = jnp.zeros_like(token)

    outs = pl.pallas_call(
        body, name=name,
        out_shape=(pltpu.SemaphoreType.DMA((7 * n,)), pltpu.SemaphoreType.DMA((7 * n,)),
                   *[pltpu.HBM(a.shape, a.dtype) for a in srcs + lands], SDS((SUB, 128), F32)),
        in_specs=[_HBM] * (2 * n) + [_ANY] * len(after),
        out_specs=(_SEM, _SEM, *[_HBM] * (2 * n), pl.BlockSpec(memory_space=pltpu.VMEM)),
        input_output_aliases={i: 2 + i for i in range(2 * n)},
        compiler_params=pltpu.CompilerParams(has_side_effects=_EFFECT),
    )(*[pltpu.with_memory_space_constraint(a, pltpu.HBM) for a in srcs + lands], *after)
    return (outs[0], outs[1], list(outs[2:2 + 2 * n])), outs[-1]


def _send_wait(mode, flight, kinds, after, name):
    send, recv, bufs = flight
    n = len(kinds)

    def body(*refs):
        src_refs, land_refs, (send_sems, recv_sems) = refs[:n], refs[n:2 * n], refs[2 * n:2 * n + 2]
        copies = _direct_copies(mode, kinds, src_refs, land_refs, send_sems, recv_sems)
        for cp in copies:
            cp.wait_send()
        for cp in copies:
            cp.wait_recv()

    outs = pl.pallas_call(
        body, name=name, out_shape=[pltpu.HBM(a.shape, a.dtype) for a in bufs],
        in_specs=[_HBM] * (2 * n) + [_SEM, _SEM, _ANY], out_specs=[_HBM] * (2 * n),
        input_output_aliases={i: i for i in range(2 * n)},
        compiler_params=pltpu.CompilerParams(has_side_effects=_EFFECT),
    )(*bufs, send, recv, after)
    return list(outs[n:])


ADAMW_BLOCK_ELEMS = 128 * 1024


def _adamw_sum(wgt, parts, m, v, name):
    nl, r, c = wgt.shape
    assert len(parts) == nl and all(q.shape == (N_DEV, r, c) for q in parts), (name, wgt.shape, [q.shape for q in parts])
    tr = next((t for t in range(min(r, 512), 0, -16) if r % t == 0 and t * c <= ADAMW_BLOCK_ELEMS and t % 16 == 0), r)
    c1 = 1.0 - ADAM_B1 ** ADAM_STEP
    c2 = 1.0 - ADAM_B2 ** ADAM_STEP

    def body(w_ref, m_ref, v_ref, *rest):
        part_refs, (g_ref, d_ref, mo_ref, vo_ref) = rest[:nl], rest[nl:]
        for q in range(nl):
            @pl.when(pl.program_id(0) == q)
            def _(q=q):
                gv = part_refs[q][0].astype(F32)
                for s in range(1, N_DEV):
                    gv = gv + part_refs[q][s].astype(F32)
                mn = ADAM_B1 * m_ref[...] + (1.0 - ADAM_B1) * gv
                vn = ADAM_B2 * v_ref[...] + (1.0 - ADAM_B2) * (gv * gv)
                g_ref[...] = gv
                d_ref[...] = -ADAM_LR * ((mn / c1) / (jnp.sqrt(vn / c2) + ADAM_EPS) + ADAM_WD * w_ref[...])
                mo_ref[...] = mn
                vo_ref[...] = vn

    spec = pl.BlockSpec((None, tr, c), lambda l, i: (l, i, 0))
    part_specs = [pl.BlockSpec((N_DEV, tr, c), lambda l, i, q=q: (0, jnp.where(l == q, i, 0), 0)) for q in range(nl)]
    return pl.pallas_call(body, grid=(nl, r // tr), in_specs=[spec] * 3 + part_specs, out_specs=[spec] * 4,
                          out_shape=[SDS((nl, r, c), F32)] * 4, compiler_params=_params('arbitrary', 'arbitrary'),
                          name=name)(wgt, m, v, *parts)


MIXER_WEIGHTS = {0: ('sc_w_in', 'sc_w_out'), 1: ('attn_w_qkv', 'attn_w_o'), 2: ('lru_w_in', 'lru_w_out')}
STACKED_OPERANDS = ('attn_w_qkv', 'mlp_w_up')
LRU_SMALL = ('lru_conv_w', 'lru_conv_b', 'lru_b_a', 'lru_b_x', 'lru_lambda')


def _layer_items(i):
    w_in, w_out = MIXER_WEIGHTS[i % N_MIXERS]
    j = i // N_MIXERS
    return [('w_in', w_in, j), ('w_out', w_out, j), ('mlp_up', 'mlp_w_up', i), ('mlp_down', 'mlp_w_down', i),
            ('ple_gate', 'ple_w_gate', i), ('ple_proj', 'ple_w_proj', i)]


def _cols_to_full(stacked):
    return jnp.moveaxis(stacked, 0, 1).reshape(stacked.shape[1], -1)


def _full_to_cols(full):
    k, n = full.shape
    return jnp.moveaxis(full.reshape(k, N_DEV, n // N_DEV), 1, 0)


def _pad_to(a, rows):
    return jnp.pad(a, ((0, rows - a.shape[0]), (0, 0)))


def _small_block(src, i):
    kind, j = i % N_MIXERS, i // N_MIXERS
    if kind == 0:
        return _pad_to(src['sc_w_conv'][j], SUB)
    if kind == 2:
        return jnp.concatenate([src[n][j].reshape(-1, src[n].shape[-1]) for n in LRU_SMALL], axis=0)
    return None


def kernel(x, p, positions, norm_mix, norm_mlp, norm_ple, norm_final, sc_w_in, sc_w_conv, sc_w_out, attn_w_qkv, attn_w_o, lru_w_in, lru_conv_w, lru_conv_b, lru_w_a, lru_b_a, lru_w_x, lru_b_x, lru_lambda, lru_w_out, mlp_w_up, mlp_w_down, ple_w_gate, ple_w_proj, loss_target, m_norm_mix, m_norm_mlp, m_norm_ple, m_norm_final, m_sc_w_in, m_sc_w_conv, m_sc_w_out, m_attn_w_qkv, m_attn_w_o, m_lru_w_in, m_lru_conv_w, m_lru_conv_b, m_lru_w_a, m_lru_b_a, m_lru_w_x, m_lru_b_x, m_lru_lambda, m_lru_w_out, m_mlp_w_up, m_mlp_w_down, m_ple_w_gate, m_ple_w_proj, v_norm_mix, v_norm_mlp, v_norm_ple, v_norm_final, v_sc_w_in, v_sc_w_conv, v_sc_w_out, v_attn_w_qkv, v_attn_w_o, v_lru_w_in, v_lru_conv_w, v_lru_conv_b, v_lru_w_a, v_lru_b_a, v_lru_w_x, v_lru_b_x, v_lru_lambda, v_lru_w_out, v_mlp_w_up, v_mlp_w_down, v_ple_w_gate, v_ple_w_proj):
    loc = dict(locals())
    shards = {n: loc[n] for n in WEIGHTS}
    moms = {n: loc['m_' + n] for n in WEIGHTS}
    vels = {n: loc['v_' + n] for n in WEIGHTS}

    depth, t, d = p.shape[0], x.shape[1], x.shape[2]

    def comm_kind(name):
        return 'stack' if SHARD_AXIS[name] == 2 else 'rows'

    part_keys = {'mlp': ('mlp_up', 'mlp_down', 'ple_gate', 'ple_proj'), 'mixer': ('w_in', 'w_out')}
    halves = [(i, part) for i in range(depth) for part in ('mixer', 'mlp')]

    def half_shards(i, part):
        items = [it for it in _layer_items(i) if it[0] in part_keys[part]]
        arrs = [shards[n][idx].astype(BF16) for _, n, idx in items]
        kinds = [comm_kind(n) for _, n, _ in items]
        small = _small_block(shards, i) if part == 'mixer' else None
        if small is not None:
            arrs.append(small)
            kinds.append('stack')
        return items, arrs, kinds

    def half_weights(i, items, kinds, outs):
        wl = {key: (_cols_to_full(o) if kd == 'stack' and n not in STACKED_OPERANDS else o)
              for (key, n, _), kd, o in zip(items, kinds, outs)}
        if len(outs) > len(items):
            wl['small'] = _cols_to_full(outs[-1])[:shards['sc_w_conv'].shape[1] if i % N_MIXERS == 0 else SUB]
        return wl

    first = [half_shards(0, part) for part in ('mixer', 'mlp')]
    outs0 = _gather_many(first[0][1] + first[1][1], first[0][2] + first[1][2], 'gather_weights_0')
    weights0 = {**half_weights(0, first[0][0], first[0][2], outs0[:len(first[0][1])]),
                **half_weights(0, first[1][0], first[1][2], outs0[len(first[0][1]):])}
    pending = {}

    def start_gather(pos, after):
        if pos >= len(halves):
            return None
        i, part = halves[pos]
        items, arrs, kinds = half_shards(i, part)
        flight, token = _send_start('gather', arrs, kinds, f'gather_weights_start_{part}_{i}', after=after)
        pending[pos] = (items, kinds, flight)
        return token

    first_token = start_gather(2, outs0[0])
    second_token = start_gather(3, first_token)

    def weights_for_layer(i, part, h):
        pos = halves.index((i, part))
        if pos == 0:
            return weights0, second_token
        if pos == 1:
            return {}, None
        items, kinds, flight = pending.pop(pos)
        outs = _send_wait('gather', flight, kinds, h, f'gather_weights_wait_{part}_{i}')
        return half_weights(i, items, kinds, outs), start_gather(pos + 2, outs[0])

    exchanges, gate_gathers, total_loss = {}, {}, []

    def gate_block(src, j):
        return jnp.concatenate([src[n][j].reshape(-1, LRU_BLOCK) for n in ('lru_w_a', 'lru_w_x')], axis=0)

    def emit_grads(i, part, gl, loss=None):
        after = None
        if loss is not None:
            total_loss.append(lax.psum(loss[0, 0], ('x', 'y', 'c')))
            after = jnp.full((SUB, 128), total_loss[0], F32)
        if 'gates' in gl:
            blk = gate_block({'lru_w_a': [gl['gates'][0]], 'lru_w_x': [gl['gates'][1]]}, 0)
            gate_gathers[i] = _send_start('gather', [blk], ['stack'], f'gather_gate_grads_start_{i}')[0]
        items = [it for it in _layer_items(i) if it[0] in part_keys[part]]
        kinds = [comm_kind(n) for _, n, _ in items]
        arrs = [_full_to_cols(gl[key]) if kd == 'stack' and gl[key].ndim == 2 else gl[key]
                for (key, _, _), kd in zip(items, kinds)]
        if part == 'mixer' and i % N_MIXERS == 0:
            arrs.append(_full_to_cols(_pad_to(gl['small'], SUB)))
        elif part == 'mixer' and i % N_MIXERS == 2:
            dv = gl['small']
            arrs.append(_full_to_cols(jnp.concatenate([dv[4:8], dv[3:4], dv[0:1], dv[1:2], dv[2:3]], axis=0)))
        kinds += ['stack'] * (len(arrs) - len(kinds))
        flight, token = _send_start('exchange', arrs, kinds, f'exchange_grads_start_{part}_{i}', after=after)
        exchanges[(i, part)] = (items, kinds, flight)
        return token

    rep = {n: shards[n] for n in ('norm_mix', 'norm_mlp', 'norm_ple', 'norm_final')}
    rep['lru_w_a'], rep['lru_w_x'] = shards['lru_w_a'].astype(BF16), shards['lru_w_x'].astype(BF16)
    loss, grad_x, rgrads = _local_step(x.reshape(t, d), p.reshape(depth, t, p.shape[3]), positions.reshape(t, 1),
                                       loss_target.reshape(t, d), rep, weights_for_layer, emit_grads)

    received, res = {}, {}

    def finish_exchange(key, after):
        items, kinds, flight = exchanges[key]
        outs = _send_wait('exchange', flight, kinds, after, f'exchange_grads_wait_{key[1]}_{key[0]}')
        for (_, n, idx), o in zip(items, outs):
            received[(n, idx)] = o
        if len(outs) > len(items):
            received[('small', key[0])] = outs[-1]

    def big_adamw(names):
        for n in names:
            res[n] = _adamw_sum(shards[n], [received[(n, l)] for l in range(shards[n].shape[0])], moms[n], vels[n],
                                f'adamw_{n}')

    last = (0, 'mixer')
    for key in exchanges:
        if key != last:
            finish_exchange(key, grad_x)
    big = [n for n in WEIGHTS if SHARD_AXIS[n] is not None and shards[n].ndim == 3 and n not in ('sc_w_conv', 'lru_conv_w')]
    late = [n for n in big if n in MIXER_WEIGHTS[0]]
    big_adamw([n for n in big if n not in late])
    finish_exchange(last, jnp.full((SUB, 128), sum(r[0].reshape(-1)[0] for r in res.values()), F32))
    big_adamw(late)


    def small_adamw(layers, name):
        w_, m_, v_ = (jnp.stack([_small_block(src, i) for i in layers]) for src in (shards, moms, vels))
        return _adamw_sum(w_, [received[('small', i)] for i in layers], m_, v_, name)

    sc = small_adamw([i for i in range(depth) if i % N_MIXERS == 0], 'adamw_sc_w_conv')
    res['sc_w_conv'] = tuple(o[:, :shards['sc_w_conv'].shape[1]] for o in sc)
    lru = small_adamw([i for i in range(depth) if i % N_MIXERS == 2], 'adamw_lru_small')
    row = 0
    for n in LRU_SMALL:
        k = shards[n].size // shards[n].shape[0] // shards[n].shape[-1]
        res[n] = tuple(o[:, row:row + k].reshape(shards[n].shape) for o in lru)
        row += k

    def all_updated():
        return jnp.full((SUB, 128), sum(r[0].reshape(-1)[0] for r in res.values()), F32)

    gate_layers = sorted(gate_gathers)
    gate_parts = [_send_wait('gather', gate_gathers[i], ['stack'], all_updated(), f'gather_gate_grads_wait_{i}')[0]
                  for i in gate_layers]
    gate_w, gate_m, gate_v = (jnp.stack([gate_block(src, j) for j in range(len(gate_layers))])
                              for src in (shards, moms, vels))
    gates = _adamw_sum(gate_w, gate_parts, gate_m, gate_v, 'adamw_lru_gates')
    half = gates[0].shape[1] // 2
    res['lru_w_a'] = tuple(o[:, :half].reshape(shards['lru_w_a'].shape) for o in gates)
    res['lru_w_x'] = tuple(o[:, half:].reshape(shards['lru_w_x'].shape) for o in gates)

    norm_names = ('norm_mix', 'norm_mlp', 'norm_ple', 'norm_final')

    def norm_block(src):
        cat = jnp.concatenate([src[n].reshape(-1, d) for n in norm_names], axis=0)
        return _pad_to(cat, -(-cat.shape[0] // HALO) * HALO)

    rfull = {n: (rgrads[n] if n == 'norm_final' else jnp.stack(rgrads[n], axis=0)) for n in norm_names}
    parts_norm, = _gather_many([norm_block(rfull)], ['stack'], 'gather_norm_grads', after=all_updated())
    norms = _adamw_sum(norm_block(shards)[None], [parts_norm], norm_block(moms)[None], norm_block(vels)[None],
                       'adamw_norms')
    row = 0
    for n in norm_names:
        k = shards[n].size // d
        res[n] = tuple(o[0, row:row + k].reshape(shards[n].shape) for o in norms)
        row += k

    return (total_loss[0], grad_x.reshape(x.shape), *[res[n][0] for n in WEIGHTS], *[res[n][1] for n in WEIGHTS],
            *[res[n][2] for n in WEIGHTS], *[res[n][3] for n in WEIGHTS])
```

```python
import functools
import math

import jax
import jax.numpy as jnp
from jax import lax
from jax.experimental import pallas as pl
from jax.experimental.pallas import tpu as pltpu

F32 = jnp.float32
BF16 = jnp.bfloat16
SDS = jax.ShapeDtypeStruct

N_DEV = 8
RMS_EPS = 1e-6
N_MIXERS = 3
HEAD_DIM = 128
DILATED_PATTERNS = ((128, 1), (512, 4), (2048, 16))
ATTN_BLOCK = 128
ROPE_THETA = 500000.0
ROPE_DIM = HEAD_DIM // 4
LRU_BLOCK = 128
LRU_C = 8.0
ADAM_LR, ADAM_B1, ADAM_B2, ADAM_EPS, ADAM_WD, ADAM_STEP = 0.001, 0.9, 0.999, 1e-08, 0.01, 10

HALO = 16
SUB = 8
VMEM_LIMIT = 56 * 1024 * 1024
NEG = -1e30

SHARD_AXIS = {
    'norm_mix': None, 'norm_mlp': None, 'norm_ple': None, 'norm_final': None,
    'sc_w_in': 2, 'sc_w_conv': 2, 'sc_w_out': 1, 'attn_w_qkv': 2, 'attn_w_o': 1,
    'lru_w_in': 2, 'lru_conv_w': 2, 'lru_conv_b': 1, 'lru_w_a': None, 'lru_b_a': 1,
    'lru_w_x': None, 'lru_b_x': 1, 'lru_lambda': 1, 'lru_w_out': 1,
    'mlp_w_up': 2, 'mlp_w_down': 1, 'ple_w_gate': 1, 'ple_w_proj': 2,
}
WEIGHTS = list(SHARD_AXIS)


def _params(*sem):
    return pltpu.CompilerParams(dimension_semantics=sem or None, vmem_limit_bytes=VMEM_LIMIT)


def _row_tile(t, pref=256):
    tr = min(t, pref)
    assert t % tr == 0 and tr % HALO == 0
    return tr


def _row(tr, c, col=0):
    return pl.BlockSpec((tr, c), lambda i, col=col: (i, col))


def _full(shape):
    return pl.BlockSpec(shape, lambda *_: (0,) * len(shape))


def _sigmoid(x):
    return 1.0 / (1.0 + jnp.exp(-x))


def _expm1(x):
    taylor = x * (1.0 + x * (0.5 + x * (1.0 / 6.0 + x * (1.0 / 24.0 + x * (1.0 / 120.0)))))
    return jnp.where(jnp.abs(x) < 0.1, taylor, jnp.exp(x) - 1.0)


def _softplus(x):
    z = jnp.exp(-jnp.abs(x))
    log1p = jnp.where(z < 0.01, z * (1.0 - z * (0.5 - z * (1.0 / 3.0 - z * 0.25))), jnp.log(1.0 + z))
    return jnp.maximum(x, 0.0) + log1p


_GELU_K = math.sqrt(2.0 / math.pi)


def _gelu_and_grad(x):
    inner = _GELU_K * (x + 0.044715 * x * x * x)
    th = jnp.tanh(inner)
    g = 0.5 * x * (1.0 + th)
    dg = 0.5 * (1.0 + th) + 0.5 * x * (1.0 - th * th) * _GELU_K * (1.0 + 3.0 * 0.044715 * x * x)
    return g, dg


def _shift_down(x, k, prev):
    row = lax.broadcasted_iota(jnp.int32, (SUB, x.shape[1]), 0)
    xr = pltpu.roll(x, k, 0)
    top = jnp.where(row < k, pltpu.roll(prev, k, 0), xr[0:SUB])
    return jnp.concatenate([top, xr[SUB:]], axis=0)


def _shift_up(x, k, nxt):
    r = x.shape[0]
    row = lax.broadcasted_iota(jnp.int32, (SUB, x.shape[1]), 0)
    xr = pltpu.roll(x, r - k, 0)
    bot = jnp.where(row >= SUB - k, pltpu.roll(nxt, SUB - k, 0), xr[r - SUB:r])
    return jnp.concatenate([xr[:r - SUB], bot], axis=0)


_DIMS = {'nn': (((1,), (0,)), ((), ())), 'nt': (((1,), (1,)), ((), ())), 'tn': (((0,), (0,)), ((), ()))}


MM_VMEM_BUDGET = 50 * 1024 * 1024
MM_MIN_TK = 1024
MM_MIN_TM = 1024


def _tile_options(dim):
    return [c for c in range(dim, 127, -128) if dim % c == 0] or [dim]


def _choose_tiles(m, n, k, n_span, k_span, a_size, b_size, mn_size, a_temp):
    best = None
    for tm in _tile_options(m):
        for tn in _tile_options(n_span):
            for tk in _tile_options(k_span):
                nk = k // tk
                need = (2 * (tm * tk * a_size + tk * tn * b_size + tm * tn * mn_size) + tm * tn * 4 * (1 + (nk > 1))
                        + tm * tk * 4 * a_temp)
                score = (-min(tk, MM_MIN_TK), -min(tm, MM_MIN_TM), -tm * tn, nk, -min(tm, 2 * MM_MIN_TM), -tn)
                if need <= MM_VMEM_BUDGET and (best is None or score < best[0]):
                    best = (score, (tm, tn, tk))
    return best[1]


def _mm(a, b, dims, name, out_dtypes=(F32,), a_pro=None, extras=(), epi=None, out_stacked=False, dep=None):
    deps = [] if dep is None else [dep]
    stacked = b.ndim == 3
    b_rows, b_cols = (b.shape[1], N_DEV * b.shape[2]) if stacked else b.shape
    if dims == 'nn':
        (m, k), (k2, n) = a.shape, (b_rows, b_cols)
    elif dims == 'nt':
        (m, k), (n, k2) = a.shape, (b_rows, b_cols)
    else:
        (k, m), (k2, n) = a.shape, (b_rows, b_cols)
    assert k == k2, (name, a.shape, b.shape)
    assert not (stacked and dims == 'tn') and not (out_stacked and (extras or dims != 'tn'))
    tm, tn, tk = _choose_tiles(
        m, n, k, n // N_DEV if (out_stacked or (stacked and dims == 'nn')) else n,
        k // N_DEV if (stacked and dims == 'nt') else k, a.dtype.itemsize, b.dtype.itemsize,
        sum(e.dtype.itemsize for e in extras) + sum(jnp.dtype(dt).itemsize for dt in out_dtypes),
        a_pro is not None or a.dtype != BF16)
    assert m % tm == 0 and n % tn == 0 and k % tk == 0, (name, m, n, k)
    nk = k // tk
    a_spec = pl.BlockSpec((tk, tm), lambda i, j, kk: (kk, i)) if dims == 'tn' else pl.BlockSpec((tm, tk), lambda i, j, kk: (i, kk))
    if not stacked:
        b_spec = pl.BlockSpec((tn, tk), lambda i, j, kk: (j, kk)) if dims == 'nt' else pl.BlockSpec((tk, tn), lambda i, j, kk: (kk, j))
    elif dims == 'nn':
        per = b.shape[2] // tn
        b_spec = pl.BlockSpec((None, tk, tn), lambda i, j, kk: (j // per, kk, j % per))
    else:
        per = b.shape[2] // tk
        b_spec = pl.BlockSpec((None, tn, tk), lambda i, j, kk: (kk // per, j, kk % per))
    if out_stacked:
        per_o = n // N_DEV // tn
        o_spec = pl.BlockSpec((None, tm, tn), lambda i, j, kk: (j // per_o, i, j % per_o))
        o_shape = (N_DEV, m, n // N_DEV)
    else:
        o_spec = pl.BlockSpec((tm, tn), lambda i, j, kk: (i, j))
        o_shape = (m, n)
    n_ex, n_out = len(extras), len(out_dtypes)
    for e in extras:
        assert e.shape == (m, n), (name, e.shape)

    def body(a_ref, b_ref, *rest):
        rest = rest[len(deps):]
        ex_refs, out_refs = rest[:n_ex], rest[n_ex:n_ex + n_out]
        kk = pl.program_id(2)
        av = a_ref[...]
        if a_pro is not None:
            av = a_pro(av.astype(F32))
        part = lax.dot_general(av.astype(BF16), b_ref[...].astype(BF16), _DIMS[dims], preferred_element_type=F32)

        def finish(res):
            outs = (res,) if epi is None else epi(res, *[e[...] for e in ex_refs])
            for o_ref, o in zip(out_refs, outs):
                o_ref[...] = o.astype(o_ref.dtype)

        if nk == 1:
            finish(part)
        else:
            acc = rest[-1]

            @pl.when(kk == 0)
            def _():
                acc[...] = part

            @pl.when(kk > 0)
            def _():
                acc[...] += part

            @pl.when(kk == nk - 1)
            def _():
                finish(acc[...])

    out = pl.pallas_call(
        body, grid=(m // tm, n // tn, nk),
        in_specs=[a_spec, b_spec] + [_ANY] * len(deps) + [o_spec] * n_ex,
        out_specs=[o_spec] * n_out,
        out_shape=[SDS(o_shape, d) for d in out_dtypes],
        scratch_shapes=[] if nk == 1 else [pltpu.VMEM((tm, tn), F32)],
        compiler_params=_params('parallel', 'parallel', 'arbitrary'), name=name)(a, b, *deps, *extras)
    return out[0] if n_out == 1 else out


def _relu2(u):
    r = jnp.maximum(u, 0.0)
    return r * r


STREAM_ROWS = 512


def _rms_fwd(h, g, name):
    t, d = h.shape
    tr = _row_tile(t, STREAM_ROWS)

    def body(h_ref, g_ref, o_ref):
        x = h_ref[...]
        r = lax.rsqrt(jnp.mean(x * x, axis=-1, keepdims=True) + RMS_EPS)
        o_ref[...] = (x * r * g_ref[...]).astype(o_ref.dtype)

    return pl.pallas_call(body, grid=(t // tr,), in_specs=[_row(tr, d), _full((1, d))], out_specs=_row(tr, d),
                          out_shape=SDS((t, d), BF16), compiler_params=_params('parallel'), name=name)(h, g.reshape(1, d))


def _rms_bwd(h, g, dhn, dres, name):
    t, d = h.shape
    tr = _row_tile(t, STREAM_ROWS)

    def body(h_ref, g_ref, dhn_ref, dres_ref, dh_ref, dg_ref):
        @pl.when(pl.program_id(0) == 0)
        def _():
            dg_ref[...] = jnp.zeros_like(dg_ref)

        x = h_ref[...]
        r = lax.rsqrt(jnp.mean(x * x, axis=-1, keepdims=True) + RMS_EPS)
        dy = dhn_ref[...].astype(F32)
        gy = dy * g_ref[...]
        dx = r * gy - x * (r * r * r) * jnp.mean(gy * x, axis=-1, keepdims=True)
        dh_ref[...] = dres_ref[...] + dx
        dg_ref[...] += jnp.sum(dy * (x * r), axis=0, keepdims=True)

    return pl.pallas_call(body, grid=(t // tr,),
                          in_specs=[_row(tr, d), _full((1, d)), _row(tr, d), _row(tr, d)],
                          out_specs=[_row(tr, d), _full((1, d))],
                          out_shape=[SDS((t, d), F32), SDS((1, d), F32)],
                          compiler_params=_params('arbitrary'), name=name)(h, g.reshape(1, d), dhn, dres)


def _head(h, g, target, name):
    t, d = h.shape
    tr = _row_tile(t, STREAM_ROWS)

    def body(h_ref, g_ref, t_ref, dh_ref, loss_ref, dg_ref):
        @pl.when(pl.program_id(0) == 0)
        def _():
            dg_ref[...] = jnp.zeros_like(dg_ref)
            loss_ref[...] = jnp.zeros_like(loss_ref)

        x = h_ref[...]
        gv = g_ref[...]
        r = lax.rsqrt(jnp.mean(x * x, axis=-1, keepdims=True) + RMS_EPS)
        xh = x * r
        e = xh * gv - t_ref[...]
        per_tok = jnp.mean(e * e, axis=-1, keepdims=True)
        loss_ref[...] += jnp.broadcast_to(0.5 * jnp.sum(per_tok, axis=0, keepdims=True), loss_ref.shape)
        dy = e * (1.0 / d)
        gy = dy * gv
        dh_ref[...] = r * gy - x * (r * r * r) * jnp.mean(gy * x, axis=-1, keepdims=True)
        dg_ref[...] += jnp.sum(dy * xh, axis=0, keepdims=True)

    return pl.pallas_call(body, grid=(t // tr,),
                          in_specs=[_row(tr, d), _full((1, d)), _row(tr, d)],
                          out_specs=[_row(tr, d), _full((1, 128)), _full((1, d))],
                          out_shape=[SDS((t, d), F32), SDS((1, 128), F32), SDS((1, d), F32)],
                          compiler_params=_params('arbitrary'), name=name)(h, g.reshape(1, d), target)


def _ple_bwd_gate(dh3, gate, pp, name):
    t, d = dh3.shape
    tr = _row_tile(t, STREAM_ROWS)

    def body(dh_ref, g_ref, pp_ref, dpp_ref, dgl_ref):
        dh = dh_ref[...]
        gt = g_ref[...].astype(F32)
        dpp_ref[...] = (dh * gt).astype(dpp_ref.dtype)
        dgl_ref[...] = (dh * pp_ref[...].astype(F32) * gt * (1.0 - gt)).astype(dgl_ref.dtype)

    return pl.pallas_call(body, grid=(t // tr,), in_specs=[_row(tr, d)] * 3, out_specs=[_row(tr, d)] * 2,
                          out_shape=[SDS((t, d), BF16), SDS((t, d), BF16)],
                          compiler_params=_params('parallel'), name=name)(dh3, gate, pp)


def _halo_prev(tr, c, col=0):
    return pl.BlockSpec((HALO, c), lambda i, col=col: (jnp.maximum(i * (tr // HALO) - 1, 0), col))


def _halo_next(tr, c, t, col=0):
    return pl.BlockSpec((HALO, c), lambda i, col=col: (jnp.minimum((i + 1) * (tr // HALO), t // HALO - 1), col))


def _sc_fwd(z, w, name):
    t, c3 = z.shape
    c = c3 // 3
    tr = _row_tile(t)

    def body(z_ref, zp_ref, w_ref, y_ref):
        i = pl.program_id(0)
        zz = z_ref[...]
        gb, cx = zz[:, :c], zz[:, c:2 * c] * zz[:, 2 * c:]
        zp = zp_ref[SUB:HALO, :]
        cxp = jnp.where(i > 0, zp[:, c:2 * c] * zp[:, 2 * c:], 0.0)
        wv = w_ref[...]
        conv = wv[2:3] * cx + wv[1:2] * _shift_down(cx, 1, cxp) + wv[0:1] * _shift_down(cx, 2, cxp)
        y_ref[...] = (gb * conv).astype(y_ref.dtype)

    return pl.pallas_call(body, grid=(t // tr,),
                          in_specs=[_row(tr, c3), _halo_prev(tr, c3), _full((3, c))],
                          out_specs=_row(tr, c), out_shape=SDS((t, c), BF16),
                          compiler_params=_params('parallel'), name=name)(z, z, w)


def _sc_bwd(dy, z, w, name):
    t, c3 = z.shape
    c = c3 // 3
    tr = _row_tile(t)
    nt = t // tr

    def body(dy_ref, dyn_ref, z_ref, zp_ref, zn_ref, w_ref, dz_ref, dw_ref):
        i = pl.program_id(0)

        @pl.when(i == 0)
        def _():
            dw_ref[...] = jnp.zeros_like(dw_ref)

        zz = z_ref[...]
        gb, gc, xi = zz[:, :c], zz[:, c:2 * c], zz[:, 2 * c:]
        cx = gc * xi
        zp = zp_ref[SUB:HALO, :]
        cxp = jnp.where(i > 0, zp[:, c:2 * c] * zp[:, 2 * c:], 0.0)
        wv = w_ref[...]
        cx1, cx2 = _shift_down(cx, 1, cxp), _shift_down(cx, 2, cxp)
        conv = wv[2:3] * cx + wv[1:2] * cx1 + wv[0:1] * cx2
        dyv = dy_ref[...]
        dconv = dyv * gb
        dcn = jnp.where(i < nt - 1, dyn_ref[0:SUB, :] * zn_ref[0:SUB, :c], 0.0)
        dcx = wv[2:3] * dconv + wv[1:2] * _shift_up(dconv, 1, dcn) + wv[0:1] * _shift_up(dconv, 2, dcn)
        dz_ref[:, :c] = (dyv * conv).astype(dz_ref.dtype)
        dz_ref[:, c:2 * c] = (dcx * xi).astype(dz_ref.dtype)
        dz_ref[:, 2 * c:] = (dcx * gc).astype(dz_ref.dtype)
        dw_ref[...] += jnp.concatenate([jnp.sum(dconv * cx2, axis=0, keepdims=True),
                                        jnp.sum(dconv * cx1, axis=0, keepdims=True),
                                        jnp.sum(dconv * cx, axis=0, keepdims=True)], axis=0)

    return pl.pallas_call(body, grid=(nt,),
                          in_specs=[_row(tr, c), _halo_next(tr, c, t), _row(tr, c3), _halo_prev(tr, c3),
                                    _halo_next(tr, c3, t), _full((3, c))],
                          out_specs=[_row(tr, c3), _full((3, c))],
                          out_shape=[SDS((t, c3), BF16), SDS((3, c), F32)],
                          compiler_params=_params('arbitrary'), name=name)(dy, dy, z, z, z, w)


def _perm(tr, dil, inverse=False):
    n = tr // dil
    a = lax.broadcasted_iota(jnp.int32, (tr, tr), 1 if inverse else 0)
    b = lax.broadcasted_iota(jnp.int32, (tr, tr), 0 if inverse else 1)
    return (b == (a % n) * dil + a // n).astype(BF16)


def _permute(pm, x, terms):
    if x.dtype == BF16:
        return jnp.dot(pm, x, preferred_element_type=F32)
    acc = None
    for _ in range(terms):
        part = x.astype(BF16)
        y = jnp.dot(pm, part, preferred_element_type=F32)
        acc = y if acc is None else acc + y
        x = x - part.astype(F32)
    return acc


def _store_dilated(o_ref, y, dil, d):
    n = y.shape[0] // dil
    for rho in range(dil):
        o_ref[:, rho * d:(rho + 1) * d] = y[rho * n:(rho + 1) * n].astype(o_ref.dtype)


def _load_dilated(ref, dil, d):
    return jnp.concatenate([ref[:, rho * d:(rho + 1) * d] for rho in range(dil)], axis=0) if dil > 1 else ref[...]


def _rope_heads(x, lane, cos, sin):
    return jnp.concatenate([_rope_apply(x[:, s:s + HEAD_DIM], lane, cos, sin)
                            for s in range(0, x.shape[1], HEAD_DIM)], axis=1)


def _rope_tables(pos, invf, sign):
    lane = lax.broadcasted_iota(jnp.int32, (pos.shape[0], HEAD_DIM), 1)
    ang = pos.astype(F32) * invf
    half = ROPE_DIM // 2
    cos = jnp.where(lane < ROPE_DIM, jnp.cos(ang), 1.0)
    sin = jnp.sin(ang) * sign
    sin = jnp.where(lane < half, -sin, jnp.where(lane < ROPE_DIM, sin, 0.0))
    return lane, cos, sin


def _rope_apply(x, lane, cos, sin):
    half = ROPE_DIM // 2
    xs = jnp.where(lane < half, pltpu.roll(x, HEAD_DIM - half, 1), pltpu.roll(x, half, 1))
    return x * cos + xs * sin


def _dilated_spec(tr, dil, d):
    return pl.BlockSpec((tr // dil, dil * d), lambda i: (i, 0))


def _rope_fwd(qkv, pos, invf, dils, name):
    t, w3 = qkv.shape
    w, ng = w3 // 3, len(dils)
    d = w // ng
    tr = _row_tile(t)

    def body(q_ref, k_ref, v_ref, pos_ref, invf_ref, *out_refs):
        lane, cos, sin = _rope_tables(pos_ref[...], invf_ref[...], 1.0)
        for g, dil in enumerate(dils):
            cs = slice(g * d, (g + 1) * d)
            vals = [_rope_heads(q_ref[:, cs], lane, cos, sin).astype(BF16),
                    _rope_heads(k_ref[:, cs], lane, cos, sin).astype(BF16), v_ref[:, cs].astype(BF16)]
            if dil > 1:
                pm = _perm(tr, dil)
                vals = [_permute(pm, a, 1) for a in vals]
            for o_ref, a in zip(out_refs[g::ng], vals):
                _store_dilated(o_ref, a, dil, d)

    outs = pl.pallas_call(body, grid=(t // tr,),
                          in_specs=[_row(tr, w, 0), _row(tr, w, 1), _row(tr, w, 2), _row(tr, 1), _full((1, HEAD_DIM))],
                          out_specs=[_dilated_spec(tr, dil, d) for dil in dils] * 3,
                          out_shape=[SDS((t // dil, dil * d), BF16) for dil in dils] * 3,
                          compiler_params=_params('parallel'), name=name)(qkv, qkv, qkv, pos, invf)
    return outs[:ng], outs[ng:2 * ng], outs[2 * ng:]


def _rope_bwd(dqs, dks, dvs, pos, invf, dils, name):
    ng = len(dils)
    t = dqs[0].shape[0] * dils[0]
    d = dqs[0].shape[1] // dils[0]
    w = ng * d
    tr = _row_tile(t)

    def body(*refs):
        dq_refs, dk_refs, dv_refs = refs[:ng], refs[ng:2 * ng], refs[2 * ng:3 * ng]
        pos_ref, invf_ref, o_ref = refs[3 * ng:]
        pos_f = jnp.broadcast_to(pos_ref[...].astype(F32), (tr, HEAD_DIM))
        for g, dil in enumerate(dils):
            pos_g = pos_f if dil == 1 else _permute(_perm(tr, dil), pos_f, 3)
            lane, cos, sin = _rope_tables(pos_g, invf_ref[...], -1.0)
            vals = [_rope_heads(_load_dilated(dq_refs[g], dil, d), lane, cos, sin),
                    _rope_heads(_load_dilated(dk_refs[g], dil, d), lane, cos, sin), _load_dilated(dv_refs[g], dil, d)]
            back = _perm(tr, dil, inverse=True) if dil > 1 else None
            for sec, a in enumerate(vals):
                a = a.astype(BF16)
                if dil > 1:
                    a = _permute(back, a, 1)
                o_ref[:, sec * w + g * d:sec * w + (g + 1) * d] = a.astype(o_ref.dtype)

    return pl.pallas_call(body, grid=(t // tr,),
                          in_specs=[_dilated_spec(tr, dil, d) for dil in dils] * 3 + [_row(tr, 1), _full((1, HEAD_DIM))],
                          out_specs=_row(tr, 3 * w), out_shape=SDS((t, 3 * w), BF16),
                          compiler_params=_params('parallel'), name=name)(*dqs, *dks, *dvs, pos, invf)


def _dilate_many(arrs, dil, terms, out_dtypes, name):
    t, d = arrs[0].shape
    tr = _row_tile(t)
    na = len(arrs)

    def body(*refs):
        pm = _perm(tr, dil)
        for a_ref, o_ref, k in zip(refs[:na], refs[na:], terms):
            _store_dilated(o_ref, _permute(pm, a_ref[...], k), dil, d)

    return pl.pallas_call(body, grid=(t // tr,), in_specs=[_row(tr, d)] * na,
                          out_specs=[_dilated_spec(tr, dil, d)] * na,
                          out_shape=[SDS((t // dil, dil * d), dt) for dt in out_dtypes],
                          compiler_params=_params('parallel'), name=name)(*arrs)


def _attn_masks():
    qi = lax.broadcasted_iota(jnp.int32, (ATTN_BLOCK, ATTN_BLOCK), 0)
    kj = lax.broadcasted_iota(jnp.int32, (ATTN_BLOCK, ATTN_BLOCK), 1)
    return kj >= qi, kj <= qi


def _attn_cols(l, width):
    ncol = width // HEAD_DIM
    cpb = max(1, min(ncol, 32 // (l // ATTN_BLOCK)))
    assert ncol % cpb == 0
    return cpb


def _attn_fwd(q, k, v, name):
    l, width = q.shape
    cpb = _attn_cols(l, width)
    nb = l // ATTN_BLOCK
    scale = HEAD_DIM ** -0.5

    def body(q_ref, k_ref, v_ref, o_ref, lse_ref):
        m_prev, m_cur = _attn_masks()
        for col in range(cpb):
            cs = slice(col * HEAD_DIM, (col + 1) * HEAD_DIM)

            def step(b, carry, cs=cs):
                r0, rp = b * ATTN_BLOCK, max(b - 1, 0) * ATTN_BLOCK
                qb = q_ref[pl.ds(r0, ATTN_BLOCK), cs]
                s_p = lax.dot_general(qb, k_ref[pl.ds(rp, ATTN_BLOCK), cs], _DIMS['nt'], preferred_element_type=F32) * scale
                s_c = lax.dot_general(qb, k_ref[pl.ds(r0, ATTN_BLOCK), cs], _DIMS['nt'], preferred_element_type=F32) * scale
                s_p = jnp.where(jnp.logical_and(m_prev, b > 0), s_p, NEG)
                s_c = jnp.where(m_cur, s_c, NEG)
                m = jnp.maximum(jnp.max(s_p, axis=-1, keepdims=True), jnp.max(s_c, axis=-1, keepdims=True))
                p_p, p_c = jnp.exp(s_p - m), jnp.exp(s_c - m)
                den = jnp.sum(p_p, axis=-1, keepdims=True) + jnp.sum(p_c, axis=-1, keepdims=True)
                acc = jnp.dot(p_p.astype(BF16), v_ref[pl.ds(rp, ATTN_BLOCK), cs], preferred_element_type=F32)
                acc += jnp.dot(p_c.astype(BF16), v_ref[pl.ds(r0, ATTN_BLOCK), cs], preferred_element_type=F32)
                o_ref[pl.ds(r0, ATTN_BLOCK), cs] = acc / den
                lse_ref[pl.ds(r0, ATTN_BLOCK), cs] = jnp.broadcast_to(m + jnp.log(den), (ATTN_BLOCK, HEAD_DIM))
                return carry

            for b in range(nb):
                step(b, 0)

    spec = pl.BlockSpec((l, cpb * HEAD_DIM), lambda j: (0, j))
    return pl.pallas_call(body, grid=(width // (cpb * HEAD_DIM),), in_specs=[spec] * 3, out_specs=[spec] * 2,
                          out_shape=[SDS((l, width), F32)] * 2,
                          compiler_params=_params('parallel'), name=name)(q, k, v)


def _attn_bwd(q, k, v, do, lse, delta, name):
    l, width = q.shape
    cpb = _attn_cols(l, width)
    nb = l // ATTN_BLOCK
    scale = HEAD_DIM ** -0.5

    def body(q_ref, k_ref, v_ref, do_ref, lse_ref, dl_ref, dq_ref, dk_ref, dv_ref):
        m_prev, m_cur = _attn_masks()
        dk_ref[...] = jnp.zeros_like(dk_ref)
        dv_ref[...] = jnp.zeros_like(dv_ref)
        for col in range(cpb):
            cs = slice(col * HEAD_DIM, (col + 1) * HEAD_DIM)

            def step(b, carry, cs=cs):
                r0, rp = b * ATTN_BLOCK, max(b - 1, 0) * ATTN_BLOCK
                qb, dob = q_ref[pl.ds(r0, ATTN_BLOCK), cs], do_ref[pl.ds(r0, ATTN_BLOCK), cs].astype(BF16)
                kp, kc = k_ref[pl.ds(rp, ATTN_BLOCK), cs], k_ref[pl.ds(r0, ATTN_BLOCK), cs]
                vp, vc = v_ref[pl.ds(rp, ATTN_BLOCK), cs], v_ref[pl.ds(r0, ATTN_BLOCK), cs]
                lse_b = lse_ref[pl.ds(r0, ATTN_BLOCK), cs]
                dl_b = dl_ref[pl.ds(r0, ATTN_BLOCK), cs]
                s_p = lax.dot_general(qb, kp, _DIMS['nt'], preferred_element_type=F32) * scale
                s_c = lax.dot_general(qb, kc, _DIMS['nt'], preferred_element_type=F32) * scale
                p_p = jnp.exp(jnp.where(jnp.logical_and(m_prev, b > 0), s_p, NEG) - lse_b)
                p_c = jnp.exp(jnp.where(m_cur, s_c, NEG) - lse_b)
                dp_p = lax.dot_general(dob, vp, _DIMS['nt'], preferred_element_type=F32)
                dp_c = lax.dot_general(dob, vc, _DIMS['nt'], preferred_element_type=F32)
                ds_p = (p_p * (dp_p - dl_b) * scale).astype(BF16)
                ds_c = (p_c * (dp_c - dl_b) * scale).astype(BF16)
                dq_ref[pl.ds(r0, ATTN_BLOCK), cs] = (jnp.dot(ds_p, kp, preferred_element_type=F32)
                                                     + jnp.dot(ds_c, kc, preferred_element_type=F32))
                dk_ref[pl.ds(rp, ATTN_BLOCK), cs] += lax.dot_general(ds_p, qb, _DIMS['tn'], preferred_element_type=F32)
                dk_ref[pl.ds(r0, ATTN_BLOCK), cs] += lax.dot_general(ds_c, qb, _DIMS['tn'], preferred_element_type=F32)
                dv_ref[pl.ds(rp, ATTN_BLOCK), cs] += lax.dot_general(p_p.astype(BF16), dob, _DIMS['tn'], preferred_element_type=F32)
                dv_ref[pl.ds(r0, ATTN_BLOCK), cs] += lax.dot_general(p_c.astype(BF16), dob, _DIMS['tn'], preferred_element_type=F32)
                return carry

            for b in range(nb):
                step(b, 0)

    spec = pl.BlockSpec((l, cpb * HEAD_DIM), lambda j: (0, j))
    return pl.pallas_call(body, grid=(width // (cpb * HEAD_DIM),), in_specs=[spec] * 6, out_specs=[spec] * 3,
                          out_shape=[SDS((l, width), F32)] * 3,
                          compiler_params=_params('parallel'), name=name)(q, k, v, do, lse, delta)


def _attn_combine(os_, lses, dils, name):
    ng = len(dils)
    t = os_[0].shape[0] * dils[0]
    d = os_[0].shape[1] // dils[0]
    tr = _row_tile(t)

    def body(*refs):
        o_refs, l_refs, o_out, lse_out = refs[:ng], refs[ng:2 * ng], refs[2 * ng], refs[2 * ng + 1]
        ovs, ls = [], []
        for g, dil in enumerate(dils):
            ov, lv = _load_dilated(o_refs[g], dil, d), _load_dilated(l_refs[g], dil, d)
            if dil > 1:
                back = _perm(tr, dil, inverse=True)
                ov, lv = _permute(back, ov, 2), _permute(back, lv, 3)
            ovs.append(ov)
            ls.append(lv)
        m = functools.reduce(jnp.maximum, ls)
        ws = [jnp.exp(x - m) for x in ls]
        den = functools.reduce(lambda a, b: a + b, ws)
        acc = functools.reduce(lambda a, b: a + b, [w * o for w, o in zip(ws, ovs)])
        o_out[...] = (acc / den).astype(o_out.dtype)
        lse_out[...] = m + jnp.log(den)

    return pl.pallas_call(body, grid=(t // tr,), in_specs=[_dilated_spec(tr, dil, d) for dil in dils] * 2,
                          out_specs=[_row(tr, d)] * 2, out_shape=[SDS((t, d), BF16), SDS((t, d), F32)],
                          compiler_params=_params('parallel'), name=name)(*os_, *lses)


def _delta_epilogue(acc, o):
    prod = acc * o.astype(F32)
    segs = [jnp.broadcast_to(jnp.sum(prod[:, s:s + HEAD_DIM], axis=-1, keepdims=True), (acc.shape[0], HEAD_DIM))
            for s in range(0, acc.shape[1], HEAD_DIM)]
    return acc, jnp.concatenate(segs, axis=-1)


LRU_TILE = 128


def _lru_gates(xr, wa_ref, ba, wx_ref, bx, lam):
    nb = wa_ref.shape[0]
    xb = xr.astype(BF16)
    ra = jnp.concatenate([jnp.dot(xb[:, n * LRU_BLOCK:(n + 1) * LRU_BLOCK], wa_ref[n], preferred_element_type=F32)
                          for n in range(nb)], axis=-1) + ba
    ia = jnp.concatenate([jnp.dot(xb[:, n * LRU_BLOCK:(n + 1) * LRU_BLOCK], wx_ref[n], preferred_element_type=F32)
                          for n in range(nb)], axis=-1) + bx
    r, ig = _sigmoid(ra), _sigmoid(ia)
    sp = _softplus(-lam)
    log_a = -LRU_C * r * sp
    a = jnp.exp(log_a)
    mult = jnp.sqrt(-_expm1(2.0 * log_a))
    return xb, r, ig, sp, a, mult


def _lru_fwd(z, cw, cb, wa, ba, wx, bx, lam, name):
    t, c2 = z.shape
    c = c2 // 2
    nb = c // LRU_BLOCK
    tr = _row_tile(t, LRU_TILE)

    def body(g_ref, x_ref, xp_ref, cw_ref, cb_ref, wa_ref, ba_ref, wx_ref, bx_ref, lam_ref,
             y_ref, hs_ref, xr_ref, car_ref):
        i = pl.program_id(0)

        @pl.when(i == 0)
        def _():
            car_ref[...] = jnp.zeros_like(car_ref)

        x0 = x_ref[...]
        xp = jnp.where(i > 0, xp_ref[SUB:HALO, :], 0.0)
        cwv = cw_ref[...]
        xr = (cb_ref[...] + cwv[3:4] * x0 + cwv[2:3] * _shift_down(x0, 1, xp)
              + cwv[1:2] * _shift_down(x0, 2, xp) + cwv[0:1] * _shift_down(x0, 3, xp))
        xr_ref[...] = xr
        _, _, ig, _, a, mult = _lru_gates(xr, wa_ref, ba_ref[...], wx_ref, bx_ref[...], lam_ref[...])
        u = mult * (ig * xr)
        row = lax.broadcasted_iota(jnp.int32, (SUB, c), 0)
        car = car_ref[...]
        for j in range(tr // SUB):
            ab, ub = a[j * SUB:(j + 1) * SUB], u[j * SUB:(j + 1) * SUB]
            for s in (1, 2, 4):
                a_sh = jnp.where(row >= s, pltpu.roll(ab, s, 0), 1.0)
                u_sh = jnp.where(row >= s, pltpu.roll(ub, s, 0), 0.0)
                ub = ab * u_sh + ub
                ab = ab * a_sh
            hb = ub + ab * car
            hs_ref[j * SUB:(j + 1) * SUB, :] = hb
            car = jnp.broadcast_to(hb[SUB - 1:SUB], (SUB, c))
        car_ref[...] = car
        gl, _ = _gelu_and_grad(g_ref[...])
        y_ref[...] = (hs_ref[...] * gl).astype(y_ref.dtype)

    return pl.pallas_call(
        body, grid=(t // tr,),
        in_specs=[_row(tr, c, 0), _row(tr, c, 1), _halo_prev(tr, c, 1), _full((4, c)), _full((1, c)),
                  _full((nb, LRU_BLOCK, LRU_BLOCK)), _full((1, c)), _full((nb, LRU_BLOCK, LRU_BLOCK)), _full((1, c)), _full((1, c))],
        out_specs=[_row(tr, c)] * 3,
        out_shape=[SDS((t, c), BF16), SDS((t, c), F32), SDS((t, c), F32)],
        scratch_shapes=[pltpu.VMEM((SUB, c), F32)],
        compiler_params=_params('arbitrary'), name=name)(
            z, z, z, cw, cb.reshape(1, c), wa, ba.reshape(1, c), wx, bx.reshape(1, c), lam.reshape(1, c))


def _lru_bwd(dy, z, xr, hs, cw, wa, ba, wx, bx, lam, name):
    t, c2 = z.shape
    c = c2 // 2
    nb = c // LRU_BLOCK
    tr = _row_tile(t, LRU_TILE)
    nt = t // tr

    def rev(col=0):
        return pl.BlockSpec((tr, c), lambda i, col=col: (nt - 1 - i, col))

    def rev_prev(col=0):
        return pl.BlockSpec((HALO, c), lambda i, col=col: (jnp.maximum((nt - 1 - i) * (tr // HALO) - 1, 0), col))

    def body(dy_ref, g_ref, x_ref, xp_ref, xr_ref, hs_ref, hp_ref, cw_ref, wa_ref, ba_ref, wx_ref, bx_ref, lam_ref,
             dz_ref, dwa_ref, dwx_ref, dvec_ref, lcar_ref, ahead_ref, dxhead_ref, lam_s):
        i = pl.program_id(0)
        first_tile = i == nt - 1

        @pl.when(i == 0)
        def _():
            lcar_ref[...] = jnp.zeros_like(lcar_ref)
            ahead_ref[...] = jnp.zeros_like(ahead_ref)
            dxhead_ref[...] = jnp.zeros_like(dxhead_ref)
            dwa_ref[...] = jnp.zeros_like(dwa_ref)
            dwx_ref[...] = jnp.zeros_like(dwx_ref)
            dvec_ref[...] = jnp.zeros_like(dvec_ref)

        xrv = xr_ref[...]
        lamv = lam_ref[...]
        xb, r, ig, sp, a, mult = _lru_gates(xrv, wa_ref, ba_ref[...], wx_ref, bx_ref[...], lamv)
        hsv = hs_ref[...]
        dyv = dy_ref[...]
        gl, dgl = _gelu_and_grad(g_ref[...])
        dhs = dyv * gl
        dz_ref[:, :c] = (dyv * hsv * dgl).astype(dz_ref.dtype)

        a_next = _shift_up(a, 1, ahead_ref[...])
        row = lax.broadcasted_iota(jnp.int32, (SUB, c), 0)
        car = lcar_ref[...]
        for j in reversed(range(tr // SUB)):
            ab, ub = a_next[j * SUB:(j + 1) * SUB], dhs[j * SUB:(j + 1) * SUB]
            for s in (1, 2, 4):
                a_sh = jnp.where(row < SUB - s, pltpu.roll(ab, SUB - s, 0), 1.0)
                u_sh = jnp.where(row < SUB - s, pltpu.roll(ub, SUB - s, 0), 0.0)
                ub = ab * u_sh + ub
                ab = ab * a_sh
            lb = ub + ab * car
            lam_s[j * SUB:(j + 1) * SUB, :] = lb
            car = jnp.broadcast_to(lb[0:1], (SUB, c))
        lcar_ref[...] = car
        ahead_ref[...] = a[0:SUB]
        lmb = lam_s[...]

        hp = jnp.where(first_tile, 0.0, hp_ref[SUB:HALO, :])
        h_prev = _shift_down(hsv, 1, hp)
        d_a = lmb * h_prev
        d_mult = lmb * (ig * xrv)
        d_ixr = lmb * mult
        d_ig = d_ixr * xrv
        dxr = d_ixr * ig
        d_la = d_a * a - d_mult * (a * a) / mult
        d_r = d_la * (-LRU_C * sp)
        d_sp = jnp.sum(d_la * (-LRU_C * r), axis=0, keepdims=True)
        d_ra = d_r * r * (1.0 - r)
        d_ia = d_ig * ig * (1.0 - ig)
        d_rab, d_iab = d_ra.astype(BF16), d_ia.astype(BF16)
        parts = []
        for n in range(nb):
            cs = slice(n * LRU_BLOCK, (n + 1) * LRU_BLOCK)
            parts.append(lax.dot_general(d_rab[:, cs], wa_ref[n], _DIMS['nt'], preferred_element_type=F32)
                         + lax.dot_general(d_iab[:, cs], wx_ref[n], _DIMS['nt'], preferred_element_type=F32))
            dwa_ref[n] += lax.dot_general(xb[:, cs], d_rab[:, cs], _DIMS['tn'], preferred_element_type=F32)
            dwx_ref[n] += lax.dot_general(xb[:, cs], d_iab[:, cs], _DIMS['tn'], preferred_element_type=F32)
        dxr = dxr + jnp.concatenate(parts, axis=-1)

        cwv = cw_ref[...]
        nxt = dxhead_ref[...]
        dx0 = (cwv[3:4] * dxr + cwv[2:3] * _shift_up(dxr, 1, nxt) + cwv[1:2] * _shift_up(dxr, 2, nxt)
               + cwv[0:1] * _shift_up(dxr, 3, nxt))
        dxhead_ref[...] = dxr[0:SUB]
        dz_ref[:, c:] = dx0.astype(dz_ref.dtype)

        x0 = x_ref[...]
        xp = jnp.where(first_tile, 0.0, xp_ref[SUB:HALO, :])
        sums = [jnp.sum(d_ra, axis=0, keepdims=True), jnp.sum(d_ia, axis=0, keepdims=True),
                d_sp * (-_sigmoid(-lamv)), jnp.sum(dxr, axis=0, keepdims=True),
                jnp.sum(dxr * _shift_down(x0, 3, xp), axis=0, keepdims=True),
                jnp.sum(dxr * _shift_down(x0, 2, xp), axis=0, keepdims=True),
                jnp.sum(dxr * _shift_down(x0, 1, xp), axis=0, keepdims=True),
                jnp.sum(dxr * x0, axis=0, keepdims=True)]
        dvec_ref[...] += jnp.concatenate(sums, axis=0)

    wspec = _full((nb, LRU_BLOCK, LRU_BLOCK))
    return pl.pallas_call(
        body, grid=(nt,),
        in_specs=[rev(), rev(0), rev(1), rev_prev(1), rev(), rev(), rev_prev(), _full((4, c)),
                  wspec, _full((1, c)), wspec, _full((1, c)), _full((1, c))],
        out_specs=[pl.BlockSpec((tr, c2), lambda i: (nt - 1 - i, 0)), wspec, wspec, _full((SUB, c))],
        out_shape=[SDS((t, c2), BF16), SDS((nb, LRU_BLOCK, LRU_BLOCK), F32), SDS((nb, LRU_BLOCK, LRU_BLOCK), F32),
                   SDS((SUB, c), F32)],
        scratch_shapes=[pltpu.VMEM((SUB, c), F32), pltpu.VMEM((SUB, c), F32), pltpu.VMEM((SUB, c), F32),
                        pltpu.VMEM((tr, c), F32)],
        compiler_params=_params('arbitrary'), name=name)(
            dy, z, z, z, xr, hs, hs, cw, wa, ba.reshape(1, c), wx, bx.reshape(1, c), lam.reshape(1, c))


def _local_step(x, p, pos, target, rep, weights_for_layer, emit_grads):
    t, d = x.shape
    depth = p.shape[0]
    w = rep
    half = ROPE_DIM // 2
    invf = ROPE_THETA ** (-2.0 * jnp.arange(half, dtype=F32) / ROPE_DIM)
    invf = jnp.concatenate([invf, invf, jnp.zeros((HEAD_DIM - ROPE_DIM,), F32)]).reshape(1, HEAD_DIM)
    dils = tuple(dil for _, dil in DILATED_PATTERNS)
    saved = []
    h = x
    for i in range(depth):
        kind, j = i % N_MIXERS, i // N_MIXERS
        wl, tok = weights_for_layer(i, 'mixer', h)
        s = {'h0': h, 'wl': wl}
        hn = _rms_fwd(h, w['norm_mix'][i], f'rms_mix_fwd_{i}')
        s['hn'] = hn
        if kind == 0:
            z = _mm(hn, wl['w_in'], 'nn', f'sc_in_{i}', dep=tok)
            y = _sc_fwd(z, wl['small'], f'sc_conv_fwd_{i}')
            h1 = _mm(y, wl['w_out'], 'nn', f'sc_out_{i}', extras=(h,), epi=lambda acc, res: (acc + res,))
            s.update(z=z, y=y)
        elif kind == 1:
            qkv = _mm(hn, wl['w_in'], 'nn', f'attn_qkv_{i}', dep=tok)
            qs, ks, vs = _rope_fwd(qkv, pos, invf, dils, f'rope_fwd_{i}')
            views = list(zip(qs, ks, vs))
            os_, lses = zip(*[_attn_fwd(qg, kg, vg, f'attn_fwd_{i}_g{g}') for g, (qg, kg, vg) in enumerate(views)])
            o, lse = _attn_combine(os_, lses, dils, f'attn_combine_{i}')
            h1 = _mm(o, wl['w_out'], 'nn', f'attn_out_{i}', extras=(h,), epi=lambda acc, res: (acc + res,))
            s.update(views=views, o=o, lse=lse)
        else:
            z = _mm(hn, wl['w_in'], 'nn', f'lru_in_{i}', dep=tok)
            sm = wl['small']
            y, hs, xr = _lru_fwd(z, sm[0:4], sm[4:5], w['lru_w_a'][j], sm[5:6], w['lru_w_x'][j], sm[6:7], sm[7:8],
                                 f'lru_fwd_{i}')
            h1 = _mm(y, wl['w_out'], 'nn', f'lru_out_{i}', extras=(h,), epi=lambda acc, res: (acc + res,))
            s.update(z=z, y=y, hs=hs, xr=xr)
        s['h1'] = h1
        more, tok = weights_for_layer(i, 'mlp', h1)
        wl.update(more)
        hm = _rms_fwd(h1, w['norm_mlp'][i], f'rms_mlp_fwd_{i}')
        u = _mm(hm, wl['mlp_up'], 'nn', f'mlp_up_{i}', out_dtypes=(BF16,), dep=tok)
        h2 = _mm(u, wl['mlp_down'], 'nn', f'mlp_down_{i}', a_pro=_relu2, extras=(h1,), epi=lambda acc, res: (acc + res,))
        hp = _rms_fwd(h2, w['norm_ple'][i], f'rms_ple_fwd_{i}')
        pp = _mm(p[i], wl['ple_proj'], 'nn', f'ple_proj_{i}', out_dtypes=(BF16,))
        h3, gate = _mm(hp, wl['ple_gate'], 'nn', f'ple_gate_{i}', out_dtypes=(F32, BF16), extras=(pp, h2),
                       epi=lambda acc, ppv, res: (res + _sigmoid(acc) * ppv, _sigmoid(acc)))
        s.update(hm=hm, u=u, h2=h2, hp=hp, pp=pp, gate=gate)
        saved.append(s)
        h = h3

    dh, loss, dg_final = _head(h, w['norm_final'], target, 'loss_head')
    grads = {n: [None] * depth for n in ('norm_mix', 'norm_mlp', 'norm_ple')}
    grads['norm_final'] = dg_final.reshape(d)
    started = None
    for i in reversed(range(depth)):
        kind, j = i % N_MIXERS, i // N_MIXERS
        s = saved[i]
        wl, gl = s['wl'], {}
        dpp, dgl = _ple_bwd_gate(dh, s['gate'], s['pp'], f'ple_bwd_gate_{i}')
        gl['ple_proj'] = _mm(p[i], dpp, 'tn', f'ple_dproj_{i}', out_dtypes=(BF16,), dep=started)
        gl['ple_gate'] = _mm(s['hp'], dgl, 'tn', f'ple_dgate_{i}', out_dtypes=(BF16,))
        dhp = _mm(dgl, wl['ple_gate'], 'nt', f'ple_dhp_{i}', out_dtypes=(BF16,))
        dh, dg = _rms_bwd(s['h2'], w['norm_ple'][i], dhp, dh, f'rms_ple_bwd_{i}')
        grads['norm_ple'][i] = dg.reshape(d)
        du = _mm(dh, wl['mlp_down'], 'nt', f'mlp_du_{i}', out_dtypes=(BF16,), extras=(s['u'],),
                 epi=lambda acc, uv: (acc * 2.0 * jnp.maximum(uv.astype(F32), 0.0),))
        gl['mlp_down'] = _mm(s['u'], dh, 'tn', f'mlp_ddown_{i}', out_dtypes=(BF16,), a_pro=_relu2)
        gl['mlp_up'] = _mm(s['hm'], du, 'tn', f'mlp_dup_{i}', out_dtypes=(BF16,), out_stacked=True)
        started = emit_grads(i, 'mlp', gl, loss if i == depth - 1 else None)
        dhm = _mm(du, wl['mlp_up'], 'nt', f'mlp_dhm_{i}', out_dtypes=(BF16,), dep=started)
        dh, dg = _rms_bwd(s['h1'], w['norm_mlp'][i], dhm, dh, f'rms_mlp_bwd_{i}')
        grads['norm_mlp'][i] = dg.reshape(d)
        gl = {}
        if kind == 0:
            dy = _mm(dh, wl['w_out'], 'nt', f'sc_dy_{i}', dep=started)
            gl['w_out'] = _mm(s['y'], dh, 'tn', f'sc_dout_{i}', out_dtypes=(BF16,))
            dz, dwc = _sc_bwd(dy, s['z'], wl['small'], f'sc_conv_bwd_{i}')
            gl['small'] = dwc
            gl['w_in'] = _mm(s['hn'], dz, 'tn', f'sc_din_{i}', out_dtypes=(BF16,))
            started = emit_grads(i, 'mixer', gl)
            dhn = _mm(dz, wl['w_in'], 'nt', f'sc_dhn_{i}', out_dtypes=(BF16,), dep=started)
        elif kind == 1:
            do, delta = _mm(dh, wl['w_out'], 'nt', f'attn_do_{i}', out_dtypes=(BF16, F32), extras=(s['o'],),
                            epi=_delta_epilogue, dep=started)
            gl['w_out'] = _mm(s['o'], dh, 'tn', f'attn_dwo_{i}', out_dtypes=(BF16,))
            rows_in = {1: (do, s['lse'], delta)}
            for dil in dils:
                if dil not in rows_in:
                    rows_in[dil] = _dilate_many([do, s['lse'], delta], dil, (1, 3, 3), (BF16, F32, F32),
                                                f'attn_dilate_{i}_d{dil}')
            dqs, dks, dvs = zip(*[_attn_bwd(*s['views'][g], *rows_in[dil], f'attn_bwd_{i}_g{g}')
                                  for g, dil in enumerate(dils)])
            dqkv = _rope_bwd(dqs, dks, dvs, pos, invf, dils, f'rope_bwd_{i}')
            gl['w_in'] = _mm(s['hn'], dqkv, 'tn', f'attn_dqkv_{i}', out_dtypes=(BF16,), out_stacked=True)
            started = emit_grads(i, 'mixer', gl)
            dhn = _mm(dqkv, wl['w_in'], 'nt', f'attn_dhn_{i}', out_dtypes=(BF16,), dep=started)
        else:
            dy = _mm(dh, wl['w_out'], 'nt', f'lru_dy_{i}', dep=started)
            gl['w_out'] = _mm(s['y'], dh, 'tn', f'lru_dout_{i}', out_dtypes=(BF16,))
            sm = wl['small']
            dz, dwa, dwx, dvec = _lru_bwd(dy, s['z'], s['xr'], s['hs'], sm[0:4], w['lru_w_a'][j], sm[5:6],
                                          w['lru_w_x'][j], sm[6:7], sm[7:8], f'lru_bwd_{i}')
            gl['gates'], gl['small'] = (dwa, dwx), dvec
            gl['w_in'] = _mm(s['hn'], dz, 'tn', f'lru_din_{i}', out_dtypes=(BF16,))
            started = emit_grads(i, 'mixer', gl)
            dhn = _mm(dz, wl['w_in'], 'nt', f'lru_dhn_{i}', out_dtypes=(BF16,), dep=started)
        dh, dg = _rms_bwd(s['h0'], w['norm_mix'][i], dhn, dh, f'rms_mix_bwd_{i}')
        grads['norm_mix'][i] = dg.reshape(d)
        started = None
    return loss, dh, grads


_MESH = pl.DeviceIdType.MESH
_ANY = pl.BlockSpec(memory_space=pl.ANY)


def _block_view(ref, kind, idx):
    if kind == 'stack':
        return ref.at[idx]
    r = ref.shape[0] // N_DEV
    return ref.at[pl.ds(idx * r, r)]


def _gather_many(arrs, kinds, name, after=None):
    n = len(arrs)
    after = [] if after is None else [after]
    out_shapes = [SDS((N_DEV,) + a.shape if kd == 'stack' else (N_DEV * a.shape[0],) + a.shape[1:], a.dtype)
                  for a, kd in zip(arrs, kinds)]

    def body(*refs):
        x_refs, out_refs = refs[:n], refs[n + len(after):2 * n + len(after)]
        send_sems, recv_sems, local_sems = refs[2 * n + len(after):]
        x, y, c = lax.axis_index('x'), lax.axis_index('y'), lax.axis_index('c')
        me, sibling = (x, y, c), (x, y, 1 - c)
        chips = [(1 - x, y), (x, 1 - y), (1 - x, 1 - y)]

        def slab(t, px, py, pc):
            return _block_view(out_refs[t], kinds[t], 4 * px + 2 * py + pc)

        def copy(t, k, block, to, src=None):
            return pltpu.make_async_remote_copy(
                src_ref=slab(t, *block) if src is None else src, dst_ref=slab(t, *block),
                send_sem=send_sems.at[7 * t + k], recv_sem=recv_sems.at[7 * t + k], device_id=to, device_id_type=_MESH)

        mine = [pltpu.make_async_copy(x_refs[t], slab(t, *me), local_sems.at[t]) for t in range(n)]
        for cp in mine:
            cp.start()
        first = [copy(t, 0, me, sibling, src=x_refs[t]) for t in range(n)]
        first += [copy(t, 1 + j, me, (*chip, c), src=x_refs[t]) for j, chip in enumerate(chips) for t in range(n)]
        for cp in first:
            cp.start()
        passed = []
        for j, chip in enumerate(chips):
            for t in range(n):
                copy(t, 1 + j, (*chip, c), me).wait_recv()
                passed.append(copy(t, 4 + j, (*chip, c), sibling))
                passed[-1].start()
        for t in range(n):
            copy(t, 0, sibling, me).wait_recv()
            for j, chip in enumerate(chips):
                copy(t, 4 + j, (*chip, 1 - c), me).wait_recv()
        for cp in first + passed:
            cp.wait_send()
        for cp in mine:
            cp.wait()

    return pl.pallas_call(
        body, out_shape=out_shapes, in_specs=[_ANY] * (n + len(after)), out_specs=[_ANY] * n,
        scratch_shapes=[pltpu.SemaphoreType.DMA((7 * n,)), pltpu.SemaphoreType.DMA((7 * n,)), pltpu.SemaphoreType.DMA((n,))],
        name=name)(*arrs, *after)


_HBM = pl.BlockSpec(memory_space=pltpu.HBM)
_SEM = pl.BlockSpec(memory_space=pltpu.SEMAPHORE)
_EFFECT = pltpu.SideEffectType.DATAFLOW_SIDE_EFFECTING


def _direct_copies(mode, kinds, src_refs, land_refs, send_sems, recv_sems):
    x, y, c = lax.axis_index('x'), lax.axis_index('y'), lax.axis_index('c')
    my_idx = 4 * x + 2 * y + c
    copies = []
    for k in range(1, N_DEV):
        px, py, pc = (1 - x if k & 4 else x, 1 - y if k & 2 else y, 1 - c if k & 1 else c)
        for t, kd in enumerate(kinds):
            if mode == 'gather':
                src, dst = src_refs[t], _block_view(land_refs[t], kd, my_idx)
            else:
                src, dst = _block_view(src_refs[t], kd, 4 * px + 2 * py + pc), land_refs[t].at[my_idx]
            copies.append(pltpu.make_async_remote_copy(
                src_ref=src, dst_ref=dst, send_sem=send_sems.at[7 * t + k - 1], recv_sem=recv_sems.at[7 * t + k - 1],
                device_id=(px, py, pc), device_id_type=_MESH))
    return copies


def _own_part(mode, kind, src, land):
    idx = 4 * lax.axis_index('x') + 2 * lax.axis_index('y') + lax.axis_index('c')
    zeros = (0,) * (src.ndim - 1)
    if mode == 'gather':
        part = src
    elif kind == 'stack':
        part = lax.dynamic_index_in_dim(src, idx, 0, keepdims=False)
    else:
        r = src.shape[0] // N_DEV
        part = lax.dynamic_slice_in_dim(src, idx * r, r, 0)
    if mode == 'gather' and kind == 'rows':
        return lax.dynamic_update_slice(land, part, (idx * part.shape[0],) + zeros)
    return lax.dynamic_update_slice(land, part[None], (idx,) + (0,) * part.ndim)


def _send_start(mode, srcs, kinds, name, after=None):
    n = len(srcs)
    after = [] if after is None else [after]
    lands = []
    for a, kd in zip(srcs, kinds):
        if mode == 'gather':
            shape = (N_DEV,) + a.shape if kd == 'stack' else (N_DEV * a.shape[0],) + a.shape[1:]
        else:
            shape = a.shape if kd == 'stack' else (N_DEV, a.shape[0] // N_DEV) + a.shape[1:]
        lands.append(_own_part(mode, kd, a, lax.empty(shape, a.dtype)))

    def body(*refs):
        src_refs, land_refs = refs[:n], refs[n:2 * n]
        send_sems, recv_sems = refs[2 * n + len(after):2 * n + len(after) + 2]
        token = refs[-1]
        for cp in _direct_copies(mode, kinds, src_refs, land_refs, send_sems, recv_sems):
            cp.start()
        token[...] = jnp.zeros_like(token)

    outs = pl.pallas_call(
        body, name=name,
        out_shape=(pltpu.SemaphoreType.DMA((7 * n,)), pltpu.SemaphoreType.DMA((7 * n,)),
                   *[pltpu.HBM(a.shape, a.dtype) for a in srcs + lands], SDS((SUB, 128), F32)),
        in_specs=[_HBM] * (2 * n) + [_ANY] * len(after),
        out_specs=(_SEM, _SEM, *[_HBM] * (2 * n), pl.BlockSpec(memory_space=pltpu.VMEM)),
        input_output_aliases={i: 2 + i for i in range(2 * n)},
        compiler_params=pltpu.CompilerParams(has_side_effects=_EFFECT),
    )(*[pltpu.with_memory_space_constraint(a, pltpu.HBM) for a in srcs + lands], *after)
    return (outs[0], outs[1], list(outs[2:2 + 2 * n])), outs[-1]


def _send_wait(mode, flight, kinds, after, name):
    send, recv, bufs = flight
    n = len(kinds)

    def body(*refs):
        src_refs, land_refs, (send_sems, recv_sems) = refs[:n], refs[n:2 * n], refs[2 * n:2 * n + 2]
        copies = _direct_copies(mode, kinds, src_refs, land_refs, send_sems, recv_sems)
        for cp in copies:
            cp.wait_send()
        for cp in copies:
            cp.wait_recv()

    outs = pl.pallas_call(
        body, name=name, out_shape=[pltpu.HBM(a.shape, a.dtype) for a in bufs],
        in_specs=[_HBM] * (2 * n) + [_SEM, _SEM, _ANY], out_specs=[_HBM] * (2 * n),
        input_output_aliases={i: i for i in range(2 * n)},
        compiler_params=pltpu.CompilerParams(has_side_effects=_EFFECT),
    )(*bufs, send, recv, after)
    return list(outs[n:])


ADAMW_BLOCK_ELEMS = 256 * 1024


def _adamw_sum(wgt, parts, m, v, name):
    nl, r, c = wgt.shape
    assert len(parts) == nl and all(q.shape == (N_DEV, r, c) for q in parts), (name, wgt.shape, [q.shape for q in parts])
    tr = next((t for t in range(min(r, 512), 0, -16) if r % t == 0 and t * c <= ADAMW_BLOCK_ELEMS and t % 16 == 0), r)
    c1 = 1.0 - ADAM_B1 ** ADAM_STEP
    c2 = 1.0 - ADAM_B2 ** ADAM_STEP

    def body(w_ref, m_ref, v_ref, *rest):
        part_refs, (g_ref, d_ref, mo_ref, vo_ref) = rest[:nl], rest[nl:]
        for q in range(nl):
            @pl.when(pl.program_id(0) == q)
            def _(q=q):
                gv = part_refs[q][0].astype(F32)
                for s in range(1, N_DEV):
                    gv = gv + part_refs[q][s].astype(F32)
                mn = ADAM_B1 * m_ref[...] + (1.0 - ADAM_B1) * gv
                vn = ADAM_B2 * v_ref[...] + (1.0 - ADAM_B2) * (gv * gv)
                g_ref[...] = gv
                d_ref[...] = -ADAM_LR * ((mn / c1) / (jnp.sqrt(vn / c2) + ADAM_EPS) + ADAM_WD * w_ref[...])
                mo_ref[...] = mn
                vo_ref[...] = vn

    spec = pl.BlockSpec((None, tr, c), lambda l, i: (l, i, 0))
    part_specs = [pl.BlockSpec((N_DEV, tr, c), lambda l, i, q=q: (0, jnp.where(l == q, i, 0), 0)) for q in range(nl)]
    return pl.pallas_call(body, grid=(nl, r // tr), in_specs=[spec] * 3 + part_specs, out_specs=[spec] * 4,
                          out_shape=[SDS((nl, r, c), F32)] * 4, compiler_params=_params('arbitrary', 'arbitrary'),
                          name=name)(wgt, m, v, *parts)


MIXER_WEIGHTS = {0: ('sc_w_in', 'sc_w_out'), 1: ('attn_w_qkv', 'attn_w_o'), 2: ('lru_w_in', 'lru_w_out')}
STACKED_OPERANDS = ('attn_w_qkv', 'mlp_w_up')
LRU_SMALL = ('lru_conv_w', 'lru_conv_b', 'lru_b_a', 'lru_b_x', 'lru_lambda')


def _layer_items(i):
    w_in, w_out = MIXER_WEIGHTS[i % N_MIXERS]
    j = i // N_MIXERS
    return [('w_in', w_in, j), ('w_out', w_out, j), ('mlp_up', 'mlp_w_up', i), ('mlp_down', 'mlp_w_down', i),
            ('ple_gate', 'ple_w_gate', i), ('ple_proj', 'ple_w_proj', i)]


def _cols_to_full(stacked):
    return jnp.moveaxis(stacked, 0, 1).reshape(stacked.shape[1], -1)


def _full_to_cols(full):
    k, n = full.shape
    return jnp.moveaxis(full.reshape(k, N_DEV, n // N_DEV), 1, 0)


def _pad_to(a, rows):
    return jnp.pad(a, ((0, rows - a.shape[0]), (0, 0)))


def _small_block(src, i):
    kind, j = i % N_MIXERS, i // N_MIXERS
    if kind == 0:
        return _pad_to(src['sc_w_conv'][j], SUB)
    if kind == 2:
        return jnp.concatenate([src[n][j].reshape(-1, src[n].shape[-1]) for n in LRU_SMALL], axis=0)
    return None


def kernel(x, p, positions, norm_mix, norm_mlp, norm_ple, norm_final, sc_w_in, sc_w_conv, sc_w_out, attn_w_qkv, attn_w_o, lru_w_in, lru_conv_w, lru_conv_b, lru_w_a, lru_b_a, lru_w_x, lru_b_x, lru_lambda, lru_w_out, mlp_w_up, mlp_w_down, ple_w_gate, ple_w_proj, loss_target, m_norm_mix, m_norm_mlp, m_norm_ple, m_norm_final, m_sc_w_in, m_sc_w_conv, m_sc_w_out, m_attn_w_qkv, m_attn_w_o, m_lru_w_in, m_lru_conv_w, m_lru_conv_b, m_lru_w_a, m_lru_b_a, m_lru_w_x, m_lru_b_x, m_lru_lambda, m_lru_w_out, m_mlp_w_up, m_mlp_w_down, m_ple_w_gate, m_ple_w_proj, v_norm_mix, v_norm_mlp, v_norm_ple, v_norm_final, v_sc_w_in, v_sc_w_conv, v_sc_w_out, v_attn_w_qkv, v_attn_w_o, v_lru_w_in, v_lru_conv_w, v_lru_conv_b, v_lru_w_a, v_lru_b_a, v_lru_w_x, v_lru_b_x, v_lru_lambda, v_lru_w_out, v_mlp_w_up, v_mlp_w_down, v_ple_w_gate, v_ple_w_proj):
    loc = dict(locals())
    shards = {n: loc[n] for n in WEIGHTS}
    moms = {n: loc['m_' + n] for n in WEIGHTS}
    vels = {n: loc['v_' + n] for n in WEIGHTS}

    depth, t, d = p.shape[0], x.shape[1], x.shape[2]

    def comm_kind(name):
        return 'stack' if SHARD_AXIS[name] == 2 else 'rows'

    part_keys = {'mlp': ('mlp_up', 'mlp_down', 'ple_gate', 'ple_proj'), 'mixer': ('w_in', 'w_out')}
    halves = [(i, part) for i in range(depth) for part in ('mixer', 'mlp')]

    def half_shards(i, part):
        items = [it for it in _layer_items(i) if it[0] in part_keys[part]]
        arrs = [shards[n][idx].astype(BF16) for _, n, idx in items]
        kinds = [comm_kind(n) for _, n, _ in items]
        small = _small_block(shards, i) if part == 'mixer' else None
        if small is not None:
            arrs.append(small)
            kinds.append('stack')
        return items, arrs, kinds

    def half_weights(i, items, kinds, outs):
        wl = {key: (_cols_to_full(o) if kd == 'stack' and n not in STACKED_OPERANDS else o)
              for (key, n, _), kd, o in zip(items, kinds, outs)}
        if len(outs) > len(items):
            wl['small'] = _cols_to_full(outs[-1])[:shards['sc_w_conv'].shape[1] if i % N_MIXERS == 0 else SUB]
        return wl

    first = [half_shards(0, part) for part in ('mixer', 'mlp')]
    outs0 = _gather_many(first[0][1] + first[1][1], first[0][2] + first[1][2], 'gather_weights_0')
    weights0 = {**half_weights(0, first[0][0], first[0][2], outs0[:len(first[0][1])]),
                **half_weights(0, first[1][0], first[1][2], outs0[len(first[0][1]):])}
    pending = {}

    def start_gather(pos, after):
        if pos >= len(halves):
            return None
        i, part = halves[pos]
        items, arrs, kinds = half_shards(i, part)
        flight, token = _send_start('gather', arrs, kinds, f'gather_weights_start_{part}_{i}', after=after)
        pending[pos] = (items, kinds, flight)
        return token

    first_token = start_gather(2, outs0[0])
    second_token = start_gather(3, first_token)

    def weights_for_layer(i, part, h):
        pos = halves.index((i, part))
        if pos == 0:
            return weights0, second_token
        if pos == 1:
            return {}, None
        items, kinds, flight = pending.pop(pos)
        outs = _send_wait('gather', flight, kinds, h, f'gather_weights_wait_{part}_{i}')
        return half_weights(i, items, kinds, outs), start_gather(pos + 2, outs[0])

    exchanges, gate_gathers, total_loss = {}, {}, []

    def gate_block(src, j):
        return jnp.concatenate([src[n][j].reshape(-1, LRU_BLOCK) for n in ('lru_w_a', 'lru_w_x')], axis=0)

    def emit_grads(i, part, gl, loss=None):
        after = None
        if loss is not None:
            total_loss.append(lax.psum(loss[0, 0], ('x', 'y', 'c')))
            after = jnp.full((SUB, 128), total_loss[0], F32)
        if 'gates' in gl:
            blk = gate_block({'lru_w_a': [gl['gates'][0]], 'lru_w_x': [gl['gates'][1]]}, 0)
            gate_gathers[i] = _send_start('gather', [blk], ['stack'], f'gather_gate_grads_start_{i}')[0]
        items = [it for it in _layer_items(i) if it[0] in part_keys[part]]
        kinds = [comm_kind(n) for _, n, _ in items]
        arrs = [_full_to_cols(gl[key]) if kd == 'stack' and gl[key].ndim == 2 else gl[key]
                for (key, _, _), kd in zip(items, kinds)]
        if part == 'mixer' and i % N_MIXERS == 0:
            arrs.append(_full_to_cols(_pad_to(gl['small'], SUB)))
        elif part == 'mixer' and i % N_MIXERS == 2:
            dv = gl['small']
            arrs.append(_full_to_cols(jnp.concatenate([dv[4:8], dv[3:4], dv[0:1], dv[1:2], dv[2:3]], axis=0)))
        kinds += ['stack'] * (len(arrs) - len(kinds))
        flight, token = _send_start('exchange', arrs, kinds, f'exchange_grads_start_{part}_{i}', after=after)
        exchanges[(i, part)] = (items, kinds, flight)
        return token

    rep = {n: shards[n] for n in ('norm_mix', 'norm_mlp', 'norm_ple', 'norm_final')}
    rep['lru_w_a'], rep['lru_w_x'] = shards['lru_w_a'].astype(BF16), shards['lru_w_x'].astype(BF16)
    loss, grad_x, rgrads = _local_step(x.reshape(t, d), p.reshape(depth, t, p.shape[3]), positions.reshape(t, 1),
                                       loss_target.reshape(t, d), rep, weights_for_layer, emit_grads)

    received, res = {}, {}

    def finish_exchange(key, after):
        items, kinds, flight = exchanges[key]
        outs = _send_wait('exchange', flight, kinds, after, f'exchange_grads_wait_{key[1]}_{key[0]}')
        for (_, n, idx), o in zip(items, outs):
            received[(n, idx)] = o
        if len(outs) > len(items):
            received[('small', key[0])] = outs[-1]

    def big_adamw(names):
        for n in names:
            res[n] = _adamw_sum(shards[n], [received[(n, l)] for l in range(shards[n].shape[0])], moms[n], vels[n],
                                f'adamw_{n}')

    last = (0, 'mixer')
    for key in exchanges:
        if key != last:
            finish_exchange(key, grad_x)
    big = [n for n in WEIGHTS if SHARD_AXIS[n] is not None and shards[n].ndim == 3 and n not in ('sc_w_conv', 'lru_conv_w')]
    late = [n for n in big if n in MIXER_WEIGHTS[0]]
    big_adamw([n for n in big if n not in late])
    finish_exchange(last, jnp.full((SUB, 128), sum(r[0].reshape(-1)[0] for r in res.values()), F32))
    big_adamw(late)


    def small_adamw(layers, name):
        w_, m_, v_ = (jnp.stack([_small_block(src, i) for i in layers]) for src in (shards, moms, vels))
        return _adamw_sum(w_, [received[('small', i)] for i in layers], m_, v_, name)

    sc = small_adamw([i for i in range(depth) if i % N_MIXERS == 0], 'adamw_sc_w_conv')
    res['sc_w_conv'] = tuple(o[:, :shards['sc_w_conv'].shape[1]] for o in sc)
    lru = small_adamw([i for i in range(depth) if i % N_MIXERS == 2], 'adamw_lru_small')
    row = 0
    for n in LRU_SMALL:
        k = shards[n].size // shards[n].shape[0] // shards[n].shape[-1]
        res[n] = tuple(o[:, row:row + k].reshape(shards[n].shape) for o in lru)
        row += k

    def all_updated():
        return jnp.full((SUB, 128), sum(r[0].reshape(-1)[0] for r in res.values()), F32)

    gate_layers = sorted(gate_gathers)
    gate_parts = [_send_wait('gather', gate_gathers[i], ['stack'], all_updated(), f'gather_gate_grads_wait_{i}')[0]
                  for i in gate_layers]
    gate_w, gate_m, gate_v = (jnp.stack([gate_block(src, j) for j in range(len(gate_layers))])
                              for src in (shards, moms, vels))
    gates = _adamw_sum(gate_w, gate_parts, gate_m, gate_v, 'adamw_lru_gates')
    half = gates[0].shape[1] // 2
    res['lru_w_a'] = tuple(o[:, :half].reshape(shards['lru_w_a'].shape) for o in gates)
    res['lru_w_x'] = tuple(o[:, half:].reshape(shards['lru_w_x'].shape) for o in gates)

    norm_names = ('norm_mix', 'norm_mlp', 'norm_ple', 'norm_final')

    def norm_block(src):
        cat = jnp.concatenate([src[n].reshape(-1, d) for n in norm_names], axis=0)
        return _pad_to(cat, -(-cat.shape[0] // HALO) * HALO)

    rfull = {n: (rgrads[n] if n == 'norm_final' else jnp.stack(rgrads[n], axis=0)) for n in norm_names}
    parts_norm, = _gather_many([norm_block(rfull)], ['stack'], 'gather_norm_grads', after=all_updated())
    norms = _adamw_sum(norm_block(shards)[None], [parts_norm], norm_block(moms)[None], norm_block(vels)[None],
                       'adamw_norms')
    row = 0
    for n in norm_names:
        k = shards[n].size // d
        res[n] = tuple(o[0, row:row + k].reshape(shards[n].shape) for o in norms)
        row += k

    return (total_loss[0], grad_x.reshape(x.shape), *[res[n][0] for n in WEIGHTS], *[res[n][1] for n in WEIGHTS],
            *[res[n][2] for n in WEIGHTS], *[res[n][3] for n in WEIGHTS])
```
